```python
import jax, jax.numpy as jnp
from jax import lax
import numpy as np

D_MODEL = 1024
BATCH = 8
SEQ = 8192
DEPTH = 2

HEAD_DIM = 64
ATTN_GROUPS = ((128, 1), (512, 4), (2048, 16))
N_GROUPS = len(ATTN_GROUPS)
HEADS_PER_GROUP = 6
N_ATTN_HEADS = N_GROUPS * HEADS_PER_GROUP
ATTN_WIDTH = N_ATTN_HEADS * HEAD_DIM
ATTN_OUT_WIDTH = HEADS_PER_GROUP * HEAD_DIM
NUM_BUCKETS = 32
MAX_DISTANCE = 2048
RET_HEADS = 4
RET_QK_DIM = 256
RET_V_DIM = 2 * RET_QK_DIM
RET_QK_WIDTH = RET_HEADS * RET_QK_DIM
RET_V_WIDTH = RET_HEADS * RET_V_DIM
RET_CHUNK = 128
ROPE_BASE = 10000.0
D_FF = -(-8 * D_MODEL // (3 * 256)) * 256
ALPHA = (2 * DEPTH) ** 0.25
BETA = (8 * DEPTH) ** -0.25
LN_EPS = 1e-5
GN_EPS = 1e-5
SPLIT_SIZES = (ATTN_WIDTH, ATTN_WIDTH, ATTN_WIDTH,
               RET_QK_WIDTH, RET_QK_WIDTH, RET_V_WIDTH, RET_V_WIDTH,
               D_MODEL, D_MODEL)
IN_COLS = sum(SPLIT_SIZES)
SPLIT_POINTS = tuple(int(v) for v in np.cumsum(SPLIT_SIZES)[:-1])

kernel_name = "hybrid_dilated_attn_retention_deepnorm"


def _t5_bucket(dist):
    max_exact = NUM_BUCKETS // 2
    large = max_exact + (np.log(np.maximum(dist, max_exact) / max_exact)
                         / np.log(MAX_DISTANCE / max_exact)
                         * (NUM_BUCKETS - max_exact)).astype(np.int32)
    large = np.minimum(large, NUM_BUCKETS - 1)
    return np.where(dist < max_exact, dist, large).astype(np.int32)


def _layer_norm(x, g, b):
    xf = x.astype(jnp.float32)
    mu = jnp.mean(xf, axis=-1, keepdims=True)
    var = jnp.mean(jnp.square(xf - mu), axis=-1, keepdims=True)
    return ((xf - mu) * lax.rsqrt(var + LN_EPS) * g + b).astype(x.dtype)


def _dilated_window_attention(q, k, v, bias_table, window, dilation):
    B, S, H, Dh = q.shape
    W = window // dilation
    L = S // dilation
    nb = -(-L // W)
    Lp = nb * W

    def to_sub(t):
        t = t.reshape(B, L, dilation, H, Dh).transpose(0, 2, 3, 1, 4)
        return jnp.pad(t, ((0, 0), (0, 0), (0, 0), (0, Lp - L), (0, 0)))

    def band(t):
        t = jnp.pad(to_sub(t), ((0, 0), (0, 0), (0, 0), (W, 0), (0, 0)))
        prev = t[:, :, :, :Lp].reshape(B, dilation, H, nb, W, Dh)
        cur = t[:, :, :, W:].reshape(B, dilation, H, nb, W, Dh)
        return jnp.concatenate([prev, cur], axis=-2)

    qs = to_sub(q).reshape(B, dilation, H, nb, W, Dh)
    kb, vb = band(k), band(v)

    qi = np.arange(W)[:, None]
    kj = np.arange(2 * W)[None, :]
    rel = qi + W - kj
    in_win = (rel >= 0) & (rel <= W)
    key_idx = np.arange(nb)[:, None, None] * W + kj[None] - W
    mask = in_win[None] & (key_idx >= 0)
    buckets = _t5_bucket(np.clip(rel, 0, W) * dilation)
    bias = jnp.moveaxis(jnp.take(bias_table, buckets, axis=0), -1, 0).astype(jnp.float32)

    s = jnp.einsum('bghnqe,bghnke->bghnqk', qs, kb).astype(jnp.float32) * (Dh ** -0.5)
    s = s + bias[None, None, :, None]
    s = jnp.where(mask[None, None, None], s, -jnp.inf)
    m = jnp.max(s, axis=-1, keepdims=True)
    p = jnp.exp(s - m)
    l = jnp.sum(p, axis=-1, keepdims=True)
    o = jnp.einsum('bghnqk,bghnke->bghnqe', (p / l).astype(v.dtype), vb)
    lse = (m + jnp.log(l))[..., 0]
    o = o.reshape(B, dilation, H, Lp, Dh)[:, :, :, :L].transpose(0, 3, 1, 2, 4).reshape(B, S, H, Dh)
    lse = lse.reshape(B, dilation, H, Lp)[..., :L].transpose(0, 3, 1, 2).reshape(B, S, H)
    return o, lse


def _retention(q, k, v):
    B, S, H, dk = q.shape
    dv = v.shape[-1]
    half = dk // 2
    pos = jnp.arange(S, dtype=jnp.float32)
    inv_freq = ROPE_BASE ** (-jnp.arange(half, dtype=jnp.float32) / half)
    ang = pos[:, None] * inv_freq[None]
    cos = jnp.cos(ang)[None, :, None]
    sin = jnp.sin(ang)[None, :, None]

    def rot(t):
        t1, t2 = t[..., :half], t[..., half:]
        return jnp.concatenate([t1 * cos - t2 * sin, t1 * sin + t2 * cos], axis=-1).astype(t.dtype)

    q = rot(q)
    k = rot(k) * (dk ** -0.5)
    log_g = jnp.log(1.0 - 2.0 ** (-5.0 - jnp.arange(H, dtype=jnp.float32)))
    C = RET_CHUNK
    nC = S // C
    n = jnp.arange(C, dtype=jnp.float32)
    diff = n[:, None] - n[None, :]
    decay_mask = jnp.where(diff >= 0, jnp.exp(log_g[:, None, None] * jnp.maximum(diff, 0.0)), 0.0)
    q_dec = jnp.exp(log_g[:, None] * (n + 1.0))
    k_dec = jnp.exp(log_g[:, None] * (C - 1.0 - n))
    chunk_dec = jnp.exp(log_g * C)

    def chunks(t):
        return t.reshape(B, nC, C, H, t.shape[-1]).transpose(1, 0, 3, 2, 4)

    def step(state, xs):
        qc, kc, vc = xs
        sc = jnp.einsum('bhnd,bhmd->bhnm', qc, kc) * decay_mask
        o = (jnp.einsum('bhnm,bhmv->bhnv', sc, vc)
             + jnp.einsum('bhnd,bhdv->bhnv', qc * q_dec[..., None], state))
        state = (state * chunk_dec[:, None, None]
                 + jnp.einsum('bhmd,bhmv->bhdv', kc * k_dec[..., None], vc))
        return state.astype(jnp.float32), o.astype(jnp.float32)

    state0 = jnp.zeros((B, H, dk, dv), jnp.float32)
    _, ys = lax.scan(step, state0, (chunks(q), chunks(k), chunks(v)))
    o = ys.transpose(1, 0, 3, 2, 4).reshape(B, S, H, dv)
    mu = jnp.mean(o, axis=-1, keepdims=True)
    var = jnp.mean(jnp.square(o - mu), axis=-1, keepdims=True)
    return (o - mu) * lax.rsqrt(var + GN_EPS)


def _hybrid_mixer(x, rel_bias, w_in, b_in, w_attn_proj, w_ret_proj, w_out):
    B, S, _ = x.shape
    z = jnp.einsum('bsd,dc->bsc', x, w_in) + b_in
    q_a, k_a, v_a, q_r, k_r, v_r, g_r, gate_a, gate_b = jnp.split(z, SPLIT_POINTS, axis=-1)
    q_a = q_a.reshape(B, S, N_GROUPS, HEADS_PER_GROUP, HEAD_DIM)
    k_a = k_a.reshape(B, S, N_GROUPS, HEADS_PER_GROUP, HEAD_DIM)
    v_a = v_a.reshape(B, S, N_GROUPS, HEADS_PER_GROUP, HEAD_DIM)
    outs, lses = [], []
    for gi, (window, dilation) in enumerate(ATTN_GROUPS):
        o, lse = _dilated_window_attention(
            q_a[:, :, gi], k_a[:, :, gi], v_a[:, :, gi],
            rel_bias[:, gi * HEADS_PER_GROUP:(gi + 1) * HEADS_PER_GROUP], window, dilation)
        outs.append(o.astype(jnp.float32))
        lses.append(lse)
    wts = jax.nn.softmax(jnp.stack(lses, axis=0), axis=0)
    y_a = jnp.sum(wts[..., None] * jnp.stack(outs, axis=0), axis=0)
    y_a = y_a.astype(x.dtype).reshape(B, S, ATTN_OUT_WIDTH)
    y_r = _retention(q_r.reshape(B, S, RET_HEADS, RET_QK_DIM),
                     k_r.reshape(B, S, RET_HEADS, RET_QK_DIM),
                     v_r.reshape(B, S, RET_HEADS, RET_V_DIM))
    y_b = (jax.nn.silu(g_r) * y_r.reshape(B, S, RET_V_WIDTH)).astype(x.dtype)
    merged = (jax.nn.sigmoid(gate_a) * jnp.einsum('bsc,cd->bsd', y_a, w_attn_proj)
              + jax.nn.sigmoid(gate_b) * jnp.einsum('bsc,cd->bsd', y_b, w_ret_proj))
    return jnp.einsum('bsd,de->bse', merged, w_out)


def _swiglu(x, w_gate, w_up, w_down):
    h = jax.nn.silu(jnp.einsum('bsd,df->bsf', x, w_gate)) * jnp.einsum('bsd,df->bsf', x, w_up)
    return jnp.einsum('bsf,fd->bsd', h, w_down)


def _fwd_setup_inputs(seed: int = 0) -> dict:
    key = jax.random.key(seed)
    ks = jax.random.split(key, 14)
    f32 = jnp.float32

    def nrm(k, shape, scale):
        return jax.random.normal(k, shape, f32) * scale

    return {
        "x": nrm(ks[0], (BATCH, SEQ, D_MODEL), 1.0),
        "rel_bias": nrm(ks[1], (NUM_BUCKETS, N_ATTN_HEADS), 0.2),
        "w_in": nrm(ks[2], (DEPTH, D_MODEL, IN_COLS), D_MODEL ** -0.5),
        "b_in": nrm(ks[3], (DEPTH, IN_COLS), 0.02),
        "w_attn_proj": nrm(ks[4], (DEPTH, ATTN_OUT_WIDTH, D_MODEL), BETA * ATTN_OUT_WIDTH ** -0.5),
        "w_ret_proj": nrm(ks[5], (DEPTH, RET_V_WIDTH, D_MODEL), BETA * RET_V_WIDTH ** -0.5),
        "w_out": nrm(ks[6], (DEPTH, D_MODEL, D_MODEL), BETA * D_MODEL ** -0.5),
        "ln1_g": 1.0 + nrm(ks[7], (DEPTH, D_MODEL), 0.02),
        "ln1_b": nrm(ks[8], (DEPTH, D_MODEL), 0.02),
        "w_ffn_gate": nrm(ks[9], (DEPTH, D_MODEL, D_FF), D_MODEL ** -0.5),
        "w_ffn_up": nrm(ks[10], (DEPTH, D_MODEL, D_FF), D_MODEL ** -0.5),
        "w_ffn_down": nrm(ks[11], (DEPTH, D_FF, D_MODEL), BETA * D_FF ** -0.5),
        "ln2_g": 1.0 + nrm(ks[12], (DEPTH, D_MODEL), 0.02),
        "ln2_b": nrm(ks[13], (DEPTH, D_MODEL), 0.02),
    }


def _fwd_reference(x, rel_bias, w_in, b_in, w_attn_proj, w_ret_proj, w_out, ln1_g, ln1_b,
              w_ffn_gate, w_ffn_up, w_ffn_down, ln2_g, ln2_b):
    for l in range(DEPTH):
        mix = _hybrid_mixer(x, rel_bias, w_in[l], b_in[l], w_attn_proj[l], w_ret_proj[l], w_out[l])
        x = _layer_norm(ALPHA * x + mix, ln1_g[l], ln1_b[l])
        ffn = _swiglu(x, w_ffn_gate[l], w_ffn_up[l], w_ffn_down[l])
        x = _layer_norm(ALPHA * x + ffn, ln2_g[l], ln2_b[l])
    return x


import jax as _jax
import jax.numpy as _jnp

TWIN_FORMAT = 'train_step'
FWD_PARAMS = ['x', 'rel_bias', 'w_in', 'b_in', 'w_attn_proj', 'w_ret_proj', 'w_out', 'ln1_g', 'ln1_b', 'w_ffn_gate', 'w_ffn_up', 'w_ffn_down', 'ln2_g', 'ln2_b']
TWIN_WEIGHTS = ['rel_bias', 'w_in', 'b_in', 'w_attn_proj', 'w_ret_proj', 'w_out', 'ln1_g', 'ln1_b', 'w_ffn_gate', 'w_ffn_up', 'w_ffn_down', 'ln2_g', 'ln2_b']
TWIN_DIFF_INPUT = 'x'
TWIN_INPUTS = ['x', 'rel_bias', 'w_in', 'b_in', 'w_attn_proj', 'w_ret_proj', 'w_out', 'ln1_g', 'ln1_b', 'w_ffn_gate', 'w_ffn_up', 'w_ffn_down', 'ln2_g', 'ln2_b', 'loss_target', 'm_rel_bias', 'm_w_in', 'm_b_in', 'm_w_attn_proj', 'm_w_ret_proj', 'm_w_out', 'm_ln1_g', 'm_ln1_b', 'm_w_ffn_gate', 'm_w_ffn_up', 'm_w_ffn_down', 'm_ln2_g', 'm_ln2_b', 'v_rel_bias', 'v_w_in', 'v_b_in', 'v_w_attn_proj', 'v_w_ret_proj', 'v_w_out', 'v_ln1_g', 'v_ln1_b', 'v_w_ffn_gate', 'v_w_ffn_up', 'v_w_ffn_down', 'v_ln2_g', 'v_ln2_b']
TWIN_OUTPUTS = ['loss', 'grad_x', 'grad_rel_bias', 'grad_w_in', 'grad_b_in', 'grad_w_attn_proj', 'grad_w_ret_proj', 'grad_w_out', 'grad_ln1_g', 'grad_ln1_b', 'grad_w_ffn_gate', 'grad_w_ffn_up', 'grad_w_ffn_down', 'grad_ln2_g', 'grad_ln2_b', 'delta_rel_bias', 'delta_w_in', 'delta_b_in', 'delta_w_attn_proj', 'delta_w_ret_proj', 'delta_w_out', 'delta_ln1_g', 'delta_ln1_b', 'delta_w_ffn_gate', 'delta_w_ffn_up', 'delta_w_ffn_down', 'delta_ln2_g', 'delta_ln2_b', 'new_m_rel_bias', 'new_m_w_in', 'new_m_b_in', 'new_m_w_attn_proj', 'new_m_w_ret_proj', 'new_m_w_out', 'new_m_ln1_g', 'new_m_ln1_b', 'new_m_w_ffn_gate', 'new_m_w_ffn_up', 'new_m_w_ffn_down', 'new_m_ln2_g', 'new_m_ln2_b', 'new_v_rel_bias', 'new_v_w_in', 'new_v_b_in', 'new_v_w_attn_proj', 'new_v_w_ret_proj', 'new_v_w_out', 'new_v_ln1_g', 'new_v_ln1_b', 'new_v_w_ffn_gate', 'new_v_w_ffn_up', 'new_v_w_ffn_down', 'new_v_ln2_g', 'new_v_ln2_b']
TWIN_LEAF_KINDS = {'loss': 'loss', 'grad_x': 'grad_x', 'grad_rel_bias': 'grad_w', 'grad_w_in': 'grad_w', 'grad_b_in': 'grad_w', 'grad_w_attn_proj': 'grad_w', 'grad_w_ret_proj': 'grad_w', 'grad_w_out': 'grad_w', 'grad_ln1_g': 'grad_w', 'grad_ln1_b': 'grad_w', 'grad_w_ffn_gate': 'grad_w', 'grad_w_ffn_up': 'grad_w', 'grad_w_ffn_down': 'grad_w', 'grad_ln2_g': 'grad_w', 'grad_ln2_b': 'grad_w', 'delta_rel_bias': 'delta_w', 'delta_w_in': 'delta_w', 'delta_b_in': 'delta_w', 'delta_w_attn_proj': 'delta_w', 'delta_w_ret_proj': 'delta_w', 'delta_w_out': 'delta_w', 'delta_ln1_g': 'delta_w', 'delta_ln1_b': 'delta_w', 'delta_w_ffn_gate': 'delta_w', 'delta_w_ffn_up': 'delta_w', 'delta_w_ffn_down': 'delta_w', 'delta_ln2_g': 'delta_w', 'delta_ln2_b': 'delta_w', 'new_m_rel_bias': 'new_m', 'new_m_w_in': 'new_m', 'new_m_b_in': 'new_m', 'new_m_w_attn_proj': 'new_m', 'new_m_w_ret_proj': 'new_m', 'new_m_w_out': 'new_m', 'new_m_ln1_g': 'new_m', 'new_m_ln1_b': 'new_m', 'new_m_w_ffn_gate': 'new_m', 'new_m_w_ffn_up': 'new_m', 'new_m_w_ffn_down': 'new_m', 'new_m_ln2_g': 'new_m', 'new_m_ln2_b': 'new_m', 'new_v_rel_bias': 'new_v', 'new_v_w_in': 'new_v', 'new_v_b_in': 'new_v', 'new_v_w_attn_proj': 'new_v', 'new_v_w_ret_proj': 'new_v', 'new_v_w_out': 'new_v', 'new_v_ln1_g': 'new_v', 'new_v_ln1_b': 'new_v', 'new_v_w_ffn_gate': 'new_v', 'new_v_w_ffn_up': 'new_v', 'new_v_w_ffn_down': 'new_v', 'new_v_ln2_g': 'new_v', 'new_v_ln2_b': 'new_v'}


def _forward(args):
    return _fwd_reference(*[args[k] for k in FWD_PARAMS])


def _output_shape():
    def fwd():
        inp = _fwd_setup_inputs(0)
        return _fwd_reference(*[inp[k] for k in FWD_PARAMS])
    out = _jax.eval_shape(fwd)
    return out.shape, out.dtype

N_MICROBATCH = 1
ADAM_LR = 0.001
ADAM_B1 = 0.9
ADAM_B2 = 0.999
ADAM_EPS = 1e-08
ADAM_WD = 0.01
ADAM_STEP = 10
PER_EXAMPLE_BATCH_AXIS = {'x': 0, 'loss_target': 0}
SHARED_INPUTS = []
_WEIGHT_DTYPES = {'rel_bias': _jnp.float32, 'w_in': _jnp.float32, 'b_in': _jnp.float32, 'w_attn_proj': _jnp.float32, 'w_ret_proj': _jnp.float32, 'w_out': _jnp.float32, 'ln1_g': _jnp.float32, 'ln1_b': _jnp.float32, 'w_ffn_gate': _jnp.float32, 'w_ffn_up': _jnp.float32, 'w_ffn_down': _jnp.float32, 'ln2_g': _jnp.float32, 'ln2_b': _jnp.float32}
MOMENT_SCALE = {'rel_bias': 7.395247e-03, 'w_in': 9.392948e-03, 'b_in': 1.539910e-02, 'w_attn_proj': 9.694584e-03, 'w_ret_proj': 2.959925e-02, 'w_out': 3.114468e-02, 'ln1_g': 2.008668e+00, 'ln1_b': 9.979389e-01, 'w_ffn_gate': 3.262237e-02, 'w_ffn_up': 3.165694e-02, 'w_ffn_down': 1.049248e-01, 'ln2_g': 4.536471e+01, 'ln2_b': 1.595471e+00}


def _to_microbatches(a, axis):
    t = _jnp.moveaxis(a, axis, 0)
    t = t.reshape((N_MICROBATCH, t.shape[0] // N_MICROBATCH) + t.shape[1:])
    return _jnp.moveaxis(t, 1, axis + 1)


def setup_inputs(seed: int = 0) -> dict:
    inp = _fwd_setup_inputs(seed)
    key = _jax.random.fold_in(_jax.random.key(seed), 7919)
    shape, _ = _output_shape()
    out = dict(inp)
    out["loss_target"] = _jax.random.normal(_jax.random.fold_in(key, 0), shape, _jnp.float32)
    for i, name in enumerate(TWIN_WEIGHTS):
        w = inp[name].astype(_jnp.float32)
        if MOMENT_SCALE is None:
            s = _jnp.sqrt(_jnp.mean(_jnp.square(w)) + 1e-30)
        else:
            s = MOMENT_SCALE[name]
        km, kv = _jax.random.split(_jax.random.fold_in(key, i + 1))
        out[name] = w
        out["m_" + name] = s * _jax.random.normal(km, w.shape, _jnp.float32)
        out["v_" + name] = (s * s) * _jax.random.uniform(kv, w.shape, _jnp.float32, 0.5, 1.5)
    if N_MICROBATCH > 1:
        for name, axis in PER_EXAMPLE_BATCH_AXIS.items():
            out[name] = _to_microbatches(out[name], axis)
    return {'x': out['x'], 'rel_bias': out['rel_bias'], 'w_in': out['w_in'], 'b_in': out['b_in'], 'w_attn_proj': out['w_attn_proj'], 'w_ret_proj': out['w_ret_proj'], 'w_out': out['w_out'], 'ln1_g': out['ln1_g'], 'ln1_b': out['ln1_b'], 'w_ffn_gate': out['w_ffn_gate'], 'w_ffn_up': out['w_ffn_up'], 'w_ffn_down': out['w_ffn_down'], 'ln2_g': out['ln2_g'], 'ln2_b': out['ln2_b'], 'loss_target': out['loss_target'], 'm_rel_bias': out['m_rel_bias'], 'm_w_in': out['m_w_in'], 'm_b_in': out['m_b_in'], 'm_w_attn_proj': out['m_w_attn_proj'], 'm_w_ret_proj': out['m_w_ret_proj'], 'm_w_out': out['m_w_out'], 'm_ln1_g': out['m_ln1_g'], 'm_ln1_b': out['m_ln1_b'], 'm_w_ffn_gate': out['m_w_ffn_gate'], 'm_w_ffn_up': out['m_w_ffn_up'], 'm_w_ffn_down': out['m_w_ffn_down'], 'm_ln2_g': out['m_ln2_g'], 'm_ln2_b': out['m_ln2_b'], 'v_rel_bias': out['v_rel_bias'], 'v_w_in': out['v_w_in'], 'v_b_in': out['v_b_in'], 'v_w_attn_proj': out['v_w_attn_proj'], 'v_w_ret_proj': out['v_w_ret_proj'], 'v_w_out': out['v_w_out'], 'v_ln1_g': out['v_ln1_g'], 'v_ln1_b': out['v_ln1_b'], 'v_w_ffn_gate': out['v_w_ffn_gate'], 'v_w_ffn_up': out['v_w_ffn_up'], 'v_w_ffn_down': out['v_w_ffn_down'], 'v_ln2_g': out['v_ln2_g'], 'v_ln2_b': out['v_ln2_b']}


def _loss(weights, diff, rest, loss_target):
    with _jax.named_scope("forward"):
        args = {**rest, TWIN_DIFF_INPUT: diff, **{k: w.astype(_WEIGHT_DTYPES[k]) for k, w in weights.items()}}
        y = _forward(args)
    with _jax.named_scope("loss_head"):
        err = _jnp.square(y.astype(_jnp.float32) - loss_target)
        return 0.5 * _jnp.sum(_jnp.mean(err, axis=-1)) if err.ndim else 0.5 * err


def _adamw(w, g, m, v):
    m = ADAM_B1 * m + (1.0 - ADAM_B1) * g
    v = ADAM_B2 * v + (1.0 - ADAM_B2) * _jnp.square(g)
    m_hat = m / (1.0 - ADAM_B1 ** ADAM_STEP)
    v_hat = v / (1.0 - ADAM_B2 ** ADAM_STEP)
    delta = -ADAM_LR * (m_hat / (_jnp.sqrt(v_hat) + ADAM_EPS) + ADAM_WD * w)
    return delta, m, v


def reference(x, rel_bias, w_in, b_in, w_attn_proj, w_ret_proj, w_out, ln1_g, ln1_b, w_ffn_gate, w_ffn_up, w_ffn_down, ln2_g, ln2_b, loss_target, m_rel_bias, m_w_in, m_b_in, m_w_attn_proj, m_w_ret_proj, m_w_out, m_ln1_g, m_ln1_b, m_w_ffn_gate, m_w_ffn_up, m_w_ffn_down, m_ln2_g, m_ln2_b, v_rel_bias, v_w_in, v_b_in, v_w_attn_proj, v_w_ret_proj, v_w_out, v_ln1_g, v_ln1_b, v_w_ffn_gate, v_w_ffn_up, v_w_ffn_down, v_ln2_g, v_ln2_b):
    given = dict(x=x, rel_bias=rel_bias, w_in=w_in, b_in=b_in, w_attn_proj=w_attn_proj, w_ret_proj=w_ret_proj, w_out=w_out, ln1_g=ln1_g, ln1_b=ln1_b, w_ffn_gate=w_ffn_gate, w_ffn_up=w_ffn_up, w_ffn_down=w_ffn_down, ln2_g=ln2_g, ln2_b=ln2_b, loss_target=loss_target, m_rel_bias=m_rel_bias, m_w_in=m_w_in, m_b_in=m_b_in, m_w_attn_proj=m_w_attn_proj, m_w_ret_proj=m_w_ret_proj, m_w_out=m_w_out, m_ln1_g=m_ln1_g, m_ln1_b=m_ln1_b, m_w_ffn_gate=m_w_ffn_gate, m_w_ffn_up=m_w_ffn_up, m_w_ffn_down=m_w_ffn_down, m_ln2_g=m_ln2_g, m_ln2_b=m_ln2_b, v_rel_bias=v_rel_bias, v_w_in=v_w_in, v_b_in=v_b_in, v_w_attn_proj=v_w_attn_proj, v_w_ret_proj=v_w_ret_proj, v_w_out=v_w_out, v_ln1_g=v_ln1_g, v_ln1_b=v_ln1_b, v_w_ffn_gate=v_w_ffn_gate, v_w_ffn_up=v_w_ffn_up, v_w_ffn_down=v_w_ffn_down, v_ln2_g=v_ln2_g, v_ln2_b=v_ln2_b)
    weights = {n: given[n] for n in TWIN_WEIGHTS}
    shared = {n: given[n] for n in SHARED_INPUTS}
    per_example = {n: given[n] for n in ['x']}
    grad_fn = _jax.value_and_grad(_loss, argnums=(0, 1))

    def one_microbatch(ex, loss_target):
        ex = dict(ex)
        diff = ex.pop(TWIN_DIFF_INPUT)
        return grad_fn(weights, diff, {**shared, **ex}, loss_target)

    if N_MICROBATCH == 1:
        loss, (grad_w, grad_x) = one_microbatch(per_example, given["loss_target"])
    else:
        def body(carry, xs):
            loss_sum, grad_sum = carry
            l_k, (gw_k, gx_k) = one_microbatch(xs[0], xs[1])
            with _jax.named_scope("update"):
                return (loss_sum + l_k, _jax.tree.map(_jnp.add, grad_sum, gw_k)), gx_k

        init = (_jnp.zeros((), _jnp.float32), _jax.tree.map(_jnp.zeros_like, weights))
        (loss, grad_w), grad_x = _jax.lax.scan(body, init, (per_example, given["loss_target"]))
    with _jax.named_scope("update"):
        delta_w, new_m, new_v = {}, {}, {}
        for n in TWIN_WEIGHTS:
            delta_w[n], new_m[n], new_v[n] = _adamw(weights[n], grad_w[n], given["m_" + n], given["v_" + n])
    return (loss, grad_x, *[grad_w[n] for n in TWIN_WEIGHTS], *[delta_w[n] for n in TWIN_WEIGHTS],
            *[new_m[n] for n in TWIN_WEIGHTS], *[new_v[n] for n in TWIN_WEIGHTS])
```

```python
import functools

import numpy as np
import jax
import jax.numpy as jnp
from jax import lax
from jax.experimental import pallas as pl
from jax.experimental.pallas import tpu as pltpu

F32 = jnp.float32
MXU_DTYPE = jnp.bfloat16

DEPTH = 2
D_MODEL = 1024
HEAD_DIM = 64
ATTN_GROUPS = ((128, 1), (512, 4), (2048, 16))
HEADS_PER_GROUP = 6
N_ATTN_HEADS = 18
ATTN_WIDTH = 1152
ATTN_OUT = 384
NUM_BUCKETS = 32
MAX_DISTANCE = 2048
RET_HEADS = 4
RET_QK = 256
RET_V = 512
RET_CHUNK = 128
ROPE_BASE = 10000.0
D_FF = 2816
IN_COLS = 11648
ALPHA = (2 * DEPTH) ** 0.25
LN_EPS = 1e-5
GN_EPS = 1e-5
ADAM_LR, ADAM_B1, ADAM_B2, ADAM_EPS, ADAM_WD, ADAM_STEP = 0.001, 0.9, 0.999, 1e-08, 0.01, 10

BLK = 128
NEG = -1e30
N_CHIPS = 4
VMEM_LIMIT = 48 * 1024 * 1024

SECTIONS = (
    ("a", 0, 3456, True),
    ("bq", 3456, 1024, False),
    ("bk", 4480, 1024, False),
    ("c", 5504, 2048, True),
    ("d", 7552, 2048, False),
    ("e", 9600, 2048, False),
)
PACK_ROWS = (("w_in", 2912), ("w_attn_proj", 96), ("w_ret_proj", 512), ("w_out", 256),
             ("w_ffn_gate", 704), ("w_ffn_up", 704), ("w_ffn_down", 704))
ROWS_PER_LAYER = sum(r for _, r in PACK_ROWS)

NN = ((1,), (0,))
NT = ((1,), (1,))
TN = ((0,), (0,))


def _dot(a, b, dims):
    return lax.dot_general(a.astype(MXU_DTYPE), b.astype(MXU_DTYPE), (dims, ((), ())),
                           preferred_element_type=F32)


def _pick(n, prefs):
    for p in prefs:
        if n % p == 0:
            return p
    raise ValueError(f"no tile for {n} among {prefs}")


TOKEN_TILES = (512, 256, 128)
FEATURE_TILES = (1152, 1024, 1408, 384, 256, 128)


def _params(n_axes, limit=VMEM_LIMIT):
    return pltpu.CompilerParams(dimension_semantics=("arbitrary",) * n_axes, vmem_limit_bytes=limit)


def _sigmoid(x):
    return 1.0 / (1.0 + jnp.exp(-x))


def _norm_rows(u, eps):
    mu = jnp.mean(u, axis=-1, keepdims=True)
    xc = u - mu
    var = jnp.mean(xc * xc, axis=-1, keepdims=True)
    rstd = lax.rsqrt(var + eps)
    return xc * rstd, rstd


def _norm_rows_bwd(dxh, xh, rstd):
    c1 = jnp.mean(dxh, axis=-1, keepdims=True)
    c2 = jnp.mean(dxh * xh, axis=-1, keepdims=True)
    return rstd * (dxh - c1 - xh * c2)


def _matmul(a, b, mode, *, name, out_dtype=F32, bias=None, addend=None, colsum=False):
    if mode == "tn":
        kd, m = a.shape
        n = b.shape[1]
        tm, tn_, tk = _pick(m, FEATURE_TILES), _pick(n, FEATURE_TILES), _pick(kd, TOKEN_TILES)
        a_spec = pl.BlockSpec((tk, tm), lambda i, j, k: (k, i))
        b_spec = pl.BlockSpec((tk, tn_), lambda i, j, k: (k, j))
    else:
        m, kd = a.shape
        n = b.shape[0] if mode == "nt" else b.shape[1]
        tm, tn_, tk = _pick(m, TOKEN_TILES), _pick(n, FEATURE_TILES), _pick(kd, FEATURE_TILES)
        a_spec = pl.BlockSpec((tm, tk), lambda i, j, k: (i, k))
        if mode == "nt":
            b_spec = pl.BlockSpec((tn_, tk), lambda i, j, k: (j, k))
        else:
            b_spec = pl.BlockSpec((tk, tn_), lambda i, j, k: (k, j))
    dims = {"nn": NN, "nt": NT, "tn": TN}[mode]
    nk = kd // tk
    has_bias, has_add = bias is not None, addend is not None
    assert not colsum or mode == "tn"

    def body(*refs):
        it = iter(refs)
        a_ref, b_ref = next(it), next(it)
        bias_ref = next(it) if has_bias else None
        add_ref = next(it) if has_add else None
        o_ref = next(it)
        cs_ref = next(it) if colsum else None
        acc_ref = next(it)
        j, k = pl.program_id(1), pl.program_id(2)

        @pl.when(k == 0)
        def _():
            if has_add:
                acc_ref[...] = add_ref[...].astype(F32)
            else:
                acc_ref[...] = jnp.zeros_like(acc_ref)

        av = a_ref[...]
        acc_ref[...] += _dot(av, b_ref[...], dims)
        if colsum:
            @pl.when(jnp.logical_and(j == 0, k == 0))
            def _():
                cs_ref[...] = jnp.zeros_like(cs_ref)

            @pl.when(j == 0)
            def _():
                cs_ref[...] += jnp.sum(av.astype(F32), axis=0, keepdims=True)

        @pl.when(k == nk - 1)
        def _():
            r = acc_ref[...]
            if has_bias:
                r = r + bias_ref[...]
            o_ref[...] = r.astype(out_dtype)

    in_specs, args = [a_spec, b_spec], [a, b]
    if has_bias:
        in_specs.append(pl.BlockSpec((1, tn_), lambda i, j, k: (0, j)))
        args.append(bias)
    if has_add:
        in_specs.append(pl.BlockSpec((tm, tn_), lambda i, j, k: (i, j)))
        args.append(addend)
    out_shape = [jax.ShapeDtypeStruct((m, n), out_dtype)]
    out_specs = [pl.BlockSpec((tm, tn_), lambda i, j, k: (i, j))]
    if colsum:
        out_shape.append(jax.ShapeDtypeStruct((1, m), F32))
        out_specs.append(pl.BlockSpec((1, tm), lambda i, j, k: (0, i)))
    res = pl.pallas_call(
        body, out_shape=out_shape, grid=(m // tm, n // tn_, nk), in_specs=in_specs, out_specs=out_specs,
        scratch_shapes=[pltpu.VMEM((tm, tn_), F32)], compiler_params=_params(3), name=name,
    )(*args)
    return res if colsum else res[0]


def _t5_bucket(dist):
    max_exact = NUM_BUCKETS // 2
    large = max_exact + (np.log(np.maximum(dist, max_exact) / max_exact)
                         / np.log(MAX_DISTANCE / max_exact) * (NUM_BUCKETS - max_exact)).astype(np.int32)
    large = np.minimum(large, NUM_BUCKETS - 1)
    return np.where(dist < max_exact, dist, large).astype(np.int32)


def _bucket_maps():
    qi = np.arange(BLK)[:, None]
    kj = np.arange(2 * BLK)[None, :]
    rel = np.clip(qi + BLK - kj, 0, BLK)
    return jnp.asarray(np.stack([_t5_bucket(rel * d) for _, d in ATTN_GROUPS]))


def _bias_tiles(rel_bias, bmaps):
    def body(tab_ref, bm_ref, o_ref):
        h = pl.program_id(0)
        bm = bm_ref[0]
        acc = jnp.zeros((BLK, 2 * BLK), F32)
        for b in range(NUM_BUCKETS):
            acc = jnp.where(bm == b, tab_ref[b, h], acc)
        o_ref[0] = acc

    return pl.pallas_call(
        body, out_shape=jax.ShapeDtypeStruct((N_ATTN_HEADS, BLK, 2 * BLK), F32), grid=(N_ATTN_HEADS,),
        in_specs=[pl.BlockSpec(memory_space=pltpu.SMEM),
                  pl.BlockSpec((1, BLK, 2 * BLK), lambda h: (h // HEADS_PER_GROUP, 0, 0))],
        out_specs=pl.BlockSpec((1, BLK, 2 * BLK), lambda h: (h, 0, 0)),
        compiler_params=_params(1), name="bias_tiles",
    )(rel_bias, bmaps)


def _bias_tiles_bwd(dbias_layers, bmaps):
    nl = len(dbias_layers)

    def body(*refs):
        bm = refs[nl][0]
        o_ref = refs[nl + 1]
        x = refs[0][0]
        for r in refs[1:nl]:
            x = x + r[0]
        lane = lax.broadcasted_iota(jnp.int32, (1, BLK), 1)
        row = jnp.zeros((1, BLK), F32)
        for b in range(NUM_BUCKETS):
            s = jnp.sum(jnp.where(bm == b, x, 0.0), axis=1, keepdims=True)
            s = jnp.sum(s, axis=0, keepdims=True)
            row = jnp.where(lane == b, s, row)
        o_ref[0] = row

    tile = pl.BlockSpec((1, BLK, 2 * BLK), lambda h: (h, 0, 0))
    out = pl.pallas_call(
        body, out_shape=jax.ShapeDtypeStruct((N_ATTN_HEADS, 1, BLK), F32), grid=(N_ATTN_HEADS,),
        in_specs=[tile] * nl + [pl.BlockSpec((1, BLK, 2 * BLK), lambda h: (h // HEADS_PER_GROUP, 0, 0))],
        out_specs=pl.BlockSpec((1, 1, BLK), lambda h: (h, 0, 0)),
        compiler_params=_params(1), name="bias_tiles_bwd",
    )(*dbias_layers, bmaps)
    return out[:, 0, :NUM_BUCKETS].T


def _head_masks():
    lane = lax.broadcasted_iota(jnp.int32, (BLK, BLK), 1)
    first = lane < HEAD_DIM
    return first, jnp.logical_not(first)


def _band_masks():
    qi = lax.broadcasted_iota(jnp.int32, (BLK, BLK), 0)
    kj = lax.broadcasted_iota(jnp.int32, (BLK, BLK), 1)
    return kj >= qi, kj <= qi


def _attn_fwd(qa, ka, va, cb, offs, bias, g, dil):
    rows = qa.shape[0]
    nb = rows // BLK
    oq, ok, ov = offs

    def cur(off):
        return pl.BlockSpec((BLK, BLK), lambda hp, r, n: (n, r * cb + off + hp))

    def prev(off):
        return pl.BlockSpec((BLK, BLK), lambda hp, r, n: (jnp.maximum(n - 1, 0), r * cb + off + hp))

    def body(q_ref, kp_ref, kc_ref, vp_ref, vc_ref, b_ref, o_ref, l_ref):
        n = pl.program_id(2)
        pen = jnp.where(n > 0, 0.0, NEG)
        mask_p, mask_c = _band_masks()
        heads = _head_masks()
        q, kp, kc, vp, vc = q_ref[...], kp_ref[...], kc_ref[...], vp_ref[...], vc_ref[...]
        o_acc = jnp.zeros((BLK, BLK), F32)
        l_acc = jnp.zeros((BLK, BLK), F32)
        for h in range(2):
            qm = jnp.where(heads[h], q.astype(F32), 0.0)
            sp = _dot(qm, kp, NT) * (HEAD_DIM ** -0.5) + b_ref[h, :, 0:BLK]
            sc = _dot(qm, kc, NT) * (HEAD_DIM ** -0.5) + b_ref[h, :, BLK:2 * BLK]
            sp = jnp.where(mask_p, sp + pen, NEG)
            sc = jnp.where(mask_c, sc, NEG)
            m = jnp.maximum(jnp.max(sp, axis=1, keepdims=True), jnp.max(sc, axis=1, keepdims=True))
            pp, pc = jnp.exp(sp - m), jnp.exp(sc - m)
            l = jnp.sum(pp, axis=1, keepdims=True) + jnp.sum(pc, axis=1, keepdims=True)
            inv = 1.0 / l
            o = _dot(pp * inv, vp, NN) + _dot(pc * inv, vc, NN)
            o_acc = jnp.where(heads[h], o, o_acc)
            l_acc = jnp.where(heads[h], m + jnp.log(l), l_acc)
        o_ref[...] = o_acc
        l_ref[...] = l_acc

    out_spec = pl.BlockSpec((BLK, BLK), lambda hp, r, n: (n, r * 3 + hp))
    shape = jax.ShapeDtypeStruct((rows, dil * ATTN_OUT), F32)
    return pl.pallas_call(
        body, out_shape=[shape, shape], grid=(3, dil, nb),
        in_specs=[cur(oq), prev(ok), cur(ok), prev(ov), cur(ov),
                  pl.BlockSpec((2, BLK, 2 * BLK), lambda hp, r, n: (g * 3 + hp, 0, 0))],
        out_specs=[out_spec, out_spec], compiler_params=_params(3), name=f"attn_fwd_g{g}",
    )(qa, ka, ka, va, va, bias)


def _attn_bwd(qa, ka, va, cb, offs, bias, dy, ya, lt, g, dil):
    rows = qa.shape[0]
    nb = rows // BLK
    oq, ok, ov = offs

    def cur(off, c):
        return pl.BlockSpec((BLK, BLK), lambda hp, r, n: (jnp.minimum(n, nb - 1), r * c + off + hp))

    def prev(off, c):
        return pl.BlockSpec((BLK, BLK), lambda hp, r, n: (jnp.clip(n - 1, 0, nb - 1), r * c + off + hp))

    def body(q_ref, kp_ref, kc_ref, vp_ref, vc_ref, b_ref, dy_ref, ya_ref, lt_ref,
             dq_ref, dk_ref, dv_ref, db_ref, ck_ref, cv_ref):
        r, n = pl.program_id(1), pl.program_id(2)

        @pl.when(jnp.logical_and(r == 0, n == 0))
        def _():
            db_ref[...] = jnp.zeros_like(db_ref)

        @pl.when(n == 0)
        def _():
            ck_ref[...] = jnp.zeros_like(ck_ref)
            cv_ref[...] = jnp.zeros_like(cv_ref)

        @pl.when(n < nb)
        def _():
            pen = jnp.where(n > 0, 0.0, NEG)
            mask_p, mask_c = _band_masks()
            heads = _head_masks()
            q, kp, kc, vp, vc = q_ref[...], kp_ref[...], kc_ref[...], vp_ref[...], vc_ref[...]
            dy_, ya_, lt_ = dy_ref[...], ya_ref[...], lt_ref[...]
            zero = jnp.zeros((BLK, BLK), F32)
            dq, dkp, dkc, dvp, dvc = zero, zero, zero, zero, zero
            scale = HEAD_DIM ** -0.5
            for h in range(2):
                hm = heads[h]
                qm = jnp.where(hm, q.astype(F32), 0.0)
                lse = jnp.max(jnp.where(hm, lt_, NEG), axis=1, keepdims=True)
                delta = jnp.sum(jnp.where(hm, dy_ * ya_, 0.0), axis=1, keepdims=True)
                sp = _dot(qm, kp, NT) * scale + b_ref[h, :, 0:BLK]
                sc = _dot(qm, kc, NT) * scale + b_ref[h, :, BLK:2 * BLK]
                pp = jnp.exp(jnp.where(mask_p, sp + pen, NEG) - lse)
                pc = jnp.exp(jnp.where(mask_c, sc, NEG) - lse)
                dym = jnp.where(hm, dy_, 0.0)
                dsp = pp * (_dot(dym, vp, NT) - delta)
                dsc = pc * (_dot(dym, vc, NT) - delta)
                db_ref[h, :, 0:BLK] += dsp
                db_ref[h, :, BLK:2 * BLK] += dsc
                dq = jnp.where(hm, (_dot(dsp, kp, NN) + _dot(dsc, kc, NN)) * scale, dq)
                dkp = jnp.where(hm, _dot(dsp, q, TN) * scale, dkp)
                dkc = jnp.where(hm, _dot(dsc, q, TN) * scale, dkc)
                dvp = jnp.where(hm, _dot(pp, dy_, TN), dvp)
                dvc = jnp.where(hm, _dot(pc, dy_, TN), dvc)
            dq_ref[...] = dq.astype(dq_ref.dtype)
            dk_ref[...] = (ck_ref[...] + dkp).astype(dk_ref.dtype)
            dv_ref[...] = (cv_ref[...] + dvp).astype(dv_ref.dtype)
            ck_ref[...] = dkc
            cv_ref[...] = dvc

        @pl.when(n == nb)
        def _():
            dk_ref[...] = ck_ref[...].astype(dk_ref.dtype)
            dv_ref[...] = cv_ref[...].astype(dv_ref.dtype)

    late = pl.BlockSpec((BLK, BLK), lambda hp, r, n: (jnp.maximum(n - 1, 0), r * 3 + hp))
    shape = jax.ShapeDtypeStruct((rows, dil * ATTN_OUT), MXU_DTYPE)
    return pl.pallas_call(
        body,
        out_shape=[shape, shape, shape, jax.ShapeDtypeStruct((HEADS_PER_GROUP, BLK, 2 * BLK), F32)],
        grid=(3, dil, nb + 1),
        in_specs=[cur(oq, cb), prev(ok, cb), cur(ok, cb), prev(ov, cb), cur(ov, cb),
                  pl.BlockSpec((2, BLK, 2 * BLK), lambda hp, r, n: (g * 3 + hp, 0, 0)),
                  cur(0, 3), cur(0, 3), cur(0, 3)],
        out_specs=[cur(0, 3), late, late, pl.BlockSpec((2, BLK, 2 * BLK), lambda hp, r, n: (hp, 0, 0))],
        scratch_shapes=[pltpu.VMEM((BLK, BLK), F32), pltpu.VMEM((BLK, BLK), F32)],
        compiler_params=_params(3), name=f"attn_bwd_g{g}",
    )(qa, ka, ka, va, va, bias, dy, ya, lt)


def _attn_combine(o_list, l_list):
    s = o_list[0].shape[0]
    tr = _pick(s, (1024, 512, 256, 128))

    def body(o0, o1, o2, l0, l1, l2, y_ref, yb_ref, lt_ref):
        a0, a1, a2 = l0[...], l1[...], l2[...]
        mx = jnp.maximum(jnp.maximum(a0, a1), a2)
        e0, e1, e2 = jnp.exp(a0 - mx), jnp.exp(a1 - mx), jnp.exp(a2 - mx)
        den = e0 + e1 + e2
        inv = 1.0 / den
        y = (e0 * inv) * o0[...] + (e1 * inv) * o1[...] + (e2 * inv) * o2[...]
        y_ref[...] = y
        yb_ref[...] = y.astype(yb_ref.dtype)
        lt_ref[...] = mx + jnp.log(den)

    spec = pl.BlockSpec((tr, ATTN_OUT), lambda i: (i, 0))
    f = jax.ShapeDtypeStruct((s, ATTN_OUT), F32)
    return pl.pallas_call(
        body, out_shape=[f, jax.ShapeDtypeStruct((s, ATTN_OUT), MXU_DTYPE), f], grid=(s // tr,),
        in_specs=[spec] * 6, out_specs=[spec] * 3, compiler_params=_params(1), name="attn_combine",
    )(*o_list, *l_list)


def _ret_consts(s):
    half = RET_QK // 2
    pos = jnp.arange(s, dtype=F32)
    inv_freq = ROPE_BASE ** (-jnp.arange(half, dtype=F32) / half)
    ang = pos[:, None] * inv_freq[None]
    log_g = jnp.log(1.0 - 2.0 ** (-5.0 - jnp.arange(RET_HEADS, dtype=F32)))
    n = jnp.arange(RET_CHUNK, dtype=F32)
    diff = n[:, None] - n[None, :]
    dmask = jnp.where(diff >= 0, jnp.exp(log_g[:, None, None] * jnp.maximum(diff, 0.0)), 0.0)
    qdec = jnp.exp(log_g[:, None] * (n + 1.0))
    kdec = jnp.exp(log_g[:, None] * (RET_CHUNK - 1.0 - n))
    cdec = jnp.exp(log_g * RET_CHUNK)
    wide = (RET_HEADS, RET_CHUNK, RET_QK)
    return dict(cos=jnp.cos(ang), sin=jnp.sin(ang), dmask=dmask,
                qdec=jnp.broadcast_to(qdec[:, :, None], wide), kdec=jnp.broadcast_to(kdec[:, :, None], wide),
                cdec=cdec)


def _rot(t, cs, sn):
    half = RET_QK // 2
    t1, t2 = t[:, :half], t[:, half:]
    return jnp.concatenate([t1 * cs - t2 * sn, t1 * sn + t2 * cs], axis=1)


def _rot_bwd(d, cs, sn):
    half = RET_QK // 2
    d1, d2 = d[:, :half], d[:, half:]
    return jnp.concatenate([d1 * cs + d2 * sn, d2 * cs - d1 * sn], axis=1)


def _ret_fwd(zq, zk, zv, zg, rc):
    s = zq.shape[0]
    nc = s // RET_CHUNK
    c = RET_CHUNK

    def body(cd_ref, q_ref, k_ref, v_ref, g_ref, cos_ref, sin_ref, dm_ref, qd_ref, kd_ref,
             yb_ref, o_ref, st_ref, state):
        h, n = pl.program_id(0), pl.program_id(1)

        @pl.when(n == 0)
        def _():
            state[...] = jnp.zeros_like(state)

        cs, sn = cos_ref[...], sin_ref[...]
        qr = _rot(q_ref[...], cs, sn)
        kr = _rot(k_ref[...], cs, sn) * (RET_QK ** -0.5)
        v = v_ref[...]
        st = state[...]
        st_ref[0, 0] = st.astype(st_ref.dtype)
        sc = _dot(qr, kr, NT) * dm_ref[0]
        o = _dot(sc, v, NN) + _dot(qr * qd_ref[0], st, NN)
        state[...] = st * cd_ref[h] + _dot(kr * kd_ref[0], v, TN)
        o_ref[...] = o
        xh, _ = _norm_rows(o, GN_EPS)
        gv = g_ref[...]
        yb_ref[...] = (gv * _sigmoid(gv) * xh).astype(yb_ref.dtype)

    qk = lambda off: pl.BlockSpec((c, RET_QK), lambda h, n: (n, off + h))
    wide = pl.BlockSpec((c, RET_V), lambda h, n: (n, h))
    rope = pl.BlockSpec((c, RET_QK // 2), lambda h, n: (n, 0))
    per_head = lambda w: pl.BlockSpec((1, c, w), lambda h, n: (h, 0, 0))
    return pl.pallas_call(
        body,
        out_shape=[jax.ShapeDtypeStruct((s, RET_HEADS * RET_V), MXU_DTYPE),
                   jax.ShapeDtypeStruct((s, RET_HEADS * RET_V), F32),
                   jax.ShapeDtypeStruct((RET_HEADS, nc, RET_QK, RET_V), MXU_DTYPE)],
        grid=(RET_HEADS, nc),
        in_specs=[pl.BlockSpec(memory_space=pltpu.SMEM), qk(0), qk(0), wide, wide, rope, rope,
                  per_head(c), per_head(RET_QK), per_head(RET_QK)],
        out_specs=[wide, wide, pl.BlockSpec((1, 1, RET_QK, RET_V), lambda h, n: (h, n, 0, 0))],
        scratch_shapes=[pltpu.VMEM((RET_QK, RET_V), F32)],
        compiler_params=_params(2), name="ret_fwd",
    )(rc["cdec"], zq, zk, zv, zg, rc["cos"], rc["sin"], rc["dmask"], rc["qdec"], rc["kdec"])


def _ret_bwd(zq, zk, zv, zg, o_ret, states, dyb, rc):
    s = zq.shape[0]
    nc = s // RET_CHUNK
    c = RET_CHUNK

    def body(cd_ref, q_ref, k_ref, v_ref, g_ref, o_ref, st_ref, dy_ref, cos_ref, sin_ref, dm_ref, qd_ref,
             kd_ref, dq_ref, dk_ref, dv_ref, dg_ref, dstate):
        h, n = pl.program_id(0), pl.program_id(1)

        @pl.when(n == 0)
        def _():
            dstate[...] = jnp.zeros_like(dstate)

        cs, sn = cos_ref[...], sin_ref[...]
        qr = _rot(q_ref[...], cs, sn)
        kr = _rot(k_ref[...], cs, sn) * (RET_QK ** -0.5)
        v = v_ref[...]
        st = st_ref[0, 0]
        dm, qd, kd = dm_ref[0], qd_ref[0], kd_ref[0]
        xh, rstd = _norm_rows(o_ref[...], GN_EPS)
        gv, dy = g_ref[...], dy_ref[...]
        sg = _sigmoid(gv)
        dg_ref[...] = dy * xh * (sg * (1.0 + gv * (1.0 - sg)))
        do = _norm_rows_bwd(dy * (gv * sg), xh, rstd)
        ds_next = dstate[...]
        sc = _dot(qr, kr, NT) * dm
        da = _dot(do, v, NT) * dm
        dv = _dot(sc, do, TN) + _dot(kr * kd, ds_next, NN)
        dqr = _dot(da, kr, NN) + _dot(do, st, NT) * qd
        dkr = _dot(da, qr, TN) + _dot(v, ds_next, NT) * kd
        dstate[...] = ds_next * cd_ref[h] + _dot(qr * qd, do, TN)
        dq_ref[...] = _rot_bwd(dqr, cs, sn)
        dk_ref[...] = _rot_bwd(dkr * (RET_QK ** -0.5), cs, sn)
        dv_ref[...] = dv.astype(dv_ref.dtype)

    rev = lambda n: nc - 1 - n
    qk = lambda off: pl.BlockSpec((c, RET_QK), lambda h, n: (rev(n), off + h))
    wide = pl.BlockSpec((c, RET_V), lambda h, n: (rev(n), h))
    rope = pl.BlockSpec((c, RET_QK // 2), lambda h, n: (rev(n), 0))
    per_head = lambda w: pl.BlockSpec((1, c, w), lambda h, n: (h, 0, 0))
    f_qk = jax.ShapeDtypeStruct((s, RET_HEADS * RET_QK), F32)
    return pl.pallas_call(
        body,
        out_shape=[f_qk, f_qk, jax.ShapeDtypeStruct((s, RET_HEADS * RET_V), MXU_DTYPE),
                   jax.ShapeDtypeStruct((s, RET_HEADS * RET_V), F32)],
        grid=(RET_HEADS, nc),
        in_specs=[pl.BlockSpec(memory_space=pltpu.SMEM), qk(0), qk(0), wide, wide, wide,
                  pl.BlockSpec((1, 1, RET_QK, RET_V), lambda h, n: (h, rev(n), 0, 0)), wide, rope, rope,
                  per_head(c), per_head(RET_QK), per_head(RET_QK)],
        out_specs=[qk(0), qk(0), wide, wide],
        scratch_shapes=[pltpu.VMEM((RET_QK, RET_V), F32)],
        compiler_params=_params(2), name="ret_bwd",
    )(rc["cdec"], zq, zk, zv, zg, o_ret, states, dyb, rc["cos"], rc["sin"], rc["dmask"], rc["qdec"],
      rc["kdec"])


ROW_TILE = 256


def _mix_fwd(x, ya_b, yb_b, ze, wap_t, wrp, wout, gam, bet):
    s = x.shape[0]
    tm = ROW_TILE

    def body(x_ref, ya_ref, yb_ref, ze_ref, wap_ref, wrp_ref, wout_ref, g_ref, b_ref,
             x1_ref, u1_ref, a1_ref, a2_ref, mg_ref):
        a1 = _dot(ya_ref[...], wap_ref[...], NT)
        a2 = _dot(yb_ref[...], wrp_ref[...], NN)
        ze_ = ze_ref[...]
        merged = _sigmoid(ze_[:, :D_MODEL]) * a1 + _sigmoid(ze_[:, D_MODEL:]) * a2
        u = ALPHA * x_ref[...] + _dot(merged, wout_ref[...], NN)
        xh, _ = _norm_rows(u, LN_EPS)
        x1_ref[...] = xh * g_ref[...] + b_ref[...]
        u1_ref[...] = u
        a1_ref[...] = a1
        a2_ref[...] = a2
        mg_ref[...] = merged.astype(mg_ref.dtype)

    row = lambda w: pl.BlockSpec((tm, w), lambda i: (i, 0))
    full = lambda a: pl.BlockSpec(a.shape, lambda i: (0, 0))
    f = jax.ShapeDtypeStruct((s, D_MODEL), F32)
    return pl.pallas_call(
        body, out_shape=[f, f, f, f, jax.ShapeDtypeStruct((s, D_MODEL), MXU_DTYPE)], grid=(s // tm,),
        in_specs=[row(D_MODEL), row(ATTN_OUT), row(RET_HEADS * RET_V), row(2 * D_MODEL), full(wap_t), full(wrp),
                  full(wout), full(gam), full(bet)],
        out_specs=[row(D_MODEL)] * 5, compiler_params=_params(1), name="mix_fwd",
    )(x, ya_b, yb_b, ze, wap_t, wrp, wout, gam, bet)


def _mix_bwd(dx1, u1, a1, a2, ze, wap_t, wrp, wout, gam):
    s = dx1.shape[0]
    tm = ROW_TILE

    def body(dx_ref, u_ref, a1_ref, a2_ref, ze_ref, wap_ref, wrp_ref, wout_ref, g_ref,
             dres_ref, du_ref, da1_ref, da2_ref, dze_ref, dya_ref, dyb_ref, dgam_ref, dbet_ref):
        @pl.when(pl.program_id(0) == 0)
        def _():
            dgam_ref[...] = jnp.zeros_like(dgam_ref)
            dbet_ref[...] = jnp.zeros_like(dbet_ref)

        dx = dx_ref[...]
        xh, rstd = _norm_rows(u_ref[...], LN_EPS)
        dgam_ref[...] += jnp.sum(dx * xh, axis=0, keepdims=True)
        dbet_ref[...] += jnp.sum(dx, axis=0, keepdims=True)
        du = _norm_rows_bwd(dx * g_ref[...], xh, rstd)
        dres_ref[...] = ALPHA * du
        du_ref[...] = du.astype(du_ref.dtype)
        dm = _dot(du, wout_ref[...], NT)
        ze_ = ze_ref[...]
        sa, sb = _sigmoid(ze_[:, :D_MODEL]), _sigmoid(ze_[:, D_MODEL:])
        da1, da2 = dm * sa, dm * sb
        dze_ref[...] = jnp.concatenate([dm * a1_ref[...] * (sa * (1.0 - sa)),
                                        dm * a2_ref[...] * (sb * (1.0 - sb))], axis=1)
        da1_ref[...] = da1.astype(da1_ref.dtype)
        da2_ref[...] = da2.astype(da2_ref.dtype)
        dya_ref[...] = _dot(da1, wap_ref[...], NN)
        dyb_ref[...] = _dot(da2, wrp_ref[...], NT)

    row = lambda w: pl.BlockSpec((tm, w), lambda i: (i, 0))
    full = lambda a: pl.BlockSpec(a.shape, lambda i: (0, 0))
    vec = pl.BlockSpec((1, D_MODEL), lambda i: (0, 0))
    f = lambda w: jax.ShapeDtypeStruct((s, w), F32)
    m = lambda w: jax.ShapeDtypeStruct((s, w), MXU_DTYPE)
    v = jax.ShapeDtypeStruct((1, D_MODEL), F32)
    return pl.pallas_call(
        body,
        out_shape=[f(D_MODEL), m(D_MODEL), m(D_MODEL), m(D_MODEL), f(2 * D_MODEL), f(ATTN_OUT),
                   f(RET_HEADS * RET_V), v, v],
        grid=(s // tm,),
        in_specs=[row(D_MODEL)] * 4 + [row(2 * D_MODEL), full(wap_t), full(wrp), full(wout), full(gam)],
        out_specs=[row(D_MODEL)] * 4 + [row(2 * D_MODEL), row(ATTN_OUT), row(RET_HEADS * RET_V), vec, vec],
        compiler_params=_params(1), name="mix_bwd",
    )(dx1, u1, a1, a2, ze, wap_t, wrp, wout, gam)


FF_CHUNK = 1408


def _ffn_fwd(x1, wg_t, wu_t, wd, gam, bet):
    s = x1.shape[0]
    tm, fc = ROW_TILE, FF_CHUNK
    nf = D_FF // fc

    def body(x_ref, wg_ref, wu_ref, wd_ref, g_ref, b_ref, x2_ref, u2_ref, a_ref, b_out_ref, h_ref, acc):
        j = pl.program_id(1)
        xv = x_ref[...]

        @pl.when(j == 0)
        def _():
            acc[...] = ALPHA * xv

        a = _dot(xv, wg_ref[...], NT)
        b = _dot(xv, wu_ref[...], NT)
        hid = a * _sigmoid(a) * b
        acc[...] += _dot(hid, wd_ref[...], NN)
        a_ref[...] = a
        b_out_ref[...] = b
        h_ref[...] = hid.astype(h_ref.dtype)

        @pl.when(j == nf - 1)
        def _():
            u = acc[...]
            xh, _ = _norm_rows(u, LN_EPS)
            u2_ref[...] = u
            x2_ref[...] = xh * g_ref[...] + b_ref[...]

    row = pl.BlockSpec((tm, D_MODEL), lambda i, j: (i, 0))
    hidden = pl.BlockSpec((tm, fc), lambda i, j: (i, j))
    wchunk = pl.BlockSpec((fc, D_MODEL), lambda i, j: (j, 0))
    vec = pl.BlockSpec((1, D_MODEL), lambda i, j: (0, 0))
    f = lambda w: jax.ShapeDtypeStruct((s, w), F32)
    return pl.pallas_call(
        body, out_shape=[f(D_MODEL), f(D_MODEL), f(D_FF), f(D_FF), jax.ShapeDtypeStruct((s, D_FF), MXU_DTYPE)],
        grid=(s // tm, nf), in_specs=[row, wchunk, wchunk, wchunk, vec, vec],
        out_specs=[row, row, hidden, hidden, hidden], scratch_shapes=[pltpu.VMEM((tm, D_MODEL), F32)],
        compiler_params=_params(2), name="ffn_fwd",
    )(x1, wg_t, wu_t, wd, gam, bet)


def _ffn_bwd(dx2, u2, fa, fb, wg_t, wu_t, wd, gam):
    s = dx2.shape[0]
    tm, fc = ROW_TILE, FF_CHUNK
    nf = D_FF // fc

    def body(dx_ref, u_ref, a_ref, b_ref, wg_ref, wu_ref, wd_ref, g_ref,
             dx1_ref, du_ref, da_ref, db_ref, dgam_ref, dbet_ref, du_scr, acc):
        i, j = pl.program_id(0), pl.program_id(1)

        @pl.when(jnp.logical_and(i == 0, j == 0))
        def _():
            dgam_ref[...] = jnp.zeros_like(dgam_ref)
            dbet_ref[...] = jnp.zeros_like(dbet_ref)

        @pl.when(j == 0)
        def _():
            dx = dx_ref[...]
            xh, rstd = _norm_rows(u_ref[...], LN_EPS)
            dgam_ref[...] += jnp.sum(dx * xh, axis=0, keepdims=True)
            dbet_ref[...] += jnp.sum(dx, axis=0, keepdims=True)
            du = _norm_rows_bwd(dx * g_ref[...], xh, rstd)
            du_scr[...] = du
            acc[...] = ALPHA * du

        du = du_scr[...]
        dh = _dot(du, wd_ref[...], NT)
        a, b = a_ref[...], b_ref[...]
        sg = _sigmoid(a)
        da = dh * b * (sg * (1.0 + a * (1.0 - sg)))
        db = dh * (a * sg)
        acc[...] += _dot(da, wg_ref[...], NN) + _dot(db, wu_ref[...], NN)
        da_ref[...] = da.astype(da_ref.dtype)
        db_ref[...] = db.astype(db_ref.dtype)

        @pl.when(j == nf - 1)
        def _():
            dx1_ref[...] = acc[...]
            du_ref[...] = du.astype(du_ref.dtype)

    row = pl.BlockSpec((tm, D_MODEL), lambda i, j: (i, 0))
    hidden = pl.BlockSpec((tm, fc), lambda i, j: (i, j))
    wchunk = pl.BlockSpec((fc, D_MODEL), lambda i, j: (j, 0))
    vec = pl.BlockSpec((1, D_MODEL), lambda i, j: (0, 0))
    v = jax.ShapeDtypeStruct((1, D_MODEL), F32)
    return pl.pallas_call(
        body,
        out_shape=[jax.ShapeDtypeStruct((s, D_MODEL), F32), jax.ShapeDtypeStruct((s, D_MODEL), MXU_DTYPE),
                   jax.ShapeDtypeStruct((s, D_FF), MXU_DTYPE), jax.ShapeDtypeStruct((s, D_FF), MXU_DTYPE), v, v],
        grid=(s // tm, nf), in_specs=[row, row, hidden, hidden, wchunk, wchunk, wchunk, vec],
        out_specs=[row, row, hidden, hidden, vec, vec],
        scratch_shapes=[pltpu.VMEM((tm, D_MODEL), F32), pltpu.VMEM((tm, D_MODEL), F32)],
        compiler_params=_params(2), name="ffn_bwd",
    )(dx2, u2, fa, fb, wg_t, wu_t, wd, gam)


def _loss_head(y, target):
    s = y.shape[0]
    tr = _pick(s, (512, 256, 128))

    def body(y_ref, t_ref, dy_ref, l_ref):
        @pl.when(pl.program_id(0) == 0)
        def _():
            l_ref[...] = jnp.zeros_like(l_ref)

        e = y_ref[...] - t_ref[...]
        dy_ref[...] = e * (1.0 / D_MODEL)
        part = jnp.sum(jnp.sum(e * e, axis=1, keepdims=True), axis=0, keepdims=True)
        l_ref[...] += part * (0.5 / D_MODEL)

    spec = pl.BlockSpec((tr, D_MODEL), lambda i: (i, 0))
    dy, part = pl.pallas_call(
        body, out_shape=[jax.ShapeDtypeStruct((s, D_MODEL), F32), jax.ShapeDtypeStruct((8, BLK), F32)],
        grid=(s // tr,), in_specs=[spec, spec], out_specs=[spec, pl.BlockSpec((8, BLK), lambda i: (0, 0))],
        compiler_params=_params(1), name="loss_head",
    )(y, target)
    return dy, part[0, 0]


def _adamw(w, g, m, v, name):
    rows, cols = w.shape
    budget = 1 << 20
    cands = [t for t in range(8, rows + 1, 8) if rows % t == 0 and t * cols * 4 <= budget]
    tr = max(cands) if cands else rows

    def body(w_ref, g_ref, m_ref, v_ref, d_ref, nm_ref, nv_ref):
        gv = g_ref[...]
        mn = ADAM_B1 * m_ref[...] + (1.0 - ADAM_B1) * gv
        vn = ADAM_B2 * v_ref[...] + (1.0 - ADAM_B2) * (gv * gv)
        m_hat = mn / (1.0 - ADAM_B1 ** ADAM_STEP)
        v_hat = vn / (1.0 - ADAM_B2 ** ADAM_STEP)
        d_ref[...] = -ADAM_LR * (m_hat / (jnp.sqrt(v_hat) + ADAM_EPS) + ADAM_WD * w_ref[...])
        nm_ref[...] = mn
        nv_ref[...] = vn

    spec = pl.BlockSpec((tr, cols), lambda i: (i, 0))
    shape = jax.ShapeDtypeStruct((rows, cols), F32)
    return pl.pallas_call(
        body, out_shape=[shape, shape, shape], grid=(rows // tr,), in_specs=[spec] * 4, out_specs=[spec] * 3,
        compiler_params=_params(1), name=name,
    )(w, g, m, v)


MESH_ID = pl.DeviceIdType.MESH
ANY = pl.BlockSpec(memory_space=pl.ANY)


def _place():
    x, y, c = lax.axis_index("x"), lax.axis_index("y"), lax.axis_index("c")
    other_chips = [(1 - x, y), (x, 1 - y), (1 - x, 1 - y)]
    return x, y, c, other_chips


def _chip_no(chip):
    return 2 * chip[0] + chip[1]


def _all_gather_rows(flat):
    r, cols = flat.shape
    hr = r // 2

    def body(f_ref, g_ref, send_sems, recv_sems, local_sem):
        x, y, c, chips = _place()
        sibling = (x, y, 1 - c)

        def piece(chip, half):
            return g_ref.at[_chip_no(chip), pl.ds(half * hr, hr), :]

        def copy(k, src, dst, to):
            return pltpu.make_async_remote_copy(src_ref=src, dst_ref=dst, send_sem=send_sems.at[k],
                                                recv_sem=recv_sems.at[k], device_id=to, device_id_type=MESH_ID)

        mine = pltpu.make_async_copy(f_ref, g_ref.at[_chip_no((x, y))], local_sem)
        mine.start()
        my_half = f_ref.at[pl.ds(c * hr, hr), :]
        first = [copy(k, my_half, piece((x, y), c), (*chip, c)) for k, chip in enumerate(chips)]
        for cp in first:
            cp.start()
        passed = [copy(3 + k, piece(chip, c), piece(chip, c), sibling) for k, chip in enumerate(chips)]
        for k, chip in enumerate(chips):
            copy(k, my_half, piece(chip, c), (*chip, c)).wait_recv()
            passed[k].start()
        for k, chip in enumerate(chips):
            copy(3 + k, my_half, piece(chip, 1 - c), sibling).wait_recv()
        for cp in first + passed:
            cp.wait_send()
        mine.wait()

    return pl.pallas_call(
        body, out_shape=jax.ShapeDtypeStruct((N_CHIPS, r, cols), flat.dtype), in_specs=[ANY], out_specs=ANY,
        scratch_shapes=[pltpu.SemaphoreType.DMA((6,)), pltpu.SemaphoreType.DMA((6,)), pltpu.SemaphoreType.DMA],
        name="all_gather_weights",
    )(flat)


def _swap_halves(gall):
    _, r, cols = gall.shape
    hr = r // 2

    def body(g_ref, a_ref, send_sem, recv_sem):
        x, y, c, _ = _place()
        cp = pltpu.make_async_remote_copy(src_ref=g_ref.at[:, pl.ds((1 - c) * hr, hr), :], dst_ref=a_ref,
                                          send_sem=send_sem, recv_sem=recv_sem, device_id=(x, y, 1 - c),
                                          device_id_type=MESH_ID)
        cp.start()
        cp.wait()

    return pl.pallas_call(
        body, out_shape=jax.ShapeDtypeStruct((N_CHIPS, hr, cols), gall.dtype), in_specs=[ANY], out_specs=ANY,
        scratch_shapes=[pltpu.SemaphoreType.DMA, pltpu.SemaphoreType.DMA], name="grad_swap_halves",
    )(gall)


def _scatter_to_chips(p):
    _, hr, cols = p.shape

    def body(p_ref, b_ref, send_sems, recv_sems):
        x, y, c, chips = _place()
        cps = [pltpu.make_async_remote_copy(src_ref=p_ref.at[_chip_no(chip)], dst_ref=b_ref.at[k],
                                            send_sem=send_sems.at[k], recv_sem=recv_sems.at[k],
                                            device_id=(*chip, c), device_id_type=MESH_ID)
               for k, chip in enumerate(chips)]
        for cp in cps:
            cp.start()
        for cp in cps:
            cp.wait()

    return pl.pallas_call(
        body, out_shape=jax.ShapeDtypeStruct((3, hr, cols), p.dtype), in_specs=[ANY], out_specs=ANY,
        scratch_shapes=[pltpu.SemaphoreType.DMA((3,)), pltpu.SemaphoreType.DMA((3,))], name="grad_scatter_chips",
    )(p)


def _share_with_sibling(red):
    hr, cols = red.shape

    def body(r_ref, full_ref, send_sem, recv_sem, local_sem):
        x, y, c, _ = _place()
        mine = pltpu.make_async_copy(r_ref, full_ref.at[pl.ds(c * hr, hr), :], local_sem)
        mine.start()
        cp = pltpu.make_async_remote_copy(src_ref=r_ref, dst_ref=full_ref.at[pl.ds(c * hr, hr), :],
                                          send_sem=send_sem, recv_sem=recv_sem, device_id=(x, y, 1 - c),
                                          device_id_type=MESH_ID)
        cp.start()
        cp.wait()
        mine.wait()

    return pl.pallas_call(
        body, out_shape=jax.ShapeDtypeStruct((2 * hr, cols), red.dtype), in_specs=[ANY], out_specs=ANY,
        scratch_shapes=[pltpu.SemaphoreType.DMA, pltpu.SemaphoreType.DMA, pltpu.SemaphoreType.DMA],
        name="grad_share_sibling",
    )(red)


def _all_reduce_small(v):
    r, cols = v.shape
    n_dev = 8

    def body(x_ref, out_ref, gat_ref, send_sems, recv_sems, local_sem):
        x, y, c, chips = _place()
        me, sibling = (x, y, c), (x, y, 1 - c)

        def slot(px, py, pc):
            return gat_ref.at[4 * px + 2 * py + pc]

        def copy(k, block, to, src=None):
            return pltpu.make_async_remote_copy(src_ref=slot(*block) if src is None else src, dst_ref=slot(*block),
                                                send_sem=send_sems.at[k], recv_sem=recv_sems.at[k], device_id=to,
                                                device_id_type=MESH_ID)

        mine = pltpu.make_async_copy(x_ref, slot(*me), local_sem)
        mine.start()
        first = [copy(0, me, sibling, src=x_ref)]
        first += [copy(1 + k, me, (*chip, c), src=x_ref) for k, chip in enumerate(chips)]
        for cp in first:
            cp.start()
        passed = [copy(4 + k, (*chip, c), sibling) for k, chip in enumerate(chips)]
        for k, chip in enumerate(chips):
            copy(1 + k, (*chip, c), me).wait_recv()
            passed[k].start()
        copy(0, sibling, me).wait_recv()
        for k, chip in enumerate(chips):
            copy(4 + k, (*chip, 1 - c), me).wait_recv()
        for cp in first + passed:
            cp.wait_send()
        mine.wait()
        acc = gat_ref[0]
        for d in range(1, n_dev):
            acc = acc + gat_ref[d]
        out_ref[...] = acc

    vmem = pl.BlockSpec(memory_space=pltpu.VMEM)
    return pl.pallas_call(
        body, out_shape=jax.ShapeDtypeStruct((r, cols), v.dtype), in_specs=[vmem], out_specs=vmem,
        scratch_shapes=[pltpu.VMEM((n_dev, r, cols), v.dtype), pltpu.SemaphoreType.DMA((7,)),
                        pltpu.SemaphoreType.DMA((7,)), pltpu.SemaphoreType.DMA],
        name="all_reduce_small",
    )(v)


def _add_own_half(gall, got, c_idx):
    _, r, cols = gall.shape
    hr = r // 2
    tr = _pick(hr, (256, 128, 64, 32, 16, 8))
    g4 = gall.reshape(N_CHIPS, 2, hr, cols)

    def body(c_ref, g_ref, a_ref, o_ref):
        o_ref[...] = g_ref[0] + a_ref[...]

    return pl.pallas_call(
        body, out_shape=jax.ShapeDtypeStruct((N_CHIPS, hr, cols), F32),
        grid_spec=pltpu.PrefetchScalarGridSpec(
            num_scalar_prefetch=1, grid=(N_CHIPS, hr // tr),
            in_specs=[pl.BlockSpec((1, 1, tr, cols), lambda j, i, c_ref: (j, c_ref[0], i, 0)),
                      pl.BlockSpec((1, tr, cols), lambda j, i, c_ref: (j, i, 0))],
            out_specs=pl.BlockSpec((1, tr, cols), lambda j, i, c_ref: (j, i, 0))),
        compiler_params=_params(2), name="grad_add_halves",
    )(c_idx, g4, got)


def _add_chip_parts(p, got, chip_idx):
    _, hr, cols = p.shape
    tr = _pick(hr, (256, 128, 64, 32, 16, 8))

    def body(j_ref, p_ref, b0_ref, b1_ref, b2_ref, o_ref):
        o_ref[...] = ((p_ref[0] + b0_ref[0]) + b1_ref[0]) + b2_ref[0]

    part = lambda k: pl.BlockSpec((1, tr, cols), lambda i, j_ref: (k, i, 0))
    return pl.pallas_call(
        body, out_shape=jax.ShapeDtypeStruct((hr, cols), F32),
        grid_spec=pltpu.PrefetchScalarGridSpec(
            num_scalar_prefetch=1, grid=(hr // tr,),
            in_specs=[pl.BlockSpec((1, tr, cols), lambda i, j_ref: (j_ref[0], i, 0)), part(0), part(1), part(2)],
            out_specs=pl.BlockSpec((tr, cols), lambda i, j_ref: (i, 0))),
        compiler_params=_params(1), name="grad_add_chips",
    )(chip_idx, p, got, got, got)


def _pack_shards(w_in, w_ap, w_rp, w_out, w_g, w_u, w_d, dtype):
    parts = []
    for l in range(DEPTH):
        parts += [w_in[l].T, w_ap[l].T.reshape(-1, D_MODEL), w_rp[l], w_out[l], w_g[l].T, w_u[l].T, w_d[l]]
    return jnp.concatenate([p.astype(dtype) for p in parts], axis=0)


def _unpack_gathered(gathered):
    layers = []
    off = 0
    for _ in range(DEPTH):
        w = {}
        for name, rows in PACK_ROWS:
            blk = gathered[:, off:off + rows]
            if name == "w_attn_proj":
                w[name] = blk.reshape(N_CHIPS * 256, ATTN_OUT)
            else:
                w[name] = blk.reshape(N_CHIPS * rows, D_MODEL)
            off += rows
        layers.append(w)
    return layers


def _pack_full_grads(grads):
    parts = []
    for g in grads:
        for name, rows in PACK_ROWS:
            parts.append(g[name].reshape(N_CHIPS, rows, D_MODEL))
    return jnp.concatenate(parts, axis=1)


def _unpack_shard_grads(full):
    out = {name: [] for name, _ in PACK_ROWS}
    off = 0
    for _ in range(DEPTH):
        for name, rows in PACK_ROWS:
            blk = full[off:off + rows]
            if name == "w_attn_proj":
                blk = blk.reshape(256, ATTN_OUT).T
            elif name in ("w_in", "w_ffn_gate", "w_ffn_up"):
                blk = blk.T
            out[name].append(blk)
            off += rows
    return {k: jnp.stack(v) for k, v in out.items()}


def _attn_views(za, s):
    views = []
    for g, (_, dil) in enumerate(ATTN_GROUPS):
        if dil == 1:
            views.append((za, za, za, ATTN_WIDTH * 3 // BLK, (0, 9, 18)))
        else:
            rows = s // dil
            part = lambda k: za[:, k * ATTN_WIDTH + g * ATTN_OUT:k * ATTN_WIDTH + (g + 1) * ATTN_OUT].reshape(
                rows, dil * ATTN_OUT)
            views.append((part(0), part(1), part(2), 3, (0, 0, 0)))
    return views


def _layer_fwd(x, w, lnp, bias, rc, tag):
    s = x.shape[0]
    z = {}
    for name, off, width, narrow in SECTIONS:
        z[name] = _matmul(x, w["w_in"][off:off + width], "nt", bias=lnp["b_in"][:, off:off + width],
                          out_dtype=MXU_DTYPE if narrow else F32, name=f"in_proj_{name}_{tag}")
    views = _attn_views(z["a"], s)
    o_list, l_list = [], []
    for g, (_, dil) in enumerate(ATTN_GROUPS):
        qa, ka, va, cb, offs = views[g]
        o, lse = _attn_fwd(qa, ka, va, cb, offs, bias, g, dil)
        o_list.append(o.reshape(s, ATTN_OUT))
        l_list.append(lse.reshape(s, ATTN_OUT))
    ya, ya_b, lt = _attn_combine(o_list, l_list)
    yb_b, o_ret, states = _ret_fwd(z["bq"], z["bk"], z["c"], z["d"], rc)
    x1, u1, a1, a2, mg_b = _mix_fwd(x, ya_b, yb_b, z["e"], w["w_attn_proj"], w["w_ret_proj"], w["w_out"],
                                    lnp["ln1_g"], lnp["ln1_b"])
    x2, u2, fa, fb, h_b = _ffn_fwd(x1, w["w_ffn_gate"], w["w_ffn_up"], w["w_ffn_down"], lnp["ln2_g"], lnp["ln2_b"])
    saved = dict(x=x, z=z, views=views, ya=ya, ya_b=ya_b, lt=lt, yb_b=yb_b, o_ret=o_ret, states=states,
                 x1=x1, u1=u1, a1=a1, a2=a2, mg_b=mg_b, u2=u2, fa=fa, fb=fb, h_b=h_b)
    return x2, saved


def _layer_bwd(dx2, w, lnp, sv, bias, rc, tag):
    s = dx2.shape[0]
    z = sv["z"]
    dx1, du2_b, da_b, db_b, dg2, dbt2 = _ffn_bwd(dx2, sv["u2"], sv["fa"], sv["fb"], w["w_ffn_gate"], w["w_ffn_up"],
                                                  w["w_ffn_down"], lnp["ln2_g"])
    gw = {}
    gw["w_ffn_down"] = _matmul(sv["h_b"], du2_b, "tn", name=f"dw_ffn_down_{tag}")
    gw["w_ffn_gate"] = _matmul(da_b, sv["x1"], "tn", name=f"dw_ffn_gate_{tag}")
    gw["w_ffn_up"] = _matmul(db_b, sv["x1"], "tn", name=f"dw_ffn_up_{tag}")
    dres, du1_b, da1_b, da2_b, dze, dya, dyb, dg1, dbt1 = _mix_bwd(
        dx1, sv["u1"], sv["a1"], sv["a2"], z["e"], w["w_attn_proj"], w["w_ret_proj"], w["w_out"], lnp["ln1_g"])
    gw["w_out"] = _matmul(sv["mg_b"], du1_b, "tn", name=f"dw_out_{tag}")
    gw["w_attn_proj"] = _matmul(da1_b, sv["ya_b"], "tn", name=f"dw_attn_proj_{tag}")
    gw["w_ret_proj"] = _matmul(sv["yb_b"], da2_b, "tn", name=f"dw_ret_proj_{tag}")
    dzq, dzk, dzv, dzg = _ret_bwd(z["bq"], z["bk"], z["c"], z["d"], sv["o_ret"], sv["states"], dyb, rc)
    dq_l, dk_l, dv_l, dbias_l = [], [], [], []
    for g, (_, dil) in enumerate(ATTN_GROUPS):
        qa, ka, va, cb, offs = sv["views"][g]
        rows = s // dil
        view = lambda t: t.reshape(rows, dil * ATTN_OUT)
        dq, dk, dv, dbg = _attn_bwd(qa, ka, va, cb, offs, bias, view(dya), view(sv["ya"]), view(sv["lt"]), g, dil)
        dq_l.append(dq.reshape(s, ATTN_OUT))
        dk_l.append(dk.reshape(s, ATTN_OUT))
        dv_l.append(dv.reshape(s, ATTN_OUT))
        dbias_l.append(dbg)
    dz = {"a": jnp.concatenate(dq_l + dk_l + dv_l, axis=1), "bq": dzq, "bk": dzk, "c": dzv, "d": dzg, "e": dze}
    dx = dres
    dw_rows, db_cols = [], []
    for name, off, width, _ in SECTIONS:
        dx = _matmul(dz[name], w["w_in"][off:off + width], "nn", addend=dx, name=f"dx_in_proj_{name}_{tag}")
        dw, cs = _matmul(dz[name], sv["x"], "tn", colsum=True, name=f"dw_in_proj_{name}_{tag}")
        dw_rows.append(dw)
        db_cols.append(cs)
    gw["w_in"] = jnp.concatenate(dw_rows, axis=0)
    small = dict(b_in=jnp.concatenate(db_cols, axis=1), ln1_g=dg1, ln1_b=dbt1, ln2_g=dg2, ln2_b=dbt2)
    return dx, gw, small, jnp.concatenate(dbias_l, axis=0)


def _forward_backward(x, target, rel_bias, layers, lnps):
    s = x.shape[0]
    bmaps = _bucket_maps()
    bias = _bias_tiles(rel_bias, bmaps)
    rc = _ret_consts(s)
    saved = []
    h = x
    for l in range(DEPTH):
        h, sv = _layer_fwd(h, layers[l], lnps[l], bias, rc, f"l{l}")
        saved.append(sv)
    dh, loss_part = _loss_head(h, target)
    gws, smalls, dbiases = [None] * DEPTH, [None] * DEPTH, [None] * DEPTH
    for l in reversed(range(DEPTH)):
        dh, gws[l], smalls[l], dbiases[l] = _layer_bwd(dh, layers[l], lnps[l], saved[l], bias, rc, f"l{l}")
    d_rel_bias = _bias_tiles_bwd(dbiases, bmaps)
    return loss_part, dh, gws, smalls, d_rel_bias


SMALL_NAMES = ("rel_bias", "b_in", "ln1_g", "ln1_b", "ln2_g", "ln2_b")
SMALL_ROWS = 32


def _pack_small(vals):
    flat = jnp.concatenate([vals[n].reshape(-1) for n in SMALL_NAMES])
    return jnp.pad(flat, (0, SMALL_ROWS * D_MODEL - flat.shape[0])).reshape(SMALL_ROWS, D_MODEL)


def _unpack_small(packed, like):
    flat = packed.reshape(-1)
    out, off = {}, 0
    for n in SMALL_NAMES:
        size = like[n].size
        out[n] = flat[off:off + size].reshape(like[n].shape)
        off += size
    return out


def kernel(x, rel_bias, w_in, b_in, w_attn_proj, w_ret_proj, w_out, ln1_g, ln1_b, w_ffn_gate, w_ffn_up, w_ffn_down, ln2_g, ln2_b, loss_target, m_rel_bias, m_w_in, m_b_in, m_w_attn_proj, m_w_ret_proj, m_w_out, m_ln1_g, m_ln1_b, m_w_ffn_gate, m_w_ffn_up, m_w_ffn_down, m_ln2_g, m_ln2_b, v_rel_bias, v_w_in, v_b_in, v_w_attn_proj, v_w_ret_proj, v_w_out, v_ln1_g, v_ln1_b, v_w_ffn_gate, v_w_ffn_up, v_w_ffn_down, v_ln2_g, v_ln2_b):
    big = dict(w_in=(w_in, m_w_in, v_w_in), w_attn_proj=(w_attn_proj, m_w_attn_proj, v_w_attn_proj),
               w_ret_proj=(w_ret_proj, m_w_ret_proj, v_w_ret_proj), w_out=(w_out, m_w_out, v_w_out),
               w_ffn_gate=(w_ffn_gate, m_w_ffn_gate, v_w_ffn_gate), w_ffn_up=(w_ffn_up, m_w_ffn_up, v_w_ffn_up),
               w_ffn_down=(w_ffn_down, m_w_ffn_down, v_w_ffn_down))
    small_w = dict(rel_bias=rel_bias, b_in=b_in, ln1_g=ln1_g, ln1_b=ln1_b, ln2_g=ln2_g, ln2_b=ln2_b)
    small_m = dict(rel_bias=m_rel_bias, b_in=m_b_in, ln1_g=m_ln1_g, ln1_b=m_ln1_b, ln2_g=m_ln2_g, ln2_b=m_ln2_b)
    small_v = dict(rel_bias=v_rel_bias, b_in=v_b_in, ln1_g=v_ln1_g, ln1_b=v_ln1_b, ln2_g=v_ln2_g, ln2_b=v_ln2_b)

    c_idx = lax.axis_index("c").astype(jnp.int32).reshape(1)
    chip_idx = (2 * lax.axis_index("x") + lax.axis_index("y")).astype(jnp.int32).reshape(1)

    flat = _pack_shards(w_in, w_attn_proj, w_ret_proj, w_out, w_ffn_gate, w_ffn_up, w_ffn_down, MXU_DTYPE)
    layers = _unpack_gathered(_all_gather_rows(flat))
    lnps = [dict(b_in=b_in[l][None], ln1_g=ln1_g[l][None], ln1_b=ln1_b[l][None], ln2_g=ln2_g[l][None],
                 ln2_b=ln2_b[l][None]) for l in range(DEPTH)]

    loss_part, dx, gws, smalls, d_rel_bias = _forward_backward(x[0], loss_target[0], rel_bias, layers, lnps)
    loss = lax.psum(loss_part, ("x", "y", "c"))

    gall = _pack_full_grads(gws)
    part = _add_own_half(gall, _swap_halves(gall), c_idx)
    red = _add_chip_parts(part, _scatter_to_chips(part), chip_idx)
    grads = _unpack_shard_grads(_share_with_sibling(red))

    small_g = dict(rel_bias=d_rel_bias)
    for n in SMALL_NAMES[1:]:
        small_g[n] = jnp.concatenate([smalls[l][n] for l in range(DEPTH)], axis=0)
    small_g = _unpack_small(_all_reduce_small(_pack_small(small_g)), small_w)

    delta, new_m, new_v = {}, {}, {}
    for n, (w, m, v) in big.items():
        two_d = lambda t: t.reshape(-1, t.shape[-1])
        d_, m_, v_ = _adamw(two_d(w), two_d(grads[n]), two_d(m), two_d(v), name=f"adamw_{n}")
        delta[n], new_m[n], new_v[n] = d_.reshape(w.shape), m_.reshape(w.shape), v_.reshape(w.shape)
    d_, m_, v_ = _adamw(_pack_small(small_w), _pack_small(small_g), _pack_small(small_m), _pack_small(small_v),
                        name="adamw_small")
    delta.update(_unpack_small(d_, small_w))
    new_m.update(_unpack_small(m_, small_w))
    new_v.update(_unpack_small(v_, small_w))
    grads.update(small_g)

    order = ("rel_bias", "w_in", "b_in", "w_attn_proj", "w_ret_proj", "w_out", "ln1_g", "ln1_b", "w_ffn_gate",
             "w_ffn_up", "w_ffn_down", "ln2_g", "ln2_b")
    return (loss, dx[None], *[grads[n] for n in order], *[delta[n] for n in order], *[new_m[n] for n in order],
            *[new_v[n] for n in order])
```

```python
import functools

import numpy as np
import jax
import jax.numpy as jnp
from jax import lax
from jax.experimental import pallas as pl
from jax.experimental.pallas import tpu as pltpu

F32 = jnp.float32
MXU_DTYPE = jnp.bfloat16

DEPTH = 2
D_MODEL = 1024
HEAD_DIM = 64
ATTN_GROUPS = ((128, 1), (512, 4), (2048, 16))
N_GROUPS = len(ATTN_GROUPS)
HEADS_PER_GROUP = 6
N_ATTN_HEADS = 18
ATTN_WIDTH = 1152
ATTN_OUT = 384
NUM_BUCKETS = 32
MAX_DISTANCE = 2048
RET_HEADS = 4
RET_QK = 256
RET_V = 512
RET_CHUNK = 128
ROPE_BASE = 10000.0
D_FF = 2816
IN_COLS = 11648
ALPHA = (2 * DEPTH) ** 0.25
LN_EPS = 1e-5
GN_EPS = 1e-5
ADAM_LR, ADAM_B1, ADAM_B2, ADAM_EPS, ADAM_WD, ADAM_STEP = 0.001, 0.9, 0.999, 1e-08, 0.01, 10

BLK = 128
NEG = -1e30
N_CHIPS = 4
VMEM_LIMIT = 48 * 1024 * 1024

SECTIONS = (
    ("a", 0, 3456, True),
    ("bq", 3456, 1024, False),
    ("bk", 4480, 1024, False),
    ("c", 5504, 2048, True),
    ("d", 7552, 2048, False),
    ("e", 9600, 2048, False),
)
PACK_ROWS = (("w_in", 2912), ("w_attn_proj", 96), ("w_ret_proj", 512), ("w_out", 256),
             ("w_ffn_gate", 704), ("w_ffn_up", 704), ("w_ffn_down", 704))
ROWS_PER_LAYER = sum(r for _, r in PACK_ROWS)

NN = ((1,), (0,))
NT = ((1,), (1,))
TN = ((0,), (0,))


def _dot(a, b, dims):
    return lax.dot_general(a.astype(MXU_DTYPE), b.astype(MXU_DTYPE), (dims, ((), ())),
                           preferred_element_type=F32)


def _pick(n, prefs):
    for p in prefs:
        if n % p == 0:
            return p
    raise ValueError(f"no tile for {n} among {prefs}")


TOKEN_TILES = (1024, 512, 256, 128)
FEATURE_TILES = (1152, 1024, 1408, 384, 256, 128)


def _params(n_axes, limit=VMEM_LIMIT):
    return pltpu.CompilerParams(dimension_semantics=("arbitrary",) * n_axes, vmem_limit_bytes=limit)


def _resident(a):
    return pl.BlockSpec(a.shape, lambda i: (0,) * a.ndim, pipeline_mode=pl.Buffered(1))


def _sigmoid(x):
    return 1.0 / (1.0 + jnp.exp(-x))


def _norm_rows(u, eps):
    mu = jnp.mean(u, axis=-1, keepdims=True)
    xc = u - mu
    var = jnp.mean(xc * xc, axis=-1, keepdims=True)
    rstd = lax.rsqrt(var + eps)
    return xc * rstd, rstd


def _norm_rows_bwd(dxh, xh, rstd):
    c1 = jnp.mean(dxh, axis=-1, keepdims=True)
    c2 = jnp.mean(dxh * xh, axis=-1, keepdims=True)
    return rstd * (dxh - c1 - xh * c2)


def _matmul(a, b, mode, *, name, out_dtype=F32, bias=None, addend=None, colsum=False):
    if mode == "tn":
        kd, m = a.shape
        n = b.shape[1]
        tm, tn_, tk = _pick(m, FEATURE_TILES), _pick(n, FEATURE_TILES), _pick(kd, TOKEN_TILES)
        a_spec = pl.BlockSpec((tk, tm), lambda i, j, k: (k, i))
        b_spec = pl.BlockSpec((tk, tn_), lambda i, j, k: (k, j))
    else:
        m, kd = a.shape
        n = b.shape[0] if mode == "nt" else b.shape[1]
        tm, tn_, tk = _pick(m, TOKEN_TILES), _pick(n, FEATURE_TILES), _pick(kd, FEATURE_TILES)
        a_spec = pl.BlockSpec((tm, tk), lambda i, j, k: (i, k))
        if mode == "nt":
            b_spec = pl.BlockSpec((tn_, tk), lambda i, j, k: (j, k))
        else:
            b_spec = pl.BlockSpec((tk, tn_), lambda i, j, k: (k, j))
    dims = {"nn": NN, "nt": NT, "tn": TN}[mode]
    nk = kd // tk
    has_bias, has_add = bias is not None, addend is not None
    assert not colsum or mode == "tn"

    def body(*refs):
        it = iter(refs)
        a_ref, b_ref = next(it), next(it)
        bias_ref = next(it) if has_bias else None
        add_ref = next(it) if has_add else None
        o_ref = next(it)
        cs_ref = next(it) if colsum else None
        acc_ref = next(it)
        j, k = pl.program_id(1), pl.program_id(2)

        @pl.when(k == 0)
        def _():
            if has_add:
                acc_ref[...] = add_ref[...].astype(F32)
            else:
                acc_ref[...] = jnp.zeros_like(acc_ref)

        av = a_ref[...]
        acc_ref[...] += _dot(av, b_ref[...], dims)
        if colsum:
            @pl.when(jnp.logical_and(j == 0, k == 0))
            def _():
                cs_ref[...] = jnp.zeros_like(cs_ref)

            @pl.when(j == 0)
            def _():
                cs_ref[...] += jnp.sum(av.astype(F32), axis=0, keepdims=True)

        @pl.when(k == nk - 1)
        def _():
            r = acc_ref[...]
            if has_bias:
                r = r + bias_ref[...]
            o_ref[...] = r.astype(out_dtype)

    in_specs, args = [a_spec, b_spec], [a, b]
    if has_bias:
        in_specs.append(pl.BlockSpec((1, tn_), lambda i, j, k: (0, j)))
        args.append(bias)
    if has_add:
        in_specs.append(pl.BlockSpec((tm, tn_), lambda i, j, k: (i, j)))
        args.append(addend)
    out_shape = [jax.ShapeDtypeStruct((m, n), out_dtype)]
    out_specs = [pl.BlockSpec((tm, tn_), lambda i, j, k: (i, j))]
    if colsum:
        out_shape.append(jax.ShapeDtypeStruct((1, m), F32))
        out_specs.append(pl.BlockSpec((1, tm), lambda i, j, k: (0, i)))
    res = pl.pallas_call(
        body, out_shape=out_shape, grid=(m // tm, n // tn_, nk), in_specs=in_specs, out_specs=out_specs,
        scratch_shapes=[pltpu.VMEM((tm, tn_), F32)], compiler_params=_params(3), name=name,
    )(*args)
    return res if colsum else res[0]


def _t5_bucket(dist):
    max_exact = NUM_BUCKETS // 2
    large = max_exact + (np.log(np.maximum(dist, max_exact) / max_exact)
                         / np.log(MAX_DISTANCE / max_exact) * (NUM_BUCKETS - max_exact)).astype(np.int32)
    large = np.minimum(large, NUM_BUCKETS - 1)
    return np.where(dist < max_exact, dist, large).astype(np.int32)


def _bucket_maps():
    qi = np.arange(BLK)[:, None]
    kj = np.arange(2 * BLK)[None, :]
    rel = np.clip(qi + BLK - kj, 0, BLK)
    return jnp.asarray(np.stack([_t5_bucket(rel * d) for _, d in ATTN_GROUPS]))


def _bias_tiles(rel_bias, bmaps):
    def body(tab_ref, bm_ref, o_ref):
        h = pl.program_id(0)
        bm = bm_ref[0]
        acc = jnp.zeros((BLK, 2 * BLK), F32)
        for b in range(NUM_BUCKETS):
            acc = jnp.where(bm == b, tab_ref[b, h], acc)
        o_ref[0] = acc

    return pl.pallas_call(
        body, out_shape=jax.ShapeDtypeStruct((N_ATTN_HEADS, BLK, 2 * BLK), F32), grid=(N_ATTN_HEADS,),
        in_specs=[pl.BlockSpec(memory_space=pltpu.SMEM),
                  pl.BlockSpec((1, BLK, 2 * BLK), lambda h: (h // HEADS_PER_GROUP, 0, 0))],
        out_specs=pl.BlockSpec((1, BLK, 2 * BLK), lambda h: (h, 0, 0)),
        compiler_params=_params(1), name="bias_tiles",
    )(rel_bias, bmaps)


def _bias_tiles_bwd(dbias_layers, bmaps):
    nl = len(dbias_layers)

    def body(*refs):
        bm = refs[nl][0]
        o_ref = refs[nl + 1]
        x = refs[0][0]
        for r in refs[1:nl]:
            x = x + r[0]
        lane = lax.broadcasted_iota(jnp.int32, (1, BLK), 1)
        row = jnp.zeros((1, BLK), F32)
        for b in range(NUM_BUCKETS):
            s = jnp.sum(jnp.where(bm == b, x, 0.0), axis=1, keepdims=True)
            s = jnp.sum(s, axis=0, keepdims=True)
            row = jnp.where(lane == b, s, row)
        o_ref[0] = row

    tile = pl.BlockSpec((1, BLK, 2 * BLK), lambda h: (h, 0, 0))
    out = pl.pallas_call(
        body, out_shape=jax.ShapeDtypeStruct((N_ATTN_HEADS, 1, BLK), F32), grid=(N_ATTN_HEADS,),
        in_specs=[tile] * nl + [pl.BlockSpec((1, BLK, 2 * BLK), lambda h: (h // HEADS_PER_GROUP, 0, 0))],
        out_specs=pl.BlockSpec((1, 1, BLK), lambda h: (h, 0, 0)),
        compiler_params=_params(1), name="bias_tiles_bwd",
    )(*dbias_layers, bmaps)
    return out[:, 0, :NUM_BUCKETS].T


def _pair_masks():
    lane = lax.broadcasted_iota(jnp.int32, (BLK, BLK), 1)
    row2 = lax.broadcasted_iota(jnp.int32, (2 * BLK, BLK), 0)
    lane2 = lax.broadcasted_iota(jnp.int32, (2 * BLK, BLK), 1)
    own = (lane2 // HEAD_DIM) == (row2 // BLK)
    qi = lax.broadcasted_iota(jnp.int32, (2 * BLK, 2 * BLK), 0) & (BLK - 1)
    kj = lax.broadcasted_iota(jnp.int32, (2 * BLK, 2 * BLK), 1)
    band = jnp.logical_and(kj >= qi, kj <= qi + BLK)
    return lane < HEAD_DIM, own, band, kj < BLK


def _pair_scores(q32, kb, bias2, own, band, is_prev, pen):
    qm = jnp.where(own, jnp.concatenate([q32, q32], axis=0), 0.0)
    s = _dot(qm, kb, NT) * (HEAD_DIM ** -0.5) + bias2
    if pen is not None:
        s = s + jnp.where(is_prev, pen, 0.0)
    return jnp.where(band, s, NEG), qm


HEAD_PAIRS = HEADS_PER_GROUP // 2


def _attn_fwd(qa, ka, va, cb, offs, bias, g, dil):
    rows = qa.shape[0]
    nb = rows // BLK
    rb = 2 if nb % 2 == 0 else 1
    oq, ok, ov = offs

    def cur(off):
        return pl.BlockSpec((rb * BLK, ATTN_OUT), lambda r, n: (n, r * cb + off))

    def prev(off):
        return pl.BlockSpec((BLK, ATTN_OUT), lambda r, n: (jnp.maximum(rb * n - 1, 0), r * cb + off))

    def body(q_ref, kp_ref, kc_ref, vp_ref, vc_ref, b_ref, o_ref, l_ref):
        n = pl.program_id(1)
        pen0 = jnp.where(n > 0, 0.0, NEG)
        first, own, band, is_prev = _pair_masks()
        o_rows, l_rows = [], []
        for t in range(rb):
            rows_t = slice(t * BLK, (t + 1) * BLK)
            o_parts, l_parts = [], []
            for hp in range(HEAD_PAIRS):
                cols = slice(hp * BLK, (hp + 1) * BLK)
                if t == 0:
                    kp, vp, pen = kp_ref[:, cols], vp_ref[:, cols], pen0
                else:
                    before = slice((t - 1) * BLK, t * BLK)
                    kp, vp, pen = kc_ref[before, cols], vc_ref[before, cols], None
                kb = jnp.concatenate([kp, kc_ref[rows_t, cols]], axis=0)
                vb = jnp.concatenate([vp, vc_ref[rows_t, cols]], axis=0)
                bias2 = jnp.concatenate([b_ref[2 * hp], b_ref[2 * hp + 1]], axis=0)
                s, _ = _pair_scores(q_ref[rows_t, cols].astype(F32), kb, bias2, own, band, is_prev, pen)
                m = jnp.max(s, axis=1, keepdims=True)
                p = jnp.exp(s - m)
                l = jnp.sum(p, axis=1, keepdims=True)
                o2 = _dot(p * (1.0 / l), vb, NN)
                lse2 = m + jnp.log(l)
                o_parts.append(jnp.where(first, o2[:BLK], o2[BLK:]))
                l_parts.append(jnp.where(first, lse2[:BLK], lse2[BLK:]))
            o_rows.append(jnp.concatenate(o_parts, axis=1))
            l_rows.append(jnp.concatenate(l_parts, axis=1))
        o_ref[...] = jnp.concatenate(o_rows, axis=0)
        l_ref[...] = jnp.concatenate(l_rows, axis=0)

    out_spec = pl.BlockSpec((rb * BLK, ATTN_OUT), lambda r, n: (n, r))
    shape = jax.ShapeDtypeStruct((rows, dil * ATTN_OUT), F32)
    return pl.pallas_call(
        body, out_shape=[shape, shape], grid=(dil, nb // rb),
        in_specs=[cur(oq), prev(ok), cur(ok), prev(ov), cur(ov),
                  pl.BlockSpec((HEADS_PER_GROUP, BLK, 2 * BLK), lambda r, n: (g, 0, 0))],
        out_specs=[out_spec, out_spec], compiler_params=_params(2), name=f"attn_fwd_g{g}",
    )(qa, ka, ka, va, va, bias)


def _attn_bwd(qa, ka, va, cb, offs, bias, dy, ya, lt, g, dil):
    rows = qa.shape[0]
    nb = rows // BLK
    oq, ok, ov = offs

    def cur(off, c):
        return pl.BlockSpec((BLK, ATTN_OUT), lambda r, n: (jnp.minimum(n, nb - 1), r * c + off))

    def prev(off, c):
        return pl.BlockSpec((BLK, ATTN_OUT), lambda r, n: (jnp.clip(n - 1, 0, nb - 1), r * c + off))

    def body(q_ref, kp_ref, kc_ref, vp_ref, vc_ref, b_ref, dy_ref, ya_ref, lt_ref,
             dq_ref, dk_ref, dv_ref, db_ref, ck_ref, cv_ref):
        r, n = pl.program_id(0), pl.program_id(1)

        @pl.when(jnp.logical_and(r == 0, n == 0))
        def _():
            db_ref[...] = jnp.zeros_like(db_ref)

        @pl.when(n == 0)
        def _():
            ck_ref[...] = jnp.zeros_like(ck_ref)
            cv_ref[...] = jnp.zeros_like(cv_ref)

        @pl.when(n < nb)
        def _():
            pen = jnp.where(n > 0, 0.0, NEG)
            first, own, band, is_prev = _pair_masks()
            second = jnp.logical_not(first)
            scale = HEAD_DIM ** -0.5
            parts = {k: [] for k in ("dq", "dkp", "dkc", "dvp", "dvc")}
            db_parts = []
            for hp in range(HEAD_PAIRS):
                cols = slice(hp * BLK, (hp + 1) * BLK)
                kb = jnp.concatenate([kp_ref[:, cols], kc_ref[:, cols]], axis=0)
                vb = jnp.concatenate([vp_ref[:, cols], vc_ref[:, cols]], axis=0)
                bias2 = jnp.concatenate([b_ref[2 * hp], b_ref[2 * hp + 1]], axis=0)
                dy_, lt_ = dy_ref[:, cols], lt_ref[:, cols]
                dyy = dy_ * ya_ref[:, cols]
                per_head = lambda t, red, fill: jnp.concatenate(
                    [red(jnp.where(first, t, fill), axis=1, keepdims=True),
                     red(jnp.where(second, t, fill), axis=1, keepdims=True)], axis=0)
                lse2 = per_head(lt_, jnp.max, NEG)
                delta2 = per_head(dyy, jnp.sum, 0.0)
                s, qm = _pair_scores(q_ref[:, cols].astype(F32), kb, bias2, own, band, is_prev, pen)
                p = jnp.exp(s - lse2)
                dym = jnp.where(own, jnp.concatenate([dy_, dy_], axis=0), 0.0)
                ds = p * (_dot(dym, vb, NT) - delta2)
                db_parts += [ds[:BLK], ds[BLK:]]
                dq2 = _dot(ds, kb, NN) * scale
                dkb = _dot(ds, qm, TN) * scale
                dvb = _dot(p, dym, TN)
                for k, val in (("dq", jnp.where(first, dq2[:BLK], dq2[BLK:])), ("dkp", dkb[:BLK]),
                               ("dkc", dkb[BLK:]), ("dvp", dvb[:BLK]), ("dvc", dvb[BLK:])):
                    parts[k].append(val)
            wide = {k: jnp.concatenate(val, axis=1) for k, val in parts.items()}
            db_ref[...] += jnp.stack(db_parts, axis=0)
            dq_ref[...] = wide["dq"].astype(dq_ref.dtype)
            dk_ref[...] = (ck_ref[...] + wide["dkp"]).astype(dk_ref.dtype)
            dv_ref[...] = (cv_ref[...] + wide["dvp"]).astype(dv_ref.dtype)
            ck_ref[...] = wide["dkc"]
            cv_ref[...] = wide["dvc"]

        @pl.when(n == nb)
        def _():
            dk_ref[...] = ck_ref[...].astype(dk_ref.dtype)
            dv_ref[...] = cv_ref[...].astype(dv_ref.dtype)

    late = pl.BlockSpec((BLK, ATTN_OUT), lambda r, n: (jnp.maximum(n - 1, 0), r))
    shape = jax.ShapeDtypeStruct((rows, dil * ATTN_OUT), MXU_DTYPE)
    return pl.pallas_call(
        body,
        out_shape=[shape, shape, shape, jax.ShapeDtypeStruct((HEADS_PER_GROUP, BLK, 2 * BLK), F32)],
        grid=(dil, nb + 1),
        in_specs=[cur(oq, cb), prev(ok, cb), cur(ok, cb), prev(ov, cb), cur(ov, cb),
                  pl.BlockSpec((HEADS_PER_GROUP, BLK, 2 * BLK), lambda r, n: (g, 0, 0)),
                  cur(0, 1), cur(0, 1), cur(0, 1)],
        out_specs=[cur(0, 1), late, late, pl.BlockSpec((HEADS_PER_GROUP, BLK, 2 * BLK), lambda r, n: (0, 0, 0))],
        scratch_shapes=[pltpu.VMEM((BLK, ATTN_OUT), F32), pltpu.VMEM((BLK, ATTN_OUT), F32)],
        compiler_params=_params(2), name=f"attn_bwd_g{g}",
    )(qa, ka, ka, va, va, bias, dy, ya, lt)


def _attn_combine(o_list, l_list):
    s = o_list[0].shape[0]
    tr = _pick(s, (1024, 512, 256, 128))

    def body(o0, o1, o2, l0, l1, l2, y_ref, yb_ref, lt_ref):
        a0, a1, a2 = l0[...], l1[...], l2[...]
        mx = jnp.maximum(jnp.maximum(a0, a1), a2)
        e0, e1, e2 = jnp.exp(a0 - mx), jnp.exp(a1 - mx), jnp.exp(a2 - mx)
        den = e0 + e1 + e2
        inv = 1.0 / den
        y = (e0 * inv) * o0[...] + (e1 * inv) * o1[...] + (e2 * inv) * o2[...]
        y_ref[...] = y
        yb_ref[...] = y.astype(yb_ref.dtype)
        lt_ref[...] = mx + jnp.log(den)

    spec = pl.BlockSpec((tr, ATTN_OUT), lambda i: (i, 0))
    f = jax.ShapeDtypeStruct((s, ATTN_OUT), F32)
    return pl.pallas_call(
        body, out_shape=[f, jax.ShapeDtypeStruct((s, ATTN_OUT), MXU_DTYPE), f], grid=(s // tr,),
        in_specs=[spec] * 6, out_specs=[spec] * 3, compiler_params=_params(1), name="attn_combine",
    )(*o_list, *l_list)


def _ret_consts(s):
    half = RET_QK // 2
    pos = jnp.arange(s, dtype=F32)
    inv_freq = ROPE_BASE ** (-jnp.arange(half, dtype=F32) / half)
    ang = pos[:, None] * inv_freq[None]
    log_g = jnp.log(1.0 - 2.0 ** (-5.0 - jnp.arange(RET_HEADS, dtype=F32)))
    n = jnp.arange(RET_CHUNK, dtype=F32)
    diff = n[:, None] - n[None, :]
    dmask = jnp.where(diff >= 0, jnp.exp(log_g[:, None, None] * jnp.maximum(diff, 0.0)), 0.0)
    qdec = jnp.exp(log_g[:, None] * (n + 1.0))
    kdec = jnp.exp(log_g[:, None] * (RET_CHUNK - 1.0 - n))
    cdec = jnp.exp(log_g * RET_CHUNK)
    wide = (RET_HEADS, RET_CHUNK, RET_QK)
    return dict(cos=jnp.cos(ang), sin=jnp.sin(ang), dmask=dmask,
                qdec=jnp.broadcast_to(qdec[:, :, None], wide), kdec=jnp.broadcast_to(kdec[:, :, None], wide),
                cdec=cdec)


def _rot(t, cs, sn):
    half = RET_QK // 2
    t1, t2 = t[:, :half], t[:, half:]
    return jnp.concatenate([t1 * cs - t2 * sn, t1 * sn + t2 * cs], axis=1)


def _rot_bwd(d, cs, sn):
    half = RET_QK // 2
    d1, d2 = d[:, :half], d[:, half:]
    return jnp.concatenate([d1 * cs + d2 * sn, d2 * cs - d1 * sn], axis=1)


def _ret_fwd(zq, zk, zv, zg, rc):
    s = zq.shape[0]
    nc = s // RET_CHUNK
    c = RET_CHUNK

    def body(cd_ref, q_ref, k_ref, v_ref, g_ref, cos_ref, sin_ref, dm_ref, qd_ref, kd_ref,
             yb_ref, o_ref, st_ref, state):
        n = pl.program_id(0)

        @pl.when(n == 0)
        def _():
            state[...] = jnp.zeros_like(state)

        cs, sn = cos_ref[...], sin_ref[...]
        for h in range(RET_HEADS):
            qs, vs = slice(h * RET_QK, (h + 1) * RET_QK), slice(h * RET_V, (h + 1) * RET_V)
            qr = _rot(q_ref[:, qs], cs, sn)
            kr = _rot(k_ref[:, qs], cs, sn) * (RET_QK ** -0.5)
            v = v_ref[:, vs]
            st = state[h]
            st_ref[h, 0] = st.astype(st_ref.dtype)
            sc = _dot(qr, kr, NT) * dm_ref[h]
            o = _dot(sc, v, NN) + _dot(qr * qd_ref[h], st, NN)
            state[h] = st * cd_ref[h] + _dot(kr * kd_ref[h], v, TN)
            o_ref[:, vs] = o
            xh, _ = _norm_rows(o, GN_EPS)
            gv = g_ref[:, vs]
            yb_ref[:, vs] = (gv * _sigmoid(gv) * xh).astype(yb_ref.dtype)

    row = lambda w: pl.BlockSpec((c, w), lambda n: (n, 0))
    const = lambda a: pl.BlockSpec(a.shape, lambda n: (0, 0, 0))
    return pl.pallas_call(
        body,
        out_shape=[jax.ShapeDtypeStruct((s, RET_HEADS * RET_V), MXU_DTYPE),
                   jax.ShapeDtypeStruct((s, RET_HEADS * RET_V), F32),
                   jax.ShapeDtypeStruct((RET_HEADS, nc, RET_QK, RET_V), MXU_DTYPE)],
        grid=(nc,),
        in_specs=[pl.BlockSpec(memory_space=pltpu.SMEM), row(RET_HEADS * RET_QK), row(RET_HEADS * RET_QK),
                  row(RET_HEADS * RET_V), row(RET_HEADS * RET_V), row(RET_QK // 2), row(RET_QK // 2),
                  const(rc["dmask"]), const(rc["qdec"]), const(rc["kdec"])],
        out_specs=[row(RET_HEADS * RET_V), row(RET_HEADS * RET_V),
                   pl.BlockSpec((RET_HEADS, 1, RET_QK, RET_V), lambda n: (0, n, 0, 0))],
        scratch_shapes=[pltpu.VMEM((RET_HEADS, RET_QK, RET_V), F32)],
        compiler_params=_params(1), name="ret_fwd",
    )(rc["cdec"], zq, zk, zv, zg, rc["cos"], rc["sin"], rc["dmask"], rc["qdec"], rc["kdec"])


def _ret_bwd(zq, zk, zv, zg, o_ret, states, dyb, rc):
    s = zq.shape[0]
    nc = s // RET_CHUNK
    c = RET_CHUNK

    def body(cd_ref, q_ref, k_ref, v_ref, g_ref, o_ref, st_ref, dy_ref, cos_ref, sin_ref, dm_ref, qd_ref,
             kd_ref, dq_ref, dk_ref, dv_ref, dg_ref, dstate):
        n = pl.program_id(0)

        @pl.when(n == 0)
        def _():
            dstate[...] = jnp.zeros_like(dstate)

        cs, sn = cos_ref[...], sin_ref[...]
        for h in range(RET_HEADS):
            qs, vs = slice(h * RET_QK, (h + 1) * RET_QK), slice(h * RET_V, (h + 1) * RET_V)
            qr = _rot(q_ref[:, qs], cs, sn)
            kr = _rot(k_ref[:, qs], cs, sn) * (RET_QK ** -0.5)
            v = v_ref[:, vs]
            st = st_ref[h, 0]
            dm, qd, kd = dm_ref[h], qd_ref[h], kd_ref[h]
            xh, rstd = _norm_rows(o_ref[:, vs], GN_EPS)
            gv, dy = g_ref[:, vs], dy_ref[:, vs]
            sg = _sigmoid(gv)
            dg_ref[:, vs] = dy * xh * (sg * (1.0 + gv * (1.0 - sg)))
            do = _norm_rows_bwd(dy * (gv * sg), xh, rstd)
            ds_next = dstate[h]
            sc = _dot(qr, kr, NT) * dm
            da = _dot(do, v, NT) * dm
            dv = _dot(sc, do, TN) + _dot(kr * kd, ds_next, NN)
            dqr = _dot(da, kr, NN) + _dot(do, st, NT) * qd
            dkr = _dot(da, qr, TN) + _dot(v, ds_next, NT) * kd
            dstate[h] = ds_next * cd_ref[h] + _dot(qr * qd, do, TN)
            dq_ref[:, qs] = _rot_bwd(dqr, cs, sn)
            dk_ref[:, qs] = _rot_bwd(dkr * (RET_QK ** -0.5), cs, sn)
            dv_ref[:, vs] = dv.astype(dv_ref.dtype)

    row = lambda w: pl.BlockSpec((c, w), lambda n: (nc - 1 - n, 0))
    const = lambda a: pl.BlockSpec(a.shape, lambda n: (0, 0, 0))
    f_qk = jax.ShapeDtypeStruct((s, RET_HEADS * RET_QK), F32)
    qk_w, v_w = RET_HEADS * RET_QK, RET_HEADS * RET_V
    return pl.pallas_call(
        body,
        out_shape=[f_qk, f_qk, jax.ShapeDtypeStruct((s, v_w), MXU_DTYPE), jax.ShapeDtypeStruct((s, v_w), F32)],
        grid=(nc,),
        in_specs=[pl.BlockSpec(memory_space=pltpu.SMEM), row(qk_w), row(qk_w), row(v_w), row(v_w), row(v_w),
                  pl.BlockSpec((RET_HEADS, 1, RET_QK, RET_V), lambda n: (0, nc - 1 - n, 0, 0)), row(v_w),
                  row(RET_QK // 2), row(RET_QK // 2), const(rc["dmask"]), const(rc["qdec"]), const(rc["kdec"])],
        out_specs=[row(qk_w), row(qk_w), row(v_w), row(v_w)],
        scratch_shapes=[pltpu.VMEM((RET_HEADS, RET_QK, RET_V), F32)],
        compiler_params=_params(1), name="ret_bwd",
    )(rc["cdec"], zq, zk, zv, zg, o_ret, states, dyb, rc["cos"], rc["sin"], rc["dmask"], rc["qdec"],
      rc["kdec"])


ROW_TILE = 256


def _mix_fwd(x, ya_b, yb_b, ze, wap_t, wrp, wout, gam, bet):
    s = x.shape[0]
    tm = ROW_TILE

    def body(x_ref, ya_ref, yb_ref, ze_ref, wap_ref, wrp_ref, wout_ref, g_ref, b_ref,
             x1_ref, u1_ref, a1_ref, a2_ref, mg_ref):
        a1 = _dot(ya_ref[...], wap_ref[...], NT)
        a2 = _dot(yb_ref[...], wrp_ref[...], NN)
        ze_ = ze_ref[...]
        merged = _sigmoid(ze_[:, :D_MODEL]) * a1 + _sigmoid(ze_[:, D_MODEL:]) * a2
        u = ALPHA * x_ref[...] + _dot(merged, wout_ref[...], NN)
        xh, _ = _norm_rows(u, LN_EPS)
        x1_ref[...] = xh * g_ref[...] + b_ref[...]
        u1_ref[...] = u
        a1_ref[...] = a1
        a2_ref[...] = a2
        mg_ref[...] = merged.astype(mg_ref.dtype)

    row = lambda w: pl.BlockSpec((tm, w), lambda i: (i, 0))
    full = _resident
    f = jax.ShapeDtypeStruct((s, D_MODEL), F32)
    return pl.pallas_call(
        body, out_shape=[f, f, f, f, jax.ShapeDtypeStruct((s, D_MODEL), MXU_DTYPE)], grid=(s // tm,),
        in_specs=[row(D_MODEL), row(ATTN_OUT), row(RET_HEADS * RET_V), row(2 * D_MODEL), full(wap_t), full(wrp),
                  full(wout), full(gam), full(bet)],
        out_specs=[row(D_MODEL)] * 5, compiler_params=_params(1), name="mix_fwd",
    )(x, ya_b, yb_b, ze, wap_t, wrp, wout, gam, bet)


def _mix_bwd(dx1, u1, a1, a2, ze, wap_t, wrp, wout, gam):
    s = dx1.shape[0]
    tm = ROW_TILE

    def body(dx_ref, u_ref, a1_ref, a2_ref, ze_ref, wap_ref, wrp_ref, wout_ref, g_ref,
             dres_ref, du_ref, da1_ref, da2_ref, dze_ref, dya_ref, dyb_ref, dgam_ref, dbet_ref):
        @pl.when(pl.program_id(0) == 0)
        def _():
            dgam_ref[...] = jnp.zeros_like(dgam_ref)
            dbet_ref[...] = jnp.zeros_like(dbet_ref)

        dx = dx_ref[...]
        xh, rstd = _norm_rows(u_ref[...], LN_EPS)
        dgam_ref[...] += jnp.sum(dx * xh, axis=0, keepdims=True)
        dbet_ref[...] += jnp.sum(dx, axis=0, keepdims=True)
        du = _norm_rows_bwd(dx * g_ref[...], xh, rstd)
        dres_ref[...] = ALPHA * du
        du_ref[...] = du.astype(du_ref.dtype)
        dm = _dot(du, wout_ref[...], NT)
        ze_ = ze_ref[...]
        sa, sb = _sigmoid(ze_[:, :D_MODEL]), _sigmoid(ze_[:, D_MODEL:])
        da1, da2 = dm * sa, dm * sb
        dze_ref[...] = jnp.concatenate([dm * a1_ref[...] * (sa * (1.0 - sa)),
                                        dm * a2_ref[...] * (sb * (1.0 - sb))], axis=1)
        da1_ref[...] = da1.astype(da1_ref.dtype)
        da2_ref[...] = da2.astype(da2_ref.dtype)
        dya_ref[...] = _dot(da1, wap_ref[...], NN)
        dyb_ref[...] = _dot(da2, wrp_ref[...], NT)

    row = lambda w: pl.BlockSpec((tm, w), lambda i: (i, 0))
    full = _resident
    vec = pl.BlockSpec((1, D_MODEL), lambda i: (0, 0))
    f = lambda w: jax.ShapeDtypeStruct((s, w), F32)
    m = lambda w: jax.ShapeDtypeStruct((s, w), MXU_DTYPE)
    v = jax.ShapeDtypeStruct((1, D_MODEL), F32)
    return pl.pallas_call(
        body,
        out_shape=[f(D_MODEL), m(D_MODEL), m(D_MODEL), m(D_MODEL), f(2 * D_MODEL), f(ATTN_OUT),
                   f(RET_HEADS * RET_V), v, v],
        grid=(s // tm,),
        in_specs=[row(D_MODEL)] * 4 + [row(2 * D_MODEL), full(wap_t), full(wrp), full(wout), full(gam)],
        out_specs=[row(D_MODEL)] * 4 + [row(2 * D_MODEL), row(ATTN_OUT), row(RET_HEADS * RET_V), vec, vec],
        compiler_params=_params(1), name="mix_bwd",
    )(dx1, u1, a1, a2, ze, wap_t, wrp, wout, gam)


FF_CHUNK = 1408


def _ffn_fwd(x1, wg_t, wu_t, wd, gam, bet):
    s = x1.shape[0]
    tm, fc = ROW_TILE, FF_CHUNK

    def body(x_ref, wg_ref, wu_ref, wd_ref, g_ref, b_ref, x2_ref, u2_ref, a_ref, b_out_ref, h_ref):
        xv = x_ref[...]
        u = ALPHA * xv
        for f0 in range(0, D_FF, fc):
            ch = slice(f0, f0 + fc)
            a = _dot(xv, wg_ref[ch, :], NT)
            b = _dot(xv, wu_ref[ch, :], NT)
            hid = a * _sigmoid(a) * b
            u = u + _dot(hid, wd_ref[ch, :], NN)
            a_ref[:, ch] = a
            b_out_ref[:, ch] = b
            h_ref[:, ch] = hid.astype(h_ref.dtype)
        xh, _ = _norm_rows(u, LN_EPS)
        u2_ref[...] = u
        x2_ref[...] = xh * g_ref[...] + b_ref[...]

    row = lambda w: pl.BlockSpec((tm, w), lambda i: (i, 0))
    f = lambda w: jax.ShapeDtypeStruct((s, w), F32)
    return pl.pallas_call(
        body, out_shape=[f(D_MODEL), f(D_MODEL), f(D_FF), f(D_FF), jax.ShapeDtypeStruct((s, D_FF), MXU_DTYPE)],
        grid=(s // tm,),
        in_specs=[row(D_MODEL), _resident(wg_t), _resident(wu_t), _resident(wd), _resident(gam), _resident(bet)],
        out_specs=[row(D_MODEL), row(D_MODEL), row(D_FF), row(D_FF), row(D_FF)],
        compiler_params=_params(1), name="ffn_fwd",
    )(x1, wg_t, wu_t, wd, gam, bet)


def _ffn_bwd(dx2, u2, fa, fb, wg_t, wu_t, wd, gam):
    s = dx2.shape[0]
    tm, fc = ROW_TILE, FF_CHUNK

    def body(dx_ref, u_ref, a_ref, b_ref, wg_ref, wu_ref, wd_ref, g_ref,
             dx1_ref, du_ref, da_ref, db_ref, dgam_ref, dbet_ref):
        @pl.when(pl.program_id(0) == 0)
        def _():
            dgam_ref[...] = jnp.zeros_like(dgam_ref)
            dbet_ref[...] = jnp.zeros_like(dbet_ref)

        dx = dx_ref[...]
        xh, rstd = _norm_rows(u_ref[...], LN_EPS)
        dgam_ref[...] += jnp.sum(dx * xh, axis=0, keepdims=True)
        dbet_ref[...] += jnp.sum(dx, axis=0, keepdims=True)
        du = _norm_rows_bwd(dx * g_ref[...], xh, rstd)
        du_ref[...] = du.astype(du_ref.dtype)
        acc = ALPHA * du
        for f0 in range(0, D_FF, fc):
            ch = slice(f0, f0 + fc)
            dh = _dot(du, wd_ref[ch, :], NT)
            a, b = a_ref[:, ch], b_ref[:, ch]
            sg = _sigmoid(a)
            da = dh * b * (sg * (1.0 + a * (1.0 - sg)))
            db = dh * (a * sg)
            acc = acc + _dot(da, wg_ref[ch, :], NN) + _dot(db, wu_ref[ch, :], NN)
            da_ref[:, ch] = da.astype(da_ref.dtype)
            db_ref[:, ch] = db.astype(db_ref.dtype)
        dx1_ref[...] = acc

    row = lambda w: pl.BlockSpec((tm, w), lambda i: (i, 0))
    vec = pl.BlockSpec((1, D_MODEL), lambda i: (0, 0))
    v = jax.ShapeDtypeStruct((1, D_MODEL), F32)
    return pl.pallas_call(
        body,
        out_shape=[jax.ShapeDtypeStruct((s, D_MODEL), F32), jax.ShapeDtypeStruct((s, D_MODEL), MXU_DTYPE),
                   jax.ShapeDtypeStruct((s, D_FF), MXU_DTYPE), jax.ShapeDtypeStruct((s, D_FF), MXU_DTYPE), v, v],
        grid=(s // tm,),
        in_specs=[row(D_MODEL), row(D_MODEL), row(D_FF), row(D_FF), _resident(wg_t), _resident(wu_t),
                  _resident(wd), _resident(gam)],
        out_specs=[row(D_MODEL), row(D_MODEL), row(D_FF), row(D_FF), vec, vec],
        compiler_params=_params(1), name="ffn_bwd",
    )(dx2, u2, fa, fb, wg_t, wu_t, wd, gam)


def _loss_head(y, target):
    s = y.shape[0]
    tr = _pick(s, (512, 256, 128))

    def body(y_ref, t_ref, dy_ref, l_ref):
        @pl.when(pl.program_id(0) == 0)
        def _():
            l_ref[...] = jnp.zeros_like(l_ref)

        e = y_ref[...] - t_ref[...]
        dy_ref[...] = e * (1.0 / D_MODEL)
        part = jnp.sum(jnp.sum(e * e, axis=1, keepdims=True), axis=0, keepdims=True)
        l_ref[...] += part * (0.5 / D_MODEL)

    spec = pl.BlockSpec((tr, D_MODEL), lambda i: (i, 0))
    dy, part = pl.pallas_call(
        body, out_shape=[jax.ShapeDtypeStruct((s, D_MODEL), F32), jax.ShapeDtypeStruct((8, BLK), F32)],
        grid=(s // tr,), in_specs=[spec, spec], out_specs=[spec, pl.BlockSpec((8, BLK), lambda i: (0, 0))],
        compiler_params=_params(1), name="loss_head",
    )(y, target)
    return dy, part[0, 0]


def _adamw(w, g, m, v, name):
    rows, cols = w.shape
    budget = 1 << 20
    cands = [t for t in range(8, rows + 1, 8) if rows % t == 0 and t * cols * 4 <= budget]
    tr = max(cands) if cands else rows

    def body(w_ref, g_ref, m_ref, v_ref, d_ref, nm_ref, nv_ref):
        gv = g_ref[...]
        mn = ADAM_B1 * m_ref[...] + (1.0 - ADAM_B1) * gv
        vn = ADAM_B2 * v_ref[...] + (1.0 - ADAM_B2) * (gv * gv)
        m_hat = mn / (1.0 - ADAM_B1 ** ADAM_STEP)
        v_hat = vn / (1.0 - ADAM_B2 ** ADAM_STEP)
        d_ref[...] = -ADAM_LR * (m_hat / (jnp.sqrt(v_hat) + ADAM_EPS) + ADAM_WD * w_ref[...])
        nm_ref[...] = mn
        nv_ref[...] = vn

    spec = pl.BlockSpec((tr, cols), lambda i: (i, 0))
    shape = jax.ShapeDtypeStruct((rows, cols), F32)
    return pl.pallas_call(
        body, out_shape=[shape, shape, shape], grid=(rows // tr,), in_specs=[spec] * 4, out_specs=[spec] * 3,
        compiler_params=_params(1), name=name,
    )(w, g, m, v)


MESH_ID = pl.DeviceIdType.MESH
ANY = pl.BlockSpec(memory_space=pl.ANY)


def _place():
    x, y, c = lax.axis_index("x"), lax.axis_index("y"), lax.axis_index("c")
    other_chips = [(1 - x, y), (x, 1 - y), (1 - x, 1 - y)]
    return x, y, c, other_chips


def _chip_no(chip):
    return 2 * chip[0] + chip[1]


def _all_gather_rows(flat):
    r, cols = flat.shape
    hr = r // 2

    def body(f_ref, g_ref, send_sems, recv_sems):
        x, y, c, chips = _place()
        sibling = (x, y, 1 - c)

        def piece(chip, half):
            return g_ref.at[_chip_no(chip), pl.ds(half * hr, hr), :]

        def copy(k, src, dst, to):
            return pltpu.make_async_remote_copy(src_ref=src, dst_ref=dst, send_sem=send_sems.at[k],
                                                recv_sem=recv_sems.at[k], device_id=to, device_id_type=MESH_ID)

        own = copy(6, f_ref, g_ref.at[_chip_no((x, y))], sibling)
        own.start()
        my_half = f_ref.at[pl.ds(c * hr, hr), :]
        first = [copy(k, my_half, piece((x, y), c), (*chip, c)) for k, chip in enumerate(chips)]
        for cp in first:
            cp.start()
        passed = [copy(3 + k, piece(chip, c), piece(chip, c), sibling) for k, chip in enumerate(chips)]
        for k, chip in enumerate(chips):
            copy(k, my_half, piece(chip, c), (*chip, c)).wait_recv()
            passed[k].start()
        for k, chip in enumerate(chips):
            copy(3 + k, my_half, piece(chip, 1 - c), sibling).wait_recv()
        for cp in first + passed:
            cp.wait_send()
        own.wait()

    return pl.pallas_call(
        body, out_shape=jax.ShapeDtypeStruct((N_CHIPS, r, cols), flat.dtype), in_specs=[ANY], out_specs=ANY,
        scratch_shapes=[pltpu.SemaphoreType.DMA((7,)), pltpu.SemaphoreType.DMA((7,))],
        name="all_gather_weights",
    )(flat)


def _swap_halves(gall):
    _, r, cols = gall.shape
    hr = r // 2

    def body(g_ref, a_ref, send_sem, recv_sem):
        x, y, c, _ = _place()
        cp = pltpu.make_async_remote_copy(src_ref=g_ref.at[:, pl.ds((1 - c) * hr, hr), :], dst_ref=a_ref,
                                          send_sem=send_sem, recv_sem=recv_sem, device_id=(x, y, 1 - c),
                                          device_id_type=MESH_ID)
        cp.start()
        cp.wait()

    return pl.pallas_call(
        body, out_shape=jax.ShapeDtypeStruct((N_CHIPS, hr, cols), gall.dtype), in_specs=[ANY], out_specs=ANY,
        scratch_shapes=[pltpu.SemaphoreType.DMA, pltpu.SemaphoreType.DMA], name="grad_swap_halves",
    )(gall)


def _scatter_to_chips(p):
    _, hr, cols = p.shape

    def body(p_ref, b_ref, send_sems, recv_sems):
        x, y, c, chips = _place()
        cps = [pltpu.make_async_remote_copy(src_ref=p_ref.at[_chip_no(chip)], dst_ref=b_ref.at[k],
                                            send_sem=send_sems.at[k], recv_sem=recv_sems.at[k],
                                            device_id=(*chip, c), device_id_type=MESH_ID)
               for k, chip in enumerate(chips)]
        for cp in cps:
            cp.start()
        for cp in cps:
            cp.wait()

    return pl.pallas_call(
        body, out_shape=jax.ShapeDtypeStruct((3, hr, cols), p.dtype), in_specs=[ANY], out_specs=ANY,
        scratch_shapes=[pltpu.SemaphoreType.DMA((3,)), pltpu.SemaphoreType.DMA((3,))], name="grad_scatter_chips",
    )(p)


def _share_with_sibling(full):
    r, cols = full.shape
    hr = r // 2

    def body(in_ref, out_ref, send_sem, recv_sem):
        x, y, c, _ = _place()
        cp = pltpu.make_async_remote_copy(src_ref=in_ref.at[pl.ds(c * hr, hr), :],
                                          dst_ref=out_ref.at[pl.ds(c * hr, hr), :],
                                          send_sem=send_sem, recv_sem=recv_sem, device_id=(x, y, 1 - c),
                                          device_id_type=MESH_ID)
        cp.start()
        cp.wait()

    return pl.pallas_call(
        body, out_shape=jax.ShapeDtypeStruct((r, cols), full.dtype), in_specs=[ANY], out_specs=ANY,
        input_output_aliases={0: 0}, scratch_shapes=[pltpu.SemaphoreType.DMA, pltpu.SemaphoreType.DMA],
        name="grad_share_sibling",
    )(full)


def _all_reduce_small(v):
    r, cols = v.shape
    n_dev = 8

    def body(x_ref, out_ref, gat_ref, send_sems, recv_sems, local_sem):
        x, y, c, chips = _place()
        me, sibling = (x, y, c), (x, y, 1 - c)

        def slot(px, py, pc):
            return gat_ref.at[4 * px + 2 * py + pc]

        def copy(k, block, to, src=None):
            return pltpu.make_async_remote_copy(src_ref=slot(*block) if src is None else src, dst_ref=slot(*block),
                                                send_sem=send_sems.at[k], recv_sem=recv_sems.at[k], device_id=to,
                                                device_id_type=MESH_ID)

        mine = pltpu.make_async_copy(x_ref, slot(*me), local_sem)
        mine.start()
        first = [copy(0, me, sibling, src=x_ref)]
        first += [copy(1 + k, me, (*chip, c), src=x_ref) for k, chip in enumerate(chips)]
        for cp in first:
            cp.start()
        passed = [copy(4 + k, (*chip, c), sibling) for k, chip in enumerate(chips)]
        for k, chip in enumerate(chips):
            copy(1 + k, (*chip, c), me).wait_recv()
            passed[k].start()
        copy(0, sibling, me).wait_recv()
        for k, chip in enumerate(chips):
            copy(4 + k, (*chip, 1 - c), me).wait_recv()
        for cp in first + passed:
            cp.wait_send()
        mine.wait()
        acc = gat_ref[0]
        for d in range(1, n_dev):
            acc = acc + gat_ref[d]
        out_ref[...] = acc

    vmem = pl.BlockSpec(memory_space=pltpu.VMEM)
    return pl.pallas_call(
        body, out_shape=jax.ShapeDtypeStruct((r, cols), v.dtype), in_specs=[vmem], out_specs=vmem,
        scratch_shapes=[pltpu.VMEM((n_dev, r, cols), v.dtype), pltpu.SemaphoreType.DMA((7,)),
                        pltpu.SemaphoreType.DMA((7,)), pltpu.SemaphoreType.DMA],
        name="all_reduce_small",
    )(v)


WIRE_DTYPE = jnp.bfloat16


def _add_own_half(gall, got, place):
    _, r, cols = gall.shape
    hr = r // 2
    tr = _pick(hr, (256, 128, 64, 32, 16, 8))
    g4 = gall.reshape(N_CHIPS, 2, hr, cols)

    def body(s_ref, g_ref, a_ref, o_ref):
        o_ref[...] = (g_ref[0] + a_ref[...]).astype(o_ref.dtype)

    return pl.pallas_call(
        body, out_shape=jax.ShapeDtypeStruct((N_CHIPS, hr, cols), WIRE_DTYPE),
        grid_spec=pltpu.PrefetchScalarGridSpec(
            num_scalar_prefetch=1, grid=(N_CHIPS, hr // tr),
            in_specs=[pl.BlockSpec((1, 1, tr, cols), lambda j, i, s: (j, s[1], i, 0)),
                      pl.BlockSpec((1, tr, cols), lambda j, i, s: (j, i, 0))],
            out_specs=pl.BlockSpec((1, tr, cols), lambda j, i, s: (j, i, 0))),
        compiler_params=_params(2), name="grad_add_halves",
    )(place, g4, got)


def _add_chip_parts(gall, got, parts, place):
    _, r, cols = gall.shape
    hr = r // 2
    tr = _pick(hr, (256, 128, 64, 32, 16, 8))
    g4 = gall.reshape(N_CHIPS, 2, hr, cols)
    nt = hr // tr

    def body(s_ref, g_ref, a_ref, b0_ref, b1_ref, b2_ref, o_ref):
        own = g_ref[0, 0] + a_ref[0]
        o_ref[...] = ((own + b0_ref[0].astype(F32)) + b1_ref[0].astype(F32)) + b2_ref[0].astype(F32)

    part = lambda k: pl.BlockSpec((1, tr, cols), lambda i, s: (k, i, 0))
    return pl.pallas_call(
        body, out_shape=jax.ShapeDtypeStruct((r, cols), F32),
        grid_spec=pltpu.PrefetchScalarGridSpec(
            num_scalar_prefetch=1, grid=(nt,),
            in_specs=[pl.BlockSpec((1, 1, tr, cols), lambda i, s: (s[0], s[1], i, 0)),
                      pl.BlockSpec((1, tr, cols), lambda i, s: (s[0], i, 0)), part(0), part(1), part(2)],
            out_specs=pl.BlockSpec((tr, cols), lambda i, s: (s[1] * nt + i, 0))),
        compiler_params=_params(1), name="grad_add_chips",
    )(place, g4, got, parts, parts, parts)


def _pack_shards(w_in, w_ap, w_rp, w_out, w_g, w_u, w_d, dtype):
    parts = []
    for l in range(DEPTH):
        parts += [w_in[l].T, w_ap[l].T.reshape(-1, D_MODEL), w_rp[l], w_out[l], w_g[l].T, w_u[l].T, w_d[l]]
    return jnp.concatenate([p.astype(dtype) for p in parts], axis=0)


def _unpack_gathered(gathered):
    layers = []
    off = 0
    for _ in range(DEPTH):
        w = {}
        for name, rows in PACK_ROWS:
            blk = gathered[:, off:off + rows]
            if name == "w_attn_proj":
                w[name] = blk.reshape(N_CHIPS * 256, ATTN_OUT)
            else:
                w[name] = blk.reshape(N_CHIPS * rows, D_MODEL)
            off += rows
        layers.append(w)
    return layers


def _pack_full_grads(grads):
    parts = []
    for g in grads:
        for name, rows in PACK_ROWS:
            parts.append(g[name].reshape(N_CHIPS, rows, D_MODEL))
    return jnp.concatenate(parts, axis=1)


def _unpack_shard_grads(full):
    out = {name: [] for name, _ in PACK_ROWS}
    off = 0
    for _ in range(DEPTH):
        for name, rows in PACK_ROWS:
            blk = full[off:off + rows]
            if name == "w_attn_proj":
                blk = blk.reshape(256, ATTN_OUT).T
            elif name in ("w_in", "w_ffn_gate", "w_ffn_up"):
                blk = blk.T
            out[name].append(blk)
            off += rows
    return {k: jnp.stack(v) for k, v in out.items()}


def _attn_views(za, s):
    views = []
    for g, (_, dil) in enumerate(ATTN_GROUPS):
        if dil == 1:
            views.append((za, za, za, 3 * N_GROUPS, (g, N_GROUPS + g, 2 * N_GROUPS + g)))
        else:
            rows = s // dil
            part = lambda k: za[:, k * ATTN_WIDTH + g * ATTN_OUT:k * ATTN_WIDTH + (g + 1) * ATTN_OUT].reshape(
                rows, dil * ATTN_OUT)
            views.append((part(0), part(1), part(2), 1, (0, 0, 0)))
    return views


def _layer_fwd(x, w, lnp, bias, rc, tag):
    s = x.shape[0]
    z = {}
    for name, off, width, narrow in SECTIONS:
        z[name] = _matmul(x, w["w_in"][off:off + width], "nt", bias=lnp["b_in"][:, off:off + width],
                          out_dtype=MXU_DTYPE if narrow else F32, name=f"in_proj_{name}_{tag}")
    views = _attn_views(z["a"], s)
    o_list, l_list = [], []
    for g, (_, dil) in enumerate(ATTN_GROUPS):
        qa, ka, va, cb, offs = views[g]
        o, lse = _attn_fwd(qa, ka, va, cb, offs, bias, g, dil)
        o_list.append(o.reshape(s, ATTN_OUT))
        l_list.append(lse.reshape(s, ATTN_OUT))
    ya, ya_b, lt = _attn_combine(o_list, l_list)
    yb_b, o_ret, states = _ret_fwd(z["bq"], z["bk"], z["c"], z["d"], rc)
    x1, u1, a1, a2, mg_b = _mix_fwd(x, ya_b, yb_b, z["e"], w["w_attn_proj"], w["w_ret_proj"], w["w_out"],
                                    lnp["ln1_g"], lnp["ln1_b"])
    x2, u2, fa, fb, h_b = _ffn_fwd(x1, w["w_ffn_gate"], w["w_ffn_up"], w["w_ffn_down"], lnp["ln2_g"], lnp["ln2_b"])
    saved = dict(x=x, z=z, views=views, ya=ya, ya_b=ya_b, lt=lt, yb_b=yb_b, o_ret=o_ret, states=states,
                 x1=x1, u1=u1, a1=a1, a2=a2, mg_b=mg_b, u2=u2, fa=fa, fb=fb, h_b=h_b)
    return x2, saved


def _layer_bwd(dx2, w, lnp, sv, bias, rc, tag):
    s = dx2.shape[0]
    z = sv["z"]
    dx1, du2_b, da_b, db_b, dg2, dbt2 = _ffn_bwd(dx2, sv["u2"], sv["fa"], sv["fb"], w["w_ffn_gate"], w["w_ffn_up"],
                                                  w["w_ffn_down"], lnp["ln2_g"])
    gw = {}
    gw["w_ffn_down"] = _matmul(sv["h_b"], du2_b, "tn", name=f"dw_ffn_down_{tag}")
    gw["w_ffn_gate"] = _matmul(da_b, sv["x1"], "tn", name=f"dw_ffn_gate_{tag}")
    gw["w_ffn_up"] = _matmul(db_b, sv["x1"], "tn", name=f"dw_ffn_up_{tag}")
    dres, du1_b, da1_b, da2_b, dze, dya, dyb, dg1, dbt1 = _mix_bwd(
        dx1, sv["u1"], sv["a1"], sv["a2"], z["e"], w["w_attn_proj"], w["w_ret_proj"], w["w_out"], lnp["ln1_g"])
    gw["w_out"] = _matmul(sv["mg_b"], du1_b, "tn", name=f"dw_out_{tag}")
    gw["w_attn_proj"] = _matmul(da1_b, sv["ya_b"], "tn", name=f"dw_attn_proj_{tag}")
    gw["w_ret_proj"] = _matmul(sv["yb_b"], da2_b, "tn", name=f"dw_ret_proj_{tag}")
    dzq, dzk, dzv, dzg = _ret_bwd(z["bq"], z["bk"], z["c"], z["d"], sv["o_ret"], sv["states"], dyb, rc)
    dq_l, dk_l, dv_l, dbias_l = [], [], [], []
    for g, (_, dil) in enumerate(ATTN_GROUPS):
        qa, ka, va, cb, offs = sv["views"][g]
        rows = s // dil
        view = lambda t: t.reshape(rows, dil * ATTN_OUT)
        dq, dk, dv, dbg = _attn_bwd(qa, ka, va, cb, offs, bias, view(dya), view(sv["ya"]), view(sv["lt"]), g, dil)
        dq_l.append(dq.reshape(s, ATTN_OUT))
        dk_l.append(dk.reshape(s, ATTN_OUT))
        dv_l.append(dv.reshape(s, ATTN_OUT))
        dbias_l.append(dbg)
    dz = {"a": jnp.concatenate(dq_l + dk_l + dv_l, axis=1), "bq": dzq, "bk": dzk, "c": dzv, "d": dzg, "e": dze}
    dx = dres
    dw_rows, db_cols = [], []
    for name, off, width, _ in SECTIONS:
        dx = _matmul(dz[name], w["w_in"][off:off + width], "nn", addend=dx, name=f"dx_in_proj_{name}_{tag}")
        dw, cs = _matmul(dz[name], sv["x"], "tn", colsum=True, name=f"dw_in_proj_{name}_{tag}")
        dw_rows.append(dw)
        db_cols.append(cs)
    gw["w_in"] = jnp.concatenate(dw_rows, axis=0)
    small = dict(b_in=jnp.concatenate(db_cols, axis=1), ln1_g=dg1, ln1_b=dbt1, ln2_g=dg2, ln2_b=dbt2)
    return dx, gw, small, jnp.concatenate(dbias_l, axis=0)


def _forward_backward(x, target, rel_bias, layers, lnps):
    s = x.shape[0]
    bmaps = _bucket_maps()
    bias = _bias_tiles(rel_bias, bmaps)
    rc = _ret_consts(s)
    saved = []
    h = x
    for l in range(DEPTH):
        h, sv = _layer_fwd(h, layers[l], lnps[l], bias, rc, f"l{l}")
        saved.append(sv)
    dh, loss_part = _loss_head(h, target)
    gws, smalls, dbiases = [None] * DEPTH, [None] * DEPTH, [None] * DEPTH
    for l in reversed(range(DEPTH)):
        dh, gws[l], smalls[l], dbiases[l] = _layer_bwd(dh, layers[l], lnps[l], saved[l], bias, rc, f"l{l}")
    d_rel_bias = _bias_tiles_bwd(dbiases, bmaps)
    return loss_part, dh, gws, smalls, d_rel_bias


SMALL_NAMES = ("rel_bias", "b_in", "ln1_g", "ln1_b", "ln2_g", "ln2_b")
SMALL_ROWS = 32


def _pack_small(vals):
    flat = jnp.concatenate([vals[n].reshape(-1) for n in SMALL_NAMES])
    return jnp.pad(flat, (0, SMALL_ROWS * D_MODEL - flat.shape[0])).reshape(SMALL_ROWS, D_MODEL)


def _unpack_small(packed, like):
    flat = packed.reshape(-1)
    out, off = {}, 0
    for n in SMALL_NAMES:
        size = like[n].size
        out[n] = flat[off:off + size].reshape(like[n].shape)
        off += size
    return out


def kernel(x, rel_bias, w_in, b_in, w_attn_proj, w_ret_proj, w_out, ln1_g, ln1_b, w_ffn_gate, w_ffn_up, w_ffn_down, ln2_g, ln2_b, loss_target, m_rel_bias, m_w_in, m_b_in, m_w_attn_proj, m_w_ret_proj, m_w_out, m_ln1_g, m_ln1_b, m_w_ffn_gate, m_w_ffn_up, m_w_ffn_down, m_ln2_g, m_ln2_b, v_rel_bias, v_w_in, v_b_in, v_w_attn_proj, v_w_ret_proj, v_w_out, v_ln1_g, v_ln1_b, v_w_ffn_gate, v_w_ffn_up, v_w_ffn_down, v_ln2_g, v_ln2_b):
    big = dict(w_in=(w_in, m_w_in, v_w_in), w_attn_proj=(w_attn_proj, m_w_attn_proj, v_w_attn_proj),
               w_ret_proj=(w_ret_proj, m_w_ret_proj, v_w_ret_proj), w_out=(w_out, m_w_out, v_w_out),
               w_ffn_gate=(w_ffn_gate, m_w_ffn_gate, v_w_ffn_gate), w_ffn_up=(w_ffn_up, m_w_ffn_up, v_w_ffn_up),
               w_ffn_down=(w_ffn_down, m_w_ffn_down, v_w_ffn_down))
    small_w = dict(rel_bias=rel_bias, b_in=b_in, ln1_g=ln1_g, ln1_b=ln1_b, ln2_g=ln2_g, ln2_b=ln2_b)
    small_m = dict(rel_bias=m_rel_bias, b_in=m_b_in, ln1_g=m_ln1_g, ln1_b=m_ln1_b, ln2_g=m_ln2_g, ln2_b=m_ln2_b)
    small_v = dict(rel_bias=v_rel_bias, b_in=v_b_in, ln1_g=v_ln1_g, ln1_b=v_ln1_b, ln2_g=v_ln2_g, ln2_b=v_ln2_b)

    place = jnp.stack([2 * lax.axis_index("x") + lax.axis_index("y"), lax.axis_index("c")]).astype(jnp.int32)

    flat = _pack_shards(w_in, w_attn_proj, w_ret_proj, w_out, w_ffn_gate, w_ffn_up, w_ffn_down, MXU_DTYPE)
    layers = _unpack_gathered(_all_gather_rows(flat))
    lnps = [dict(b_in=b_in[l][None], ln1_g=ln1_g[l][None], ln1_b=ln1_b[l][None], ln2_g=ln2_g[l][None],
                 ln2_b=ln2_b[l][None]) for l in range(DEPTH)]

    loss_part, dx, gws, smalls, d_rel_bias = _forward_backward(x[0], loss_target[0], rel_bias, layers, lnps)
    loss = lax.psum(loss_part, ("x", "y", "c"))

    gall = _pack_full_grads(gws)
    got = _swap_halves(gall)
    parts = _scatter_to_chips(_add_own_half(gall, got, place))
    grads = _unpack_shard_grads(_share_with_sibling(_add_chip_parts(gall, got, parts, place)))

    small_g = dict(rel_bias=d_rel_bias)
    for n in SMALL_NAMES[1:]:
        small_g[n] = jnp.concatenate([smalls[l][n] for l in range(DEPTH)], axis=0)
    small_g = _unpack_small(_all_reduce_small(_pack_small(small_g)), small_w)

    delta, new_m, new_v = {}, {}, {}
    for n, (w, m, v) in big.items():
        two_d = lambda t: t.reshape(-1, t.shape[-1])
        d_, m_, v_ = _adamw(two_d(w), two_d(grads[n]), two_d(m), two_d(v), name=f"adamw_{n}")
        delta[n], new_m[n], new_v[n] = d_.reshape(w.shape), m_.reshape(w.shape), v_.reshape(w.shape)
    d_, m_, v_ = _adamw(_pack_small(small_w), _pack_small(small_g), _pack_small(small_m), _pack_small(small_v),
                        name="adamw_small")
    delta.update(_unpack_small(d_, small_w))
    new_m.update(_unpack_small(m_, small_w))
    new_v.update(_unpack_small(v_, small_w))
    grads.update(small_g)

    order = ("rel_bias", "w_in", "b_in", "w_attn_proj", "w_ret_proj", "w_out", "ln1_g", "ln1_b", "w_ffn_gate",
             "w_ffn_up", "w_ffn_down", "ln2_g", "ln2_b")
    return (loss, dx[None], *[grads[n] for n in order], *[delta[n] for n in order], *[new_m[n] for n in order],
            *[new_v[n] for n in order])
```

```python
import functools

import numpy as np
import jax
import jax.numpy as jnp
from jax import lax
from jax.experimental import pallas as pl
from jax.experimental.pallas import tpu as pltpu
from jax.experimental.pallas import tpu_sc as plsc

F32 = jnp.float32
MXU_DTYPE = jnp.bfloat16

DEPTH = 2
D_MODEL = 1024
HEAD_DIM = 64
ATTN_GROUPS = ((128, 1), (512, 4), (2048, 16))
N_GROUPS = len(ATTN_GROUPS)
HEADS_PER_GROUP = 6
N_ATTN_HEADS = 18
ATTN_WIDTH = 1152
ATTN_OUT = 384
NUM_BUCKETS = 32
MAX_DISTANCE = 2048
RET_HEADS = 4
RET_QK = 256
RET_V = 512
RET_CHUNK = 128
ROPE_BASE = 10000.0
D_FF = 2816
IN_COLS = 11648
ALPHA = (2 * DEPTH) ** 0.25
LN_EPS = 1e-5
GN_EPS = 1e-5
ADAM_LR, ADAM_B1, ADAM_B2, ADAM_EPS, ADAM_WD, ADAM_STEP = 0.001, 0.9, 0.999, 1e-08, 0.01, 10

BLK = 128
NEG = -1e30
N_CHIPS = 4
VMEM_LIMIT = 48 * 1024 * 1024

SECTIONS = (
    ("a", 0, 3456, True),
    ("bq", 3456, 1024, False),
    ("bk", 4480, 1024, False),
    ("c", 5504, 2048, True),
    ("d", 7552, 2048, False),
    ("e", 9600, 2048, False),
)
PACK_ROWS = (("w_in", 2912), ("w_attn_proj", 96), ("w_ret_proj", 512), ("w_out", 256),
             ("w_ffn_gate", 704), ("w_ffn_up", 704), ("w_ffn_down", 704))
ROWS_PER_LAYER = sum(r for _, r in PACK_ROWS)

NN = ((1,), (0,))
NT = ((1,), (1,))
TN = ((0,), (0,))


def _dot(a, b, dims):
    return lax.dot_general(a.astype(MXU_DTYPE), b.astype(MXU_DTYPE), (dims, ((), ())),
                           preferred_element_type=F32)


def _pick(n, prefs):
    for p in prefs:
        if n % p == 0:
            return p
    raise ValueError(f"no tile for {n} among {prefs}")


TOKEN_TILES = (1024, 512, 256, 128)
FEATURE_TILES = (1152, 1024, 1408, 384, 256, 128)


def _params(n_axes, limit=VMEM_LIMIT):
    return pltpu.CompilerParams(dimension_semantics=("arbitrary",) * n_axes, vmem_limit_bytes=limit)


def _resident(a):
    return pl.BlockSpec(a.shape, lambda i: (0,) * a.ndim, pipeline_mode=pl.Buffered(1))


def _sigmoid(x):
    return 1.0 / (1.0 + jnp.exp(-x))


def _norm_rows(u, eps):
    mu = jnp.mean(u, axis=-1, keepdims=True)
    xc = u - mu
    var = jnp.mean(xc * xc, axis=-1, keepdims=True)
    rstd = lax.rsqrt(var + eps)
    return xc * rstd, rstd


def _norm_rows_bwd(dxh, xh, rstd):
    c1 = jnp.mean(dxh, axis=-1, keepdims=True)
    c2 = jnp.mean(dxh * xh, axis=-1, keepdims=True)
    return rstd * (dxh - c1 - xh * c2)


def _matmul(a, b, mode, *, name, out_dtype=F32, bias=None, addend=None, colsum=False):
    if mode == "tn":
        kd, m = a.shape
        n = b.shape[1]
        tm, tn_, tk = _pick(m, FEATURE_TILES), _pick(n, FEATURE_TILES), _pick(kd, TOKEN_TILES)
        a_spec = pl.BlockSpec((tk, tm), lambda i, j, k: (k, i))
        b_spec = pl.BlockSpec((tk, tn_), lambda i, j, k: (k, j))
    else:
        m, kd = a.shape
        n = b.shape[0] if mode == "nt" else b.shape[1]
        tm, tn_, tk = _pick(m, TOKEN_TILES), _pick(n, FEATURE_TILES), _pick(kd, FEATURE_TILES)
        a_spec = pl.BlockSpec((tm, tk), lambda i, j, k: (i, k))
        if mode == "nt":
            b_spec = pl.BlockSpec((tn_, tk), lambda i, j, k: (j, k))
        else:
            b_spec = pl.BlockSpec((tk, tn_), lambda i, j, k: (k, j))
    dims = {"nn": NN, "nt": NT, "tn": TN}[mode]
    nk = kd // tk
    has_bias, has_add = bias is not None, addend is not None
    assert not colsum or mode == "tn"

    def body(*refs):
        it = iter(refs)
        a_ref, b_ref = next(it), next(it)
        bias_ref = next(it) if has_bias else None
        add_ref = next(it) if has_add else None
        o_ref = next(it)
        cs_ref = next(it) if colsum else None
        acc_ref = next(it)
        j, k = pl.program_id(1), pl.program_id(2)

        @pl.when(k == 0)
        def _():
            if has_add:
                acc_ref[...] = add_ref[...].astype(F32)
            else:
                acc_ref[...] = jnp.zeros_like(acc_ref)

        av = a_ref[...]
        acc_ref[...] += _dot(av, b_ref[...], dims)
        if colsum:
            @pl.when(jnp.logical_and(j == 0, k == 0))
            def _():
                cs_ref[...] = jnp.zeros_like(cs_ref)

            @pl.when(j == 0)
            def _():
                cs_ref[...] += jnp.sum(av.astype(F32), axis=0, keepdims=True)

        @pl.when(k == nk - 1)
        def _():
            r = acc_ref[...]
            if has_bias:
                r = r + bias_ref[...]
            o_ref[...] = r.astype(out_dtype)

    in_specs, args = [a_spec, b_spec], [a, b]
    if has_bias:
        in_specs.append(pl.BlockSpec((1, tn_), lambda i, j, k: (0, j)))
        args.append(bias)
    if has_add:
        in_specs.append(pl.BlockSpec((tm, tn_), lambda i, j, k: (i, j)))
        args.append(addend)
    out_shape = [jax.ShapeDtypeStruct((m, n), out_dtype)]
    out_specs = [pl.BlockSpec((tm, tn_), lambda i, j, k: (i, j))]
    if colsum:
        out_shape.append(jax.ShapeDtypeStruct((1, m), F32))
        out_specs.append(pl.BlockSpec((1, tm), lambda i, j, k: (0, i)))
    res = pl.pallas_call(
        body, out_shape=out_shape, grid=(m // tm, n // tn_, nk), in_specs=in_specs, out_specs=out_specs,
        scratch_shapes=[pltpu.VMEM((tm, tn_), F32)], compiler_params=_params(3), name=name,
    )(*args)
    return res if colsum else res[0]


def _t5_bucket(dist):
    max_exact = NUM_BUCKETS // 2
    large = max_exact + (np.log(np.maximum(dist, max_exact) / max_exact)
                         / np.log(MAX_DISTANCE / max_exact) * (NUM_BUCKETS - max_exact)).astype(np.int32)
    large = np.minimum(large, NUM_BUCKETS - 1)
    return np.where(dist < max_exact, dist, large).astype(np.int32)


def _bucket_maps():
    qi = np.arange(BLK)[:, None]
    kj = np.arange(2 * BLK)[None, :]
    rel = np.clip(qi + BLK - kj, 0, BLK)
    return jnp.asarray(np.stack([_t5_bucket(rel * d) for _, d in ATTN_GROUPS]))


def _bias_tiles(rel_bias, bmaps):
    def body(tab_ref, bm_ref, o_ref):
        h = pl.program_id(0)
        bm = bm_ref[0]
        acc = jnp.zeros((BLK, 2 * BLK), F32)
        for b in range(NUM_BUCKETS):
            acc = jnp.where(bm == b, tab_ref[b, h], acc)
        o_ref[0] = acc

    return pl.pallas_call(
        body, out_shape=jax.ShapeDtypeStruct((N_ATTN_HEADS, BLK, 2 * BLK), F32), grid=(N_ATTN_HEADS,),
        in_specs=[pl.BlockSpec(memory_space=pltpu.SMEM),
                  pl.BlockSpec((1, BLK, 2 * BLK), lambda h: (h // HEADS_PER_GROUP, 0, 0))],
        out_specs=pl.BlockSpec((1, BLK, 2 * BLK), lambda h: (h, 0, 0)),
        compiler_params=_params(1), name="bias_tiles",
    )(rel_bias, bmaps)


def _bias_tiles_bwd(dbias_layers, bmaps):
    nl = len(dbias_layers)

    def body(*refs):
        bm = refs[nl][0]
        o_ref = refs[nl + 1]
        x = refs[0][0]
        for r in refs[1:nl]:
            x = x + r[0]
        lane = lax.broadcasted_iota(jnp.int32, (1, BLK), 1)
        row = jnp.zeros((1, BLK), F32)
        for b in range(NUM_BUCKETS):
            s = jnp.sum(jnp.where(bm == b, x, 0.0), axis=1, keepdims=True)
            s = jnp.sum(s, axis=0, keepdims=True)
            row = jnp.where(lane == b, s, row)
        o_ref[0] = row

    tile = pl.BlockSpec((1, BLK, 2 * BLK), lambda h: (h, 0, 0))
    out = pl.pallas_call(
        body, out_shape=jax.ShapeDtypeStruct((N_ATTN_HEADS, 1, BLK), F32), grid=(N_ATTN_HEADS,),
        in_specs=[tile] * nl + [pl.BlockSpec((1, BLK, 2 * BLK), lambda h: (h // HEADS_PER_GROUP, 0, 0))],
        out_specs=pl.BlockSpec((1, 1, BLK), lambda h: (h, 0, 0)),
        compiler_params=_params(1), name="bias_tiles_bwd",
    )(*dbias_layers, bmaps)
    return out[:, 0, :NUM_BUCKETS].T


def _pair_masks():
    lane = lax.broadcasted_iota(jnp.int32, (BLK, BLK), 1)
    row2 = lax.broadcasted_iota(jnp.int32, (2 * BLK, BLK), 0)
    lane2 = lax.broadcasted_iota(jnp.int32, (2 * BLK, BLK), 1)
    own = (lane2 // HEAD_DIM) == (row2 // BLK)
    qi = lax.broadcasted_iota(jnp.int32, (2 * BLK, 2 * BLK), 0) & (BLK - 1)
    kj = lax.broadcasted_iota(jnp.int32, (2 * BLK, 2 * BLK), 1)
    band = jnp.logical_and(kj >= qi, kj <= qi + BLK)
    return lane < HEAD_DIM, own, band, kj < BLK


def _pair_scores(q32, kb, bias2, own, band, is_prev, pen):
    qm = jnp.where(own, jnp.concatenate([q32, q32], axis=0), 0.0)
    s = _dot(qm, kb, NT) * (HEAD_DIM ** -0.5) + bias2
    if pen is not None:
        s = s + jnp.where(is_prev, pen, 0.0)
    return jnp.where(band, s, NEG), qm


HEAD_PAIRS = HEADS_PER_GROUP // 2


def _attn_fwd(qa, ka, va, cb, offs, bias, g, dil):
    rows = qa.shape[0]
    nb = rows // BLK
    rb = 2 if nb % 2 == 0 else 1
    oq, ok, ov = offs

    def cur(off):
        return pl.BlockSpec((rb * BLK, ATTN_OUT), lambda r, n: (n, r * cb + off))

    def prev(off):
        return pl.BlockSpec((BLK, ATTN_OUT), lambda r, n: (jnp.maximum(rb * n - 1, 0), r * cb + off))

    def body(q_ref, kp_ref, kc_ref, vp_ref, vc_ref, b_ref, o_ref, l_ref):
        n = pl.program_id(1)
        pen0 = jnp.where(n > 0, 0.0, NEG)
        first, own, band, is_prev = _pair_masks()
        o_rows, l_rows = [], []
        for t in range(rb):
            rows_t = slice(t * BLK, (t + 1) * BLK)
            o_parts, l_parts = [], []
            for hp in range(HEAD_PAIRS):
                cols = slice(hp * BLK, (hp + 1) * BLK)
                if t == 0:
                    kp, vp, pen = kp_ref[:, cols], vp_ref[:, cols], pen0
                else:
                    before = slice((t - 1) * BLK, t * BLK)
                    kp, vp, pen = kc_ref[before, cols], vc_ref[before, cols], None
                kb = jnp.concatenate([kp, kc_ref[rows_t, cols]], axis=0)
                vb = jnp.concatenate([vp, vc_ref[rows_t, cols]], axis=0)
                bias2 = jnp.concatenate([b_ref[2 * hp], b_ref[2 * hp + 1]], axis=0)
                s, _ = _pair_scores(q_ref[rows_t, cols].astype(F32), kb, bias2, own, band, is_prev, pen)
                m = jnp.max(s, axis=1, keepdims=True)
                p = jnp.exp(s - m)
                l = jnp.sum(p, axis=1, keepdims=True)
                o2 = _dot(p * (1.0 / l), vb, NN)
                lse2 = m + jnp.log(l)
                o_parts.append(jnp.where(first, o2[:BLK], o2[BLK:]))
                l_parts.append(jnp.where(first, lse2[:BLK], lse2[BLK:]))
            o_rows.append(jnp.concatenate(o_parts, axis=1))
            l_rows.append(jnp.concatenate(l_parts, axis=1))
        o_ref[...] = jnp.concatenate(o_rows, axis=0)
        l_ref[...] = jnp.concatenate(l_rows, axis=0)

    out_spec = pl.BlockSpec((rb * BLK, ATTN_OUT), lambda r, n: (n, r))
    shape = jax.ShapeDtypeStruct((rows, dil * ATTN_OUT), F32)
    return pl.pallas_call(
        body, out_shape=[shape, shape], grid=(dil, nb // rb),
        in_specs=[cur(oq), prev(ok), cur(ok), prev(ov), cur(ov),
                  pl.BlockSpec((HEADS_PER_GROUP, BLK, 2 * BLK), lambda r, n: (g, 0, 0))],
        out_specs=[out_spec, out_spec], compiler_params=_params(2), name=f"attn_fwd_g{g}",
    )(qa, ka, ka, va, va, bias)


def _attn_bwd(qa, ka, va, cb, offs, bias, dy, ya, lt, g, dil):
    rows = qa.shape[0]
    nb = rows // BLK
    oq, ok, ov = offs

    def cur(off, c):
        return pl.BlockSpec((BLK, ATTN_OUT), lambda r, n: (jnp.minimum(n, nb - 1), r * c + off))

    def prev(off, c):
        return pl.BlockSpec((BLK, ATTN_OUT), lambda r, n: (jnp.clip(n - 1, 0, nb - 1), r * c + off))

    def body(q_ref, kp_ref, kc_ref, vp_ref, vc_ref, b_ref, dy_ref, ya_ref, lt_ref,
             dq_ref, dk_ref, dv_ref, db_ref, ck_ref, cv_ref):
        r, n = pl.program_id(0), pl.program_id(1)

        @pl.when(jnp.logical_and(r == 0, n == 0))
        def _():
            db_ref[...] = jnp.zeros_like(db_ref)

        @pl.when(n == 0)
        def _():
            ck_ref[...] = jnp.zeros_like(ck_ref)
            cv_ref[...] = jnp.zeros_like(cv_ref)

        @pl.when(n < nb)
        def _():
            pen = jnp.where(n > 0, 0.0, NEG)
            first, own, band, is_prev = _pair_masks()
            second = jnp.logical_not(first)
            scale = HEAD_DIM ** -0.5
            parts = {k: [] for k in ("dq", "dkp", "dkc", "dvp", "dvc")}
            db_parts = []
            for hp in range(HEAD_PAIRS):
                cols = slice(hp * BLK, (hp + 1) * BLK)
                kb = jnp.concatenate([kp_ref[:, cols], kc_ref[:, cols]], axis=0)
                vb = jnp.concatenate([vp_ref[:, cols], vc_ref[:, cols]], axis=0)
                bias2 = jnp.concatenate([b_ref[2 * hp], b_ref[2 * hp + 1]], axis=0)
                dy_, lt_ = dy_ref[:, cols], lt_ref[:, cols]
                dyy = dy_ * ya_ref[:, cols]
                per_head = lambda t, red, fill: jnp.concatenate(
                    [red(jnp.where(first, t, fill), axis=1, keepdims=True),
                     red(jnp.where(second, t, fill), axis=1, keepdims=True)], axis=0)
                lse2 = per_head(lt_, jnp.max, NEG)
                delta2 = per_head(dyy, jnp.sum, 0.0)
                s, qm = _pair_scores(q_ref[:, cols].astype(F32), kb, bias2, own, band, is_prev, pen)
                p = jnp.exp(s - lse2)
                dym = jnp.where(own, jnp.concatenate([dy_, dy_], axis=0), 0.0)
                ds = p * (_dot(dym, vb, NT) - delta2)
                db_parts += [ds[:BLK], ds[BLK:]]
                dq2 = _dot(ds, kb, NN) * scale
                dkb = _dot(ds, qm, TN) * scale
                dvb = _dot(p, dym, TN)
                for k, val in (("dq", jnp.where(first, dq2[:BLK], dq2[BLK:])), ("dkp", dkb[:BLK]),
                               ("dkc", dkb[BLK:]), ("dvp", dvb[:BLK]), ("dvc", dvb[BLK:])):
                    parts[k].append(val)
            wide = {k: jnp.concatenate(val, axis=1) for k, val in parts.items()}
            db_ref[...] += jnp.stack(db_parts, axis=0)
            dq_ref[...] = wide["dq"].astype(dq_ref.dtype)
            dk_ref[...] = (ck_ref[...] + wide["dkp"]).astype(dk_ref.dtype)
            dv_ref[...] = (cv_ref[...] + wide["dvp"]).astype(dv_ref.dtype)
            ck_ref[...] = wide["dkc"]
            cv_ref[...] = wide["dvc"]

        @pl.when(n == nb)
        def _():
            dk_ref[...] = ck_ref[...].astype(dk_ref.dtype)
            dv_ref[...] = cv_ref[...].astype(dv_ref.dtype)

    late = pl.BlockSpec((BLK, ATTN_OUT), lambda r, n: (jnp.maximum(n - 1, 0), r))
    shape = jax.ShapeDtypeStruct((rows, dil * ATTN_OUT), MXU_DTYPE)
    return pl.pallas_call(
        body,
        out_shape=[shape, shape, shape, jax.ShapeDtypeStruct((HEADS_PER_GROUP, BLK, 2 * BLK), F32)],
        grid=(dil, nb + 1),
        in_specs=[cur(oq, cb), prev(ok, cb), cur(ok, cb), prev(ov, cb), cur(ov, cb),
                  pl.BlockSpec((HEADS_PER_GROUP, BLK, 2 * BLK), lambda r, n: (g, 0, 0)),
                  cur(0, 1), cur(0, 1), cur(0, 1)],
        out_specs=[cur(0, 1), late, late, pl.BlockSpec((HEADS_PER_GROUP, BLK, 2 * BLK), lambda r, n: (0, 0, 0))],
        scratch_shapes=[pltpu.VMEM((BLK, ATTN_OUT), F32), pltpu.VMEM((BLK, ATTN_OUT), F32)],
        compiler_params=_params(2), name=f"attn_bwd_g{g}",
    )(qa, ka, ka, va, va, bias, dy, ya, lt)


def _attn_combine(o_list, l_list):
    s = o_list[0].shape[0]
    tr = _pick(s, (1024, 512, 256, 128))

    def body(o0, o1, o2, l0, l1, l2, y_ref, yb_ref, lt_ref):
        a0, a1, a2 = l0[...], l1[...], l2[...]
        mx = jnp.maximum(jnp.maximum(a0, a1), a2)
        e0, e1, e2 = jnp.exp(a0 - mx), jnp.exp(a1 - mx), jnp.exp(a2 - mx)
        den = e0 + e1 + e2
        inv = 1.0 / den
        y = (e0 * inv) * o0[...] + (e1 * inv) * o1[...] + (e2 * inv) * o2[...]
        y_ref[...] = y
        yb_ref[...] = y.astype(yb_ref.dtype)
        lt_ref[...] = mx + jnp.log(den)

    spec = pl.BlockSpec((tr, ATTN_OUT), lambda i: (i, 0))
    f = jax.ShapeDtypeStruct((s, ATTN_OUT), F32)
    return pl.pallas_call(
        body, out_shape=[f, jax.ShapeDtypeStruct((s, ATTN_OUT), MXU_DTYPE), f], grid=(s // tr,),
        in_specs=[spec] * 6, out_specs=[spec] * 3, compiler_params=_params(1), name="attn_combine",
    )(*o_list, *l_list)


def _ret_consts(s):
    half = RET_QK // 2
    pos = jnp.arange(s, dtype=F32)
    inv_freq = ROPE_BASE ** (-jnp.arange(half, dtype=F32) / half)
    ang = pos[:, None] * inv_freq[None]
    log_g = jnp.log(1.0 - 2.0 ** (-5.0 - jnp.arange(RET_HEADS, dtype=F32)))
    n = jnp.arange(RET_CHUNK, dtype=F32)
    diff = n[:, None] - n[None, :]
    dmask = jnp.where(diff >= 0, jnp.exp(log_g[:, None, None] * jnp.maximum(diff, 0.0)), 0.0)
    qdec = jnp.exp(log_g[:, None] * (n + 1.0))
    kdec = jnp.exp(log_g[:, None] * (RET_CHUNK - 1.0 - n))
    cdec = jnp.exp(log_g * RET_CHUNK)
    wide = (RET_HEADS, RET_CHUNK, RET_QK)
    return dict(cos=jnp.cos(ang), sin=jnp.sin(ang), dmask=dmask,
                qdec=jnp.broadcast_to(qdec[:, :, None], wide), kdec=jnp.broadcast_to(kdec[:, :, None], wide),
                cdec=cdec)


def _rot(t, cs, sn):
    half = RET_QK // 2
    t1, t2 = t[:, :half], t[:, half:]
    return jnp.concatenate([t1 * cs - t2 * sn, t1 * sn + t2 * cs], axis=1)


def _rot_bwd(d, cs, sn):
    half = RET_QK // 2
    d1, d2 = d[:, :half], d[:, half:]
    return jnp.concatenate([d1 * cs + d2 * sn, d2 * cs - d1 * sn], axis=1)


def _ret_fwd(zq, zk, zv, zg, rc):
    s = zq.shape[0]
    nc = s // RET_CHUNK
    c = RET_CHUNK

    def body(cd_ref, q_ref, k_ref, v_ref, g_ref, cos_ref, sin_ref, dm_ref, qd_ref, kd_ref,
             yb_ref, o_ref, st_ref, state):
        n = pl.program_id(0)

        @pl.when(n == 0)
        def _():
            state[...] = jnp.zeros_like(state)

        cs, sn = cos_ref[...], sin_ref[...]
        for h in range(RET_HEADS):
            qs, vs = slice(h * RET_QK, (h + 1) * RET_QK), slice(h * RET_V, (h + 1) * RET_V)
            qr = _rot(q_ref[:, qs], cs, sn)
            kr = _rot(k_ref[:, qs], cs, sn) * (RET_QK ** -0.5)
            v = v_ref[:, vs]
            st = state[h]
            st_ref[h, 0] = st.astype(st_ref.dtype)
            sc = _dot(qr, kr, NT) * dm_ref[h]
            o = _dot(sc, v, NN) + _dot(qr * qd_ref[h], st, NN)
            state[h] = st * cd_ref[h] + _dot(kr * kd_ref[h], v, TN)
            o_ref[:, vs] = o
            xh, _ = _norm_rows(o, GN_EPS)
            gv = g_ref[:, vs]
            yb_ref[:, vs] = (gv * _sigmoid(gv) * xh).astype(yb_ref.dtype)

    row = lambda w: pl.BlockSpec((c, w), lambda n: (n, 0))
    const = lambda a: pl.BlockSpec(a.shape, lambda n: (0, 0, 0))
    return pl.pallas_call(
        body,
        out_shape=[jax.ShapeDtypeStruct((s, RET_HEADS * RET_V), MXU_DTYPE),
                   jax.ShapeDtypeStruct((s, RET_HEADS * RET_V), F32),
                   jax.ShapeDtypeStruct((RET_HEADS, nc, RET_QK, RET_V), MXU_DTYPE)],
        grid=(nc,),
        in_specs=[pl.BlockSpec(memory_space=pltpu.SMEM), row(RET_HEADS * RET_QK), row(RET_HEADS * RET_QK),
                  row(RET_HEADS * RET_V), row(RET_HEADS * RET_V), row(RET_QK // 2), row(RET_QK // 2),
                  const(rc["dmask"]), const(rc["qdec"]), const(rc["kdec"])],
        out_specs=[row(RET_HEADS * RET_V), row(RET_HEADS * RET_V),
                   pl.BlockSpec((RET_HEADS, 1, RET_QK, RET_V), lambda n: (0, n, 0, 0))],
        scratch_shapes=[pltpu.VMEM((RET_HEADS, RET_QK, RET_V), F32)],
        compiler_params=_params(1), name="ret_fwd",
    )(rc["cdec"], zq, zk, zv, zg, rc["cos"], rc["sin"], rc["dmask"], rc["qdec"], rc["kdec"])


def _ret_bwd(zq, zk, zv, zg, o_ret, states, dyb, rc):
    s = zq.shape[0]
    nc = s // RET_CHUNK
    c = RET_CHUNK

    def body(cd_ref, q_ref, k_ref, v_ref, g_ref, o_ref, st_ref, dy_ref, cos_ref, sin_ref, dm_ref, qd_ref,
             kd_ref, dq_ref, dk_ref, dv_ref, dg_ref, dstate):
        n = pl.program_id(0)

        @pl.when(n == 0)
        def _():
            dstate[...] = jnp.zeros_like(dstate)

        cs, sn = cos_ref[...], sin_ref[...]
        for h in range(RET_HEADS):
            qs, vs = slice(h * RET_QK, (h + 1) * RET_QK), slice(h * RET_V, (h + 1) * RET_V)
            qr = _rot(q_ref[:, qs], cs, sn)
            kr = _rot(k_ref[:, qs], cs, sn) * (RET_QK ** -0.5)
            v = v_ref[:, vs]
            st = st_ref[h, 0]
            dm, qd, kd = dm_ref[h], qd_ref[h], kd_ref[h]
            xh, rstd = _norm_rows(o_ref[:, vs], GN_EPS)
            gv, dy = g_ref[:, vs], dy_ref[:, vs]
            sg = _sigmoid(gv)
            dg_ref[:, vs] = dy * xh * (sg * (1.0 + gv * (1.0 - sg)))
            do = _norm_rows_bwd(dy * (gv * sg), xh, rstd)
            ds_next = dstate[h]
            sc = _dot(qr, kr, NT) * dm
            da = _dot(do, v, NT) * dm
            dv = _dot(sc, do, TN) + _dot(kr * kd, ds_next, NN)
            dqr = _dot(da, kr, NN) + _dot(do, st, NT) * qd
            dkr = _dot(da, qr, TN) + _dot(v, ds_next, NT) * kd
            dstate[h] = ds_next * cd_ref[h] + _dot(qr * qd, do, TN)
            dq_ref[:, qs] = _rot_bwd(dqr, cs, sn)
            dk_ref[:, qs] = _rot_bwd(dkr * (RET_QK ** -0.5), cs, sn)
            dv_ref[:, vs] = dv.astype(dv_ref.dtype)

    row = lambda w: pl.BlockSpec((c, w), lambda n: (nc - 1 - n, 0))
    const = lambda a: pl.BlockSpec(a.shape, lambda n: (0, 0, 0))
    f_qk = jax.ShapeDtypeStruct((s, RET_HEADS * RET_QK), F32)
    qk_w, v_w = RET_HEADS * RET_QK, RET_HEADS * RET_V
    return pl.pallas_call(
        body,
        out_shape=[f_qk, f_qk, jax.ShapeDtypeStruct((s, v_w), MXU_DTYPE), jax.ShapeDtypeStruct((s, v_w), F32)],
        grid=(nc,),
        in_specs=[pl.BlockSpec(memory_space=pltpu.SMEM), row(qk_w), row(qk_w), row(v_w), row(v_w), row(v_w),
                  pl.BlockSpec((RET_HEADS, 1, RET_QK, RET_V), lambda n: (0, nc - 1 - n, 0, 0)), row(v_w),
                  row(RET_QK // 2), row(RET_QK // 2), const(rc["dmask"]), const(rc["qdec"]), const(rc["kdec"])],
        out_specs=[row(qk_w), row(qk_w), row(v_w), row(v_w)],
        scratch_shapes=[pltpu.VMEM((RET_HEADS, RET_QK, RET_V), F32)],
        compiler_params=_params(1), name="ret_bwd",
    )(rc["cdec"], zq, zk, zv, zg, o_ret, states, dyb, rc["cos"], rc["sin"], rc["dmask"], rc["qdec"],
      rc["kdec"])


ROW_TILE = 256


def _mix_fwd(x, ya_b, yb_b, ze, wap_t, wrp, wout, gam, bet):
    s = x.shape[0]
    tm = ROW_TILE

    def body(x_ref, ya_ref, yb_ref, ze_ref, wap_ref, wrp_ref, wout_ref, g_ref, b_ref,
             x1_ref, u1_ref, a1_ref, a2_ref, mg_ref):
        a1 = _dot(ya_ref[...], wap_ref[...], NT)
        a2 = _dot(yb_ref[...], wrp_ref[...], NN)
        ze_ = ze_ref[...]
        merged = _sigmoid(ze_[:, :D_MODEL]) * a1 + _sigmoid(ze_[:, D_MODEL:]) * a2
        u = ALPHA * x_ref[...] + _dot(merged, wout_ref[...], NN)
        xh, _ = _norm_rows(u, LN_EPS)
        x1_ref[...] = xh * g_ref[...] + b_ref[...]
        u1_ref[...] = u
        a1_ref[...] = a1
        a2_ref[...] = a2
        mg_ref[...] = merged.astype(mg_ref.dtype)

    row = lambda w: pl.BlockSpec((tm, w), lambda i: (i, 0))
    full = _resident
    f = jax.ShapeDtypeStruct((s, D_MODEL), F32)
    return pl.pallas_call(
        body, out_shape=[f, f, f, f, jax.ShapeDtypeStruct((s, D_MODEL), MXU_DTYPE)], grid=(s // tm,),
        in_specs=[row(D_MODEL), row(ATTN_OUT), row(RET_HEADS * RET_V), row(2 * D_MODEL), full(wap_t), full(wrp),
                  full(wout), full(gam), full(bet)],
        out_specs=[row(D_MODEL)] * 5, compiler_params=_params(1), name="mix_fwd",
    )(x, ya_b, yb_b, ze, wap_t, wrp, wout, gam, bet)


def _mix_bwd(dx1, u1, a1, a2, ze, wap_t, wrp, wout, gam):
    s = dx1.shape[0]
    tm = ROW_TILE

    def body(dx_ref, u_ref, a1_ref, a2_ref, ze_ref, wap_ref, wrp_ref, wout_ref, g_ref,
             dres_ref, du_ref, da1_ref, da2_ref, dze_ref, dya_ref, dyb_ref, dgam_ref, dbet_ref):
        @pl.when(pl.program_id(0) == 0)
        def _():
            dgam_ref[...] = jnp.zeros_like(dgam_ref)
            dbet_ref[...] = jnp.zeros_like(dbet_ref)

        dx = dx_ref[...]
        xh, rstd = _norm_rows(u_ref[...], LN_EPS)
        dgam_ref[...] += jnp.sum(dx * xh, axis=0, keepdims=True)
        dbet_ref[...] += jnp.sum(dx, axis=0, keepdims=True)
        du = _norm_rows_bwd(dx * g_ref[...], xh, rstd)
        dres_ref[...] = ALPHA * du
        du_ref[...] = du.astype(du_ref.dtype)
        dm = _dot(du, wout_ref[...], NT)
        ze_ = ze_ref[...]
        sa, sb = _sigmoid(ze_[:, :D_MODEL]), _sigmoid(ze_[:, D_MODEL:])
        da1, da2 = dm * sa, dm * sb
        dze_ref[...] = jnp.concatenate([dm * a1_ref[...] * (sa * (1.0 - sa)),
                                        dm * a2_ref[...] * (sb * (1.0 - sb))], axis=1)
        da1_ref[...] = da1.astype(da1_ref.dtype)
        da2_ref[...] = da2.astype(da2_ref.dtype)
        dya_ref[...] = _dot(da1, wap_ref[...], NN)
        dyb_ref[...] = _dot(da2, wrp_ref[...], NT)

    row = lambda w: pl.BlockSpec((tm, w), lambda i: (i, 0))
    full = _resident
    vec = pl.BlockSpec((1, D_MODEL), lambda i: (0, 0))
    f = lambda w: jax.ShapeDtypeStruct((s, w), F32)
    m = lambda w: jax.ShapeDtypeStruct((s, w), MXU_DTYPE)
    v = jax.ShapeDtypeStruct((1, D_MODEL), F32)
    return pl.pallas_call(
        body,
        out_shape=[f(D_MODEL), m(D_MODEL), m(D_MODEL), m(D_MODEL), f(2 * D_MODEL), f(ATTN_OUT),
                   f(RET_HEADS * RET_V), v, v],
        grid=(s // tm,),
        in_specs=[row(D_MODEL)] * 4 + [row(2 * D_MODEL), full(wap_t), full(wrp), full(wout), full(gam)],
        out_specs=[row(D_MODEL)] * 4 + [row(2 * D_MODEL), row(ATTN_OUT), row(RET_HEADS * RET_V), vec, vec],
        compiler_params=_params(1), name="mix_bwd",
    )(dx1, u1, a1, a2, ze, wap_t, wrp, wout, gam)


FF_CHUNK = 1408


def _ffn_fwd(x1, wg_t, wu_t, wd, gam, bet):
    s = x1.shape[0]
    tm, fc = ROW_TILE, FF_CHUNK

    def body(x_ref, wg_ref, wu_ref, wd_ref, g_ref, b_ref, x2_ref, u2_ref, a_ref, b_out_ref, h_ref):
        xv = x_ref[...]
        u = ALPHA * xv
        for f0 in range(0, D_FF, fc):
            ch = slice(f0, f0 + fc)
            a = _dot(xv, wg_ref[ch, :], NT)
            b = _dot(xv, wu_ref[ch, :], NT)
            hid = a * _sigmoid(a) * b
            u = u + _dot(hid, wd_ref[ch, :], NN)
            a_ref[:, ch] = a
            b_out_ref[:, ch] = b
            h_ref[:, ch] = hid.astype(h_ref.dtype)
        xh, _ = _norm_rows(u, LN_EPS)
        u2_ref[...] = u
        x2_ref[...] = xh * g_ref[...] + b_ref[...]

    row = lambda w: pl.BlockSpec((tm, w), lambda i: (i, 0))
    f = lambda w: jax.ShapeDtypeStruct((s, w), F32)
    return pl.pallas_call(
        body, out_shape=[f(D_MODEL), f(D_MODEL), f(D_FF), f(D_FF), jax.ShapeDtypeStruct((s, D_FF), MXU_DTYPE)],
        grid=(s // tm,),
        in_specs=[row(D_MODEL), _resident(wg_t), _resident(wu_t), _resident(wd), _resident(gam), _resident(bet)],
        out_specs=[row(D_MODEL), row(D_MODEL), row(D_FF), row(D_FF), row(D_FF)],
        compiler_params=_params(1), name="ffn_fwd",
    )(x1, wg_t, wu_t, wd, gam, bet)


def _ffn_bwd(dx2, u2, fa, fb, wg_t, wu_t, wd, gam):
    s = dx2.shape[0]
    tm, fc = ROW_TILE, FF_CHUNK

    def body(dx_ref, u_ref, a_ref, b_ref, wg_ref, wu_ref, wd_ref, g_ref,
             dx1_ref, du_ref, da_ref, db_ref, dgam_ref, dbet_ref):
        @pl.when(pl.program_id(0) == 0)
        def _():
            dgam_ref[...] = jnp.zeros_like(dgam_ref)
            dbet_ref[...] = jnp.zeros_like(dbet_ref)

        dx = dx_ref[...]
        xh, rstd = _norm_rows(u_ref[...], LN_EPS)
        dgam_ref[...] += jnp.sum(dx * xh, axis=0, keepdims=True)
        dbet_ref[...] += jnp.sum(dx, axis=0, keepdims=True)
        du = _norm_rows_bwd(dx * g_ref[...], xh, rstd)
        du_ref[...] = du.astype(du_ref.dtype)
        acc = ALPHA * du
        for f0 in range(0, D_FF, fc):
            ch = slice(f0, f0 + fc)
            dh = _dot(du, wd_ref[ch, :], NT)
            a, b = a_ref[:, ch], b_ref[:, ch]
            sg = _sigmoid(a)
            da = dh * b * (sg * (1.0 + a * (1.0 - sg)))
            db = dh * (a * sg)
            acc = acc + _dot(da, wg_ref[ch, :], NN) + _dot(db, wu_ref[ch, :], NN)
            da_ref[:, ch] = da.astype(da_ref.dtype)
            db_ref[:, ch] = db.astype(db_ref.dtype)
        dx1_ref[...] = acc

    row = lambda w: pl.BlockSpec((tm, w), lambda i: (i, 0))
    vec = pl.BlockSpec((1, D_MODEL), lambda i: (0, 0))
    v = jax.ShapeDtypeStruct((1, D_MODEL), F32)
    return pl.pallas_call(
        body,
        out_shape=[jax.ShapeDtypeStruct((s, D_MODEL), F32), jax.ShapeDtypeStruct((s, D_MODEL), MXU_DTYPE),
                   jax.ShapeDtypeStruct((s, D_FF), MXU_DTYPE), jax.ShapeDtypeStruct((s, D_FF), MXU_DTYPE), v, v],
        grid=(s // tm,),
        in_specs=[row(D_MODEL), row(D_MODEL), row(D_FF), row(D_FF), _resident(wg_t), _resident(wu_t),
                  _resident(wd), _resident(gam)],
        out_specs=[row(D_MODEL), row(D_MODEL), row(D_FF), row(D_FF), vec, vec],
        compiler_params=_params(1), name="ffn_bwd",
    )(dx2, u2, fa, fb, wg_t, wu_t, wd, gam)


def _loss_head(y, target):
    s = y.shape[0]
    tr = _pick(s, (512, 256, 128))

    def body(y_ref, t_ref, dy_ref, l_ref):
        @pl.when(pl.program_id(0) == 0)
        def _():
            l_ref[...] = jnp.zeros_like(l_ref)

        e = y_ref[...] - t_ref[...]
        dy_ref[...] = e * (1.0 / D_MODEL)
        part = jnp.sum(jnp.sum(e * e, axis=1, keepdims=True), axis=0, keepdims=True)
        l_ref[...] += part * (0.5 / D_MODEL)

    spec = pl.BlockSpec((tr, D_MODEL), lambda i: (i, 0))
    dy, part = pl.pallas_call(
        body, out_shape=[jax.ShapeDtypeStruct((s, D_MODEL), F32), jax.ShapeDtypeStruct((8, BLK), F32)],
        grid=(s // tr,), in_specs=[spec, spec], out_specs=[spec, pl.BlockSpec((8, BLK), lambda i: (0, 0))],
        compiler_params=_params(1), name="loss_head",
    )(y, target)
    return dy, part[0, 0]


def _adamw(w, g, m, v, name):
    rows, cols = w.shape
    budget = 1 << 20
    cands = [t for t in range(8, rows + 1, 8) if rows % t == 0 and t * cols * 4 <= budget]
    tr = max(cands) if cands else rows

    def body(w_ref, g_ref, m_ref, v_ref, d_ref, nm_ref, nv_ref):
        gv = g_ref[...]
        mn = ADAM_B1 * m_ref[...] + (1.0 - ADAM_B1) * gv
        vn = ADAM_B2 * v_ref[...] + (1.0 - ADAM_B2) * (gv * gv)
        m_hat = mn / (1.0 - ADAM_B1 ** ADAM_STEP)
        v_hat = vn / (1.0 - ADAM_B2 ** ADAM_STEP)
        d_ref[...] = -ADAM_LR * (m_hat / (jnp.sqrt(v_hat) + ADAM_EPS) + ADAM_WD * w_ref[...])
        nm_ref[...] = mn
        nv_ref[...] = vn

    spec = pl.BlockSpec((tr, cols), lambda i: (i, 0))
    shape = jax.ShapeDtypeStruct((rows, cols), F32)
    return pl.pallas_call(
        body, out_shape=[shape, shape, shape], grid=(rows // tr,), in_specs=[spec] * 4, out_specs=[spec] * 3,
        compiler_params=_params(1), name=name,
    )(w, g, m, v)


MESH_ID = pl.DeviceIdType.MESH
ANY = pl.BlockSpec(memory_space=pl.ANY)


def _place():
    x, y, c = lax.axis_index("x"), lax.axis_index("y"), lax.axis_index("c")
    other_chips = [(1 - x, y), (x, 1 - y), (1 - x, 1 - y)]
    return x, y, c, other_chips


def _chip_no(chip):
    return 2 * chip[0] + chip[1]


def _gather_exchange(f_ref, g_ref, send_sems, recv_sems, hr):
    x, y, c, chips = _place()
    sibling = (x, y, 1 - c)

    def piece(chip, half):
        return g_ref.at[_chip_no(chip), pl.ds(half * hr, hr), :]

    def copy(k, src, dst, to):
        return pltpu.make_async_remote_copy(src_ref=src, dst_ref=dst, send_sem=send_sems.at[k],
                                            recv_sem=recv_sems.at[k], device_id=to, device_id_type=MESH_ID)

    own = copy(6, f_ref, g_ref.at[_chip_no((x, y))], sibling)
    own.start()
    my_half = f_ref.at[pl.ds(c * hr, hr), :]
    first = [copy(k, my_half, piece((x, y), c), (*chip, c)) for k, chip in enumerate(chips)]
    for cp in first:
        cp.start()
    passed = [copy(3 + k, piece(chip, c), piece(chip, c), sibling) for k, chip in enumerate(chips)]
    for k, chip in enumerate(chips):
        copy(k, my_half, piece(chip, c), (*chip, c)).wait_recv()
        passed[k].start()
    for k, chip in enumerate(chips):
        copy(3 + k, my_half, piece(chip, 1 - c), sibling).wait_recv()
    for cp in first + passed:
        cp.wait_send()
    own.wait()


GATHER_SEMS = (pltpu.SemaphoreType.DMA((7,)), pltpu.SemaphoreType.DMA((7,)))


def _all_gather_rows(flat):
    r, cols = flat.shape

    def body(f_ref, g_ref, send_sems, recv_sems):
        _gather_exchange(f_ref, g_ref, send_sems, recv_sems, r // 2)

    return pl.pallas_call(
        body, out_shape=jax.ShapeDtypeStruct((N_CHIPS, r, cols), flat.dtype), in_specs=[ANY], out_specs=ANY,
        scratch_shapes=list(GATHER_SEMS), name="all_gather_weights",
    )(flat)


def _all_gather_rows_behind(flat):
    r, cols = flat.shape
    f_ref = jax.new_ref(flat, memory_space=pltpu.MemorySpace.HBM)
    g_ref = jax.empty_ref(jax.ShapeDtypeStruct((N_CHIPS, r, cols), flat.dtype), memory_space=pltpu.MemorySpace.HBM)

    @pl.kernel(mesh=plsc.ScalarSubcoreMesh(axis_name="sequencer", num_cores=1), name="all_gather_weights_behind",
               scratch_types=GATHER_SEMS, compiler_params=pltpu.CompilerParams(collective_id=1))
    def launch(send_sems, recv_sems):
        x, y, c, chips = _place()
        barrier = pltpu.get_barrier_semaphore()
        peers = [(x, y, 1 - c)] + [(*chip, c) for chip in chips]
        for peer in peers:
            pl.semaphore_signal(barrier, inc=1, device_id=peer, device_id_type=MESH_ID)
        pl.semaphore_wait(barrier, len(peers))
        _gather_exchange(f_ref, g_ref, send_sems, recv_sems, r // 2)

    launch()
    return g_ref[...]


def _swap_halves(gall):
    _, r, cols = gall.shape
    hr = r // 2

    def body(g_ref, a_ref, send_sem, recv_sem):
        x, y, c, _ = _place()
        cp = pltpu.make_async_remote_copy(src_ref=g_ref.at[:, pl.ds((1 - c) * hr, hr), :], dst_ref=a_ref,
                                          send_sem=send_sem, recv_sem=recv_sem, device_id=(x, y, 1 - c),
                                          device_id_type=MESH_ID)
        cp.start()
        cp.wait()

    return pl.pallas_call(
        body, out_shape=jax.ShapeDtypeStruct((N_CHIPS, hr, cols), gall.dtype), in_specs=[ANY], out_specs=ANY,
        scratch_shapes=[pltpu.SemaphoreType.DMA, pltpu.SemaphoreType.DMA], name="grad_swap_halves",
    )(gall)


def _scatter_to_chips(p):
    _, hr, cols = p.shape

    def body(p_ref, b_ref, send_sems, recv_sems):
        x, y, c, chips = _place()
        cps = [pltpu.make_async_remote_copy(src_ref=p_ref.at[_chip_no(chip)], dst_ref=b_ref.at[k],
                                            send_sem=send_sems.at[k], recv_sem=recv_sems.at[k],
                                            device_id=(*chip, c), device_id_type=MESH_ID)
               for k, chip in enumerate(chips)]
        for cp in cps:
            cp.start()
        for cp in cps:
            cp.wait()

    return pl.pallas_call(
        body, out_shape=jax.ShapeDtypeStruct((3, hr, cols), p.dtype), in_specs=[ANY], out_specs=ANY,
        scratch_shapes=[pltpu.SemaphoreType.DMA((3,)), pltpu.SemaphoreType.DMA((3,))], name="grad_scatter_chips",
    )(p)


def _share_with_sibling(full):
    r, cols = full.shape
    hr = r // 2

    def body(in_ref, out_ref, send_sem, recv_sem):
        x, y, c, _ = _place()
        cp = pltpu.make_async_remote_copy(src_ref=in_ref.at[pl.ds(c * hr, hr), :],
                                          dst_ref=out_ref.at[pl.ds(c * hr, hr), :],
                                          send_sem=send_sem, recv_sem=recv_sem, device_id=(x, y, 1 - c),
                                          device_id_type=MESH_ID)
        cp.start()
        cp.wait()

    return pl.pallas_call(
        body, out_shape=jax.ShapeDtypeStruct((r, cols), full.dtype), in_specs=[ANY], out_specs=ANY,
        input_output_aliases={0: 0}, scratch_shapes=[pltpu.SemaphoreType.DMA, pltpu.SemaphoreType.DMA],
        name="grad_share_sibling",
    )(full)


def _all_reduce_small(v):
    r, cols = v.shape
    n_dev = 8

    def body(x_ref, out_ref, gat_ref, send_sems, recv_sems, local_sem):
        x, y, c, chips = _place()
        me, sibling = (x, y, c), (x, y, 1 - c)

        def slot(px, py, pc):
            return gat_ref.at[4 * px + 2 * py + pc]

        def copy(k, block, to, src=None):
            return pltpu.make_async_remote_copy(src_ref=slot(*block) if src is None else src, dst_ref=slot(*block),
                                                send_sem=send_sems.at[k], recv_sem=recv_sems.at[k], device_id=to,
                                                device_id_type=MESH_ID)

        mine = pltpu.make_async_copy(x_ref, slot(*me), local_sem)
        mine.start()
        first = [copy(0, me, sibling, src=x_ref)]
        first += [copy(1 + k, me, (*chip, c), src=x_ref) for k, chip in enumerate(chips)]
        for cp in first:
            cp.start()
        passed = [copy(4 + k, (*chip, c), sibling) for k, chip in enumerate(chips)]
        for k, chip in enumerate(chips):
            copy(1 + k, (*chip, c), me).wait_recv()
            passed[k].start()
        copy(0, sibling, me).wait_recv()
        for k, chip in enumerate(chips):
            copy(4 + k, (*chip, 1 - c), me).wait_recv()
        for cp in first + passed:
            cp.wait_send()
        mine.wait()
        acc = gat_ref[0]
        for d in range(1, n_dev):
            acc = acc + gat_ref[d]
        out_ref[...] = acc

    vmem = pl.BlockSpec(memory_space=pltpu.VMEM)
    return pl.pallas_call(
        body, out_shape=jax.ShapeDtypeStruct((r, cols), v.dtype), in_specs=[vmem], out_specs=vmem,
        scratch_shapes=[pltpu.VMEM((n_dev, r, cols), v.dtype), pltpu.SemaphoreType.DMA((7,)),
                        pltpu.SemaphoreType.DMA((7,)), pltpu.SemaphoreType.DMA],
        name="all_reduce_small",
    )(v)


WIRE_DTYPE = jnp.bfloat16


def _add_own_half(gall, got, place):
    _, r, cols = gall.shape
    hr = r // 2
    tr = _pick(hr, (256, 128, 64, 32, 16, 8))
    g4 = gall.reshape(N_CHIPS, 2, hr, cols)

    def body(s_ref, g_ref, a_ref, o_ref):
        o_ref[...] = (g_ref[0] + a_ref[...]).astype(o_ref.dtype)

    return pl.pallas_call(
        body, out_shape=jax.ShapeDtypeStruct((N_CHIPS, hr, cols), WIRE_DTYPE),
        grid_spec=pltpu.PrefetchScalarGridSpec(
            num_scalar_prefetch=1, grid=(N_CHIPS, hr // tr),
            in_specs=[pl.BlockSpec((1, 1, tr, cols), lambda j, i, s: (j, s[1], i, 0)),
                      pl.BlockSpec((1, tr, cols), lambda j, i, s: (j, i, 0))],
            out_specs=pl.BlockSpec((1, tr, cols), lambda j, i, s: (j, i, 0))),
        compiler_params=_params(2), name="grad_add_halves",
    )(place, g4, got)


def _add_chip_parts(gall, got, parts, place):
    _, r, cols = gall.shape
    hr = r // 2
    tr = _pick(hr, (256, 128, 64, 32, 16, 8))
    g4 = gall.reshape(N_CHIPS, 2, hr, cols)
    nt = hr // tr

    def body(s_ref, g_ref, a_ref, b0_ref, b1_ref, b2_ref, o_ref):
        own = g_ref[0, 0] + a_ref[0]
        o_ref[...] = ((own + b0_ref[0].astype(F32)) + b1_ref[0].astype(F32)) + b2_ref[0].astype(F32)

    part = lambda k: pl.BlockSpec((1, tr, cols), lambda i, s: (k, i, 0))
    return pl.pallas_call(
        body, out_shape=jax.ShapeDtypeStruct((r, cols), F32),
        grid_spec=pltpu.PrefetchScalarGridSpec(
            num_scalar_prefetch=1, grid=(nt,),
            in_specs=[pl.BlockSpec((1, 1, tr, cols), lambda i, s: (s[0], s[1], i, 0)),
                      pl.BlockSpec((1, tr, cols), lambda i, s: (s[0], i, 0)), part(0), part(1), part(2)],
            out_specs=pl.BlockSpec((tr, cols), lambda i, s: (s[1] * nt + i, 0))),
        compiler_params=_params(1), name="grad_add_chips",
    )(place, g4, got, parts, parts, parts)


def _pack_shards(l, w_in, w_ap, w_rp, w_out, w_g, w_u, w_d, dtype):
    parts = [w_in[l].T, w_ap[l].T.reshape(-1, D_MODEL), w_rp[l], w_out[l], w_g[l].T, w_u[l].T, w_d[l]]
    return jnp.concatenate([p.astype(dtype) for p in parts], axis=0)


def _unpack_gathered(gathered):
    w, off = {}, 0
    for name, rows in PACK_ROWS:
        blk = gathered[:, off:off + rows]
        if name == "w_attn_proj":
            w[name] = blk.reshape(N_CHIPS * 256, ATTN_OUT)
        else:
            w[name] = blk.reshape(N_CHIPS * rows, D_MODEL)
        off += rows
    return w


def _pack_full_grads(grads):
    parts = []
    for g in grads:
        for name, rows in PACK_ROWS:
            parts.append(g[name].reshape(N_CHIPS, rows, D_MODEL))
    return jnp.concatenate(parts, axis=1)


COLUMN_SHARDED = ("w_in", "w_attn_proj", "w_ffn_gate", "w_ffn_up")


def _unpack_shard_grads(full):
    out = {name: [] for name, _ in PACK_ROWS}
    off = 0
    for _ in range(DEPTH):
        for name, rows in PACK_ROWS:
            blk = full[off:off + rows]
            out[name].append(blk.reshape(256, ATTN_OUT) if name == "w_attn_proj" else blk)
            off += rows
    return {k: jnp.stack(v) for k, v in out.items()}


def _attn_views(za, s):
    views = []
    for g, (_, dil) in enumerate(ATTN_GROUPS):
        if dil == 1:
            views.append((za, za, za, 3 * N_GROUPS, (g, N_GROUPS + g, 2 * N_GROUPS + g)))
        else:
            rows = s // dil
            part = lambda k: za[:, k * ATTN_WIDTH + g * ATTN_OUT:k * ATTN_WIDTH + (g + 1) * ATTN_OUT].reshape(
                rows, dil * ATTN_OUT)
            views.append((part(0), part(1), part(2), 1, (0, 0, 0)))
    return views


def _layer_fwd(x, w, lnp, bias, rc, tag):
    s = x.shape[0]
    z = {}
    for name, off, width, narrow in SECTIONS:
        z[name] = _matmul(x, w["w_in"][off:off + width], "nt", bias=lnp["b_in"][:, off:off + width],
                          out_dtype=MXU_DTYPE if narrow else F32, name=f"in_proj_{name}_{tag}")
    views = _attn_views(z["a"], s)
    o_list, l_list = [], []
    for g, (_, dil) in enumerate(ATTN_GROUPS):
        qa, ka, va, cb, offs = views[g]
        o, lse = _attn_fwd(qa, ka, va, cb, offs, bias, g, dil)
        o_list.append(o.reshape(s, ATTN_OUT))
        l_list.append(lse.reshape(s, ATTN_OUT))
    ya, ya_b, lt = _attn_combine(o_list, l_list)
    yb_b, o_ret, states = _ret_fwd(z["bq"], z["bk"], z["c"], z["d"], rc)
    x1, u1, a1, a2, mg_b = _mix_fwd(x, ya_b, yb_b, z["e"], w["w_attn_proj"], w["w_ret_proj"], w["w_out"],
                                    lnp["ln1_g"], lnp["ln1_b"])
    x2, u2, fa, fb, h_b = _ffn_fwd(x1, w["w_ffn_gate"], w["w_ffn_up"], w["w_ffn_down"], lnp["ln2_g"], lnp["ln2_b"])
    saved = dict(x=x, z=z, views=views, ya=ya, ya_b=ya_b, lt=lt, yb_b=yb_b, o_ret=o_ret, states=states,
                 x1=x1, u1=u1, a1=a1, a2=a2, mg_b=mg_b, u2=u2, fa=fa, fb=fb, h_b=h_b)
    return x2, saved


def _layer_bwd(dx2, w, lnp, sv, bias, rc, tag):
    s = dx2.shape[0]
    z = sv["z"]
    dx1, du2_b, da_b, db_b, dg2, dbt2 = _ffn_bwd(dx2, sv["u2"], sv["fa"], sv["fb"], w["w_ffn_gate"], w["w_ffn_up"],
                                                  w["w_ffn_down"], lnp["ln2_g"])
    gw = {}
    gw["w_ffn_down"] = _matmul(sv["h_b"], du2_b, "tn", name=f"dw_ffn_down_{tag}")
    gw["w_ffn_gate"] = _matmul(da_b, sv["x1"], "tn", name=f"dw_ffn_gate_{tag}")
    gw["w_ffn_up"] = _matmul(db_b, sv["x1"], "tn", name=f"dw_ffn_up_{tag}")
    dres, du1_b, da1_b, da2_b, dze, dya, dyb, dg1, dbt1 = _mix_bwd(
        dx1, sv["u1"], sv["a1"], sv["a2"], z["e"], w["w_attn_proj"], w["w_ret_proj"], w["w_out"], lnp["ln1_g"])
    gw["w_out"] = _matmul(sv["mg_b"], du1_b, "tn", name=f"dw_out_{tag}")
    gw["w_attn_proj"] = _matmul(da1_b, sv["ya_b"], "tn", name=f"dw_attn_proj_{tag}")
    gw["w_ret_proj"] = _matmul(sv["yb_b"], da2_b, "tn", name=f"dw_ret_proj_{tag}")
    dzq, dzk, dzv, dzg = _ret_bwd(z["bq"], z["bk"], z["c"], z["d"], sv["o_ret"], sv["states"], dyb, rc)
    dq_l, dk_l, dv_l, dbias_l = [], [], [], []
    for g, (_, dil) in enumerate(ATTN_GROUPS):
        qa, ka, va, cb, offs = sv["views"][g]
        rows = s // dil
        view = lambda t: t.reshape(rows, dil * ATTN_OUT)
        dq, dk, dv, dbg = _attn_bwd(qa, ka, va, cb, offs, bias, view(dya), view(sv["ya"]), view(sv["lt"]), g, dil)
        dq_l.append(dq.reshape(s, ATTN_OUT))
        dk_l.append(dk.reshape(s, ATTN_OUT))
        dv_l.append(dv.reshape(s, ATTN_OUT))
        dbias_l.append(dbg)
    dz = {"a": jnp.concatenate(dq_l + dk_l + dv_l, axis=1), "bq": dzq, "bk": dzk, "c": dzv, "d": dzg, "e": dze}
    dx = dres
    dw_rows, db_cols = [], []
    for name, off, width, _ in SECTIONS:
        dx = _matmul(dz[name], w["w_in"][off:off + width], "nn", addend=dx, name=f"dx_in_proj_{name}_{tag}")
        dw, cs = _matmul(dz[name], sv["x"], "tn", colsum=True, name=f"dw_in_proj_{name}_{tag}")
        dw_rows.append(dw)
        db_cols.append(cs)
    gw["w_in"] = jnp.concatenate(dw_rows, axis=0)
    small = dict(b_in=jnp.concatenate(db_cols, axis=1), ln1_g=dg1, ln1_b=dbt1, ln2_g=dg2, ln2_b=dbt2)
    return dx, gw, small, jnp.concatenate(dbias_l, axis=0)


def _forward_backward(x, target, rel_bias, layers, lnps):
    s = x.shape[0]
    bmaps = _bucket_maps()
    bias = _bias_tiles(rel_bias, bmaps)
    rc = _ret_consts(s)
    saved = []
    h = x
    for l in range(DEPTH):
        h, sv = _layer_fwd(h, layers[l], lnps[l], bias, rc, f"l{l}")
        saved.append(sv)
    dh, loss_part = _loss_head(h, target)
    gws, smalls, dbiases = [None] * DEPTH, [None] * DEPTH, [None] * DEPTH
    for l in reversed(range(DEPTH)):
        dh, gws[l], smalls[l], dbiases[l] = _layer_bwd(dh, layers[l], lnps[l], saved[l], bias, rc, f"l{l}")
    d_rel_bias = _bias_tiles_bwd(dbiases, bmaps)
    return loss_part, dh, gws, smalls, d_rel_bias


SMALL_NAMES = ("rel_bias", "b_in", "ln1_g", "ln1_b", "ln2_g", "ln2_b")
SMALL_ROWS = 32


def _pack_small(vals):
    flat = jnp.concatenate([vals[n].reshape(-1) for n in SMALL_NAMES])
    return jnp.pad(flat, (0, SMALL_ROWS * D_MODEL - flat.shape[0])).reshape(SMALL_ROWS, D_MODEL)


def _unpack_small(packed, like):
    flat = packed.reshape(-1)
    out, off = {}, 0
    for n in SMALL_NAMES:
        size = like[n].size
        out[n] = flat[off:off + size].reshape(like[n].shape)
        off += size
    return out


def kernel(x, rel_bias, w_in, b_in, w_attn_proj, w_ret_proj, w_out, ln1_g, ln1_b, w_ffn_gate, w_ffn_up, w_ffn_down, ln2_g, ln2_b, loss_target, m_rel_bias, m_w_in, m_b_in, m_w_attn_proj, m_w_ret_proj, m_w_out, m_ln1_g, m_ln1_b, m_w_ffn_gate, m_w_ffn_up, m_w_ffn_down, m_ln2_g, m_ln2_b, v_rel_bias, v_w_in, v_b_in, v_w_attn_proj, v_w_ret_proj, v_w_out, v_ln1_g, v_ln1_b, v_w_ffn_gate, v_w_ffn_up, v_w_ffn_down, v_ln2_g, v_ln2_b):
    big = dict(w_in=(w_in, m_w_in, v_w_in), w_attn_proj=(w_attn_proj, m_w_attn_proj, v_w_attn_proj),
               w_ret_proj=(w_ret_proj, m_w_ret_proj, v_w_ret_proj), w_out=(w_out, m_w_out, v_w_out),
               w_ffn_gate=(w_ffn_gate, m_w_ffn_gate, v_w_ffn_gate), w_ffn_up=(w_ffn_up, m_w_ffn_up, v_w_ffn_up),
               w_ffn_down=(w_ffn_down, m_w_ffn_down, v_w_ffn_down))
    small_w = dict(rel_bias=rel_bias, b_in=b_in, ln1_g=ln1_g, ln1_b=ln1_b, ln2_g=ln2_g, ln2_b=ln2_b)
    small_m = dict(rel_bias=m_rel_bias, b_in=m_b_in, ln1_g=m_ln1_g, ln1_b=m_ln1_b, ln2_g=m_ln2_g, ln2_b=m_ln2_b)
    small_v = dict(rel_bias=v_rel_bias, b_in=v_b_in, ln1_g=v_ln1_g, ln1_b=v_ln1_b, ln2_g=v_ln2_g, ln2_b=v_ln2_b)

    place = jnp.stack([2 * lax.axis_index("x") + lax.axis_index("y"), lax.axis_index("c")]).astype(jnp.int32)

    flats = [_pack_shards(l, w_in, w_attn_proj, w_ret_proj, w_out, w_ffn_gate, w_ffn_up, w_ffn_down, MXU_DTYPE)
             for l in range(DEPTH)]
    layers = [_unpack_gathered(_all_gather_rows(flats[0]))]
    layers += [_unpack_gathered(_all_gather_rows_behind(f)) for f in flats[1:]]
    lnps = [dict(b_in=b_in[l][None], ln1_g=ln1_g[l][None], ln1_b=ln1_b[l][None], ln2_g=ln2_g[l][None],
                 ln2_b=ln2_b[l][None]) for l in range(DEPTH)]

    loss_part, dx, gws, smalls, d_rel_bias = _forward_backward(x[0], loss_target[0], rel_bias, layers, lnps)
    loss = lax.psum(loss_part, ("x", "y", "c"))

    gall = _pack_full_grads(gws)
    got = _swap_halves(gall)
    parts = _scatter_to_chips(_add_own_half(gall, got, place))
    grads = _unpack_shard_grads(_share_with_sibling(_add_chip_parts(gall, got, parts, place)))

    small_g = dict(rel_bias=d_rel_bias)
    for n in SMALL_NAMES[1:]:
        small_g[n] = jnp.concatenate([smalls[l][n] for l in range(DEPTH)], axis=0)
    small_g = _unpack_small(_all_reduce_small(_pack_small(small_g)), small_w)

    delta, new_m, new_v = {}, {}, {}
    for n, (w, m, v) in big.items():
        turn = (lambda t: jnp.swapaxes(t, 1, 2)) if n in COLUMN_SHARDED else (lambda t: t)
        two_d = lambda t: t.reshape(-1, t.shape[-1])
        g = grads[n]
        d_, m_, v_ = _adamw(two_d(turn(w)), two_d(g), two_d(turn(m)), two_d(turn(v)), name=f"adamw_{n}")
        grads[n], delta[n], new_m[n], new_v[n] = (turn(t.reshape(g.shape)) for t in (g, d_, m_, v_))
    d_, m_, v_ = _adamw(_pack_small(small_w), _pack_small(small_g), _pack_small(small_m), _pack_small(small_v),
                        name="adamw_small")
    delta.update(_unpack_small(d_, small_w))
    new_m.update(_unpack_small(m_, small_w))
    new_v.update(_unpack_small(v_, small_w))
    grads.update(small_g)

    order = ("rel_bias", "w_in", "b_in", "w_attn_proj", "w_ret_proj", "w_out", "ln1_g", "ln1_b", "w_ffn_gate",
             "w_ffn_up", "w_ffn_down", "ln2_g", "ln2_b")
    return (loss, dx[None], *[grads[n] for n in order], *[delta[n] for n in order], *[new_m[n] for n in order],
            *[new_v[n] for n in order])
```

```python
import functools

import numpy as np
import jax
import jax.numpy as jnp
from jax import lax
from jax.experimental import pallas as pl
from jax.experimental.pallas import tpu as pltpu
from jax.experimental.pallas import tpu_sc as plsc

F32 = jnp.float32
MXU_DTYPE = jnp.bfloat16

DEPTH = 2
D_MODEL = 1024
HEAD_DIM = 64
ATTN_GROUPS = ((128, 1), (512, 4), (2048, 16))
N_GROUPS = len(ATTN_GROUPS)
HEADS_PER_GROUP = 6
N_ATTN_HEADS = 18
ATTN_WIDTH = 1152
ATTN_OUT = 384
NUM_BUCKETS = 32
MAX_DISTANCE = 2048
RET_HEADS = 4
RET_QK = 256
RET_V = 512
RET_CHUNK = 128
ROPE_BASE = 10000.0
D_FF = 2816
IN_COLS = 11648
ALPHA = (2 * DEPTH) ** 0.25
LN_EPS = 1e-5
GN_EPS = 1e-5
ADAM_LR, ADAM_B1, ADAM_B2, ADAM_EPS, ADAM_WD, ADAM_STEP = 0.001, 0.9, 0.999, 1e-08, 0.01, 10

BLK = 128
NEG = -1e30
N_CHIPS = 4
VMEM_LIMIT = 48 * 1024 * 1024

SECTIONS = (
    ("a", 0, 3456, True),
    ("bq", 3456, 1024, False),
    ("bk", 4480, 1024, False),
    ("c", 5504, 2048, True),
    ("d", 7552, 2048, False),
    ("e", 9600, 2048, False),
)
PACK_ROWS = (("w_in", 2912), ("w_attn_proj", 96), ("w_ret_proj", 512), ("w_out", 256),
             ("w_ffn_gate", 704), ("w_ffn_up", 704), ("w_ffn_down", 704))
ROWS_PER_LAYER = sum(r for _, r in PACK_ROWS)

NN = ((1,), (0,))
NT = ((1,), (1,))
TN = ((0,), (0,))


def _dot(a, b, dims):
    return lax.dot_general(a.astype(MXU_DTYPE), b.astype(MXU_DTYPE), (dims, ((), ())),
                           preferred_element_type=F32)


def _pick(n, prefs):
    for p in prefs:
        if n % p == 0:
            return p
    raise ValueError(f"no tile for {n} among {prefs}")


TOKEN_TILES = (1024, 512, 256, 128)
FEATURE_TILES = (1152, 1024, 1408, 384, 256, 128)


def _params(n_axes, limit=VMEM_LIMIT):
    return pltpu.CompilerParams(dimension_semantics=("arbitrary",) * n_axes, vmem_limit_bytes=limit)


def _resident(a):
    return pl.BlockSpec(a.shape, lambda i: (0,) * a.ndim, pipeline_mode=pl.Buffered(1))


def _sigmoid(x):
    return 1.0 / (1.0 + jnp.exp(-x))


def _norm_rows(u, eps):
    mu = jnp.mean(u, axis=-1, keepdims=True)
    xc = u - mu
    var = jnp.mean(xc * xc, axis=-1, keepdims=True)
    rstd = lax.rsqrt(var + eps)
    return xc * rstd, rstd


def _norm_rows_bwd(dxh, xh, rstd):
    c1 = jnp.mean(dxh, axis=-1, keepdims=True)
    c2 = jnp.mean(dxh * xh, axis=-1, keepdims=True)
    return rstd * (dxh - c1 - xh * c2)


def _matmul(a, b, mode, *, name, out_dtype=F32, bias=None, addend=None, colsum=False):
    if mode == "tn":
        kd, m = a.shape
        n = b.shape[1]
        tm, tn_, tk = _pick(m, FEATURE_TILES), _pick(n, FEATURE_TILES), _pick(kd, TOKEN_TILES)
        a_spec = pl.BlockSpec((tk, tm), lambda i, j, k: (k, i))
        b_spec = pl.BlockSpec((tk, tn_), lambda i, j, k: (k, j))
    else:
        m, kd = a.shape
        n = b.shape[0] if mode == "nt" else b.shape[1]
        tm, tn_, tk = _pick(m, TOKEN_TILES), _pick(n, FEATURE_TILES), _pick(kd, FEATURE_TILES)
        a_spec = pl.BlockSpec((tm, tk), lambda i, j, k: (i, k))
        if mode == "nt":
            b_spec = pl.BlockSpec((tn_, tk), lambda i, j, k: (j, k))
        else:
            b_spec = pl.BlockSpec((tk, tn_), lambda i, j, k: (k, j))
    dims = {"nn": NN, "nt": NT, "tn": TN}[mode]
    nk = kd // tk
    has_bias, has_add = bias is not None, addend is not None
    assert not colsum or mode == "tn"

    def body(*refs):
        it = iter(refs)
        a_ref, b_ref = next(it), next(it)
        bias_ref = next(it) if has_bias else None
        add_ref = next(it) if has_add else None
        o_ref = next(it)
        cs_ref = next(it) if colsum else None
        acc_ref = next(it)
        j, k = pl.program_id(1), pl.program_id(2)

        @pl.when(k == 0)
        def _():
            if has_add:
                acc_ref[...] = add_ref[...].astype(F32)
            else:
                acc_ref[...] = jnp.zeros_like(acc_ref)

        av = a_ref[...]
        acc_ref[...] += _dot(av, b_ref[...], dims)
        if colsum:
            @pl.when(jnp.logical_and(j == 0, k == 0))
            def _():
                cs_ref[...] = jnp.zeros_like(cs_ref)

            @pl.when(j == 0)
            def _():
                cs_ref[...] += jnp.sum(av.astype(F32), axis=0, keepdims=True)

        @pl.when(k == nk - 1)
        def _():
            r = acc_ref[...]
            if has_bias:
                r = r + bias_ref[...]
            o_ref[...] = r.astype(out_dtype)

    in_specs, args = [a_spec, b_spec], [a, b]
    if has_bias:
        in_specs.append(pl.BlockSpec((1, tn_), lambda i, j, k: (0, j)))
        args.append(bias)
    if has_add:
        in_specs.append(pl.BlockSpec((tm, tn_), lambda i, j, k: (i, j)))
        args.append(addend)
    out_shape = [jax.ShapeDtypeStruct((m, n), out_dtype)]
    out_specs = [pl.BlockSpec((tm, tn_), lambda i, j, k: (i, j))]
    if colsum:
        out_shape.append(jax.ShapeDtypeStruct((1, m), F32))
        out_specs.append(pl.BlockSpec((1, tm), lambda i, j, k: (0, i)))
    res = pl.pallas_call(
        body, out_shape=out_shape, grid=(m // tm, n // tn_, nk), in_specs=in_specs, out_specs=out_specs,
        scratch_shapes=[pltpu.VMEM((tm, tn_), F32)], compiler_params=_params(3), name=name,
    )(*args)
    return res if colsum else res[0]


def _t5_bucket(dist):
    max_exact = NUM_BUCKETS // 2
    large = max_exact + (np.log(np.maximum(dist, max_exact) / max_exact)
                         / np.log(MAX_DISTANCE / max_exact) * (NUM_BUCKETS - max_exact)).astype(np.int32)
    large = np.minimum(large, NUM_BUCKETS - 1)
    return np.where(dist < max_exact, dist, large).astype(np.int32)


def _bucket_maps():
    qi = np.arange(BLK)[:, None]
    kj = np.arange(2 * BLK)[None, :]
    rel = np.clip(qi + BLK - kj, 0, BLK)
    return jnp.asarray(np.stack([_t5_bucket(rel * d) for _, d in ATTN_GROUPS]))


def _bias_tiles(rel_bias, bmaps):
    def body(tab_ref, bm_ref, o_ref):
        h = pl.program_id(0)
        bm = bm_ref[0]
        acc = jnp.zeros((BLK, 2 * BLK), F32)
        for b in range(NUM_BUCKETS):
            acc = jnp.where(bm == b, tab_ref[b, h], acc)
        o_ref[0] = acc

    return pl.pallas_call(
        body, out_shape=jax.ShapeDtypeStruct((N_ATTN_HEADS, BLK, 2 * BLK), F32), grid=(N_ATTN_HEADS,),
        in_specs=[pl.BlockSpec(memory_space=pltpu.SMEM),
                  pl.BlockSpec((1, BLK, 2 * BLK), lambda h: (h // HEADS_PER_GROUP, 0, 0))],
        out_specs=pl.BlockSpec((1, BLK, 2 * BLK), lambda h: (h, 0, 0)),
        compiler_params=_params(1), name="bias_tiles",
    )(rel_bias, bmaps)


def _bias_tiles_bwd(dbias_layers, bmaps):
    nl = len(dbias_layers)

    def body(*refs):
        bm = refs[nl][0]
        o_ref = refs[nl + 1]
        x = refs[0][0]
        for r in refs[1:nl]:
            x = x + r[0]
        lane = lax.broadcasted_iota(jnp.int32, (1, BLK), 1)
        row = jnp.zeros((1, BLK), F32)
        for b in range(NUM_BUCKETS):
            s = jnp.sum(jnp.where(bm == b, x, 0.0), axis=1, keepdims=True)
            s = jnp.sum(s, axis=0, keepdims=True)
            row = jnp.where(lane == b, s, row)
        o_ref[0] = row

    tile = pl.BlockSpec((1, BLK, 2 * BLK), lambda h: (h, 0, 0))
    out = pl.pallas_call(
        body, out_shape=jax.ShapeDtypeStruct((N_ATTN_HEADS, 1, BLK), F32), grid=(N_ATTN_HEADS,),
        in_specs=[tile] * nl + [pl.BlockSpec((1, BLK, 2 * BLK), lambda h: (h // HEADS_PER_GROUP, 0, 0))],
        out_specs=pl.BlockSpec((1, 1, BLK), lambda h: (h, 0, 0)),
        compiler_params=_params(1), name="bias_tiles_bwd",
    )(*dbias_layers, bmaps)
    return out[:, 0, :NUM_BUCKETS].T


def _pair_masks():
    lane = lax.broadcasted_iota(jnp.int32, (BLK, BLK), 1)
    row2 = lax.broadcasted_iota(jnp.int32, (2 * BLK, BLK), 0)
    lane2 = lax.broadcasted_iota(jnp.int32, (2 * BLK, BLK), 1)
    own = (lane2 // HEAD_DIM) == (row2 // BLK)
    qi = lax.broadcasted_iota(jnp.int32, (2 * BLK, 2 * BLK), 0) & (BLK - 1)
    kj = lax.broadcasted_iota(jnp.int32, (2 * BLK, 2 * BLK), 1)
    band = jnp.logical_and(kj >= qi, kj <= qi + BLK)
    return lane < HEAD_DIM, own, band, kj < BLK


def _pair_scores(q32, kb, bias2, own, band, is_prev, pen):
    qm = jnp.where(own, jnp.concatenate([q32, q32], axis=0), 0.0)
    s = _dot(qm, kb, NT) * (HEAD_DIM ** -0.5) + bias2
    if pen is not None:
        s = s + jnp.where(is_prev, pen, 0.0)
    return jnp.where(band, s, NEG), qm


HEAD_PAIRS = HEADS_PER_GROUP // 2


def _attn_fwd(qa, ka, va, cb, offs, bias, g, dil):
    rows = qa.shape[0]
    nb = rows // BLK
    rb = 2 if nb % 2 == 0 else 1
    oq, ok, ov = offs

    def cur(off):
        return pl.BlockSpec((rb * BLK, ATTN_OUT), lambda r, n: (n, r * cb + off))

    def prev(off):
        return pl.BlockSpec((BLK, ATTN_OUT), lambda r, n: (jnp.maximum(rb * n - 1, 0), r * cb + off))

    def body(q_ref, kp_ref, kc_ref, vp_ref, vc_ref, b_ref, o_ref, l_ref):
        n = pl.program_id(1)
        pen0 = jnp.where(n > 0, 0.0, NEG)
        first, own, band, is_prev = _pair_masks()
        o_rows, l_rows = [], []
        for t in range(rb):
            rows_t = slice(t * BLK, (t + 1) * BLK)
            o_parts, l_parts = [], []
            for hp in range(HEAD_PAIRS):
                cols = slice(hp * BLK, (hp + 1) * BLK)
                if t == 0:
                    kp, vp, pen = kp_ref[:, cols], vp_ref[:, cols], pen0
                else:
                    before = slice((t - 1) * BLK, t * BLK)
                    kp, vp, pen = kc_ref[before, cols], vc_ref[before, cols], None
                kb = jnp.concatenate([kp, kc_ref[rows_t, cols]], axis=0)
                vb = jnp.concatenate([vp, vc_ref[rows_t, cols]], axis=0)
                bias2 = jnp.concatenate([b_ref[2 * hp], b_ref[2 * hp + 1]], axis=0)
                s, _ = _pair_scores(q_ref[rows_t, cols].astype(F32), kb, bias2, own, band, is_prev, pen)
                m = jnp.max(s, axis=1, keepdims=True)
                p = jnp.exp(s - m)
                l = jnp.sum(p, axis=1, keepdims=True)
                o2 = _dot(p * (1.0 / l), vb, NN)
                lse2 = m + jnp.log(l)
                o_parts.append(jnp.where(first, o2[:BLK], o2[BLK:]))
                l_parts.append(jnp.where(first, lse2[:BLK], lse2[BLK:]))
            o_rows.append(jnp.concatenate(o_parts, axis=1))
            l_rows.append(jnp.concatenate(l_parts, axis=1))
        o_ref[...] = jnp.concatenate(o_rows, axis=0)
        l_ref[...] = jnp.concatenate(l_rows, axis=0)

    out_spec = pl.BlockSpec((rb * BLK, ATTN_OUT), lambda r, n: (n, r))
    shape = jax.ShapeDtypeStruct((rows, dil * ATTN_OUT), F32)
    return pl.pallas_call(
        body, out_shape=[shape, shape], grid=(dil, nb // rb),
        in_specs=[cur(oq), prev(ok), cur(ok), prev(ov), cur(ov),
                  pl.BlockSpec((HEADS_PER_GROUP, BLK, 2 * BLK), lambda r, n: (g, 0, 0))],
        out_specs=[out_spec, out_spec], compiler_params=_params(2), name=f"attn_fwd_g{g}",
    )(qa, ka, ka, va, va, bias)


def _attn_bwd(qa, ka, va, cb, offs, bias, dy, ya, lt, g, dil):
    rows = qa.shape[0]
    nb = rows // BLK
    oq, ok, ov = offs

    def cur(off, c):
        return pl.BlockSpec((BLK, ATTN_OUT), lambda r, n: (jnp.minimum(n, nb - 1), r * c + off))

    def prev(off, c):
        return pl.BlockSpec((BLK, ATTN_OUT), lambda r, n: (jnp.clip(n - 1, 0, nb - 1), r * c + off))

    def body(q_ref, kp_ref, kc_ref, vp_ref, vc_ref, b_ref, dy_ref, ya_ref, lt_ref,
             dq_ref, dk_ref, dv_ref, db_ref, ck_ref, cv_ref):
        r, n = pl.program_id(0), pl.program_id(1)

        @pl.when(jnp.logical_and(r == 0, n == 0))
        def _():
            db_ref[...] = jnp.zeros_like(db_ref)

        @pl.when(n == 0)
        def _():
            ck_ref[...] = jnp.zeros_like(ck_ref)
            cv_ref[...] = jnp.zeros_like(cv_ref)

        @pl.when(n < nb)
        def _():
            pen = jnp.where(n > 0, 0.0, NEG)
            first, own, band, is_prev = _pair_masks()
            second = jnp.logical_not(first)
            scale = HEAD_DIM ** -0.5
            parts = {k: [] for k in ("dq", "dkp", "dkc", "dvp", "dvc")}
            db_parts = []
            for hp in range(HEAD_PAIRS):
                cols = slice(hp * BLK, (hp + 1) * BLK)
                kb = jnp.concatenate([kp_ref[:, cols], kc_ref[:, cols]], axis=0)
                vb = jnp.concatenate([vp_ref[:, cols], vc_ref[:, cols]], axis=0)
                bias2 = jnp.concatenate([b_ref[2 * hp], b_ref[2 * hp + 1]], axis=0)
                dy_, lt_ = dy_ref[:, cols], lt_ref[:, cols]
                dyy = dy_ * ya_ref[:, cols]
                per_head = lambda t, red, fill: jnp.concatenate(
                    [red(jnp.where(first, t, fill), axis=1, keepdims=True),
                     red(jnp.where(second, t, fill), axis=1, keepdims=True)], axis=0)
                lse2 = per_head(lt_, jnp.max, NEG)
                delta2 = per_head(dyy, jnp.sum, 0.0)
                s, qm = _pair_scores(q_ref[:, cols].astype(F32), kb, bias2, own, band, is_prev, pen)
                p = jnp.exp(s - lse2)
                dym = jnp.where(own, jnp.concatenate([dy_, dy_], axis=0), 0.0)
                ds = p * (_dot(dym, vb, NT) - delta2)
                db_parts += [ds[:BLK], ds[BLK:]]
                dq2 = _dot(ds, kb, NN) * scale
                dkb = _dot(ds, qm, TN) * scale
                dvb = _dot(p, dym, TN)
                for k, val in (("dq", jnp.where(first, dq2[:BLK], dq2[BLK:])), ("dkp", dkb[:BLK]),
                               ("dkc", dkb[BLK:]), ("dvp", dvb[:BLK]), ("dvc", dvb[BLK:])):
                    parts[k].append(val)
            wide = {k: jnp.concatenate(val, axis=1) for k, val in parts.items()}
            db_ref[...] += jnp.stack(db_parts, axis=0)
            dq_ref[...] = wide["dq"].astype(dq_ref.dtype)
            dk_ref[...] = (ck_ref[...] + wide["dkp"]).astype(dk_ref.dtype)
            dv_ref[...] = (cv_ref[...] + wide["dvp"]).astype(dv_ref.dtype)
            ck_ref[...] = wide["dkc"]
            cv_ref[...] = wide["dvc"]

        @pl.when(n == nb)
        def _():
            dk_ref[...] = ck_ref[...].astype(dk_ref.dtype)
            dv_ref[...] = cv_ref[...].astype(dv_ref.dtype)

    late = pl.BlockSpec((BLK, ATTN_OUT), lambda r, n: (jnp.maximum(n - 1, 0), r))
    shape = jax.ShapeDtypeStruct((rows, dil * ATTN_OUT), MXU_DTYPE)
    return pl.pallas_call(
        body,
        out_shape=[shape, shape, shape, jax.ShapeDtypeStruct((HEADS_PER_GROUP, BLK, 2 * BLK), F32)],
        grid=(dil, nb + 1),
        in_specs=[cur(oq, cb), prev(ok, cb), cur(ok, cb), prev(ov, cb), cur(ov, cb),
                  pl.BlockSpec((HEADS_PER_GROUP, BLK, 2 * BLK), lambda r, n: (g, 0, 0)),
                  cur(0, 1), cur(0, 1), cur(0, 1)],
        out_specs=[cur(0, 1), late, late, pl.BlockSpec((HEADS_PER_GROUP, BLK, 2 * BLK), lambda r, n: (0, 0, 0))],
        scratch_shapes=[pltpu.VMEM((BLK, ATTN_OUT), F32), pltpu.VMEM((BLK, ATTN_OUT), F32)],
        compiler_params=_params(2), name=f"attn_bwd_g{g}",
    )(qa, ka, ka, va, va, bias, dy, ya, lt)


def _attn_combine(o_list, l_list):
    s = o_list[0].shape[0]
    tr = _pick(s, (1024, 512, 256, 128))

    def body(o0, o1, o2, l0, l1, l2, y_ref, yb_ref, lt_ref):
        a0, a1, a2 = l0[...], l1[...], l2[...]
        mx = jnp.maximum(jnp.maximum(a0, a1), a2)
        e0, e1, e2 = jnp.exp(a0 - mx), jnp.exp(a1 - mx), jnp.exp(a2 - mx)
        den = e0 + e1 + e2
        inv = 1.0 / den
        y = (e0 * inv) * o0[...] + (e1 * inv) * o1[...] + (e2 * inv) * o2[...]
        y_ref[...] = y
        yb_ref[...] = y.astype(yb_ref.dtype)
        lt_ref[...] = mx + jnp.log(den)

    spec = pl.BlockSpec((tr, ATTN_OUT), lambda i: (i, 0))
    f = jax.ShapeDtypeStruct((s, ATTN_OUT), F32)
    return pl.pallas_call(
        body, out_shape=[f, jax.ShapeDtypeStruct((s, ATTN_OUT), MXU_DTYPE), f], grid=(s // tr,),
        in_specs=[spec] * 6, out_specs=[spec] * 3, compiler_params=_params(1), name="attn_combine",
    )(*o_list, *l_list)


def _ret_consts(s):
    half = RET_QK // 2
    pos = jnp.arange(s, dtype=F32)
    inv_freq = ROPE_BASE ** (-jnp.arange(half, dtype=F32) / half)
    ang = pos[:, None] * inv_freq[None]
    log_g = jnp.log(1.0 - 2.0 ** (-5.0 - jnp.arange(RET_HEADS, dtype=F32)))
    n = jnp.arange(RET_CHUNK, dtype=F32)
    diff = n[:, None] - n[None, :]
    dmask = jnp.where(diff >= 0, jnp.exp(log_g[:, None, None] * jnp.maximum(diff, 0.0)), 0.0)
    qdec = jnp.exp(log_g[:, None] * (n + 1.0))
    kdec = jnp.exp(log_g[:, None] * (RET_CHUNK - 1.0 - n))
    cdec = jnp.exp(log_g * RET_CHUNK)
    wide = (RET_HEADS, RET_CHUNK, RET_QK)
    return dict(cos=jnp.cos(ang), sin=jnp.sin(ang), dmask=dmask,
                qdec=jnp.broadcast_to(qdec[:, :, None], wide), kdec=jnp.broadcast_to(kdec[:, :, None], wide),
                cdec=cdec)


def _rot(t, cs, sn):
    half = RET_QK // 2
    t1, t2 = t[:, :half], t[:, half:]
    return jnp.concatenate([t1 * cs - t2 * sn, t1 * sn + t2 * cs], axis=1)


def _rot_bwd(d, cs, sn):
    half = RET_QK // 2
    d1, d2 = d[:, :half], d[:, half:]
    return jnp.concatenate([d1 * cs + d2 * sn, d2 * cs - d1 * sn], axis=1)


def _ret_fwd(zq, zk, zv, zg, rc):
    s = zq.shape[0]
    nc = s // RET_CHUNK
    c = RET_CHUNK

    def body(cd_ref, q_ref, k_ref, v_ref, g_ref, cos_ref, sin_ref, dm_ref, qd_ref, kd_ref,
             yb_ref, o_ref, st_ref, state):
        n = pl.program_id(0)

        @pl.when(n == 0)
        def _():
            state[...] = jnp.zeros_like(state)

        cs, sn = cos_ref[...], sin_ref[...]
        for h in range(RET_HEADS):
            qs, vs = slice(h * RET_QK, (h + 1) * RET_QK), slice(h * RET_V, (h + 1) * RET_V)
            qr = _rot(q_ref[:, qs], cs, sn)
            kr = _rot(k_ref[:, qs], cs, sn) * (RET_QK ** -0.5)
            v = v_ref[:, vs]
            st = state[h]
            st_ref[h, 0] = st.astype(st_ref.dtype)
            sc = _dot(qr, kr, NT) * dm_ref[h]
            o = _dot(sc, v, NN) + _dot(qr * qd_ref[h], st, NN)
            state[h] = st * cd_ref[h] + _dot(kr * kd_ref[h], v, TN)
            o_ref[:, vs] = o
            xh, _ = _norm_rows(o, GN_EPS)
            gv = g_ref[:, vs]
            yb_ref[:, vs] = (gv * _sigmoid(gv) * xh).astype(yb_ref.dtype)

    row = lambda w: pl.BlockSpec((c, w), lambda n: (n, 0))
    const = lambda a: pl.BlockSpec(a.shape, lambda n: (0, 0, 0))
    return pl.pallas_call(
        body,
        out_shape=[jax.ShapeDtypeStruct((s, RET_HEADS * RET_V), MXU_DTYPE),
                   jax.ShapeDtypeStruct((s, RET_HEADS * RET_V), F32),
                   jax.ShapeDtypeStruct((RET_HEADS, nc, RET_QK, RET_V), MXU_DTYPE)],
        grid=(nc,),
        in_specs=[pl.BlockSpec(memory_space=pltpu.SMEM), row(RET_HEADS * RET_QK), row(RET_HEADS * RET_QK),
                  row(RET_HEADS * RET_V), row(RET_HEADS * RET_V), row(RET_QK // 2), row(RET_QK // 2),
                  const(rc["dmask"]), const(rc["qdec"]), const(rc["kdec"])],
        out_specs=[row(RET_HEADS * RET_V), row(RET_HEADS * RET_V),
                   pl.BlockSpec((RET_HEADS, 1, RET_QK, RET_V), lambda n: (0, n, 0, 0))],
        scratch_shapes=[pltpu.VMEM((RET_HEADS, RET_QK, RET_V), F32)],
        compiler_params=_params(1), name="ret_fwd",
    )(rc["cdec"], zq, zk, zv, zg, rc["cos"], rc["sin"], rc["dmask"], rc["qdec"], rc["kdec"])


def _ret_bwd(zq, zk, zv, zg, o_ret, states, dyb, rc):
    s = zq.shape[0]
    nc = s // RET_CHUNK
    c = RET_CHUNK

    def body(cd_ref, q_ref, k_ref, v_ref, g_ref, o_ref, st_ref, dy_ref, cos_ref, sin_ref, dm_ref, qd_ref,
             kd_ref, dq_ref, dk_ref, dv_ref, dg_ref, dstate):
        n = pl.program_id(0)

        @pl.when(n == 0)
        def _():
            dstate[...] = jnp.zeros_like(dstate)

        cs, sn = cos_ref[...], sin_ref[...]
        for h in range(RET_HEADS):
            qs, vs = slice(h * RET_QK, (h + 1) * RET_QK), slice(h * RET_V, (h + 1) * RET_V)
            qr = _rot(q_ref[:, qs], cs, sn)
            kr = _rot(k_ref[:, qs], cs, sn) * (RET_QK ** -0.5)
            v = v_ref[:, vs]
            st = st_ref[h, 0]
            dm, qd, kd = dm_ref[h], qd_ref[h], kd_ref[h]
            xh, rstd = _norm_rows(o_ref[:, vs], GN_EPS)
            gv, dy = g_ref[:, vs], dy_ref[:, vs]
            sg = _sigmoid(gv)
            dg_ref[:, vs] = dy * xh * (sg * (1.0 + gv * (1.0 - sg)))
            do = _norm_rows_bwd(dy * (gv * sg), xh, rstd)
            ds_next = dstate[h]
            sc = _dot(qr, kr, NT) * dm
            da = _dot(do, v, NT) * dm
            dv = _dot(sc, do, TN) + _dot(kr * kd, ds_next, NN)
            dqr = _dot(da, kr, NN) + _dot(do, st, NT) * qd
            dkr = _dot(da, qr, TN) + _dot(v, ds_next, NT) * kd
            dstate[h] = ds_next * cd_ref[h] + _dot(qr * qd, do, TN)
            dq_ref[:, qs] = _rot_bwd(dqr, cs, sn)
            dk_ref[:, qs] = _rot_bwd(dkr * (RET_QK ** -0.5), cs, sn)
            dv_ref[:, vs] = dv.astype(dv_ref.dtype)

    row = lambda w: pl.BlockSpec((c, w), lambda n: (nc - 1 - n, 0))
    const = lambda a: pl.BlockSpec(a.shape, lambda n: (0, 0, 0))
    f_qk = jax.ShapeDtypeStruct((s, RET_HEADS * RET_QK), F32)
    qk_w, v_w = RET_HEADS * RET_QK, RET_HEADS * RET_V
    return pl.pallas_call(
        body,
        out_shape=[f_qk, f_qk, jax.ShapeDtypeStruct((s, v_w), MXU_DTYPE), jax.ShapeDtypeStruct((s, v_w), F32)],
        grid=(nc,),
        in_specs=[pl.BlockSpec(memory_space=pltpu.SMEM), row(qk_w), row(qk_w), row(v_w), row(v_w), row(v_w),
                  pl.BlockSpec((RET_HEADS, 1, RET_QK, RET_V), lambda n: (0, nc - 1 - n, 0, 0)), row(v_w),
                  row(RET_QK // 2), row(RET_QK // 2), const(rc["dmask"]), const(rc["qdec"]), const(rc["kdec"])],
        out_specs=[row(qk_w), row(qk_w), row(v_w), row(v_w)],
        scratch_shapes=[pltpu.VMEM((RET_HEADS, RET_QK, RET_V), F32)],
        compiler_params=_params(1), name="ret_bwd",
    )(rc["cdec"], zq, zk, zv, zg, o_ret, states, dyb, rc["cos"], rc["sin"], rc["dmask"], rc["qdec"],
      rc["kdec"])


ROW_TILE = 256


def _mix_fwd(x, ya_b, yb_b, ze, wap_t, wrp, wout, gam, bet):
    s = x.shape[0]
    tm = ROW_TILE

    def body(x_ref, ya_ref, yb_ref, ze_ref, wap_ref, wrp_ref, wout_ref, g_ref, b_ref,
             x1_ref, u1_ref, a1_ref, a2_ref, mg_ref):
        a1 = _dot(ya_ref[...], wap_ref[...], NT)
        a2 = _dot(yb_ref[...], wrp_ref[...], NN)
        ze_ = ze_ref[...]
        merged = _sigmoid(ze_[:, :D_MODEL]) * a1 + _sigmoid(ze_[:, D_MODEL:]) * a2
        u = ALPHA * x_ref[...] + _dot(merged, wout_ref[...], NN)
        xh, _ = _norm_rows(u, LN_EPS)
        x1_ref[...] = xh * g_ref[...] + b_ref[...]
        u1_ref[...] = u
        a1_ref[...] = a1
        a2_ref[...] = a2
        mg_ref[...] = merged.astype(mg_ref.dtype)

    row = lambda w: pl.BlockSpec((tm, w), lambda i: (i, 0))
    full = _resident
    f = jax.ShapeDtypeStruct((s, D_MODEL), F32)
    return pl.pallas_call(
        body, out_shape=[f, f, f, f, jax.ShapeDtypeStruct((s, D_MODEL), MXU_DTYPE)], grid=(s // tm,),
        in_specs=[row(D_MODEL), row(ATTN_OUT), row(RET_HEADS * RET_V), row(2 * D_MODEL), full(wap_t), full(wrp),
                  full(wout), full(gam), full(bet)],
        out_specs=[row(D_MODEL)] * 5, compiler_params=_params(1), name="mix_fwd",
    )(x, ya_b, yb_b, ze, wap_t, wrp, wout, gam, bet)


def _mix_bwd(dx1, u1, a1, a2, ze, wap_t, wrp, wout, gam):
    s = dx1.shape[0]
    tm = ROW_TILE

    def body(dx_ref, u_ref, a1_ref, a2_ref, ze_ref, wap_ref, wrp_ref, wout_ref, g_ref,
             dres_ref, du_ref, da1_ref, da2_ref, dze_ref, dya_ref, dyb_ref, dgam_ref, dbet_ref):
        @pl.when(pl.program_id(0) == 0)
        def _():
            dgam_ref[...] = jnp.zeros_like(dgam_ref)
            dbet_ref[...] = jnp.zeros_like(dbet_ref)

        dx = dx_ref[...]
        xh, rstd = _norm_rows(u_ref[...], LN_EPS)
        dgam_ref[...] += jnp.sum(dx * xh, axis=0, keepdims=True)
        dbet_ref[...] += jnp.sum(dx, axis=0, keepdims=True)
        du = _norm_rows_bwd(dx * g_ref[...], xh, rstd)
        dres_ref[...] = ALPHA * du
        du_ref[...] = du.astype(du_ref.dtype)
        dm = _dot(du, wout_ref[...], NT)
        ze_ = ze_ref[...]
        sa, sb = _sigmoid(ze_[:, :D_MODEL]), _sigmoid(ze_[:, D_MODEL:])
        da1, da2 = dm * sa, dm * sb
        dze_ref[...] = jnp.concatenate([dm * a1_ref[...] * (sa * (1.0 - sa)),
                                        dm * a2_ref[...] * (sb * (1.0 - sb))], axis=1)
        da1_ref[...] = da1.astype(da1_ref.dtype)
        da2_ref[...] = da2.astype(da2_ref.dtype)
        dya_ref[...] = _dot(da1, wap_ref[...], NN)
        dyb_ref[...] = _dot(da2, wrp_ref[...], NT)

    row = lambda w: pl.BlockSpec((tm, w), lambda i: (i, 0))
    full = _resident
    vec = pl.BlockSpec((1, D_MODEL), lambda i: (0, 0))
    f = lambda w: jax.ShapeDtypeStruct((s, w), F32)
    m = lambda w: jax.ShapeDtypeStruct((s, w), MXU_DTYPE)
    v = jax.ShapeDtypeStruct((1, D_MODEL), F32)
    return pl.pallas_call(
        body,
        out_shape=[f(D_MODEL), m(D_MODEL), m(D_MODEL), m(D_MODEL), f(2 * D_MODEL), f(ATTN_OUT),
                   f(RET_HEADS * RET_V), v, v],
        grid=(s // tm,),
        in_specs=[row(D_MODEL)] * 4 + [row(2 * D_MODEL), full(wap_t), full(wrp), full(wout), full(gam)],
        out_specs=[row(D_MODEL)] * 4 + [row(2 * D_MODEL), row(ATTN_OUT), row(RET_HEADS * RET_V), vec, vec],
        compiler_params=_params(1), name="mix_bwd",
    )(dx1, u1, a1, a2, ze, wap_t, wrp, wout, gam)


FF_CHUNK = 1408


def _ffn_fwd(x1, wg_t, wu_t, wd, gam, bet):
    s = x1.shape[0]
    tm, fc = ROW_TILE, FF_CHUNK

    def body(x_ref, wg_ref, wu_ref, wd_ref, g_ref, b_ref, x2_ref, u2_ref, a_ref, b_out_ref, h_ref):
        xv = x_ref[...]
        u = ALPHA * xv
        for f0 in range(0, D_FF, fc):
            ch = slice(f0, f0 + fc)
            a = _dot(xv, wg_ref[ch, :], NT)
            b = _dot(xv, wu_ref[ch, :], NT)
            hid = a * _sigmoid(a) * b
            u = u + _dot(hid, wd_ref[ch, :], NN)
            a_ref[:, ch] = a
            b_out_ref[:, ch] = b
            h_ref[:, ch] = hid.astype(h_ref.dtype)
        xh, _ = _norm_rows(u, LN_EPS)
        u2_ref[...] = u
        x2_ref[...] = xh * g_ref[...] + b_ref[...]

    row = lambda w: pl.BlockSpec((tm, w), lambda i: (i, 0))
    f = lambda w: jax.ShapeDtypeStruct((s, w), F32)
    return pl.pallas_call(
        body, out_shape=[f(D_MODEL), f(D_MODEL), f(D_FF), f(D_FF), jax.ShapeDtypeStruct((s, D_FF), MXU_DTYPE)],
        grid=(s // tm,),
        in_specs=[row(D_MODEL), _resident(wg_t), _resident(wu_t), _resident(wd), _resident(gam), _resident(bet)],
        out_specs=[row(D_MODEL), row(D_MODEL), row(D_FF), row(D_FF), row(D_FF)],
        compiler_params=_params(1), name="ffn_fwd",
    )(x1, wg_t, wu_t, wd, gam, bet)


def _ffn_bwd(dx2, u2, fa, fb, wg_t, wu_t, wd, gam):
    s = dx2.shape[0]
    tm, fc = ROW_TILE, FF_CHUNK

    def body(dx_ref, u_ref, a_ref, b_ref, wg_ref, wu_ref, wd_ref, g_ref,
             dx1_ref, du_ref, da_ref, db_ref, dgam_ref, dbet_ref):
        @pl.when(pl.program_id(0) == 0)
        def _():
            dgam_ref[...] = jnp.zeros_like(dgam_ref)
            dbet_ref[...] = jnp.zeros_like(dbet_ref)

        dx = dx_ref[...]
        xh, rstd = _norm_rows(u_ref[...], LN_EPS)
        dgam_ref[...] += jnp.sum(dx * xh, axis=0, keepdims=True)
        dbet_ref[...] += jnp.sum(dx, axis=0, keepdims=True)
        du = _norm_rows_bwd(dx * g_ref[...], xh, rstd)
        du_ref[...] = du.astype(du_ref.dtype)
        acc = ALPHA * du
        for f0 in range(0, D_FF, fc):
            ch = slice(f0, f0 + fc)
            dh = _dot(du, wd_ref[ch, :], NT)
            a, b = a_ref[:, ch], b_ref[:, ch]
            sg = _sigmoid(a)
            da = dh * b * (sg * (1.0 + a * (1.0 - sg)))
            db = dh * (a * sg)
            acc = acc + _dot(da, wg_ref[ch, :], NN) + _dot(db, wu_ref[ch, :], NN)
            da_ref[:, ch] = da.astype(da_ref.dtype)
            db_ref[:, ch] = db.astype(db_ref.dtype)
        dx1_ref[...] = acc

    row = lambda w: pl.BlockSpec((tm, w), lambda i: (i, 0))
    vec = pl.BlockSpec((1, D_MODEL), lambda i: (0, 0))
    v = jax.ShapeDtypeStruct((1, D_MODEL), F32)
    return pl.pallas_call(
        body,
        out_shape=[jax.ShapeDtypeStruct((s, D_MODEL), F32), jax.ShapeDtypeStruct((s, D_MODEL), MXU_DTYPE),
                   jax.ShapeDtypeStruct((s, D_FF), MXU_DTYPE), jax.ShapeDtypeStruct((s, D_FF), MXU_DTYPE), v, v],
        grid=(s // tm,),
        in_specs=[row(D_MODEL), row(D_MODEL), row(D_FF), row(D_FF), _resident(wg_t), _resident(wu_t),
                  _resident(wd), _resident(gam)],
        out_specs=[row(D_MODEL), row(D_MODEL), row(D_FF), row(D_FF), vec, vec],
        compiler_params=_params(1), name="ffn_bwd",
    )(dx2, u2, fa, fb, wg_t, wu_t, wd, gam)


def _loss_head(y, target):
    s = y.shape[0]
    tr = _pick(s, (512, 256, 128))

    def body(y_ref, t_ref, dy_ref, l_ref):
        @pl.when(pl.program_id(0) == 0)
        def _():
            l_ref[...] = jnp.zeros_like(l_ref)

        e = y_ref[...] - t_ref[...]
        dy_ref[...] = e * (1.0 / D_MODEL)
        part = jnp.sum(jnp.sum(e * e, axis=1, keepdims=True), axis=0, keepdims=True)
        l_ref[...] += part * (0.5 / D_MODEL)

    spec = pl.BlockSpec((tr, D_MODEL), lambda i: (i, 0))
    dy, part = pl.pallas_call(
        body, out_shape=[jax.ShapeDtypeStruct((s, D_MODEL), F32), jax.ShapeDtypeStruct((8, BLK), F32)],
        grid=(s // tr,), in_specs=[spec, spec], out_specs=[spec, pl.BlockSpec((8, BLK), lambda i: (0, 0))],
        compiler_params=_params(1), name="loss_head",
    )(y, target)
    return dy, part[0, 0]


def _adamw(w, g, m, v, name):
    rows, cols = w.shape
    budget = 1 << 20
    cands = [t for t in range(8, rows + 1, 8) if rows % t == 0 and t * cols * 4 <= budget]
    tr = max(cands) if cands else rows

    def body(w_ref, g_ref, m_ref, v_ref, d_ref, nm_ref, nv_ref):
        gv = g_ref[...]
        mn = ADAM_B1 * m_ref[...] + (1.0 - ADAM_B1) * gv
        vn = ADAM_B2 * v_ref[...] + (1.0 - ADAM_B2) * (gv * gv)
        m_hat = mn / (1.0 - ADAM_B1 ** ADAM_STEP)
        v_hat = vn / (1.0 - ADAM_B2 ** ADAM_STEP)
        d_ref[...] = -ADAM_LR * (m_hat / (jnp.sqrt(v_hat) + ADAM_EPS) + ADAM_WD * w_ref[...])
        nm_ref[...] = mn
        nv_ref[...] = vn

    spec = pl.BlockSpec((tr, cols), lambda i: (i, 0))
    shape = jax.ShapeDtypeStruct((rows, cols), F32)
    return pl.pallas_call(
        body, out_shape=[shape, shape, shape], grid=(rows // tr,), in_specs=[spec] * 4, out_specs=[spec] * 3,
        compiler_params=_params(1), name=name,
    )(w, g, m, v)


MESH_ID = pl.DeviceIdType.MESH
ANY = pl.BlockSpec(memory_space=pl.ANY)


def _place():
    x, y, c = lax.axis_index("x"), lax.axis_index("y"), lax.axis_index("c")
    other_chips = [(1 - x, y), (x, 1 - y), (1 - x, 1 - y)]
    return x, y, c, other_chips


def _chip_no(chip):
    return 2 * chip[0] + chip[1]


def _gather_exchange(f_ref, g_ref, send_sems, recv_sems, hr):
    x, y, c, chips = _place()
    sibling = (x, y, 1 - c)

    def piece(chip, half):
        return g_ref.at[_chip_no(chip), pl.ds(half * hr, hr), :]

    def copy(k, src, dst, to):
        return pltpu.make_async_remote_copy(src_ref=src, dst_ref=dst, send_sem=send_sems.at[k],
                                            recv_sem=recv_sems.at[k], device_id=to, device_id_type=MESH_ID)

    own = copy(6, f_ref, g_ref.at[_chip_no((x, y))], sibling)
    own.start()
    my_half = f_ref.at[pl.ds(c * hr, hr), :]
    first = [copy(k, my_half, piece((x, y), c), (*chip, c)) for k, chip in enumerate(chips)]
    for cp in first:
        cp.start()
    passed = [copy(3 + k, piece(chip, c), piece(chip, c), sibling) for k, chip in enumerate(chips)]
    for k, chip in enumerate(chips):
        copy(k, my_half, piece(chip, c), (*chip, c)).wait_recv()
        passed[k].start()
    for k, chip in enumerate(chips):
        copy(3 + k, my_half, piece(chip, 1 - c), sibling).wait_recv()
    for cp in first + passed:
        cp.wait_send()
    own.wait()


GATHER_SEMS = (pltpu.SemaphoreType.DMA((7,)), pltpu.SemaphoreType.DMA((7,)))


HBM = pltpu.MemorySpace.HBM
ONE_SEM_PAIR = (pltpu.SemaphoreType.DMA, pltpu.SemaphoreType.DMA)


def _sibling_of_me():
    x, y, c, _ = _place()
    return [(x, y, 1 - c)]


def _same_core_of_other_chips():
    x, y, c, chips = _place()
    return [(*chip, c) for chip in chips]


def _on_sequencer(name, collective_id, sems, peers, exchange):
    @pl.kernel(mesh=plsc.ScalarSubcoreMesh(axis_name="sequencer", num_cores=1), name=name, scratch_types=sems,
               compiler_params=pltpu.CompilerParams(collective_id=collective_id))
    def launch(*sem_refs):
        barrier = pltpu.get_barrier_semaphore()
        devices = peers()
        for peer in devices:
            pl.semaphore_signal(barrier, inc=1, device_id=peer, device_id_type=MESH_ID)
        pl.semaphore_wait(barrier, len(devices))
        exchange(*sem_refs)

    launch()


def _all_gather_rows(flat, behind=False):
    r, cols = flat.shape
    out = jax.ShapeDtypeStruct((N_CHIPS, r, cols), flat.dtype)
    if behind:
        f_ref, g_ref = jax.new_ref(flat, memory_space=HBM), jax.empty_ref(out, memory_space=HBM)
        _on_sequencer("all_gather_weights_behind", 1, GATHER_SEMS,
                      lambda: _sibling_of_me() + _same_core_of_other_chips(),
                      lambda s, r_: _gather_exchange(f_ref, g_ref, s, r_, r // 2))
        return g_ref[...]

    def body(f_ref, g_ref, send_sems, recv_sems):
        _gather_exchange(f_ref, g_ref, send_sems, recv_sems, r // 2)

    return pl.pallas_call(body, out_shape=out, in_specs=[ANY], out_specs=ANY, scratch_shapes=list(GATHER_SEMS),
                          name="all_gather_weights")(flat)


def _swap_halves(gall, behind=False):
    _, r, cols = gall.shape
    hr = r // 2
    out = jax.ShapeDtypeStruct((N_CHIPS, hr, cols), gall.dtype)

    def exchange(g_ref, a_ref, send_sem, recv_sem):
        x, y, c, _ = _place()
        cp = pltpu.make_async_remote_copy(src_ref=g_ref.at[:, pl.ds((1 - c) * hr, hr), :], dst_ref=a_ref,
                                          send_sem=send_sem, recv_sem=recv_sem, device_id=(x, y, 1 - c),
                                          device_id_type=MESH_ID)
        cp.start()
        cp.wait()

    if behind:
        g_ref, a_ref = jax.new_ref(gall, memory_space=HBM), jax.empty_ref(out, memory_space=HBM)
        _on_sequencer("grad_swap_halves_behind", 2, ONE_SEM_PAIR, _sibling_of_me,
                      lambda s, r_: exchange(g_ref, a_ref, s, r_))
        return a_ref[...]

    def body(g_ref, a_ref, send_sem, recv_sem):
        exchange(g_ref, a_ref, send_sem, recv_sem)

    return pl.pallas_call(body, out_shape=out, in_specs=[ANY], out_specs=ANY, scratch_shapes=list(ONE_SEM_PAIR),
                          name="grad_swap_halves")(gall)


def _scatter_to_chips(p, behind=False):
    _, hr, cols = p.shape
    out = jax.ShapeDtypeStruct((3, hr, cols), p.dtype)
    sems = (pltpu.SemaphoreType.DMA((3,)), pltpu.SemaphoreType.DMA((3,)))

    def exchange(p_ref, b_ref, send_sems, recv_sems):
        x, y, c, chips = _place()
        cps = [pltpu.make_async_remote_copy(src_ref=p_ref.at[_chip_no(chip)], dst_ref=b_ref.at[k],
                                            send_sem=send_sems.at[k], recv_sem=recv_sems.at[k],
                                            device_id=(*chip, c), device_id_type=MESH_ID)
               for k, chip in enumerate(chips)]
        for cp in cps:
            cp.start()
        for cp in cps:
            cp.wait()

    if behind:
        p_ref, b_ref = jax.new_ref(p, memory_space=HBM), jax.empty_ref(out, memory_space=HBM)
        _on_sequencer("grad_scatter_chips_behind", 3, sems, _same_core_of_other_chips,
                      lambda s, r_: exchange(p_ref, b_ref, s, r_))
        return b_ref[...]

    def body(p_ref, b_ref, send_sems, recv_sems):
        exchange(p_ref, b_ref, send_sems, recv_sems)

    return pl.pallas_call(body, out_shape=out, in_specs=[ANY], out_specs=ANY, scratch_shapes=list(sems),
                          name="grad_scatter_chips")(p)


def _share_with_sibling(full, behind=False):
    r, cols = full.shape
    hr = r // 2

    def exchange(in_ref, out_ref, send_sem, recv_sem):
        x, y, c, _ = _place()
        cp = pltpu.make_async_remote_copy(src_ref=in_ref.at[pl.ds(c * hr, hr), :],
                                          dst_ref=out_ref.at[pl.ds(c * hr, hr), :],
                                          send_sem=send_sem, recv_sem=recv_sem, device_id=(x, y, 1 - c),
                                          device_id_type=MESH_ID)
        cp.start()
        cp.wait()

    if behind:
        full_ref = jax.new_ref(full, memory_space=HBM)
        _on_sequencer("grad_share_sibling_behind", 4, ONE_SEM_PAIR, _sibling_of_me,
                      lambda s, r_: exchange(full_ref, full_ref, s, r_))
        return full_ref[...]

    def body(in_ref, out_ref, send_sem, recv_sem):
        exchange(in_ref, out_ref, send_sem, recv_sem)

    return pl.pallas_call(
        body, out_shape=jax.ShapeDtypeStruct((r, cols), full.dtype), in_specs=[ANY], out_specs=ANY,
        input_output_aliases={0: 0}, scratch_shapes=list(ONE_SEM_PAIR), name="grad_share_sibling",
    )(full)


def _all_reduce_small(v):
    r, cols = v.shape
    n_dev = 8

    def body(x_ref, out_ref, gat_ref, send_sems, recv_sems, local_sem):
        x, y, c, chips = _place()
        me, sibling = (x, y, c), (x, y, 1 - c)

        def slot(px, py, pc):
            return gat_ref.at[4 * px + 2 * py + pc]

        def copy(k, block, to, src=None):
            return pltpu.make_async_remote_copy(src_ref=slot(*block) if src is None else src, dst_ref=slot(*block),
                                                send_sem=send_sems.at[k], recv_sem=recv_sems.at[k], device_id=to,
                                                device_id_type=MESH_ID)

        mine = pltpu.make_async_copy(x_ref, slot(*me), local_sem)
        mine.start()
        first = [copy(0, me, sibling, src=x_ref)]
        first += [copy(1 + k, me, (*chip, c), src=x_ref) for k, chip in enumerate(chips)]
        for cp in first:
            cp.start()
        passed = [copy(4 + k, (*chip, c), sibling) for k, chip in enumerate(chips)]
        for k, chip in enumerate(chips):
            copy(1 + k, (*chip, c), me).wait_recv()
            passed[k].start()
        copy(0, sibling, me).wait_recv()
        for k, chip in enumerate(chips):
            copy(4 + k, (*chip, 1 - c), me).wait_recv()
        for cp in first + passed:
            cp.wait_send()
        mine.wait()
        acc = gat_ref[0]
        for d in range(1, n_dev):
            acc = acc + gat_ref[d]
        out_ref[...] = acc

    vmem = pl.BlockSpec(memory_space=pltpu.VMEM)
    return pl.pallas_call(
        body, out_shape=jax.ShapeDtypeStruct((r, cols), v.dtype), in_specs=[vmem], out_specs=vmem,
        scratch_shapes=[pltpu.VMEM((n_dev, r, cols), v.dtype), pltpu.SemaphoreType.DMA((7,)),
                        pltpu.SemaphoreType.DMA((7,)), pltpu.SemaphoreType.DMA],
        name="all_reduce_small",
    )(v)


WIRE_DTYPE = jnp.bfloat16


def _add_own_half(gall, got, place):
    _, r, cols = gall.shape
    hr = r // 2
    tr = _pick(hr, (256, 128, 64, 32, 16, 8))
    g4 = gall.reshape(N_CHIPS, 2, hr, cols)

    def body(s_ref, g_ref, a_ref, o_ref):
        o_ref[...] = (g_ref[0] + a_ref[...]).astype(o_ref.dtype)

    return pl.pallas_call(
        body, out_shape=jax.ShapeDtypeStruct((N_CHIPS, hr, cols), WIRE_DTYPE),
        grid_spec=pltpu.PrefetchScalarGridSpec(
            num_scalar_prefetch=1, grid=(N_CHIPS, hr // tr),
            in_specs=[pl.BlockSpec((1, 1, tr, cols), lambda j, i, s: (j, s[1], i, 0)),
                      pl.BlockSpec((1, tr, cols), lambda j, i, s: (j, i, 0))],
            out_specs=pl.BlockSpec((1, tr, cols), lambda j, i, s: (j, i, 0))),
        compiler_params=_params(2), name="grad_add_halves",
    )(place, g4, got)


def _add_chip_parts(gall, got, parts, place):
    _, r, cols = gall.shape
    hr = r // 2
    tr = _pick(hr, (256, 128, 64, 32, 16, 8))
    g4 = gall.reshape(N_CHIPS, 2, hr, cols)
    nt = hr // tr

    def body(s_ref, g_ref, a_ref, b0_ref, b1_ref, b2_ref, o_ref):
        own = g_ref[0, 0] + a_ref[0]
        o_ref[...] = ((own + b0_ref[0].astype(F32)) + b1_ref[0].astype(F32)) + b2_ref[0].astype(F32)

    part = lambda k: pl.BlockSpec((1, tr, cols), lambda i, s: (k, i, 0))
    return pl.pallas_call(
        body, out_shape=jax.ShapeDtypeStruct((r, cols), F32),
        grid_spec=pltpu.PrefetchScalarGridSpec(
            num_scalar_prefetch=1, grid=(nt,),
            in_specs=[pl.BlockSpec((1, 1, tr, cols), lambda i, s: (s[0], s[1], i, 0)),
                      pl.BlockSpec((1, tr, cols), lambda i, s: (s[0], i, 0)), part(0), part(1), part(2)],
            out_specs=pl.BlockSpec((tr, cols), lambda i, s: (s[1] * nt + i, 0))),
        compiler_params=_params(1), name="grad_add_chips",
    )(place, g4, got, parts, parts, parts)


def _pack_shards(l, w_in, w_ap, w_rp, w_out, w_g, w_u, w_d, dtype):
    parts = [w_in[l].T, w_ap[l].T.reshape(-1, D_MODEL), w_rp[l], w_out[l], w_g[l].T, w_u[l].T, w_d[l]]
    return jnp.concatenate([p.astype(dtype) for p in parts], axis=0)


def _unpack_gathered(gathered):
    w, off = {}, 0
    for name, rows in PACK_ROWS:
        blk = gathered[:, off:off + rows]
        if name == "w_attn_proj":
            w[name] = blk.reshape(N_CHIPS * 256, ATTN_OUT)
        else:
            w[name] = blk.reshape(N_CHIPS * rows, D_MODEL)
        off += rows
    return w


def _pack_full_grads(g):
    return jnp.concatenate([g[name].reshape(N_CHIPS, rows, D_MODEL) for name, rows in PACK_ROWS], axis=1)


COLUMN_SHARDED = ("w_in", "w_attn_proj", "w_ffn_gate", "w_ffn_up")


def _unpack_shard_grads(fulls):
    out = {name: [] for name, _ in PACK_ROWS}
    for full in fulls:
        off = 0
        for name, rows in PACK_ROWS:
            blk = full[off:off + rows]
            out[name].append(blk.reshape(256, ATTN_OUT) if name == "w_attn_proj" else blk)
            off += rows
    return {k: jnp.stack(v) for k, v in out.items()}


class _GradReduction:
    def __init__(self, gw, place, behind):
        self.gall, self.place, self.behind, self.stage = _pack_full_grads(gw), place, behind, 0
        self.pending = None

    def advance(self, value=None):
        if self.stage > 0 and value is not None:
            value, self.pending = lax.optimization_barrier((value, self.pending))
        if self.stage == 0:
            self.pending = _swap_halves(self.gall, self.behind)
        elif self.stage == 1:
            self.got = self.pending
            self.pending = _scatter_to_chips(_add_own_half(self.gall, self.got, self.place), self.behind)
        elif self.stage == 2:
            self.pending = _share_with_sibling(
                _add_chip_parts(self.gall, self.got, self.pending, self.place), self.behind)
        self.stage += 1
        return value

    def result(self):
        while self.stage < 3:
            self.advance()
        return self.pending


def _attn_views(za, s):
    views = []
    for g, (_, dil) in enumerate(ATTN_GROUPS):
        if dil == 1:
            views.append((za, za, za, 3 * N_GROUPS, (g, N_GROUPS + g, 2 * N_GROUPS + g)))
        else:
            rows = s // dil
            part = lambda k: za[:, k * ATTN_WIDTH + g * ATTN_OUT:k * ATTN_WIDTH + (g + 1) * ATTN_OUT].reshape(
                rows, dil * ATTN_OUT)
            views.append((part(0), part(1), part(2), 1, (0, 0, 0)))
    return views


def _layer_fwd(x, w, lnp, bias, rc, tag):
    s = x.shape[0]
    z = {}
    for name, off, width, narrow in SECTIONS:
        z[name] = _matmul(x, w["w_in"][off:off + width], "nt", bias=lnp["b_in"][:, off:off + width],
                          out_dtype=MXU_DTYPE if narrow else F32, name=f"in_proj_{name}_{tag}")
    views = _attn_views(z["a"], s)
    o_list, l_list = [], []
    for g, (_, dil) in enumerate(ATTN_GROUPS):
        qa, ka, va, cb, offs = views[g]
        o, lse = _attn_fwd(qa, ka, va, cb, offs, bias, g, dil)
        o_list.append(o.reshape(s, ATTN_OUT))
        l_list.append(lse.reshape(s, ATTN_OUT))
    ya, ya_b, lt = _attn_combine(o_list, l_list)
    yb_b, o_ret, states = _ret_fwd(z["bq"], z["bk"], z["c"], z["d"], rc)
    x1, u1, a1, a2, mg_b = _mix_fwd(x, ya_b, yb_b, z["e"], w["w_attn_proj"], w["w_ret_proj"], w["w_out"],
                                    lnp["ln1_g"], lnp["ln1_b"])
    x2, u2, fa, fb, h_b = _ffn_fwd(x1, w["w_ffn_gate"], w["w_ffn_up"], w["w_ffn_down"], lnp["ln2_g"], lnp["ln2_b"])
    saved = dict(x=x, z=z, views=views, ya=ya, ya_b=ya_b, lt=lt, yb_b=yb_b, o_ret=o_ret, states=states,
                 x1=x1, u1=u1, a1=a1, a2=a2, mg_b=mg_b, u2=u2, fa=fa, fb=fb, h_b=h_b)
    return x2, saved


def _layer_bwd(dx2, w, lnp, sv, bias, rc, tag, behind=None):
    s = dx2.shape[0]
    z = sv["z"]
    step = behind.advance if behind is not None else (lambda v: v)
    dx2 = step(dx2)
    dx1, du2_b, da_b, db_b, dg2, dbt2 = _ffn_bwd(dx2, sv["u2"], sv["fa"], sv["fb"], w["w_ffn_gate"], w["w_ffn_up"],
                                                  w["w_ffn_down"], lnp["ln2_g"])
    dx1 = step(dx1)
    gw = {}
    gw["w_ffn_down"] = _matmul(sv["h_b"], du2_b, "tn", name=f"dw_ffn_down_{tag}")
    gw["w_ffn_gate"] = _matmul(da_b, sv["x1"], "tn", name=f"dw_ffn_gate_{tag}")
    gw["w_ffn_up"] = _matmul(db_b, sv["x1"], "tn", name=f"dw_ffn_up_{tag}")
    dres, du1_b, da1_b, da2_b, dze, dya, dyb, dg1, dbt1 = _mix_bwd(
        dx1, sv["u1"], sv["a1"], sv["a2"], z["e"], w["w_attn_proj"], w["w_ret_proj"], w["w_out"], lnp["ln1_g"])
    gw["w_out"] = _matmul(sv["mg_b"], du1_b, "tn", name=f"dw_out_{tag}")
    gw["w_attn_proj"] = _matmul(da1_b, sv["ya_b"], "tn", name=f"dw_attn_proj_{tag}")
    gw["w_ret_proj"] = _matmul(sv["yb_b"], da2_b, "tn", name=f"dw_ret_proj_{tag}")
    dzq, dzk, dzv, dzg = _ret_bwd(z["bq"], z["bk"], z["c"], z["d"], sv["o_ret"], sv["states"], dyb, rc)
    dzq = step(dzq)
    dq_l, dk_l, dv_l, dbias_l = [], [], [], []
    for g, (_, dil) in enumerate(ATTN_GROUPS):
        qa, ka, va, cb, offs = sv["views"][g]
        rows = s // dil
        view = lambda t: t.reshape(rows, dil * ATTN_OUT)
        dq, dk, dv, dbg = _attn_bwd(qa, ka, va, cb, offs, bias, view(dya), view(sv["ya"]), view(sv["lt"]), g, dil)
        dq_l.append(dq.reshape(s, ATTN_OUT))
        dk_l.append(dk.reshape(s, ATTN_OUT))
        dv_l.append(dv.reshape(s, ATTN_OUT))
        dbias_l.append(dbg)
    dz = {"a": jnp.concatenate(dq_l + dk_l + dv_l, axis=1), "bq": dzq, "bk": dzk, "c": dzv, "d": dzg, "e": dze}
    dx = dres
    dw_rows, db_cols = [], []
    for name, off, width, _ in SECTIONS:
        dx = _matmul(dz[name], w["w_in"][off:off + width], "nn", addend=dx, name=f"dx_in_proj_{name}_{tag}")
        dw, cs = _matmul(dz[name], sv["x"], "tn", colsum=True, name=f"dw_in_proj_{name}_{tag}")
        dw_rows.append(dw)
        db_cols.append(cs)
    gw["w_in"] = jnp.concatenate(dw_rows, axis=0)
    small = dict(b_in=jnp.concatenate(db_cols, axis=1), ln1_g=dg1, ln1_b=dbt1, ln2_g=dg2, ln2_b=dbt2)
    return step(dx), gw, small, jnp.concatenate(dbias_l, axis=0)


def _forward_backward(x, target, rel_bias, layers, lnps, on_grads=None):
    s = x.shape[0]
    bmaps = _bucket_maps()
    bias = _bias_tiles(rel_bias, bmaps)
    rc = _ret_consts(s)
    saved = []
    h = x
    for l in range(DEPTH):
        h, sv = _layer_fwd(h, layers[l], lnps[l], bias, rc, f"l{l}")
        saved.append(sv)
    dh, loss_part = _loss_head(h, target)
    gws, smalls, dbiases = [None] * DEPTH, [None] * DEPTH, [None] * DEPTH
    behind = None
    for l in reversed(range(DEPTH)):
        dh, gws[l], smalls[l], dbiases[l] = _layer_bwd(dh, layers[l], lnps[l], saved[l], bias, rc, f"l{l}", behind)
        behind = on_grads(l, gws[l]) if on_grads is not None else None
    d_rel_bias = _bias_tiles_bwd(dbiases, bmaps)
    return loss_part, dh, gws, smalls, d_rel_bias


SMALL_NAMES = ("rel_bias", "b_in", "ln1_g", "ln1_b", "ln2_g", "ln2_b")
SMALL_ROWS = 32


def _pack_small(vals):
    flat = jnp.concatenate([vals[n].reshape(-1) for n in SMALL_NAMES])
    return jnp.pad(flat, (0, SMALL_ROWS * D_MODEL - flat.shape[0])).reshape(SMALL_ROWS, D_MODEL)


def _unpack_small(packed, like):
    flat = packed.reshape(-1)
    out, off = {}, 0
    for n in SMALL_NAMES:
        size = like[n].size
        out[n] = flat[off:off + size].reshape(like[n].shape)
        off += size
    return out


def kernel(x, rel_bias, w_in, b_in, w_attn_proj, w_ret_proj, w_out, ln1_g, ln1_b, w_ffn_gate, w_ffn_up, w_ffn_down, ln2_g, ln2_b, loss_target, m_rel_bias, m_w_in, m_b_in, m_w_attn_proj, m_w_ret_proj, m_w_out, m_ln1_g, m_ln1_b, m_w_ffn_gate, m_w_ffn_up, m_w_ffn_down, m_ln2_g, m_ln2_b, v_rel_bias, v_w_in, v_b_in, v_w_attn_proj, v_w_ret_proj, v_w_out, v_ln1_g, v_ln1_b, v_w_ffn_gate, v_w_ffn_up, v_w_ffn_down, v_ln2_g, v_ln2_b):
    big = dict(w_in=(w_in, m_w_in, v_w_in), w_attn_proj=(w_attn_proj, m_w_attn_proj, v_w_attn_proj),
               w_ret_proj=(w_ret_proj, m_w_ret_proj, v_w_ret_proj), w_out=(w_out, m_w_out, v_w_out),
               w_ffn_gate=(w_ffn_gate, m_w_ffn_gate, v_w_ffn_gate), w_ffn_up=(w_ffn_up, m_w_ffn_up, v_w_ffn_up),
               w_ffn_down=(w_ffn_down, m_w_ffn_down, v_w_ffn_down))
    small_w = dict(rel_bias=rel_bias, b_in=b_in, ln1_g=ln1_g, ln1_b=ln1_b, ln2_g=ln2_g, ln2_b=ln2_b)
    small_m = dict(rel_bias=m_rel_bias, b_in=m_b_in, ln1_g=m_ln1_g, ln1_b=m_ln1_b, ln2_g=m_ln2_g, ln2_b=m_ln2_b)
    small_v = dict(rel_bias=v_rel_bias, b_in=v_b_in, ln1_g=v_ln1_g, ln1_b=v_ln1_b, ln2_g=v_ln2_g, ln2_b=v_ln2_b)

    place = jnp.stack([2 * lax.axis_index("x") + lax.axis_index("y"), lax.axis_index("c")]).astype(jnp.int32)

    flats = [_pack_shards(l, w_in, w_attn_proj, w_ret_proj, w_out, w_ffn_gate, w_ffn_up, w_ffn_down, MXU_DTYPE)
             for l in range(DEPTH)]
    gathered = _all_gather_rows(flats[0])
    layers = [None] * DEPTH
    for l in range(1, DEPTH):
        gathered, nxt = lax.optimization_barrier((gathered, flats[l]))
        layers[l - 1] = _unpack_gathered(gathered)
        gathered = _all_gather_rows(nxt, behind=True)
    layers[DEPTH - 1] = _unpack_gathered(gathered)
    lnps = [dict(b_in=b_in[l][None], ln1_g=ln1_g[l][None], ln1_b=ln1_b[l][None], ln2_g=ln2_g[l][None],
                 ln2_b=ln2_b[l][None]) for l in range(DEPTH)]

    reductions = [None] * DEPTH

    def on_grads(l, gw):
        reductions[l] = _GradReduction(gw, place, behind=l > 0)
        return reductions[l] if l > 0 else None

    loss_part, dx, _, smalls, d_rel_bias = _forward_backward(x[0], loss_target[0], rel_bias, layers, lnps, on_grads)
    loss = lax.psum(loss_part, ("x", "y", "c"))
    grads = _unpack_shard_grads([r.result() for r in reductions])

    small_g = dict(rel_bias=d_rel_bias)
    for n in SMALL_NAMES[1:]:
        small_g[n] = jnp.concatenate([smalls[l][n] for l in range(DEPTH)], axis=0)
    small_g = _unpack_small(_all_reduce_small(_pack_small(small_g)), small_w)

    delta, new_m, new_v = {}, {}, {}
    for n, (w, m, v) in big.items():
        turn = (lambda t: jnp.swapaxes(t, 1, 2)) if n in COLUMN_SHARDED else (lambda t: t)
        two_d = lambda t: t.reshape(-1, t.shape[-1])
        g = grads[n]
        d_, m_, v_ = _adamw(two_d(turn(w)), two_d(g), two_d(turn(m)), two_d(turn(v)), name=f"adamw_{n}")
        grads[n], delta[n], new_m[n], new_v[n] = (turn(t.reshape(g.shape)) for t in (g, d_, m_, v_))
    d_, m_, v_ = _adamw(_pack_small(small_w), _pack_small(small_g), _pack_small(small_m), _pack_small(small_v),
                        name="adamw_small")
    delta.update(_unpack_small(d_, small_w))
    new_m.update(_unpack_small(m_, small_w))
    new_v.update(_unpack_small(v_, small_w))
    grads.update(small_g)

    order = ("rel_bias", "w_in", "b_in", "w_attn_proj", "w_ret_proj", "w_out", "ln1_g", "ln1_b", "w_ffn_gate",
             "w_ffn_up", "w_ffn_down", "ln2_g", "ln2_b")
    return (loss, dx[None], *[grads[n] for n in order], *[delta[n] for n in order], *[new_m[n] for n in order],
            *[new_v[n] for n in order])
```

```python
import functools

import numpy as np
import jax
import jax.numpy as jnp
from jax import lax
from jax.experimental import pallas as pl
from jax.experimental.pallas import tpu as pltpu
from jax.experimental.pallas import tpu_sc as plsc

F32 = jnp.float32
MXU_DTYPE = jnp.bfloat16

DEPTH = 2
D_MODEL = 1024
HEAD_DIM = 64
ATTN_GROUPS = ((128, 1), (512, 4), (2048, 16))
N_GROUPS = len(ATTN_GROUPS)
HEADS_PER_GROUP = 6
N_ATTN_HEADS = 18
ATTN_WIDTH = 1152
ATTN_OUT = 384
NUM_BUCKETS = 32
MAX_DISTANCE = 2048
RET_HEADS = 4
RET_QK = 256
RET_V = 512
RET_CHUNK = 128
ROPE_BASE = 10000.0
D_FF = 2816
IN_COLS = 11648
ALPHA = (2 * DEPTH) ** 0.25
LN_EPS = 1e-5
GN_EPS = 1e-5
ADAM_LR, ADAM_B1, ADAM_B2, ADAM_EPS, ADAM_WD, ADAM_STEP = 0.001, 0.9, 0.999, 1e-08, 0.01, 10

BLK = 128
NEG = -1e30
N_CHIPS = 4
VMEM_LIMIT = 48 * 1024 * 1024

SECTIONS = (
    ("a", 0, 3456, True),
    ("bq", 3456, 1024, False),
    ("bk", 4480, 1024, False),
    ("c", 5504, 2048, True),
    ("d", 7552, 2048, False),
    ("e", 9600, 2048, False),
)
PACK_GROUPS = ((("w_in", 2912),),
               (("w_attn_proj", 96), ("w_ret_proj", 512), ("w_out", 256), ("w_ffn_gate", 704), ("w_ffn_up", 704),
                ("w_ffn_down", 704)))
IN_GROUP, REST_GROUP = 0, 1
COLUMN_SHARDED = ("w_in", "w_attn_proj", "w_ffn_gate", "w_ffn_up")

NN = ((1,), (0,))
NT = ((1,), (1,))
TN = ((0,), (0,))


def _dot(a, b, dims):
    return lax.dot_general(a.astype(MXU_DTYPE), b.astype(MXU_DTYPE), (dims, ((), ())),
                           preferred_element_type=F32)


def _pick(n, prefs):
    for p in prefs:
        if n % p == 0:
            return p
    raise ValueError(f"no tile for {n} among {prefs}")


TOKEN_TILES = (1024, 512, 256, 128)
FEATURE_TILES = (1152, 1024, 1408, 384, 256, 128)


def _params(n_axes, limit=VMEM_LIMIT):
    return pltpu.CompilerParams(dimension_semantics=("arbitrary",) * n_axes, vmem_limit_bytes=limit)


def _resident(a):
    return pl.BlockSpec(a.shape, lambda i: (0,) * a.ndim, pipeline_mode=pl.Buffered(1))


def _sigmoid(x):
    return 1.0 / (1.0 + jnp.exp(-x))


def _norm_rows(u, eps):
    mu = jnp.mean(u, axis=-1, keepdims=True)
    xc = u - mu
    var = jnp.mean(xc * xc, axis=-1, keepdims=True)
    rstd = lax.rsqrt(var + eps)
    return xc * rstd, rstd


def _norm_rows_bwd(dxh, xh, rstd):
    c1 = jnp.mean(dxh, axis=-1, keepdims=True)
    c2 = jnp.mean(dxh * xh, axis=-1, keepdims=True)
    return rstd * (dxh - c1 - xh * c2)


def _matmul(a, b, mode, *, name, out_dtype=F32, bias=None, addend=None, colsum=False):
    if mode == "tn":
        kd, m = a.shape
        n = b.shape[1]
        tm, tn_, tk = _pick(m, FEATURE_TILES), _pick(n, FEATURE_TILES), _pick(kd, TOKEN_TILES)
        a_spec = pl.BlockSpec((tk, tm), lambda i, j, k: (k, i))
        b_spec = pl.BlockSpec((tk, tn_), lambda i, j, k: (k, j))
    else:
        m, kd = a.shape
        n = b.shape[0] if mode == "nt" else b.shape[1]
        tm, tn_, tk = _pick(m, TOKEN_TILES), _pick(n, FEATURE_TILES), _pick(kd, FEATURE_TILES)
        a_spec = pl.BlockSpec((tm, tk), lambda i, j, k: (i, k))
        if mode == "nt":
            b_spec = pl.BlockSpec((tn_, tk), lambda i, j, k: (j, k))
        else:
            b_spec = pl.BlockSpec((tk, tn_), lambda i, j, k: (k, j))
    dims = {"nn": NN, "nt": NT, "tn": TN}[mode]
    nk = kd // tk
    has_bias, has_add = bias is not None, addend is not None
    assert not colsum or mode == "tn"

    def body(*refs):
        it = iter(refs)
        a_ref, b_ref = next(it), next(it)
        bias_ref = next(it) if has_bias else None
        add_ref = next(it) if has_add else None
        o_ref = next(it)
        cs_ref = next(it) if colsum else None
        acc_ref = next(it)
        j, k = pl.program_id(1), pl.program_id(2)

        @pl.when(k == 0)
        def _():
            if has_add:
                acc_ref[...] = add_ref[...].astype(F32)
            else:
                acc_ref[...] = jnp.zeros_like(acc_ref)

        av = a_ref[...]
        acc_ref[...] += _dot(av, b_ref[...], dims)
        if colsum:
            @pl.when(jnp.logical_and(j == 0, k == 0))
            def _():
                cs_ref[...] = jnp.zeros_like(cs_ref)

            @pl.when(j == 0)
            def _():
                cs_ref[...] += jnp.sum(av.astype(F32), axis=0, keepdims=True)

        @pl.when(k == nk - 1)
        def _():
            r = acc_ref[...]
            if has_bias:
                r = r + bias_ref[...]
            o_ref[...] = r.astype(out_dtype)

    in_specs, args = [a_spec, b_spec], [a, b]
    if has_bias:
        in_specs.append(pl.BlockSpec((1, tn_), lambda i, j, k: (0, j)))
        args.append(bias)
    if has_add:
        in_specs.append(pl.BlockSpec((tm, tn_), lambda i, j, k: (i, j)))
        args.append(addend)
    out_shape = [jax.ShapeDtypeStruct((m, n), out_dtype)]
    out_specs = [pl.BlockSpec((tm, tn_), lambda i, j, k: (i, j))]
    if colsum:
        out_shape.append(jax.ShapeDtypeStruct((1, m), F32))
        out_specs.append(pl.BlockSpec((1, tm), lambda i, j, k: (0, i)))
    res = pl.pallas_call(
        body, out_shape=out_shape, grid=(m // tm, n // tn_, nk), in_specs=in_specs, out_specs=out_specs,
        scratch_shapes=[pltpu.VMEM((tm, tn_), F32)], compiler_params=_params(3), name=name,
    )(*args)
    return res if colsum else res[0]


def _t5_bucket(dist):
    max_exact = NUM_BUCKETS // 2
    large = max_exact + (np.log(np.maximum(dist, max_exact) / max_exact)
                         / np.log(MAX_DISTANCE / max_exact) * (NUM_BUCKETS - max_exact)).astype(np.int32)
    large = np.minimum(large, NUM_BUCKETS - 1)
    return np.where(dist < max_exact, dist, large).astype(np.int32)


def _bucket_maps():
    qi = np.arange(BLK)[:, None]
    kj = np.arange(2 * BLK)[None, :]
    rel = np.clip(qi + BLK - kj, 0, BLK)
    return jnp.asarray(np.stack([_t5_bucket(rel * d) for _, d in ATTN_GROUPS]))


def _bias_tiles(rel_bias, bmaps):
    def body(tab_ref, bm_ref, o_ref):
        h = pl.program_id(0)
        bm = bm_ref[0]
        acc = jnp.zeros((BLK, 2 * BLK), F32)
        for b in range(NUM_BUCKETS):
            acc = jnp.where(bm == b, tab_ref[b, h], acc)
        o_ref[0] = acc

    return pl.pallas_call(
        body, out_shape=jax.ShapeDtypeStruct((N_ATTN_HEADS, BLK, 2 * BLK), F32), grid=(N_ATTN_HEADS,),
        in_specs=[pl.BlockSpec(memory_space=pltpu.SMEM),
                  pl.BlockSpec((1, BLK, 2 * BLK), lambda h: (h // HEADS_PER_GROUP, 0, 0))],
        out_specs=pl.BlockSpec((1, BLK, 2 * BLK), lambda h: (h, 0, 0)),
        compiler_params=_params(1), name="bias_tiles",
    )(rel_bias, bmaps)


def _bias_tiles_bwd(dbias_layers, bmaps):
    nl = len(dbias_layers)

    def body(*refs):
        bm = refs[nl][0]
        o_ref = refs[nl + 1]
        x = refs[0][0]
        for r in refs[1:nl]:
            x = x + r[0]
        lane = lax.broadcasted_iota(jnp.int32, (1, BLK), 1)
        row = jnp.zeros((1, BLK), F32)
        for b in range(NUM_BUCKETS):
            s = jnp.sum(jnp.where(bm == b, x, 0.0), axis=1, keepdims=True)
            s = jnp.sum(s, axis=0, keepdims=True)
            row = jnp.where(lane == b, s, row)
        o_ref[0] = row

    tile = pl.BlockSpec((1, BLK, 2 * BLK), lambda h: (h, 0, 0))
    out = pl.pallas_call(
        body, out_shape=jax.ShapeDtypeStruct((N_ATTN_HEADS, 1, BLK), F32), grid=(N_ATTN_HEADS,),
        in_specs=[tile] * nl + [pl.BlockSpec((1, BLK, 2 * BLK), lambda h: (h // HEADS_PER_GROUP, 0, 0))],
        out_specs=pl.BlockSpec((1, 1, BLK), lambda h: (h, 0, 0)),
        compiler_params=_params(1), name="bias_tiles_bwd",
    )(*dbias_layers, bmaps)
    return out[:, 0, :NUM_BUCKETS].T


def _pair_masks():
    lane = lax.broadcasted_iota(jnp.int32, (BLK, BLK), 1)
    row2 = lax.broadcasted_iota(jnp.int32, (2 * BLK, BLK), 0)
    lane2 = lax.broadcasted_iota(jnp.int32, (2 * BLK, BLK), 1)
    own = (lane2 // HEAD_DIM) == (row2 // BLK)
    qi = lax.broadcasted_iota(jnp.int32, (2 * BLK, 2 * BLK), 0) & (BLK - 1)
    kj = lax.broadcasted_iota(jnp.int32, (2 * BLK, 2 * BLK), 1)
    band = jnp.logical_and(kj >= qi, kj <= qi + BLK)
    return lane < HEAD_DIM, own, band, kj < BLK


def _pair_scores(q32, kb, bias2, own, band, is_prev, pen):
    qm = jnp.where(own, jnp.concatenate([q32, q32], axis=0), 0.0)
    s = _dot(qm, kb, NT) * (HEAD_DIM ** -0.5) + bias2
    if pen is not None:
        s = s + jnp.where(is_prev, pen, 0.0)
    return jnp.where(band, s, NEG), qm


HEAD_PAIRS = HEADS_PER_GROUP // 2


def _attn_fwd(qa, ka, va, cb, offs, bias, g, dil):
    rows = qa.shape[0]
    nb = rows // BLK
    rb = 2 if nb % 2 == 0 else 1
    oq, ok, ov = offs

    def cur(off):
        return pl.BlockSpec((rb * BLK, ATTN_OUT), lambda r, n: (n, r * cb + off))

    def prev(off):
        return pl.BlockSpec((BLK, ATTN_OUT), lambda r, n: (jnp.maximum(rb * n - 1, 0), r * cb + off))

    def body(q_ref, kp_ref, kc_ref, vp_ref, vc_ref, b_ref, o_ref, l_ref):
        n = pl.program_id(1)
        pen0 = jnp.where(n > 0, 0.0, NEG)
        first, own, band, is_prev = _pair_masks()
        o_rows, l_rows = [], []
        for t in range(rb):
            rows_t = slice(t * BLK, (t + 1) * BLK)
            o_parts, l_parts = [], []
            for hp in range(HEAD_PAIRS):
                cols = slice(hp * BLK, (hp + 1) * BLK)
                if t == 0:
                    kp, vp, pen = kp_ref[:, cols], vp_ref[:, cols], pen0
                else:
                    before = slice((t - 1) * BLK, t * BLK)
                    kp, vp, pen = kc_ref[before, cols], vc_ref[before, cols], None
                kb = jnp.concatenate([kp, kc_ref[rows_t, cols]], axis=0)
                vb = jnp.concatenate([vp, vc_ref[rows_t, cols]], axis=0)
                bias2 = jnp.concatenate([b_ref[2 * hp], b_ref[2 * hp + 1]], axis=0)
                s, _ = _pair_scores(q_ref[rows_t, cols].astype(F32), kb, bias2, own, band, is_prev, pen)
                m = jnp.max(s, axis=1, keepdims=True)
                p = jnp.exp(s - m)
                l = jnp.sum(p, axis=1, keepdims=True)
                o2 = _dot(p * (1.0 / l), vb, NN)
                lse2 = m + jnp.log(l)
                o_parts.append(jnp.where(first, o2[:BLK], o2[BLK:]))
                l_parts.append(jnp.where(first, lse2[:BLK], lse2[BLK:]))
            o_rows.append(jnp.concatenate(o_parts, axis=1))
            l_rows.append(jnp.concatenate(l_parts, axis=1))
        o_ref[...] = jnp.concatenate(o_rows, axis=0)
        l_ref[...] = jnp.concatenate(l_rows, axis=0)

    out_spec = pl.BlockSpec((rb * BLK, ATTN_OUT), lambda r, n: (n, r))
    shape = jax.ShapeDtypeStruct((rows, dil * ATTN_OUT), F32)
    return pl.pallas_call(
        body, out_shape=[shape, shape], grid=(dil, nb // rb),
        in_specs=[cur(oq), prev(ok), cur(ok), prev(ov), cur(ov),
                  pl.BlockSpec((HEADS_PER_GROUP, BLK, 2 * BLK), lambda r, n: (g, 0, 0))],
        out_specs=[out_spec, out_spec], compiler_params=_params(2), name=f"attn_fwd_g{g}",
    )(qa, ka, ka, va, va, bias)


def _attn_bwd(qa, ka, va, cb, offs, bias, dy, ya, lt, g, dil):
    rows = qa.shape[0]
    nb = rows // BLK
    oq, ok, ov = offs

    def cur(off, c):
        return pl.BlockSpec((BLK, ATTN_OUT), lambda r, n: (jnp.minimum(n, nb - 1), r * c + off))

    def prev(off, c):
        return pl.BlockSpec((BLK, ATTN_OUT), lambda r, n: (jnp.clip(n - 1, 0, nb - 1), r * c + off))

    def body(q_ref, kp_ref, kc_ref, vp_ref, vc_ref, b_ref, dy_ref, ya_ref, lt_ref,
             dq_ref, dk_ref, dv_ref, db_ref, ck_ref, cv_ref):
        r, n = pl.program_id(0), pl.program_id(1)

        @pl.when(jnp.logical_and(r == 0, n == 0))
        def _():
            db_ref[...] = jnp.zeros_like(db_ref)

        @pl.when(n == 0)
        def _():
            ck_ref[...] = jnp.zeros_like(ck_ref)
            cv_ref[...] = jnp.zeros_like(cv_ref)

        @pl.when(n < nb)
        def _():
            pen = jnp.where(n > 0, 0.0, NEG)
            first, own, band, is_prev = _pair_masks()
            second = jnp.logical_not(first)
            scale = HEAD_DIM ** -0.5
            parts = {k: [] for k in ("dq", "dkp", "dkc", "dvp", "dvc")}
            db_parts = []
            for hp in range(HEAD_PAIRS):
                cols = slice(hp * BLK, (hp + 1) * BLK)
                kb = jnp.concatenate([kp_ref[:, cols], kc_ref[:, cols]], axis=0)
                vb = jnp.concatenate([vp_ref[:, cols], vc_ref[:, cols]], axis=0)
                bias2 = jnp.concatenate([b_ref[2 * hp], b_ref[2 * hp + 1]], axis=0)
                dy_, lt_ = dy_ref[:, cols], lt_ref[:, cols]
                dyy = dy_ * ya_ref[:, cols]
                per_head = lambda t, red, fill: jnp.concatenate(
                    [red(jnp.where(first, t, fill), axis=1, keepdims=True),
                     red(jnp.where(second, t, fill), axis=1, keepdims=True)], axis=0)
                lse2 = per_head(lt_, jnp.max, NEG)
                delta2 = per_head(dyy, jnp.sum, 0.0)
                s, qm = _pair_scores(q_ref[:, cols].astype(F32), kb, bias2, own, band, is_prev, pen)
                p = jnp.exp(s - lse2)
                dym = jnp.where(own, jnp.concatenate([dy_, dy_], axis=0), 0.0)
                ds = p * (_dot(dym, vb, NT) - delta2)
                db_parts += [ds[:BLK], ds[BLK:]]
                dq2 = _dot(ds, kb, NN) * scale
                dkb = _dot(ds, qm, TN) * scale
                dvb = _dot(p, dym, TN)
                for k, val in (("dq", jnp.where(first, dq2[:BLK], dq2[BLK:])), ("dkp", dkb[:BLK]),
                               ("dkc", dkb[BLK:]), ("dvp", dvb[:BLK]), ("dvc", dvb[BLK:])):
                    parts[k].append(val)
            wide = {k: jnp.concatenate(val, axis=1) for k, val in parts.items()}
            db_ref[...] += jnp.stack(db_parts, axis=0)
            dq_ref[...] = wide["dq"].astype(dq_ref.dtype)
            dk_ref[...] = (ck_ref[...] + wide["dkp"]).astype(dk_ref.dtype)
            dv_ref[...] = (cv_ref[...] + wide["dvp"]).astype(dv_ref.dtype)
            ck_ref[...] = wide["dkc"]
            cv_ref[...] = wide["dvc"]

        @pl.when(n == nb)
        def _():
            dk_ref[...] = ck_ref[...].astype(dk_ref.dtype)
            dv_ref[...] = cv_ref[...].astype(dv_ref.dtype)

    late = pl.BlockSpec((BLK, ATTN_OUT), lambda r, n: (jnp.maximum(n - 1, 0), r))
    shape = jax.ShapeDtypeStruct((rows, dil * ATTN_OUT), MXU_DTYPE)
    return pl.pallas_call(
        body,
        out_shape=[shape, shape, shape, jax.ShapeDtypeStruct((HEADS_PER_GROUP, BLK, 2 * BLK), F32)],
        grid=(dil, nb + 1),
        in_specs=[cur(oq, cb), prev(ok, cb), cur(ok, cb), prev(ov, cb), cur(ov, cb),
                  pl.BlockSpec((HEADS_PER_GROUP, BLK, 2 * BLK), lambda r, n: (g, 0, 0)),
                  cur(0, 1), cur(0, 1), cur(0, 1)],
        out_specs=[cur(0, 1), late, late, pl.BlockSpec((HEADS_PER_GROUP, BLK, 2 * BLK), lambda r, n: (0, 0, 0))],
        scratch_shapes=[pltpu.VMEM((BLK, ATTN_OUT), F32), pltpu.VMEM((BLK, ATTN_OUT), F32)],
        compiler_params=_params(2), name=f"attn_bwd_g{g}",
    )(qa, ka, ka, va, va, bias, dy, ya, lt)


def _attn_combine(o_list, l_list):
    s = o_list[0].shape[0]
    tr = _pick(s, (1024, 512, 256, 128))

    def body(o0, o1, o2, l0, l1, l2, y_ref, yb_ref, lt_ref):
        a0, a1, a2 = l0[...], l1[...], l2[...]
        mx = jnp.maximum(jnp.maximum(a0, a1), a2)
        e0, e1, e2 = jnp.exp(a0 - mx), jnp.exp(a1 - mx), jnp.exp(a2 - mx)
        den = e0 + e1 + e2
        inv = 1.0 / den
        y = (e0 * inv) * o0[...] + (e1 * inv) * o1[...] + (e2 * inv) * o2[...]
        y_ref[...] = y
        yb_ref[...] = y.astype(yb_ref.dtype)
        lt_ref[...] = mx + jnp.log(den)

    spec = pl.BlockSpec((tr, ATTN_OUT), lambda i: (i, 0))
    f = jax.ShapeDtypeStruct((s, ATTN_OUT), F32)
    return pl.pallas_call(
        body, out_shape=[f, jax.ShapeDtypeStruct((s, ATTN_OUT), MXU_DTYPE), f], grid=(s // tr,),
        in_specs=[spec] * 6, out_specs=[spec] * 3, compiler_params=_params(1), name="attn_combine",
    )(*o_list, *l_list)


def _ret_consts(s):
    half = RET_QK // 2
    pos = jnp.arange(s, dtype=F32)
    inv_freq = ROPE_BASE ** (-jnp.arange(half, dtype=F32) / half)
    ang = pos[:, None] * inv_freq[None]
    log_g = jnp.log(1.0 - 2.0 ** (-5.0 - jnp.arange(RET_HEADS, dtype=F32)))
    n = jnp.arange(RET_CHUNK, dtype=F32)
    diff = n[:, None] - n[None, :]
    dmask = jnp.where(diff >= 0, jnp.exp(log_g[:, None, None] * jnp.maximum(diff, 0.0)), 0.0)
    qdec = jnp.exp(log_g[:, None] * (n + 1.0))
    kdec = jnp.exp(log_g[:, None] * (RET_CHUNK - 1.0 - n))
    cdec = jnp.exp(log_g * RET_CHUNK)
    wide = (RET_HEADS, RET_CHUNK, RET_QK)
    return dict(cos=jnp.cos(ang), sin=jnp.sin(ang), dmask=dmask,
                qdec=jnp.broadcast_to(qdec[:, :, None], wide), kdec=jnp.broadcast_to(kdec[:, :, None], wide),
                cdec=cdec)


def _rot(t, cs, sn):
    half = RET_QK // 2
    t1, t2 = t[:, :half], t[:, half:]
    return jnp.concatenate([t1 * cs - t2 * sn, t1 * sn + t2 * cs], axis=1)


def _rot_bwd(d, cs, sn):
    half = RET_QK // 2
    d1, d2 = d[:, :half], d[:, half:]
    return jnp.concatenate([d1 * cs + d2 * sn, d2 * cs - d1 * sn], axis=1)


def _ret_fwd(zq, zk, zv, zg, rc):
    s = zq.shape[0]
    nc = s // RET_CHUNK
    c = RET_CHUNK

    def body(cd_ref, q_ref, k_ref, v_ref, g_ref, cos_ref, sin_ref, dm_ref, qd_ref, kd_ref,
             yb_ref, o_ref, st_ref, state):
        n = pl.program_id(0)

        @pl.when(n == 0)
        def _():
            state[...] = jnp.zeros_like(state)

        cs, sn = cos_ref[...], sin_ref[...]
        for h in range(RET_HEADS):
            qs, vs = slice(h * RET_QK, (h + 1) * RET_QK), slice(h * RET_V, (h + 1) * RET_V)
            qr = _rot(q_ref[:, qs], cs, sn)
            kr = _rot(k_ref[:, qs], cs, sn) * (RET_QK ** -0.5)
            v = v_ref[:, vs]
            st = state[h]
            st_ref[h, 0] = st.astype(st_ref.dtype)
            sc = _dot(qr, kr, NT) * dm_ref[h]
            o = _dot(sc, v, NN) + _dot(qr * qd_ref[h], st, NN)
            state[h] = st * cd_ref[h] + _dot(kr * kd_ref[h], v, TN)
            o_ref[:, vs] = o
            xh, _ = _norm_rows(o, GN_EPS)
            gv = g_ref[:, vs]
            yb_ref[:, vs] = (gv * _sigmoid(gv) * xh).astype(yb_ref.dtype)

    row = lambda w: pl.BlockSpec((c, w), lambda n: (n, 0))
    const = lambda a: pl.BlockSpec(a.shape, lambda n: (0, 0, 0))
    return pl.pallas_call(
        body,
        out_shape=[jax.ShapeDtypeStruct((s, RET_HEADS * RET_V), MXU_DTYPE),
                   jax.ShapeDtypeStruct((s, RET_HEADS * RET_V), F32),
                   jax.ShapeDtypeStruct((RET_HEADS, nc, RET_QK, RET_V), MXU_DTYPE)],
        grid=(nc,),
        in_specs=[pl.BlockSpec(memory_space=pltpu.SMEM), row(RET_HEADS * RET_QK), row(RET_HEADS * RET_QK),
                  row(RET_HEADS * RET_V), row(RET_HEADS * RET_V), row(RET_QK // 2), row(RET_QK // 2),
                  const(rc["dmask"]), const(rc["qdec"]), const(rc["kdec"])],
        out_specs=[row(RET_HEADS * RET_V), row(RET_HEADS * RET_V),
                   pl.BlockSpec((RET_HEADS, 1, RET_QK, RET_V), lambda n: (0, n, 0, 0))],
        scratch_shapes=[pltpu.VMEM((RET_HEADS, RET_QK, RET_V), F32)],
        compiler_params=_params(1), name="ret_fwd",
    )(rc["cdec"], zq, zk, zv, zg, rc["cos"], rc["sin"], rc["dmask"], rc["qdec"], rc["kdec"])


def _ret_bwd(zq, zk, zv, zg, o_ret, states, dyb, rc):
    s = zq.shape[0]
    nc = s // RET_CHUNK
    c = RET_CHUNK

    def body(cd_ref, q_ref, k_ref, v_ref, g_ref, o_ref, st_ref, dy_ref, cos_ref, sin_ref, dm_ref, qd_ref,
             kd_ref, dq_ref, dk_ref, dv_ref, dg_ref, dstate):
        n = pl.program_id(0)

        @pl.when(n == 0)
        def _():
            dstate[...] = jnp.zeros_like(dstate)

        cs, sn = cos_ref[...], sin_ref[...]
        for h in range(RET_HEADS):
            qs, vs = slice(h * RET_QK, (h + 1) * RET_QK), slice(h * RET_V, (h + 1) * RET_V)
            qr = _rot(q_ref[:, qs], cs, sn)
            kr = _rot(k_ref[:, qs], cs, sn) * (RET_QK ** -0.5)
            v = v_ref[:, vs]
            st = st_ref[h, 0]
            dm, qd, kd = dm_ref[h], qd_ref[h], kd_ref[h]
            xh, rstd = _norm_rows(o_ref[:, vs], GN_EPS)
            gv, dy = g_ref[:, vs], dy_ref[:, vs]
            sg = _sigmoid(gv)
            dg_ref[:, vs] = dy * xh * (sg * (1.0 + gv * (1.0 - sg)))
            do = _norm_rows_bwd(dy * (gv * sg), xh, rstd)
            ds_next = dstate[h]
            sc = _dot(qr, kr, NT) * dm
            da = _dot(do, v, NT) * dm
            dv = _dot(sc, do, TN) + _dot(kr * kd, ds_next, NN)
            dqr = _dot(da, kr, NN) + _dot(do, st, NT) * qd
            dkr = _dot(da, qr, TN) + _dot(v, ds_next, NT) * kd
            dstate[h] = ds_next * cd_ref[h] + _dot(qr * qd, do, TN)
            dq_ref[:, qs] = _rot_bwd(dqr, cs, sn)
            dk_ref[:, qs] = _rot_bwd(dkr * (RET_QK ** -0.5), cs, sn)
            dv_ref[:, vs] = dv.astype(dv_ref.dtype)

    row = lambda w: pl.BlockSpec((c, w), lambda n: (nc - 1 - n, 0))
    const = lambda a: pl.BlockSpec(a.shape, lambda n: (0, 0, 0))
    f_qk = jax.ShapeDtypeStruct((s, RET_HEADS * RET_QK), F32)
    qk_w, v_w = RET_HEADS * RET_QK, RET_HEADS * RET_V
    return pl.pallas_call(
        body,
        out_shape=[f_qk, f_qk, jax.ShapeDtypeStruct((s, v_w), MXU_DTYPE), jax.ShapeDtypeStruct((s, v_w), F32)],
        grid=(nc,),
        in_specs=[pl.BlockSpec(memory_space=pltpu.SMEM), row(qk_w), row(qk_w), row(v_w), row(v_w), row(v_w),
                  pl.BlockSpec((RET_HEADS, 1, RET_QK, RET_V), lambda n: (0, nc - 1 - n, 0, 0)), row(v_w),
                  row(RET_QK // 2), row(RET_QK // 2), const(rc["dmask"]), const(rc["qdec"]), const(rc["kdec"])],
        out_specs=[row(qk_w), row(qk_w), row(v_w), row(v_w)],
        scratch_shapes=[pltpu.VMEM((RET_HEADS, RET_QK, RET_V), F32)],
        compiler_params=_params(1), name="ret_bwd",
    )(rc["cdec"], zq, zk, zv, zg, o_ret, states, dyb, rc["cos"], rc["sin"], rc["dmask"], rc["qdec"],
      rc["kdec"])


ROW_TILE = 256


def _mix_fwd(x, ya_b, yb_b, ze, wap_t, wrp, wout, gam, bet):
    s = x.shape[0]
    tm = ROW_TILE

    def body(x_ref, ya_ref, yb_ref, ze_ref, wap_ref, wrp_ref, wout_ref, g_ref, b_ref,
             x1_ref, u1_ref, a1_ref, a2_ref, mg_ref):
        a1 = _dot(ya_ref[...], wap_ref[...], NT)
        a2 = _dot(yb_ref[...], wrp_ref[...], NN)
        ze_ = ze_ref[...]
        merged = _sigmoid(ze_[:, :D_MODEL]) * a1 + _sigmoid(ze_[:, D_MODEL:]) * a2
        u = ALPHA * x_ref[...] + _dot(merged, wout_ref[...], NN)
        xh, _ = _norm_rows(u, LN_EPS)
        x1_ref[...] = xh * g_ref[...] + b_ref[...]
        u1_ref[...] = u
        a1_ref[...] = a1
        a2_ref[...] = a2
        mg_ref[...] = merged.astype(mg_ref.dtype)

    row = lambda w: pl.BlockSpec((tm, w), lambda i: (i, 0))
    full = _resident
    f = jax.ShapeDtypeStruct((s, D_MODEL), F32)
    return pl.pallas_call(
        body, out_shape=[f, f, f, f, jax.ShapeDtypeStruct((s, D_MODEL), MXU_DTYPE)], grid=(s // tm,),
        in_specs=[row(D_MODEL), row(ATTN_OUT), row(RET_HEADS * RET_V), row(2 * D_MODEL), full(wap_t), full(wrp),
                  full(wout), full(gam), full(bet)],
        out_specs=[row(D_MODEL)] * 5, compiler_params=_params(1), name="mix_fwd",
    )(x, ya_b, yb_b, ze, wap_t, wrp, wout, gam, bet)


def _mix_bwd(dx1, u1, a1, a2, ze, wap_t, wrp, wout, gam):
    s = dx1.shape[0]
    tm = ROW_TILE

    def body(dx_ref, u_ref, a1_ref, a2_ref, ze_ref, wap_ref, wrp_ref, wout_ref, g_ref,
             dres_ref, du_ref, da1_ref, da2_ref, dze_ref, dya_ref, dyb_ref, dgam_ref, dbet_ref):
        @pl.when(pl.program_id(0) == 0)
        def _():
            dgam_ref[...] = jnp.zeros_like(dgam_ref)
            dbet_ref[...] = jnp.zeros_like(dbet_ref)

        dx = dx_ref[...]
        xh, rstd = _norm_rows(u_ref[...], LN_EPS)
        dgam_ref[...] += jnp.sum(dx * xh, axis=0, keepdims=True)
        dbet_ref[...] += jnp.sum(dx, axis=0, keepdims=True)
        du = _norm_rows_bwd(dx * g_ref[...], xh, rstd)
        dres_ref[...] = ALPHA * du
        du_ref[...] = du.astype(du_ref.dtype)
        dm = _dot(du, wout_ref[...], NT)
        ze_ = ze_ref[...]
        sa, sb = _sigmoid(ze_[:, :D_MODEL]), _sigmoid(ze_[:, D_MODEL:])
        da1, da2 = dm * sa, dm * sb
        dze_ref[...] = jnp.concatenate([dm * a1_ref[...] * (sa * (1.0 - sa)),
                                        dm * a2_ref[...] * (sb * (1.0 - sb))], axis=1)
        da1_ref[...] = da1.astype(da1_ref.dtype)
        da2_ref[...] = da2.astype(da2_ref.dtype)
        dya_ref[...] = _dot(da1, wap_ref[...], NN)
        dyb_ref[...] = _dot(da2, wrp_ref[...], NT)

    row = lambda w: pl.BlockSpec((tm, w), lambda i: (i, 0))
    full = _resident
    vec = pl.BlockSpec((1, D_MODEL), lambda i: (0, 0))
    f = lambda w: jax.ShapeDtypeStruct((s, w), F32)
    m = lambda w: jax.ShapeDtypeStruct((s, w), MXU_DTYPE)
    v = jax.ShapeDtypeStruct((1, D_MODEL), F32)
    return pl.pallas_call(
        body,
        out_shape=[f(D_MODEL), m(D_MODEL), m(D_MODEL), m(D_MODEL), f(2 * D_MODEL), f(ATTN_OUT),
                   f(RET_HEADS * RET_V), v, v],
        grid=(s // tm,),
        in_specs=[row(D_MODEL)] * 4 + [row(2 * D_MODEL), full(wap_t), full(wrp), full(wout), full(gam)],
        out_specs=[row(D_MODEL)] * 4 + [row(2 * D_MODEL), row(ATTN_OUT), row(RET_HEADS * RET_V), vec, vec],
        compiler_params=_params(1), name="mix_bwd",
    )(dx1, u1, a1, a2, ze, wap_t, wrp, wout, gam)


FF_CHUNK = 1408


def _ffn_fwd(x1, wg_t, wu_t, wd, gam, bet):
    s = x1.shape[0]
    tm, fc = ROW_TILE, FF_CHUNK

    def body(x_ref, wg_ref, wu_ref, wd_ref, g_ref, b_ref, x2_ref, u2_ref, a_ref, b_out_ref, h_ref):
        xv = x_ref[...]
        u = ALPHA * xv
        for f0 in range(0, D_FF, fc):
            ch = slice(f0, f0 + fc)
            a = _dot(xv, wg_ref[ch, :], NT)
            b = _dot(xv, wu_ref[ch, :], NT)
            hid = a * _sigmoid(a) * b
            u = u + _dot(hid, wd_ref[ch, :], NN)
            a_ref[:, ch] = a
            b_out_ref[:, ch] = b
            h_ref[:, ch] = hid.astype(h_ref.dtype)
        xh, _ = _norm_rows(u, LN_EPS)
        u2_ref[...] = u
        x2_ref[...] = xh * g_ref[...] + b_ref[...]

    row = lambda w: pl.BlockSpec((tm, w), lambda i: (i, 0))
    f = lambda w: jax.ShapeDtypeStruct((s, w), F32)
    return pl.pallas_call(
        body, out_shape=[f(D_MODEL), f(D_MODEL), f(D_FF), f(D_FF), jax.ShapeDtypeStruct((s, D_FF), MXU_DTYPE)],
        grid=(s // tm,),
        in_specs=[row(D_MODEL), _resident(wg_t), _resident(wu_t), _resident(wd), _resident(gam), _resident(bet)],
        out_specs=[row(D_MODEL), row(D_MODEL), row(D_FF), row(D_FF), row(D_FF)],
        compiler_params=_params(1), name="ffn_fwd",
    )(x1, wg_t, wu_t, wd, gam, bet)


def _ffn_bwd(dx2, u2, fa, fb, wg_t, wu_t, wd, gam):
    s = dx2.shape[0]
    tm, fc = ROW_TILE, FF_CHUNK

    def body(dx_ref, u_ref, a_ref, b_ref, wg_ref, wu_ref, wd_ref, g_ref,
             dx1_ref, du_ref, da_ref, db_ref, dgam_ref, dbet_ref):
        @pl.when(pl.program_id(0) == 0)
        def _():
            dgam_ref[...] = jnp.zeros_like(dgam_ref)
            dbet_ref[...] = jnp.zeros_like(dbet_ref)

        dx = dx_ref[...]
        xh, rstd = _norm_rows(u_ref[...], LN_EPS)
        dgam_ref[...] += jnp.sum(dx * xh, axis=0, keepdims=True)
        dbet_ref[...] += jnp.sum(dx, axis=0, keepdims=True)
        du = _norm_rows_bwd(dx * g_ref[...], xh, rstd)
        du_ref[...] = du.astype(du_ref.dtype)
        acc = ALPHA * du
        for f0 in range(0, D_FF, fc):
            ch = slice(f0, f0 + fc)
            dh = _dot(du, wd_ref[ch, :], NT)
            a, b = a_ref[:, ch], b_ref[:, ch]
            sg = _sigmoid(a)
            da = dh * b * (sg * (1.0 + a * (1.0 - sg)))
            db = dh * (a * sg)
            acc = acc + _dot(da, wg_ref[ch, :], NN) + _dot(db, wu_ref[ch, :], NN)
            da_ref[:, ch] = da.astype(da_ref.dtype)
            db_ref[:, ch] = db.astype(db_ref.dtype)
        dx1_ref[...] = acc

    row = lambda w: pl.BlockSpec((tm, w), lambda i: (i, 0))
    vec = pl.BlockSpec((1, D_MODEL), lambda i: (0, 0))
    v = jax.ShapeDtypeStruct((1, D_MODEL), F32)
    return pl.pallas_call(
        body,
        out_shape=[jax.ShapeDtypeStruct((s, D_MODEL), F32), jax.ShapeDtypeStruct((s, D_MODEL), MXU_DTYPE),
                   jax.ShapeDtypeStruct((s, D_FF), MXU_DTYPE), jax.ShapeDtypeStruct((s, D_FF), MXU_DTYPE), v, v],
        grid=(s // tm,),
        in_specs=[row(D_MODEL), row(D_MODEL), row(D_FF), row(D_FF), _resident(wg_t), _resident(wu_t),
                  _resident(wd), _resident(gam)],
        out_specs=[row(D_MODEL), row(D_MODEL), row(D_FF), row(D_FF), vec, vec],
        compiler_params=_params(1), name="ffn_bwd",
    )(dx2, u2, fa, fb, wg_t, wu_t, wd, gam)


def _loss_head(y, target):
    s = y.shape[0]
    tr = _pick(s, (512, 256, 128))

    def body(y_ref, t_ref, dy_ref, l_ref):
        @pl.when(pl.program_id(0) == 0)
        def _():
            l_ref[...] = jnp.zeros_like(l_ref)

        e = y_ref[...] - t_ref[...]
        dy_ref[...] = e * (1.0 / D_MODEL)
        part = jnp.sum(jnp.sum(e * e, axis=1, keepdims=True), axis=0, keepdims=True)
        l_ref[...] += part * (0.5 / D_MODEL)

    spec = pl.BlockSpec((tr, D_MODEL), lambda i: (i, 0))
    dy, part = pl.pallas_call(
        body, out_shape=[jax.ShapeDtypeStruct((s, D_MODEL), F32), jax.ShapeDtypeStruct((8, BLK), F32)],
        grid=(s // tr,), in_specs=[spec, spec], out_specs=[spec, pl.BlockSpec((8, BLK), lambda i: (0, 0))],
        compiler_params=_params(1), name="loss_head",
    )(y, target)
    return dy, part[0, 0]


def _adamw(w, g, m, v, name):
    rows, cols = w.shape
    budget = 1 << 20
    cands = [t for t in range(8, rows + 1, 8) if rows % t == 0 and t * cols * 4 <= budget]
    tr = max(cands) if cands else rows

    def body(w_ref, g_ref, m_ref, v_ref, d_ref, nm_ref, nv_ref):
        gv = g_ref[...]
        mn = ADAM_B1 * m_ref[...] + (1.0 - ADAM_B1) * gv
        vn = ADAM_B2 * v_ref[...] + (1.0 - ADAM_B2) * (gv * gv)
        m_hat = mn / (1.0 - ADAM_B1 ** ADAM_STEP)
        v_hat = vn / (1.0 - ADAM_B2 ** ADAM_STEP)
        d_ref[...] = -ADAM_LR * (m_hat / (jnp.sqrt(v_hat) + ADAM_EPS) + ADAM_WD * w_ref[...])
        nm_ref[...] = mn
        nv_ref[...] = vn

    spec = pl.BlockSpec((tr, cols), lambda i: (i, 0))
    shape = jax.ShapeDtypeStruct((rows, cols), F32)
    return pl.pallas_call(
        body, out_shape=[shape, shape, shape], grid=(rows // tr,), in_specs=[spec] * 4, out_specs=[spec] * 3,
        compiler_params=_params(1), name=name,
    )(w, g, m, v)


MESH_ID = pl.DeviceIdType.MESH
ANY = pl.BlockSpec(memory_space=pl.ANY)


def _place():
    x, y, c = lax.axis_index("x"), lax.axis_index("y"), lax.axis_index("c")
    other_chips = [(1 - x, y), (x, 1 - y), (1 - x, 1 - y)]
    return x, y, c, other_chips


def _chip_no(chip):
    return 2 * chip[0] + chip[1]


def _gather_exchange(f_ref, g_ref, send_sems, recv_sems, hr):
    x, y, c, chips = _place()
    sibling = (x, y, 1 - c)

    def piece(chip, half):
        return g_ref.at[_chip_no(chip), pl.ds(half * hr, hr), :]

    def copy(k, src, dst, to):
        return pltpu.make_async_remote_copy(src_ref=src, dst_ref=dst, send_sem=send_sems.at[k],
                                            recv_sem=recv_sems.at[k], device_id=to, device_id_type=MESH_ID)

    own = copy(6, f_ref, g_ref.at[_chip_no((x, y))], sibling)
    own.start()
    my_half = f_ref.at[pl.ds(c * hr, hr), :]
    first = [copy(k, my_half, piece((x, y), c), (*chip, c)) for k, chip in enumerate(chips)]
    for cp in first:
        cp.start()
    passed = [copy(3 + k, piece(chip, c), piece(chip, c), sibling) for k, chip in enumerate(chips)]
    for k, chip in enumerate(chips):
        copy(k, my_half, piece(chip, c), (*chip, c)).wait_recv()
        passed[k].start()
    for k, chip in enumerate(chips):
        copy(3 + k, my_half, piece(chip, 1 - c), sibling).wait_recv()
    for cp in first + passed:
        cp.wait_send()
    own.wait()


GATHER_SEMS = (pltpu.SemaphoreType.DMA((7,)), pltpu.SemaphoreType.DMA((7,)))


HBM = pltpu.MemorySpace.HBM
ONE_SEM_PAIR = (pltpu.SemaphoreType.DMA, pltpu.SemaphoreType.DMA)


def _sibling_of_me():
    x, y, c, _ = _place()
    return [(x, y, 1 - c)]


def _same_core_of_other_chips():
    x, y, c, chips = _place()
    return [(*chip, c) for chip in chips]


def _on_sequencer(name, collective_id, sems, peers, exchange):
    @pl.kernel(mesh=plsc.ScalarSubcoreMesh(axis_name="sequencer", num_cores=1), name=name, scratch_types=sems,
               compiler_params=pltpu.CompilerParams(collective_id=collective_id))
    def launch(*sem_refs):
        barrier = pltpu.get_barrier_semaphore()
        devices = peers()
        for peer in devices:
            pl.semaphore_signal(barrier, inc=1, device_id=peer, device_id_type=MESH_ID)
        pl.semaphore_wait(barrier, len(devices))
        exchange(*sem_refs)

    launch()


def _all_gather_rows(flat, behind=None):
    r, cols = flat.shape
    out = jax.ShapeDtypeStruct((N_CHIPS, r, cols), flat.dtype)
    if behind is not None:
        f_ref, g_ref = jax.new_ref(flat, memory_space=HBM), jax.empty_ref(out, memory_space=HBM)
        _on_sequencer(f"all_gather_weights_behind_{behind[1]}", behind[0], GATHER_SEMS,
                      lambda: _sibling_of_me() + _same_core_of_other_chips(),
                      lambda s, r_: _gather_exchange(f_ref, g_ref, s, r_, r // 2))
        return g_ref[...]

    def body(f_ref, g_ref, send_sems, recv_sems):
        _gather_exchange(f_ref, g_ref, send_sems, recv_sems, r // 2)

    return pl.pallas_call(body, out_shape=out, in_specs=[ANY], out_specs=ANY, scratch_shapes=list(GATHER_SEMS),
                          name="all_gather_weights")(flat)


def _swap_halves(gall, behind=None):
    _, r, cols = gall.shape
    hr = r // 2
    out = jax.ShapeDtypeStruct((N_CHIPS, hr, cols), gall.dtype)

    def exchange(g_ref, a_ref, send_sem, recv_sem):
        x, y, c, _ = _place()
        cp = pltpu.make_async_remote_copy(src_ref=g_ref.at[:, pl.ds((1 - c) * hr, hr), :], dst_ref=a_ref,
                                          send_sem=send_sem, recv_sem=recv_sem, device_id=(x, y, 1 - c),
                                          device_id_type=MESH_ID)
        cp.start()
        cp.wait()

    if behind is not None:
        g_ref, a_ref = jax.new_ref(gall, memory_space=HBM), jax.empty_ref(out, memory_space=HBM)
        _on_sequencer(f"grad_swap_halves_behind_{behind[1]}", behind[0], ONE_SEM_PAIR, _sibling_of_me,
                      lambda s, r_: exchange(g_ref, a_ref, s, r_))
        return a_ref[...]

    def body(g_ref, a_ref, send_sem, recv_sem):
        exchange(g_ref, a_ref, send_sem, recv_sem)

    return pl.pallas_call(body, out_shape=out, in_specs=[ANY], out_specs=ANY, scratch_shapes=list(ONE_SEM_PAIR),
                          name="grad_swap_halves")(gall)


def _scatter_to_chips(p, behind=None):
    _, hr, cols = p.shape
    out = jax.ShapeDtypeStruct((3, hr, cols), p.dtype)
    sems = (pltpu.SemaphoreType.DMA((3,)), pltpu.SemaphoreType.DMA((3,)))

    def exchange(p_ref, b_ref, send_sems, recv_sems):
        x, y, c, chips = _place()
        cps = [pltpu.make_async_remote_copy(src_ref=p_ref.at[_chip_no(chip)], dst_ref=b_ref.at[k],
                                            send_sem=send_sems.at[k], recv_sem=recv_sems.at[k],
                                            device_id=(*chip, c), device_id_type=MESH_ID)
               for k, chip in enumerate(chips)]
        for cp in cps:
            cp.start()
        for cp in cps:
            cp.wait()

    if behind is not None:
        p_ref, b_ref = jax.new_ref(p, memory_space=HBM), jax.empty_ref(out, memory_space=HBM)
        _on_sequencer(f"grad_scatter_chips_behind_{behind[1]}", behind[0], sems, _same_core_of_other_chips,
                      lambda s, r_: exchange(p_ref, b_ref, s, r_))
        return b_ref[...]

    def body(p_ref, b_ref, send_sems, recv_sems):
        exchange(p_ref, b_ref, send_sems, recv_sems)

    return pl.pallas_call(body, out_shape=out, in_specs=[ANY], out_specs=ANY, scratch_shapes=list(sems),
                          name="grad_scatter_chips")(p)


def _share_with_sibling(full, behind=None):
    r, cols = full.shape
    hr = r // 2

    def exchange(in_ref, out_ref, send_sem, recv_sem):
        x, y, c, _ = _place()
        cp = pltpu.make_async_remote_copy(src_ref=in_ref.at[pl.ds(c * hr, hr), :],
                                          dst_ref=out_ref.at[pl.ds(c * hr, hr), :],
                                          send_sem=send_sem, recv_sem=recv_sem, device_id=(x, y, 1 - c),
                                          device_id_type=MESH_ID)
        cp.start()
        cp.wait()

    if behind is not None:
        full_ref = jax.new_ref(full, memory_space=HBM)
        _on_sequencer(f"grad_share_sibling_behind_{behind[1]}", behind[0], ONE_SEM_PAIR, _sibling_of_me,
                      lambda s, r_: exchange(full_ref, full_ref, s, r_))
        return full_ref[...]

    def body(in_ref, out_ref, send_sem, recv_sem):
        exchange(in_ref, out_ref, send_sem, recv_sem)

    return pl.pallas_call(
        body, out_shape=jax.ShapeDtypeStruct((r, cols), full.dtype), in_specs=[ANY], out_specs=ANY,
        input_output_aliases={0: 0}, scratch_shapes=list(ONE_SEM_PAIR), name="grad_share_sibling",
    )(full)


def _all_reduce_small(v):
    r, cols = v.shape
    n_dev = 8

    def body(x_ref, out_ref, gat_ref, send_sems, recv_sems, local_sem):
        x, y, c, chips = _place()
        me, sibling = (x, y, c), (x, y, 1 - c)

        def slot(px, py, pc):
            return gat_ref.at[4 * px + 2 * py + pc]

        def copy(k, block, to, src=None):
            return pltpu.make_async_remote_copy(src_ref=slot(*block) if src is None else src, dst_ref=slot(*block),
                                                send_sem=send_sems.at[k], recv_sem=recv_sems.at[k], device_id=to,
                                                device_id_type=MESH_ID)

        mine = pltpu.make_async_copy(x_ref, slot(*me), local_sem)
        mine.start()
        first = [copy(0, me, sibling, src=x_ref)]
        first += [copy(1 + k, me, (*chip, c), src=x_ref) for k, chip in enumerate(chips)]
        for cp in first:
            cp.start()
        passed = [copy(4 + k, (*chip, c), sibling) for k, chip in enumerate(chips)]
        for k, chip in enumerate(chips):
            copy(1 + k, (*chip, c), me).wait_recv()
            passed[k].start()
        copy(0, sibling, me).wait_recv()
        for k, chip in enumerate(chips):
            copy(4 + k, (*chip, 1 - c), me).wait_recv()
        for cp in first + passed:
            cp.wait_send()
        mine.wait()
        acc = gat_ref[0]
        for d in range(1, n_dev):
            acc = acc + gat_ref[d]
        out_ref[...] = acc

    vmem = pl.BlockSpec(memory_space=pltpu.VMEM)
    return pl.pallas_call(
        body, out_shape=jax.ShapeDtypeStruct((r, cols), v.dtype), in_specs=[vmem], out_specs=vmem,
        scratch_shapes=[pltpu.VMEM((n_dev, r, cols), v.dtype), pltpu.SemaphoreType.DMA((7,)),
                        pltpu.SemaphoreType.DMA((7,)), pltpu.SemaphoreType.DMA],
        name="all_reduce_small",
    )(v)


WIRE_DTYPE = jnp.bfloat16


def _add_own_half(gall, got, place):
    _, r, cols = gall.shape
    hr = r // 2
    tr = _pick(hr, (256, 128, 64, 32, 16, 8))
    g4 = gall.reshape(N_CHIPS, 2, hr, cols)

    def body(s_ref, g_ref, a_ref, o_ref):
        o_ref[...] = (g_ref[0] + a_ref[...]).astype(o_ref.dtype)

    return pl.pallas_call(
        body, out_shape=jax.ShapeDtypeStruct((N_CHIPS, hr, cols), WIRE_DTYPE),
        grid_spec=pltpu.PrefetchScalarGridSpec(
            num_scalar_prefetch=1, grid=(N_CHIPS, hr // tr),
            in_specs=[pl.BlockSpec((1, 1, tr, cols), lambda j, i, s: (j, s[1], i, 0)),
                      pl.BlockSpec((1, tr, cols), lambda j, i, s: (j, i, 0))],
            out_specs=pl.BlockSpec((1, tr, cols), lambda j, i, s: (j, i, 0))),
        compiler_params=_params(2), name="grad_add_halves",
    )(place, g4, got)


def _add_chip_parts(gall, got, parts, place):
    _, r, cols = gall.shape
    hr = r // 2
    tr = _pick(hr, (256, 128, 64, 32, 16, 8))
    g4 = gall.reshape(N_CHIPS, 2, hr, cols)
    nt = hr // tr

    def body(s_ref, g_ref, a_ref, b0_ref, b1_ref, b2_ref, o_ref):
        own = g_ref[0, 0] + a_ref[0]
        o_ref[...] = ((own + b0_ref[0].astype(F32)) + b1_ref[0].astype(F32)) + b2_ref[0].astype(F32)

    part = lambda k: pl.BlockSpec((1, tr, cols), lambda i, s: (k, i, 0))
    return pl.pallas_call(
        body, out_shape=jax.ShapeDtypeStruct((r, cols), F32),
        grid_spec=pltpu.PrefetchScalarGridSpec(
            num_scalar_prefetch=1, grid=(nt,),
            in_specs=[pl.BlockSpec((1, 1, tr, cols), lambda i, s: (s[0], s[1], i, 0)),
                      pl.BlockSpec((1, tr, cols), lambda i, s: (s[0], i, 0)), part(0), part(1), part(2)],
            out_specs=pl.BlockSpec((tr, cols), lambda i, s: (s[1] * nt + i, 0))),
        compiler_params=_params(1), name="grad_add_chips",
    )(place, g4, got, parts, parts, parts)


def _pack_shards(weights, l, group, dtype):
    parts = []
    for name, _ in group:
        w = weights[name][l]
        parts.append((w.T if name in COLUMN_SHARDED else w).reshape(-1, D_MODEL).astype(dtype))
    return parts[0] if len(parts) == 1 else jnp.concatenate(parts, axis=0)


def _unpack_gathered(gathered, group):
    w, off = {}, 0
    for name, rows in group:
        blk = gathered[:, off:off + rows]
        if name == "w_attn_proj":
            w[name] = blk.reshape(N_CHIPS * 256, ATTN_OUT)
        else:
            w[name] = blk.reshape(N_CHIPS * rows, D_MODEL)
        off += rows
    return w


def _pack_full_grads(g, group):
    parts = [g[name].reshape(N_CHIPS, rows, D_MODEL) for name, rows in group]
    return parts[0] if len(parts) == 1 else jnp.concatenate(parts, axis=1)


def _unpack_shard_grads(fulls):
    out = {}
    for gi, group in enumerate(PACK_GROUPS):
        off = 0
        for name, rows in group:
            blks = [fulls[(l, gi)][off:off + rows] for l in range(DEPTH)]
            out[name] = jnp.stack([b.reshape(256, ATTN_OUT) if name == "w_attn_proj" else b for b in blks])
            off += rows
    return out


class _StaticWeights:
    def __init__(self, layers):
        self.layers = layers

    def group(self, l, gi):
        return self.layers[l]

    def layer(self, l):
        return self.layers[l]


class _WeightGathers:
    def __init__(self, weights):
        self.flats = {(l, gi): _pack_shards(weights, l, g, MXU_DTYPE)
                      for l in range(DEPTH) for gi, g in enumerate(PACK_GROUPS)}
        self.order = sorted(self.flats)
        self.landed, self.unpacked, self.issued = {}, {}, 0
        self._issue()

    def _issue(self, after=None):
        key = self.order[self.issued]
        flat = self.flats[key]
        if self.issued == 0:
            self.landed[key] = _all_gather_rows(flat)
        else:
            prev = self.order[self.issued - 1]
            ties = (self.landed[prev], flat) if after is None else (self.landed[prev], flat, after)
            tied = lax.optimization_barrier(ties)
            self.landed[prev], flat = tied[0], tied[1]
            after = tied[2] if after is not None else None
            self.landed[key] = _all_gather_rows(flat, behind=(1, f"l{key[0]}g{key[1]}"))
        self.issued += 1
        return after

    def prefetch(self, after):
        return self._issue(after) if self.issued < len(self.order) else after

    def group(self, l, gi):
        key = (l, gi)
        if key not in self.unpacked:
            self.unpacked[key] = _unpack_gathered(self.landed[key], PACK_GROUPS[gi])
        return self.unpacked[key]

    def layer(self, l):
        w = {}
        for gi in range(len(PACK_GROUPS)):
            w.update(self.group(l, gi))
        return w


class _GradReduction:
    LAST = 3

    def __init__(self, gw, group, place, behind):
        self.gall, self.place, self.behind, self.stage = _pack_full_grads(gw, group), place, behind, 0
        self.pending = None

    def _how(self, k):
        return None if self.behind is None else (self.behind[0][k], self.behind[1])

    def advance(self, value=None):
        if self.stage > self.LAST:
            return value
        if self.stage > 0 and value is not None:
            value, self.pending = lax.optimization_barrier((value, self.pending))
        if self.stage == 0:
            self.pending = _swap_halves(self.gall, self._how(0))
        elif self.stage == 1:
            self.got = self.pending
            self.pending = _scatter_to_chips(_add_own_half(self.gall, self.got, self.place), self._how(1))
        elif self.stage == 2:
            self.pending = _share_with_sibling(
                _add_chip_parts(self.gall, self.got, self.pending, self.place), self._how(2))
        self.stage += 1
        return value

    def result(self):
        while self.stage < self.LAST:
            self.advance()
        return self.pending


class _GradReductions:
    def __init__(self, place):
        self.place, self.all = place, {}

    def start(self, l, gi, gw):
        hidden = (l, gi) != (0, IN_GROUP)
        slot = len(self.all) % 2
        how = ((2 + 3 * slot, 3 + 3 * slot, 4 + 3 * slot), f"l{l}g{gi}") if hidden else None
        red = _GradReduction(gw, PACK_GROUPS[gi], self.place, how)
        if hidden:
            red.advance()
        self.all[(l, gi)] = red

    def tick(self, value):
        for red in self.all.values():
            if red.behind is not None:
                value = red.advance(value)
        return value

    def results(self):
        return {key: red.result() for key, red in self.all.items()}


def _attn_views(za, s):
    views = []
    for g, (_, dil) in enumerate(ATTN_GROUPS):
        if dil == 1:
            views.append((za, za, za, 3 * N_GROUPS, (g, N_GROUPS + g, 2 * N_GROUPS + g)))
        else:
            rows = s // dil
            part = lambda k: za[:, k * ATTN_WIDTH + g * ATTN_OUT:k * ATTN_WIDTH + (g + 1) * ATTN_OUT].reshape(
                rows, dil * ATTN_OUT)
            views.append((part(0), part(1), part(2), 1, (0, 0, 0)))
    return views


def _layer_fwd(x, weights, l, lnp, bias, rc, prefetch):
    s = x.shape[0]
    tag = f"l{l}"
    w_in = weights.group(l, IN_GROUP)["w_in"]
    z = {}
    for name, off, width, narrow in SECTIONS:
        z[name] = _matmul(x, w_in[off:off + width], "nt", bias=lnp["b_in"][:, off:off + width],
                          out_dtype=MXU_DTYPE if narrow else F32, name=f"in_proj_{name}_{tag}")
        if name == "a":
            z[name] = prefetch(z[name])
    w = weights.group(l, REST_GROUP)
    views = _attn_views(z["a"], s)
    o_list, l_list = [], []
    for g, (_, dil) in enumerate(ATTN_GROUPS):
        qa, ka, va, cb, offs = views[g]
        o, lse = _attn_fwd(qa, ka, va, cb, offs, bias, g, dil)
        o_list.append(o.reshape(s, ATTN_OUT))
        l_list.append(lse.reshape(s, ATTN_OUT))
    ya, ya_b, lt = _attn_combine(o_list, l_list)
    yb_b, o_ret, states = _ret_fwd(z["bq"], z["bk"], z["c"], z["d"], rc)
    x1, u1, a1, a2, mg_b = _mix_fwd(x, ya_b, yb_b, z["e"], w["w_attn_proj"], w["w_ret_proj"], w["w_out"],
                                    lnp["ln1_g"], lnp["ln1_b"])
    x1 = prefetch(x1)
    x2, u2, fa, fb, h_b = _ffn_fwd(x1, w["w_ffn_gate"], w["w_ffn_up"], w["w_ffn_down"], lnp["ln2_g"], lnp["ln2_b"])
    saved = dict(x=x, z=z, views=views, ya=ya, ya_b=ya_b, lt=lt, yb_b=yb_b, o_ret=o_ret, states=states,
                 x1=x1, u1=u1, a1=a1, a2=a2, mg_b=mg_b, u2=u2, fa=fa, fb=fb, h_b=h_b)
    return x2, saved


def _layer_bwd(dx2, w, l, lnp, sv, bias, rc, reductions=None):
    s = dx2.shape[0]
    tag = f"l{l}"
    z = sv["z"]
    step = reductions.tick if reductions is not None else (lambda v: v)
    dx1, du2_b, da_b, db_b, dg2, dbt2 = _ffn_bwd(dx2, sv["u2"], sv["fa"], sv["fb"], w["w_ffn_gate"], w["w_ffn_up"],
                                                  w["w_ffn_down"], lnp["ln2_g"])
    dx1 = step(dx1)
    gw = {}
    gw["w_ffn_down"] = _matmul(sv["h_b"], du2_b, "tn", name=f"dw_ffn_down_{tag}")
    gw["w_ffn_gate"] = _matmul(da_b, sv["x1"], "tn", name=f"dw_ffn_gate_{tag}")
    gw["w_ffn_up"] = _matmul(db_b, sv["x1"], "tn", name=f"dw_ffn_up_{tag}")
    dres, du1_b, da1_b, da2_b, dze, dya, dyb, dg1, dbt1 = _mix_bwd(
        dx1, sv["u1"], sv["a1"], sv["a2"], z["e"], w["w_attn_proj"], w["w_ret_proj"], w["w_out"], lnp["ln1_g"])
    gw["w_out"] = _matmul(sv["mg_b"], du1_b, "tn", name=f"dw_out_{tag}")
    gw["w_attn_proj"] = _matmul(da1_b, sv["ya_b"], "tn", name=f"dw_attn_proj_{tag}")
    gw["w_ret_proj"] = _matmul(sv["yb_b"], da2_b, "tn", name=f"dw_ret_proj_{tag}")
    dyb = step(dyb)
    if reductions is not None:
        reductions.start(l, REST_GROUP, gw)
    dzq, dzk, dzv, dzg = _ret_bwd(z["bq"], z["bk"], z["c"], z["d"], sv["o_ret"], sv["states"], dyb, rc)
    dzq = step(dzq)
    dq_l, dk_l, dv_l, dbias_l = [], [], [], []
    for g, (_, dil) in enumerate(ATTN_GROUPS):
        qa, ka, va, cb, offs = sv["views"][g]
        rows = s // dil
        view = lambda t: t.reshape(rows, dil * ATTN_OUT)
        dq, dk, dv, dbg = _attn_bwd(qa, ka, va, cb, offs, bias, view(dya), view(sv["ya"]), view(sv["lt"]), g, dil)
        dq_l.append(dq.reshape(s, ATTN_OUT))
        dk_l.append(dk.reshape(s, ATTN_OUT))
        dv_l.append(dv.reshape(s, ATTN_OUT))
        dbias_l.append(dbg)
    dz = {"a": step(jnp.concatenate(dq_l + dk_l + dv_l, axis=1)), "bq": dzq, "bk": dzk, "c": dzv, "d": dzg,
          "e": dze}
    dx = dres
    for name, off, width, _ in SECTIONS:
        dx = _matmul(dz[name], w["w_in"][off:off + width], "nn", addend=dx, name=f"dx_in_proj_{name}_{tag}")
    dx = step(dx)
    dw_rows, db_cols = [], []
    for name, off, width, _ in SECTIONS:
        dw, cs = _matmul(dz[name], sv["x"], "tn", colsum=True, name=f"dw_in_proj_{name}_{tag}")
        dw_rows.append(dw)
        db_cols.append(cs)
    gw["w_in"] = jnp.concatenate(dw_rows, axis=0)
    if reductions is not None:
        reductions.start(l, IN_GROUP, gw)
    small = dict(b_in=jnp.concatenate(db_cols, axis=1), ln1_g=dg1, ln1_b=dbt1, ln2_g=dg2, ln2_b=dbt2)
    return dx, gw, small, jnp.concatenate(dbias_l, axis=0)


def _forward_backward(x, target, rel_bias, weights, lnps, reductions=None):
    s = x.shape[0]
    bmaps = _bucket_maps()
    bias = _bias_tiles(rel_bias, bmaps)
    rc = _ret_consts(s)
    prefetch = getattr(weights, "prefetch", lambda v: v)
    saved = []
    h = x
    for l in range(DEPTH):
        h, sv = _layer_fwd(h, weights, l, lnps[l], bias, rc, prefetch)
        saved.append(sv)
    dh, loss_part = _loss_head(h, target)
    gws, smalls, dbiases = [None] * DEPTH, [None] * DEPTH, [None] * DEPTH
    for l in reversed(range(DEPTH)):
        dh, gws[l], smalls[l], dbiases[l] = _layer_bwd(dh, weights.layer(l), l, lnps[l], saved[l], bias, rc,
                                                       reductions)
    d_rel_bias = _bias_tiles_bwd(dbiases, bmaps)
    return loss_part, dh, gws, smalls, d_rel_bias


SMALL_NAMES = ("rel_bias", "b_in", "ln1_g", "ln1_b", "ln2_g", "ln2_b")
SMALL_ROWS = 32


def _pack_small(vals):
    flat = jnp.concatenate([vals[n].reshape(-1) for n in SMALL_NAMES])
    return jnp.pad(flat, (0, SMALL_ROWS * D_MODEL - flat.shape[0])).reshape(SMALL_ROWS, D_MODEL)


def _unpack_small(packed, like):
    flat = packed.reshape(-1)
    out, off = {}, 0
    for n in SMALL_NAMES:
        size = like[n].size
        out[n] = flat[off:off + size].reshape(like[n].shape)
        off += size
    return out


def kernel(x, rel_bias, w_in, b_in, w_attn_proj, w_ret_proj, w_out, ln1_g, ln1_b, w_ffn_gate, w_ffn_up, w_ffn_down, ln2_g, ln2_b, loss_target, m_rel_bias, m_w_in, m_b_in, m_w_attn_proj, m_w_ret_proj, m_w_out, m_ln1_g, m_ln1_b, m_w_ffn_gate, m_w_ffn_up, m_w_ffn_down, m_ln2_g, m_ln2_b, v_rel_bias, v_w_in, v_b_in, v_w_attn_proj, v_w_ret_proj, v_w_out, v_ln1_g, v_ln1_b, v_w_ffn_gate, v_w_ffn_up, v_w_ffn_down, v_ln2_g, v_ln2_b):
    big = dict(w_in=(w_in, m_w_in, v_w_in), w_attn_proj=(w_attn_proj, m_w_attn_proj, v_w_attn_proj),
               w_ret_proj=(w_ret_proj, m_w_ret_proj, v_w_ret_proj), w_out=(w_out, m_w_out, v_w_out),
               w_ffn_gate=(w_ffn_gate, m_w_ffn_gate, v_w_ffn_gate), w_ffn_up=(w_ffn_up, m_w_ffn_up, v_w_ffn_up),
               w_ffn_down=(w_ffn_down, m_w_ffn_down, v_w_ffn_down))
    small_w = dict(rel_bias=rel_bias, b_in=b_in, ln1_g=ln1_g, ln1_b=ln1_b, ln2_g=ln2_g, ln2_b=ln2_b)
    small_m = dict(rel_bias=m_rel_bias, b_in=m_b_in, ln1_g=m_ln1_g, ln1_b=m_ln1_b, ln2_g=m_ln2_g, ln2_b=m_ln2_b)
    small_v = dict(rel_bias=v_rel_bias, b_in=v_b_in, ln1_g=v_ln1_g, ln1_b=v_ln1_b, ln2_g=v_ln2_g, ln2_b=v_ln2_b)

    place = jnp.stack([2 * lax.axis_index("x") + lax.axis_index("y"), lax.axis_index("c")]).astype(jnp.int32)

    weights = _WeightGathers({n: w for n, (w, _, _) in big.items()})
    reductions = _GradReductions(place)
    lnps = [dict(b_in=b_in[l][None], ln1_g=ln1_g[l][None], ln1_b=ln1_b[l][None], ln2_g=ln2_g[l][None],
                 ln2_b=ln2_b[l][None]) for l in range(DEPTH)]
    loss_part, dx, _, smalls, d_rel_bias = _forward_backward(x[0], loss_target[0], rel_bias, weights, lnps,
                                                             reductions)
    loss = lax.psum(loss_part, ("x", "y", "c"))
    grads = _unpack_shard_grads(reductions.results())

    small_g = dict(rel_bias=d_rel_bias)
    for n in SMALL_NAMES[1:]:
        small_g[n] = jnp.concatenate([smalls[l][n] for l in range(DEPTH)], axis=0)
    small_g = _unpack_small(_all_reduce_small(_pack_small(small_g)), small_w)

    delta, new_m, new_v = {}, {}, {}
    for n, (w, m, v) in big.items():
        turn = (lambda t: jnp.swapaxes(t, 1, 2)) if n in COLUMN_SHARDED else (lambda t: t)
        two_d = lambda t: t.reshape(-1, t.shape[-1])
        g = grads[n]
        d_, m_, v_ = _adamw(two_d(turn(w)), two_d(g), two_d(turn(m)), two_d(turn(v)), name=f"adamw_{n}")
        grads[n], delta[n], new_m[n], new_v[n] = (turn(t.reshape(g.shape)) for t in (g, d_, m_, v_))
    d_, m_, v_ = _adamw(_pack_small(small_w), _pack_small(small_g), _pack_small(small_m), _pack_small(small_v),
                        name="adamw_small")
    delta.update(_unpack_small(d_, small_w))
    new_m.update(_unpack_small(m_, small_w))
    new_v.update(_unpack_small(v_, small_w))
    grads.update(small_g)

    order = ("rel_bias", "w_in", "b_in", "w_attn_proj", "w_ret_proj", "w_out", "ln1_g", "ln1_b", "w_ffn_gate",
             "w_ffn_up", "w_ffn_down", "ln2_g", "ln2_b")
    return (loss, dx[None], *[grads[n] for n in order], *[delta[n] for n in order], *[new_m[n] for n in order],
            *[new_v[n] for n in order])
```

```python
import functools

import numpy as np
import jax
import jax.numpy as jnp
from jax import lax
from jax.experimental import pallas as pl
from jax.experimental.pallas import tpu as pltpu
from jax.experimental.pallas import tpu_sc as plsc

F32 = jnp.float32
MXU_DTYPE = jnp.bfloat16

DEPTH = 2
D_MODEL = 1024
HEAD_DIM = 64
ATTN_GROUPS = ((128, 1), (512, 4), (2048, 16))
N_GROUPS = len(ATTN_GROUPS)
HEADS_PER_GROUP = 6
N_ATTN_HEADS = 18
ATTN_WIDTH = 1152
ATTN_OUT = 384
NUM_BUCKETS = 32
MAX_DISTANCE = 2048
RET_HEADS = 4
RET_QK = 256
RET_V = 512
RET_CHUNK = 128
ROPE_BASE = 10000.0
D_FF = 2816
IN_COLS = 11648
ALPHA = (2 * DEPTH) ** 0.25
LN_EPS = 1e-5
GN_EPS = 1e-5
ADAM_LR, ADAM_B1, ADAM_B2, ADAM_EPS, ADAM_WD, ADAM_STEP = 0.001, 0.9, 0.999, 1e-08, 0.01, 10

BLK = 128
NEG = -1e30
N_CHIPS = 4
VMEM_LIMIT = 48 * 1024 * 1024

SECTIONS = (
    ("a", 0, 3456, True),
    ("bq", 3456, 1024, False),
    ("bk", 4480, 1024, False),
    ("c", 5504, 2048, True),
    ("d", 7552, 2048, False),
    ("e", 9600, 2048, False),
)
PACK_GROUPS = ((("w_in", 2912),),
               (("w_attn_proj", 96), ("w_ret_proj", 512), ("w_out", 256), ("w_ffn_gate", 704), ("w_ffn_up", 704),
                ("w_ffn_down", 704)))
IN_GROUP, REST_GROUP = 0, 1
COLUMN_SHARDED = ("w_in", "w_attn_proj", "w_ffn_gate", "w_ffn_up")

NN = ((1,), (0,))
NT = ((1,), (1,))
TN = ((0,), (0,))


def _dot(a, b, dims):
    return lax.dot_general(a.astype(MXU_DTYPE), b.astype(MXU_DTYPE), (dims, ((), ())),
                           preferred_element_type=F32)


def _pick(n, prefs):
    for p in prefs:
        if n % p == 0:
            return p
    raise ValueError(f"no tile for {n} among {prefs}")


def _row_tile(n, most=512, unit=16):
    return max(t for t in range(unit, most + 1, unit) if n % t == 0)


TOKEN_TILES = (1024, 512, 256, 128)
FEATURE_TILES = (1152, 1024, 1408, 384, 256, 128)


def _params(n_axes, limit=VMEM_LIMIT):
    return pltpu.CompilerParams(dimension_semantics=("arbitrary",) * n_axes, vmem_limit_bytes=limit)


def _resident(a):
    return pl.BlockSpec(a.shape, lambda i: (0,) * a.ndim, pipeline_mode=pl.Buffered(1))


def _sigmoid(x):
    return 1.0 / (1.0 + jnp.exp(-x))


def _norm_rows(u, eps):
    mu = jnp.mean(u, axis=-1, keepdims=True)
    xc = u - mu
    var = jnp.mean(xc * xc, axis=-1, keepdims=True)
    rstd = lax.rsqrt(var + eps)
    return xc * rstd, rstd


def _norm_rows_bwd(dxh, xh, rstd):
    c1 = jnp.mean(dxh, axis=-1, keepdims=True)
    c2 = jnp.mean(dxh * xh, axis=-1, keepdims=True)
    return rstd * (dxh - c1 - xh * c2)


def _matmul(a, b, mode, *, name, out_dtype=F32, bias=None, addend=None, colsum=False):
    if mode == "tn":
        kd, m = a.shape
        n = b.shape[1]
        tm, tn_, tk = _pick(m, FEATURE_TILES), _pick(n, FEATURE_TILES), _pick(kd, TOKEN_TILES)
        a_spec = pl.BlockSpec((tk, tm), lambda i, j, k: (k, i))
        b_spec = pl.BlockSpec((tk, tn_), lambda i, j, k: (k, j))
    else:
        m, kd = a.shape
        n = b.shape[0] if mode == "nt" else b.shape[1]
        tm, tn_, tk = _pick(m, TOKEN_TILES), _pick(n, FEATURE_TILES), _pick(kd, FEATURE_TILES)
        a_spec = pl.BlockSpec((tm, tk), lambda i, j, k: (i, k))
        if mode == "nt":
            b_spec = pl.BlockSpec((tn_, tk), lambda i, j, k: (j, k))
        else:
            b_spec = pl.BlockSpec((tk, tn_), lambda i, j, k: (k, j))
    dims = {"nn": NN, "nt": NT, "tn": TN}[mode]
    nk = kd // tk
    has_bias, has_add = bias is not None, addend is not None
    assert not colsum or mode == "tn"

    def body(*refs):
        it = iter(refs)
        a_ref, b_ref = next(it), next(it)
        bias_ref = next(it) if has_bias else None
        add_ref = next(it) if has_add else None
        o_ref = next(it)
        cs_ref = next(it) if colsum else None
        acc_ref = next(it)
        j, k = pl.program_id(1), pl.program_id(2)

        @pl.when(k == 0)
        def _():
            if has_add:
                acc_ref[...] = add_ref[...].astype(F32)
            else:
                acc_ref[...] = jnp.zeros_like(acc_ref)

        av = a_ref[...]
        acc_ref[...] += _dot(av, b_ref[...], dims)
        if colsum:
            @pl.when(jnp.logical_and(j == 0, k == 0))
            def _():
                cs_ref[...] = jnp.zeros_like(cs_ref)

            @pl.when(j == 0)
            def _():
                cs_ref[...] += jnp.sum(av.astype(F32), axis=0, keepdims=True)

        @pl.when(k == nk - 1)
        def _():
            r = acc_ref[...]
            if has_bias:
                r = r + bias_ref[...]
            o_ref[...] = r.astype(out_dtype)

    in_specs, args = [a_spec, b_spec], [a, b]
    if has_bias:
        in_specs.append(pl.BlockSpec((1, tn_), lambda i, j, k: (0, j)))
        args.append(bias)
    if has_add:
        in_specs.append(pl.BlockSpec((tm, tn_), lambda i, j, k: (i, j)))
        args.append(addend)
    out_shape = [jax.ShapeDtypeStruct((m, n), out_dtype)]
    out_specs = [pl.BlockSpec((tm, tn_), lambda i, j, k: (i, j))]
    if colsum:
        out_shape.append(jax.ShapeDtypeStruct((1, m), F32))
        out_specs.append(pl.BlockSpec((1, tm), lambda i, j, k: (0, i)))
    res = pl.pallas_call(
        body, out_shape=out_shape, grid=(m // tm, n // tn_, nk), in_specs=in_specs, out_specs=out_specs,
        scratch_shapes=[pltpu.VMEM((tm, tn_), F32)], compiler_params=_params(3), name=name,
    )(*args)
    return res if colsum else res[0]


def _t5_bucket(dist):
    max_exact = NUM_BUCKETS // 2
    large = max_exact + (np.log(np.maximum(dist, max_exact) / max_exact)
                         / np.log(MAX_DISTANCE / max_exact) * (NUM_BUCKETS - max_exact)).astype(np.int32)
    large = np.minimum(large, NUM_BUCKETS - 1)
    return np.where(dist < max_exact, dist, large).astype(np.int32)


def _bucket_maps():
    qi = np.arange(BLK)[:, None]
    kj = np.arange(2 * BLK)[None, :]
    rel = np.clip(qi + BLK - kj, 0, BLK)
    return jnp.asarray(np.stack([_t5_bucket(rel * d) for _, d in ATTN_GROUPS]))


def _bias_tiles(rel_bias, bmaps):
    def body(tab_ref, bm_ref, o_ref):
        h = pl.program_id(0)
        bm = bm_ref[0]
        acc = jnp.zeros((BLK, 2 * BLK), F32)
        for b in range(NUM_BUCKETS):
            acc = jnp.where(bm == b, tab_ref[b, h], acc)
        o_ref[0] = acc

    return pl.pallas_call(
        body, out_shape=jax.ShapeDtypeStruct((N_ATTN_HEADS, BLK, 2 * BLK), F32), grid=(N_ATTN_HEADS,),
        in_specs=[pl.BlockSpec(memory_space=pltpu.SMEM),
                  pl.BlockSpec((1, BLK, 2 * BLK), lambda h: (h // HEADS_PER_GROUP, 0, 0))],
        out_specs=pl.BlockSpec((1, BLK, 2 * BLK), lambda h: (h, 0, 0)),
        compiler_params=_params(1), name="bias_tiles",
    )(rel_bias, bmaps)


def _bias_tiles_bwd(dbias_layers, bmaps):
    nl = len(dbias_layers)

    def body(*refs):
        bm = refs[nl][0]
        o_ref = refs[nl + 1]
        x = refs[0][0]
        for r in refs[1:nl]:
            x = x + r[0]
        lane = lax.broadcasted_iota(jnp.int32, (1, BLK), 1)
        row = jnp.zeros((1, BLK), F32)
        for b in range(NUM_BUCKETS):
            s = jnp.sum(jnp.where(bm == b, x, 0.0), axis=1, keepdims=True)
            s = jnp.sum(s, axis=0, keepdims=True)
            row = jnp.where(lane == b, s, row)
        o_ref[0] = row

    tile = pl.BlockSpec((1, BLK, 2 * BLK), lambda h: (h, 0, 0))
    out = pl.pallas_call(
        body, out_shape=jax.ShapeDtypeStruct((N_ATTN_HEADS, 1, BLK), F32), grid=(N_ATTN_HEADS,),
        in_specs=[tile] * nl + [pl.BlockSpec((1, BLK, 2 * BLK), lambda h: (h // HEADS_PER_GROUP, 0, 0))],
        out_specs=pl.BlockSpec((1, 1, BLK), lambda h: (h, 0, 0)),
        compiler_params=_params(1), name="bias_tiles_bwd",
    )(*dbias_layers, bmaps)
    return out[:, 0, :NUM_BUCKETS].T


def _pair_masks():
    lane = lax.broadcasted_iota(jnp.int32, (BLK, BLK), 1)
    row2 = lax.broadcasted_iota(jnp.int32, (2 * BLK, BLK), 0)
    lane2 = lax.broadcasted_iota(jnp.int32, (2 * BLK, BLK), 1)
    own = (lane2 // HEAD_DIM) == (row2 // BLK)
    qi = lax.broadcasted_iota(jnp.int32, (2 * BLK, 2 * BLK), 0) & (BLK - 1)
    kj = lax.broadcasted_iota(jnp.int32, (2 * BLK, 2 * BLK), 1)
    band = jnp.logical_and(kj >= qi, kj <= qi + BLK)
    return lane < HEAD_DIM, own, band, kj < BLK


def _pair_scores(q32, kb, bias2, own, band, is_prev, pen):
    qm = jnp.where(own, jnp.concatenate([q32, q32], axis=0), 0.0)
    s = _dot(qm, kb, NT) * (HEAD_DIM ** -0.5) + bias2
    if pen is not None:
        s = s + jnp.where(is_prev, pen, 0.0)
    return jnp.where(band, s, NEG), qm


HEAD_PAIRS = HEADS_PER_GROUP // 2


def _attn_fwd(qa, ka, va, cb, offs, bias, g, dil):
    rows = qa.shape[0]
    nb = rows // BLK
    rb = 2 if nb % 2 == 0 else 1
    oq, ok, ov = offs

    def cur(off):
        return pl.BlockSpec((rb * BLK, ATTN_OUT), lambda r, n: (n, r * cb + off))

    def prev(off):
        return pl.BlockSpec((BLK, ATTN_OUT), lambda r, n: (jnp.maximum(rb * n - 1, 0), r * cb + off))

    def body(q_ref, kp_ref, kc_ref, vp_ref, vc_ref, b_ref, o_ref, l_ref):
        n = pl.program_id(1)
        pen0 = jnp.where(n > 0, 0.0, NEG)
        first, own, band, is_prev = _pair_masks()
        o_rows, l_rows = [], []
        for t in range(rb):
            rows_t = slice(t * BLK, (t + 1) * BLK)
            o_parts, l_parts = [], []
            for hp in range(HEAD_PAIRS):
                cols = slice(hp * BLK, (hp + 1) * BLK)
                if t == 0:
                    kp, vp, pen = kp_ref[:, cols], vp_ref[:, cols], pen0
                else:
                    before = slice((t - 1) * BLK, t * BLK)
                    kp, vp, pen = kc_ref[before, cols], vc_ref[before, cols], None
                kb = jnp.concatenate([kp, kc_ref[rows_t, cols]], axis=0)
                vb = jnp.concatenate([vp, vc_ref[rows_t, cols]], axis=0)
                bias2 = jnp.concatenate([b_ref[2 * hp], b_ref[2 * hp + 1]], axis=0)
                s, _ = _pair_scores(q_ref[rows_t, cols].astype(F32), kb, bias2, own, band, is_prev, pen)
                m = jnp.max(s, axis=1, keepdims=True)
                p = jnp.exp(s - m)
                l = jnp.sum(p, axis=1, keepdims=True)
                o2 = _dot(p * (1.0 / l), vb, NN)
                lse2 = m + jnp.log(l)
                o_parts.append(jnp.where(first, o2[:BLK], o2[BLK:]))
                l_parts.append(jnp.where(first, lse2[:BLK], lse2[BLK:]))
            o_rows.append(jnp.concatenate(o_parts, axis=1))
            l_rows.append(jnp.concatenate(l_parts, axis=1))
        o_ref[...] = jnp.concatenate(o_rows, axis=0)
        l_ref[...] = jnp.concatenate(l_rows, axis=0)

    out_spec = pl.BlockSpec((rb * BLK, ATTN_OUT), lambda r, n: (n, r))
    shape = jax.ShapeDtypeStruct((rows, dil * ATTN_OUT), F32)
    return pl.pallas_call(
        body, out_shape=[shape, shape], grid=(dil, nb // rb),
        in_specs=[cur(oq), prev(ok), cur(ok), prev(ov), cur(ov),
                  pl.BlockSpec((HEADS_PER_GROUP, BLK, 2 * BLK), lambda r, n: (g, 0, 0))],
        out_specs=[out_spec, out_spec], compiler_params=_params(2), name=f"attn_fwd_g{g}",
    )(qa, ka, ka, va, va, bias)


def _attn_bwd(qa, ka, va, cb, offs, bias, dy, ya, lt, g, dil):
    rows = qa.shape[0]
    nb = rows // BLK
    oq, ok, ov = offs

    def cur(off, c):
        return pl.BlockSpec((BLK, ATTN_OUT), lambda r, n: (jnp.minimum(n, nb - 1), r * c + off))

    def prev(off, c):
        return pl.BlockSpec((BLK, ATTN_OUT), lambda r, n: (jnp.clip(n - 1, 0, nb - 1), r * c + off))

    def body(q_ref, kp_ref, kc_ref, vp_ref, vc_ref, b_ref, dy_ref, ya_ref, lt_ref,
             dq_ref, dk_ref, dv_ref, db_ref, ck_ref, cv_ref):
        r, n = pl.program_id(0), pl.program_id(1)

        @pl.when(jnp.logical_and(r == 0, n == 0))
        def _():
            db_ref[...] = jnp.zeros_like(db_ref)

        @pl.when(n == 0)
        def _():
            ck_ref[...] = jnp.zeros_like(ck_ref)
            cv_ref[...] = jnp.zeros_like(cv_ref)

        @pl.when(n < nb)
        def _():
            pen = jnp.where(n > 0, 0.0, NEG)
            first, own, band, is_prev = _pair_masks()
            second = jnp.logical_not(first)
            scale = HEAD_DIM ** -0.5
            parts = {k: [] for k in ("dq", "dkp", "dkc", "dvp", "dvc")}
            db_parts = []
            for hp in range(HEAD_PAIRS):
                cols = slice(hp * BLK, (hp + 1) * BLK)
                kb = jnp.concatenate([kp_ref[:, cols], kc_ref[:, cols]], axis=0)
                vb = jnp.concatenate([vp_ref[:, cols], vc_ref[:, cols]], axis=0)
                bias2 = jnp.concatenate([b_ref[2 * hp], b_ref[2 * hp + 1]], axis=0)
                dy_, lt_ = dy_ref[:, cols], lt_ref[:, cols]
                dyy = dy_ * ya_ref[:, cols]
                per_head = lambda t, red, fill: jnp.concatenate(
                    [red(jnp.where(first, t, fill), axis=1, keepdims=True),
                     red(jnp.where(second, t, fill), axis=1, keepdims=True)], axis=0)
                lse2 = per_head(lt_, jnp.max, NEG)
                delta2 = per_head(dyy, jnp.sum, 0.0)
                s, qm = _pair_scores(q_ref[:, cols].astype(F32), kb, bias2, own, band, is_prev, pen)
                p = jnp.exp(s - lse2)
                dym = jnp.where(own, jnp.concatenate([dy_, dy_], axis=0), 0.0)
                ds = p * (_dot(dym, vb, NT) - delta2)
                db_parts += [ds[:BLK], ds[BLK:]]
                dq2 = _dot(ds, kb, NN) * scale
                dkb = _dot(ds, qm, TN) * scale
                dvb = _dot(p, dym, TN)
                for k, val in (("dq", jnp.where(first, dq2[:BLK], dq2[BLK:])), ("dkp", dkb[:BLK]),
                               ("dkc", dkb[BLK:]), ("dvp", dvb[:BLK]), ("dvc", dvb[BLK:])):
                    parts[k].append(val)
            wide = {k: jnp.concatenate(val, axis=1) for k, val in parts.items()}
            db_ref[...] += jnp.stack(db_parts, axis=0)
            dq_ref[...] = wide["dq"].astype(dq_ref.dtype)
            dk_ref[...] = (ck_ref[...] + wide["dkp"]).astype(dk_ref.dtype)
            dv_ref[...] = (cv_ref[...] + wide["dvp"]).astype(dv_ref.dtype)
            ck_ref[...] = wide["dkc"]
            cv_ref[...] = wide["dvc"]

        @pl.when(n == nb)
        def _():
            dk_ref[...] = ck_ref[...].astype(dk_ref.dtype)
            dv_ref[...] = cv_ref[...].astype(dv_ref.dtype)

    late = pl.BlockSpec((BLK, ATTN_OUT), lambda r, n: (jnp.maximum(n - 1, 0), r))
    shape = jax.ShapeDtypeStruct((rows, dil * ATTN_OUT), MXU_DTYPE)
    return pl.pallas_call(
        body,
        out_shape=[shape, shape, shape, jax.ShapeDtypeStruct((HEADS_PER_GROUP, BLK, 2 * BLK), F32)],
        grid=(dil, nb + 1),
        in_specs=[cur(oq, cb), prev(ok, cb), cur(ok, cb), prev(ov, cb), cur(ov, cb),
                  pl.BlockSpec((HEADS_PER_GROUP, BLK, 2 * BLK), lambda r, n: (g, 0, 0)),
                  cur(0, 1), cur(0, 1), cur(0, 1)],
        out_specs=[cur(0, 1), late, late, pl.BlockSpec((HEADS_PER_GROUP, BLK, 2 * BLK), lambda r, n: (0, 0, 0))],
        scratch_shapes=[pltpu.VMEM((BLK, ATTN_OUT), F32), pltpu.VMEM((BLK, ATTN_OUT), F32)],
        compiler_params=_params(2), name=f"attn_bwd_g{g}",
    )(qa, ka, ka, va, va, bias, dy, ya, lt)


def _attn_combine(o_list, l_list):
    s = o_list[0].shape[0]
    tr = _pick(s, (1024, 512, 256, 128))

    def body(o0, o1, o2, l0, l1, l2, y_ref, yb_ref, lt_ref):
        a0, a1, a2 = l0[...], l1[...], l2[...]
        mx = jnp.maximum(jnp.maximum(a0, a1), a2)
        e0, e1, e2 = jnp.exp(a0 - mx), jnp.exp(a1 - mx), jnp.exp(a2 - mx)
        den = e0 + e1 + e2
        inv = 1.0 / den
        y = (e0 * inv) * o0[...] + (e1 * inv) * o1[...] + (e2 * inv) * o2[...]
        y_ref[...] = y
        yb_ref[...] = y.astype(yb_ref.dtype)
        lt_ref[...] = mx + jnp.log(den)

    spec = pl.BlockSpec((tr, ATTN_OUT), lambda i: (i, 0))
    f = jax.ShapeDtypeStruct((s, ATTN_OUT), F32)
    return pl.pallas_call(
        body, out_shape=[f, jax.ShapeDtypeStruct((s, ATTN_OUT), MXU_DTYPE), f], grid=(s // tr,),
        in_specs=[spec] * 6, out_specs=[spec] * 3, compiler_params=_params(1), name="attn_combine",
    )(*o_list, *l_list)


def _ret_consts(s):
    half = RET_QK // 2
    pos = jnp.arange(s, dtype=F32)
    inv_freq = ROPE_BASE ** (-jnp.arange(half, dtype=F32) / half)
    ang = pos[:, None] * inv_freq[None]
    log_g = jnp.log(1.0 - 2.0 ** (-5.0 - jnp.arange(RET_HEADS, dtype=F32)))
    n = jnp.arange(RET_CHUNK, dtype=F32)
    diff = n[:, None] - n[None, :]
    dmask = jnp.where(diff >= 0, jnp.exp(log_g[:, None, None] * jnp.maximum(diff, 0.0)), 0.0)
    qdec = jnp.exp(log_g[:, None] * (n + 1.0))
    kdec = jnp.exp(log_g[:, None] * (RET_CHUNK - 1.0 - n))
    cdec = jnp.exp(log_g * RET_CHUNK)
    wide = (RET_HEADS, RET_CHUNK, RET_QK)
    return dict(cos=jnp.cos(ang), sin=jnp.sin(ang), dmask=dmask,
                qdec=jnp.broadcast_to(qdec[:, :, None], wide), kdec=jnp.broadcast_to(kdec[:, :, None], wide),
                cdec=cdec)


def _rot(t, cs, sn):
    half = RET_QK // 2
    t1, t2 = t[:, :half], t[:, half:]
    return jnp.concatenate([t1 * cs - t2 * sn, t1 * sn + t2 * cs], axis=1)


def _rot_bwd(d, cs, sn):
    half = RET_QK // 2
    d1, d2 = d[:, :half], d[:, half:]
    return jnp.concatenate([d1 * cs + d2 * sn, d2 * cs - d1 * sn], axis=1)


def _ret_fwd(zq, zk, zv, zg, rc):
    s = zq.shape[0]
    nc = s // RET_CHUNK
    c = RET_CHUNK

    def body(cd_ref, q_ref, k_ref, v_ref, g_ref, cos_ref, sin_ref, dm_ref, qd_ref, kd_ref,
             yb_ref, o_ref, st_ref, state):
        n = pl.program_id(0)

        @pl.when(n == 0)
        def _():
            state[...] = jnp.zeros_like(state)

        cs, sn = cos_ref[...], sin_ref[...]
        for h in range(RET_HEADS):
            qs, vs = slice(h * RET_QK, (h + 1) * RET_QK), slice(h * RET_V, (h + 1) * RET_V)
            qr = _rot(q_ref[:, qs], cs, sn)
            kr = _rot(k_ref[:, qs], cs, sn) * (RET_QK ** -0.5)
            v = v_ref[:, vs]
            st = state[h]
            st_ref[h, 0] = st.astype(st_ref.dtype)
            sc = _dot(qr, kr, NT) * dm_ref[h]
            o = _dot(sc, v, NN) + _dot(qr * qd_ref[h], st, NN)
            state[h] = st * cd_ref[h] + _dot(kr * kd_ref[h], v, TN)
            o_ref[:, vs] = o
            xh, _ = _norm_rows(o, GN_EPS)
            gv = g_ref[:, vs]
            yb_ref[:, vs] = (gv * _sigmoid(gv) * xh).astype(yb_ref.dtype)

    row = lambda w: pl.BlockSpec((c, w), lambda n: (n, 0))
    const = lambda a: pl.BlockSpec(a.shape, lambda n: (0, 0, 0))
    return pl.pallas_call(
        body,
        out_shape=[jax.ShapeDtypeStruct((s, RET_HEADS * RET_V), MXU_DTYPE),
                   jax.ShapeDtypeStruct((s, RET_HEADS * RET_V), F32),
                   jax.ShapeDtypeStruct((RET_HEADS, nc, RET_QK, RET_V), MXU_DTYPE)],
        grid=(nc,),
        in_specs=[pl.BlockSpec(memory_space=pltpu.SMEM), row(RET_HEADS * RET_QK), row(RET_HEADS * RET_QK),
                  row(RET_HEADS * RET_V), row(RET_HEADS * RET_V), row(RET_QK // 2), row(RET_QK // 2),
                  const(rc["dmask"]), const(rc["qdec"]), const(rc["kdec"])],
        out_specs=[row(RET_HEADS * RET_V), row(RET_HEADS * RET_V),
                   pl.BlockSpec((RET_HEADS, 1, RET_QK, RET_V), lambda n: (0, n, 0, 0))],
        scratch_shapes=[pltpu.VMEM((RET_HEADS, RET_QK, RET_V), F32)],
        compiler_params=_params(1), name="ret_fwd",
    )(rc["cdec"], zq, zk, zv, zg, rc["cos"], rc["sin"], rc["dmask"], rc["qdec"], rc["kdec"])


def _ret_bwd(zq, zk, zv, zg, o_ret, states, dyb, rc):
    s = zq.shape[0]
    nc = s // RET_CHUNK
    c = RET_CHUNK

    def body(cd_ref, q_ref, k_ref, v_ref, g_ref, o_ref, st_ref, dy_ref, cos_ref, sin_ref, dm_ref, qd_ref,
             kd_ref, dq_ref, dk_ref, dv_ref, dg_ref, dstate):
        n = pl.program_id(0)

        @pl.when(n == 0)
        def _():
            dstate[...] = jnp.zeros_like(dstate)

        cs, sn = cos_ref[...], sin_ref[...]
        for h in range(RET_HEADS):
            qs, vs = slice(h * RET_QK, (h + 1) * RET_QK), slice(h * RET_V, (h + 1) * RET_V)
            qr = _rot(q_ref[:, qs], cs, sn)
            kr = _rot(k_ref[:, qs], cs, sn) * (RET_QK ** -0.5)
            v = v_ref[:, vs]
            st = st_ref[h, 0]
            dm, qd, kd = dm_ref[h], qd_ref[h], kd_ref[h]
            xh, rstd = _norm_rows(o_ref[:, vs], GN_EPS)
            gv, dy = g_ref[:, vs], dy_ref[:, vs]
            sg = _sigmoid(gv)
            dg_ref[:, vs] = dy * xh * (sg * (1.0 + gv * (1.0 - sg)))
            do = _norm_rows_bwd(dy * (gv * sg), xh, rstd)
            ds_next = dstate[h]
            sc = _dot(qr, kr, NT) * dm
            da = _dot(do, v, NT) * dm
            dv = _dot(sc, do, TN) + _dot(kr * kd, ds_next, NN)
            dqr = _dot(da, kr, NN) + _dot(do, st, NT) * qd
            dkr = _dot(da, qr, TN) + _dot(v, ds_next, NT) * kd
            dstate[h] = ds_next * cd_ref[h] + _dot(qr * qd, do, TN)
            dq_ref[:, qs] = _rot_bwd(dqr, cs, sn)
            dk_ref[:, qs] = _rot_bwd(dkr * (RET_QK ** -0.5), cs, sn)
            dv_ref[:, vs] = dv.astype(dv_ref.dtype)

    row = lambda w: pl.BlockSpec((c, w), lambda n: (nc - 1 - n, 0))
    const = lambda a: pl.BlockSpec(a.shape, lambda n: (0, 0, 0))
    f_qk = jax.ShapeDtypeStruct((s, RET_HEADS * RET_QK), F32)
    qk_w, v_w = RET_HEADS * RET_QK, RET_HEADS * RET_V
    return pl.pallas_call(
        body,
        out_shape=[f_qk, f_qk, jax.ShapeDtypeStruct((s, v_w), MXU_DTYPE), jax.ShapeDtypeStruct((s, v_w), F32)],
        grid=(nc,),
        in_specs=[pl.BlockSpec(memory_space=pltpu.SMEM), row(qk_w), row(qk_w), row(v_w), row(v_w), row(v_w),
                  pl.BlockSpec((RET_HEADS, 1, RET_QK, RET_V), lambda n: (0, nc - 1 - n, 0, 0)), row(v_w),
                  row(RET_QK // 2), row(RET_QK // 2), const(rc["dmask"]), const(rc["qdec"]), const(rc["kdec"])],
        out_specs=[row(qk_w), row(qk_w), row(v_w), row(v_w)],
        scratch_shapes=[pltpu.VMEM((RET_HEADS, RET_QK, RET_V), F32)],
        compiler_params=_params(1), name="ret_bwd",
    )(rc["cdec"], zq, zk, zv, zg, o_ret, states, dyb, rc["cos"], rc["sin"], rc["dmask"], rc["qdec"],
      rc["kdec"])


ROW_TILE = 256


def _mix_fwd(x, ya_b, yb_b, ze, wap_t, wrp, wout, gam, bet):
    s = x.shape[0]
    tm = ROW_TILE

    def body(x_ref, ya_ref, yb_ref, ze_ref, wap_ref, wrp_ref, wout_ref, g_ref, b_ref,
             x1_ref, u1_ref, a1_ref, a2_ref, mg_ref):
        a1 = _dot(ya_ref[...], wap_ref[...], NT)
        a2 = _dot(yb_ref[...], wrp_ref[...], NN)
        ze_ = ze_ref[...]
        merged = _sigmoid(ze_[:, :D_MODEL]) * a1 + _sigmoid(ze_[:, D_MODEL:]) * a2
        u = ALPHA * x_ref[...] + _dot(merged, wout_ref[...], NN)
        xh, _ = _norm_rows(u, LN_EPS)
        x1_ref[...] = xh * g_ref[...] + b_ref[...]
        u1_ref[...] = u
        a1_ref[...] = a1
        a2_ref[...] = a2
        mg_ref[...] = merged.astype(mg_ref.dtype)

    row = lambda w: pl.BlockSpec((tm, w), lambda i: (i, 0))
    full = _resident
    f = jax.ShapeDtypeStruct((s, D_MODEL), F32)
    return pl.pallas_call(
        body, out_shape=[f, f, f, f, jax.ShapeDtypeStruct((s, D_MODEL), MXU_DTYPE)], grid=(s // tm,),
        in_specs=[row(D_MODEL), row(ATTN_OUT), row(RET_HEADS * RET_V), row(2 * D_MODEL), full(wap_t), full(wrp),
                  full(wout), full(gam), full(bet)],
        out_specs=[row(D_MODEL)] * 5, compiler_params=_params(1), name="mix_fwd",
    )(x, ya_b, yb_b, ze, wap_t, wrp, wout, gam, bet)


def _mix_bwd(dx1, u1, a1, a2, ze, wap_t, wrp, wout, gam):
    s = dx1.shape[0]
    tm = ROW_TILE

    def body(dx_ref, u_ref, a1_ref, a2_ref, ze_ref, wap_ref, wrp_ref, wout_ref, g_ref,
             dres_ref, du_ref, da1_ref, da2_ref, dze_ref, dya_ref, dyb_ref, dgam_ref, dbet_ref):
        @pl.when(pl.program_id(0) == 0)
        def _():
            dgam_ref[...] = jnp.zeros_like(dgam_ref)
            dbet_ref[...] = jnp.zeros_like(dbet_ref)

        dx = dx_ref[...]
        xh, rstd = _norm_rows(u_ref[...], LN_EPS)
        dgam_ref[...] += jnp.sum(dx * xh, axis=0, keepdims=True)
        dbet_ref[...] += jnp.sum(dx, axis=0, keepdims=True)
        du = _norm_rows_bwd(dx * g_ref[...], xh, rstd)
        dres_ref[...] = ALPHA * du
        du_ref[...] = du.astype(du_ref.dtype)
        dm = _dot(du, wout_ref[...], NT)
        ze_ = ze_ref[...]
        sa, sb = _sigmoid(ze_[:, :D_MODEL]), _sigmoid(ze_[:, D_MODEL:])
        da1, da2 = dm * sa, dm * sb
        dze_ref[...] = jnp.concatenate([dm * a1_ref[...] * (sa * (1.0 - sa)),
                                        dm * a2_ref[...] * (sb * (1.0 - sb))], axis=1)
        da1_ref[...] = da1.astype(da1_ref.dtype)
        da2_ref[...] = da2.astype(da2_ref.dtype)
        dya_ref[...] = _dot(da1, wap_ref[...], NN)
        dyb_ref[...] = _dot(da2, wrp_ref[...], NT)

    row = lambda w: pl.BlockSpec((tm, w), lambda i: (i, 0))
    full = _resident
    vec = pl.BlockSpec((1, D_MODEL), lambda i: (0, 0))
    f = lambda w: jax.ShapeDtypeStruct((s, w), F32)
    m = lambda w: jax.ShapeDtypeStruct((s, w), MXU_DTYPE)
    v = jax.ShapeDtypeStruct((1, D_MODEL), F32)
    return pl.pallas_call(
        body,
        out_shape=[f(D_MODEL), m(D_MODEL), m(D_MODEL), m(D_MODEL), f(2 * D_MODEL), f(ATTN_OUT),
                   f(RET_HEADS * RET_V), v, v],
        grid=(s // tm,),
        in_specs=[row(D_MODEL)] * 4 + [row(2 * D_MODEL), full(wap_t), full(wrp), full(wout), full(gam)],
        out_specs=[row(D_MODEL)] * 4 + [row(2 * D_MODEL), row(ATTN_OUT), row(RET_HEADS * RET_V), vec, vec],
        compiler_params=_params(1), name="mix_bwd",
    )(dx1, u1, a1, a2, ze, wap_t, wrp, wout, gam)


FF_CHUNK = 1408


def _ffn_fwd(x1, wg_t, wu_t, wd, gam, bet):
    s = x1.shape[0]
    tm, fc = ROW_TILE, FF_CHUNK

    def body(x_ref, wg_ref, wu_ref, wd_ref, g_ref, b_ref, x2_ref, u2_ref, a_ref, b_out_ref, h_ref):
        xv = x_ref[...]
        u = ALPHA * xv
        for f0 in range(0, D_FF, fc):
            ch = slice(f0, f0 + fc)
            a = _dot(xv, wg_ref[ch, :], NT)
            b = _dot(xv, wu_ref[ch, :], NT)
            hid = a * _sigmoid(a) * b
            u = u + _dot(hid, wd_ref[ch, :], NN)
            a_ref[:, ch] = a
            b_out_ref[:, ch] = b
            h_ref[:, ch] = hid.astype(h_ref.dtype)
        xh, _ = _norm_rows(u, LN_EPS)
        u2_ref[...] = u
        x2_ref[...] = xh * g_ref[...] + b_ref[...]

    row = lambda w: pl.BlockSpec((tm, w), lambda i: (i, 0))
    f = lambda w: jax.ShapeDtypeStruct((s, w), F32)
    return pl.pallas_call(
        body, out_shape=[f(D_MODEL), f(D_MODEL), f(D_FF), f(D_FF), jax.ShapeDtypeStruct((s, D_FF), MXU_DTYPE)],
        grid=(s // tm,),
        in_specs=[row(D_MODEL), _resident(wg_t), _resident(wu_t), _resident(wd), _resident(gam), _resident(bet)],
        out_specs=[row(D_MODEL), row(D_MODEL), row(D_FF), row(D_FF), row(D_FF)],
        compiler_params=_params(1), name="ffn_fwd",
    )(x1, wg_t, wu_t, wd, gam, bet)


def _ffn_bwd(dx2, u2, fa, fb, wg_t, wu_t, wd, gam):
    s = dx2.shape[0]
    tm, fc = ROW_TILE, FF_CHUNK

    def body(dx_ref, u_ref, a_ref, b_ref, wg_ref, wu_ref, wd_ref, g_ref,
             dx1_ref, du_ref, da_ref, db_ref, dgam_ref, dbet_ref):
        @pl.when(pl.program_id(0) == 0)
        def _():
            dgam_ref[...] = jnp.zeros_like(dgam_ref)
            dbet_ref[...] = jnp.zeros_like(dbet_ref)

        dx = dx_ref[...]
        xh, rstd = _norm_rows(u_ref[...], LN_EPS)
        dgam_ref[...] += jnp.sum(dx * xh, axis=0, keepdims=True)
        dbet_ref[...] += jnp.sum(dx, axis=0, keepdims=True)
        du = _norm_rows_bwd(dx * g_ref[...], xh, rstd)
        du_ref[...] = du.astype(du_ref.dtype)
        acc = ALPHA * du
        for f0 in range(0, D_FF, fc):
            ch = slice(f0, f0 + fc)
            dh = _dot(du, wd_ref[ch, :], NT)
            a, b = a_ref[:, ch], b_ref[:, ch]
            sg = _sigmoid(a)
            da = dh * b * (sg * (1.0 + a * (1.0 - sg)))
            db = dh * (a * sg)
            acc = acc + _dot(da, wg_ref[ch, :], NN) + _dot(db, wu_ref[ch, :], NN)
            da_ref[:, ch] = da.astype(da_ref.dtype)
            db_ref[:, ch] = db.astype(db_ref.dtype)
        dx1_ref[...] = acc

    row = lambda w: pl.BlockSpec((tm, w), lambda i: (i, 0))
    vec = pl.BlockSpec((1, D_MODEL), lambda i: (0, 0))
    v = jax.ShapeDtypeStruct((1, D_MODEL), F32)
    return pl.pallas_call(
        body,
        out_shape=[jax.ShapeDtypeStruct((s, D_MODEL), F32), jax.ShapeDtypeStruct((s, D_MODEL), MXU_DTYPE),
                   jax.ShapeDtypeStruct((s, D_FF), MXU_DTYPE), jax.ShapeDtypeStruct((s, D_FF), MXU_DTYPE), v, v],
        grid=(s // tm,),
        in_specs=[row(D_MODEL), row(D_MODEL), row(D_FF), row(D_FF), _resident(wg_t), _resident(wu_t),
                  _resident(wd), _resident(gam)],
        out_specs=[row(D_MODEL), row(D_MODEL), row(D_FF), row(D_FF), vec, vec],
        compiler_params=_params(1), name="ffn_bwd",
    )(dx2, u2, fa, fb, wg_t, wu_t, wd, gam)


def _loss_head(y, target):
    s = y.shape[0]
    tr = _pick(s, (512, 256, 128))

    def body(y_ref, t_ref, dy_ref, l_ref):
        @pl.when(pl.program_id(0) == 0)
        def _():
            l_ref[...] = jnp.zeros_like(l_ref)

        e = y_ref[...] - t_ref[...]
        dy_ref[...] = e * (1.0 / D_MODEL)
        part = jnp.sum(jnp.sum(e * e, axis=1, keepdims=True), axis=0, keepdims=True)
        l_ref[...] += part * (0.5 / D_MODEL)

    spec = pl.BlockSpec((tr, D_MODEL), lambda i: (i, 0))
    dy, part = pl.pallas_call(
        body, out_shape=[jax.ShapeDtypeStruct((s, D_MODEL), F32), jax.ShapeDtypeStruct((8, BLK), F32)],
        grid=(s // tr,), in_specs=[spec, spec], out_specs=[spec, pl.BlockSpec((8, BLK), lambda i: (0, 0))],
        compiler_params=_params(1), name="loss_head",
    )(y, target)
    return dy, part[0, 0]


def _adamw(w, g, m, v, name):
    rows, cols = w.shape
    budget = 1 << 20
    cands = [t for t in range(8, rows + 1, 8) if rows % t == 0 and t * cols * 4 <= budget]
    tr = max(cands) if cands else rows

    def body(w_ref, g_ref, m_ref, v_ref, d_ref, nm_ref, nv_ref):
        gv = g_ref[...]
        mn = ADAM_B1 * m_ref[...] + (1.0 - ADAM_B1) * gv
        vn = ADAM_B2 * v_ref[...] + (1.0 - ADAM_B2) * (gv * gv)
        m_hat = mn / (1.0 - ADAM_B1 ** ADAM_STEP)
        v_hat = vn / (1.0 - ADAM_B2 ** ADAM_STEP)
        d_ref[...] = -ADAM_LR * (m_hat / (jnp.sqrt(v_hat) + ADAM_EPS) + ADAM_WD * w_ref[...])
        nm_ref[...] = mn
        nv_ref[...] = vn

    spec = pl.BlockSpec((tr, cols), lambda i: (i, 0))
    shape = jax.ShapeDtypeStruct((rows, cols), F32)
    return pl.pallas_call(
        body, out_shape=[shape, shape, shape], grid=(rows // tr,), in_specs=[spec] * 4, out_specs=[spec] * 3,
        compiler_params=_params(1), name=name,
    )(w, g, m, v)


MESH_ID = pl.DeviceIdType.MESH
ANY = pl.BlockSpec(memory_space=pl.ANY)


def _place():
    x, y, c = lax.axis_index("x"), lax.axis_index("y"), lax.axis_index("c")
    other_chips = [(1 - x, y), (x, 1 - y), (1 - x, 1 - y)]
    return x, y, c, other_chips


def _chip_no(chip):
    return 2 * chip[0] + chip[1]


def _gather_exchange(f_ref, g_ref, send_sems, recv_sems, hr):
    x, y, c, chips = _place()
    sibling = (x, y, 1 - c)

    def piece(chip, half):
        return g_ref.at[_chip_no(chip), pl.ds(half * hr, hr), :]

    def copy(k, src, dst, to):
        return pltpu.make_async_remote_copy(src_ref=src, dst_ref=dst, send_sem=send_sems.at[k],
                                            recv_sem=recv_sems.at[k], device_id=to, device_id_type=MESH_ID)

    own = copy(6, f_ref, g_ref.at[_chip_no((x, y))], sibling)
    own.start()
    my_half = f_ref.at[pl.ds(c * hr, hr), :]
    first = [copy(k, my_half, piece((x, y), c), (*chip, c)) for k, chip in enumerate(chips)]
    for cp in first:
        cp.start()
    passed = [copy(3 + k, piece(chip, c), piece(chip, c), sibling) for k, chip in enumerate(chips)]
    for k, chip in enumerate(chips):
        copy(k, my_half, piece(chip, c), (*chip, c)).wait_recv()
        passed[k].start()
    for k, chip in enumerate(chips):
        copy(3 + k, my_half, piece(chip, 1 - c), sibling).wait_recv()
    for cp in first + passed:
        cp.wait_send()
    own.wait()


GATHER_SEMS = (pltpu.SemaphoreType.DMA((7,)), pltpu.SemaphoreType.DMA((7,)))


HBM = pltpu.MemorySpace.HBM
ONE_SEM_PAIR = (pltpu.SemaphoreType.DMA, pltpu.SemaphoreType.DMA)


def _sibling_of_me():
    x, y, c, _ = _place()
    return [(x, y, 1 - c)]


def _same_core_of_other_chips():
    x, y, c, chips = _place()
    return [(*chip, c) for chip in chips]


def _on_sequencer(name, collective_id, sems, peers, exchange):
    @pl.kernel(mesh=plsc.ScalarSubcoreMesh(axis_name="sequencer", num_cores=1), name=name, scratch_types=sems,
               compiler_params=pltpu.CompilerParams(collective_id=collective_id))
    def launch(*sem_refs):
        barrier = pltpu.get_barrier_semaphore()
        devices = peers()
        for peer in devices:
            pl.semaphore_signal(barrier, inc=1, device_id=peer, device_id_type=MESH_ID)
        pl.semaphore_wait(barrier, len(devices))
        exchange(*sem_refs)

    launch()


def _all_gather_rows(flat, behind=None):
    r, cols = flat.shape
    out = jax.ShapeDtypeStruct((N_CHIPS, r, cols), flat.dtype)
    if behind is not None:
        f_ref, g_ref = jax.new_ref(flat, memory_space=HBM), jax.empty_ref(out, memory_space=HBM)
        _on_sequencer(f"all_gather_weights_behind_{behind[1]}", behind[0], GATHER_SEMS,
                      lambda: _sibling_of_me() + _same_core_of_other_chips(),
                      lambda s, r_: _gather_exchange(f_ref, g_ref, s, r_, r // 2))
        return g_ref[...]

    def body(f_ref, g_ref, send_sems, recv_sems):
        _gather_exchange(f_ref, g_ref, send_sems, recv_sems, r // 2)

    return pl.pallas_call(body, out_shape=out, in_specs=[ANY], out_specs=ANY, scratch_shapes=list(GATHER_SEMS),
                          name="all_gather_weights")(flat)


def _swap_halves(gall, behind=None):
    _, r, cols = gall.shape
    hr = r // 2
    out = jax.ShapeDtypeStruct((N_CHIPS, hr, cols), gall.dtype)

    def exchange(g_ref, a_ref, send_sem, recv_sem):
        x, y, c, _ = _place()
        cp = pltpu.make_async_remote_copy(src_ref=g_ref.at[:, pl.ds((1 - c) * hr, hr), :], dst_ref=a_ref,
                                          send_sem=send_sem, recv_sem=recv_sem, device_id=(x, y, 1 - c),
                                          device_id_type=MESH_ID)
        cp.start()
        cp.wait()

    if behind is not None:
        g_ref, a_ref = jax.new_ref(gall, memory_space=HBM), jax.empty_ref(out, memory_space=HBM)
        _on_sequencer(f"grad_swap_halves_behind_{behind[1]}", behind[0], ONE_SEM_PAIR, _sibling_of_me,
                      lambda s, r_: exchange(g_ref, a_ref, s, r_))
        return a_ref[...]

    def body(g_ref, a_ref, send_sem, recv_sem):
        exchange(g_ref, a_ref, send_sem, recv_sem)

    return pl.pallas_call(body, out_shape=out, in_specs=[ANY], out_specs=ANY, scratch_shapes=list(ONE_SEM_PAIR),
                          name="grad_swap_halves")(gall)


def _scatter_to_chips(p, behind=None):
    _, hr, cols = p.shape
    out = jax.ShapeDtypeStruct((3, hr, cols), p.dtype)
    sems = (pltpu.SemaphoreType.DMA((3,)), pltpu.SemaphoreType.DMA((3,)))

    def exchange(p_ref, b_ref, send_sems, recv_sems):
        x, y, c, chips = _place()
        cps = [pltpu.make_async_remote_copy(src_ref=p_ref.at[_chip_no(chip)], dst_ref=b_ref.at[k],
                                            send_sem=send_sems.at[k], recv_sem=recv_sems.at[k],
                                            device_id=(*chip, c), device_id_type=MESH_ID)
               for k, chip in enumerate(chips)]
        for cp in cps:
            cp.start()
        for cp in cps:
            cp.wait()

    if behind is not None:
        p_ref, b_ref = jax.new_ref(p, memory_space=HBM), jax.empty_ref(out, memory_space=HBM)
        _on_sequencer(f"grad_scatter_chips_behind_{behind[1]}", behind[0], sems, _same_core_of_other_chips,
                      lambda s, r_: exchange(p_ref, b_ref, s, r_))
        return b_ref[...]

    def body(p_ref, b_ref, send_sems, recv_sems):
        exchange(p_ref, b_ref, send_sems, recv_sems)

    return pl.pallas_call(body, out_shape=out, in_specs=[ANY], out_specs=ANY, scratch_shapes=list(sems),
                          name="grad_scatter_chips")(p)


def _share_with_sibling(full, behind=None):
    r, cols = full.shape
    hr = r // 2

    def exchange(in_ref, out_ref, send_sem, recv_sem):
        x, y, c, _ = _place()
        cp = pltpu.make_async_remote_copy(src_ref=in_ref.at[pl.ds(c * hr, hr), :],
                                          dst_ref=out_ref.at[pl.ds(c * hr, hr), :],
                                          send_sem=send_sem, recv_sem=recv_sem, device_id=(x, y, 1 - c),
                                          device_id_type=MESH_ID)
        cp.start()
        cp.wait()

    if behind is not None:
        full_ref = jax.new_ref(full, memory_space=HBM)
        _on_sequencer(f"grad_share_sibling_behind_{behind[1]}", behind[0], ONE_SEM_PAIR, _sibling_of_me,
                      lambda s, r_: exchange(full_ref, full_ref, s, r_))
        return full_ref[...]

    def body(in_ref, out_ref, send_sem, recv_sem):
        exchange(in_ref, out_ref, send_sem, recv_sem)

    return pl.pallas_call(
        body, out_shape=jax.ShapeDtypeStruct((r, cols), full.dtype), in_specs=[ANY], out_specs=ANY,
        input_output_aliases={0: 0}, scratch_shapes=list(ONE_SEM_PAIR), name="grad_share_sibling",
    )(full)


def _all_reduce_small(v):
    r, cols = v.shape
    n_dev = 8

    def body(x_ref, out_ref, gat_ref, send_sems, recv_sems, local_sem):
        x, y, c, chips = _place()
        me, sibling = (x, y, c), (x, y, 1 - c)

        def slot(px, py, pc):
            return gat_ref.at[4 * px + 2 * py + pc]

        def copy(k, block, to, src=None):
            return pltpu.make_async_remote_copy(src_ref=slot(*block) if src is None else src, dst_ref=slot(*block),
                                                send_sem=send_sems.at[k], recv_sem=recv_sems.at[k], device_id=to,
                                                device_id_type=MESH_ID)

        mine = pltpu.make_async_copy(x_ref, slot(*me), local_sem)
        mine.start()
        first = [copy(0, me, sibling, src=x_ref)]
        first += [copy(1 + k, me, (*chip, c), src=x_ref) for k, chip in enumerate(chips)]
        for cp in first:
            cp.start()
        passed = [copy(4 + k, (*chip, c), sibling) for k, chip in enumerate(chips)]
        for k, chip in enumerate(chips):
            copy(1 + k, (*chip, c), me).wait_recv()
            passed[k].start()
        copy(0, sibling, me).wait_recv()
        for k, chip in enumerate(chips):
            copy(4 + k, (*chip, 1 - c), me).wait_recv()
        for cp in first + passed:
            cp.wait_send()
        mine.wait()
        acc = gat_ref[0]
        for d in range(1, n_dev):
            acc = acc + gat_ref[d]
        out_ref[...] = acc

    vmem = pl.BlockSpec(memory_space=pltpu.VMEM)
    return pl.pallas_call(
        body, out_shape=jax.ShapeDtypeStruct((r, cols), v.dtype), in_specs=[vmem], out_specs=vmem,
        scratch_shapes=[pltpu.VMEM((n_dev, r, cols), v.dtype), pltpu.SemaphoreType.DMA((7,)),
                        pltpu.SemaphoreType.DMA((7,)), pltpu.SemaphoreType.DMA],
        name="all_reduce_small",
    )(v)


WIRE_DTYPE = jnp.bfloat16


def _add_own_half(gall, got, place):
    _, r, cols = gall.shape
    hr = r // 2
    tr = _row_tile(hr)
    g4 = gall.reshape(N_CHIPS, 2, hr, cols)

    def body(s_ref, g_ref, a_ref, o_ref):
        o_ref[...] = (g_ref[0] + a_ref[...]).astype(o_ref.dtype)

    return pl.pallas_call(
        body, out_shape=jax.ShapeDtypeStruct((N_CHIPS, hr, cols), WIRE_DTYPE),
        grid_spec=pltpu.PrefetchScalarGridSpec(
            num_scalar_prefetch=1, grid=(N_CHIPS, hr // tr),
            in_specs=[pl.BlockSpec((1, 1, tr, cols), lambda j, i, s: (j, s[1], i, 0)),
                      pl.BlockSpec((1, tr, cols), lambda j, i, s: (j, i, 0))],
            out_specs=pl.BlockSpec((1, tr, cols), lambda j, i, s: (j, i, 0))),
        compiler_params=_params(2), name="grad_add_halves",
    )(place, g4, got)


def _add_chip_parts(gall, got, parts, place):
    _, r, cols = gall.shape
    hr = r // 2
    tr = _row_tile(hr)
    g4 = gall.reshape(N_CHIPS, 2, hr, cols)
    nt = hr // tr

    def body(s_ref, g_ref, a_ref, b0_ref, b1_ref, b2_ref, o_ref):
        own = g_ref[0, 0] + a_ref[0]
        o_ref[...] = ((own + b0_ref[0].astype(F32)) + b1_ref[0].astype(F32)) + b2_ref[0].astype(F32)

    part = lambda k: pl.BlockSpec((1, tr, cols), lambda i, s: (k, i, 0))
    return pl.pallas_call(
        body, out_shape=jax.ShapeDtypeStruct((r, cols), F32),
        grid_spec=pltpu.PrefetchScalarGridSpec(
            num_scalar_prefetch=1, grid=(nt,),
            in_specs=[pl.BlockSpec((1, 1, tr, cols), lambda i, s: (s[0], s[1], i, 0)),
                      pl.BlockSpec((1, tr, cols), lambda i, s: (s[0], i, 0)), part(0), part(1), part(2)],
            out_specs=pl.BlockSpec((tr, cols), lambda i, s: (s[1] * nt + i, 0))),
        compiler_params=_params(1), name="grad_add_chips",
    )(place, g4, got, parts, parts, parts)


def _pack_shards(weights, l, group, dtype):
    parts = []
    for name, _ in group:
        w = weights[name][l]
        parts.append((w.T if name in COLUMN_SHARDED else w).reshape(-1, D_MODEL).astype(dtype))
    return parts[0] if len(parts) == 1 else jnp.concatenate(parts, axis=0)


def _unpack_gathered(gathered, group):
    w, off = {}, 0
    for name, rows in group:
        blk = gathered[:, off:off + rows]
        if name == "w_attn_proj":
            w[name] = blk.reshape(N_CHIPS * 256, ATTN_OUT)
        else:
            w[name] = blk.reshape(N_CHIPS * rows, D_MODEL)
        off += rows
    return w


def _pack_full_grads(g, group):
    parts = [g[name].reshape(N_CHIPS, rows, D_MODEL) for name, rows in group]
    return parts[0] if len(parts) == 1 else jnp.concatenate(parts, axis=1)


def _unpack_shard_grads(fulls):
    out = {}
    for gi, group in enumerate(PACK_GROUPS):
        off = 0
        for name, rows in group:
            blks = [fulls[(l, gi)][off:off + rows] for l in range(DEPTH)]
            out[name] = jnp.stack([b.reshape(256, ATTN_OUT) if name == "w_attn_proj" else b for b in blks])
            off += rows
    return out


class _StaticWeights:
    def __init__(self, layers):
        self.layers = layers

    def group(self, l, gi):
        return self.layers[l]

    def layer(self, l):
        return self.layers[l]


class _WeightGathers:
    def __init__(self, weights):
        self.flats = {(l, gi): _pack_shards(weights, l, g, MXU_DTYPE)
                      for l in range(DEPTH) for gi, g in enumerate(PACK_GROUPS)}
        self.order = sorted(self.flats)
        self.landed, self.unpacked, self.issued = {}, {}, 0
        self._issue()

    def _issue(self, after=None):
        key = self.order[self.issued]
        flat = self.flats[key]
        if self.issued == 0:
            self.landed[key] = _all_gather_rows(flat)
        else:
            prev = self.order[self.issued - 1]
            ties = (self.landed[prev], flat) if after is None else (self.landed[prev], flat, after)
            tied = lax.optimization_barrier(ties)
            self.landed[prev], flat = tied[0], tied[1]
            after = tied[2] if after is not None else None
            self.landed[key] = _all_gather_rows(flat, behind=(1, f"l{key[0]}g{key[1]}"))
        self.issued += 1
        return after

    def prefetch(self, after):
        return self._issue(after) if self.issued < len(self.order) else after

    def group(self, l, gi):
        key = (l, gi)
        if key not in self.unpacked:
            self.unpacked[key] = _unpack_gathered(self.landed[key], PACK_GROUPS[gi])
        return self.unpacked[key]

    def layer(self, l):
        w = {}
        for gi in range(len(PACK_GROUPS)):
            w.update(self.group(l, gi))
        return w


class _GradReduction:
    LAST = 3

    def __init__(self, gw, group, place, behind):
        self.gall, self.place, self.behind, self.stage = _pack_full_grads(gw, group), place, behind, 0
        self.pending = None

    def _how(self, k):
        return None if self.behind is None else (self.behind[0][k], self.behind[1])

    def advance(self, value=None):
        if self.stage > self.LAST:
            return value
        if self.stage > 0 and value is not None:
            value, self.pending = lax.optimization_barrier((value, self.pending))
        if self.stage == 0:
            self.pending = _swap_halves(self.gall, self._how(0))
        elif self.stage == 1:
            self.got = self.pending
            self.pending = _scatter_to_chips(_add_own_half(self.gall, self.got, self.place), self._how(1))
        elif self.stage == 2:
            self.pending = _share_with_sibling(
                _add_chip_parts(self.gall, self.got, self.pending, self.place), self._how(2))
        self.stage += 1
        return value

    def result(self):
        while self.stage < self.LAST:
            self.advance()
        return self.pending


class _GradReductions:
    def __init__(self, place):
        self.place, self.all = place, {}

    def start(self, l, gi, gw):
        hidden = (l, gi) != (0, IN_GROUP)
        slot = len(self.all) % 2
        how = ((2 + 3 * slot, 3 + 3 * slot, 4 + 3 * slot), f"l{l}g{gi}") if hidden else None
        red = _GradReduction(gw, PACK_GROUPS[gi], self.place, how)
        if hidden:
            red.advance()
        self.all[(l, gi)] = red

    def tick(self, value):
        for red in self.all.values():
            if red.behind is not None:
                value = red.advance(value)
        return value

    def results(self):
        return {key: red.result() for key, red in self.all.items()}


def _attn_views(za, s):
    views = []
    for g, (_, dil) in enumerate(ATTN_GROUPS):
        if dil == 1:
            views.append((za, za, za, 3 * N_GROUPS, (g, N_GROUPS + g, 2 * N_GROUPS + g)))
        else:
            rows = s // dil
            part = lambda k: za[:, k * ATTN_WIDTH + g * ATTN_OUT:k * ATTN_WIDTH + (g + 1) * ATTN_OUT].reshape(
                rows, dil * ATTN_OUT)
            views.append((part(0), part(1), part(2), 1, (0, 0, 0)))
    return views


def _layer_fwd(x, weights, l, lnp, bias, rc, prefetch):
    s = x.shape[0]
    tag = f"l{l}"
    w_in = weights.group(l, IN_GROUP)["w_in"]
    z = {}
    for name, off, width, narrow in SECTIONS:
        z[name] = _matmul(x, w_in[off:off + width], "nt", bias=lnp["b_in"][:, off:off + width],
                          out_dtype=MXU_DTYPE if narrow else F32, name=f"in_proj_{name}_{tag}")
        if name == "a":
            z[name] = prefetch(z[name])
    w = weights.group(l, REST_GROUP)
    views = _attn_views(z["a"], s)
    o_list, l_list = [], []
    for g, (_, dil) in enumerate(ATTN_GROUPS):
        qa, ka, va, cb, offs = views[g]
        o, lse = _attn_fwd(qa, ka, va, cb, offs, bias, g, dil)
        o_list.append(o.reshape(s, ATTN_OUT))
        l_list.append(lse.reshape(s, ATTN_OUT))
    ya, ya_b, lt = _attn_combine(o_list, l_list)
    yb_b, o_ret, states = _ret_fwd(z["bq"], z["bk"], z["c"], z["d"], rc)
    x1, u1, a1, a2, mg_b = _mix_fwd(x, ya_b, yb_b, z["e"], w["w_attn_proj"], w["w_ret_proj"], w["w_out"],
                                    lnp["ln1_g"], lnp["ln1_b"])
    x1 = prefetch(x1)
    x2, u2, fa, fb, h_b = _ffn_fwd(x1, w["w_ffn_gate"], w["w_ffn_up"], w["w_ffn_down"], lnp["ln2_g"], lnp["ln2_b"])
    saved = dict(x=x, z=z, views=views, ya=ya, ya_b=ya_b, lt=lt, yb_b=yb_b, o_ret=o_ret, states=states,
                 x1=x1, u1=u1, a1=a1, a2=a2, mg_b=mg_b, u2=u2, fa=fa, fb=fb, h_b=h_b)
    return x2, saved


def _layer_bwd(dx2, w, l, lnp, sv, bias, rc, reductions=None):
    s = dx2.shape[0]
    tag = f"l{l}"
    z = sv["z"]
    step = reductions.tick if reductions is not None else (lambda v: v)
    dx1, du2_b, da_b, db_b, dg2, dbt2 = _ffn_bwd(dx2, sv["u2"], sv["fa"], sv["fb"], w["w_ffn_gate"], w["w_ffn_up"],
                                                  w["w_ffn_down"], lnp["ln2_g"])
    dx1 = step(dx1)
    gw = {}
    gw["w_ffn_down"] = _matmul(sv["h_b"], du2_b, "tn", name=f"dw_ffn_down_{tag}")
    gw["w_ffn_gate"] = _matmul(da_b, sv["x1"], "tn", name=f"dw_ffn_gate_{tag}")
    gw["w_ffn_up"] = _matmul(db_b, sv["x1"], "tn", name=f"dw_ffn_up_{tag}")
    dres, du1_b, da1_b, da2_b, dze, dya, dyb, dg1, dbt1 = _mix_bwd(
        dx1, sv["u1"], sv["a1"], sv["a2"], z["e"], w["w_attn_proj"], w["w_ret_proj"], w["w_out"], lnp["ln1_g"])
    gw["w_out"] = _matmul(sv["mg_b"], du1_b, "tn", name=f"dw_out_{tag}")
    gw["w_attn_proj"] = _matmul(da1_b, sv["ya_b"], "tn", name=f"dw_attn_proj_{tag}")
    gw["w_ret_proj"] = _matmul(sv["yb_b"], da2_b, "tn", name=f"dw_ret_proj_{tag}")
    dyb = step(dyb)
    if reductions is not None:
        reductions.start(l, REST_GROUP, gw)
    dzq, dzk, dzv, dzg = _ret_bwd(z["bq"], z["bk"], z["c"], z["d"], sv["o_ret"], sv["states"], dyb, rc)
    dzq = step(dzq)
    dq_l, dk_l, dv_l, dbias_l = [], [], [], []
    for g, (_, dil) in enumerate(ATTN_GROUPS):
        qa, ka, va, cb, offs = sv["views"][g]
        rows = s // dil
        view = lambda t: t.reshape(rows, dil * ATTN_OUT)
        dq, dk, dv, dbg = _attn_bwd(qa, ka, va, cb, offs, bias, view(dya), view(sv["ya"]), view(sv["lt"]), g, dil)
        dq_l.append(dq.reshape(s, ATTN_OUT))
        dk_l.append(dk.reshape(s, ATTN_OUT))
        dv_l.append(dv.reshape(s, ATTN_OUT))
        dbias_l.append(dbg)
    dz = {"a": step(jnp.concatenate(dq_l + dk_l + dv_l, axis=1)), "bq": dzq, "bk": dzk, "c": dzv, "d": dzg,
          "e": dze}
    dx = dres
    for name, off, width, _ in SECTIONS:
        dx = _matmul(dz[name], w["w_in"][off:off + width], "nn", addend=dx, name=f"dx_in_proj_{name}_{tag}")
    dx = step(dx)
    dw_rows, db_cols = [], []
    for name, off, width, _ in SECTIONS:
        dw, cs = _matmul(dz[name], sv["x"], "tn", colsum=True, name=f"dw_in_proj_{name}_{tag}")
        dw_rows.append(dw)
        db_cols.append(cs)
    gw["w_in"] = jnp.concatenate(dw_rows, axis=0)
    if reductions is not None:
        reductions.start(l, IN_GROUP, gw)
    small = dict(b_in=jnp.concatenate(db_cols, axis=1), ln1_g=dg1, ln1_b=dbt1, ln2_g=dg2, ln2_b=dbt2)
    return dx, gw, small, jnp.concatenate(dbias_l, axis=0)


def _forward_backward(x, target, rel_bias, weights, lnps, reductions=None):
    s = x.shape[0]
    bmaps = _bucket_maps()
    bias = _bias_tiles(rel_bias, bmaps)
    rc = _ret_consts(s)
    prefetch = getattr(weights, "prefetch", lambda v: v)
    saved = []
    h = x
    for l in range(DEPTH):
        h, sv = _layer_fwd(h, weights, l, lnps[l], bias, rc, prefetch)
        saved.append(sv)
    dh, loss_part = _loss_head(h, target)
    gws, smalls, dbiases = [None] * DEPTH, [None] * DEPTH, [None] * DEPTH
    for l in reversed(range(DEPTH)):
        dh, gws[l], smalls[l], dbiases[l] = _layer_bwd(dh, weights.layer(l), l, lnps[l], saved[l], bias, rc,
                                                       reductions)
    d_rel_bias = _bias_tiles_bwd(dbiases, bmaps)
    return loss_part, dh, gws, smalls, d_rel_bias


SMALL_NAMES = ("rel_bias", "b_in", "ln1_g", "ln1_b", "ln2_g", "ln2_b")
SMALL_ROWS = 32


def _pack_small(vals):
    flat = jnp.concatenate([vals[n].reshape(-1) for n in SMALL_NAMES])
    return jnp.pad(flat, (0, SMALL_ROWS * D_MODEL - flat.shape[0])).reshape(SMALL_ROWS, D_MODEL)


def _unpack_small(packed, like):
    flat = packed.reshape(-1)
    out, off = {}, 0
    for n in SMALL_NAMES:
        size = like[n].size
        out[n] = flat[off:off + size].reshape(like[n].shape)
        off += size
    return out


def kernel(x, rel_bias, w_in, b_in, w_attn_proj, w_ret_proj, w_out, ln1_g, ln1_b, w_ffn_gate, w_ffn_up, w_ffn_down, ln2_g, ln2_b, loss_target, m_rel_bias, m_w_in, m_b_in, m_w_attn_proj, m_w_ret_proj, m_w_out, m_ln1_g, m_ln1_b, m_w_ffn_gate, m_w_ffn_up, m_w_ffn_down, m_ln2_g, m_ln2_b, v_rel_bias, v_w_in, v_b_in, v_w_attn_proj, v_w_ret_proj, v_w_out, v_ln1_g, v_ln1_b, v_w_ffn_gate, v_w_ffn_up, v_w_ffn_down, v_ln2_g, v_ln2_b):
    big = dict(w_in=(w_in, m_w_in, v_w_in), w_attn_proj=(w_attn_proj, m_w_attn_proj, v_w_attn_proj),
               w_ret_proj=(w_ret_proj, m_w_ret_proj, v_w_ret_proj), w_out=(w_out, m_w_out, v_w_out),
               w_ffn_gate=(w_ffn_gate, m_w_ffn_gate, v_w_ffn_gate), w_ffn_up=(w_ffn_up, m_w_ffn_up, v_w_ffn_up),
               w_ffn_down=(w_ffn_down, m_w_ffn_down, v_w_ffn_down))
    small_w = dict(rel_bias=rel_bias, b_in=b_in, ln1_g=ln1_g, ln1_b=ln1_b, ln2_g=ln2_g, ln2_b=ln2_b)
    small_m = dict(rel_bias=m_rel_bias, b_in=m_b_in, ln1_g=m_ln1_g, ln1_b=m_ln1_b, ln2_g=m_ln2_g, ln2_b=m_ln2_b)
    small_v = dict(rel_bias=v_rel_bias, b_in=v_b_in, ln1_g=v_ln1_g, ln1_b=v_ln1_b, ln2_g=v_ln2_g, ln2_b=v_ln2_b)

    place = jnp.stack([2 * lax.axis_index("x") + lax.axis_index("y"), lax.axis_index("c")]).astype(jnp.int32)

    weights = _WeightGathers({n: w for n, (w, _, _) in big.items()})
    reductions = _GradReductions(place)
    lnps = [dict(b_in=b_in[l][None], ln1_g=ln1_g[l][None], ln1_b=ln1_b[l][None], ln2_g=ln2_g[l][None],
                 ln2_b=ln2_b[l][None]) for l in range(DEPTH)]
    loss_part, dx, _, smalls, d_rel_bias = _forward_backward(x[0], loss_target[0], rel_bias, weights, lnps,
                                                             reductions)
    loss = lax.psum(loss_part, ("x", "y", "c"))
    grads = _unpack_shard_grads(reductions.results())

    small_g = dict(rel_bias=d_rel_bias)
    for n in SMALL_NAMES[1:]:
        small_g[n] = jnp.concatenate([smalls[l][n] for l in range(DEPTH)], axis=0)
    small_g = _unpack_small(_all_reduce_small(_pack_small(small_g)), small_w)

    delta, new_m, new_v = {}, {}, {}
    for n, (w, m, v) in big.items():
        turn = (lambda t: jnp.swapaxes(t, 1, 2)) if n in COLUMN_SHARDED else (lambda t: t)
        two_d = lambda t: t.reshape(-1, t.shape[-1])
        g = grads[n]
        d_, m_, v_ = _adamw(two_d(turn(w)), two_d(g), two_d(turn(m)), two_d(turn(v)), name=f"adamw_{n}")
        grads[n], delta[n], new_m[n], new_v[n] = (turn(t.reshape(g.shape)) for t in (g, d_, m_, v_))
    d_, m_, v_ = _adamw(_pack_small(small_w), _pack_small(small_g), _pack_small(small_m), _pack_small(small_v),
                        name="adamw_small")
    delta.update(_unpack_small(d_, small_w))
    new_m.update(_unpack_small(m_, small_w))
    new_v.update(_unpack_small(v_, small_w))
    grads.update(small_g)

    order = ("rel_bias", "w_in", "b_in", "w_attn_proj", "w_ret_proj", "w_out", "ln1_g", "ln1_b", "w_ffn_gate",
             "w_ffn_up", "w_ffn_down", "ln2_g", "ln2_b")
    return (loss, dx[None], *[grads[n] for n in order], *[delta[n] for n in order], *[new_m[n] for n in order],
            *[new_v[n] for n in order])
```

```python
import functools

import numpy as np
import jax
import jax.numpy as jnp
from jax import lax
from jax.experimental import pallas as pl
from jax.experimental.pallas import tpu as pltpu
from jax.experimental.pallas import tpu_sc as plsc

F32 = jnp.float32
MXU_DTYPE = jnp.bfloat16

DEPTH = 2
D_MODEL = 1024
HEAD_DIM = 64
ATTN_GROUPS = ((128, 1), (512, 4), (2048, 16))
N_GROUPS = len(ATTN_GROUPS)
HEADS_PER_GROUP = 6
N_ATTN_HEADS = 18
ATTN_WIDTH = 1152
ATTN_OUT = 384
NUM_BUCKETS = 32
MAX_DISTANCE = 2048
RET_HEADS = 4
RET_QK = 256
RET_V = 512
RET_CHUNK = 128
ROPE_BASE = 10000.0
D_FF = 2816
IN_COLS = 11648
ALPHA = (2 * DEPTH) ** 0.25
LN_EPS = 1e-5
GN_EPS = 1e-5
ADAM_LR, ADAM_B1, ADAM_B2, ADAM_EPS, ADAM_WD, ADAM_STEP = 0.001, 0.9, 0.999, 1e-08, 0.01, 10

BLK = 128
NEG = -1e30
N_CHIPS = 4
VMEM_LIMIT = 48 * 1024 * 1024

SECTIONS = (
    ("a", 0, 3456, True),
    ("bq", 3456, 1024, False),
    ("bk", 4480, 1024, False),
    ("c", 5504, 2048, True),
    ("d", 7552, 2048, False),
    ("e", 9600, 2048, False),
)
PACK_GROUPS = ((("w_in", 2912),),
               (("w_attn_proj", 96), ("w_ret_proj", 512), ("w_out", 256), ("w_ffn_gate", 704), ("w_ffn_up", 704),
                ("w_ffn_down", 704)))
IN_GROUP, REST_GROUP = 0, 1
COLUMN_SHARDED = ("w_in", "w_attn_proj", "w_ffn_gate", "w_ffn_up")

NN = ((1,), (0,))
NT = ((1,), (1,))
TN = ((0,), (0,))


def _dot(a, b, dims):
    return lax.dot_general(a.astype(MXU_DTYPE), b.astype(MXU_DTYPE), (dims, ((), ())),
                           preferred_element_type=F32)


def _pick(n, prefs):
    for p in prefs:
        if n % p == 0:
            return p
    raise ValueError(f"no tile for {n} among {prefs}")


def _row_tile(n, most=512, unit=16):
    return max(t for t in range(unit, most + 1, unit) if n % t == 0)


TOKEN_TILES = (1024, 512, 256, 128)
FEATURE_TILES = (1152, 1024, 1408, 384, 256, 128)


def _params(n_axes, limit=VMEM_LIMIT):
    return pltpu.CompilerParams(dimension_semantics=("arbitrary",) * n_axes, vmem_limit_bytes=limit)


def _resident(a):
    return pl.BlockSpec(a.shape, lambda i: (0,) * a.ndim, pipeline_mode=pl.Buffered(1))


def _sigmoid(x):
    return 1.0 / (1.0 + jnp.exp(-x))


def _norm_rows(u, eps):
    mu = jnp.mean(u, axis=-1, keepdims=True)
    xc = u - mu
    var = jnp.mean(xc * xc, axis=-1, keepdims=True)
    rstd = lax.rsqrt(var + eps)
    return xc * rstd, rstd


def _norm_rows_bwd(dxh, xh, rstd):
    c1 = jnp.mean(dxh, axis=-1, keepdims=True)
    c2 = jnp.mean(dxh * xh, axis=-1, keepdims=True)
    return rstd * (dxh - c1 - xh * c2)


def _matmul(a, b, mode, *, name, out_dtype=F32, bias=None, addend=None, colsum=False):
    if mode == "tn":
        kd, m = a.shape
        n = b.shape[1]
        tm, tn_, tk = _pick(m, FEATURE_TILES), _pick(n, FEATURE_TILES), _pick(kd, TOKEN_TILES)
        a_spec = pl.BlockSpec((tk, tm), lambda i, j, k: (k, i))
        b_spec = pl.BlockSpec((tk, tn_), lambda i, j, k: (k, j))
    else:
        m, kd = a.shape
        n = b.shape[0] if mode == "nt" else b.shape[1]
        tm, tn_, tk = _pick(m, TOKEN_TILES), _pick(n, FEATURE_TILES), _pick(kd, FEATURE_TILES)
        a_spec = pl.BlockSpec((tm, tk), lambda i, j, k: (i, k))
        if mode == "nt":
            b_spec = pl.BlockSpec((tn_, tk), lambda i, j, k: (j, k))
        else:
            b_spec = pl.BlockSpec((tk, tn_), lambda i, j, k: (k, j))
    dims = {"nn": NN, "nt": NT, "tn": TN}[mode]
    nk = kd // tk
    has_bias, has_add = bias is not None, addend is not None
    assert not colsum or mode == "tn"

    def body(*refs):
        it = iter(refs)
        a_ref, b_ref = next(it), next(it)
        bias_ref = next(it) if has_bias else None
        add_ref = next(it) if has_add else None
        o_ref = next(it)
        cs_ref = next(it) if colsum else None
        acc_ref = next(it)
        j, k = pl.program_id(1), pl.program_id(2)

        @pl.when(k == 0)
        def _():
            if has_add:
                acc_ref[...] = add_ref[...].astype(F32)
            else:
                acc_ref[...] = jnp.zeros_like(acc_ref)

        av = a_ref[...]
        acc_ref[...] += _dot(av, b_ref[...], dims)
        if colsum:
            @pl.when(jnp.logical_and(j == 0, k == 0))
            def _():
                cs_ref[...] = jnp.zeros_like(cs_ref)

            @pl.when(j == 0)
            def _():
                cs_ref[...] += jnp.sum(av.astype(F32), axis=0, keepdims=True)

        @pl.when(k == nk - 1)
        def _():
            r = acc_ref[...]
            if has_bias:
                r = r + bias_ref[...]
            o_ref[...] = r.astype(out_dtype)

    in_specs, args = [a_spec, b_spec], [a, b]
    if has_bias:
        in_specs.append(pl.BlockSpec((1, tn_), lambda i, j, k: (0, j)))
        args.append(bias)
    if has_add:
        in_specs.append(pl.BlockSpec((tm, tn_), lambda i, j, k: (i, j)))
        args.append(addend)
    out_shape = [jax.ShapeDtypeStruct((m, n), out_dtype)]
    out_specs = [pl.BlockSpec((tm, tn_), lambda i, j, k: (i, j))]
    if colsum:
        out_shape.append(jax.ShapeDtypeStruct((1, m), F32))
        out_specs.append(pl.BlockSpec((1, tm), lambda i, j, k: (0, i)))
    res = pl.pallas_call(
        body, out_shape=out_shape, grid=(m // tm, n // tn_, nk), in_specs=in_specs, out_specs=out_specs,
        scratch_shapes=[pltpu.VMEM((tm, tn_), F32)], compiler_params=_params(3), name=name,
    )(*args)
    return res if colsum else res[0]


def _t5_bucket(dist):
    max_exact = NUM_BUCKETS // 2
    large = max_exact + (np.log(np.maximum(dist, max_exact) / max_exact)
                         / np.log(MAX_DISTANCE / max_exact) * (NUM_BUCKETS - max_exact)).astype(np.int32)
    large = np.minimum(large, NUM_BUCKETS - 1)
    return np.where(dist < max_exact, dist, large).astype(np.int32)


def _bucket_maps():
    qi = np.arange(BLK)[:, None]
    kj = np.arange(2 * BLK)[None, :]
    rel = np.clip(qi + BLK - kj, 0, BLK)
    return jnp.asarray(np.stack([_t5_bucket(rel * d) for _, d in ATTN_GROUPS]))


def _bias_tiles(rel_bias, bmaps):
    def body(tab_ref, bm_ref, o_ref):
        h = pl.program_id(0)
        bm = bm_ref[0]
        acc = jnp.zeros((BLK, 2 * BLK), F32)
        for b in range(NUM_BUCKETS):
            acc = jnp.where(bm == b, tab_ref[b, h], acc)
        o_ref[0] = acc

    return pl.pallas_call(
        body, out_shape=jax.ShapeDtypeStruct((N_ATTN_HEADS, BLK, 2 * BLK), F32), grid=(N_ATTN_HEADS,),
        in_specs=[pl.BlockSpec(memory_space=pltpu.SMEM),
                  pl.BlockSpec((1, BLK, 2 * BLK), lambda h: (h // HEADS_PER_GROUP, 0, 0))],
        out_specs=pl.BlockSpec((1, BLK, 2 * BLK), lambda h: (h, 0, 0)),
        compiler_params=_params(1), name="bias_tiles",
    )(rel_bias, bmaps)


def _bias_tiles_bwd(dbias_layers, bmaps):
    nl = len(dbias_layers)

    def body(*refs):
        bm = refs[nl][0]
        o_ref = refs[nl + 1]
        x = refs[0][0]
        for r in refs[1:nl]:
            x = x + r[0]
        lane = lax.broadcasted_iota(jnp.int32, (1, BLK), 1)
        row = jnp.zeros((1, BLK), F32)
        for b in range(NUM_BUCKETS):
            s = jnp.sum(jnp.where(bm == b, x, 0.0), axis=1, keepdims=True)
            s = jnp.sum(s, axis=0, keepdims=True)
            row = jnp.where(lane == b, s, row)
        o_ref[0] = row

    tile = pl.BlockSpec((1, BLK, 2 * BLK), lambda h: (h, 0, 0))
    out = pl.pallas_call(
        body, out_shape=jax.ShapeDtypeStruct((N_ATTN_HEADS, 1, BLK), F32), grid=(N_ATTN_HEADS,),
        in_specs=[tile] * nl + [pl.BlockSpec((1, BLK, 2 * BLK), lambda h: (h // HEADS_PER_GROUP, 0, 0))],
        out_specs=pl.BlockSpec((1, 1, BLK), lambda h: (h, 0, 0)),
        compiler_params=_params(1), name="bias_tiles_bwd",
    )(*dbias_layers, bmaps)
    return out[:, 0, :NUM_BUCKETS].T


def _pair_masks():
    lane = lax.broadcasted_iota(jnp.int32, (BLK, BLK), 1)
    row2 = lax.broadcasted_iota(jnp.int32, (2 * BLK, BLK), 0)
    lane2 = lax.broadcasted_iota(jnp.int32, (2 * BLK, BLK), 1)
    own = (lane2 // HEAD_DIM) == (row2 // BLK)
    qi = lax.broadcasted_iota(jnp.int32, (2 * BLK, 2 * BLK), 0) & (BLK - 1)
    kj = lax.broadcasted_iota(jnp.int32, (2 * BLK, 2 * BLK), 1)
    band = jnp.logical_and(kj >= qi, kj <= qi + BLK)
    return lane < HEAD_DIM, own, band, kj < BLK


def _pair_scores(q32, kb, bias2, own, band, is_prev, pen):
    qm = jnp.where(own, jnp.concatenate([q32, q32], axis=0), 0.0)
    s = _dot(qm, kb, NT) * (HEAD_DIM ** -0.5) + bias2
    if pen is not None:
        s = s + jnp.where(is_prev, pen, 0.0)
    return jnp.where(band, s, NEG), qm


HEAD_PAIRS = HEADS_PER_GROUP // 2


def _attn_fwd(qa, ka, va, cb, offs, bias, g, dil):
    rows = qa.shape[0]
    nb = rows // BLK
    rb = 2 if nb % 2 == 0 else 1
    oq, ok, ov = offs

    def cur(off):
        return pl.BlockSpec((rb * BLK, ATTN_OUT), lambda r, n: (n, r * cb + off))

    def prev(off):
        return pl.BlockSpec((BLK, ATTN_OUT), lambda r, n: (jnp.maximum(rb * n - 1, 0), r * cb + off))

    def body(q_ref, kp_ref, kc_ref, vp_ref, vc_ref, b_ref, o_ref, l_ref):
        n = pl.program_id(1)
        pen0 = jnp.where(n > 0, 0.0, NEG)
        first, own, band, is_prev = _pair_masks()
        o_rows, l_rows = [], []
        for t in range(rb):
            rows_t = slice(t * BLK, (t + 1) * BLK)
            o_parts, l_parts = [], []
            for hp in range(HEAD_PAIRS):
                cols = slice(hp * BLK, (hp + 1) * BLK)
                if t == 0:
                    kp, vp, pen = kp_ref[:, cols], vp_ref[:, cols], pen0
                else:
                    before = slice((t - 1) * BLK, t * BLK)
                    kp, vp, pen = kc_ref[before, cols], vc_ref[before, cols], None
                kb = jnp.concatenate([kp, kc_ref[rows_t, cols]], axis=0)
                vb = jnp.concatenate([vp, vc_ref[rows_t, cols]], axis=0)
                bias2 = jnp.concatenate([b_ref[2 * hp], b_ref[2 * hp + 1]], axis=0)
                s, _ = _pair_scores(q_ref[rows_t, cols].astype(F32), kb, bias2, own, band, is_prev, pen)
                m = jnp.max(s, axis=1, keepdims=True)
                p = jnp.exp(s - m)
                l = jnp.sum(p, axis=1, keepdims=True)
                o2 = _dot(p * (1.0 / l), vb, NN)
                lse2 = m + jnp.log(l)
                o_parts.append(jnp.where(first, o2[:BLK], o2[BLK:]))
                l_parts.append(jnp.where(first, lse2[:BLK], lse2[BLK:]))
            o_rows.append(jnp.concatenate(o_parts, axis=1))
            l_rows.append(jnp.concatenate(l_parts, axis=1))
        o_ref[...] = jnp.concatenate(o_rows, axis=0)
        l_ref[...] = jnp.concatenate(l_rows, axis=0)

    out_spec = pl.BlockSpec((rb * BLK, ATTN_OUT), lambda r, n: (n, r))
    shape = jax.ShapeDtypeStruct((rows, dil * ATTN_OUT), F32)
    return pl.pallas_call(
        body, out_shape=[shape, shape], grid=(dil, nb // rb),
        in_specs=[cur(oq), prev(ok), cur(ok), prev(ov), cur(ov),
                  pl.BlockSpec((HEADS_PER_GROUP, BLK, 2 * BLK), lambda r, n: (g, 0, 0))],
        out_specs=[out_spec, out_spec], compiler_params=_params(2), name=f"attn_fwd_g{g}",
    )(qa, ka, ka, va, va, bias)


def _attn_bwd(qa, ka, va, cb, offs, bias, dy, ya, lt, g, dil):
    rows = qa.shape[0]
    nb = rows // BLK
    oq, ok, ov = offs

    def cur(off, c):
        return pl.BlockSpec((BLK, ATTN_OUT), lambda r, n: (jnp.minimum(n, nb - 1), r * c + off))

    def prev(off, c):
        return pl.BlockSpec((BLK, ATTN_OUT), lambda r, n: (jnp.clip(n - 1, 0, nb - 1), r * c + off))

    def body(q_ref, kp_ref, kc_ref, vp_ref, vc_ref, b_ref, dy_ref, ya_ref, lt_ref,
             dq_ref, dk_ref, dv_ref, db_ref, ck_ref, cv_ref):
        r, n = pl.program_id(0), pl.program_id(1)

        @pl.when(jnp.logical_and(r == 0, n == 0))
        def _():
            db_ref[...] = jnp.zeros_like(db_ref)

        @pl.when(n == 0)
        def _():
            ck_ref[...] = jnp.zeros_like(ck_ref)
            cv_ref[...] = jnp.zeros_like(cv_ref)

        @pl.when(n < nb)
        def _():
            pen = jnp.where(n > 0, 0.0, NEG)
            first, own, band, is_prev = _pair_masks()
            second = jnp.logical_not(first)
            scale = HEAD_DIM ** -0.5
            parts = {k: [] for k in ("dq", "dkp", "dkc", "dvp", "dvc")}
            db_parts = []
            for hp in range(HEAD_PAIRS):
                cols = slice(hp * BLK, (hp + 1) * BLK)
                kb = jnp.concatenate([kp_ref[:, cols], kc_ref[:, cols]], axis=0)
                vb = jnp.concatenate([vp_ref[:, cols], vc_ref[:, cols]], axis=0)
                bias2 = jnp.concatenate([b_ref[2 * hp], b_ref[2 * hp + 1]], axis=0)
                dy_, lt_ = dy_ref[:, cols], lt_ref[:, cols]
                dyy = dy_ * ya_ref[:, cols]
                per_head = lambda t, red, fill: jnp.concatenate(
                    [red(jnp.where(first, t, fill), axis=1, keepdims=True),
                     red(jnp.where(second, t, fill), axis=1, keepdims=True)], axis=0)
                lse2 = per_head(lt_, jnp.max, NEG)
                delta2 = per_head(dyy, jnp.sum, 0.0)
                s, qm = _pair_scores(q_ref[:, cols].astype(F32), kb, bias2, own, band, is_prev, pen)
                p = jnp.exp(s - lse2)
                dym = jnp.where(own, jnp.concatenate([dy_, dy_], axis=0), 0.0)
                ds = p * (_dot(dym, vb, NT) - delta2)
                db_parts += [ds[:BLK], ds[BLK:]]
                dq2 = _dot(ds, kb, NN) * scale
                dkb = _dot(ds, qm, TN) * scale
                dvb = _dot(p, dym, TN)
                for k, val in (("dq", jnp.where(first, dq2[:BLK], dq2[BLK:])), ("dkp", dkb[:BLK]),
                               ("dkc", dkb[BLK:]), ("dvp", dvb[:BLK]), ("dvc", dvb[BLK:])):
                    parts[k].append(val)
            wide = {k: jnp.concatenate(val, axis=1) for k, val in parts.items()}
            db_ref[...] += jnp.stack(db_parts, axis=0)
            dq_ref[...] = wide["dq"].astype(dq_ref.dtype)
            dk_ref[...] = (ck_ref[...] + wide["dkp"]).astype(dk_ref.dtype)
            dv_ref[...] = (cv_ref[...] + wide["dvp"]).astype(dv_ref.dtype)
            ck_ref[...] = wide["dkc"]
            cv_ref[...] = wide["dvc"]

        @pl.when(n == nb)
        def _():
            dk_ref[...] = ck_ref[...].astype(dk_ref.dtype)
            dv_ref[...] = cv_ref[...].astype(dv_ref.dtype)

    late = pl.BlockSpec((BLK, ATTN_OUT), lambda r, n: (jnp.maximum(n - 1, 0), r))
    shape = jax.ShapeDtypeStruct((rows, dil * ATTN_OUT), MXU_DTYPE)
    return pl.pallas_call(
        body,
        out_shape=[shape, shape, shape, jax.ShapeDtypeStruct((HEADS_PER_GROUP, BLK, 2 * BLK), F32)],
        grid=(dil, nb + 1),
        in_specs=[cur(oq, cb), prev(ok, cb), cur(ok, cb), prev(ov, cb), cur(ov, cb),
                  pl.BlockSpec((HEADS_PER_GROUP, BLK, 2 * BLK), lambda r, n: (g, 0, 0)),
                  cur(0, 1), cur(0, 1), cur(0, 1)],
        out_specs=[cur(0, 1), late, late, pl.BlockSpec((HEADS_PER_GROUP, BLK, 2 * BLK), lambda r, n: (0, 0, 0))],
        scratch_shapes=[pltpu.VMEM((BLK, ATTN_OUT), F32), pltpu.VMEM((BLK, ATTN_OUT), F32)],
        compiler_params=_params(2), name=f"attn_bwd_g{g}",
    )(qa, ka, ka, va, va, bias, dy, ya, lt)


def _attn_combine(o_list, l_list):
    s = o_list[0].shape[0]
    tr = _pick(s, (1024, 512, 256, 128))

    def body(o0, o1, o2, l0, l1, l2, y_ref, yb_ref, lt_ref):
        a0, a1, a2 = l0[...], l1[...], l2[...]
        mx = jnp.maximum(jnp.maximum(a0, a1), a2)
        e0, e1, e2 = jnp.exp(a0 - mx), jnp.exp(a1 - mx), jnp.exp(a2 - mx)
        den = e0 + e1 + e2
        inv = 1.0 / den
        y = (e0 * inv) * o0[...] + (e1 * inv) * o1[...] + (e2 * inv) * o2[...]
        y_ref[...] = y
        yb_ref[...] = y.astype(yb_ref.dtype)
        lt_ref[...] = mx + jnp.log(den)

    spec = pl.BlockSpec((tr, ATTN_OUT), lambda i: (i, 0))
    f = jax.ShapeDtypeStruct((s, ATTN_OUT), F32)
    return pl.pallas_call(
        body, out_shape=[f, jax.ShapeDtypeStruct((s, ATTN_OUT), MXU_DTYPE), f], grid=(s // tr,),
        in_specs=[spec] * 6, out_specs=[spec] * 3, compiler_params=_params(1), name="attn_combine",
    )(*o_list, *l_list)


def _ret_consts(s):
    half = RET_QK // 2
    pos = jnp.arange(s, dtype=F32)
    inv_freq = ROPE_BASE ** (-jnp.arange(half, dtype=F32) / half)
    ang = pos[:, None] * inv_freq[None]
    log_g = jnp.log(1.0 - 2.0 ** (-5.0 - jnp.arange(RET_HEADS, dtype=F32)))
    n = jnp.arange(RET_CHUNK, dtype=F32)
    diff = n[:, None] - n[None, :]
    dmask = jnp.where(diff >= 0, jnp.exp(log_g[:, None, None] * jnp.maximum(diff, 0.0)), 0.0)
    qdec = jnp.exp(log_g[:, None] * (n + 1.0))
    kdec = jnp.exp(log_g[:, None] * (RET_CHUNK - 1.0 - n))
    cdec = jnp.exp(log_g * RET_CHUNK)
    wide = (RET_HEADS, RET_CHUNK, RET_QK)
    return dict(cos=jnp.cos(ang), sin=jnp.sin(ang), dmask=dmask,
                qdec=jnp.broadcast_to(qdec[:, :, None], wide), kdec=jnp.broadcast_to(kdec[:, :, None], wide),
                cdec=cdec)


def _rot(t, cs, sn):
    half = RET_QK // 2
    t1, t2 = t[:, :half], t[:, half:]
    return jnp.concatenate([t1 * cs - t2 * sn, t1 * sn + t2 * cs], axis=1)


def _rot_bwd(d, cs, sn):
    half = RET_QK // 2
    d1, d2 = d[:, :half], d[:, half:]
    return jnp.concatenate([d1 * cs + d2 * sn, d2 * cs - d1 * sn], axis=1)


def _ret_fwd(zq, zk, zv, zg, rc):
    s = zq.shape[0]
    nc = s // RET_CHUNK
    c = RET_CHUNK

    def body(cd_ref, q_ref, k_ref, v_ref, g_ref, cos_ref, sin_ref, dm_ref, qd_ref, kd_ref,
             yb_ref, o_ref, st_ref, state):
        n = pl.program_id(0)

        @pl.when(n == 0)
        def _():
            state[...] = jnp.zeros_like(state)

        cs, sn = cos_ref[...], sin_ref[...]
        for h in range(RET_HEADS):
            qs, vs = slice(h * RET_QK, (h + 1) * RET_QK), slice(h * RET_V, (h + 1) * RET_V)
            qr = _rot(q_ref[:, qs], cs, sn)
            kr = _rot(k_ref[:, qs], cs, sn) * (RET_QK ** -0.5)
            v = v_ref[:, vs]
            st = state[h]
            st_ref[h, 0] = st.astype(st_ref.dtype)
            sc = _dot(qr, kr, NT) * dm_ref[h]
            o = _dot(sc, v, NN) + _dot(qr * qd_ref[h], st, NN)
            state[h] = st * cd_ref[h] + _dot(kr * kd_ref[h], v, TN)
            o_ref[:, vs] = o
            xh, _ = _norm_rows(o, GN_EPS)
            gv = g_ref[:, vs]
            yb_ref[:, vs] = (gv * _sigmoid(gv) * xh).astype(yb_ref.dtype)

    row = lambda w: pl.BlockSpec((c, w), lambda n: (n, 0))
    const = lambda a: pl.BlockSpec(a.shape, lambda n: (0, 0, 0))
    return pl.pallas_call(
        body,
        out_shape=[jax.ShapeDtypeStruct((s, RET_HEADS * RET_V), MXU_DTYPE),
                   jax.ShapeDtypeStruct((s, RET_HEADS * RET_V), F32),
                   jax.ShapeDtypeStruct((RET_HEADS, nc, RET_QK, RET_V), MXU_DTYPE)],
        grid=(nc,),
        in_specs=[pl.BlockSpec(memory_space=pltpu.SMEM), row(RET_HEADS * RET_QK), row(RET_HEADS * RET_QK),
                  row(RET_HEADS * RET_V), row(RET_HEADS * RET_V), row(RET_QK // 2), row(RET_QK // 2),
                  const(rc["dmask"]), const(rc["qdec"]), const(rc["kdec"])],
        out_specs=[row(RET_HEADS * RET_V), row(RET_HEADS * RET_V),
                   pl.BlockSpec((RET_HEADS, 1, RET_QK, RET_V), lambda n: (0, n, 0, 0))],
        scratch_shapes=[pltpu.VMEM((RET_HEADS, RET_QK, RET_V), F32)],
        compiler_params=_params(1), name="ret_fwd",
    )(rc["cdec"], zq, zk, zv, zg, rc["cos"], rc["sin"], rc["dmask"], rc["qdec"], rc["kdec"])


def _ret_bwd(zq, zk, zv, zg, o_ret, states, dyb, rc):
    s = zq.shape[0]
    nc = s // RET_CHUNK
    c = RET_CHUNK

    def body(cd_ref, q_ref, k_ref, v_ref, g_ref, o_ref, st_ref, dy_ref, cos_ref, sin_ref, dm_ref, qd_ref,
             kd_ref, dq_ref, dk_ref, dv_ref, dg_ref, dstate):
        n = pl.program_id(0)

        @pl.when(n == 0)
        def _():
            dstate[...] = jnp.zeros_like(dstate)

        cs, sn = cos_ref[...], sin_ref[...]
        for h in range(RET_HEADS):
            qs, vs = slice(h * RET_QK, (h + 1) * RET_QK), slice(h * RET_V, (h + 1) * RET_V)
            qr = _rot(q_ref[:, qs], cs, sn)
            kr = _rot(k_ref[:, qs], cs, sn) * (RET_QK ** -0.5)
            v = v_ref[:, vs]
            st = st_ref[h, 0]
            dm, qd, kd = dm_ref[h], qd_ref[h], kd_ref[h]
            xh, rstd = _norm_rows(o_ref[:, vs], GN_EPS)
            gv, dy = g_ref[:, vs], dy_ref[:, vs]
            sg = _sigmoid(gv)
            dg_ref[:, vs] = (dy * xh * (sg * (1.0 + gv * (1.0 - sg)))).astype(dg_ref.dtype)
            do = _norm_rows_bwd(dy * (gv * sg), xh, rstd)
            ds_next = dstate[h]
            sc = _dot(qr, kr, NT) * dm
            da = _dot(do, v, NT) * dm
            dv = _dot(sc, do, TN) + _dot(kr * kd, ds_next, NN)
            dqr = _dot(da, kr, NN) + _dot(do, st, NT) * qd
            dkr = _dot(da, qr, TN) + _dot(v, ds_next, NT) * kd
            dstate[h] = ds_next * cd_ref[h] + _dot(qr * qd, do, TN)
            dq_ref[:, qs] = _rot_bwd(dqr, cs, sn).astype(dq_ref.dtype)
            dk_ref[:, qs] = _rot_bwd(dkr * (RET_QK ** -0.5), cs, sn).astype(dk_ref.dtype)
            dv_ref[:, vs] = dv.astype(dv_ref.dtype)

    row = lambda w: pl.BlockSpec((c, w), lambda n: (nc - 1 - n, 0))
    const = lambda a: pl.BlockSpec(a.shape, lambda n: (0, 0, 0))
    qk_w, v_w = RET_HEADS * RET_QK, RET_HEADS * RET_V
    g_qk, g_v = jax.ShapeDtypeStruct((s, qk_w), MXU_DTYPE), jax.ShapeDtypeStruct((s, v_w), MXU_DTYPE)
    return pl.pallas_call(
        body,
        out_shape=[g_qk, g_qk, g_v, g_v],
        grid=(nc,),
        in_specs=[pl.BlockSpec(memory_space=pltpu.SMEM), row(qk_w), row(qk_w), row(v_w), row(v_w), row(v_w),
                  pl.BlockSpec((RET_HEADS, 1, RET_QK, RET_V), lambda n: (0, nc - 1 - n, 0, 0)), row(v_w),
                  row(RET_QK // 2), row(RET_QK // 2), const(rc["dmask"]), const(rc["qdec"]), const(rc["kdec"])],
        out_specs=[row(qk_w), row(qk_w), row(v_w), row(v_w)],
        scratch_shapes=[pltpu.VMEM((RET_HEADS, RET_QK, RET_V), F32)],
        compiler_params=_params(1), name="ret_bwd",
    )(rc["cdec"], zq, zk, zv, zg, o_ret, states, dyb, rc["cos"], rc["sin"], rc["dmask"], rc["qdec"],
      rc["kdec"])


ROW_TILE = 256


def _mix_fwd(x, ya_b, yb_b, ze, wap_t, wrp, wout, gam, bet):
    s = x.shape[0]
    tm = ROW_TILE

    def body(x_ref, ya_ref, yb_ref, ze_ref, wap_ref, wrp_ref, wout_ref, g_ref, b_ref,
             x1_ref, u1_ref, a1_ref, a2_ref, mg_ref, x1b_ref):
        a1 = _dot(ya_ref[...], wap_ref[...], NT)
        a2 = _dot(yb_ref[...], wrp_ref[...], NN)
        ze_ = ze_ref[...]
        merged = _sigmoid(ze_[:, :D_MODEL]) * a1 + _sigmoid(ze_[:, D_MODEL:]) * a2
        u = ALPHA * x_ref[...] + _dot(merged, wout_ref[...], NN)
        xh, _ = _norm_rows(u, LN_EPS)
        x1 = xh * g_ref[...] + b_ref[...]
        x1_ref[...] = x1
        x1b_ref[...] = x1.astype(x1b_ref.dtype)
        u1_ref[...] = u
        a1_ref[...] = a1
        a2_ref[...] = a2
        mg_ref[...] = merged.astype(mg_ref.dtype)

    row = lambda w: pl.BlockSpec((tm, w), lambda i: (i, 0))
    full = _resident
    f = jax.ShapeDtypeStruct((s, D_MODEL), F32)
    m = jax.ShapeDtypeStruct((s, D_MODEL), MXU_DTYPE)
    return pl.pallas_call(
        body, out_shape=[f, f, f, f, m, m], grid=(s // tm,),
        in_specs=[row(D_MODEL), row(ATTN_OUT), row(RET_HEADS * RET_V), row(2 * D_MODEL), full(wap_t), full(wrp),
                  full(wout), full(gam), full(bet)],
        out_specs=[row(D_MODEL)] * 6, compiler_params=_params(1), name="mix_fwd",
    )(x, ya_b, yb_b, ze, wap_t, wrp, wout, gam, bet)


def _mix_bwd(dx1, u1, a1, a2, ze, wap_t, wrp, wout, gam):
    s = dx1.shape[0]
    tm = ROW_TILE

    def body(dx_ref, u_ref, a1_ref, a2_ref, ze_ref, wap_ref, wrp_ref, wout_ref, g_ref,
             dres_ref, du_ref, da1_ref, da2_ref, dze_ref, dya_ref, dyb_ref, dgam_ref, dbet_ref):
        @pl.when(pl.program_id(0) == 0)
        def _():
            dgam_ref[...] = jnp.zeros_like(dgam_ref)
            dbet_ref[...] = jnp.zeros_like(dbet_ref)

        dx = dx_ref[...]
        xh, rstd = _norm_rows(u_ref[...], LN_EPS)
        dgam_ref[...] += jnp.sum(dx * xh, axis=0, keepdims=True)
        dbet_ref[...] += jnp.sum(dx, axis=0, keepdims=True)
        du = _norm_rows_bwd(dx * g_ref[...], xh, rstd)
        dres_ref[...] = ALPHA * du
        du_ref[...] = du.astype(du_ref.dtype)
        dm = _dot(du, wout_ref[...], NT)
        ze_ = ze_ref[...]
        sa, sb = _sigmoid(ze_[:, :D_MODEL]), _sigmoid(ze_[:, D_MODEL:])
        da1, da2 = dm * sa, dm * sb
        dze_ref[...] = jnp.concatenate([dm * a1_ref[...] * (sa * (1.0 - sa)),
                                        dm * a2_ref[...] * (sb * (1.0 - sb))], axis=1).astype(dze_ref.dtype)
        da1_ref[...] = da1.astype(da1_ref.dtype)
        da2_ref[...] = da2.astype(da2_ref.dtype)
        dya_ref[...] = _dot(da1, wap_ref[...], NN)
        dyb_ref[...] = _dot(da2, wrp_ref[...], NT)

    row = lambda w: pl.BlockSpec((tm, w), lambda i: (i, 0))
    full = _resident
    vec = pl.BlockSpec((1, D_MODEL), lambda i: (0, 0))
    f = lambda w: jax.ShapeDtypeStruct((s, w), F32)
    m = lambda w: jax.ShapeDtypeStruct((s, w), MXU_DTYPE)
    v = jax.ShapeDtypeStruct((1, D_MODEL), F32)
    return pl.pallas_call(
        body,
        out_shape=[f(D_MODEL), m(D_MODEL), m(D_MODEL), m(D_MODEL), m(2 * D_MODEL), f(ATTN_OUT),
                   f(RET_HEADS * RET_V), v, v],
        grid=(s // tm,),
        in_specs=[row(D_MODEL)] * 4 + [row(2 * D_MODEL), full(wap_t), full(wrp), full(wout), full(gam)],
        out_specs=[row(D_MODEL)] * 4 + [row(2 * D_MODEL), row(ATTN_OUT), row(RET_HEADS * RET_V), vec, vec],
        compiler_params=_params(1), name="mix_bwd",
    )(dx1, u1, a1, a2, ze, wap_t, wrp, wout, gam)


FF_CHUNK = 1408


def _ffn_fwd(x1, wg_t, wu_t, wd, gam, bet):
    s = x1.shape[0]
    tm, fc = ROW_TILE, FF_CHUNK

    def body(x_ref, wg_ref, wu_ref, wd_ref, g_ref, b_ref, x2_ref, u2_ref, a_ref, b_out_ref, h_ref, x2b_ref):
        xv = x_ref[...]
        u = ALPHA * xv
        for f0 in range(0, D_FF, fc):
            ch = slice(f0, f0 + fc)
            a = _dot(xv, wg_ref[ch, :], NT)
            b = _dot(xv, wu_ref[ch, :], NT)
            hid = a * _sigmoid(a) * b
            u = u + _dot(hid, wd_ref[ch, :], NN)
            a_ref[:, ch] = a
            b_out_ref[:, ch] = b
            h_ref[:, ch] = hid.astype(h_ref.dtype)
        xh, _ = _norm_rows(u, LN_EPS)
        u2_ref[...] = u
        x2 = xh * g_ref[...] + b_ref[...]
        x2_ref[...] = x2
        x2b_ref[...] = x2.astype(x2b_ref.dtype)

    row = lambda w: pl.BlockSpec((tm, w), lambda i: (i, 0))
    f = lambda w: jax.ShapeDtypeStruct((s, w), F32)
    m = lambda w: jax.ShapeDtypeStruct((s, w), MXU_DTYPE)
    return pl.pallas_call(
        body, out_shape=[f(D_MODEL), f(D_MODEL), f(D_FF), f(D_FF), m(D_FF), m(D_MODEL)],
        grid=(s // tm,),
        in_specs=[row(D_MODEL), _resident(wg_t), _resident(wu_t), _resident(wd), _resident(gam), _resident(bet)],
        out_specs=[row(D_MODEL), row(D_MODEL), row(D_FF), row(D_FF), row(D_FF), row(D_MODEL)],
        compiler_params=_params(1), name="ffn_fwd",
    )(x1, wg_t, wu_t, wd, gam, bet)


def _ffn_bwd(dx2, u2, fa, fb, wg_t, wu_t, wd, gam):
    s = dx2.shape[0]
    tm, fc = ROW_TILE, FF_CHUNK

    def body(dx_ref, u_ref, a_ref, b_ref, wg_ref, wu_ref, wd_ref, g_ref,
             dx1_ref, du_ref, da_ref, db_ref, dgam_ref, dbet_ref):
        @pl.when(pl.program_id(0) == 0)
        def _():
            dgam_ref[...] = jnp.zeros_like(dgam_ref)
            dbet_ref[...] = jnp.zeros_like(dbet_ref)

        dx = dx_ref[...]
        xh, rstd = _norm_rows(u_ref[...], LN_EPS)
        dgam_ref[...] += jnp.sum(dx * xh, axis=0, keepdims=True)
        dbet_ref[...] += jnp.sum(dx, axis=0, keepdims=True)
        du = _norm_rows_bwd(dx * g_ref[...], xh, rstd)
        du_ref[...] = du.astype(du_ref.dtype)
        acc = ALPHA * du
        for f0 in range(0, D_FF, fc):
            ch = slice(f0, f0 + fc)
            dh = _dot(du, wd_ref[ch, :], NT)
            a, b = a_ref[:, ch], b_ref[:, ch]
            sg = _sigmoid(a)
            da = dh * b * (sg * (1.0 + a * (1.0 - sg)))
            db = dh * (a * sg)
            acc = acc + _dot(da, wg_ref[ch, :], NN) + _dot(db, wu_ref[ch, :], NN)
            da_ref[:, ch] = da.astype(da_ref.dtype)
            db_ref[:, ch] = db.astype(db_ref.dtype)
        dx1_ref[...] = acc

    row = lambda w: pl.BlockSpec((tm, w), lambda i: (i, 0))
    vec = pl.BlockSpec((1, D_MODEL), lambda i: (0, 0))
    v = jax.ShapeDtypeStruct((1, D_MODEL), F32)
    return pl.pallas_call(
        body,
        out_shape=[jax.ShapeDtypeStruct((s, D_MODEL), F32), jax.ShapeDtypeStruct((s, D_MODEL), MXU_DTYPE),
                   jax.ShapeDtypeStruct((s, D_FF), MXU_DTYPE), jax.ShapeDtypeStruct((s, D_FF), MXU_DTYPE), v, v],
        grid=(s // tm,),
        in_specs=[row(D_MODEL), row(D_MODEL), row(D_FF), row(D_FF), _resident(wg_t), _resident(wu_t),
                  _resident(wd), _resident(gam)],
        out_specs=[row(D_MODEL), row(D_MODEL), row(D_FF), row(D_FF), vec, vec],
        compiler_params=_params(1), name="ffn_bwd",
    )(dx2, u2, fa, fb, wg_t, wu_t, wd, gam)


def _loss_head(y, target):
    s = y.shape[0]
    tr = _pick(s, (512, 256, 128))

    def body(y_ref, t_ref, dy_ref, l_ref):
        @pl.when(pl.program_id(0) == 0)
        def _():
            l_ref[...] = jnp.zeros_like(l_ref)

        e = y_ref[...] - t_ref[...]
        dy_ref[...] = e * (1.0 / D_MODEL)
        part = jnp.sum(jnp.sum(e * e, axis=1, keepdims=True), axis=0, keepdims=True)
        l_ref[...] += part * (0.5 / D_MODEL)

    spec = pl.BlockSpec((tr, D_MODEL), lambda i: (i, 0))
    dy, part = pl.pallas_call(
        body, out_shape=[jax.ShapeDtypeStruct((s, D_MODEL), F32), jax.ShapeDtypeStruct((8, BLK), F32)],
        grid=(s // tr,), in_specs=[spec, spec], out_specs=[spec, pl.BlockSpec((8, BLK), lambda i: (0, 0))],
        compiler_params=_params(1), name="loss_head",
    )(y, target)
    return dy, part[0, 0]


def _adamw(w, g, m, v, name):
    rows, cols = w.shape
    budget = 1 << 20
    cands = [t for t in range(8, rows + 1, 8) if rows % t == 0 and t * cols * 4 <= budget]
    tr = max(cands) if cands else rows

    def body(w_ref, g_ref, m_ref, v_ref, d_ref, nm_ref, nv_ref):
        gv = g_ref[...]
        mn = ADAM_B1 * m_ref[...] + (1.0 - ADAM_B1) * gv
        vn = ADAM_B2 * v_ref[...] + (1.0 - ADAM_B2) * (gv * gv)
        m_hat = mn / (1.0 - ADAM_B1 ** ADAM_STEP)
        v_hat = vn / (1.0 - ADAM_B2 ** ADAM_STEP)
        d_ref[...] = -ADAM_LR * (m_hat / (jnp.sqrt(v_hat) + ADAM_EPS) + ADAM_WD * w_ref[...])
        nm_ref[...] = mn
        nv_ref[...] = vn

    spec = pl.BlockSpec((tr, cols), lambda i: (i, 0))
    shape = jax.ShapeDtypeStruct((rows, cols), F32)
    return pl.pallas_call(
        body, out_shape=[shape, shape, shape], grid=(rows // tr,), in_specs=[spec] * 4, out_specs=[spec] * 3,
        compiler_params=_params(1), name=name,
    )(w, g, m, v)


MESH_ID = pl.DeviceIdType.MESH
ANY = pl.BlockSpec(memory_space=pl.ANY)


def _place():
    x, y, c = lax.axis_index("x"), lax.axis_index("y"), lax.axis_index("c")
    other_chips = [(1 - x, y), (x, 1 - y), (1 - x, 1 - y)]
    return x, y, c, other_chips


def _chip_no(chip):
    return 2 * chip[0] + chip[1]


def _gather_exchange(f_ref, g_ref, send_sems, recv_sems, hr):
    x, y, c, chips = _place()
    sibling = (x, y, 1 - c)

    def piece(chip, half):
        return g_ref.at[_chip_no(chip), pl.ds(half * hr, hr), :]

    def copy(k, src, dst, to):
        return pltpu.make_async_remote_copy(src_ref=src, dst_ref=dst, send_sem=send_sems.at[k],
                                            recv_sem=recv_sems.at[k], device_id=to, device_id_type=MESH_ID)

    own = copy(6, f_ref, g_ref.at[_chip_no((x, y))], sibling)
    own.start()
    my_half = f_ref.at[pl.ds(c * hr, hr), :]
    first = [copy(k, my_half, piece((x, y), c), (*chip, c)) for k, chip in enumerate(chips)]
    for cp in first:
        cp.start()
    passed = [copy(3 + k, piece(chip, c), piece(chip, c), sibling) for k, chip in enumerate(chips)]
    for k, chip in enumerate(chips):
        copy(k, my_half, piece(chip, c), (*chip, c)).wait_recv()
        passed[k].start()
    for k, chip in enumerate(chips):
        copy(3 + k, my_half, piece(chip, 1 - c), sibling).wait_recv()
    for cp in first + passed:
        cp.wait_send()
    own.wait()


GATHER_SEMS = (pltpu.SemaphoreType.DMA((7,)), pltpu.SemaphoreType.DMA((7,)))


HBM = pltpu.MemorySpace.HBM
ONE_SEM_PAIR = (pltpu.SemaphoreType.DMA, pltpu.SemaphoreType.DMA)


def _sibling_of_me():
    x, y, c, _ = _place()
    return [(x, y, 1 - c)]


def _same_core_of_other_chips():
    x, y, c, chips = _place()
    return [(*chip, c) for chip in chips]


def _on_sequencer(name, collective_id, sems, peers, exchange):
    @pl.kernel(mesh=plsc.ScalarSubcoreMesh(axis_name="sequencer", num_cores=1), name=name, scratch_types=sems,
               compiler_params=pltpu.CompilerParams(collective_id=collective_id))
    def launch(*sem_refs):
        barrier = pltpu.get_barrier_semaphore()
        devices = peers()
        for peer in devices:
            pl.semaphore_signal(barrier, inc=1, device_id=peer, device_id_type=MESH_ID)
        pl.semaphore_wait(barrier, len(devices))
        exchange(*sem_refs)

    launch()


def _all_gather_rows(flat, behind=None):
    r, cols = flat.shape
    out = jax.ShapeDtypeStruct((N_CHIPS, r, cols), flat.dtype)
    if behind is not None:
        f_ref, g_ref = jax.new_ref(flat, memory_space=HBM), jax.empty_ref(out, memory_space=HBM)
        _on_sequencer(f"all_gather_weights_behind_{behind[1]}", behind[0], GATHER_SEMS,
                      lambda: _sibling_of_me() + _same_core_of_other_chips(),
                      lambda s, r_: _gather_exchange(f_ref, g_ref, s, r_, r // 2))
        return g_ref[...]

    def body(f_ref, g_ref, send_sems, recv_sems):
        _gather_exchange(f_ref, g_ref, send_sems, recv_sems, r // 2)

    return pl.pallas_call(body, out_shape=out, in_specs=[ANY], out_specs=ANY, scratch_shapes=list(GATHER_SEMS),
                          name="all_gather_weights")(flat)


def _swap_halves(gall, behind=None):
    _, r, cols = gall.shape
    hr = r // 2
    out = jax.ShapeDtypeStruct((N_CHIPS, hr, cols), gall.dtype)

    def exchange(g_ref, a_ref, send_sem, recv_sem):
        x, y, c, _ = _place()
        cp = pltpu.make_async_remote_copy(src_ref=g_ref.at[:, pl.ds((1 - c) * hr, hr), :], dst_ref=a_ref,
                                          send_sem=send_sem, recv_sem=recv_sem, device_id=(x, y, 1 - c),
                                          device_id_type=MESH_ID)
        cp.start()
        cp.wait()

    if behind is not None:
        g_ref, a_ref = jax.new_ref(gall, memory_space=HBM), jax.empty_ref(out, memory_space=HBM)
        _on_sequencer(f"grad_swap_halves_behind_{behind[1]}", behind[0], ONE_SEM_PAIR, _sibling_of_me,
                      lambda s, r_: exchange(g_ref, a_ref, s, r_))
        return a_ref[...]

    def body(g_ref, a_ref, send_sem, recv_sem):
        exchange(g_ref, a_ref, send_sem, recv_sem)

    return pl.pallas_call(body, out_shape=out, in_specs=[ANY], out_specs=ANY, scratch_shapes=list(ONE_SEM_PAIR),
                          name="grad_swap_halves")(gall)


def _scatter_to_chips(p, behind=None):
    _, hr, cols = p.shape
    out = jax.ShapeDtypeStruct((3, hr, cols), p.dtype)
    sems = (pltpu.SemaphoreType.DMA((3,)), pltpu.SemaphoreType.DMA((3,)))

    def exchange(p_ref, b_ref, send_sems, recv_sems):
        x, y, c, chips = _place()
        cps = [pltpu.make_async_remote_copy(src_ref=p_ref.at[_chip_no(chip)], dst_ref=b_ref.at[k],
                                            send_sem=send_sems.at[k], recv_sem=recv_sems.at[k],
                                            device_id=(*chip, c), device_id_type=MESH_ID)
               for k, chip in enumerate(chips)]
        for cp in cps:
            cp.start()
        for cp in cps:
            cp.wait()

    if behind is not None:
        p_ref, b_ref = jax.new_ref(p, memory_space=HBM), jax.empty_ref(out, memory_space=HBM)
        _on_sequencer(f"grad_scatter_chips_behind_{behind[1]}", behind[0], sems, _same_core_of_other_chips,
                      lambda s, r_: exchange(p_ref, b_ref, s, r_))
        return b_ref[...]

    def body(p_ref, b_ref, send_sems, recv_sems):
        exchange(p_ref, b_ref, send_sems, recv_sems)

    return pl.pallas_call(body, out_shape=out, in_specs=[ANY], out_specs=ANY, scratch_shapes=list(sems),
                          name="grad_scatter_chips")(p)


def _share_with_sibling(full, behind=None):
    r, cols = full.shape
    hr = r // 2

    def exchange(in_ref, out_ref, send_sem, recv_sem):
        x, y, c, _ = _place()
        cp = pltpu.make_async_remote_copy(src_ref=in_ref.at[pl.ds(c * hr, hr), :],
                                          dst_ref=out_ref.at[pl.ds(c * hr, hr), :],
                                          send_sem=send_sem, recv_sem=recv_sem, device_id=(x, y, 1 - c),
                                          device_id_type=MESH_ID)
        cp.start()
        cp.wait()

    if behind is not None:
        full_ref = jax.new_ref(full, memory_space=HBM)
        _on_sequencer(f"grad_share_sibling_behind_{behind[1]}", behind[0], ONE_SEM_PAIR, _sibling_of_me,
                      lambda s, r_: exchange(full_ref, full_ref, s, r_))
        return full_ref[...]

    def body(in_ref, out_ref, send_sem, recv_sem):
        exchange(in_ref, out_ref, send_sem, recv_sem)

    return pl.pallas_call(
        body, out_shape=jax.ShapeDtypeStruct((r, cols), full.dtype), in_specs=[ANY], out_specs=ANY,
        input_output_aliases={0: 0}, scratch_shapes=list(ONE_SEM_PAIR), name="grad_share_sibling",
    )(full)


def _all_reduce_small(v):
    r, cols = v.shape
    n_dev = 8

    def body(x_ref, out_ref, gat_ref, send_sems, recv_sems, local_sem):
        x, y, c, chips = _place()
        me, sibling = (x, y, c), (x, y, 1 - c)

        def slot(px, py, pc):
            return gat_ref.at[4 * px + 2 * py + pc]

        def copy(k, block, to, src=None):
            return pltpu.make_async_remote_copy(src_ref=slot(*block) if src is None else src, dst_ref=slot(*block),
                                                send_sem=send_sems.at[k], recv_sem=recv_sems.at[k], device_id=to,
                                                device_id_type=MESH_ID)

        mine = pltpu.make_async_copy(x_ref, slot(*me), local_sem)
        mine.start()
        first = [copy(0, me, sibling, src=x_ref)]
        first += [copy(1 + k, me, (*chip, c), src=x_ref) for k, chip in enumerate(chips)]
        for cp in first:
            cp.start()
        passed = [copy(4 + k, (*chip, c), sibling) for k, chip in enumerate(chips)]
        for k, chip in enumerate(chips):
            copy(1 + k, (*chip, c), me).wait_recv()
            passed[k].start()
        copy(0, sibling, me).wait_recv()
        for k, chip in enumerate(chips):
            copy(4 + k, (*chip, 1 - c), me).wait_recv()
        for cp in first + passed:
            cp.wait_send()
        mine.wait()
        acc = gat_ref[0]
        for d in range(1, n_dev):
            acc = acc + gat_ref[d]
        out_ref[...] = acc

    vmem = pl.BlockSpec(memory_space=pltpu.VMEM)
    return pl.pallas_call(
        body, out_shape=jax.ShapeDtypeStruct((r, cols), v.dtype), in_specs=[vmem], out_specs=vmem,
        scratch_shapes=[pltpu.VMEM((n_dev, r, cols), v.dtype), pltpu.SemaphoreType.DMA((7,)),
                        pltpu.SemaphoreType.DMA((7,)), pltpu.SemaphoreType.DMA],
        name="all_reduce_small",
    )(v)


WIRE_DTYPE = jnp.bfloat16


def _add_own_half(gall, got, place):
    _, r, cols = gall.shape
    hr = r // 2
    tr = _row_tile(hr)
    g4 = gall.reshape(N_CHIPS, 2, hr, cols)

    def body(s_ref, g_ref, a_ref, o_ref):
        o_ref[...] = (g_ref[0] + a_ref[...]).astype(o_ref.dtype)

    return pl.pallas_call(
        body, out_shape=jax.ShapeDtypeStruct((N_CHIPS, hr, cols), WIRE_DTYPE),
        grid_spec=pltpu.PrefetchScalarGridSpec(
            num_scalar_prefetch=1, grid=(N_CHIPS, hr // tr),
            in_specs=[pl.BlockSpec((1, 1, tr, cols), lambda j, i, s: (j, s[1], i, 0)),
                      pl.BlockSpec((1, tr, cols), lambda j, i, s: (j, i, 0))],
            out_specs=pl.BlockSpec((1, tr, cols), lambda j, i, s: (j, i, 0))),
        compiler_params=_params(2), name="grad_add_halves",
    )(place, g4, got)


def _add_chip_parts(gall, got, parts, place):
    _, r, cols = gall.shape
    hr = r // 2
    tr = _row_tile(hr)
    g4 = gall.reshape(N_CHIPS, 2, hr, cols)
    nt = hr // tr

    def body(s_ref, g_ref, a_ref, b0_ref, b1_ref, b2_ref, o_ref):
        own = g_ref[0, 0] + a_ref[0]
        o_ref[...] = ((own + b0_ref[0].astype(F32)) + b1_ref[0].astype(F32)) + b2_ref[0].astype(F32)

    part = lambda k: pl.BlockSpec((1, tr, cols), lambda i, s: (k, i, 0))
    return pl.pallas_call(
        body, out_shape=jax.ShapeDtypeStruct((r, cols), F32),
        grid_spec=pltpu.PrefetchScalarGridSpec(
            num_scalar_prefetch=1, grid=(nt,),
            in_specs=[pl.BlockSpec((1, 1, tr, cols), lambda i, s: (s[0], s[1], i, 0)),
                      pl.BlockSpec((1, tr, cols), lambda i, s: (s[0], i, 0)), part(0), part(1), part(2)],
            out_specs=pl.BlockSpec((tr, cols), lambda i, s: (s[1] * nt + i, 0))),
        compiler_params=_params(1), name="grad_add_chips",
    )(place, g4, got, parts, parts, parts)


def _pack_shards(weights, l, group, dtype):
    parts = []
    for name, _ in group:
        w = weights[name][l]
        parts.append((w.T if name in COLUMN_SHARDED else w).reshape(-1, D_MODEL).astype(dtype))
    return parts[0] if len(parts) == 1 else jnp.concatenate(parts, axis=0)


def _unpack_gathered(gathered, group):
    w, off = {}, 0
    for name, rows in group:
        blk = gathered[:, off:off + rows]
        if name == "w_attn_proj":
            w[name] = blk.reshape(N_CHIPS * 256, ATTN_OUT)
        else:
            w[name] = blk.reshape(N_CHIPS * rows, D_MODEL)
        off += rows
    return w


def _pack_full_grads(g, group):
    parts = [g[name].reshape(N_CHIPS, rows, D_MODEL) for name, rows in group]
    return parts[0] if len(parts) == 1 else jnp.concatenate(parts, axis=1)


def _unpack_shard_grads(fulls):
    out = {}
    for gi, group in enumerate(PACK_GROUPS):
        off = 0
        for name, rows in group:
            blks = [fulls[(l, gi)][off:off + rows] for l in range(DEPTH)]
            out[name] = jnp.stack([b.reshape(256, ATTN_OUT) if name == "w_attn_proj" else b for b in blks])
            off += rows
    return out


class _StaticWeights:
    def __init__(self, layers):
        self.layers = layers

    def group(self, l, gi):
        return self.layers[l]

    def layer(self, l):
        return self.layers[l]


class _WeightGathers:
    def __init__(self, weights):
        self.flats = {(l, gi): _pack_shards(weights, l, g, MXU_DTYPE)
                      for l in range(DEPTH) for gi, g in enumerate(PACK_GROUPS)}
        self.order = sorted(self.flats)
        self.landed, self.unpacked, self.issued = {}, {}, 0
        self._issue()

    def _issue(self, after=None):
        key = self.order[self.issued]
        flat = self.flats[key]
        if self.issued == 0:
            self.landed[key] = _all_gather_rows(flat)
        else:
            prev = self.order[self.issued - 1]
            ties = (self.landed[prev], flat) if after is None else (self.landed[prev], flat, after)
            tied = lax.optimization_barrier(ties)
            self.landed[prev], flat = tied[0], tied[1]
            after = tied[2] if after is not None else None
            self.landed[key] = _all_gather_rows(flat, behind=(1, f"l{key[0]}g{key[1]}"))
        self.issued += 1
        return after

    def prefetch(self, after):
        return self._issue(after) if self.issued < len(self.order) else after

    def group(self, l, gi):
        key = (l, gi)
        if key not in self.unpacked:
            self.unpacked[key] = _unpack_gathered(self.landed[key], PACK_GROUPS[gi])
        return self.unpacked[key]

    def layer(self, l):
        w = {}
        for gi in range(len(PACK_GROUPS)):
            w.update(self.group(l, gi))
        return w


class _GradReduction:
    LAST = 3

    def __init__(self, gw, group, place, behind):
        self.gall, self.place, self.behind, self.stage = _pack_full_grads(gw, group), place, behind, 0
        self.pending = None

    def _how(self, k):
        return None if self.behind is None else (self.behind[0][k], self.behind[1])

    def advance(self, value=None):
        if self.stage > self.LAST:
            return value
        if self.stage > 0 and value is not None:
            value, self.pending = lax.optimization_barrier((value, self.pending))
        if self.stage == 0:
            self.pending = _swap_halves(self.gall, self._how(0))
        elif self.stage == 1:
            self.got = self.pending
            self.pending = _scatter_to_chips(_add_own_half(self.gall, self.got, self.place), self._how(1))
        elif self.stage == 2:
            self.pending = _share_with_sibling(
                _add_chip_parts(self.gall, self.got, self.pending, self.place), self._how(2))
        self.stage += 1
        return value

    def result(self):
        while self.stage < self.LAST:
            self.advance()
        return self.pending


class _GradReductions:
    def __init__(self, place):
        self.place, self.all = place, {}

    def start(self, l, gi, gw):
        hidden = (l, gi) != (0, IN_GROUP)
        slot = len(self.all) % 2
        how = ((2 + 3 * slot, 3 + 3 * slot, 4 + 3 * slot), f"l{l}g{gi}") if hidden else None
        red = _GradReduction(gw, PACK_GROUPS[gi], self.place, how)
        if hidden:
            red.advance()
        self.all[(l, gi)] = red

    def tick(self, value):
        for red in self.all.values():
            if red.behind is not None:
                value = red.advance(value)
        return value

    def results(self):
        return {key: red.result() for key, red in self.all.items()}


def _attn_views(za, s):
    views = []
    for g, (_, dil) in enumerate(ATTN_GROUPS):
        if dil == 1:
            views.append((za, za, za, 3 * N_GROUPS, (g, N_GROUPS + g, 2 * N_GROUPS + g)))
        else:
            rows = s // dil
            part = lambda k: za[:, k * ATTN_WIDTH + g * ATTN_OUT:k * ATTN_WIDTH + (g + 1) * ATTN_OUT].reshape(
                rows, dil * ATTN_OUT)
            views.append((part(0), part(1), part(2), 1, (0, 0, 0)))
    return views


def _layer_fwd(x, x_b, weights, l, lnp, bias, rc, prefetch):
    s = x.shape[0]
    tag = f"l{l}"
    w_in = weights.group(l, IN_GROUP)["w_in"]
    z = {}
    for name, off, width, narrow in SECTIONS:
        z[name] = _matmul(x_b, w_in[off:off + width], "nt", bias=lnp["b_in"][:, off:off + width],
                          out_dtype=MXU_DTYPE if narrow else F32, name=f"in_proj_{name}_{tag}")
        if name == "a":
            z[name] = prefetch(z[name])
    w = weights.group(l, REST_GROUP)
    views = _attn_views(z["a"], s)
    o_list, l_list = [], []
    for g, (_, dil) in enumerate(ATTN_GROUPS):
        qa, ka, va, cb, offs = views[g]
        o, lse = _attn_fwd(qa, ka, va, cb, offs, bias, g, dil)
        o_list.append(o.reshape(s, ATTN_OUT))
        l_list.append(lse.reshape(s, ATTN_OUT))
    ya, ya_b, lt = _attn_combine(o_list, l_list)
    yb_b, o_ret, states = _ret_fwd(z["bq"], z["bk"], z["c"], z["d"], rc)
    x1, u1, a1, a2, mg_b, x1_b = _mix_fwd(x, ya_b, yb_b, z["e"], w["w_attn_proj"], w["w_ret_proj"], w["w_out"],
                                          lnp["ln1_g"], lnp["ln1_b"])
    x1 = prefetch(x1)
    x2, u2, fa, fb, h_b, x2_b = _ffn_fwd(x1, w["w_ffn_gate"], w["w_ffn_up"], w["w_ffn_down"], lnp["ln2_g"],
                                         lnp["ln2_b"])
    saved = dict(x_b=x_b, z=z, views=views, ya=ya, ya_b=ya_b, lt=lt, yb_b=yb_b, o_ret=o_ret, states=states,
                 x1_b=x1_b, u1=u1, a1=a1, a2=a2, mg_b=mg_b, u2=u2, fa=fa, fb=fb, h_b=h_b)
    return x2, x2_b, saved


def _layer_bwd(dx2, w, l, lnp, sv, bias, rc, reductions=None):
    s = dx2.shape[0]
    tag = f"l{l}"
    z = sv["z"]
    step = reductions.tick if reductions is not None else (lambda v: v)
    dx1, du2_b, da_b, db_b, dg2, dbt2 = _ffn_bwd(dx2, sv["u2"], sv["fa"], sv["fb"], w["w_ffn_gate"], w["w_ffn_up"],
                                                  w["w_ffn_down"], lnp["ln2_g"])
    dx1 = step(dx1)
    gw = {}
    gw["w_ffn_down"] = _matmul(sv["h_b"], du2_b, "tn", name=f"dw_ffn_down_{tag}")
    gw["w_ffn_gate"] = _matmul(da_b, sv["x1_b"], "tn", name=f"dw_ffn_gate_{tag}")
    gw["w_ffn_up"] = _matmul(db_b, sv["x1_b"], "tn", name=f"dw_ffn_up_{tag}")
    dres, du1_b, da1_b, da2_b, dze, dya, dyb, dg1, dbt1 = _mix_bwd(
        dx1, sv["u1"], sv["a1"], sv["a2"], z["e"], w["w_attn_proj"], w["w_ret_proj"], w["w_out"], lnp["ln1_g"])
    gw["w_out"] = _matmul(sv["mg_b"], du1_b, "tn", name=f"dw_out_{tag}")
    gw["w_attn_proj"] = _matmul(da1_b, sv["ya_b"], "tn", name=f"dw_attn_proj_{tag}")
    gw["w_ret_proj"] = _matmul(sv["yb_b"], da2_b, "tn", name=f"dw_ret_proj_{tag}")
    dyb = step(dyb)
    if reductions is not None:
        reductions.start(l, REST_GROUP, gw)
    dzq, dzk, dzv, dzg = _ret_bwd(z["bq"], z["bk"], z["c"], z["d"], sv["o_ret"], sv["states"], dyb, rc)
    dzq = step(dzq)
    dq_l, dk_l, dv_l, dbias_l = [], [], [], []
    for g, (_, dil) in enumerate(ATTN_GROUPS):
        qa, ka, va, cb, offs = sv["views"][g]
        rows = s // dil
        view = lambda t: t.reshape(rows, dil * ATTN_OUT)
        dq, dk, dv, dbg = _attn_bwd(qa, ka, va, cb, offs, bias, view(dya), view(sv["ya"]), view(sv["lt"]), g, dil)
        dq_l.append(dq.reshape(s, ATTN_OUT))
        dk_l.append(dk.reshape(s, ATTN_OUT))
        dv_l.append(dv.reshape(s, ATTN_OUT))
        dbias_l.append(dbg)
    dz = {"a": step(jnp.concatenate(dq_l + dk_l + dv_l, axis=1)), "bq": dzq, "bk": dzk, "c": dzv, "d": dzg,
          "e": dze}
    dx = dres
    for name, off, width, _ in SECTIONS:
        dx = _matmul(dz[name], w["w_in"][off:off + width], "nn", addend=dx, name=f"dx_in_proj_{name}_{tag}")
    dx = step(dx)
    dw_rows, db_cols = [], []
    for name, off, width, _ in SECTIONS:
        dw, cs = _matmul(dz[name], sv["x_b"], "tn", colsum=True, name=f"dw_in_proj_{name}_{tag}")
        dw_rows.append(dw)
        db_cols.append(cs)
    gw["w_in"] = jnp.concatenate(dw_rows, axis=0)
    if reductions is not None:
        reductions.start(l, IN_GROUP, gw)
    small = dict(b_in=jnp.concatenate(db_cols, axis=1), ln1_g=dg1, ln1_b=dbt1, ln2_g=dg2, ln2_b=dbt2)
    return dx, gw, small, jnp.concatenate(dbias_l, axis=0)


def _forward_backward(x, target, rel_bias, weights, lnps, reductions=None):
    s = x.shape[0]
    bmaps = _bucket_maps()
    bias = _bias_tiles(rel_bias, bmaps)
    rc = _ret_consts(s)
    prefetch = getattr(weights, "prefetch", lambda v: v)
    saved = []
    h, h_b = x, x.astype(MXU_DTYPE)
    for l in range(DEPTH):
        h, h_b, sv = _layer_fwd(h, h_b, weights, l, lnps[l], bias, rc, prefetch)
        saved.append(sv)
    dh, loss_part = _loss_head(h, target)
    gws, smalls, dbiases = [None] * DEPTH, [None] * DEPTH, [None] * DEPTH
    for l in reversed(range(DEPTH)):
        dh, gws[l], smalls[l], dbiases[l] = _layer_bwd(dh, weights.layer(l), l, lnps[l], saved[l], bias, rc,
                                                       reductions)
    d_rel_bias = _bias_tiles_bwd(dbiases, bmaps)
    return loss_part, dh, gws, smalls, d_rel_bias


SMALL_NAMES = ("rel_bias", "b_in", "ln1_g", "ln1_b", "ln2_g", "ln2_b")
SMALL_ROWS = 32


def _pack_small(vals):
    flat = jnp.concatenate([vals[n].reshape(-1) for n in SMALL_NAMES])
    return jnp.pad(flat, (0, SMALL_ROWS * D_MODEL - flat.shape[0])).reshape(SMALL_ROWS, D_MODEL)


def _unpack_small(packed, like):
    flat = packed.reshape(-1)
    out, off = {}, 0
    for n in SMALL_NAMES:
        size = like[n].size
        out[n] = flat[off:off + size].reshape(like[n].shape)
        off += size
    return out


def kernel(x, rel_bias, w_in, b_in, w_attn_proj, w_ret_proj, w_out, ln1_g, ln1_b, w_ffn_gate, w_ffn_up, w_ffn_down, ln2_g, ln2_b, loss_target, m_rel_bias, m_w_in, m_b_in, m_w_attn_proj, m_w_ret_proj, m_w_out, m_ln1_g, m_ln1_b, m_w_ffn_gate, m_w_ffn_up, m_w_ffn_down, m_ln2_g, m_ln2_b, v_rel_bias, v_w_in, v_b_in, v_w_attn_proj, v_w_ret_proj, v_w_out, v_ln1_g, v_ln1_b, v_w_ffn_gate, v_w_ffn_up, v_w_ffn_down, v_ln2_g, v_ln2_b):
    big = dict(w_in=(w_in, m_w_in, v_w_in), w_attn_proj=(w_attn_proj, m_w_attn_proj, v_w_attn_proj),
               w_ret_proj=(w_ret_proj, m_w_ret_proj, v_w_ret_proj), w_out=(w_out, m_w_out, v_w_out),
               w_ffn_gate=(w_ffn_gate, m_w_ffn_gate, v_w_ffn_gate), w_ffn_up=(w_ffn_up, m_w_ffn_up, v_w_ffn_up),
               w_ffn_down=(w_ffn_down, m_w_ffn_down, v_w_ffn_down))
    small_w = dict(rel_bias=rel_bias, b_in=b_in, ln1_g=ln1_g, ln1_b=ln1_b, ln2_g=ln2_g, ln2_b=ln2_b)
    small_m = dict(rel_bias=m_rel_bias, b_in=m_b_in, ln1_g=m_ln1_g, ln1_b=m_ln1_b, ln2_g=m_ln2_g, ln2_b=m_ln2_b)
    small_v = dict(rel_bias=v_rel_bias, b_in=v_b_in, ln1_g=v_ln1_g, ln1_b=v_ln1_b, ln2_g=v_ln2_g, ln2_b=v_ln2_b)

    place = jnp.stack([2 * lax.axis_index("x") + lax.axis_index("y"), lax.axis_index("c")]).astype(jnp.int32)

    weights = _WeightGathers({n: w for n, (w, _, _) in big.items()})
    reductions = _GradReductions(place)
    lnps = [dict(b_in=b_in[l][None], ln1_g=ln1_g[l][None], ln1_b=ln1_b[l][None], ln2_g=ln2_g[l][None],
                 ln2_b=ln2_b[l][None]) for l in range(DEPTH)]
    loss_part, dx, _, smalls, d_rel_bias = _forward_backward(x[0], loss_target[0], rel_bias, weights, lnps,
                                                             reductions)
    loss = lax.psum(loss_part, ("x", "y", "c"))
    grads = _unpack_shard_grads(reductions.results())

    small_g = dict(rel_bias=d_rel_bias)
    for n in SMALL_NAMES[1:]:
        small_g[n] = jnp.concatenate([smalls[l][n] for l in range(DEPTH)], axis=0)
    small_g = _unpack_small(_all_reduce_small(_pack_small(small_g)), small_w)

    delta, new_m, new_v = {}, {}, {}
    for n, (w, m, v) in big.items():
        turn = (lambda t: jnp.swapaxes(t, 1, 2)) if n in COLUMN_SHARDED else (lambda t: t)
        two_d = lambda t: t.reshape(-1, t.shape[-1])
        g = grads[n]
        d_, m_, v_ = _adamw(two_d(turn(w)), two_d(g), two_d(turn(m)), two_d(turn(v)), name=f"adamw_{n}")
        grads[n], delta[n], new_m[n], new_v[n] = (turn(t.reshape(g.shape)) for t in (g, d_, m_, v_))
    d_, m_, v_ = _adamw(_pack_small(small_w), _pack_small(small_g), _pack_small(small_m), _pack_small(small_v),
                        name="adamw_small")
    delta.update(_unpack_small(d_, small_w))
    new_m.update(_unpack_small(m_, small_w))
    new_v.update(_unpack_small(v_, small_w))
    grads.update(small_g)

    order = ("rel_bias", "w_in", "b_in", "w_attn_proj", "w_ret_proj", "w_out", "ln1_g", "ln1_b", "w_ffn_gate",
             "w_ffn_up", "w_ffn_down", "ln2_g", "ln2_b")
    return (loss, dx[None], *[grads[n] for n in order], *[delta[n] for n in order], *[new_m[n] for n in order],
            *[new_v[n] for n in order])
```

```python
import functools

import numpy as np
import jax
import jax.numpy as jnp
from jax import lax
from jax.experimental import pallas as pl
from jax.experimental.pallas import tpu as pltpu
from jax.experimental.pallas import tpu_sc as plsc

F32 = jnp.float32
MXU_DTYPE = jnp.bfloat16

DEPTH = 2
D_MODEL = 1024
HEAD_DIM = 64
ATTN_GROUPS = ((128, 1), (512, 4), (2048, 16))
N_GROUPS = len(ATTN_GROUPS)
HEADS_PER_GROUP = 6
N_ATTN_HEADS = 18
ATTN_WIDTH = 1152
ATTN_OUT = 384
NUM_BUCKETS = 32
MAX_DISTANCE = 2048
RET_HEADS = 4
RET_QK = 256
RET_V = 512
RET_CHUNK = 128
ROPE_BASE = 10000.0
D_FF = 2816
IN_COLS = 11648
ALPHA = (2 * DEPTH) ** 0.25
LN_EPS = 1e-5
GN_EPS = 1e-5
ADAM_LR, ADAM_B1, ADAM_B2, ADAM_EPS, ADAM_WD, ADAM_STEP = 0.001, 0.9, 0.999, 1e-08, 0.01, 10

BLK = 128
NEG = -1e30
N_CHIPS = 4
VMEM_LIMIT = 48 * 1024 * 1024

SECTIONS = tuple(
    (f"a{g}", tuple((k * ATTN_WIDTH + g * ATTN_OUT, ATTN_OUT) for k in range(3)), True, dil)
    for g, (_, dil) in enumerate(ATTN_GROUPS)
) + (
    ("bq", ((3456, 1024),), False, 1),
    ("bk", ((4480, 1024),), False, 1),
    ("c", ((5504, 2048),), True, 1),
    ("d", ((7552, 2048),), False, 1),
    ("e", ((9600, 2048),), False, 1),
)
GROUP_WIDTH = 3 * ATTN_OUT
LANES = 128


def _section_rows(w, runs):
    axis = 1 if w.shape[0] == 1 else 0
    parts = [lax.slice_in_dim(w, first, first + width, axis=axis) for first, width in runs]
    return parts[0] if len(parts) == 1 else jnp.concatenate(parts, axis=axis)


def _unsection(pieces, axis):
    runs = []
    for name, sec_runs, _, _ in SECTIONS:
        off = 0
        for first, width in sec_runs:
            runs.append((first, lax.slice_in_dim(pieces[name], off, off + width, axis=axis)))
            off += width
    return jnp.concatenate([p for _, p in sorted(runs, key=lambda t: t[0])], axis=axis)
PACK_GROUPS = ((("w_in", 2912),),
               (("w_attn_proj", 96), ("w_ret_proj", 512), ("w_out", 256), ("w_ffn_gate", 704), ("w_ffn_up", 704),
                ("w_ffn_down", 704)))
IN_GROUP, REST_GROUP = 0, 1
COLUMN_SHARDED = ("w_in", "w_attn_proj", "w_ffn_gate", "w_ffn_up")

NN = ((1,), (0,))
NT = ((1,), (1,))
TN = ((0,), (0,))


def _dot(a, b, dims):
    return lax.dot_general(a.astype(MXU_DTYPE), b.astype(MXU_DTYPE), (dims, ((), ())),
                           preferred_element_type=F32)


def _pick(n, prefs):
    for p in prefs:
        if n % p == 0:
            return p
    raise ValueError(f"no tile for {n} among {prefs}")


def _row_tile(n, most=512, unit=16):
    return max(t for t in range(unit, most + 1, unit) if n % t == 0)


TOKEN_TILES = (1024, 512, 256, 128)
FEATURE_TILES = (1152, 1024, 1408, 384, 256, 128)


def _params(n_axes, limit=VMEM_LIMIT):
    return pltpu.CompilerParams(dimension_semantics=("arbitrary",) * n_axes, vmem_limit_bytes=limit)


def _resident(a):
    return pl.BlockSpec(a.shape, lambda i: (0,) * a.ndim, pipeline_mode=pl.Buffered(1))


def _sigmoid(x):
    return 1.0 / (1.0 + jnp.exp(-x))


def _norm_rows(u, eps):
    mu = jnp.mean(u, axis=-1, keepdims=True)
    xc = u - mu
    var = jnp.mean(xc * xc, axis=-1, keepdims=True)
    rstd = lax.rsqrt(var + eps)
    return xc * rstd, rstd


def _norm_rows_bwd(dxh, xh, rstd):
    c1 = jnp.mean(dxh, axis=-1, keepdims=True)
    c2 = jnp.mean(dxh * xh, axis=-1, keepdims=True)
    return rstd * (dxh - c1 - xh * c2)


def _phase_to_tokens(src_ref, scr, dil, width):
    n = scr.shape[1] // dil
    for r in range(dil):
        for cb in range(width // LANES):
            col = r * width + cb * LANES
            scr.at[cb][pl.ds(r, n, stride=dil), :] = src_ref[:, col:col + LANES].astype(F32)


def _tokens_to_phase(scr, dst_ref, dil, width):
    n = scr.shape[1] // dil
    for r in range(dil):
        for cb in range(width // LANES):
            col = r * width + cb * LANES
            dst_ref[:, col:col + LANES] = scr.at[cb][pl.ds(r, n, stride=dil), :].astype(dst_ref.dtype)


def _blocks_to_cols(scr):
    return jnp.concatenate([scr[cb] for cb in range(scr.shape[0])], axis=1)


def _cols_to_blocks(val, scr):
    for cb in range(scr.shape[0]):
        scr[cb] = val[:, cb * LANES:(cb + 1) * LANES]


def _matmul(a, b, mode, *, name, out_dtype=F32, bias=None, addend=None, colsum=False, a_phase=1, out_phase=1):
    d_a, d_o = a_phase, out_phase
    if mode == "tn":
        kd, m = a.shape[0] * d_a, a.shape[1] // d_a
        n = b.shape[1]
        tm, tn_, tk = (m if d_a > 1 else _pick(m, FEATURE_TILES)), _pick(n, FEATURE_TILES), _pick(kd, TOKEN_TILES)
        a_spec = (pl.BlockSpec((tk // d_a, d_a * m), lambda i, j, k: (k, 0)) if d_a > 1 else
                  pl.BlockSpec((tk, tm), lambda i, j, k: (k, i)))
        b_spec = pl.BlockSpec((tk, tn_), lambda i, j, k: (k, j))
        feat, tok = m, tk
    else:
        m, kd = a.shape[0] * d_a, a.shape[1] // d_a
        n = b.shape[0] if mode == "nt" else b.shape[1]
        tm, tn_ = _pick(m, TOKEN_TILES), (n if d_o > 1 else _pick(n, FEATURE_TILES))
        tk = kd if d_a > 1 else _pick(kd, FEATURE_TILES)
        a_spec = (pl.BlockSpec((tm // d_a, d_a * kd), lambda i, j, k: (i, 0)) if d_a > 1 else
                  pl.BlockSpec((tm, tk), lambda i, j, k: (i, k)))
        if mode == "nt":
            b_spec = pl.BlockSpec((tn_, tk), lambda i, j, k: (j, k))
        else:
            b_spec = pl.BlockSpec((tk, tn_), lambda i, j, k: (k, j))
        feat, tok = kd, tm
    dims = {"nn": NN, "nt": NT, "tn": TN}[mode]
    nk = kd // tk
    has_bias, has_add = bias is not None, addend is not None
    assert not colsum or mode == "tn"
    assert d_o == 1 or (mode == "nt" and not has_add)

    def body(*refs):
        it = iter(refs)
        a_ref, b_ref = next(it), next(it)
        bias_ref = next(it) if has_bias else None
        add_ref = next(it) if has_add else None
        o_ref = next(it)
        cs_ref = next(it) if colsum else None
        acc_ref = next(it)
        a_scr = next(it) if d_a > 1 else None
        o_scr = next(it) if d_o > 1 else None
        j, k = pl.program_id(1), pl.program_id(2)

        @pl.when(k == 0)
        def _():
            if has_add:
                acc_ref[...] = add_ref[...].astype(F32)
            else:
                acc_ref[...] = jnp.zeros_like(acc_ref)

        if d_a > 1:
            _phase_to_tokens(a_ref, a_scr, d_a, feat)
            av = _blocks_to_cols(a_scr)
        else:
            av = a_ref[...]
        acc_ref[...] += _dot(av, b_ref[...], dims)
        if colsum:
            @pl.when(jnp.logical_and(j == 0, k == 0))
            def _():
                cs_ref[...] = jnp.zeros_like(cs_ref)

            @pl.when(j == 0)
            def _():
                cs_ref[...] += jnp.sum(av.astype(F32), axis=0, keepdims=True)

        @pl.when(k == nk - 1)
        def _():
            r = acc_ref[...]
            if has_bias:
                r = r + bias_ref[...]
            if d_o > 1:
                _cols_to_blocks(r, o_scr)
                _tokens_to_phase(o_scr, o_ref, d_o, n)
            else:
                o_ref[...] = r.astype(out_dtype)

    in_specs, args = [a_spec, b_spec], [a, b]
    if has_bias:
        in_specs.append(pl.BlockSpec((1, tn_), lambda i, j, k: (0, j)))
        args.append(bias)
    if has_add:
        in_specs.append(pl.BlockSpec((tm, tn_), lambda i, j, k: (i, j)))
        args.append(addend)
    if d_o > 1:
        out_shape = [jax.ShapeDtypeStruct((m // d_o, d_o * n), out_dtype)]
        out_specs = [pl.BlockSpec((tm // d_o, d_o * n), lambda i, j, k: (i, 0))]
    else:
        out_shape = [jax.ShapeDtypeStruct((m, n), out_dtype)]
        out_specs = [pl.BlockSpec((tm, tn_), lambda i, j, k: (i, j))]
    if colsum:
        out_shape.append(jax.ShapeDtypeStruct((1, m), F32))
        out_specs.append(pl.BlockSpec((1, tm), lambda i, j, k: (0, i)))
    scratch = [pltpu.VMEM((tm, tn_), F32)]
    if d_a > 1:
        scratch.append(pltpu.VMEM((feat // LANES, tok, LANES), F32))
    if d_o > 1:
        scratch.append(pltpu.VMEM((n // LANES, tm, LANES), F32))
    res = pl.pallas_call(
        body, out_shape=out_shape, grid=(m // tm, n // tn_, nk), in_specs=in_specs, out_specs=out_specs,
        scratch_shapes=scratch, compiler_params=_params(3), name=name,
    )(*args)
    return res if colsum else res[0]


def _t5_bucket(dist):
    max_exact = NUM_BUCKETS // 2
    large = max_exact + (np.log(np.maximum(dist, max_exact) / max_exact)
                         / np.log(MAX_DISTANCE / max_exact) * (NUM_BUCKETS - max_exact)).astype(np.int32)
    large = np.minimum(large, NUM_BUCKETS - 1)
    return np.where(dist < max_exact, dist, large).astype(np.int32)


def _bucket_maps():
    qi = np.arange(BLK)[:, None]
    kj = np.arange(2 * BLK)[None, :]
    rel = np.clip(qi + BLK - kj, 0, BLK)
    return jnp.asarray(np.stack([_t5_bucket(rel * d) for _, d in ATTN_GROUPS]))


def _bias_tiles(rel_bias, bmaps):
    def body(tab_ref, bm_ref, o_ref):
        h = pl.program_id(0)
        bm = bm_ref[0]
        acc = jnp.zeros((BLK, 2 * BLK), F32)
        for b in range(NUM_BUCKETS):
            acc = jnp.where(bm == b, tab_ref[b, h], acc)
        o_ref[0] = acc

    return pl.pallas_call(
        body, out_shape=jax.ShapeDtypeStruct((N_ATTN_HEADS, BLK, 2 * BLK), F32), grid=(N_ATTN_HEADS,),
        in_specs=[pl.BlockSpec(memory_space=pltpu.SMEM),
                  pl.BlockSpec((1, BLK, 2 * BLK), lambda h: (h // HEADS_PER_GROUP, 0, 0))],
        out_specs=pl.BlockSpec((1, BLK, 2 * BLK), lambda h: (h, 0, 0)),
        compiler_params=_params(1), name="bias_tiles",
    )(rel_bias, bmaps)


def _bias_tiles_bwd(dbias_layers, bmaps):
    nl = len(dbias_layers)

    def body(*refs):
        bm = refs[nl][0]
        o_ref = refs[nl + 1]
        x = refs[0][0]
        for r in refs[1:nl]:
            x = x + r[0]
        lane = lax.broadcasted_iota(jnp.int32, (1, BLK), 1)
        row = jnp.zeros((1, BLK), F32)
        for b in range(NUM_BUCKETS):
            s = jnp.sum(jnp.where(bm == b, x, 0.0), axis=1, keepdims=True)
            s = jnp.sum(s, axis=0, keepdims=True)
            row = jnp.where(lane == b, s, row)
        o_ref[0] = row

    tile = pl.BlockSpec((1, BLK, 2 * BLK), lambda h: (h, 0, 0))
    out = pl.pallas_call(
        body, out_shape=jax.ShapeDtypeStruct((N_ATTN_HEADS, 1, BLK), F32), grid=(N_ATTN_HEADS,),
        in_specs=[tile] * nl + [pl.BlockSpec((1, BLK, 2 * BLK), lambda h: (h // HEADS_PER_GROUP, 0, 0))],
        out_specs=pl.BlockSpec((1, 1, BLK), lambda h: (h, 0, 0)),
        compiler_params=_params(1), name="bias_tiles_bwd",
    )(*dbias_layers, bmaps)
    return out[:, 0, :NUM_BUCKETS].T


def _pair_masks():
    lane = lax.broadcasted_iota(jnp.int32, (BLK, BLK), 1)
    row2 = lax.broadcasted_iota(jnp.int32, (2 * BLK, BLK), 0)
    lane2 = lax.broadcasted_iota(jnp.int32, (2 * BLK, BLK), 1)
    own = (lane2 // HEAD_DIM) == (row2 // BLK)
    qi = lax.broadcasted_iota(jnp.int32, (2 * BLK, 2 * BLK), 0) & (BLK - 1)
    kj = lax.broadcasted_iota(jnp.int32, (2 * BLK, 2 * BLK), 1)
    band = jnp.logical_and(kj >= qi, kj <= qi + BLK)
    return lane < HEAD_DIM, own, band, kj < BLK


def _pair_scores(q32, kb, bias2, own, band, is_prev, pen):
    qm = jnp.where(own, jnp.concatenate([q32, q32], axis=0), 0.0)
    s = _dot(qm, kb, NT) * (HEAD_DIM ** -0.5) + bias2
    if pen is not None:
        s = s + jnp.where(is_prev, pen, 0.0)
    return jnp.where(band, s, NEG), qm


HEAD_PAIRS = HEADS_PER_GROUP // 2


def _attn_fwd(qkv, bias, g, dil):
    rows = qkv.shape[0]
    nb = rows // BLK
    rb = 2 if nb % 2 == 0 else 1
    cur = pl.BlockSpec((rb * BLK, GROUP_WIDTH), lambda r, n: (n, r))
    prev = pl.BlockSpec((BLK, GROUP_WIDTH), lambda r, n: (jnp.maximum(rb * n - 1, 0), r))

    def body(c_ref, p_ref, b_ref, o_ref, l_ref):
        n = pl.program_id(1)
        pen0 = jnp.where(n > 0, 0.0, NEG)
        first, own, band, is_prev = _pair_masks()
        o_rows, l_rows = [], []
        for t in range(rb):
            rows_t = slice(t * BLK, (t + 1) * BLK)
            o_parts, l_parts = [], []
            for hp in range(HEAD_PAIRS):
                qc, kc, vc = (slice(k * ATTN_OUT + hp * BLK, k * ATTN_OUT + (hp + 1) * BLK) for k in range(3))
                if t == 0:
                    kp, vp, pen = p_ref[:, kc], p_ref[:, vc], pen0
                else:
                    before = slice((t - 1) * BLK, t * BLK)
                    kp, vp, pen = c_ref[before, kc], c_ref[before, vc], None
                kb = jnp.concatenate([kp, c_ref[rows_t, kc]], axis=0)
                vb = jnp.concatenate([vp, c_ref[rows_t, vc]], axis=0)
                bias2 = jnp.concatenate([b_ref[2 * hp], b_ref[2 * hp + 1]], axis=0)
                s, _ = _pair_scores(c_ref[rows_t, qc].astype(F32), kb, bias2, own, band, is_prev, pen)
                m = jnp.max(s, axis=1, keepdims=True)
                p = jnp.exp(s - m)
                l = jnp.sum(p, axis=1, keepdims=True)
                o2 = _dot(p * (1.0 / l), vb, NN)
                lse2 = m + jnp.log(l)
                o_parts.append(jnp.where(first, o2[:BLK], o2[BLK:]))
                l_parts.append(jnp.where(first, lse2[:BLK], lse2[BLK:]))
            o_rows.append(jnp.concatenate(o_parts, axis=1))
            l_rows.append(jnp.concatenate(l_parts, axis=1))
        o_ref[...] = jnp.concatenate(o_rows, axis=0)
        l_ref[...] = jnp.concatenate(l_rows, axis=0)

    out_spec = pl.BlockSpec((rb * BLK, ATTN_OUT), lambda r, n: (n, r))
    shape = jax.ShapeDtypeStruct((rows, dil * ATTN_OUT), F32)
    return pl.pallas_call(
        body, out_shape=[shape, shape], grid=(dil, nb // rb),
        in_specs=[cur, prev, pl.BlockSpec((HEADS_PER_GROUP, BLK, 2 * BLK), lambda r, n: (g, 0, 0))],
        out_specs=[out_spec, out_spec], compiler_params=_params(2), name=f"attn_fwd_g{g}",
    )(qkv, qkv, bias)


def _attn_bwd(qkv, bias, dy, ya, lt, g, dil):
    rows = qkv.shape[0]
    nb = rows // BLK
    cur = lambda w: pl.BlockSpec((BLK, w), lambda r, n: (jnp.minimum(n, nb - 1), r))
    prev = pl.BlockSpec((BLK, GROUP_WIDTH), lambda r, n: (jnp.clip(n - 1, 0, nb - 1), r))

    def body(c_ref, p_ref, b_ref, dy_ref, ya_ref, lt_ref, dz_ref, db_ref, cq_ref, ck_ref, cv_ref):
        r, n = pl.program_id(0), pl.program_id(1)

        @pl.when(jnp.logical_and(r == 0, n == 0))
        def _():
            db_ref[...] = jnp.zeros_like(db_ref)

        @pl.when(n == 0)
        def _():
            cq_ref[...] = jnp.zeros_like(cq_ref)
            ck_ref[...] = jnp.zeros_like(ck_ref)
            cv_ref[...] = jnp.zeros_like(cv_ref)

        @pl.when(n < nb)
        def _():
            pen = jnp.where(n > 0, 0.0, NEG)
            first, own, band, is_prev = _pair_masks()
            second = jnp.logical_not(first)
            scale = HEAD_DIM ** -0.5
            parts = {k: [] for k in ("dq", "dkp", "dkc", "dvp", "dvc")}
            db_parts = []
            for hp in range(HEAD_PAIRS):
                cols = slice(hp * BLK, (hp + 1) * BLK)
                qc, kc, vc = (slice(k * ATTN_OUT + hp * BLK, k * ATTN_OUT + (hp + 1) * BLK) for k in range(3))
                kb = jnp.concatenate([p_ref[:, kc], c_ref[:, kc]], axis=0)
                vb = jnp.concatenate([p_ref[:, vc], c_ref[:, vc]], axis=0)
                bias2 = jnp.concatenate([b_ref[2 * hp], b_ref[2 * hp + 1]], axis=0)
                dy_, lt_ = dy_ref[:, cols], lt_ref[:, cols]
                dyy = dy_ * ya_ref[:, cols]
                per_head = lambda t, red, fill: jnp.concatenate(
                    [red(jnp.where(first, t, fill), axis=1, keepdims=True),
                     red(jnp.where(second, t, fill), axis=1, keepdims=True)], axis=0)
                lse2 = per_head(lt_, jnp.max, NEG)
                delta2 = per_head(dyy, jnp.sum, 0.0)
                s, qm = _pair_scores(c_ref[:, qc].astype(F32), kb, bias2, own, band, is_prev, pen)
                p = jnp.exp(s - lse2)
                dym = jnp.where(own, jnp.concatenate([dy_, dy_], axis=0), 0.0)
                ds = p * (_dot(dym, vb, NT) - delta2)
                db_parts += [ds[:BLK], ds[BLK:]]
                dq2 = _dot(ds, kb, NN) * scale
                dkb = _dot(ds, qm, TN) * scale
                dvb = _dot(p, dym, TN)
                for k, val in (("dq", jnp.where(first, dq2[:BLK], dq2[BLK:])), ("dkp", dkb[:BLK]),
                               ("dkc", dkb[BLK:]), ("dvp", dvb[:BLK]), ("dvc", dvb[BLK:])):
                    parts[k].append(val)
            wide = {k: jnp.concatenate(val, axis=1) for k, val in parts.items()}
            db_ref[...] += jnp.stack(db_parts, axis=0)
            dz_ref[...] = jnp.concatenate([cq_ref[...], ck_ref[...] + wide["dkp"], cv_ref[...] + wide["dvp"]],
                                          axis=1).astype(dz_ref.dtype)
            cq_ref[...] = wide["dq"]
            ck_ref[...] = wide["dkc"]
            cv_ref[...] = wide["dvc"]

        @pl.when(n == nb)
        def _():
            dz_ref[...] = jnp.concatenate([cq_ref[...], ck_ref[...], cv_ref[...]], axis=1).astype(dz_ref.dtype)

    late = pl.BlockSpec((BLK, GROUP_WIDTH), lambda r, n: (jnp.maximum(n - 1, 0), r))
    carry = pltpu.VMEM((BLK, ATTN_OUT), F32)
    return pl.pallas_call(
        body,
        out_shape=[jax.ShapeDtypeStruct((rows, dil * GROUP_WIDTH), MXU_DTYPE),
                   jax.ShapeDtypeStruct((HEADS_PER_GROUP, BLK, 2 * BLK), F32)],
        grid=(dil, nb + 1),
        in_specs=[cur(GROUP_WIDTH), prev, pl.BlockSpec((HEADS_PER_GROUP, BLK, 2 * BLK), lambda r, n: (g, 0, 0)),
                  cur(ATTN_OUT), cur(ATTN_OUT), cur(ATTN_OUT)],
        out_specs=[late, pl.BlockSpec((HEADS_PER_GROUP, BLK, 2 * BLK), lambda r, n: (0, 0, 0))],
        scratch_shapes=[carry, carry, carry], compiler_params=_params(2), name=f"attn_bwd_g{g}",
    )(qkv, qkv, bias, dy, ya, lt)


def _attn_combine(o_list, l_list):
    dils = [d for _, d in ATTN_GROUPS]
    s = o_list[0].shape[0] * dils[0]
    tr = _pick(s, (512, 256, 128))
    n_g = len(dils)

    def body(*refs):
        o_refs, l_refs = refs[:n_g], refs[n_g:2 * n_g]
        yb_ref = refs[2 * n_g]
        y_refs, lt_refs = refs[2 * n_g + 1:3 * n_g + 1], refs[3 * n_g + 1:4 * n_g + 1]
        scr_o, scr_l = refs[4 * n_g + 1], refs[4 * n_g + 2]
        os_, ls_ = [], []
        for g in range(n_g):
            if dils[g] == 1:
                os_.append(o_refs[g][...])
                ls_.append(l_refs[g][...])
            else:
                _phase_to_tokens(o_refs[g], scr_o, dils[g], ATTN_OUT)
                _phase_to_tokens(l_refs[g], scr_l, dils[g], ATTN_OUT)
                os_.append(_blocks_to_cols(scr_o))
                ls_.append(_blocks_to_cols(scr_l))
        mx = functools.reduce(jnp.maximum, ls_)
        es = [jnp.exp(a - mx) for a in ls_]
        den = functools.reduce(lambda a, b: a + b, es)
        inv = 1.0 / den
        y = functools.reduce(lambda a, b: a + b, [(e * inv) * o for e, o in zip(es, os_)])
        lt = mx + jnp.log(den)
        yb_ref[...] = y.astype(yb_ref.dtype)
        _cols_to_blocks(y, scr_o)
        _cols_to_blocks(lt, scr_l)
        for g in range(n_g):
            if dils[g] == 1:
                y_refs[g][...] = y
                lt_refs[g][...] = lt
            else:
                _tokens_to_phase(scr_o, y_refs[g], dils[g], ATTN_OUT)
                _tokens_to_phase(scr_l, lt_refs[g], dils[g], ATTN_OUT)

    view = lambda d: pl.BlockSpec((tr // d, d * ATTN_OUT), lambda i: (i, 0))
    views = [view(d) for d in dils]
    f = lambda d: jax.ShapeDtypeStruct((s // d, d * ATTN_OUT), F32)
    fs = [f(d) for d in dils]
    scratch = pltpu.VMEM((ATTN_OUT // LANES, tr, LANES), F32)
    res = pl.pallas_call(
        body, out_shape=[jax.ShapeDtypeStruct((s, ATTN_OUT), MXU_DTYPE)] + fs + fs, grid=(s // tr,),
        in_specs=views + views, out_specs=[view(1)] + views + views, scratch_shapes=[scratch, scratch],
        compiler_params=_params(1), name="attn_combine",
    )(*o_list, *l_list)
    return res[0], list(res[1:1 + n_g]), list(res[1 + n_g:])


def _ret_consts(s):
    half = RET_QK // 2
    pos = jnp.arange(s, dtype=F32)
    inv_freq = ROPE_BASE ** (-jnp.arange(half, dtype=F32) / half)
    ang = pos[:, None] * inv_freq[None]
    log_g = jnp.log(1.0 - 2.0 ** (-5.0 - jnp.arange(RET_HEADS, dtype=F32)))
    n = jnp.arange(RET_CHUNK, dtype=F32)
    diff = n[:, None] - n[None, :]
    dmask = jnp.where(diff >= 0, jnp.exp(log_g[:, None, None] * jnp.maximum(diff, 0.0)), 0.0)
    qdec = jnp.exp(log_g[:, None] * (n + 1.0))
    kdec = jnp.exp(log_g[:, None] * (RET_CHUNK - 1.0 - n))
    cdec = jnp.exp(log_g * RET_CHUNK)
    wide = (RET_HEADS, RET_CHUNK, RET_QK)
    return dict(cos=jnp.cos(ang), sin=jnp.sin(ang), dmask=dmask,
                qdec=jnp.broadcast_to(qdec[:, :, None], wide), kdec=jnp.broadcast_to(kdec[:, :, None], wide),
                cdec=cdec)


def _rot(t, cs, sn):
    half = RET_QK // 2
    t1, t2 = t[:, :half], t[:, half:]
    return jnp.concatenate([t1 * cs - t2 * sn, t1 * sn + t2 * cs], axis=1)


def _rot_bwd(d, cs, sn):
    half = RET_QK // 2
    d1, d2 = d[:, :half], d[:, half:]
    return jnp.concatenate([d1 * cs + d2 * sn, d2 * cs - d1 * sn], axis=1)


def _ret_fwd(zq, zk, zv, zg, rc):
    s = zq.shape[0]
    nc = s // RET_CHUNK
    c = RET_CHUNK

    def body(cd_ref, q_ref, k_ref, v_ref, g_ref, cos_ref, sin_ref, dm_ref, qd_ref, kd_ref,
             yb_ref, o_ref, st_ref, state):
        n = pl.program_id(0)

        @pl.when(n == 0)
        def _():
            state[...] = jnp.zeros_like(state)

        cs, sn = cos_ref[...], sin_ref[...]
        for h in range(RET_HEADS):
            qs, vs = slice(h * RET_QK, (h + 1) * RET_QK), slice(h * RET_V, (h + 1) * RET_V)
            qr = _rot(q_ref[:, qs], cs, sn)
            kr = _rot(k_ref[:, qs], cs, sn) * (RET_QK ** -0.5)
            v = v_ref[:, vs]
            st = state[h]
            st_ref[h, 0] = st.astype(st_ref.dtype)
            sc = _dot(qr, kr, NT) * dm_ref[h]
            o = _dot(sc, v, NN) + _dot(qr * qd_ref[h], st, NN)
            state[h] = st * cd_ref[h] + _dot(kr * kd_ref[h], v, TN)
            o_ref[:, vs] = o
            xh, _ = _norm_rows(o, GN_EPS)
            gv = g_ref[:, vs]
            yb_ref[:, vs] = (gv * _sigmoid(gv) * xh).astype(yb_ref.dtype)

    row = lambda w: pl.BlockSpec((c, w), lambda n: (n, 0))
    const = lambda a: pl.BlockSpec(a.shape, lambda n: (0, 0, 0))
    return pl.pallas_call(
        body,
        out_shape=[jax.ShapeDtypeStruct((s, RET_HEADS * RET_V), MXU_DTYPE),
                   jax.ShapeDtypeStruct((s, RET_HEADS * RET_V), F32),
                   jax.ShapeDtypeStruct((RET_HEADS, nc, RET_QK, RET_V), MXU_DTYPE)],
        grid=(nc,),
        in_specs=[pl.BlockSpec(memory_space=pltpu.SMEM), row(RET_HEADS * RET_QK), row(RET_HEADS * RET_QK),
                  row(RET_HEADS * RET_V), row(RET_HEADS * RET_V), row(RET_QK // 2), row(RET_QK // 2),
                  const(rc["dmask"]), const(rc["qdec"]), const(rc["kdec"])],
        out_specs=[row(RET_HEADS * RET_V), row(RET_HEADS * RET_V),
                   pl.BlockSpec((RET_HEADS, 1, RET_QK, RET_V), lambda n: (0, n, 0, 0))],
        scratch_shapes=[pltpu.VMEM((RET_HEADS, RET_QK, RET_V), F32)],
        compiler_params=_params(1), name="ret_fwd",
    )(rc["cdec"], zq, zk, zv, zg, rc["cos"], rc["sin"], rc["dmask"], rc["qdec"], rc["kdec"])


def _ret_bwd(zq, zk, zv, zg, o_ret, states, dyb, rc):
    s = zq.shape[0]
    nc = s // RET_CHUNK
    c = RET_CHUNK

    def body(cd_ref, q_ref, k_ref, v_ref, g_ref, o_ref, st_ref, dy_ref, cos_ref, sin_ref, dm_ref, qd_ref,
             kd_ref, dq_ref, dk_ref, dv_ref, dg_ref, dstate):
        n = pl.program_id(0)

        @pl.when(n == 0)
        def _():
            dstate[...] = jnp.zeros_like(dstate)

        cs, sn = cos_ref[...], sin_ref[...]
        for h in range(RET_HEADS):
            qs, vs = slice(h * RET_QK, (h + 1) * RET_QK), slice(h * RET_V, (h + 1) * RET_V)
            qr = _rot(q_ref[:, qs], cs, sn)
            kr = _rot(k_ref[:, qs], cs, sn) * (RET_QK ** -0.5)
            v = v_ref[:, vs]
            st = st_ref[h, 0]
            dm, qd, kd = dm_ref[h], qd_ref[h], kd_ref[h]
            xh, rstd = _norm_rows(o_ref[:, vs], GN_EPS)
            gv, dy = g_ref[:, vs], dy_ref[:, vs]
            sg = _sigmoid(gv)
            dg_ref[:, vs] = (dy * xh * (sg * (1.0 + gv * (1.0 - sg)))).astype(dg_ref.dtype)
            do = _norm_rows_bwd(dy * (gv * sg), xh, rstd)
            ds_next = dstate[h]
            sc = _dot(qr, kr, NT) * dm
            da = _dot(do, v, NT) * dm
            dv = _dot(sc, do, TN) + _dot(kr * kd, ds_next, NN)
            dqr = _dot(da, kr, NN) + _dot(do, st, NT) * qd
            dkr = _dot(da, qr, TN) + _dot(v, ds_next, NT) * kd
            dstate[h] = ds_next * cd_ref[h] + _dot(qr * qd, do, TN)
            dq_ref[:, qs] = _rot_bwd(dqr, cs, sn).astype(dq_ref.dtype)
            dk_ref[:, qs] = _rot_bwd(dkr * (RET_QK ** -0.5), cs, sn).astype(dk_ref.dtype)
            dv_ref[:, vs] = dv.astype(dv_ref.dtype)

    row = lambda w: pl.BlockSpec((c, w), lambda n: (nc - 1 - n, 0))
    const = lambda a: pl.BlockSpec(a.shape, lambda n: (0, 0, 0))
    qk_w, v_w = RET_HEADS * RET_QK, RET_HEADS * RET_V
    g_qk, g_v = jax.ShapeDtypeStruct((s, qk_w), MXU_DTYPE), jax.ShapeDtypeStruct((s, v_w), MXU_DTYPE)
    return pl.pallas_call(
        body,
        out_shape=[g_qk, g_qk, g_v, g_v],
        grid=(nc,),
        in_specs=[pl.BlockSpec(memory_space=pltpu.SMEM), row(qk_w), row(qk_w), row(v_w), row(v_w), row(v_w),
                  pl.BlockSpec((RET_HEADS, 1, RET_QK, RET_V), lambda n: (0, nc - 1 - n, 0, 0)), row(v_w),
                  row(RET_QK // 2), row(RET_QK // 2), const(rc["dmask"]), const(rc["qdec"]), const(rc["kdec"])],
        out_specs=[row(qk_w), row(qk_w), row(v_w), row(v_w)],
        scratch_shapes=[pltpu.VMEM((RET_HEADS, RET_QK, RET_V), F32)],
        compiler_params=_params(1), name="ret_bwd",
    )(rc["cdec"], zq, zk, zv, zg, o_ret, states, dyb, rc["cos"], rc["sin"], rc["dmask"], rc["qdec"],
      rc["kdec"])


ROW_TILE = 256


def _mix_fwd(x, ya_b, yb_b, ze, wap_t, wrp, wout, gam, bet):
    s = x.shape[0]
    tm = ROW_TILE

    def body(x_ref, ya_ref, yb_ref, ze_ref, wap_ref, wrp_ref, wout_ref, g_ref, b_ref,
             x1_ref, u1_ref, a1_ref, a2_ref, mg_ref, x1b_ref):
        a1 = _dot(ya_ref[...], wap_ref[...], NT)
        a2 = _dot(yb_ref[...], wrp_ref[...], NN)
        ze_ = ze_ref[...]
        merged = _sigmoid(ze_[:, :D_MODEL]) * a1 + _sigmoid(ze_[:, D_MODEL:]) * a2
        u = ALPHA * x_ref[...] + _dot(merged, wout_ref[...], NN)
        xh, _ = _norm_rows(u, LN_EPS)
        x1 = xh * g_ref[...] + b_ref[...]
        x1_ref[...] = x1
        x1b_ref[...] = x1.astype(x1b_ref.dtype)
        u1_ref[...] = u
        a1_ref[...] = a1
        a2_ref[...] = a2
        mg_ref[...] = merged.astype(mg_ref.dtype)

    row = lambda w: pl.BlockSpec((tm, w), lambda i: (i, 0))
    full = _resident
    f = jax.ShapeDtypeStruct((s, D_MODEL), F32)
    m = jax.ShapeDtypeStruct((s, D_MODEL), MXU_DTYPE)
    return pl.pallas_call(
        body, out_shape=[f, f, f, f, m, m], grid=(s // tm,),
        in_specs=[row(D_MODEL), row(ATTN_OUT), row(RET_HEADS * RET_V), row(2 * D_MODEL), full(wap_t), full(wrp),
                  full(wout), full(gam), full(bet)],
        out_specs=[row(D_MODEL)] * 6, compiler_params=_params(1), name="mix_fwd",
    )(x, ya_b, yb_b, ze, wap_t, wrp, wout, gam, bet)


def _mix_bwd(dx1, u1, a1, a2, ze, wap_t, wrp, wout, gam):
    s = dx1.shape[0]
    tm = ROW_TILE

    dils = [d for _, d in ATTN_GROUPS]

    def body(dx_ref, u_ref, a1_ref, a2_ref, ze_ref, wap_ref, wrp_ref, wout_ref, g_ref,
             dres_ref, du_ref, da1_ref, da2_ref, dze_ref, dyb_ref, dgam_ref, dbet_ref, *rest):
        dya_refs, dya_scr = rest[:len(dils)], rest[len(dils)]
        @pl.when(pl.program_id(0) == 0)
        def _():
            dgam_ref[...] = jnp.zeros_like(dgam_ref)
            dbet_ref[...] = jnp.zeros_like(dbet_ref)

        dx = dx_ref[...]
        xh, rstd = _norm_rows(u_ref[...], LN_EPS)
        dgam_ref[...] += jnp.sum(dx * xh, axis=0, keepdims=True)
        dbet_ref[...] += jnp.sum(dx, axis=0, keepdims=True)
        du = _norm_rows_bwd(dx * g_ref[...], xh, rstd)
        dres_ref[...] = ALPHA * du
        du_ref[...] = du.astype(du_ref.dtype)
        dm = _dot(du, wout_ref[...], NT)
        ze_ = ze_ref[...]
        sa, sb = _sigmoid(ze_[:, :D_MODEL]), _sigmoid(ze_[:, D_MODEL:])
        da1, da2 = dm * sa, dm * sb
        dze_ref[...] = jnp.concatenate([dm * a1_ref[...] * (sa * (1.0 - sa)),
                                        dm * a2_ref[...] * (sb * (1.0 - sb))], axis=1).astype(dze_ref.dtype)
        da1_ref[...] = da1.astype(da1_ref.dtype)
        da2_ref[...] = da2.astype(da2_ref.dtype)
        dyb_ref[...] = _dot(da2, wrp_ref[...], NT)
        dya = _dot(da1, wap_ref[...], NN)
        _cols_to_blocks(dya, dya_scr)
        for g, d in enumerate(dils):
            if d == 1:
                dya_refs[g][...] = dya
            else:
                _tokens_to_phase(dya_scr, dya_refs[g], d, ATTN_OUT)

    row = lambda w: pl.BlockSpec((tm, w), lambda i: (i, 0))
    full = _resident
    vec = pl.BlockSpec((1, D_MODEL), lambda i: (0, 0))
    f = lambda w: jax.ShapeDtypeStruct((s, w), F32)
    m = lambda w: jax.ShapeDtypeStruct((s, w), MXU_DTYPE)
    v = jax.ShapeDtypeStruct((1, D_MODEL), F32)
    res = pl.pallas_call(
        body,
        out_shape=[f(D_MODEL), m(D_MODEL), m(D_MODEL), m(D_MODEL), m(2 * D_MODEL), f(RET_HEADS * RET_V), v, v]
        + [jax.ShapeDtypeStruct((s // d, d * ATTN_OUT), F32) for d in dils],
        grid=(s // tm,),
        in_specs=[row(D_MODEL)] * 4 + [row(2 * D_MODEL), full(wap_t), full(wrp), full(wout), full(gam)],
        out_specs=[row(D_MODEL)] * 4 + [row(2 * D_MODEL), row(RET_HEADS * RET_V), vec, vec]
        + [pl.BlockSpec((tm // d, d * ATTN_OUT), lambda i: (i, 0)) for d in dils],
        scratch_shapes=[pltpu.VMEM((ATTN_OUT // LANES, tm, LANES), F32)],
        compiler_params=_params(1), name="mix_bwd",
    )(dx1, u1, a1, a2, ze, wap_t, wrp, wout, gam)
    return (*res[:8], list(res[8:]))


FF_CHUNK = 1408


def _ffn_fwd(x1, wg_t, wu_t, wd, gam, bet):
    s = x1.shape[0]
    tm, fc = ROW_TILE, FF_CHUNK

    def body(x_ref, wg_ref, wu_ref, wd_ref, g_ref, b_ref, x2_ref, u2_ref, a_ref, b_out_ref, h_ref, x2b_ref):
        xv = x_ref[...]
        u = ALPHA * xv
        for f0 in range(0, D_FF, fc):
            ch = slice(f0, f0 + fc)
            a = _dot(xv, wg_ref[ch, :], NT)
            b = _dot(xv, wu_ref[ch, :], NT)
            hid = a * _sigmoid(a) * b
            u = u + _dot(hid, wd_ref[ch, :], NN)
            a_ref[:, ch] = a
            b_out_ref[:, ch] = b
            h_ref[:, ch] = hid.astype(h_ref.dtype)
        xh, _ = _norm_rows(u, LN_EPS)
        u2_ref[...] = u
        x2 = xh * g_ref[...] + b_ref[...]
        x2_ref[...] = x2
        x2b_ref[...] = x2.astype(x2b_ref.dtype)

    row = lambda w: pl.BlockSpec((tm, w), lambda i: (i, 0))
    f = lambda w: jax.ShapeDtypeStruct((s, w), F32)
    m = lambda w: jax.ShapeDtypeStruct((s, w), MXU_DTYPE)
    return pl.pallas_call(
        body, out_shape=[f(D_MODEL), f(D_MODEL), f(D_FF), f(D_FF), m(D_FF), m(D_MODEL)],
        grid=(s // tm,),
        in_specs=[row(D_MODEL), _resident(wg_t), _resident(wu_t), _resident(wd), _resident(gam), _resident(bet)],
        out_specs=[row(D_MODEL), row(D_MODEL), row(D_FF), row(D_FF), row(D_FF), row(D_MODEL)],
        compiler_params=_params(1), name="ffn_fwd",
    )(x1, wg_t, wu_t, wd, gam, bet)


def _ffn_bwd(dx2, u2, fa, fb, wg_t, wu_t, wd, gam):
    s = dx2.shape[0]
    tm, fc = ROW_TILE, FF_CHUNK

    def body(dx_ref, u_ref, a_ref, b_ref, wg_ref, wu_ref, wd_ref, g_ref,
             dx1_ref, du_ref, da_ref, db_ref, dgam_ref, dbet_ref):
        @pl.when(pl.program_id(0) == 0)
        def _():
            dgam_ref[...] = jnp.zeros_like(dgam_ref)
            dbet_ref[...] = jnp.zeros_like(dbet_ref)

        dx = dx_ref[...]
        xh, rstd = _norm_rows(u_ref[...], LN_EPS)
        dgam_ref[...] += jnp.sum(dx * xh, axis=0, keepdims=True)
        dbet_ref[...] += jnp.sum(dx, axis=0, keepdims=True)
        du = _norm_rows_bwd(dx * g_ref[...], xh, rstd)
        du_ref[...] = du.astype(du_ref.dtype)
        acc = ALPHA * du
        for f0 in range(0, D_FF, fc):
            ch = slice(f0, f0 + fc)
            dh = _dot(du, wd_ref[ch, :], NT)
            a, b = a_ref[:, ch], b_ref[:, ch]
            sg = _sigmoid(a)
            da = dh * b * (sg * (1.0 + a * (1.0 - sg)))
            db = dh * (a * sg)
            acc = acc + _dot(da, wg_ref[ch, :], NN) + _dot(db, wu_ref[ch, :], NN)
            da_ref[:, ch] = da.astype(da_ref.dtype)
            db_ref[:, ch] = db.astype(db_ref.dtype)
        dx1_ref[...] = acc

    row = lambda w: pl.BlockSpec((tm, w), lambda i: (i, 0))
    vec = pl.BlockSpec((1, D_MODEL), lambda i: (0, 0))
    v = jax.ShapeDtypeStruct((1, D_MODEL), F32)
    return pl.pallas_call(
        body,
        out_shape=[jax.ShapeDtypeStruct((s, D_MODEL), F32), jax.ShapeDtypeStruct((s, D_MODEL), MXU_DTYPE),
                   jax.ShapeDtypeStruct((s, D_FF), MXU_DTYPE), jax.ShapeDtypeStruct((s, D_FF), MXU_DTYPE), v, v],
        grid=(s // tm,),
        in_specs=[row(D_MODEL), row(D_MODEL), row(D_FF), row(D_FF), _resident(wg_t), _resident(wu_t),
                  _resident(wd), _resident(gam)],
        out_specs=[row(D_MODEL), row(D_MODEL), row(D_FF), row(D_FF), vec, vec],
        compiler_params=_params(1), name="ffn_bwd",
    )(dx2, u2, fa, fb, wg_t, wu_t, wd, gam)


def _loss_head(y, target):
    s = y.shape[0]
    tr = _pick(s, (512, 256, 128))

    def body(y_ref, t_ref, dy_ref, l_ref):
        @pl.when(pl.program_id(0) == 0)
        def _():
            l_ref[...] = jnp.zeros_like(l_ref)

        e = y_ref[...] - t_ref[...]
        dy_ref[...] = e * (1.0 / D_MODEL)
        part = jnp.sum(jnp.sum(e * e, axis=1, keepdims=True), axis=0, keepdims=True)
        l_ref[...] += part * (0.5 / D_MODEL)

    spec = pl.BlockSpec((tr, D_MODEL), lambda i: (i, 0))
    dy, part = pl.pallas_call(
        body, out_shape=[jax.ShapeDtypeStruct((s, D_MODEL), F32), jax.ShapeDtypeStruct((8, BLK), F32)],
        grid=(s // tr,), in_specs=[spec, spec], out_specs=[spec, pl.BlockSpec((8, BLK), lambda i: (0, 0))],
        compiler_params=_params(1), name="loss_head",
    )(y, target)
    return dy, part[0, 0]


def _adamw(w, g, m, v, name):
    rows, cols = w.shape
    budget = 1 << 20
    cands = [t for t in range(8, rows + 1, 8) if rows % t == 0 and t * cols * 4 <= budget]
    tr = max(cands) if cands else rows

    def body(w_ref, g_ref, m_ref, v_ref, d_ref, nm_ref, nv_ref):
        gv = g_ref[...]
        mn = ADAM_B1 * m_ref[...] + (1.0 - ADAM_B1) * gv
        vn = ADAM_B2 * v_ref[...] + (1.0 - ADAM_B2) * (gv * gv)
        m_hat = mn / (1.0 - ADAM_B1 ** ADAM_STEP)
        v_hat = vn / (1.0 - ADAM_B2 ** ADAM_STEP)
        d_ref[...] = -ADAM_LR * (m_hat / (jnp.sqrt(v_hat) + ADAM_EPS) + ADAM_WD * w_ref[...])
        nm_ref[...] = mn
        nv_ref[...] = vn

    spec = pl.BlockSpec((tr, cols), lambda i: (i, 0))
    shape = jax.ShapeDtypeStruct((rows, cols), F32)
    return pl.pallas_call(
        body, out_shape=[shape, shape, shape], grid=(rows // tr,), in_specs=[spec] * 4, out_specs=[spec] * 3,
        compiler_params=_params(1), name=name,
    )(w, g, m, v)


MESH_ID = pl.DeviceIdType.MESH
ANY = pl.BlockSpec(memory_space=pl.ANY)


def _place():
    x, y, c = lax.axis_index("x"), lax.axis_index("y"), lax.axis_index("c")
    other_chips = [(1 - x, y), (x, 1 - y), (1 - x, 1 - y)]
    return x, y, c, other_chips


def _chip_no(chip):
    return 2 * chip[0] + chip[1]


def _gather_exchange(f_ref, g_ref, send_sems, recv_sems, hr):
    x, y, c, chips = _place()
    sibling = (x, y, 1 - c)

    def piece(chip, half):
        return g_ref.at[_chip_no(chip), pl.ds(half * hr, hr), :]

    def copy(k, src, dst, to):
        return pltpu.make_async_remote_copy(src_ref=src, dst_ref=dst, send_sem=send_sems.at[k],
                                            recv_sem=recv_sems.at[k], device_id=to, device_id_type=MESH_ID)

    own = copy(6, f_ref, g_ref.at[_chip_no((x, y))], sibling)
    own.start()
    my_half = f_ref.at[pl.ds(c * hr, hr), :]
    first = [copy(k, my_half, piece((x, y), c), (*chip, c)) for k, chip in enumerate(chips)]
    for cp in first:
        cp.start()
    passed = [copy(3 + k, piece(chip, c), piece(chip, c), sibling) for k, chip in enumerate(chips)]
    for k, chip in enumerate(chips):
        copy(k, my_half, piece(chip, c), (*chip, c)).wait_recv()
        passed[k].start()
    for k, chip in enumerate(chips):
        copy(3 + k, my_half, piece(chip, 1 - c), sibling).wait_recv()
    for cp in first + passed:
        cp.wait_send()
    own.wait()


GATHER_SEMS = (pltpu.SemaphoreType.DMA((7,)), pltpu.SemaphoreType.DMA((7,)))


HBM = pltpu.MemorySpace.HBM
ONE_SEM_PAIR = (pltpu.SemaphoreType.DMA, pltpu.SemaphoreType.DMA)


def _sibling_of_me():
    x, y, c, _ = _place()
    return [(x, y, 1 - c)]


def _same_core_of_other_chips():
    x, y, c, chips = _place()
    return [(*chip, c) for chip in chips]


def _on_sequencer(name, collective_id, sems, peers, exchange):
    @pl.kernel(mesh=plsc.ScalarSubcoreMesh(axis_name="sequencer", num_cores=1), name=name, scratch_types=sems,
               compiler_params=pltpu.CompilerParams(collective_id=collective_id))
    def launch(*sem_refs):
        barrier = pltpu.get_barrier_semaphore()
        devices = peers()
        for peer in devices:
            pl.semaphore_signal(barrier, inc=1, device_id=peer, device_id_type=MESH_ID)
        pl.semaphore_wait(barrier, len(devices))
        exchange(*sem_refs)

    launch()


def _all_gather_rows(flat, behind=None):
    r, cols = flat.shape
    out = jax.ShapeDtypeStruct((N_CHIPS, r, cols), flat.dtype)
    if behind is not None:
        f_ref, g_ref = jax.new_ref(flat, memory_space=HBM), jax.empty_ref(out, memory_space=HBM)
        _on_sequencer(f"all_gather_weights_behind_{behind[1]}", behind[0], GATHER_SEMS,
                      lambda: _sibling_of_me() + _same_core_of_other_chips(),
                      lambda s, r_: _gather_exchange(f_ref, g_ref, s, r_, r // 2))
        return g_ref[...]

    def body(f_ref, g_ref, send_sems, recv_sems):
        _gather_exchange(f_ref, g_ref, send_sems, recv_sems, r // 2)

    return pl.pallas_call(body, out_shape=out, in_specs=[ANY], out_specs=ANY, scratch_shapes=list(GATHER_SEMS),
                          name="all_gather_weights")(flat)


def _swap_halves(gall, behind=None):
    _, r, cols = gall.shape
    hr = r // 2
    out = jax.ShapeDtypeStruct((N_CHIPS, hr, cols), gall.dtype)

    def exchange(g_ref, a_ref, send_sem, recv_sem):
        x, y, c, _ = _place()
        cp = pltpu.make_async_remote_copy(src_ref=g_ref.at[:, pl.ds((1 - c) * hr, hr), :], dst_ref=a_ref,
                                          send_sem=send_sem, recv_sem=recv_sem, device_id=(x, y, 1 - c),
                                          device_id_type=MESH_ID)
        cp.start()
        cp.wait()

    if behind is not None:
        g_ref, a_ref = jax.new_ref(gall, memory_space=HBM), jax.empty_ref(out, memory_space=HBM)
        _on_sequencer(f"grad_swap_halves_behind_{behind[1]}", behind[0], ONE_SEM_PAIR, _sibling_of_me,
                      lambda s, r_: exchange(g_ref, a_ref, s, r_))
        return a_ref[...]

    def body(g_ref, a_ref, send_sem, recv_sem):
        exchange(g_ref, a_ref, send_sem, recv_sem)

    return pl.pallas_call(body, out_shape=out, in_specs=[ANY], out_specs=ANY, scratch_shapes=list(ONE_SEM_PAIR),
                          name="grad_swap_halves")(gall)


def _scatter_to_chips(p, behind=None):
    _, hr, cols = p.shape
    out = jax.ShapeDtypeStruct((3, hr, cols), p.dtype)
    sems = (pltpu.SemaphoreType.DMA((3,)), pltpu.SemaphoreType.DMA((3,)))

    def exchange(p_ref, b_ref, send_sems, recv_sems):
        x, y, c, chips = _place()
        cps = [pltpu.make_async_remote_copy(src_ref=p_ref.at[_chip_no(chip)], dst_ref=b_ref.at[k],
                                            send_sem=send_sems.at[k], recv_sem=recv_sems.at[k],
                                            device_id=(*chip, c), device_id_type=MESH_ID)
               for k, chip in enumerate(chips)]
        for cp in cps:
            cp.start()
        for cp in cps:
            cp.wait()

    if behind is not None:
        p_ref, b_ref = jax.new_ref(p, memory_space=HBM), jax.empty_ref(out, memory_space=HBM)
        _on_sequencer(f"grad_scatter_chips_behind_{behind[1]}", behind[0], sems, _same_core_of_other_chips,
                      lambda s, r_: exchange(p_ref, b_ref, s, r_))
        return b_ref[...]

    def body(p_ref, b_ref, send_sems, recv_sems):
        exchange(p_ref, b_ref, send_sems, recv_sems)

    return pl.pallas_call(body, out_shape=out, in_specs=[ANY], out_specs=ANY, scratch_shapes=list(sems),
                          name="grad_scatter_chips")(p)


def _share_with_sibling(full, behind=None):
    r, cols = full.shape
    hr = r // 2

    def exchange(in_ref, out_ref, send_sem, recv_sem):
        x, y, c, _ = _place()
        cp = pltpu.make_async_remote_copy(src_ref=in_ref.at[pl.ds(c * hr, hr), :],
                                          dst_ref=out_ref.at[pl.ds(c * hr, hr), :],
                                          send_sem=send_sem, recv_sem=recv_sem, device_id=(x, y, 1 - c),
                                          device_id_type=MESH_ID)
        cp.start()
        cp.wait()

    if behind is not None:
        full_ref = jax.new_ref(full, memory_space=HBM)
        _on_sequencer(f"grad_share_sibling_behind_{behind[1]}", behind[0], ONE_SEM_PAIR, _sibling_of_me,
                      lambda s, r_: exchange(full_ref, full_ref, s, r_))
        return full_ref[...]

    def body(in_ref, out_ref, send_sem, recv_sem):
        exchange(in_ref, out_ref, send_sem, recv_sem)

    return pl.pallas_call(
        body, out_shape=jax.ShapeDtypeStruct((r, cols), full.dtype), in_specs=[ANY], out_specs=ANY,
        input_output_aliases={0: 0}, scratch_shapes=list(ONE_SEM_PAIR), name="grad_share_sibling",
    )(full)


def _all_reduce_small(v):
    r, cols = v.shape
    n_dev = 8

    def body(x_ref, out_ref, gat_ref, send_sems, recv_sems, local_sem):
        x, y, c, chips = _place()
        me, sibling = (x, y, c), (x, y, 1 - c)

        def slot(px, py, pc):
            return gat_ref.at[4 * px + 2 * py + pc]

        def copy(k, block, to, src=None):
            return pltpu.make_async_remote_copy(src_ref=slot(*block) if src is None else src, dst_ref=slot(*block),
                                                send_sem=send_sems.at[k], recv_sem=recv_sems.at[k], device_id=to,
                                                device_id_type=MESH_ID)

        mine = pltpu.make_async_copy(x_ref, slot(*me), local_sem)
        mine.start()
        first = [copy(0, me, sibling, src=x_ref)]
        first += [copy(1 + k, me, (*chip, c), src=x_ref) for k, chip in enumerate(chips)]
        for cp in first:
            cp.start()
        passed = [copy(4 + k, (*chip, c), sibling) for k, chip in enumerate(chips)]
        for k, chip in enumerate(chips):
            copy(1 + k, (*chip, c), me).wait_recv()
            passed[k].start()
        copy(0, sibling, me).wait_recv()
        for k, chip in enumerate(chips):
            copy(4 + k, (*chip, 1 - c), me).wait_recv()
        for cp in first + passed:
            cp.wait_send()
        mine.wait()
        acc = gat_ref[0]
        for d in range(1, n_dev):
            acc = acc + gat_ref[d]
        out_ref[...] = acc

    vmem = pl.BlockSpec(memory_space=pltpu.VMEM)
    return pl.pallas_call(
        body, out_shape=jax.ShapeDtypeStruct((r, cols), v.dtype), in_specs=[vmem], out_specs=vmem,
        scratch_shapes=[pltpu.VMEM((n_dev, r, cols), v.dtype), pltpu.SemaphoreType.DMA((7,)),
                        pltpu.SemaphoreType.DMA((7,)), pltpu.SemaphoreType.DMA],
        name="all_reduce_small",
    )(v)


WIRE_DTYPE = jnp.bfloat16


def _add_own_half(gall, got, place):
    _, r, cols = gall.shape
    hr = r // 2
    tr = _row_tile(hr)
    g4 = gall.reshape(N_CHIPS, 2, hr, cols)

    def body(s_ref, g_ref, a_ref, o_ref):
        o_ref[...] = (g_ref[0] + a_ref[...]).astype(o_ref.dtype)

    return pl.pallas_call(
        body, out_shape=jax.ShapeDtypeStruct((N_CHIPS, hr, cols), WIRE_DTYPE),
        grid_spec=pltpu.PrefetchScalarGridSpec(
            num_scalar_prefetch=1, grid=(N_CHIPS, hr // tr),
            in_specs=[pl.BlockSpec((1, 1, tr, cols), lambda j, i, s: (j, s[1], i, 0)),
                      pl.BlockSpec((1, tr, cols), lambda j, i, s: (j, i, 0))],
            out_specs=pl.BlockSpec((1, tr, cols), lambda j, i, s: (j, i, 0))),
        compiler_params=_params(2), name="grad_add_halves",
    )(place, g4, got)


def _add_chip_parts(gall, got, parts, place):
    _, r, cols = gall.shape
    hr = r // 2
    tr = _row_tile(hr)
    g4 = gall.reshape(N_CHIPS, 2, hr, cols)
    nt = hr // tr

    def body(s_ref, g_ref, a_ref, b0_ref, b1_ref, b2_ref, o_ref):
        own = g_ref[0, 0] + a_ref[0]
        o_ref[...] = ((own + b0_ref[0].astype(F32)) + b1_ref[0].astype(F32)) + b2_ref[0].astype(F32)

    part = lambda k: pl.BlockSpec((1, tr, cols), lambda i, s: (k, i, 0))
    return pl.pallas_call(
        body, out_shape=jax.ShapeDtypeStruct((r, cols), F32),
        grid_spec=pltpu.PrefetchScalarGridSpec(
            num_scalar_prefetch=1, grid=(nt,),
            in_specs=[pl.BlockSpec((1, 1, tr, cols), lambda i, s: (s[0], s[1], i, 0)),
                      pl.BlockSpec((1, tr, cols), lambda i, s: (s[0], i, 0)), part(0), part(1), part(2)],
            out_specs=pl.BlockSpec((tr, cols), lambda i, s: (s[1] * nt + i, 0))),
        compiler_params=_params(1), name="grad_add_chips",
    )(place, g4, got, parts, parts, parts)


def _pack_shards(weights, l, group, dtype):
    parts = []
    for name, _ in group:
        w = weights[name][l]
        parts.append((w.T if name in COLUMN_SHARDED else w).reshape(-1, D_MODEL).astype(dtype))
    return parts[0] if len(parts) == 1 else jnp.concatenate(parts, axis=0)


def _unpack_gathered(gathered, group):
    w, off = {}, 0
    for name, rows in group:
        blk = gathered[:, off:off + rows]
        if name == "w_attn_proj":
            w[name] = blk.reshape(N_CHIPS * 256, ATTN_OUT)
        else:
            w[name] = blk.reshape(N_CHIPS * rows, D_MODEL)
        off += rows
    return w


def _pack_full_grads(g, group):
    parts = [g[name].reshape(N_CHIPS, rows, D_MODEL) for name, rows in group]
    return parts[0] if len(parts) == 1 else jnp.concatenate(parts, axis=1)


def _unpack_shard_grads(fulls):
    out = {}
    for gi, group in enumerate(PACK_GROUPS):
        off = 0
        for name, rows in group:
            blks = [fulls[(l, gi)][off:off + rows] for l in range(DEPTH)]
            out[name] = jnp.stack([b.reshape(256, ATTN_OUT) if name == "w_attn_proj" else b for b in blks])
            off += rows
    return out


class _StaticWeights:
    def __init__(self, layers):
        self.layers = layers

    def group(self, l, gi):
        return self.layers[l]

    def layer(self, l):
        return self.layers[l]


class _WeightGathers:
    def __init__(self, weights):
        self.flats = {(l, gi): _pack_shards(weights, l, g, MXU_DTYPE)
                      for l in range(DEPTH) for gi, g in enumerate(PACK_GROUPS)}
        self.order = sorted(self.flats)
        self.landed, self.unpacked, self.issued = {}, {}, 0
        self._issue()

    def _issue(self, after=None):
        key = self.order[self.issued]
        flat = self.flats[key]
        if self.issued == 0:
            self.landed[key] = _all_gather_rows(flat)
        else:
            prev = self.order[self.issued - 1]
            ties = (self.landed[prev], flat) if after is None else (self.landed[prev], flat, after)
            tied = lax.optimization_barrier(ties)
            self.landed[prev], flat = tied[0], tied[1]
            after = tied[2] if after is not None else None
            self.landed[key] = _all_gather_rows(flat, behind=(1, f"l{key[0]}g{key[1]}"))
        self.issued += 1
        return after

    def prefetch(self, after):
        return self._issue(after) if self.issued < len(self.order) else after

    def group(self, l, gi):
        key = (l, gi)
        if key not in self.unpacked:
            self.unpacked[key] = _unpack_gathered(self.landed[key], PACK_GROUPS[gi])
        return self.unpacked[key]

    def layer(self, l):
        w = {}
        for gi in range(len(PACK_GROUPS)):
            w.update(self.group(l, gi))
        return w


class _GradReduction:
    LAST = 3

    def __init__(self, gw, group, place, behind):
        self.gall, self.place, self.behind, self.stage = _pack_full_grads(gw, group), place, behind, 0
        self.pending = None

    def _how(self, k):
        return None if self.behind is None else (self.behind[0][k], self.behind[1])

    def advance(self, value=None):
        if self.stage > self.LAST:
            return value
        if self.stage > 0 and value is not None:
            value, self.pending = lax.optimization_barrier((value, self.pending))
        if self.stage == 0:
            self.pending = _swap_halves(self.gall, self._how(0))
        elif self.stage == 1:
            self.got = self.pending
            self.pending = _scatter_to_chips(_add_own_half(self.gall, self.got, self.place), self._how(1))
        elif self.stage == 2:
            self.pending = _share_with_sibling(
                _add_chip_parts(self.gall, self.got, self.pending, self.place), self._how(2))
        self.stage += 1
        return value

    def result(self):
        while self.stage < self.LAST:
            self.advance()
        return self.pending


class _GradReductions:
    def __init__(self, place):
        self.place, self.all = place, {}

    def start(self, l, gi, gw):
        hidden = (l, gi) != (0, IN_GROUP)
        slot = len(self.all) % 2
        how = ((2 + 3 * slot, 3 + 3 * slot, 4 + 3 * slot), f"l{l}g{gi}") if hidden else None
        red = _GradReduction(gw, PACK_GROUPS[gi], self.place, how)
        if hidden:
            red.advance()
        self.all[(l, gi)] = red

    def tick(self, value):
        for red in self.all.values():
            if red.behind is not None:
                value = red.advance(value)
        return value

    def results(self):
        return {key: red.result() for key, red in self.all.items()}


def _layer_fwd(x, x_b, weights, l, lnp, bias, rc, prefetch):
    tag = f"l{l}"
    w_in = weights.group(l, IN_GROUP)["w_in"]
    z = {}
    for i, (name, runs, narrow, dil) in enumerate(SECTIONS):
        z[name] = _matmul(x_b, _section_rows(w_in, runs), "nt", bias=_section_rows(lnp["b_in"], runs),
                          out_dtype=MXU_DTYPE if narrow else F32, out_phase=dil, name=f"in_proj_{name}_{tag}")
        if i == 0:
            z[name] = prefetch(z[name])
    w = weights.group(l, REST_GROUP)
    o_list, l_list = [], []
    for g, (_, dil) in enumerate(ATTN_GROUPS):
        o, lse = _attn_fwd(z[f"a{g}"], bias, g, dil)
        o_list.append(o)
        l_list.append(lse)
    ya_b, ya_views, lt_views = _attn_combine(o_list, l_list)
    yb_b, o_ret, states = _ret_fwd(z["bq"], z["bk"], z["c"], z["d"], rc)
    x1, u1, a1, a2, mg_b, x1_b = _mix_fwd(x, ya_b, yb_b, z["e"], w["w_attn_proj"], w["w_ret_proj"], w["w_out"],
                                          lnp["ln1_g"], lnp["ln1_b"])
    x1 = prefetch(x1)
    x2, u2, fa, fb, h_b, x2_b = _ffn_fwd(x1, w["w_ffn_gate"], w["w_ffn_up"], w["w_ffn_down"], lnp["ln2_g"],
                                         lnp["ln2_b"])
    saved = dict(x_b=x_b, z=z, ya_views=ya_views, ya_b=ya_b, lt_views=lt_views, yb_b=yb_b, o_ret=o_ret, states=states,
                 x1_b=x1_b, u1=u1, a1=a1, a2=a2, mg_b=mg_b, u2=u2, fa=fa, fb=fb, h_b=h_b)
    return x2, x2_b, saved


def _layer_bwd(dx2, w, l, lnp, sv, bias, rc, reductions=None):
    s = dx2.shape[0]
    tag = f"l{l}"
    z = sv["z"]
    step = reductions.tick if reductions is not None else (lambda v: v)
    dx1, du2_b, da_b, db_b, dg2, dbt2 = _ffn_bwd(dx2, sv["u2"], sv["fa"], sv["fb"], w["w_ffn_gate"], w["w_ffn_up"],
                                                  w["w_ffn_down"], lnp["ln2_g"])
    dx1 = step(dx1)
    gw = {}
    gw["w_ffn_down"] = _matmul(sv["h_b"], du2_b, "tn", name=f"dw_ffn_down_{tag}")
    gw["w_ffn_gate"] = _matmul(da_b, sv["x1_b"], "tn", name=f"dw_ffn_gate_{tag}")
    gw["w_ffn_up"] = _matmul(db_b, sv["x1_b"], "tn", name=f"dw_ffn_up_{tag}")
    dres, du1_b, da1_b, da2_b, dze, dyb, dg1, dbt1, dya_views = _mix_bwd(
        dx1, sv["u1"], sv["a1"], sv["a2"], z["e"], w["w_attn_proj"], w["w_ret_proj"], w["w_out"], lnp["ln1_g"])
    gw["w_out"] = _matmul(sv["mg_b"], du1_b, "tn", name=f"dw_out_{tag}")
    gw["w_attn_proj"] = _matmul(da1_b, sv["ya_b"], "tn", name=f"dw_attn_proj_{tag}")
    gw["w_ret_proj"] = _matmul(sv["yb_b"], da2_b, "tn", name=f"dw_ret_proj_{tag}")
    dyb = step(dyb)
    if reductions is not None:
        reductions.start(l, REST_GROUP, gw)
    dzq, dzk, dzv, dzg = _ret_bwd(z["bq"], z["bk"], z["c"], z["d"], sv["o_ret"], sv["states"], dyb, rc)
    dzq = step(dzq)
    dz = {"bq": dzq, "bk": dzk, "c": dzv, "d": dzg, "e": dze}
    dbias_l = []
    for g, (_, dil) in enumerate(ATTN_GROUPS):
        dz[f"a{g}"], dbg = _attn_bwd(z[f"a{g}"], bias, dya_views[g], sv["ya_views"][g], sv["lt_views"][g], g, dil)
        dbias_l.append(dbg)
    last = f"a{N_GROUPS - 1}"
    dz[last] = step(dz[last])
    dx = dres
    for name, runs, _, dil in SECTIONS:
        dx = _matmul(dz[name], _section_rows(w["w_in"], runs), "nn", addend=dx, a_phase=dil,
                     name=f"dx_in_proj_{name}_{tag}")
    dx = step(dx)
    dw, db = {}, {}
    for name, _, _, dil in SECTIONS:
        dw[name], db[name] = _matmul(dz[name], sv["x_b"], "tn", colsum=True, a_phase=dil,
                                     name=f"dw_in_proj_{name}_{tag}")
    gw["w_in"] = _unsection(dw, axis=0)
    if reductions is not None:
        reductions.start(l, IN_GROUP, gw)
    small = dict(b_in=_unsection(db, axis=1), ln1_g=dg1, ln1_b=dbt1, ln2_g=dg2, ln2_b=dbt2)
    return dx, gw, small, jnp.concatenate(dbias_l, axis=0)


def _forward_backward(x, target, rel_bias, weights, lnps, reductions=None):
    s = x.shape[0]
    bmaps = _bucket_maps()
    bias = _bias_tiles(rel_bias, bmaps)
    rc = _ret_consts(s)
    prefetch = getattr(weights, "prefetch", lambda v: v)
    saved = []
    h, h_b = x, x.astype(MXU_DTYPE)
    for l in range(DEPTH):
        h, h_b, sv = _layer_fwd(h, h_b, weights, l, lnps[l], bias, rc, prefetch)
        saved.append(sv)
    dh, loss_part = _loss_head(h, target)
    gws, smalls, dbiases = [None] * DEPTH, [None] * DEPTH, [None] * DEPTH
    for l in reversed(range(DEPTH)):
        dh, gws[l], smalls[l], dbiases[l] = _layer_bwd(dh, weights.layer(l), l, lnps[l], saved[l], bias, rc,
                                                       reductions)
    d_rel_bias = _bias_tiles_bwd(dbiases, bmaps)
    return loss_part, dh, gws, smalls, d_rel_bias


SMALL_NAMES = ("rel_bias", "b_in", "ln1_g", "ln1_b", "ln2_g", "ln2_b")
SMALL_ROWS = 32


def _pack_small(vals):
    flat = jnp.concatenate([vals[n].reshape(-1) for n in SMALL_NAMES])
    return jnp.pad(flat, (0, SMALL_ROWS * D_MODEL - flat.shape[0])).reshape(SMALL_ROWS, D_MODEL)


def _unpack_small(packed, like):
    flat = packed.reshape(-1)
    out, off = {}, 0
    for n in SMALL_NAMES:
        size = like[n].size
        out[n] = flat[off:off + size].reshape(like[n].shape)
        off += size
    return out


def kernel(x, rel_bias, w_in, b_in, w_attn_proj, w_ret_proj, w_out, ln1_g, ln1_b, w_ffn_gate, w_ffn_up, w_ffn_down, ln2_g, ln2_b, loss_target, m_rel_bias, m_w_in, m_b_in, m_w_attn_proj, m_w_ret_proj, m_w_out, m_ln1_g, m_ln1_b, m_w_ffn_gate, m_w_ffn_up, m_w_ffn_down, m_ln2_g, m_ln2_b, v_rel_bias, v_w_in, v_b_in, v_w_attn_proj, v_w_ret_proj, v_w_out, v_ln1_g, v_ln1_b, v_w_ffn_gate, v_w_ffn_up, v_w_ffn_down, v_ln2_g, v_ln2_b):
    big = dict(w_in=(w_in, m_w_in, v_w_in), w_attn_proj=(w_attn_proj, m_w_attn_proj, v_w_attn_proj),
               w_ret_proj=(w_ret_proj, m_w_ret_proj, v_w_ret_proj), w_out=(w_out, m_w_out, v_w_out),
               w_ffn_gate=(w_ffn_gate, m_w_ffn_gate, v_w_ffn_gate), w_ffn_up=(w_ffn_up, m_w_ffn_up, v_w_ffn_up),
               w_ffn_down=(w_ffn_down, m_w_ffn_down, v_w_ffn_down))
    small_w = dict(rel_bias=rel_bias, b_in=b_in, ln1_g=ln1_g, ln1_b=ln1_b, ln2_g=ln2_g, ln2_b=ln2_b)
    small_m = dict(rel_bias=m_rel_bias, b_in=m_b_in, ln1_g=m_ln1_g, ln1_b=m_ln1_b, ln2_g=m_ln2_g, ln2_b=m_ln2_b)
    small_v = dict(rel_bias=v_rel_bias, b_in=v_b_in, ln1_g=v_ln1_g, ln1_b=v_ln1_b, ln2_g=v_ln2_g, ln2_b=v_ln2_b)

    place = jnp.stack([2 * lax.axis_index("x") + lax.axis_index("y"), lax.axis_index("c")]).astype(jnp.int32)

    weights = _WeightGathers({n: w for n, (w, _, _) in big.items()})
    reductions = _GradReductions(place)
    lnps = [dict(b_in=b_in[l][None], ln1_g=ln1_g[l][None], ln1_b=ln1_b[l][None], ln2_g=ln2_g[l][None],
                 ln2_b=ln2_b[l][None]) for l in range(DEPTH)]
    loss_part, dx, _, smalls, d_rel_bias = _forward_backward(x[0], loss_target[0], rel_bias, weights, lnps,
                                                             reductions)
    loss = lax.psum(loss_part, ("x", "y", "c"))
    grads = _unpack_shard_grads(reductions.results())

    small_g = dict(rel_bias=d_rel_bias)
    for n in SMALL_NAMES[1:]:
        small_g[n] = jnp.concatenate([smalls[l][n] for l in range(DEPTH)], axis=0)
    small_g = _unpack_small(_all_reduce_small(_pack_small(small_g)), small_w)

    delta, new_m, new_v = {}, {}, {}
    for n, (w, m, v) in big.items():
        turn = (lambda t: jnp.swapaxes(t, 1, 2)) if n in COLUMN_SHARDED else (lambda t: t)
        two_d = lambda t: t.reshape(-1, t.shape[-1])
        g = grads[n]
        d_, m_, v_ = _adamw(two_d(turn(w)), two_d(g), two_d(turn(m)), two_d(turn(v)), name=f"adamw_{n}")
        grads[n], delta[n], new_m[n], new_v[n] = (turn(t.reshape(g.shape)) for t in (g, d_, m_, v_))
    d_, m_, v_ = _adamw(_pack_small(small_w), _pack_small(small_g), _pack_small(small_m), _pack_small(small_v),
                        name="adamw_small")
    delta.update(_unpack_small(d_, small_w))
    new_m.update(_unpack_small(m_, small_w))
    new_v.update(_unpack_small(v_, small_w))
    grads.update(small_g)

    order = ("rel_bias", "w_in", "b_in", "w_attn_proj", "w_ret_proj", "w_out", "ln1_g", "ln1_b", "w_ffn_gate",
             "w_ffn_up", "w_ffn_down", "ln2_g", "ln2_b")
    return (loss, dx[None], *[grads[n] for n in order], *[delta[n] for n in order], *[new_m[n] for n in order],
            *[new_v[n] for n in order])
```

```python
import functools

import numpy as np
import jax
import jax.numpy as jnp
from jax import lax
from jax.experimental import pallas as pl
from jax.experimental.pallas import tpu as pltpu
from jax.experimental.pallas import tpu_sc as plsc

F32 = jnp.float32
MXU_DTYPE = jnp.bfloat16

DEPTH = 2
D_MODEL = 1024
HEAD_DIM = 64
ATTN_GROUPS = ((128, 1), (512, 4), (2048, 16))
N_GROUPS = len(ATTN_GROUPS)
HEADS_PER_GROUP = 6
N_ATTN_HEADS = 18
ATTN_WIDTH = 1152
ATTN_OUT = 384
NUM_BUCKETS = 32
MAX_DISTANCE = 2048
RET_HEADS = 4
RET_QK = 256
RET_V = 512
RET_CHUNK = 128
ROPE_BASE = 10000.0
D_FF = 2816
IN_COLS = 11648
ALPHA = (2 * DEPTH) ** 0.25
LN_EPS = 1e-5
GN_EPS = 1e-5
ADAM_LR, ADAM_B1, ADAM_B2, ADAM_EPS, ADAM_WD, ADAM_STEP = 0.001, 0.9, 0.999, 1e-08, 0.01, 10

BLK = 128
NEG = -1e30
N_CHIPS = 4
VMEM_LIMIT = 48 * 1024 * 1024

SECTIONS = tuple(
    (f"a{g}", tuple((k * ATTN_WIDTH + g * ATTN_OUT, ATTN_OUT) for k in range(3)), True, dil)
    for g, (_, dil) in enumerate(ATTN_GROUPS)
) + (
    ("bq", ((3456, 1024),), False, 1),
    ("bk", ((4480, 1024),), False, 1),
    ("c", ((5504, 2048),), True, 1),
    ("d", ((7552, 2048),), False, 1),
    ("e", ((9600, 2048),), False, 1),
)
DX_PARTS = (SECTIONS[:5], SECTIONS[5:])
GROUP_WIDTH = 3 * ATTN_OUT
LANES = 128


def _section_rows(w, runs):
    axis = 1 if w.shape[0] == 1 else 0
    parts = [lax.slice_in_dim(w, first, first + width, axis=axis) for first, width in runs]
    return parts[0] if len(parts) == 1 else jnp.concatenate(parts, axis=axis)


def _unsection(pieces, axis):
    runs = []
    for name, sec_runs, _, _ in SECTIONS:
        off = 0
        for first, width in sec_runs:
            runs.append((first, lax.slice_in_dim(pieces[name], off, off + width, axis=axis)))
            off += width
    return jnp.concatenate([p for _, p in sorted(runs, key=lambda t: t[0])], axis=axis)
PACK_GROUPS = ((("w_in", 2912),),
               (("w_attn_proj", 96), ("w_ret_proj", 512), ("w_out", 256), ("w_ffn_gate", 704), ("w_ffn_up", 704),
                ("w_ffn_down", 704)))
IN_GROUP, REST_GROUP = 0, 1
COLUMN_SHARDED = ("w_in", "w_attn_proj", "w_ffn_gate", "w_ffn_up")

NN = ((1,), (0,))
NT = ((1,), (1,))
TN = ((0,), (0,))


def _dot(a, b, dims):
    return lax.dot_general(a.astype(MXU_DTYPE), b.astype(MXU_DTYPE), (dims, ((), ())),
                           preferred_element_type=F32)


def _pick(n, prefs):
    for p in prefs:
        if n % p == 0:
            return p
    raise ValueError(f"no tile for {n} among {prefs}")


def _row_tile(n, most=512, unit=16):
    return max(t for t in range(unit, most + 1, unit) if n % t == 0)


TOKEN_TILES = (1024, 512, 256, 128)
FEATURE_TILES = (1152, 1024, 1408, 384, 256, 128)


def _params(n_axes, limit=VMEM_LIMIT):
    return pltpu.CompilerParams(dimension_semantics=("arbitrary",) * n_axes, vmem_limit_bytes=limit)


def _resident(a):
    return pl.BlockSpec(a.shape, lambda i: (0,) * a.ndim, pipeline_mode=pl.Buffered(1))


def _sigmoid(x):
    return 1.0 / (1.0 + jnp.exp(-x))


def _norm_rows(u, eps):
    mu = jnp.mean(u, axis=-1, keepdims=True)
    xc = u - mu
    var = jnp.mean(xc * xc, axis=-1, keepdims=True)
    rstd = lax.rsqrt(var + eps)
    return xc * rstd, rstd


def _norm_rows_bwd(dxh, xh, rstd):
    c1 = jnp.mean(dxh, axis=-1, keepdims=True)
    c2 = jnp.mean(dxh * xh, axis=-1, keepdims=True)
    return rstd * (dxh - c1 - xh * c2)


def _phase_to_tokens(src_ref, scr, dil, width):
    n = scr.shape[1] // dil
    for r in range(dil):
        for cb in range(width // LANES):
            col = r * width + cb * LANES
            scr.at[cb][pl.ds(r, n, stride=dil), :] = src_ref[:, col:col + LANES].astype(F32)


def _tokens_to_phase(scr, dst_ref, dil, width):
    n = scr.shape[1] // dil
    for r in range(dil):
        for cb in range(width // LANES):
            col = r * width + cb * LANES
            dst_ref[:, col:col + LANES] = scr.at[cb][pl.ds(r, n, stride=dil), :].astype(dst_ref.dtype)


def _blocks_to_cols(scr):
    return jnp.concatenate([scr[cb] for cb in range(scr.shape[0])], axis=1)


def _cols_to_blocks(val, scr):
    for cb in range(scr.shape[0]):
        scr[cb] = val[:, cb * LANES:(cb + 1) * LANES]


def _matmul(a, b, mode, *, name, out_dtype=F32, bias=None, addend=None, colsum=False, a_phase=1, out_phase=1):
    d_a, d_o = a_phase, out_phase
    if mode == "tn":
        kd, m = a.shape[0] * d_a, a.shape[1] // d_a
        n = b.shape[1]
        tm, tn_ = (m if d_a > 1 else _pick(m, FEATURE_TILES)), _pick(n, FEATURE_TILES)
        tk = _pick(kd, TOKEN_TILES if d_a > 1 else (2048,) + TOKEN_TILES)
        a_spec = (pl.BlockSpec((tk // d_a, d_a * m), lambda i, j, k: (k, 0)) if d_a > 1 else
                  pl.BlockSpec((tk, tm), lambda i, j, k: (k, i)))
        b_spec = pl.BlockSpec((tk, tn_), lambda i, j, k: (k, j))
        feat, tok = m, tk
    else:
        m, kd = a.shape[0] * d_a, a.shape[1] // d_a
        n = b.shape[0] if mode == "nt" else b.shape[1]
        tm, tn_ = _pick(m, TOKEN_TILES), (n if d_o > 1 else _pick(n, FEATURE_TILES))
        tk = kd if d_a > 1 else _pick(kd, FEATURE_TILES)
        a_spec = (pl.BlockSpec((tm // d_a, d_a * kd), lambda i, j, k: (i, 0)) if d_a > 1 else
                  pl.BlockSpec((tm, tk), lambda i, j, k: (i, k)))
        if mode == "nt":
            b_spec = pl.BlockSpec((tn_, tk), lambda i, j, k: (j, k))
        else:
            b_spec = pl.BlockSpec((tk, tn_), lambda i, j, k: (k, j))
        feat, tok = kd, tm
    dims = {"nn": NN, "nt": NT, "tn": TN}[mode]
    nk = kd // tk
    has_bias, has_add = bias is not None, addend is not None
    assert not colsum or mode == "tn"
    assert d_o == 1 or (mode == "nt" and not has_add)

    def body(*refs):
        it = iter(refs)
        a_ref, b_ref = next(it), next(it)
        bias_ref = next(it) if has_bias else None
        add_ref = next(it) if has_add else None
        o_ref = next(it)
        cs_ref = next(it) if colsum else None
        acc_ref = next(it)
        a_scr = next(it) if d_a > 1 else None
        o_scr = next(it) if d_o > 1 else None
        j, k = pl.program_id(1), pl.program_id(2)

        @pl.when(k == 0)
        def _():
            if has_add:
                acc_ref[...] = add_ref[...].astype(F32)
            else:
                acc_ref[...] = jnp.zeros_like(acc_ref)

        if d_a > 1:
            _phase_to_tokens(a_ref, a_scr, d_a, feat)
            av = _blocks_to_cols(a_scr)
        else:
            av = a_ref[...]
        acc_ref[...] += _dot(av, b_ref[...], dims)
        if colsum:
            @pl.when(jnp.logical_and(j == 0, k == 0))
            def _():
                cs_ref[...] = jnp.zeros_like(cs_ref)

            @pl.when(j == 0)
            def _():
                cs_ref[...] += jnp.sum(av.astype(F32), axis=0, keepdims=True)

        @pl.when(k == nk - 1)
        def _():
            r = acc_ref[...]
            if has_bias:
                r = r + bias_ref[...]
            if d_o > 1:
                _cols_to_blocks(r, o_scr)
                _tokens_to_phase(o_scr, o_ref, d_o, n)
            else:
                o_ref[...] = r.astype(out_dtype)

    in_specs, args = [a_spec, b_spec], [a, b]
    if has_bias:
        in_specs.append(pl.BlockSpec((1, tn_), lambda i, j, k: (0, j)))
        args.append(bias)
    if has_add:
        in_specs.append(pl.BlockSpec((tm, tn_), lambda i, j, k: (i, j)))
        args.append(addend)
    if d_o > 1:
        out_shape = [jax.ShapeDtypeStruct((m // d_o, d_o * n), out_dtype)]
        out_specs = [pl.BlockSpec((tm // d_o, d_o * n), lambda i, j, k: (i, 0))]
    else:
        out_shape = [jax.ShapeDtypeStruct((m, n), out_dtype)]
        out_specs = [pl.BlockSpec((tm, tn_), lambda i, j, k: (i, j))]
    if colsum:
        out_shape.append(jax.ShapeDtypeStruct((1, m), F32))
        out_specs.append(pl.BlockSpec((1, tm), lambda i, j, k: (0, i)))
    scratch = [pltpu.VMEM((tm, tn_), F32)]
    if d_a > 1:
        scratch.append(pltpu.VMEM((feat // LANES, tok, LANES), F32))
    if d_o > 1:
        scratch.append(pltpu.VMEM((n // LANES, tm, LANES), F32))
    res = pl.pallas_call(
        body, out_shape=out_shape, grid=(m // tm, n // tn_, nk), in_specs=in_specs, out_specs=out_specs,
        scratch_shapes=scratch, compiler_params=_params(3), name=name,
    )(*args)
    return res if colsum else res[0]


def _in_proj(x_b, w_rows, bias_cols, sections, name):
    s = x_b.shape[0]
    tm = 512
    widths = [sum(w for _, w in runs) for _, runs, _, _ in sections]
    dils = [dil for _, _, _, dil in sections]
    offs = [sum(widths[:i]) for i in range(len(widths))]
    widest_phased = max([w for w, d in zip(widths, dils) if d > 1], default=0)

    def body(x_ref, w_ref, b_ref, *rest):
        o_refs = rest[:len(sections)]
        scr = rest[len(sections)] if widest_phased else None
        xv = x_ref[...]
        for o_ref, width, dil, off in zip(o_refs, widths, dils, offs):
            r = _dot(xv, w_ref[off:off + width, :], NT) + b_ref[:, off:off + width]
            if dil > 1:
                _cols_to_blocks(r, scr)
                _tokens_to_phase(scr, o_ref, dil, width)
            else:
                o_ref[...] = r.astype(o_ref.dtype)

    out_shape = [jax.ShapeDtypeStruct((s // d, d * w), MXU_DTYPE if narrow else F32)
                 for w, d, (_, _, narrow, _) in zip(widths, dils, sections)]
    out_specs = [pl.BlockSpec((tm // d, d * w), lambda i: (i, 0)) for w, d in zip(widths, dils)]
    scratch = [pltpu.VMEM((widest_phased // LANES, tm, LANES), F32)] if widest_phased else []
    return pl.pallas_call(
        body, out_shape=out_shape, grid=(s // tm,),
        in_specs=[pl.BlockSpec((tm, D_MODEL), lambda i: (i, 0)), _resident(w_rows), _resident(bias_cols)],
        out_specs=out_specs, scratch_shapes=scratch, compiler_params=_params(1), name=name,
    )(x_b, w_rows, bias_cols)


def _dx_in_proj(addend, dzs, w_rows, sections, name):
    s = addend.shape[0]
    tm = 512
    widths = [sum(w for _, w in runs) for _, runs, _, _ in sections]
    dils = [dil for _, _, _, dil in sections]
    offs = [sum(widths[:i]) for i in range(len(widths))]
    widest_phased = max([w for w, d in zip(widths, dils) if d > 1], default=0)

    def body(*refs):
        add_ref, w_ref, o_ref = refs[0], refs[1], refs[2 + len(dzs)]
        dz_refs = refs[2:2 + len(dzs)]
        scr = refs[3 + len(dzs)] if widest_phased else None
        acc = add_ref[...]
        for dz_ref, width, dil, off in zip(dz_refs, widths, dils, offs):
            if dil > 1:
                _phase_to_tokens(dz_ref, scr, dil, width)
                av = _blocks_to_cols(scr)
            else:
                av = dz_ref[...]
            acc = acc + _dot(av, w_ref[off:off + width, :], NN)
        o_ref[...] = acc

    row = pl.BlockSpec((tm, D_MODEL), lambda i: (i, 0))
    dz_specs = [pl.BlockSpec((tm // d, d * w), lambda i: (i, 0)) for w, d in zip(widths, dils)]
    scratch = [pltpu.VMEM((widest_phased // LANES, tm, LANES), F32)] if widest_phased else []
    return pl.pallas_call(
        body, out_shape=jax.ShapeDtypeStruct((s, D_MODEL), F32), grid=(s // tm,),
        in_specs=[row, _resident(w_rows)] + dz_specs, out_specs=row, scratch_shapes=scratch,
        compiler_params=_params(1), name=name,
    )(addend, w_rows, *dzs)


def _t5_bucket(dist):
    max_exact = NUM_BUCKETS // 2
    large = max_exact + (np.log(np.maximum(dist, max_exact) / max_exact)
                         / np.log(MAX_DISTANCE / max_exact) * (NUM_BUCKETS - max_exact)).astype(np.int32)
    large = np.minimum(large, NUM_BUCKETS - 1)
    return np.where(dist < max_exact, dist, large).astype(np.int32)


def _bucket_maps():
    qi = np.arange(BLK)[:, None]
    kj = np.arange(2 * BLK)[None, :]
    rel = np.clip(qi + BLK - kj, 0, BLK)
    return jnp.asarray(np.stack([_t5_bucket(rel * d) for _, d in ATTN_GROUPS]))


def _bias_tiles(rel_bias, bmaps):
    def body(tab_ref, bm_ref, o_ref):
        h = pl.program_id(0)
        bm = bm_ref[0]
        acc = jnp.zeros((BLK, 2 * BLK), F32)
        for b in range(NUM_BUCKETS):
            acc = jnp.where(bm == b, tab_ref[b, h], acc)
        o_ref[0] = acc

    return pl.pallas_call(
        body, out_shape=jax.ShapeDtypeStruct((N_ATTN_HEADS, BLK, 2 * BLK), F32), grid=(N_ATTN_HEADS,),
        in_specs=[pl.BlockSpec(memory_space=pltpu.SMEM),
                  pl.BlockSpec((1, BLK, 2 * BLK), lambda h: (h // HEADS_PER_GROUP, 0, 0))],
        out_specs=pl.BlockSpec((1, BLK, 2 * BLK), lambda h: (h, 0, 0)),
        compiler_params=_params(1), name="bias_tiles",
    )(rel_bias, bmaps)


def _bias_tiles_bwd(dbias_layers, bmaps):
    nl = len(dbias_layers)

    def body(*refs):
        bm = refs[nl][0]
        o_ref = refs[nl + 1]
        x = refs[0][0]
        for r in refs[1:nl]:
            x = x + r[0]
        lane = lax.broadcasted_iota(jnp.int32, (1, BLK), 1)
        row = jnp.zeros((1, BLK), F32)
        for b in range(NUM_BUCKETS):
            s = jnp.sum(jnp.where(bm == b, x, 0.0), axis=1, keepdims=True)
            s = jnp.sum(s, axis=0, keepdims=True)
            row = jnp.where(lane == b, s, row)
        o_ref[0] = row

    tile = pl.BlockSpec((1, BLK, 2 * BLK), lambda h: (h, 0, 0))
    out = pl.pallas_call(
        body, out_shape=jax.ShapeDtypeStruct((N_ATTN_HEADS, 1, BLK), F32), grid=(N_ATTN_HEADS,),
        in_specs=[tile] * nl + [pl.BlockSpec((1, BLK, 2 * BLK), lambda h: (h // HEADS_PER_GROUP, 0, 0))],
        out_specs=pl.BlockSpec((1, 1, BLK), lambda h: (h, 0, 0)),
        compiler_params=_params(1), name="bias_tiles_bwd",
    )(*dbias_layers, bmaps)
    return out[:, 0, :NUM_BUCKETS].T


def _pair_masks():
    lane = lax.broadcasted_iota(jnp.int32, (BLK, BLK), 1)
    row2 = lax.broadcasted_iota(jnp.int32, (2 * BLK, BLK), 0)
    lane2 = lax.broadcasted_iota(jnp.int32, (2 * BLK, BLK), 1)
    own = (lane2 // HEAD_DIM) == (row2 // BLK)
    qi = lax.broadcasted_iota(jnp.int32, (2 * BLK, 2 * BLK), 0) & (BLK - 1)
    kj = lax.broadcasted_iota(jnp.int32, (2 * BLK, 2 * BLK), 1)
    band = jnp.logical_and(kj >= qi, kj <= qi + BLK)
    return lane < HEAD_DIM, own, band, kj < BLK


def _pair_scores(q32, kb, bias2, own, band, is_prev, pen):
    qm = jnp.where(own, jnp.concatenate([q32, q32], axis=0), 0.0)
    s = _dot(qm, kb, NT) * (HEAD_DIM ** -0.5) + bias2
    if pen is not None:
        s = s + jnp.where(is_prev, pen, 0.0)
    return jnp.where(band, s, NEG), qm


HEAD_PAIRS = HEADS_PER_GROUP // 2


def _attn_fwd(qkv, bias, g, dil):
    rows = qkv.shape[0]
    nb = rows // BLK
    rb = 2 if nb % 2 == 0 else 1
    cur = pl.BlockSpec((rb * BLK, GROUP_WIDTH), lambda r, n: (n, r))
    prev = pl.BlockSpec((BLK, GROUP_WIDTH), lambda r, n: (jnp.maximum(rb * n - 1, 0), r))

    def body(c_ref, p_ref, b_ref, o_ref, l_ref):
        n = pl.program_id(1)
        pen0 = jnp.where(n > 0, 0.0, NEG)
        first, own, band, is_prev = _pair_masks()
        o_rows, l_rows = [], []
        for t in range(rb):
            rows_t = slice(t * BLK, (t + 1) * BLK)
            o_parts, l_parts = [], []
            for hp in range(HEAD_PAIRS):
                qc, kc, vc = (slice(k * ATTN_OUT + hp * BLK, k * ATTN_OUT + (hp + 1) * BLK) for k in range(3))
                if t == 0:
                    kp, vp, pen = p_ref[:, kc], p_ref[:, vc], pen0
                else:
                    before = slice((t - 1) * BLK, t * BLK)
                    kp, vp, pen = c_ref[before, kc], c_ref[before, vc], None
                kb = jnp.concatenate([kp, c_ref[rows_t, kc]], axis=0)
                vb = jnp.concatenate([vp, c_ref[rows_t, vc]], axis=0)
                bias2 = jnp.concatenate([b_ref[2 * hp], b_ref[2 * hp + 1]], axis=0)
                s, _ = _pair_scores(c_ref[rows_t, qc].astype(F32), kb, bias2, own, band, is_prev, pen)
                m = jnp.max(s, axis=1, keepdims=True)
                p = jnp.exp(s - m)
                l = jnp.sum(p, axis=1, keepdims=True)
                o2 = _dot(p * (1.0 / l), vb, NN)
                lse2 = m + jnp.log(l)
                o_parts.append(jnp.where(first, o2[:BLK], o2[BLK:]))
                l_parts.append(jnp.where(first, lse2[:BLK], lse2[BLK:]))
            o_rows.append(jnp.concatenate(o_parts, axis=1))
            l_rows.append(jnp.concatenate(l_parts, axis=1))
        o_ref[...] = jnp.concatenate(o_rows, axis=0)
        l_ref[...] = jnp.concatenate(l_rows, axis=0)

    out_spec = pl.BlockSpec((rb * BLK, ATTN_OUT), lambda r, n: (n, r))
    shape = jax.ShapeDtypeStruct((rows, dil * ATTN_OUT), F32)
    return pl.pallas_call(
        body, out_shape=[shape, shape], grid=(dil, nb // rb),
        in_specs=[cur, prev, pl.BlockSpec((HEADS_PER_GROUP, BLK, 2 * BLK), lambda r, n: (g, 0, 0))],
        out_specs=[out_spec, out_spec], compiler_params=_params(2), name=f"attn_fwd_g{g}",
    )(qkv, qkv, bias)


def _attn_bwd(qkv, bias, dy, ya, lt, g, dil):
    rows = qkv.shape[0]
    nb = rows // BLK
    cur = lambda w: pl.BlockSpec((BLK, w), lambda r, n: (jnp.minimum(n, nb - 1), r))
    prev = pl.BlockSpec((BLK, GROUP_WIDTH), lambda r, n: (jnp.clip(n - 1, 0, nb - 1), r))

    def body(c_ref, p_ref, b_ref, dy_ref, ya_ref, lt_ref, dz_ref, db_ref, cq_ref, ck_ref, cv_ref):
        r, n = pl.program_id(0), pl.program_id(1)

        @pl.when(jnp.logical_and(r == 0, n == 0))
        def _():
            db_ref[...] = jnp.zeros_like(db_ref)

        @pl.when(n == 0)
        def _():
            cq_ref[...] = jnp.zeros_like(cq_ref)
            ck_ref[...] = jnp.zeros_like(ck_ref)
            cv_ref[...] = jnp.zeros_like(cv_ref)

        @pl.when(n < nb)
        def _():
            pen = jnp.where(n > 0, 0.0, NEG)
            first, own, band, is_prev = _pair_masks()
            second = jnp.logical_not(first)
            scale = HEAD_DIM ** -0.5
            parts = {k: [] for k in ("dq", "dkp", "dkc", "dvp", "dvc")}
            db_parts = []
            for hp in range(HEAD_PAIRS):
                cols = slice(hp * BLK, (hp + 1) * BLK)
                qc, kc, vc = (slice(k * ATTN_OUT + hp * BLK, k * ATTN_OUT + (hp + 1) * BLK) for k in range(3))
                kb = jnp.concatenate([p_ref[:, kc], c_ref[:, kc]], axis=0)
                vb = jnp.concatenate([p_ref[:, vc], c_ref[:, vc]], axis=0)
                bias2 = jnp.concatenate([b_ref[2 * hp], b_ref[2 * hp + 1]], axis=0)
                dy_, lt_ = dy_ref[:, cols], lt_ref[:, cols]
                dyy = dy_ * ya_ref[:, cols]
                per_head = lambda t, red, fill: jnp.concatenate(
                    [red(jnp.where(first, t, fill), axis=1, keepdims=True),
                     red(jnp.where(second, t, fill), axis=1, keepdims=True)], axis=0)
                lse2 = per_head(lt_, jnp.max, NEG)
                delta2 = per_head(dyy, jnp.sum, 0.0)
                s, qm = _pair_scores(c_ref[:, qc].astype(F32), kb, bias2, own, band, is_prev, pen)
                p = jnp.exp(s - lse2)
                dym = jnp.where(own, jnp.concatenate([dy_, dy_], axis=0), 0.0)
                ds = p * (_dot(dym, vb, NT) - delta2)
                db_parts += [ds[:BLK], ds[BLK:]]
                dq2 = _dot(ds, kb, NN) * scale
                dkb = _dot(ds, qm, TN) * scale
                dvb = _dot(p, dym, TN)
                for k, val in (("dq", jnp.where(first, dq2[:BLK], dq2[BLK:])), ("dkp", dkb[:BLK]),
                               ("dkc", dkb[BLK:]), ("dvp", dvb[:BLK]), ("dvc", dvb[BLK:])):
                    parts[k].append(val)
            wide = {k: jnp.concatenate(val, axis=1) for k, val in parts.items()}
            db_ref[...] += jnp.stack(db_parts, axis=0)
            dz_ref[...] = jnp.concatenate([cq_ref[...], ck_ref[...] + wide["dkp"], cv_ref[...] + wide["dvp"]],
                                          axis=1).astype(dz_ref.dtype)
            cq_ref[...] = wide["dq"]
            ck_ref[...] = wide["dkc"]
            cv_ref[...] = wide["dvc"]

        @pl.when(n == nb)
        def _():
            dz_ref[...] = jnp.concatenate([cq_ref[...], ck_ref[...], cv_ref[...]], axis=1).astype(dz_ref.dtype)

    late = pl.BlockSpec((BLK, GROUP_WIDTH), lambda r, n: (jnp.maximum(n - 1, 0), r))
    carry = pltpu.VMEM((BLK, ATTN_OUT), F32)
    return pl.pallas_call(
        body,
        out_shape=[jax.ShapeDtypeStruct((rows, dil * GROUP_WIDTH), MXU_DTYPE),
                   jax.ShapeDtypeStruct((HEADS_PER_GROUP, BLK, 2 * BLK), F32)],
        grid=(dil, nb + 1),
        in_specs=[cur(GROUP_WIDTH), prev, pl.BlockSpec((HEADS_PER_GROUP, BLK, 2 * BLK), lambda r, n: (g, 0, 0)),
                  cur(ATTN_OUT), cur(ATTN_OUT), cur(ATTN_OUT)],
        out_specs=[late, pl.BlockSpec((HEADS_PER_GROUP, BLK, 2 * BLK), lambda r, n: (0, 0, 0))],
        scratch_shapes=[carry, carry, carry], compiler_params=_params(2), name=f"attn_bwd_g{g}",
    )(qkv, qkv, bias, dy, ya, lt)


def _attn_combine(o_list, l_list):
    dils = [d for _, d in ATTN_GROUPS]
    s = o_list[0].shape[0] * dils[0]
    tr = _pick(s, (512, 256, 128))
    n_g = len(dils)

    def body(*refs):
        o_refs, l_refs = refs[:n_g], refs[n_g:2 * n_g]
        yb_ref = refs[2 * n_g]
        y_refs, lt_refs = refs[2 * n_g + 1:3 * n_g + 1], refs[3 * n_g + 1:4 * n_g + 1]
        scr_o, scr_l = refs[4 * n_g + 1], refs[4 * n_g + 2]
        os_, ls_ = [], []
        for g in range(n_g):
            if dils[g] == 1:
                os_.append(o_refs[g][...])
                ls_.append(l_refs[g][...])
            else:
                _phase_to_tokens(o_refs[g], scr_o, dils[g], ATTN_OUT)
                _phase_to_tokens(l_refs[g], scr_l, dils[g], ATTN_OUT)
                os_.append(_blocks_to_cols(scr_o))
                ls_.append(_blocks_to_cols(scr_l))
        mx = functools.reduce(jnp.maximum, ls_)
        es = [jnp.exp(a - mx) for a in ls_]
        den = functools.reduce(lambda a, b: a + b, es)
        inv = 1.0 / den
        y = functools.reduce(lambda a, b: a + b, [(e * inv) * o for e, o in zip(es, os_)])
        lt = mx + jnp.log(den)
        yb_ref[...] = y.astype(yb_ref.dtype)
        _cols_to_blocks(y, scr_o)
        _cols_to_blocks(lt, scr_l)
        for g in range(n_g):
            if dils[g] == 1:
                y_refs[g][...] = y
                lt_refs[g][...] = lt
            else:
                _tokens_to_phase(scr_o, y_refs[g], dils[g], ATTN_OUT)
                _tokens_to_phase(scr_l, lt_refs[g], dils[g], ATTN_OUT)

    view = lambda d: pl.BlockSpec((tr // d, d * ATTN_OUT), lambda i: (i, 0))
    views = [view(d) for d in dils]
    f = lambda d: jax.ShapeDtypeStruct((s // d, d * ATTN_OUT), F32)
    fs = [f(d) for d in dils]
    scratch = pltpu.VMEM((ATTN_OUT // LANES, tr, LANES), F32)
    res = pl.pallas_call(
        body, out_shape=[jax.ShapeDtypeStruct((s, ATTN_OUT), MXU_DTYPE)] + fs + fs, grid=(s // tr,),
        in_specs=views + views, out_specs=[view(1)] + views + views, scratch_shapes=[scratch, scratch],
        compiler_params=_params(1), name="attn_combine",
    )(*o_list, *l_list)
    return res[0], list(res[1:1 + n_g]), list(res[1 + n_g:])


def _ret_consts(s):
    half = RET_QK // 2
    pos = jnp.arange(s, dtype=F32)
    inv_freq = ROPE_BASE ** (-jnp.arange(half, dtype=F32) / half)
    ang = pos[:, None] * inv_freq[None]
    log_g = jnp.log(1.0 - 2.0 ** (-5.0 - jnp.arange(RET_HEADS, dtype=F32)))
    n = jnp.arange(RET_CHUNK, dtype=F32)
    diff = n[:, None] - n[None, :]
    dmask = jnp.where(diff >= 0, jnp.exp(log_g[:, None, None] * jnp.maximum(diff, 0.0)), 0.0)
    qdec = jnp.exp(log_g[:, None] * (n + 1.0))
    kdec = jnp.exp(log_g[:, None] * (RET_CHUNK - 1.0 - n))
    cdec = jnp.exp(log_g * RET_CHUNK)
    wide = (RET_HEADS, RET_CHUNK, RET_QK)
    return dict(cos=jnp.cos(ang), sin=jnp.sin(ang), dmask=dmask,
                qdec=jnp.broadcast_to(qdec[:, :, None], wide), kdec=jnp.broadcast_to(kdec[:, :, None], wide),
                cdec=cdec)


def _rot(t, cs, sn):
    half = RET_QK // 2
    t1, t2 = t[:, :half], t[:, half:]
    return jnp.concatenate([t1 * cs - t2 * sn, t1 * sn + t2 * cs], axis=1)


def _rot_bwd(d, cs, sn):
    half = RET_QK // 2
    d1, d2 = d[:, :half], d[:, half:]
    return jnp.concatenate([d1 * cs + d2 * sn, d2 * cs - d1 * sn], axis=1)


def _ret_fwd(zq, zk, zv, zg, rc):
    s = zq.shape[0]
    nc = s // RET_CHUNK
    c = RET_CHUNK

    def body(cd_ref, q_ref, k_ref, v_ref, g_ref, cos_ref, sin_ref, dm_ref, qd_ref, kd_ref,
             yb_ref, o_ref, st_ref, state):
        n = pl.program_id(0)

        @pl.when(n == 0)
        def _():
            state[...] = jnp.zeros_like(state)

        cs, sn = cos_ref[...], sin_ref[...]
        for h in range(RET_HEADS):
            qs, vs = slice(h * RET_QK, (h + 1) * RET_QK), slice(h * RET_V, (h + 1) * RET_V)
            qr = _rot(q_ref[:, qs], cs, sn)
            kr = _rot(k_ref[:, qs], cs, sn) * (RET_QK ** -0.5)
            v = v_ref[:, vs]
            st = state[h]
            st_ref[h, 0] = st.astype(st_ref.dtype)
            sc = _dot(qr, kr, NT) * dm_ref[h]
            o = _dot(sc, v, NN) + _dot(qr * qd_ref[h], st, NN)
            state[h] = st * cd_ref[h] + _dot(kr * kd_ref[h], v, TN)
            o_ref[:, vs] = o
            xh, _ = _norm_rows(o, GN_EPS)
            gv = g_ref[:, vs]
            yb_ref[:, vs] = (gv * _sigmoid(gv) * xh).astype(yb_ref.dtype)

    row = lambda w: pl.BlockSpec((c, w), lambda n: (n, 0))
    const = lambda a: pl.BlockSpec(a.shape, lambda n: (0, 0, 0))
    return pl.pallas_call(
        body,
        out_shape=[jax.ShapeDtypeStruct((s, RET_HEADS * RET_V), MXU_DTYPE),
                   jax.ShapeDtypeStruct((s, RET_HEADS * RET_V), F32),
                   jax.ShapeDtypeStruct((RET_HEADS, nc, RET_QK, RET_V), MXU_DTYPE)],
        grid=(nc,),
        in_specs=[pl.BlockSpec(memory_space=pltpu.SMEM), row(RET_HEADS * RET_QK), row(RET_HEADS * RET_QK),
                  row(RET_HEADS * RET_V), row(RET_HEADS * RET_V), row(RET_QK // 2), row(RET_QK // 2),
                  const(rc["dmask"]), const(rc["qdec"]), const(rc["kdec"])],
        out_specs=[row(RET_HEADS * RET_V), row(RET_HEADS * RET_V),
                   pl.BlockSpec((RET_HEADS, 1, RET_QK, RET_V), lambda n: (0, n, 0, 0))],
        scratch_shapes=[pltpu.VMEM((RET_HEADS, RET_QK, RET_V), F32)],
        compiler_params=_params(1), name="ret_fwd",
    )(rc["cdec"], zq, zk, zv, zg, rc["cos"], rc["sin"], rc["dmask"], rc["qdec"], rc["kdec"])


def _ret_bwd(zq, zk, zv, zg, o_ret, states, dyb, rc):
    s = zq.shape[0]
    nc = s // RET_CHUNK
    c = RET_CHUNK

    def body(cd_ref, q_ref, k_ref, v_ref, g_ref, o_ref, st_ref, dy_ref, cos_ref, sin_ref, dm_ref, qd_ref,
             kd_ref, dq_ref, dk_ref, dv_ref, dg_ref, dstate):
        n = pl.program_id(0)

        @pl.when(n == 0)
        def _():
            dstate[...] = jnp.zeros_like(dstate)

        cs, sn = cos_ref[...], sin_ref[...]
        for h in range(RET_HEADS):
            qs, vs = slice(h * RET_QK, (h + 1) * RET_QK), slice(h * RET_V, (h + 1) * RET_V)
            qr = _rot(q_ref[:, qs], cs, sn)
            kr = _rot(k_ref[:, qs], cs, sn) * (RET_QK ** -0.5)
            v = v_ref[:, vs]
            st = st_ref[h, 0]
            dm, qd, kd = dm_ref[h], qd_ref[h], kd_ref[h]
            xh, rstd = _norm_rows(o_ref[:, vs], GN_EPS)
            gv, dy = g_ref[:, vs], dy_ref[:, vs]
            sg = _sigmoid(gv)
            dg_ref[:, vs] = (dy * xh * (sg * (1.0 + gv * (1.0 - sg)))).astype(dg_ref.dtype)
            do = _norm_rows_bwd(dy * (gv * sg), xh, rstd)
            ds_next = dstate[h]
            sc = _dot(qr, kr, NT) * dm
            da = _dot(do, v, NT) * dm
            dv = _dot(sc, do, TN) + _dot(kr * kd, ds_next, NN)
            dqr = _dot(da, kr, NN) + _dot(do, st, NT) * qd
            dkr = _dot(da, qr, TN) + _dot(v, ds_next, NT) * kd
            dstate[h] = ds_next * cd_ref[h] + _dot(qr * qd, do, TN)
            dq_ref[:, qs] = _rot_bwd(dqr, cs, sn).astype(dq_ref.dtype)
            dk_ref[:, qs] = _rot_bwd(dkr * (RET_QK ** -0.5), cs, sn).astype(dk_ref.dtype)
            dv_ref[:, vs] = dv.astype(dv_ref.dtype)

    row = lambda w: pl.BlockSpec((c, w), lambda n: (nc - 1 - n, 0))
    const = lambda a: pl.BlockSpec(a.shape, lambda n: (0, 0, 0))
    qk_w, v_w = RET_HEADS * RET_QK, RET_HEADS * RET_V
    g_qk, g_v = jax.ShapeDtypeStruct((s, qk_w), MXU_DTYPE), jax.ShapeDtypeStruct((s, v_w), MXU_DTYPE)
    return pl.pallas_call(
        body,
        out_shape=[g_qk, g_qk, g_v, g_v],
        grid=(nc,),
        in_specs=[pl.BlockSpec(memory_space=pltpu.SMEM), row(qk_w), row(qk_w), row(v_w), row(v_w), row(v_w),
                  pl.BlockSpec((RET_HEADS, 1, RET_QK, RET_V), lambda n: (0, nc - 1 - n, 0, 0)), row(v_w),
                  row(RET_QK // 2), row(RET_QK // 2), const(rc["dmask"]), const(rc["qdec"]), const(rc["kdec"])],
        out_specs=[row(qk_w), row(qk_w), row(v_w), row(v_w)],
        scratch_shapes=[pltpu.VMEM((RET_HEADS, RET_QK, RET_V), F32)],
        compiler_params=_params(1), name="ret_bwd",
    )(rc["cdec"], zq, zk, zv, zg, o_ret, states, dyb, rc["cos"], rc["sin"], rc["dmask"], rc["qdec"],
      rc["kdec"])


ROW_TILE = 256


def _mix_fwd(x, ya_b, yb_b, ze, wap_t, wrp, wout, gam, bet):
    s = x.shape[0]
    tm = ROW_TILE

    def body(x_ref, ya_ref, yb_ref, ze_ref, wap_ref, wrp_ref, wout_ref, g_ref, b_ref,
             x1_ref, u1_ref, a1_ref, a2_ref, mg_ref, x1b_ref):
        a1 = _dot(ya_ref[...], wap_ref[...], NT)
        a2 = _dot(yb_ref[...], wrp_ref[...], NN)
        ze_ = ze_ref[...]
        merged = _sigmoid(ze_[:, :D_MODEL]) * a1 + _sigmoid(ze_[:, D_MODEL:]) * a2
        u = ALPHA * x_ref[...] + _dot(merged, wout_ref[...], NN)
        xh, _ = _norm_rows(u, LN_EPS)
        x1 = xh * g_ref[...] + b_ref[...]
        x1_ref[...] = x1
        x1b_ref[...] = x1.astype(x1b_ref.dtype)
        u1_ref[...] = u
        a1_ref[...] = a1
        a2_ref[...] = a2
        mg_ref[...] = merged.astype(mg_ref.dtype)

    row = lambda w: pl.BlockSpec((tm, w), lambda i: (i, 0))
    full = _resident
    f = jax.ShapeDtypeStruct((s, D_MODEL), F32)
    m = jax.ShapeDtypeStruct((s, D_MODEL), MXU_DTYPE)
    return pl.pallas_call(
        body, out_shape=[f, f, f, f, m, m], grid=(s // tm,),
        in_specs=[row(D_MODEL), row(ATTN_OUT), row(RET_HEADS * RET_V), row(2 * D_MODEL), full(wap_t), full(wrp),
                  full(wout), full(gam), full(bet)],
        out_specs=[row(D_MODEL)] * 6, compiler_params=_params(1), name="mix_fwd",
    )(x, ya_b, yb_b, ze, wap_t, wrp, wout, gam, bet)


def _mix_bwd(dx1, u1, a1, a2, ze, wap_t, wrp, wout, gam):
    s = dx1.shape[0]
    tm = ROW_TILE

    dils = [d for _, d in ATTN_GROUPS]

    def body(dx_ref, u_ref, a1_ref, a2_ref, ze_ref, wap_ref, wrp_ref, wout_ref, g_ref,
             dres_ref, du_ref, da1_ref, da2_ref, dze_ref, dyb_ref, dgam_ref, dbet_ref, *rest):
        dya_refs, dya_scr = rest[:len(dils)], rest[len(dils)]
        @pl.when(pl.program_id(0) == 0)
        def _():
            dgam_ref[...] = jnp.zeros_like(dgam_ref)
            dbet_ref[...] = jnp.zeros_like(dbet_ref)

        dx = dx_ref[...]
        xh, rstd = _norm_rows(u_ref[...], LN_EPS)
        dgam_ref[...] += jnp.sum(dx * xh, axis=0, keepdims=True)
        dbet_ref[...] += jnp.sum(dx, axis=0, keepdims=True)
        du = _norm_rows_bwd(dx * g_ref[...], xh, rstd)
        dres_ref[...] = ALPHA * du
        du_ref[...] = du.astype(du_ref.dtype)
        dm = _dot(du, wout_ref[...], NT)
        ze_ = ze_ref[...]
        sa, sb = _sigmoid(ze_[:, :D_MODEL]), _sigmoid(ze_[:, D_MODEL:])
        da1, da2 = dm * sa, dm * sb
        dze_ref[...] = jnp.concatenate([dm * a1_ref[...] * (sa * (1.0 - sa)),
                                        dm * a2_ref[...] * (sb * (1.0 - sb))], axis=1).astype(dze_ref.dtype)
        da1_ref[...] = da1.astype(da1_ref.dtype)
        da2_ref[...] = da2.astype(da2_ref.dtype)
        dyb_ref[...] = _dot(da2, wrp_ref[...], NT)
        dya = _dot(da1, wap_ref[...], NN)
        _cols_to_blocks(dya, dya_scr)
        for g, d in enumerate(dils):
            if d == 1:
                dya_refs[g][...] = dya
            else:
                _tokens_to_phase(dya_scr, dya_refs[g], d, ATTN_OUT)

    row = lambda w: pl.BlockSpec((tm, w), lambda i: (i, 0))
    full = _resident
    vec = pl.BlockSpec((1, D_MODEL), lambda i: (0, 0))
    f = lambda w: jax.ShapeDtypeStruct((s, w), F32)
    m = lambda w: jax.ShapeDtypeStruct((s, w), MXU_DTYPE)
    v = jax.ShapeDtypeStruct((1, D_MODEL), F32)
    res = pl.pallas_call(
        body,
        out_shape=[f(D_MODEL), m(D_MODEL), m(D_MODEL), m(D_MODEL), m(2 * D_MODEL), f(RET_HEADS * RET_V), v, v]
        + [jax.ShapeDtypeStruct((s // d, d * ATTN_OUT), F32) for d in dils],
        grid=(s // tm,),
        in_specs=[row(D_MODEL)] * 4 + [row(2 * D_MODEL), full(wap_t), full(wrp), full(wout), full(gam)],
        out_specs=[row(D_MODEL)] * 4 + [row(2 * D_MODEL), row(RET_HEADS * RET_V), vec, vec]
        + [pl.BlockSpec((tm // d, d * ATTN_OUT), lambda i: (i, 0)) for d in dils],
        scratch_shapes=[pltpu.VMEM((ATTN_OUT // LANES, tm, LANES), F32)],
        compiler_params=_params(1), name="mix_bwd",
    )(dx1, u1, a1, a2, ze, wap_t, wrp, wout, gam)
    return (*res[:8], list(res[8:]))


FF_CHUNK = 1408


def _ffn_fwd(x1, wg_t, wu_t, wd, gam, bet):
    s = x1.shape[0]
    tm, fc = ROW_TILE, FF_CHUNK

    def body(x_ref, wg_ref, wu_ref, wd_ref, g_ref, b_ref, x2_ref, u2_ref, a_ref, b_out_ref, h_ref, x2b_ref):
        xv = x_ref[...]
        u = ALPHA * xv
        for f0 in range(0, D_FF, fc):
            ch = slice(f0, f0 + fc)
            a = _dot(xv, wg_ref[ch, :], NT)
            b = _dot(xv, wu_ref[ch, :], NT)
            hid = a * _sigmoid(a) * b
            u = u + _dot(hid, wd_ref[ch, :], NN)
            a_ref[:, ch] = a
            b_out_ref[:, ch] = b
            h_ref[:, ch] = hid.astype(h_ref.dtype)
        xh, _ = _norm_rows(u, LN_EPS)
        u2_ref[...] = u
        x2 = xh * g_ref[...] + b_ref[...]
        x2_ref[...] = x2
        x2b_ref[...] = x2.astype(x2b_ref.dtype)

    row = lambda w: pl.BlockSpec((tm, w), lambda i: (i, 0))
    f = lambda w: jax.ShapeDtypeStruct((s, w), F32)
    m = lambda w: jax.ShapeDtypeStruct((s, w), MXU_DTYPE)
    return pl.pallas_call(
        body, out_shape=[f(D_MODEL), f(D_MODEL), f(D_FF), f(D_FF), m(D_FF), m(D_MODEL)],
        grid=(s // tm,),
        in_specs=[row(D_MODEL), _resident(wg_t), _resident(wu_t), _resident(wd), _resident(gam), _resident(bet)],
        out_specs=[row(D_MODEL), row(D_MODEL), row(D_FF), row(D_FF), row(D_FF), row(D_MODEL)],
        compiler_params=_params(1), name="ffn_fwd",
    )(x1, wg_t, wu_t, wd, gam, bet)


def _ffn_bwd(dx2, u2, fa, fb, wg_t, wu_t, wd, gam):
    s = dx2.shape[0]
    tm, fc = ROW_TILE, FF_CHUNK

    def body(dx_ref, u_ref, a_ref, b_ref, wg_ref, wu_ref, wd_ref, g_ref,
             dx1_ref, du_ref, da_ref, db_ref, dgam_ref, dbet_ref):
        @pl.when(pl.program_id(0) == 0)
        def _():
            dgam_ref[...] = jnp.zeros_like(dgam_ref)
            dbet_ref[...] = jnp.zeros_like(dbet_ref)

        dx = dx_ref[...]
        xh, rstd = _norm_rows(u_ref[...], LN_EPS)
        dgam_ref[...] += jnp.sum(dx * xh, axis=0, keepdims=True)
        dbet_ref[...] += jnp.sum(dx, axis=0, keepdims=True)
        du = _norm_rows_bwd(dx * g_ref[...], xh, rstd)
        du_ref[...] = du.astype(du_ref.dtype)
        acc = ALPHA * du
        for f0 in range(0, D_FF, fc):
            ch = slice(f0, f0 + fc)
            dh = _dot(du, wd_ref[ch, :], NT)
            a, b = a_ref[:, ch], b_ref[:, ch]
            sg = _sigmoid(a)
            da = dh * b * (sg * (1.0 + a * (1.0 - sg)))
            db = dh * (a * sg)
            acc = acc + _dot(da, wg_ref[ch, :], NN) + _dot(db, wu_ref[ch, :], NN)
            da_ref[:, ch] = da.astype(da_ref.dtype)
            db_ref[:, ch] = db.astype(db_ref.dtype)
        dx1_ref[...] = acc

    row = lambda w: pl.BlockSpec((tm, w), lambda i: (i, 0))
    vec = pl.BlockSpec((1, D_MODEL), lambda i: (0, 0))
    v = jax.ShapeDtypeStruct((1, D_MODEL), F32)
    return pl.pallas_call(
        body,
        out_shape=[jax.ShapeDtypeStruct((s, D_MODEL), F32), jax.ShapeDtypeStruct((s, D_MODEL), MXU_DTYPE),
                   jax.ShapeDtypeStruct((s, D_FF), MXU_DTYPE), jax.ShapeDtypeStruct((s, D_FF), MXU_DTYPE), v, v],
        grid=(s // tm,),
        in_specs=[row(D_MODEL), row(D_MODEL), row(D_FF), row(D_FF), _resident(wg_t), _resident(wu_t),
                  _resident(wd), _resident(gam)],
        out_specs=[row(D_MODEL), row(D_MODEL), row(D_FF), row(D_FF), vec, vec],
        compiler_params=_params(1), name="ffn_bwd",
    )(dx2, u2, fa, fb, wg_t, wu_t, wd, gam)


def _loss_head(y, target):
    s = y.shape[0]
    tr = _pick(s, (512, 256, 128))

    def body(y_ref, t_ref, dy_ref, l_ref):
        @pl.when(pl.program_id(0) == 0)
        def _():
            l_ref[...] = jnp.zeros_like(l_ref)

        e = y_ref[...] - t_ref[...]
        dy_ref[...] = e * (1.0 / D_MODEL)
        part = jnp.sum(jnp.sum(e * e, axis=1, keepdims=True), axis=0, keepdims=True)
        l_ref[...] += part * (0.5 / D_MODEL)

    spec = pl.BlockSpec((tr, D_MODEL), lambda i: (i, 0))
    dy, part = pl.pallas_call(
        body, out_shape=[jax.ShapeDtypeStruct((s, D_MODEL), F32), jax.ShapeDtypeStruct((8, BLK), F32)],
        grid=(s // tr,), in_specs=[spec, spec], out_specs=[spec, pl.BlockSpec((8, BLK), lambda i: (0, 0))],
        compiler_params=_params(1), name="loss_head",
    )(y, target)
    return dy, part[0, 0]


def _adamw(w, g, m, v, name):
    rows, cols = w.shape
    budget = 1 << 20
    cands = [t for t in range(8, rows + 1, 8) if rows % t == 0 and t * cols * 4 <= budget]
    tr = max(cands) if cands else rows

    def body(w_ref, g_ref, m_ref, v_ref, d_ref, nm_ref, nv_ref):
        gv = g_ref[...]
        mn = ADAM_B1 * m_ref[...] + (1.0 - ADAM_B1) * gv
        vn = ADAM_B2 * v_ref[...] + (1.0 - ADAM_B2) * (gv * gv)
        m_hat = mn / (1.0 - ADAM_B1 ** ADAM_STEP)
        v_hat = vn / (1.0 - ADAM_B2 ** ADAM_STEP)
        d_ref[...] = -ADAM_LR * (m_hat / (jnp.sqrt(v_hat) + ADAM_EPS) + ADAM_WD * w_ref[...])
        nm_ref[...] = mn
        nv_ref[...] = vn

    spec = pl.BlockSpec((tr, cols), lambda i: (i, 0))
    shape = jax.ShapeDtypeStruct((rows, cols), F32)
    return pl.pallas_call(
        body, out_shape=[shape, shape, shape], grid=(rows // tr,), in_specs=[spec] * 4, out_specs=[spec] * 3,
        compiler_params=_params(1), name=name,
    )(w, g, m, v)


MESH_ID = pl.DeviceIdType.MESH
ANY = pl.BlockSpec(memory_space=pl.ANY)


def _place():
    x, y, c = lax.axis_index("x"), lax.axis_index("y"), lax.axis_index("c")
    other_chips = [(1 - x, y), (x, 1 - y), (1 - x, 1 - y)]
    return x, y, c, other_chips


def _chip_no(chip):
    return 2 * chip[0] + chip[1]


def _gather_exchange(f_ref, g_ref, send_sems, recv_sems, hr):
    x, y, c, chips = _place()
    sibling = (x, y, 1 - c)

    def piece(chip, half):
        return g_ref.at[_chip_no(chip), pl.ds(half * hr, hr), :]

    def copy(k, src, dst, to):
        return pltpu.make_async_remote_copy(src_ref=src, dst_ref=dst, send_sem=send_sems.at[k],
                                            recv_sem=recv_sems.at[k], device_id=to, device_id_type=MESH_ID)

    own = copy(6, f_ref, g_ref.at[_chip_no((x, y))], sibling)
    own.start()
    my_half = f_ref.at[pl.ds(c * hr, hr), :]
    first = [copy(k, my_half, piece((x, y), c), (*chip, c)) for k, chip in enumerate(chips)]
    for cp in first:
        cp.start()
    passed = [copy(3 + k, piece(chip, c), piece(chip, c), sibling) for k, chip in enumerate(chips)]
    for k, chip in enumerate(chips):
        copy(k, my_half, piece(chip, c), (*chip, c)).wait_recv()
        passed[k].start()
    for k, chip in enumerate(chips):
        copy(3 + k, my_half, piece(chip, 1 - c), sibling).wait_recv()
    for cp in first + passed:
        cp.wait_send()
    own.wait()


GATHER_SEMS = (pltpu.SemaphoreType.DMA((7,)), pltpu.SemaphoreType.DMA((7,)))


HBM = pltpu.MemorySpace.HBM
ONE_SEM_PAIR = (pltpu.SemaphoreType.DMA, pltpu.SemaphoreType.DMA)


def _sibling_of_me():
    x, y, c, _ = _place()
    return [(x, y, 1 - c)]


def _same_core_of_other_chips():
    x, y, c, chips = _place()
    return [(*chip, c) for chip in chips]


def _on_sequencer(name, collective_id, sems, peers, exchange):
    @pl.kernel(mesh=plsc.ScalarSubcoreMesh(axis_name="sequencer", num_cores=1), name=name, scratch_types=sems,
               compiler_params=pltpu.CompilerParams(collective_id=collective_id))
    def launch(*sem_refs):
        barrier = pltpu.get_barrier_semaphore()
        devices = peers()
        for peer in devices:
            pl.semaphore_signal(barrier, inc=1, device_id=peer, device_id_type=MESH_ID)
        pl.semaphore_wait(barrier, len(devices))
        exchange(*sem_refs)

    launch()


def _all_gather_rows(flat, behind=None):
    r, cols = flat.shape
    out = jax.ShapeDtypeStruct((N_CHIPS, r, cols), flat.dtype)
    if behind is not None:
        f_ref, g_ref = jax.new_ref(flat, memory_space=HBM), jax.empty_ref(out, memory_space=HBM)
        _on_sequencer(f"all_gather_weights_behind_{behind[1]}", behind[0], GATHER_SEMS,
                      lambda: _sibling_of_me() + _same_core_of_other_chips(),
                      lambda s, r_: _gather_exchange(f_ref, g_ref, s, r_, r // 2))
        return g_ref[...]

    def body(f_ref, g_ref, send_sems, recv_sems):
        _gather_exchange(f_ref, g_ref, send_sems, recv_sems, r // 2)

    return pl.pallas_call(body, out_shape=out, in_specs=[ANY], out_specs=ANY, scratch_shapes=list(GATHER_SEMS),
                          name="all_gather_weights")(flat)


def _swap_halves(gall, behind=None):
    _, r, cols = gall.shape
    hr = r // 2
    out = jax.ShapeDtypeStruct((N_CHIPS, hr, cols), gall.dtype)

    def exchange(g_ref, a_ref, send_sem, recv_sem):
        x, y, c, _ = _place()
        cp = pltpu.make_async_remote_copy(src_ref=g_ref.at[:, pl.ds((1 - c) * hr, hr), :], dst_ref=a_ref,
                                          send_sem=send_sem, recv_sem=recv_sem, device_id=(x, y, 1 - c),
                                          device_id_type=MESH_ID)
        cp.start()
        cp.wait()

    if behind is not None:
        g_ref, a_ref = jax.new_ref(gall, memory_space=HBM), jax.empty_ref(out, memory_space=HBM)
        _on_sequencer(f"grad_swap_halves_behind_{behind[1]}", behind[0], ONE_SEM_PAIR, _sibling_of_me,
                      lambda s, r_: exchange(g_ref, a_ref, s, r_))
        return a_ref[...]

    def body(g_ref, a_ref, send_sem, recv_sem):
        exchange(g_ref, a_ref, send_sem, recv_sem)

    return pl.pallas_call(body, out_shape=out, in_specs=[ANY], out_specs=ANY, scratch_shapes=list(ONE_SEM_PAIR),
                          name="grad_swap_halves")(gall)


def _scatter_to_chips(p, behind=None):
    _, hr, cols = p.shape
    out = jax.ShapeDtypeStruct((3, hr, cols), p.dtype)
    sems = (pltpu.SemaphoreType.DMA((3,)), pltpu.SemaphoreType.DMA((3,)))

    def exchange(p_ref, b_ref, send_sems, recv_sems):
        x, y, c, chips = _place()
        cps = [pltpu.make_async_remote_copy(src_ref=p_ref.at[_chip_no(chip)], dst_ref=b_ref.at[k],
                                            send_sem=send_sems.at[k], recv_sem=recv_sems.at[k],
                                            device_id=(*chip, c), device_id_type=MESH_ID)
               for k, chip in enumerate(chips)]
        for cp in cps:
            cp.start()
        for cp in cps:
            cp.wait()

    if behind is not None:
        p_ref, b_ref = jax.new_ref(p, memory_space=HBM), jax.empty_ref(out, memory_space=HBM)
        _on_sequencer(f"grad_scatter_chips_behind_{behind[1]}", behind[0], sems, _same_core_of_other_chips,
                      lambda s, r_: exchange(p_ref, b_ref, s, r_))
        return b_ref[...]

    def body(p_ref, b_ref, send_sems, recv_sems):
        exchange(p_ref, b_ref, send_sems, recv_sems)

    return pl.pallas_call(body, out_shape=out, in_specs=[ANY], out_specs=ANY, scratch_shapes=list(sems),
                          name="grad_scatter_chips")(p)


def _share_with_sibling(full, behind=None):
    r, cols = full.shape
    hr = r // 2

    def exchange(in_ref, out_ref, send_sem, recv_sem):
        x, y, c, _ = _place()
        cp = pltpu.make_async_remote_copy(src_ref=in_ref.at[pl.ds(c * hr, hr), :],
                                          dst_ref=out_ref.at[pl.ds(c * hr, hr), :],
                                          send_sem=send_sem, recv_sem=recv_sem, device_id=(x, y, 1 - c),
                                          device_id_type=MESH_ID)
        cp.start()
        cp.wait()

    if behind is not None:
        full_ref = jax.new_ref(full, memory_space=HBM)
        _on_sequencer(f"grad_share_sibling_behind_{behind[1]}", behind[0], ONE_SEM_PAIR, _sibling_of_me,
                      lambda s, r_: exchange(full_ref, full_ref, s, r_))
        return full_ref[...]

    def body(in_ref, out_ref, send_sem, recv_sem):
        exchange(in_ref, out_ref, send_sem, recv_sem)

    return pl.pallas_call(
        body, out_shape=jax.ShapeDtypeStruct((r, cols), full.dtype), in_specs=[ANY], out_specs=ANY,
        input_output_aliases={0: 0}, scratch_shapes=list(ONE_SEM_PAIR), name="grad_share_sibling",
    )(full)


def _all_reduce_small(v):
    r, cols = v.shape
    n_dev = 8

    def body(x_ref, out_ref, gat_ref, send_sems, recv_sems, local_sem):
        x, y, c, chips = _place()
        me, sibling = (x, y, c), (x, y, 1 - c)

        def slot(px, py, pc):
            return gat_ref.at[4 * px + 2 * py + pc]

        def copy(k, block, to, src=None):
            return pltpu.make_async_remote_copy(src_ref=slot(*block) if src is None else src, dst_ref=slot(*block),
                                                send_sem=send_sems.at[k], recv_sem=recv_sems.at[k], device_id=to,
                                                device_id_type=MESH_ID)

        mine = pltpu.make_async_copy(x_ref, slot(*me), local_sem)
        mine.start()
        first = [copy(0, me, sibling, src=x_ref)]
        first += [copy(1 + k, me, (*chip, c), src=x_ref) for k, chip in enumerate(chips)]
        for cp in first:
            cp.start()
        passed = [copy(4 + k, (*chip, c), sibling) for k, chip in enumerate(chips)]
        for k, chip in enumerate(chips):
            copy(1 + k, (*chip, c), me).wait_recv()
            passed[k].start()
        copy(0, sibling, me).wait_recv()
        for k, chip in enumerate(chips):
            copy(4 + k, (*chip, 1 - c), me).wait_recv()
        for cp in first + passed:
            cp.wait_send()
        mine.wait()
        acc = gat_ref[0]
        for d in range(1, n_dev):
            acc = acc + gat_ref[d]
        out_ref[...] = acc

    vmem = pl.BlockSpec(memory_space=pltpu.VMEM)
    return pl.pallas_call(
        body, out_shape=jax.ShapeDtypeStruct((r, cols), v.dtype), in_specs=[vmem], out_specs=vmem,
        scratch_shapes=[pltpu.VMEM((n_dev, r, cols), v.dtype), pltpu.SemaphoreType.DMA((7,)),
                        pltpu.SemaphoreType.DMA((7,)), pltpu.SemaphoreType.DMA],
        name="all_reduce_small",
    )(v)


WIRE_DTYPE = jnp.bfloat16


def _add_own_half(gall, got, place):
    _, r, cols = gall.shape
    hr = r // 2
    tr = _row_tile(hr)
    g4 = gall.reshape(N_CHIPS, 2, hr, cols)

    def body(s_ref, g_ref, a_ref, o_ref):
        o_ref[...] = (g_ref[0] + a_ref[...]).astype(o_ref.dtype)

    return pl.pallas_call(
        body, out_shape=jax.ShapeDtypeStruct((N_CHIPS, hr, cols), WIRE_DTYPE),
        grid_spec=pltpu.PrefetchScalarGridSpec(
            num_scalar_prefetch=1, grid=(N_CHIPS, hr // tr),
            in_specs=[pl.BlockSpec((1, 1, tr, cols), lambda j, i, s: (j, s[1], i, 0)),
                      pl.BlockSpec((1, tr, cols), lambda j, i, s: (j, i, 0))],
            out_specs=pl.BlockSpec((1, tr, cols), lambda j, i, s: (j, i, 0))),
        compiler_params=_params(2), name="grad_add_halves",
    )(place, g4, got)


def _add_chip_parts(gall, got, parts, place):
    _, r, cols = gall.shape
    hr = r // 2
    tr = _row_tile(hr)
    g4 = gall.reshape(N_CHIPS, 2, hr, cols)
    nt = hr // tr

    def body(s_ref, g_ref, a_ref, b0_ref, b1_ref, b2_ref, o_ref):
        own = g_ref[0, 0] + a_ref[0]
        o_ref[...] = ((own + b0_ref[0].astype(F32)) + b1_ref[0].astype(F32)) + b2_ref[0].astype(F32)

    part = lambda k: pl.BlockSpec((1, tr, cols), lambda i, s: (k, i, 0))
    return pl.pallas_call(
        body, out_shape=jax.ShapeDtypeStruct((r, cols), F32),
        grid_spec=pltpu.PrefetchScalarGridSpec(
            num_scalar_prefetch=1, grid=(nt,),
            in_specs=[pl.BlockSpec((1, 1, tr, cols), lambda i, s: (s[0], s[1], i, 0)),
                      pl.BlockSpec((1, tr, cols), lambda i, s: (s[0], i, 0)), part(0), part(1), part(2)],
            out_specs=pl.BlockSpec((tr, cols), lambda i, s: (s[1] * nt + i, 0))),
        compiler_params=_params(1), name="grad_add_chips",
    )(place, g4, got, parts, parts, parts)


def _pack_shards(weights, l, group, dtype):
    parts = []
    for name, _ in group:
        w = weights[name][l]
        parts.append((w.T if name in COLUMN_SHARDED else w).reshape(-1, D_MODEL).astype(dtype))
    return parts[0] if len(parts) == 1 else jnp.concatenate(parts, axis=0)


def _unpack_gathered(gathered, group):
    w, off = {}, 0
    for name, rows in group:
        blk = gathered[:, off:off + rows]
        if name == "w_attn_proj":
            w[name] = blk.reshape(N_CHIPS * 256, ATTN_OUT)
        else:
            w[name] = blk.reshape(N_CHIPS * rows, D_MODEL)
        off += rows
    return w


def _pack_full_grads(g, group):
    parts = [g[name].reshape(N_CHIPS, rows, D_MODEL) for name, rows in group]
    return parts[0] if len(parts) == 1 else jnp.concatenate(parts, axis=1)


def _unpack_shard_grads(fulls):
    out = {}
    for gi, group in enumerate(PACK_GROUPS):
        off = 0
        for name, rows in group:
            blks = [fulls[(l, gi)][off:off + rows] for l in range(DEPTH)]
            out[name] = jnp.stack([b.reshape(256, ATTN_OUT) if name == "w_attn_proj" else b for b in blks])
            off += rows
    return out


class _StaticWeights:
    def __init__(self, layers):
        self.layers = layers

    def group(self, l, gi):
        return self.layers[l]

    def layer(self, l):
        return self.layers[l]


class _WeightGathers:
    def __init__(self, weights):
        self.flats = {(l, gi): _pack_shards(weights, l, g, MXU_DTYPE)
                      for l in range(DEPTH) for gi, g in enumerate(PACK_GROUPS)}
        self.order = sorted(self.flats)
        self.landed, self.unpacked, self.issued = {}, {}, 0
        self._issue()

    def _issue(self, after=None):
        key = self.order[self.issued]
        flat = self.flats[key]
        if self.issued == 0:
            self.landed[key] = _all_gather_rows(flat)
        else:
            prev = self.order[self.issued - 1]
            ties = (self.landed[prev], flat) if after is None else (self.landed[prev], flat, after)
            tied = lax.optimization_barrier(ties)
            self.landed[prev], flat = tied[0], tied[1]
            after = tied[2] if after is not None else None
            self.landed[key] = _all_gather_rows(flat, behind=(1, f"l{key[0]}g{key[1]}"))
        self.issued += 1
        return after

    def prefetch(self, after):
        return self._issue(after) if self.issued < len(self.order) else after

    def group(self, l, gi):
        key = (l, gi)
        if key not in self.unpacked:
            self.unpacked[key] = _unpack_gathered(self.landed[key], PACK_GROUPS[gi])
        return self.unpacked[key]

    def layer(self, l):
        w = {}
        for gi in range(len(PACK_GROUPS)):
            w.update(self.group(l, gi))
        return w


class _GradReduction:
    LAST = 3

    def __init__(self, gw, group, place, behind):
        self.gall, self.place, self.behind, self.stage = _pack_full_grads(gw, group), place, behind, 0
        self.pending = None

    def _how(self, k):
        return None if self.behind is None else (self.behind[0][k], self.behind[1])

    def advance(self, value=None):
        if self.stage > self.LAST:
            return value
        if self.stage > 0 and value is not None:
            value, self.pending = lax.optimization_barrier((value, self.pending))
        if self.stage == 0:
            self.pending = _swap_halves(self.gall, self._how(0))
        elif self.stage == 1:
            self.got = self.pending
            self.pending = _scatter_to_chips(_add_own_half(self.gall, self.got, self.place), self._how(1))
        elif self.stage == 2:
            self.pending = _share_with_sibling(
                _add_chip_parts(self.gall, self.got, self.pending, self.place), self._how(2))
        self.stage += 1
        return value

    def result(self):
        while self.stage < self.LAST:
            self.advance()
        return self.pending


class _GradReductions:
    def __init__(self, place):
        self.place, self.all = place, {}

    def start(self, l, gi, gw):
        hidden = (l, gi) != (0, IN_GROUP)
        slot = len(self.all) % 2
        how = ((2 + 3 * slot, 3 + 3 * slot, 4 + 3 * slot), f"l{l}g{gi}") if hidden else None
        red = _GradReduction(gw, PACK_GROUPS[gi], self.place, how)
        if hidden:
            red.advance()
        self.all[(l, gi)] = red

    def tick(self, value):
        for red in self.all.values():
            if red.behind is not None:
                value = red.advance(value)
        return value

    def results(self):
        return {key: red.result() for key, red in self.all.items()}


def _layer_fwd(x, x_b, weights, l, lnp, bias, rc, prefetch):
    tag = f"l{l}"
    w_in = weights.group(l, IN_GROUP)["w_in"]
    z = {}
    for part, secs in enumerate(DX_PARTS):
        runs = tuple(run for _, sec_runs, _, _ in secs for run in sec_runs)
        outs = _in_proj(x_b, _section_rows(w_in, runs), _section_rows(lnp["b_in"], runs), secs,
                        name=f"in_proj_{part}_{tag}")
        z.update({name: o for (name, _, _, _), o in zip(secs, outs)})
        if part == 0:
            z["a0"] = prefetch(z["a0"])
    w = weights.group(l, REST_GROUP)
    o_list, l_list = [], []
    for g, (_, dil) in enumerate(ATTN_GROUPS):
        o, lse = _attn_fwd(z[f"a{g}"], bias, g, dil)
        o_list.append(o)
        l_list.append(lse)
    ya_b, ya_views, lt_views = _attn_combine(o_list, l_list)
    yb_b, o_ret, states = _ret_fwd(z["bq"], z["bk"], z["c"], z["d"], rc)
    x1, u1, a1, a2, mg_b, x1_b = _mix_fwd(x, ya_b, yb_b, z["e"], w["w_attn_proj"], w["w_ret_proj"], w["w_out"],
                                          lnp["ln1_g"], lnp["ln1_b"])
    x1 = prefetch(x1)
    x2, u2, fa, fb, h_b, x2_b = _ffn_fwd(x1, w["w_ffn_gate"], w["w_ffn_up"], w["w_ffn_down"], lnp["ln2_g"],
                                         lnp["ln2_b"])
    saved = dict(x_b=x_b, z=z, ya_views=ya_views, ya_b=ya_b, lt_views=lt_views, yb_b=yb_b, o_ret=o_ret, states=states,
                 x1_b=x1_b, u1=u1, a1=a1, a2=a2, mg_b=mg_b, u2=u2, fa=fa, fb=fb, h_b=h_b)
    return x2, x2_b, saved


def _layer_bwd(dx2, w, l, lnp, sv, bias, rc, reductions=None):
    s = dx2.shape[0]
    tag = f"l{l}"
    z = sv["z"]
    step = reductions.tick if reductions is not None else (lambda v: v)
    dx1, du2_b, da_b, db_b, dg2, dbt2 = _ffn_bwd(dx2, sv["u2"], sv["fa"], sv["fb"], w["w_ffn_gate"], w["w_ffn_up"],
                                                  w["w_ffn_down"], lnp["ln2_g"])
    dx1 = step(dx1)
    gw = {}
    gw["w_ffn_down"] = _matmul(sv["h_b"], du2_b, "tn", name=f"dw_ffn_down_{tag}")
    gw["w_ffn_gate"] = _matmul(da_b, sv["x1_b"], "tn", name=f"dw_ffn_gate_{tag}")
    gw["w_ffn_up"] = _matmul(db_b, sv["x1_b"], "tn", name=f"dw_ffn_up_{tag}")
    dres, du1_b, da1_b, da2_b, dze, dyb, dg1, dbt1, dya_views = _mix_bwd(
        dx1, sv["u1"], sv["a1"], sv["a2"], z["e"], w["w_attn_proj"], w["w_ret_proj"], w["w_out"], lnp["ln1_g"])
    gw["w_out"] = _matmul(sv["mg_b"], du1_b, "tn", name=f"dw_out_{tag}")
    gw["w_attn_proj"] = _matmul(da1_b, sv["ya_b"], "tn", name=f"dw_attn_proj_{tag}")
    gw["w_ret_proj"] = _matmul(sv["yb_b"], da2_b, "tn", name=f"dw_ret_proj_{tag}")
    dyb = step(dyb)
    if reductions is not None:
        reductions.start(l, REST_GROUP, gw)
    dzq, dzk, dzv, dzg = _ret_bwd(z["bq"], z["bk"], z["c"], z["d"], sv["o_ret"], sv["states"], dyb, rc)
    dzq = step(dzq)
    dz = {"bq": dzq, "bk": dzk, "c": dzv, "d": dzg, "e": dze}
    dbias_l = []
    for g, (_, dil) in enumerate(ATTN_GROUPS):
        dz[f"a{g}"], dbg = _attn_bwd(z[f"a{g}"], bias, dya_views[g], sv["ya_views"][g], sv["lt_views"][g], g, dil)
        dbias_l.append(dbg)
    last = f"a{N_GROUPS - 1}"
    dz[last] = step(dz[last])
    dx = dres
    for part, secs in enumerate(DX_PARTS):
        runs = tuple(run for _, sec_runs, _, _ in secs for run in sec_runs)
        dx = _dx_in_proj(dx, [dz[name] for name, _, _, _ in secs], _section_rows(w["w_in"], runs), secs,
                         name=f"dx_in_proj_{part}_{tag}")
    dx = step(dx)
    dw, db = {}, {}
    for name, _, _, dil in SECTIONS:
        dw[name], db[name] = _matmul(dz[name], sv["x_b"], "tn", colsum=True, a_phase=dil,
                                     name=f"dw_in_proj_{name}_{tag}")
    gw["w_in"] = _unsection(dw, axis=0)
    if reductions is not None:
        reductions.start(l, IN_GROUP, gw)
    small = dict(b_in=_unsection(db, axis=1), ln1_g=dg1, ln1_b=dbt1, ln2_g=dg2, ln2_b=dbt2)
    return dx, gw, small, jnp.concatenate(dbias_l, axis=0)


def _forward_backward(x, target, rel_bias, weights, lnps, reductions=None):
    s = x.shape[0]
    bmaps = _bucket_maps()
    bias = _bias_tiles(rel_bias, bmaps)
    rc = _ret_consts(s)
    prefetch = getattr(weights, "prefetch", lambda v: v)
    saved = []
    h, h_b = x, x.astype(MXU_DTYPE)
    for l in range(DEPTH):
        h, h_b, sv = _layer_fwd(h, h_b, weights, l, lnps[l], bias, rc, prefetch)
        saved.append(sv)
    dh, loss_part = _loss_head(h, target)
    gws, smalls, dbiases = [None] * DEPTH, [None] * DEPTH, [None] * DEPTH
    for l in reversed(range(DEPTH)):
        dh, gws[l], smalls[l], dbiases[l] = _layer_bwd(dh, weights.layer(l), l, lnps[l], saved[l], bias, rc,
                                                       reductions)
    d_rel_bias = _bias_tiles_bwd(dbiases, bmaps)
    return loss_part, dh, gws, smalls, d_rel_bias


SMALL_NAMES = ("rel_bias", "b_in", "ln1_g", "ln1_b", "ln2_g", "ln2_b")
SMALL_ROWS = 32


def _pack_small(vals):
    flat = jnp.concatenate([vals[n].reshape(-1) for n in SMALL_NAMES])
    return jnp.pad(flat, (0, SMALL_ROWS * D_MODEL - flat.shape[0])).reshape(SMALL_ROWS, D_MODEL)


def _unpack_small(packed, like):
    flat = packed.reshape(-1)
    out, off = {}, 0
    for n in SMALL_NAMES:
        size = like[n].size
        out[n] = flat[off:off + size].reshape(like[n].shape)
        off += size
    return out


def kernel(x, rel_bias, w_in, b_in, w_attn_proj, w_ret_proj, w_out, ln1_g, ln1_b, w_ffn_gate, w_ffn_up, w_ffn_down, ln2_g, ln2_b, loss_target, m_rel_bias, m_w_in, m_b_in, m_w_attn_proj, m_w_ret_proj, m_w_out, m_ln1_g, m_ln1_b, m_w_ffn_gate, m_w_ffn_up, m_w_ffn_down, m_ln2_g, m_ln2_b, v_rel_bias, v_w_in, v_b_in, v_w_attn_proj, v_w_ret_proj, v_w_out, v_ln1_g, v_ln1_b, v_w_ffn_gate, v_w_ffn_up, v_w_ffn_down, v_ln2_g, v_ln2_b):
    big = dict(w_in=(w_in, m_w_in, v_w_in), w_attn_proj=(w_attn_proj, m_w_attn_proj, v_w_attn_proj),
               w_ret_proj=(w_ret_proj, m_w_ret_proj, v_w_ret_proj), w_out=(w_out, m_w_out, v_w_out),
               w_ffn_gate=(w_ffn_gate, m_w_ffn_gate, v_w_ffn_gate), w_ffn_up=(w_ffn_up, m_w_ffn_up, v_w_ffn_up),
               w_ffn_down=(w_ffn_down, m_w_ffn_down, v_w_ffn_down))
    small_w = dict(rel_bias=rel_bias, b_in=b_in, ln1_g=ln1_g, ln1_b=ln1_b, ln2_g=ln2_g, ln2_b=ln2_b)
    small_m = dict(rel_bias=m_rel_bias, b_in=m_b_in, ln1_g=m_ln1_g, ln1_b=m_ln1_b, ln2_g=m_ln2_g, ln2_b=m_ln2_b)
    small_v = dict(rel_bias=v_rel_bias, b_in=v_b_in, ln1_g=v_ln1_g, ln1_b=v_ln1_b, ln2_g=v_ln2_g, ln2_b=v_ln2_b)

    place = jnp.stack([2 * lax.axis_index("x") + lax.axis_index("y"), lax.axis_index("c")]).astype(jnp.int32)

    weights = _WeightGathers({n: w for n, (w, _, _) in big.items()})
    reductions = _GradReductions(place)
    lnps = [dict(b_in=b_in[l][None], ln1_g=ln1_g[l][None], ln1_b=ln1_b[l][None], ln2_g=ln2_g[l][None],
                 ln2_b=ln2_b[l][None]) for l in range(DEPTH)]
    loss_part, dx, _, smalls, d_rel_bias = _forward_backward(x[0], loss_target[0], rel_bias, weights, lnps,
                                                             reductions)
    loss = lax.psum(loss_part, ("x", "y", "c"))
    grads = _unpack_shard_grads(reductions.results())

    small_g = dict(rel_bias=d_rel_bias)
    for n in SMALL_NAMES[1:]:
        small_g[n] = jnp.concatenate([smalls[l][n] for l in range(DEPTH)], axis=0)
    small_g = _unpack_small(_all_reduce_small(_pack_small(small_g)), small_w)

    delta, new_m, new_v = {}, {}, {}
    for n, (w, m, v) in big.items():
        turn = (lambda t: jnp.swapaxes(t, 1, 2)) if n in COLUMN_SHARDED else (lambda t: t)
        two_d = lambda t: t.reshape(-1, t.shape[-1])
        g = grads[n]
        d_, m_, v_ = _adamw(two_d(turn(w)), two_d(g), two_d(turn(m)), two_d(turn(v)), name=f"adamw_{n}")
        grads[n], delta[n], new_m[n], new_v[n] = (turn(t.reshape(g.shape)) for t in (g, d_, m_, v_))
    d_, m_, v_ = _adamw(_pack_small(small_w), _pack_small(small_g), _pack_small(small_m), _pack_small(small_v),
                        name="adamw_small")
    delta.update(_unpack_small(d_, small_w))
    new_m.update(_unpack_small(m_, small_w))
    new_v.update(_unpack_small(v_, small_w))
    grads.update(small_g)

    order = ("rel_bias", "w_in", "b_in", "w_attn_proj", "w_ret_proj", "w_out", "ln1_g", "ln1_b", "w_ffn_gate",
             "w_ffn_up", "w_ffn_down", "ln2_g", "ln2_b")
    return (loss, dx[None], *[grads[n] for n in order], *[delta[n] for n in order], *[new_m[n] for n in order],
            *[new_v[n] for n in order])
```

```python
import functools

import numpy as np
import jax
import jax.numpy as jnp
from jax import lax
from jax.experimental import pallas as pl
from jax.experimental.pallas import tpu as pltpu
from jax.experimental.pallas import tpu_sc as plsc

F32 = jnp.float32
MXU_DTYPE = jnp.bfloat16

DEPTH = 2
D_MODEL = 1024
HEAD_DIM = 64
ATTN_GROUPS = ((128, 1), (512, 4), (2048, 16))
N_GROUPS = len(ATTN_GROUPS)
HEADS_PER_GROUP = 6
N_ATTN_HEADS = 18
ATTN_WIDTH = 1152
ATTN_OUT = 384
NUM_BUCKETS = 32
MAX_DISTANCE = 2048
RET_HEADS = 4
RET_QK = 256
RET_V = 512
RET_CHUNK = 128
ROPE_BASE = 10000.0
D_FF = 2816
IN_COLS = 11648
ALPHA = (2 * DEPTH) ** 0.25
LN_EPS = 1e-5
GN_EPS = 1e-5
ADAM_LR, ADAM_B1, ADAM_B2, ADAM_EPS, ADAM_WD, ADAM_STEP = 0.001, 0.9, 0.999, 1e-08, 0.01, 10

BLK = 128
NEG = -1e30
N_CHIPS = 4
VMEM_LIMIT = 48 * 1024 * 1024

SECTIONS = tuple(
    (f"a{g}", tuple((k * ATTN_WIDTH + g * ATTN_OUT, ATTN_OUT) for k in range(3)), True, dil)
    for g, (_, dil) in enumerate(ATTN_GROUPS)
) + (
    ("bq", ((3456, 1024),), False, 1),
    ("bk", ((4480, 1024),), False, 1),
    ("c", ((5504, 2048),), True, 1),
    ("d", ((7552, 2048),), False, 1),
    ("e", ((9600, 2048),), False, 1),
)
DX_PARTS = (SECTIONS[:5], SECTIONS[5:])
GROUP_WIDTH = 3 * ATTN_OUT
LANES = 128


def _section_rows(w, runs):
    axis = 1 if w.shape[0] == 1 else 0
    parts = [lax.slice_in_dim(w, first, first + width, axis=axis) for first, width in runs]
    return parts[0] if len(parts) == 1 else jnp.concatenate(parts, axis=axis)


def _unsection(pieces, axis):
    runs = []
    for name, sec_runs, _, _ in SECTIONS:
        off = 0
        for first, width in sec_runs:
            runs.append((first, lax.slice_in_dim(pieces[name], off, off + width, axis=axis)))
            off += width
    return jnp.concatenate([p for _, p in sorted(runs, key=lambda t: t[0])], axis=axis)
PACK_GROUPS = ((("w_in", 2912),),
               (("w_attn_proj", 96), ("w_ret_proj", 512), ("w_out", 256), ("w_ffn_gate", 704), ("w_ffn_up", 704),
                ("w_ffn_down", 704)))
IN_GROUP, REST_GROUP = 0, 1
COLUMN_SHARDED = ("w_in", "w_attn_proj", "w_ffn_gate", "w_ffn_up")

NN = ((1,), (0,))
NT = ((1,), (1,))
TN = ((0,), (0,))


def _dot(a, b, dims):
    return lax.dot_general(a.astype(MXU_DTYPE), b.astype(MXU_DTYPE), (dims, ((), ())),
                           preferred_element_type=F32)


def _pick(n, prefs):
    for p in prefs:
        if n % p == 0:
            return p
    raise ValueError(f"no tile for {n} among {prefs}")


def _row_tile(n, most=512, unit=16):
    return max(t for t in range(unit, most + 1, unit) if n % t == 0)


TOKEN_TILES = (1024, 512, 256, 128)
FEATURE_TILES = (1152, 1024, 1408, 384, 256, 128)


def _params(n_axes, limit=VMEM_LIMIT):
    return pltpu.CompilerParams(dimension_semantics=("arbitrary",) * n_axes, vmem_limit_bytes=limit)


def _resident(a):
    return pl.BlockSpec(a.shape, lambda i: (0,) * a.ndim, pipeline_mode=pl.Buffered(1))


def _sigmoid(x):
    return 1.0 / (1.0 + jnp.exp(-x))


def _norm_rows(u, eps):
    mu = jnp.mean(u, axis=-1, keepdims=True)
    xc = u - mu
    var = jnp.mean(xc * xc, axis=-1, keepdims=True)
    rstd = lax.rsqrt(var + eps)
    return xc * rstd, rstd


def _norm_rows_bwd(dxh, xh, rstd):
    c1 = jnp.mean(dxh, axis=-1, keepdims=True)
    c2 = jnp.mean(dxh * xh, axis=-1, keepdims=True)
    return rstd * (dxh - c1 - xh * c2)


def _phase_to_tokens(src_ref, scr, dil, width):
    n = scr.shape[1] // dil
    for r in range(dil):
        for cb in range(width // LANES):
            col = r * width + cb * LANES
            scr.at[cb][pl.ds(r, n, stride=dil), :] = src_ref[:, col:col + LANES].astype(F32)


def _tokens_to_phase(scr, dst_ref, dil, width):
    n = scr.shape[1] // dil
    for r in range(dil):
        for cb in range(width // LANES):
            col = r * width + cb * LANES
            dst_ref[:, col:col + LANES] = scr.at[cb][pl.ds(r, n, stride=dil), :].astype(dst_ref.dtype)


def _blocks_to_cols(scr):
    return jnp.concatenate([scr[cb] for cb in range(scr.shape[0])], axis=1)


def _cols_to_blocks(val, scr):
    for cb in range(scr.shape[0]):
        scr[cb] = val[:, cb * LANES:(cb + 1) * LANES]


def _matmul(a, b, mode, *, name, out_dtype=F32, bias=None, addend=None, colsum=False, a_phase=1, out_phase=1):
    d_a, d_o = a_phase, out_phase
    if mode == "tn":
        kd, m = a.shape[0] * d_a, a.shape[1] // d_a
        n = b.shape[1]
        tm, tn_ = (m if d_a > 1 else _pick(m, FEATURE_TILES)), _pick(n, FEATURE_TILES)
        tk = _pick(kd, TOKEN_TILES if d_a > 1 else (2048,) + TOKEN_TILES)
        a_spec = (pl.BlockSpec((tk // d_a, d_a * m), lambda i, j, k: (k, 0)) if d_a > 1 else
                  pl.BlockSpec((tk, tm), lambda i, j, k: (k, i)))
        b_spec = pl.BlockSpec((tk, tn_), lambda i, j, k: (k, j))
        feat, tok = m, tk
    else:
        m, kd = a.shape[0] * d_a, a.shape[1] // d_a
        n = b.shape[0] if mode == "nt" else b.shape[1]
        tm, tn_ = _pick(m, TOKEN_TILES), (n if d_o > 1 else _pick(n, FEATURE_TILES))
        tk = kd if d_a > 1 else _pick(kd, FEATURE_TILES)
        a_spec = (pl.BlockSpec((tm // d_a, d_a * kd), lambda i, j, k: (i, 0)) if d_a > 1 else
                  pl.BlockSpec((tm, tk), lambda i, j, k: (i, k)))
        if mode == "nt":
            b_spec = pl.BlockSpec((tn_, tk), lambda i, j, k: (j, k))
        else:
            b_spec = pl.BlockSpec((tk, tn_), lambda i, j, k: (k, j))
        feat, tok = kd, tm
    dims = {"nn": NN, "nt": NT, "tn": TN}[mode]
    nk = kd // tk
    has_bias, has_add = bias is not None, addend is not None
    assert not colsum or mode == "tn"
    assert d_o == 1 or (mode == "nt" and not has_add)

    def body(*refs):
        it = iter(refs)
        a_ref, b_ref = next(it), next(it)
        bias_ref = next(it) if has_bias else None
        add_ref = next(it) if has_add else None
        o_ref = next(it)
        cs_ref = next(it) if colsum else None
        acc_ref = next(it)
        a_scr = next(it) if d_a > 1 else None
        o_scr = next(it) if d_o > 1 else None
        j, k = pl.program_id(1), pl.program_id(2)

        @pl.when(k == 0)
        def _():
            if has_add:
                acc_ref[...] = add_ref[...].astype(F32)
            else:
                acc_ref[...] = jnp.zeros_like(acc_ref)

        if d_a > 1:
            _phase_to_tokens(a_ref, a_scr, d_a, feat)
            av = _blocks_to_cols(a_scr)
        else:
            av = a_ref[...]
        acc_ref[...] += _dot(av, b_ref[...], dims)
        if colsum:
            @pl.when(jnp.logical_and(j == 0, k == 0))
            def _():
                cs_ref[...] = jnp.zeros_like(cs_ref)

            @pl.when(j == 0)
            def _():
                cs_ref[...] += jnp.sum(av.astype(F32), axis=0, keepdims=True)

        @pl.when(k == nk - 1)
        def _():
            r = acc_ref[...]
            if has_bias:
                r = r + bias_ref[...]
            if d_o > 1:
                _cols_to_blocks(r, o_scr)
                _tokens_to_phase(o_scr, o_ref, d_o, n)
            else:
                o_ref[...] = r.astype(out_dtype)

    in_specs, args = [a_spec, b_spec], [a, b]
    if has_bias:
        in_specs.append(pl.BlockSpec((1, tn_), lambda i, j, k: (0, j)))
        args.append(bias)
    if has_add:
        in_specs.append(pl.BlockSpec((tm, tn_), lambda i, j, k: (i, j)))
        args.append(addend)
    if d_o > 1:
        out_shape = [jax.ShapeDtypeStruct((m // d_o, d_o * n), out_dtype)]
        out_specs = [pl.BlockSpec((tm // d_o, d_o * n), lambda i, j, k: (i, 0))]
    else:
        out_shape = [jax.ShapeDtypeStruct((m, n), out_dtype)]
        out_specs = [pl.BlockSpec((tm, tn_), lambda i, j, k: (i, j))]
    if colsum:
        out_shape.append(jax.ShapeDtypeStruct((1, m), F32))
        out_specs.append(pl.BlockSpec((1, tm), lambda i, j, k: (0, i)))
    scratch = [pltpu.VMEM((tm, tn_), F32)]
    if d_a > 1:
        scratch.append(pltpu.VMEM((feat // LANES, tok, LANES), F32))
    if d_o > 1:
        scratch.append(pltpu.VMEM((n // LANES, tm, LANES), F32))
    res = pl.pallas_call(
        body, out_shape=out_shape, grid=(m // tm, n // tn_, nk), in_specs=in_specs, out_specs=out_specs,
        scratch_shapes=scratch, compiler_params=_params(3), name=name,
    )(*args)
    return res if colsum else res[0]


def _in_proj(x_b, w_rows, bias_cols, sections, name):
    s = x_b.shape[0]
    tm = 512
    widths = [sum(w for _, w in runs) for _, runs, _, _ in sections]
    dils = [dil for _, _, _, dil in sections]
    offs = [sum(widths[:i]) for i in range(len(widths))]
    widest_phased = max([w for w, d in zip(widths, dils) if d > 1], default=0)

    def body(x_ref, w_ref, b_ref, *rest):
        o_refs = rest[:len(sections)]
        scr = rest[len(sections)] if widest_phased else None
        xv = x_ref[...]
        for o_ref, width, dil, off in zip(o_refs, widths, dils, offs):
            r = _dot(xv, w_ref[off:off + width, :], NT) + b_ref[:, off:off + width]
            if dil > 1:
                _cols_to_blocks(r, scr)
                _tokens_to_phase(scr, o_ref, dil, width)
            else:
                o_ref[...] = r.astype(o_ref.dtype)

    out_shape = [jax.ShapeDtypeStruct((s // d, d * w), MXU_DTYPE if narrow else F32)
                 for w, d, (_, _, narrow, _) in zip(widths, dils, sections)]
    out_specs = [pl.BlockSpec((tm // d, d * w), lambda i: (i, 0)) for w, d in zip(widths, dils)]
    scratch = [pltpu.VMEM((widest_phased // LANES, tm, LANES), F32)] if widest_phased else []
    return pl.pallas_call(
        body, out_shape=out_shape, grid=(s // tm,),
        in_specs=[pl.BlockSpec((tm, D_MODEL), lambda i: (i, 0)), _resident(w_rows), _resident(bias_cols)],
        out_specs=out_specs, scratch_shapes=scratch, compiler_params=_params(1), name=name,
    )(x_b, w_rows, bias_cols)


def _dx_in_proj(addend, dzs, w_rows, sections, name):
    s = addend.shape[0]
    tm = 512
    widths = [sum(w for _, w in runs) for _, runs, _, _ in sections]
    dils = [dil for _, _, _, dil in sections]
    offs = [sum(widths[:i]) for i in range(len(widths))]
    widest_phased = max([w for w, d in zip(widths, dils) if d > 1], default=0)

    def body(*refs):
        add_ref, w_ref, o_ref = refs[0], refs[1], refs[2 + len(dzs)]
        dz_refs = refs[2:2 + len(dzs)]
        scr = refs[3 + len(dzs)] if widest_phased else None
        acc = add_ref[...]
        for dz_ref, width, dil, off in zip(dz_refs, widths, dils, offs):
            if dil > 1:
                _phase_to_tokens(dz_ref, scr, dil, width)
                av = _blocks_to_cols(scr)
            else:
                av = dz_ref[...]
            acc = acc + _dot(av, w_ref[off:off + width, :], NN)
        o_ref[...] = acc

    row = pl.BlockSpec((tm, D_MODEL), lambda i: (i, 0))
    dz_specs = [pl.BlockSpec((tm // d, d * w), lambda i: (i, 0)) for w, d in zip(widths, dils)]
    scratch = [pltpu.VMEM((widest_phased // LANES, tm, LANES), F32)] if widest_phased else []
    return pl.pallas_call(
        body, out_shape=jax.ShapeDtypeStruct((s, D_MODEL), F32), grid=(s // tm,),
        in_specs=[row, _resident(w_rows)] + dz_specs, out_specs=row, scratch_shapes=scratch,
        compiler_params=_params(1), name=name,
    )(addend, w_rows, *dzs)


def _t5_bucket(dist):
    max_exact = NUM_BUCKETS // 2
    large = max_exact + (np.log(np.maximum(dist, max_exact) / max_exact)
                         / np.log(MAX_DISTANCE / max_exact) * (NUM_BUCKETS - max_exact)).astype(np.int32)
    large = np.minimum(large, NUM_BUCKETS - 1)
    return np.where(dist < max_exact, dist, large).astype(np.int32)


def _bucket_maps():
    qi = np.arange(BLK)[:, None]
    kj = np.arange(2 * BLK)[None, :]
    rel = np.clip(qi + BLK - kj, 0, BLK)
    return jnp.asarray(np.stack([_t5_bucket(rel * d) for _, d in ATTN_GROUPS]))


def _bias_tiles(rel_bias, bmaps):
    def body(tab_ref, bm_ref, o_ref):
        h = pl.program_id(0)
        bm = bm_ref[0]
        acc = jnp.zeros((BLK, 2 * BLK), F32)
        for b in range(NUM_BUCKETS):
            acc = jnp.where(bm == b, tab_ref[b, h], acc)
        o_ref[0] = acc

    return pl.pallas_call(
        body, out_shape=jax.ShapeDtypeStruct((N_ATTN_HEADS, BLK, 2 * BLK), F32), grid=(N_ATTN_HEADS,),
        in_specs=[pl.BlockSpec(memory_space=pltpu.SMEM),
                  pl.BlockSpec((1, BLK, 2 * BLK), lambda h: (h // HEADS_PER_GROUP, 0, 0))],
        out_specs=pl.BlockSpec((1, BLK, 2 * BLK), lambda h: (h, 0, 0)),
        compiler_params=_params(1), name="bias_tiles",
    )(rel_bias, bmaps)


def _bias_tiles_bwd(dbias_layers, bmaps):
    nl = len(dbias_layers)

    def body(*refs):
        bm = refs[nl][0]
        o_ref = refs[nl + 1]
        x = refs[0][0]
        for r in refs[1:nl]:
            x = x + r[0]
        lane = lax.broadcasted_iota(jnp.int32, (1, BLK), 1)
        row = jnp.zeros((1, BLK), F32)
        for b in range(NUM_BUCKETS):
            s = jnp.sum(jnp.where(bm == b, x, 0.0), axis=1, keepdims=True)
            s = jnp.sum(s, axis=0, keepdims=True)
            row = jnp.where(lane == b, s, row)
        o_ref[0] = row

    tile = pl.BlockSpec((1, BLK, 2 * BLK), lambda h: (h, 0, 0))
    out = pl.pallas_call(
        body, out_shape=jax.ShapeDtypeStruct((N_ATTN_HEADS, 1, BLK), F32), grid=(N_ATTN_HEADS,),
        in_specs=[tile] * nl + [pl.BlockSpec((1, BLK, 2 * BLK), lambda h: (h // HEADS_PER_GROUP, 0, 0))],
        out_specs=pl.BlockSpec((1, 1, BLK), lambda h: (h, 0, 0)),
        compiler_params=_params(1), name="bias_tiles_bwd",
    )(*dbias_layers, bmaps)
    return out[:, 0, :NUM_BUCKETS].T


def _pair_masks():
    lane = lax.broadcasted_iota(jnp.int32, (BLK, BLK), 1)
    row2 = lax.broadcasted_iota(jnp.int32, (2 * BLK, BLK), 0)
    lane2 = lax.broadcasted_iota(jnp.int32, (2 * BLK, BLK), 1)
    own = (lane2 // HEAD_DIM) == (row2 // BLK)
    qi = lax.broadcasted_iota(jnp.int32, (2 * BLK, 2 * BLK), 0) & (BLK - 1)
    kj = lax.broadcasted_iota(jnp.int32, (2 * BLK, 2 * BLK), 1)
    band = jnp.logical_and(kj >= qi, kj <= qi + BLK)
    return lane < HEAD_DIM, own, band, kj < BLK


def _pair_scores(q32, kb, bias2, own, band, is_prev, pen):
    qm = jnp.where(own, jnp.concatenate([q32, q32], axis=0), 0.0)
    s = _dot(qm, kb, NT) * (HEAD_DIM ** -0.5) + bias2
    if pen is not None:
        s = s + jnp.where(is_prev, pen, 0.0)
    return jnp.where(band, s, NEG), qm


HEAD_PAIRS = HEADS_PER_GROUP // 2


def _attn_fwd(qkv, bias, g, dil):
    rows = qkv.shape[0]
    nb = rows // BLK
    rb = 2 if nb % 2 == 0 else 1
    cur = pl.BlockSpec((rb * BLK, GROUP_WIDTH), lambda r, n: (n, r))
    prev = pl.BlockSpec((BLK, GROUP_WIDTH), lambda r, n: (jnp.maximum(rb * n - 1, 0), r))

    def body(c_ref, p_ref, b_ref, o_ref, l_ref):
        n = pl.program_id(1)
        pen0 = jnp.where(n > 0, 0.0, NEG)
        first, own, band, is_prev = _pair_masks()
        o_rows, l_rows = [], []
        for t in range(rb):
            rows_t = slice(t * BLK, (t + 1) * BLK)
            o_parts, l_parts = [], []
            for hp in range(HEAD_PAIRS):
                qc, kc, vc = (slice(k * ATTN_OUT + hp * BLK, k * ATTN_OUT + (hp + 1) * BLK) for k in range(3))
                if t == 0:
                    kp, vp, pen = p_ref[:, kc], p_ref[:, vc], pen0
                else:
                    before = slice((t - 1) * BLK, t * BLK)
                    kp, vp, pen = c_ref[before, kc], c_ref[before, vc], None
                kb = jnp.concatenate([kp, c_ref[rows_t, kc]], axis=0)
                vb = jnp.concatenate([vp, c_ref[rows_t, vc]], axis=0)
                bias2 = jnp.concatenate([b_ref[2 * hp], b_ref[2 * hp + 1]], axis=0)
                s, _ = _pair_scores(c_ref[rows_t, qc].astype(F32), kb, bias2, own, band, is_prev, pen)
                m = jnp.max(s, axis=1, keepdims=True)
                p = jnp.exp(s - m)
                l = jnp.sum(p, axis=1, keepdims=True)
                o2 = _dot(p * (1.0 / l), vb, NN)
                lse2 = m + jnp.log(l)
                o_parts.append(jnp.where(first, o2[:BLK], o2[BLK:]))
                l_parts.append(jnp.where(first, lse2[:BLK], lse2[BLK:]))
            o_rows.append(jnp.concatenate(o_parts, axis=1))
            l_rows.append(jnp.concatenate(l_parts, axis=1))
        o_ref[...] = jnp.concatenate(o_rows, axis=0)
        l_ref[...] = jnp.concatenate(l_rows, axis=0)

    out_spec = pl.BlockSpec((rb * BLK, ATTN_OUT), lambda r, n: (n, r))
    shape = jax.ShapeDtypeStruct((rows, dil * ATTN_OUT), F32)
    return pl.pallas_call(
        body, out_shape=[shape, shape], grid=(dil, nb // rb),
        in_specs=[cur, prev, pl.BlockSpec((HEADS_PER_GROUP, BLK, 2 * BLK), lambda r, n: (g, 0, 0))],
        out_specs=[out_spec, out_spec], compiler_params=_params(2), name=f"attn_fwd_g{g}",
    )(qkv, qkv, bias)


def _attn_bwd(qkv, bias, dy, ya, lt, g, dil):
    rows = qkv.shape[0]
    nb = rows // BLK
    rb = 2 if nb % 2 == 0 else 1
    ns = nb // rb
    cur = lambda w: pl.BlockSpec((rb * BLK, w), lambda r, n: (jnp.minimum(n, ns - 1), r))
    prev = pl.BlockSpec((BLK, GROUP_WIDTH), lambda r, n: (jnp.clip(rb * n - 1, 0, nb - 1), r))

    def body(c_ref, p_ref, b_ref, dy_ref, ya_ref, lt_ref, dz_ref, db_ref, cq_ref, ckv_full, ckv_part):
        r, n = pl.program_id(0), pl.program_id(1)

        @pl.when(jnp.logical_and(r == 0, n == 0))
        def _():
            db_ref[...] = jnp.zeros_like(db_ref)

        @pl.when(n == 0)
        def _():
            cq_ref[...] = jnp.zeros_like(cq_ref)
            ckv_full[...] = jnp.zeros_like(ckv_full)
            ckv_part[...] = jnp.zeros_like(ckv_part)

        def late(dk_add, dv_add):
            kv = ckv_part[...]
            k_last, v_last = kv[:, :ATTN_OUT] + dk_add, kv[:, ATTN_OUT:] + dv_add
            if rb == 1:
                return jnp.concatenate([cq_ref[...], k_last, v_last], axis=1)
            full = ckv_full[...]
            return jnp.concatenate([cq_ref[...],
                                    jnp.concatenate([full[:, :ATTN_OUT], k_last], axis=0),
                                    jnp.concatenate([full[:, ATTN_OUT:], v_last], axis=0)], axis=1)

        @pl.when(n < ns)
        def _():
            pen0 = jnp.where(n > 0, 0.0, NEG)
            first, own, band, is_prev = _pair_masks()
            second = jnp.logical_not(first)
            scale = HEAD_DIM ** -0.5
            db_sum, wides = None, []
            for t in range(rb):
                rows_t = slice(t * BLK, (t + 1) * BLK)
                parts = {k: [] for k in ("dq", "dkp", "dkc", "dvp", "dvc")}
                db_parts = []
                for hp in range(HEAD_PAIRS):
                    cols = slice(hp * BLK, (hp + 1) * BLK)
                    qc, kc, vc = (slice(k * ATTN_OUT + hp * BLK, k * ATTN_OUT + (hp + 1) * BLK) for k in range(3))
                    if t == 0:
                        kp, vp, pen = p_ref[:, kc], p_ref[:, vc], pen0
                    else:
                        before = slice((t - 1) * BLK, t * BLK)
                        kp, vp, pen = c_ref[before, kc], c_ref[before, vc], None
                    kb = jnp.concatenate([kp, c_ref[rows_t, kc]], axis=0)
                    vb = jnp.concatenate([vp, c_ref[rows_t, vc]], axis=0)
                    bias2 = jnp.concatenate([b_ref[2 * hp], b_ref[2 * hp + 1]], axis=0)
                    dy_, lt_ = dy_ref[rows_t, cols], lt_ref[rows_t, cols]
                    dyy = dy_ * ya_ref[rows_t, cols]
                    per_head = lambda v, red, fill: jnp.concatenate(
                        [red(jnp.where(first, v, fill), axis=1, keepdims=True),
                         red(jnp.where(second, v, fill), axis=1, keepdims=True)], axis=0)
                    lse2 = per_head(lt_, jnp.max, NEG)
                    delta2 = per_head(dyy, jnp.sum, 0.0)
                    s, qm = _pair_scores(c_ref[rows_t, qc].astype(F32), kb, bias2, own, band, is_prev, pen)
                    p = jnp.exp(s - lse2)
                    dym = jnp.where(own, jnp.concatenate([dy_, dy_], axis=0), 0.0)
                    ds = p * (_dot(dym, vb, NT) - delta2)
                    db_parts += [ds[:BLK], ds[BLK:]]
                    dq2 = _dot(ds, kb, NN) * scale
                    dkb = _dot(ds, qm, TN) * scale
                    dvb = _dot(p, dym, TN)
                    for k, val in (("dq", jnp.where(first, dq2[:BLK], dq2[BLK:])), ("dkp", dkb[:BLK]),
                                   ("dkc", dkb[BLK:]), ("dvp", dvb[:BLK]), ("dvc", dvb[BLK:])):
                        parts[k].append(val)
                wides.append({k: jnp.concatenate(val, axis=1) for k, val in parts.items()})
                db_t = jnp.stack(db_parts, axis=0)
                db_sum = db_t if db_sum is None else db_sum + db_t
            db_ref[...] += db_sum
            dz_ref[...] = late(wides[0]["dkp"], wides[0]["dvp"]).astype(dz_ref.dtype)
            cq_ref[...] = jnp.concatenate([w["dq"] for w in wides], axis=0)
            if rb == 2:
                ckv_full[...] = jnp.concatenate([wides[0]["dkc"] + wides[1]["dkp"],
                                                 wides[0]["dvc"] + wides[1]["dvp"]], axis=1)
            ckv_part[...] = jnp.concatenate([wides[-1]["dkc"], wides[-1]["dvc"]], axis=1)

        @pl.when(n == ns)
        def _():
            zero = jnp.zeros((BLK, ATTN_OUT), F32)
            dz_ref[...] = late(zero, zero).astype(dz_ref.dtype)

    out_late = pl.BlockSpec((rb * BLK, GROUP_WIDTH), lambda r, n: (jnp.maximum(n - 1, 0), r))
    kv_carry = pltpu.VMEM((BLK, 2 * ATTN_OUT), F32)
    return pl.pallas_call(
        body,
        out_shape=[jax.ShapeDtypeStruct((rows, dil * GROUP_WIDTH), MXU_DTYPE),
                   jax.ShapeDtypeStruct((HEADS_PER_GROUP, BLK, 2 * BLK), F32)],
        grid=(dil, ns + 1),
        in_specs=[cur(GROUP_WIDTH), prev, pl.BlockSpec((HEADS_PER_GROUP, BLK, 2 * BLK), lambda r, n: (g, 0, 0)),
                  cur(ATTN_OUT), cur(ATTN_OUT), cur(ATTN_OUT)],
        out_specs=[out_late, pl.BlockSpec((HEADS_PER_GROUP, BLK, 2 * BLK), lambda r, n: (0, 0, 0))],
        scratch_shapes=[pltpu.VMEM((rb * BLK, ATTN_OUT), F32), kv_carry, kv_carry],
        compiler_params=_params(2), name=f"attn_bwd_g{g}",
    )(qkv, qkv, bias, dy, ya, lt)


def _attn_combine(o_list, l_list):
    dils = [d for _, d in ATTN_GROUPS]
    s = o_list[0].shape[0] * dils[0]
    tr = _pick(s, (512, 256, 128))
    n_g = len(dils)

    def body(*refs):
        o_refs, l_refs = refs[:n_g], refs[n_g:2 * n_g]
        yb_ref = refs[2 * n_g]
        y_refs, lt_refs = refs[2 * n_g + 1:3 * n_g + 1], refs[3 * n_g + 1:4 * n_g + 1]
        scr_o, scr_l = refs[4 * n_g + 1], refs[4 * n_g + 2]
        os_, ls_ = [], []
        for g in range(n_g):
            if dils[g] == 1:
                os_.append(o_refs[g][...])
                ls_.append(l_refs[g][...])
            else:
                _phase_to_tokens(o_refs[g], scr_o, dils[g], ATTN_OUT)
                _phase_to_tokens(l_refs[g], scr_l, dils[g], ATTN_OUT)
                os_.append(_blocks_to_cols(scr_o))
                ls_.append(_blocks_to_cols(scr_l))
        mx = functools.reduce(jnp.maximum, ls_)
        es = [jnp.exp(a - mx) for a in ls_]
        den = functools.reduce(lambda a, b: a + b, es)
        inv = 1.0 / den
        y = functools.reduce(lambda a, b: a + b, [(e * inv) * o for e, o in zip(es, os_)])
        lt = mx + jnp.log(den)
        yb_ref[...] = y.astype(yb_ref.dtype)
        _cols_to_blocks(y, scr_o)
        _cols_to_blocks(lt, scr_l)
        for g in range(n_g):
            if dils[g] == 1:
                y_refs[g][...] = y
                lt_refs[g][...] = lt
            else:
                _tokens_to_phase(scr_o, y_refs[g], dils[g], ATTN_OUT)
                _tokens_to_phase(scr_l, lt_refs[g], dils[g], ATTN_OUT)

    view = lambda d: pl.BlockSpec((tr // d, d * ATTN_OUT), lambda i: (i, 0))
    views = [view(d) for d in dils]
    f = lambda d: jax.ShapeDtypeStruct((s // d, d * ATTN_OUT), F32)
    fs = [f(d) for d in dils]
    scratch = pltpu.VMEM((ATTN_OUT // LANES, tr, LANES), F32)
    res = pl.pallas_call(
        body, out_shape=[jax.ShapeDtypeStruct((s, ATTN_OUT), MXU_DTYPE)] + fs + fs, grid=(s // tr,),
        in_specs=views + views, out_specs=[view(1)] + views + views, scratch_shapes=[scratch, scratch],
        compiler_params=_params(1), name="attn_combine",
    )(*o_list, *l_list)
    return res[0], list(res[1:1 + n_g]), list(res[1 + n_g:])


def _ret_consts(s):
    half = RET_QK // 2
    pos = jnp.arange(s, dtype=F32)
    inv_freq = ROPE_BASE ** (-jnp.arange(half, dtype=F32) / half)
    ang = pos[:, None] * inv_freq[None]
    log_g = jnp.log(1.0 - 2.0 ** (-5.0 - jnp.arange(RET_HEADS, dtype=F32)))
    n = jnp.arange(RET_CHUNK, dtype=F32)
    diff = n[:, None] - n[None, :]
    dmask = jnp.where(diff >= 0, jnp.exp(log_g[:, None, None] * jnp.maximum(diff, 0.0)), 0.0)
    qdec = jnp.exp(log_g[:, None] * (n + 1.0))
    kdec = jnp.exp(log_g[:, None] * (RET_CHUNK - 1.0 - n))
    cdec = jnp.exp(log_g * RET_CHUNK)
    wide = (RET_HEADS, RET_CHUNK, RET_QK)
    return dict(cos=jnp.cos(ang), sin=jnp.sin(ang), dmask=dmask,
                qdec=jnp.broadcast_to(qdec[:, :, None], wide), kdec=jnp.broadcast_to(kdec[:, :, None], wide),
                cdec=cdec)


def _rot(t, cs, sn):
    half = RET_QK // 2
    t1, t2 = t[:, :half], t[:, half:]
    return jnp.concatenate([t1 * cs - t2 * sn, t1 * sn + t2 * cs], axis=1)


def _rot_bwd(d, cs, sn):
    half = RET_QK // 2
    d1, d2 = d[:, :half], d[:, half:]
    return jnp.concatenate([d1 * cs + d2 * sn, d2 * cs - d1 * sn], axis=1)


def _ret_fwd(zq, zk, zv, zg, rc):
    s = zq.shape[0]
    nc = s // RET_CHUNK
    c = RET_CHUNK

    def body(cd_ref, q_ref, k_ref, v_ref, g_ref, cos_ref, sin_ref, dm_ref, qd_ref, kd_ref,
             yb_ref, o_ref, st_ref, state):
        n = pl.program_id(0)

        @pl.when(n == 0)
        def _():
            state[...] = jnp.zeros_like(state)

        cs, sn = cos_ref[...], sin_ref[...]
        for h in range(RET_HEADS):
            qs, vs = slice(h * RET_QK, (h + 1) * RET_QK), slice(h * RET_V, (h + 1) * RET_V)
            qr = _rot(q_ref[:, qs], cs, sn)
            kr = _rot(k_ref[:, qs], cs, sn) * (RET_QK ** -0.5)
            v = v_ref[:, vs]
            st = state[h]
            st_ref[h, 0] = st.astype(st_ref.dtype)
            sc = _dot(qr, kr, NT) * dm_ref[h]
            o = _dot(sc, v, NN) + _dot(qr * qd_ref[h], st, NN)
            state[h] = st * cd_ref[h] + _dot(kr * kd_ref[h], v, TN)
            o_ref[:, vs] = o
            xh, _ = _norm_rows(o, GN_EPS)
            gv = g_ref[:, vs]
            yb_ref[:, vs] = (gv * _sigmoid(gv) * xh).astype(yb_ref.dtype)

    row = lambda w: pl.BlockSpec((c, w), lambda n: (n, 0))
    const = lambda a: pl.BlockSpec(a.shape, lambda n: (0, 0, 0))
    return pl.pallas_call(
        body,
        out_shape=[jax.ShapeDtypeStruct((s, RET_HEADS * RET_V), MXU_DTYPE),
                   jax.ShapeDtypeStruct((s, RET_HEADS * RET_V), F32),
                   jax.ShapeDtypeStruct((RET_HEADS, nc, RET_QK, RET_V), MXU_DTYPE)],
        grid=(nc,),
        in_specs=[pl.BlockSpec(memory_space=pltpu.SMEM), row(RET_HEADS * RET_QK), row(RET_HEADS * RET_QK),
                  row(RET_HEADS * RET_V), row(RET_HEADS * RET_V), row(RET_QK // 2), row(RET_QK // 2),
                  const(rc["dmask"]), const(rc["qdec"]), const(rc["kdec"])],
        out_specs=[row(RET_HEADS * RET_V), row(RET_HEADS * RET_V),
                   pl.BlockSpec((RET_HEADS, 1, RET_QK, RET_V), lambda n: (0, n, 0, 0))],
        scratch_shapes=[pltpu.VMEM((RET_HEADS, RET_QK, RET_V), F32)],
        compiler_params=_params(1), name="ret_fwd",
    )(rc["cdec"], zq, zk, zv, zg, rc["cos"], rc["sin"], rc["dmask"], rc["qdec"], rc["kdec"])


def _ret_bwd(zq, zk, zv, zg, o_ret, states, dyb, rc):
    s = zq.shape[0]
    nc = s // RET_CHUNK
    c = RET_CHUNK

    def body(cd_ref, q_ref, k_ref, v_ref, g_ref, o_ref, st_ref, dy_ref, cos_ref, sin_ref, dm_ref, qd_ref,
             kd_ref, dq_ref, dk_ref, dv_ref, dg_ref, dstate):
        n = pl.program_id(0)

        @pl.when(n == 0)
        def _():
            dstate[...] = jnp.zeros_like(dstate)

        cs, sn = cos_ref[...], sin_ref[...]
        for h in range(RET_HEADS):
            qs, vs = slice(h * RET_QK, (h + 1) * RET_QK), slice(h * RET_V, (h + 1) * RET_V)
            qr = _rot(q_ref[:, qs], cs, sn)
            kr = _rot(k_ref[:, qs], cs, sn) * (RET_QK ** -0.5)
            v = v_ref[:, vs]
            st = st_ref[h, 0]
            dm, qd, kd = dm_ref[h], qd_ref[h], kd_ref[h]
            xh, rstd = _norm_rows(o_ref[:, vs], GN_EPS)
            gv, dy = g_ref[:, vs], dy_ref[:, vs]
            sg = _sigmoid(gv)
            dg_ref[:, vs] = (dy * xh * (sg * (1.0 + gv * (1.0 - sg)))).astype(dg_ref.dtype)
            do = _norm_rows_bwd(dy * (gv * sg), xh, rstd)
            ds_next = dstate[h]
            sc = _dot(qr, kr, NT) * dm
            da = _dot(do, v, NT) * dm
            dv = _dot(sc, do, TN) + _dot(kr * kd, ds_next, NN)
            dqr = _dot(da, kr, NN) + _dot(do, st, NT) * qd
            dkr = _dot(da, qr, TN) + _dot(v, ds_next, NT) * kd
            dstate[h] = ds_next * cd_ref[h] + _dot(qr * qd, do, TN)
            dq_ref[:, qs] = _rot_bwd(dqr, cs, sn).astype(dq_ref.dtype)
            dk_ref[:, qs] = _rot_bwd(dkr * (RET_QK ** -0.5), cs, sn).astype(dk_ref.dtype)
            dv_ref[:, vs] = dv.astype(dv_ref.dtype)

    row = lambda w: pl.BlockSpec((c, w), lambda n: (nc - 1 - n, 0))
    const = lambda a: pl.BlockSpec(a.shape, lambda n: (0, 0, 0))
    qk_w, v_w = RET_HEADS * RET_QK, RET_HEADS * RET_V
    g_qk, g_v = jax.ShapeDtypeStruct((s, qk_w), MXU_DTYPE), jax.ShapeDtypeStruct((s, v_w), MXU_DTYPE)
    return pl.pallas_call(
        body,
        out_shape=[g_qk, g_qk, g_v, g_v],
        grid=(nc,),
        in_specs=[pl.BlockSpec(memory_space=pltpu.SMEM), row(qk_w), row(qk_w), row(v_w), row(v_w), row(v_w),
                  pl.BlockSpec((RET_HEADS, 1, RET_QK, RET_V), lambda n: (0, nc - 1 - n, 0, 0)), row(v_w),
                  row(RET_QK // 2), row(RET_QK // 2), const(rc["dmask"]), const(rc["qdec"]), const(rc["kdec"])],
        out_specs=[row(qk_w), row(qk_w), row(v_w), row(v_w)],
        scratch_shapes=[pltpu.VMEM((RET_HEADS, RET_QK, RET_V), F32)],
        compiler_params=_params(1), name="ret_bwd",
    )(rc["cdec"], zq, zk, zv, zg, o_ret, states, dyb, rc["cos"], rc["sin"], rc["dmask"], rc["qdec"],
      rc["kdec"])


ROW_TILE = 256


def _mix_fwd(x, ya_b, yb_b, ze, wap_t, wrp, wout, gam, bet):
    s = x.shape[0]
    tm = ROW_TILE

    def body(x_ref, ya_ref, yb_ref, ze_ref, wap_ref, wrp_ref, wout_ref, g_ref, b_ref,
             x1_ref, u1_ref, a1_ref, a2_ref, mg_ref, x1b_ref):
        a1 = _dot(ya_ref[...], wap_ref[...], NT)
        a2 = _dot(yb_ref[...], wrp_ref[...], NN)
        ze_ = ze_ref[...]
        merged = _sigmoid(ze_[:, :D_MODEL]) * a1 + _sigmoid(ze_[:, D_MODEL:]) * a2
        u = ALPHA * x_ref[...] + _dot(merged, wout_ref[...], NN)
        xh, _ = _norm_rows(u, LN_EPS)
        x1 = xh * g_ref[...] + b_ref[...]
        x1_ref[...] = x1
        x1b_ref[...] = x1.astype(x1b_ref.dtype)
        u1_ref[...] = u
        a1_ref[...] = a1.astype(a1_ref.dtype)
        a2_ref[...] = a2.astype(a2_ref.dtype)
        mg_ref[...] = merged.astype(mg_ref.dtype)

    row = lambda w: pl.BlockSpec((tm, w), lambda i: (i, 0))
    full = _resident
    f = jax.ShapeDtypeStruct((s, D_MODEL), F32)
    m = jax.ShapeDtypeStruct((s, D_MODEL), MXU_DTYPE)
    return pl.pallas_call(
        body, out_shape=[f, f, m, m, m, m], grid=(s // tm,),
        in_specs=[row(D_MODEL), row(ATTN_OUT), row(RET_HEADS * RET_V), row(2 * D_MODEL), full(wap_t), full(wrp),
                  full(wout), full(gam), full(bet)],
        out_specs=[row(D_MODEL)] * 6, compiler_params=_params(1), name="mix_fwd",
    )(x, ya_b, yb_b, ze, wap_t, wrp, wout, gam, bet)


def _mix_bwd(dx1, u1, a1, a2, ze, wap_t, wrp, wout, gam):
    s = dx1.shape[0]
    tm = ROW_TILE

    dils = [d for _, d in ATTN_GROUPS]

    def body(dx_ref, u_ref, a1_ref, a2_ref, ze_ref, wap_ref, wrp_ref, wout_ref, g_ref,
             dres_ref, du_ref, da1_ref, da2_ref, dze_ref, dyb_ref, dgam_ref, dbet_ref, *rest):
        dya_refs, dya_scr = rest[:len(dils)], rest[len(dils)]
        @pl.when(pl.program_id(0) == 0)
        def _():
            dgam_ref[...] = jnp.zeros_like(dgam_ref)
            dbet_ref[...] = jnp.zeros_like(dbet_ref)

        dx = dx_ref[...]
        xh, rstd = _norm_rows(u_ref[...], LN_EPS)
        dgam_ref[...] += jnp.sum(dx * xh, axis=0, keepdims=True)
        dbet_ref[...] += jnp.sum(dx, axis=0, keepdims=True)
        du = _norm_rows_bwd(dx * g_ref[...], xh, rstd)
        dres_ref[...] = ALPHA * du
        du_ref[...] = du.astype(du_ref.dtype)
        dm = _dot(du, wout_ref[...], NT)
        ze_ = ze_ref[...]
        sa, sb = _sigmoid(ze_[:, :D_MODEL]), _sigmoid(ze_[:, D_MODEL:])
        da1, da2 = dm * sa, dm * sb
        dze_ref[...] = jnp.concatenate([dm * a1_ref[...] * (sa * (1.0 - sa)),
                                        dm * a2_ref[...] * (sb * (1.0 - sb))], axis=1).astype(dze_ref.dtype)
        da1_ref[...] = da1.astype(da1_ref.dtype)
        da2_ref[...] = da2.astype(da2_ref.dtype)
        dyb_ref[...] = _dot(da2, wrp_ref[...], NT)
        dya = _dot(da1, wap_ref[...], NN)
        _cols_to_blocks(dya, dya_scr)
        for g, d in enumerate(dils):
            if d == 1:
                dya_refs[g][...] = dya
            else:
                _tokens_to_phase(dya_scr, dya_refs[g], d, ATTN_OUT)

    row = lambda w: pl.BlockSpec((tm, w), lambda i: (i, 0))
    full = _resident
    vec = pl.BlockSpec((1, D_MODEL), lambda i: (0, 0))
    f = lambda w: jax.ShapeDtypeStruct((s, w), F32)
    m = lambda w: jax.ShapeDtypeStruct((s, w), MXU_DTYPE)
    v = jax.ShapeDtypeStruct((1, D_MODEL), F32)
    res = pl.pallas_call(
        body,
        out_shape=[f(D_MODEL), m(D_MODEL), m(D_MODEL), m(D_MODEL), m(2 * D_MODEL), f(RET_HEADS * RET_V), v, v]
        + [jax.ShapeDtypeStruct((s // d, d * ATTN_OUT), F32) for d in dils],
        grid=(s // tm,),
        in_specs=[row(D_MODEL)] * 4 + [row(2 * D_MODEL), full(wap_t), full(wrp), full(wout), full(gam)],
        out_specs=[row(D_MODEL)] * 4 + [row(2 * D_MODEL), row(RET_HEADS * RET_V), vec, vec]
        + [pl.BlockSpec((tm // d, d * ATTN_OUT), lambda i: (i, 0)) for d in dils],
        scratch_shapes=[pltpu.VMEM((ATTN_OUT // LANES, tm, LANES), F32)],
        compiler_params=_params(1), name="mix_bwd",
    )(dx1, u1, a1, a2, ze, wap_t, wrp, wout, gam)
    return (*res[:8], list(res[8:]))


FF_CHUNK = 1408


def _ffn_fwd(x1, wg_t, wu_t, wd, gam, bet):
    s = x1.shape[0]
    tm, fc = ROW_TILE, FF_CHUNK

    def body(x_ref, wg_ref, wu_ref, wd_ref, g_ref, b_ref, x2_ref, u2_ref, a_ref, b_out_ref, h_ref, x2b_ref):
        xv = x_ref[...]
        u = ALPHA * xv
        for f0 in range(0, D_FF, fc):
            ch = slice(f0, f0 + fc)
            a = _dot(xv, wg_ref[ch, :], NT)
            b = _dot(xv, wu_ref[ch, :], NT)
            hid = a * _sigmoid(a) * b
            u = u + _dot(hid, wd_ref[ch, :], NN)
            a_ref[:, ch] = a
            b_out_ref[:, ch] = b
            h_ref[:, ch] = hid.astype(h_ref.dtype)
        xh, _ = _norm_rows(u, LN_EPS)
        u2_ref[...] = u
        x2 = xh * g_ref[...] + b_ref[...]
        x2_ref[...] = x2
        x2b_ref[...] = x2.astype(x2b_ref.dtype)

    row = lambda w: pl.BlockSpec((tm, w), lambda i: (i, 0))
    f = lambda w: jax.ShapeDtypeStruct((s, w), F32)
    m = lambda w: jax.ShapeDtypeStruct((s, w), MXU_DTYPE)
    return pl.pallas_call(
        body, out_shape=[f(D_MODEL), f(D_MODEL), f(D_FF), f(D_FF), m(D_FF), m(D_MODEL)],
        grid=(s // tm,),
        in_specs=[row(D_MODEL), _resident(wg_t), _resident(wu_t), _resident(wd), _resident(gam), _resident(bet)],
        out_specs=[row(D_MODEL), row(D_MODEL), row(D_FF), row(D_FF), row(D_FF), row(D_MODEL)],
        compiler_params=_params(1), name="ffn_fwd",
    )(x1, wg_t, wu_t, wd, gam, bet)


def _ffn_bwd(dx2, u2, fa, fb, wg_t, wu_t, wd, gam):
    s = dx2.shape[0]
    tm, fc = ROW_TILE, FF_CHUNK

    def body(dx_ref, u_ref, a_ref, b_ref, wg_ref, wu_ref, wd_ref, g_ref,
             dx1_ref, du_ref, da_ref, db_ref, dgam_ref, dbet_ref):
        @pl.when(pl.program_id(0) == 0)
        def _():
            dgam_ref[...] = jnp.zeros_like(dgam_ref)
            dbet_ref[...] = jnp.zeros_like(dbet_ref)

        dx = dx_ref[...]
        xh, rstd = _norm_rows(u_ref[...], LN_EPS)
        dgam_ref[...] += jnp.sum(dx * xh, axis=0, keepdims=True)
        dbet_ref[...] += jnp.sum(dx, axis=0, keepdims=True)
        du = _norm_rows_bwd(dx * g_ref[...], xh, rstd)
        du_ref[...] = du.astype(du_ref.dtype)
        acc = ALPHA * du
        for f0 in range(0, D_FF, fc):
            ch = slice(f0, f0 + fc)
            dh = _dot(du, wd_ref[ch, :], NT)
            a, b = a_ref[:, ch], b_ref[:, ch]
            sg = _sigmoid(a)
            da = dh * b * (sg * (1.0 + a * (1.0 - sg)))
            db = dh * (a * sg)
            acc = acc + _dot(da, wg_ref[ch, :], NN) + _dot(db, wu_ref[ch, :], NN)
            da_ref[:, ch] = da.astype(da_ref.dtype)
            db_ref[:, ch] = db.astype(db_ref.dtype)
        dx1_ref[...] = acc

    row = lambda w: pl.BlockSpec((tm, w), lambda i: (i, 0))
    vec = pl.BlockSpec((1, D_MODEL), lambda i: (0, 0))
    v = jax.ShapeDtypeStruct((1, D_MODEL), F32)
    return pl.pallas_call(
        body,
        out_shape=[jax.ShapeDtypeStruct((s, D_MODEL), F32), jax.ShapeDtypeStruct((s, D_MODEL), MXU_DTYPE),
                   jax.ShapeDtypeStruct((s, D_FF), MXU_DTYPE), jax.ShapeDtypeStruct((s, D_FF), MXU_DTYPE), v, v],
        grid=(s // tm,),
        in_specs=[row(D_MODEL), row(D_MODEL), row(D_FF), row(D_FF), _resident(wg_t), _resident(wu_t),
                  _resident(wd), _resident(gam)],
        out_specs=[row(D_MODEL), row(D_MODEL), row(D_FF), row(D_FF), vec, vec],
        compiler_params=_params(1), name="ffn_bwd",
    )(dx2, u2, fa, fb, wg_t, wu_t, wd, gam)


def _loss_head(y, target):
    s = y.shape[0]
    tr = _pick(s, (512, 256, 128))

    def body(y_ref, t_ref, dy_ref, l_ref):
        @pl.when(pl.program_id(0) == 0)
        def _():
            l_ref[...] = jnp.zeros_like(l_ref)

        e = y_ref[...] - t_ref[...]
        dy_ref[...] = e * (1.0 / D_MODEL)
        part = jnp.sum(jnp.sum(e * e, axis=1, keepdims=True), axis=0, keepdims=True)
        l_ref[...] += part * (0.5 / D_MODEL)

    spec = pl.BlockSpec((tr, D_MODEL), lambda i: (i, 0))
    dy, part = pl.pallas_call(
        body, out_shape=[jax.ShapeDtypeStruct((s, D_MODEL), F32), jax.ShapeDtypeStruct((8, BLK), F32)],
        grid=(s // tr,), in_specs=[spec, spec], out_specs=[spec, pl.BlockSpec((8, BLK), lambda i: (0, 0))],
        compiler_params=_params(1), name="loss_head",
    )(y, target)
    return dy, part[0, 0]


def _adamw(w, g, m, v, name):
    rows, cols = w.shape
    budget = 1 << 20
    cands = [t for t in range(8, rows + 1, 8) if rows % t == 0 and t * cols * 4 <= budget]
    tr = max(cands) if cands else rows

    def body(w_ref, g_ref, m_ref, v_ref, d_ref, nm_ref, nv_ref):
        gv = g_ref[...]
        mn = ADAM_B1 * m_ref[...] + (1.0 - ADAM_B1) * gv
        vn = ADAM_B2 * v_ref[...] + (1.0 - ADAM_B2) * (gv * gv)
        m_hat = mn / (1.0 - ADAM_B1 ** ADAM_STEP)
        v_hat = vn / (1.0 - ADAM_B2 ** ADAM_STEP)
        d_ref[...] = -ADAM_LR * (m_hat / (jnp.sqrt(v_hat) + ADAM_EPS) + ADAM_WD * w_ref[...])
        nm_ref[...] = mn
        nv_ref[...] = vn

    spec = pl.BlockSpec((tr, cols), lambda i: (i, 0))
    shape = jax.ShapeDtypeStruct((rows, cols), F32)
    return pl.pallas_call(
        body, out_shape=[shape, shape, shape], grid=(rows // tr,), in_specs=[spec] * 4, out_specs=[spec] * 3,
        compiler_params=_params(1), name=name,
    )(w, g, m, v)


MESH_ID = pl.DeviceIdType.MESH
ANY = pl.BlockSpec(memory_space=pl.ANY)


def _place():
    x, y, c = lax.axis_index("x"), lax.axis_index("y"), lax.axis_index("c")
    other_chips = [(1 - x, y), (x, 1 - y), (1 - x, 1 - y)]
    return x, y, c, other_chips


def _chip_no(chip):
    return 2 * chip[0] + chip[1]


def _gather_exchange(f_ref, g_ref, send_sems, recv_sems, hr):
    x, y, c, chips = _place()
    sibling = (x, y, 1 - c)

    def piece(chip, half):
        return g_ref.at[_chip_no(chip), pl.ds(half * hr, hr), :]

    def copy(k, src, dst, to):
        return pltpu.make_async_remote_copy(src_ref=src, dst_ref=dst, send_sem=send_sems.at[k],
                                            recv_sem=recv_sems.at[k], device_id=to, device_id_type=MESH_ID)

    own = copy(6, f_ref, g_ref.at[_chip_no((x, y))], sibling)
    own.start()
    my_half = f_ref.at[pl.ds(c * hr, hr), :]
    first = [copy(k, my_half, piece((x, y), c), (*chip, c)) for k, chip in enumerate(chips)]
    for cp in first:
        cp.start()
    passed = [copy(3 + k, piece(chip, c), piece(chip, c), sibling) for k, chip in enumerate(chips)]
    for k, chip in enumerate(chips):
        copy(k, my_half, piece(chip, c), (*chip, c)).wait_recv()
        passed[k].start()
    for k, chip in enumerate(chips):
        copy(3 + k, my_half, piece(chip, 1 - c), sibling).wait_recv()
    for cp in first + passed:
        cp.wait_send()
    own.wait()


GATHER_SEMS = (pltpu.SemaphoreType.DMA((7,)), pltpu.SemaphoreType.DMA((7,)))


HBM = pltpu.MemorySpace.HBM
ONE_SEM_PAIR = (pltpu.SemaphoreType.DMA, pltpu.SemaphoreType.DMA)


def _sibling_of_me():
    x, y, c, _ = _place()
    return [(x, y, 1 - c)]


def _same_core_of_other_chips():
    x, y, c, chips = _place()
    return [(*chip, c) for chip in chips]


def _on_sequencer(name, collective_id, sems, peers, exchange):
    @pl.kernel(mesh=plsc.ScalarSubcoreMesh(axis_name="sequencer", num_cores=1), name=name, scratch_types=sems,
               compiler_params=pltpu.CompilerParams(collective_id=collective_id))
    def launch(*sem_refs):
        barrier = pltpu.get_barrier_semaphore()
        devices = peers()
        for peer in devices:
            pl.semaphore_signal(barrier, inc=1, device_id=peer, device_id_type=MESH_ID)
        pl.semaphore_wait(barrier, len(devices))
        exchange(*sem_refs)

    launch()


def _all_gather_rows(flat, behind=None):
    r, cols = flat.shape
    out = jax.ShapeDtypeStruct((N_CHIPS, r, cols), flat.dtype)
    if behind is not None:
        f_ref, g_ref = jax.new_ref(flat, memory_space=HBM), jax.empty_ref(out, memory_space=HBM)
        _on_sequencer(f"all_gather_weights_behind_{behind[1]}", behind[0], GATHER_SEMS,
                      lambda: _sibling_of_me() + _same_core_of_other_chips(),
                      lambda s, r_: _gather_exchange(f_ref, g_ref, s, r_, r // 2))
        return g_ref[...]

    def body(f_ref, g_ref, send_sems, recv_sems):
        _gather_exchange(f_ref, g_ref, send_sems, recv_sems, r // 2)

    return pl.pallas_call(body, out_shape=out, in_specs=[ANY], out_specs=ANY, scratch_shapes=list(GATHER_SEMS),
                          name="all_gather_weights")(flat)


def _swap_halves(gall, behind=None):
    _, r, cols = gall.shape
    hr = r // 2
    out = jax.ShapeDtypeStruct((N_CHIPS, hr, cols), gall.dtype)

    def exchange(g_ref, a_ref, send_sem, recv_sem):
        x, y, c, _ = _place()
        cp = pltpu.make_async_remote_copy(src_ref=g_ref.at[:, pl.ds((1 - c) * hr, hr), :], dst_ref=a_ref,
                                          send_sem=send_sem, recv_sem=recv_sem, device_id=(x, y, 1 - c),
                                          device_id_type=MESH_ID)
        cp.start()
        cp.wait()

    if behind is not None:
        g_ref, a_ref = jax.new_ref(gall, memory_space=HBM), jax.empty_ref(out, memory_space=HBM)
        _on_sequencer(f"grad_swap_halves_behind_{behind[1]}", behind[0], ONE_SEM_PAIR, _sibling_of_me,
                      lambda s, r_: exchange(g_ref, a_ref, s, r_))
        return a_ref[...]

    def body(g_ref, a_ref, send_sem, recv_sem):
        exchange(g_ref, a_ref, send_sem, recv_sem)

    return pl.pallas_call(body, out_shape=out, in_specs=[ANY], out_specs=ANY, scratch_shapes=list(ONE_SEM_PAIR),
                          name="grad_swap_halves")(gall)


def _scatter_to_chips(p, behind=None):
    _, hr, cols = p.shape
    out = jax.ShapeDtypeStruct((3, hr, cols), p.dtype)
    sems = (pltpu.SemaphoreType.DMA((3,)), pltpu.SemaphoreType.DMA((3,)))

    def exchange(p_ref, b_ref, send_sems, recv_sems):
        x, y, c, chips = _place()
        cps = [pltpu.make_async_remote_copy(src_ref=p_ref.at[_chip_no(chip)], dst_ref=b_ref.at[k],
                                            send_sem=send_sems.at[k], recv_sem=recv_sems.at[k],
                                            device_id=(*chip, c), device_id_type=MESH_ID)
               for k, chip in enumerate(chips)]
        for cp in cps:
            cp.start()
        for cp in cps:
            cp.wait()

    if behind is not None:
        p_ref, b_ref = jax.new_ref(p, memory_space=HBM), jax.empty_ref(out, memory_space=HBM)
        _on_sequencer(f"grad_scatter_chips_behind_{behind[1]}", behind[0], sems, _same_core_of_other_chips,
                      lambda s, r_: exchange(p_ref, b_ref, s, r_))
        return b_ref[...]

    def body(p_ref, b_ref, send_sems, recv_sems):
        exchange(p_ref, b_ref, send_sems, recv_sems)

    return pl.pallas_call(body, out_shape=out, in_specs=[ANY], out_specs=ANY, scratch_shapes=list(sems),
                          name="grad_scatter_chips")(p)


def _share_with_sibling(full, behind=None):
    r, cols = full.shape
    hr = r // 2

    def exchange(in_ref, out_ref, send_sem, recv_sem):
        x, y, c, _ = _place()
        cp = pltpu.make_async_remote_copy(src_ref=in_ref.at[pl.ds(c * hr, hr), :],
                                          dst_ref=out_ref.at[pl.ds(c * hr, hr), :],
                                          send_sem=send_sem, recv_sem=recv_sem, device_id=(x, y, 1 - c),
                                          device_id_type=MESH_ID)
        cp.start()
        cp.wait()

    if behind is not None:
        full_ref = jax.new_ref(full, memory_space=HBM)
        _on_sequencer(f"grad_share_sibling_behind_{behind[1]}", behind[0], ONE_SEM_PAIR, _sibling_of_me,
                      lambda s, r_: exchange(full_ref, full_ref, s, r_))
        return full_ref[...]

    def body(in_ref, out_ref, send_sem, recv_sem):
        exchange(in_ref, out_ref, send_sem, recv_sem)

    return pl.pallas_call(
        body, out_shape=jax.ShapeDtypeStruct((r, cols), full.dtype), in_specs=[ANY], out_specs=ANY,
        input_output_aliases={0: 0}, scratch_shapes=list(ONE_SEM_PAIR), name="grad_share_sibling",
    )(full)


def _all_reduce_small(v):
    r, cols = v.shape
    n_dev = 8

    def body(x_ref, out_ref, gat_ref, send_sems, recv_sems, local_sem):
        x, y, c, chips = _place()
        me, sibling = (x, y, c), (x, y, 1 - c)

        def slot(px, py, pc):
            return gat_ref.at[4 * px + 2 * py + pc]

        def copy(k, block, to, src=None):
            return pltpu.make_async_remote_copy(src_ref=slot(*block) if src is None else src, dst_ref=slot(*block),
                                                send_sem=send_sems.at[k], recv_sem=recv_sems.at[k], device_id=to,
                                                device_id_type=MESH_ID)

        mine = pltpu.make_async_copy(x_ref, slot(*me), local_sem)
        mine.start()
        first = [copy(0, me, sibling, src=x_ref)]
        first += [copy(1 + k, me, (*chip, c), src=x_ref) for k, chip in enumerate(chips)]
        for cp in first:
            cp.start()
        passed = [copy(4 + k, (*chip, c), sibling) for k, chip in enumerate(chips)]
        for k, chip in enumerate(chips):
            copy(1 + k, (*chip, c), me).wait_recv()
            passed[k].start()
        copy(0, sibling, me).wait_recv()
        for k, chip in enumerate(chips):
            copy(4 + k, (*chip, 1 - c), me).wait_recv()
        for cp in first + passed:
            cp.wait_send()
        mine.wait()
        acc = gat_ref[0]
        for d in range(1, n_dev):
            acc = acc + gat_ref[d]
        out_ref[...] = acc

    vmem = pl.BlockSpec(memory_space=pltpu.VMEM)
    return pl.pallas_call(
        body, out_shape=jax.ShapeDtypeStruct((r, cols), v.dtype), in_specs=[vmem], out_specs=vmem,
        scratch_shapes=[pltpu.VMEM((n_dev, r, cols), v.dtype), pltpu.SemaphoreType.DMA((7,)),
                        pltpu.SemaphoreType.DMA((7,)), pltpu.SemaphoreType.DMA],
        name="all_reduce_small",
    )(v)


WIRE_DTYPE = jnp.bfloat16


def _add_own_half(gall, got, place):
    _, r, cols = gall.shape
    hr = r // 2
    tr = _row_tile(hr)
    g4 = gall.reshape(N_CHIPS, 2, hr, cols)

    def body(s_ref, g_ref, a_ref, o_ref):
        o_ref[...] = (g_ref[0] + a_ref[...]).astype(o_ref.dtype)

    return pl.pallas_call(
        body, out_shape=jax.ShapeDtypeStruct((N_CHIPS, hr, cols), WIRE_DTYPE),
        grid_spec=pltpu.PrefetchScalarGridSpec(
            num_scalar_prefetch=1, grid=(N_CHIPS, hr // tr),
            in_specs=[pl.BlockSpec((1, 1, tr, cols), lambda j, i, s: (j, s[1], i, 0)),
                      pl.BlockSpec((1, tr, cols), lambda j, i, s: (j, i, 0))],
            out_specs=pl.BlockSpec((1, tr, cols), lambda j, i, s: (j, i, 0))),
        compiler_params=_params(2), name="grad_add_halves",
    )(place, g4, got)


def _add_chip_parts(gall, got, parts, place):
    _, r, cols = gall.shape
    hr = r // 2
    tr = _row_tile(hr)
    g4 = gall.reshape(N_CHIPS, 2, hr, cols)
    nt = hr // tr

    def body(s_ref, g_ref, a_ref, b0_ref, b1_ref, b2_ref, o_ref):
        own = g_ref[0, 0] + a_ref[0]
        o_ref[...] = ((own + b0_ref[0].astype(F32)) + b1_ref[0].astype(F32)) + b2_ref[0].astype(F32)

    part = lambda k: pl.BlockSpec((1, tr, cols), lambda i, s: (k, i, 0))
    return pl.pallas_call(
        body, out_shape=jax.ShapeDtypeStruct((r, cols), F32),
        grid_spec=pltpu.PrefetchScalarGridSpec(
            num_scalar_prefetch=1, grid=(nt,),
            in_specs=[pl.BlockSpec((1, 1, tr, cols), lambda i, s: (s[0], s[1], i, 0)),
                      pl.BlockSpec((1, tr, cols), lambda i, s: (s[0], i, 0)), part(0), part(1), part(2)],
            out_specs=pl.BlockSpec((tr, cols), lambda i, s: (s[1] * nt + i, 0))),
        compiler_params=_params(1), name="grad_add_chips",
    )(place, g4, got, parts, parts, parts)


def _pack_shards(weights, l, group, dtype):
    parts = []
    for name, _ in group:
        w = weights[name][l]
        parts.append((w.T if name in COLUMN_SHARDED else w).reshape(-1, D_MODEL).astype(dtype))
    return parts[0] if len(parts) == 1 else jnp.concatenate(parts, axis=0)


def _unpack_gathered(gathered, group):
    w, off = {}, 0
    for name, rows in group:
        blk = gathered[:, off:off + rows]
        if name == "w_attn_proj":
            w[name] = blk.reshape(N_CHIPS * 256, ATTN_OUT)
        else:
            w[name] = blk.reshape(N_CHIPS * rows, D_MODEL)
        off += rows
    return w


def _pack_full_grads(g, group):
    parts = [g[name].reshape(N_CHIPS, rows, D_MODEL) for name, rows in group]
    return parts[0] if len(parts) == 1 else jnp.concatenate(parts, axis=1)


def _unpack_shard_grads(fulls, gi):
    out, off = {}, 0
    for name, rows in PACK_GROUPS[gi]:
        blks = [full[off:off + rows] for full in fulls]
        out[name] = jnp.stack([b.reshape(256, ATTN_OUT) if name == "w_attn_proj" else b for b in blks])
        off += rows
    return out


class _StaticWeights:
    def __init__(self, layers):
        self.layers = layers

    def group(self, l, gi):
        return self.layers[l]

    def layer(self, l):
        return self.layers[l]


class _WeightGathers:
    def __init__(self, weights):
        self.flats = {(l, gi): _pack_shards(weights, l, g, MXU_DTYPE)
                      for l in range(DEPTH) for gi, g in enumerate(PACK_GROUPS)}
        self.order = sorted(self.flats)
        self.landed, self.unpacked, self.issued = {}, {}, 0
        self._issue()

    def _issue(self, after=None):
        key = self.order[self.issued]
        flat = self.flats[key]
        if self.issued == 0:
            self.landed[key] = _all_gather_rows(flat)
        else:
            prev = self.order[self.issued - 1]
            ties = (self.landed[prev], flat) if after is None else (self.landed[prev], flat, after)
            tied = lax.optimization_barrier(ties)
            self.landed[prev], flat = tied[0], tied[1]
            after = tied[2] if after is not None else None
            self.landed[key] = _all_gather_rows(flat, behind=(1, f"l{key[0]}g{key[1]}"))
        self.issued += 1
        return after

    def prefetch(self, after):
        return self._issue(after) if self.issued < len(self.order) else after

    def group(self, l, gi):
        key = (l, gi)
        if key not in self.unpacked:
            self.unpacked[key] = _unpack_gathered(self.landed[key], PACK_GROUPS[gi])
        return self.unpacked[key]

    def layer(self, l):
        w = {}
        for gi in range(len(PACK_GROUPS)):
            w.update(self.group(l, gi))
        return w


class _GradReduction:
    LAST = 3

    def __init__(self, gw, group, place, behind):
        self.gall, self.place, self.behind, self.stage = _pack_full_grads(gw, group), place, behind, 0
        self.pending = None

    def _how(self, k):
        return None if self.behind is None else (self.behind[0][k], self.behind[1])

    def advance(self, value=None):
        if self.stage > self.LAST:
            return value
        if self.stage > 0 and value is not None:
            value, self.pending = lax.optimization_barrier((value, self.pending))
        if self.stage == 0:
            self.pending = _swap_halves(self.gall, self._how(0))
        elif self.stage == 1:
            self.got = self.pending
            self.pending = _scatter_to_chips(_add_own_half(self.gall, self.got, self.place), self._how(1))
        elif self.stage == 2:
            self.pending = _share_with_sibling(
                _add_chip_parts(self.gall, self.got, self.pending, self.place), self._how(2))
        self.stage += 1
        return value

    def result(self):
        while self.stage < self.LAST:
            self.advance()
        return self.pending


class _GradReductions:
    def __init__(self, place):
        self.place, self.all = place, {}

    def start(self, l, gi, gw):
        slot = len(self.all) % 2
        how = ((2 + 3 * slot, 3 + 3 * slot, 4 + 3 * slot), f"l{l}g{gi}")
        red = _GradReduction(gw, PACK_GROUPS[gi], self.place, how)
        red.advance()
        self.all[(l, gi)] = red

    def tick(self, value):
        for red in self.all.values():
            if red.behind is not None:
                value = red.advance(value)
        return value

    def results(self):
        return {key: red.result() for key, red in self.all.items()}


def _layer_fwd(x, x_b, weights, l, lnp, bias, rc, prefetch):
    tag = f"l{l}"
    w_in = weights.group(l, IN_GROUP)["w_in"]
    z = {}
    for part, secs in enumerate(DX_PARTS):
        runs = tuple(run for _, sec_runs, _, _ in secs for run in sec_runs)
        outs = _in_proj(x_b, _section_rows(w_in, runs), _section_rows(lnp["b_in"], runs), secs,
                        name=f"in_proj_{part}_{tag}")
        z.update({name: o for (name, _, _, _), o in zip(secs, outs)})
        if part == 0:
            z["a0"] = prefetch(z["a0"])
    w = weights.group(l, REST_GROUP)
    o_list, l_list = [], []
    for g, (_, dil) in enumerate(ATTN_GROUPS):
        o, lse = _attn_fwd(z[f"a{g}"], bias, g, dil)
        o_list.append(o)
        l_list.append(lse)
    ya_b, ya_views, lt_views = _attn_combine(o_list, l_list)
    yb_b, o_ret, states = _ret_fwd(z["bq"], z["bk"], z["c"], z["d"], rc)
    x1, u1, a1, a2, mg_b, x1_b = _mix_fwd(x, ya_b, yb_b, z["e"], w["w_attn_proj"], w["w_ret_proj"], w["w_out"],
                                          lnp["ln1_g"], lnp["ln1_b"])
    x1 = prefetch(x1)
    x2, u2, fa, fb, h_b, x2_b = _ffn_fwd(x1, w["w_ffn_gate"], w["w_ffn_up"], w["w_ffn_down"], lnp["ln2_g"],
                                         lnp["ln2_b"])
    saved = dict(x_b=x_b, z=z, ya_views=ya_views, ya_b=ya_b, lt_views=lt_views, yb_b=yb_b, o_ret=o_ret, states=states,
                 x1_b=x1_b, u1=u1, a1=a1, a2=a2, mg_b=mg_b, u2=u2, fa=fa, fb=fb, h_b=h_b)
    return x2, x2_b, saved


def _layer_bwd(dx2, w, l, lnp, sv, bias, rc, reductions=None):
    s = dx2.shape[0]
    tag = f"l{l}"
    z = sv["z"]
    step = reductions.tick if reductions is not None else (lambda v: v)
    dx1, du2_b, da_b, db_b, dg2, dbt2 = _ffn_bwd(dx2, sv["u2"], sv["fa"], sv["fb"], w["w_ffn_gate"], w["w_ffn_up"],
                                                  w["w_ffn_down"], lnp["ln2_g"])
    dx1 = step(dx1)
    gw = {}
    gw["w_ffn_down"] = _matmul(sv["h_b"], du2_b, "tn", name=f"dw_ffn_down_{tag}")
    gw["w_ffn_gate"] = _matmul(da_b, sv["x1_b"], "tn", name=f"dw_ffn_gate_{tag}")
    gw["w_ffn_up"] = _matmul(db_b, sv["x1_b"], "tn", name=f"dw_ffn_up_{tag}")
    dres, du1_b, da1_b, da2_b, dze, dyb, dg1, dbt1, dya_views = _mix_bwd(
        dx1, sv["u1"], sv["a1"], sv["a2"], z["e"], w["w_attn_proj"], w["w_ret_proj"], w["w_out"], lnp["ln1_g"])
    gw["w_out"] = _matmul(sv["mg_b"], du1_b, "tn", name=f"dw_out_{tag}")
    gw["w_attn_proj"] = _matmul(da1_b, sv["ya_b"], "tn", name=f"dw_attn_proj_{tag}")
    gw["w_ret_proj"] = _matmul(sv["yb_b"], da2_b, "tn", name=f"dw_ret_proj_{tag}")
    dyb = step(dyb)
    if reductions is not None:
        reductions.start(l, REST_GROUP, gw)
    dzq, dzk, dzv, dzg = _ret_bwd(z["bq"], z["bk"], z["c"], z["d"], sv["o_ret"], sv["states"], dyb, rc)
    dzq = step(dzq)
    dz = {"bq": dzq, "bk": dzk, "c": dzv, "d": dzg, "e": dze}
    dbias_l = []
    for g, (_, dil) in enumerate(ATTN_GROUPS):
        dz[f"a{g}"], dbg = _attn_bwd(z[f"a{g}"], bias, dya_views[g], sv["ya_views"][g], sv["lt_views"][g], g, dil)
        dbias_l.append(dbg)
    last = f"a{N_GROUPS - 1}"
    dz[last] = step(dz[last])
    dx = dres
    for part, secs in enumerate(DX_PARTS):
        runs = tuple(run for _, sec_runs, _, _ in secs for run in sec_runs)
        dx = _dx_in_proj(dx, [dz[name] for name, _, _, _ in secs], _section_rows(w["w_in"], runs), secs,
                         name=f"dx_in_proj_{part}_{tag}")
    dx = step(dx)
    dw, db = {}, {}
    for name, _, _, dil in SECTIONS:
        dw[name], db[name] = _matmul(dz[name], sv["x_b"], "tn", colsum=True, a_phase=dil,
                                     name=f"dw_in_proj_{name}_{tag}")
    gw["w_in"] = _unsection(dw, axis=0)
    if reductions is not None:
        reductions.start(l, IN_GROUP, gw)
    small = dict(b_in=_unsection(db, axis=1), ln1_g=dg1, ln1_b=dbt1, ln2_g=dg2, ln2_b=dbt2)
    return dx, gw, small, jnp.concatenate(dbias_l, axis=0)


def _forward_backward(x, target, rel_bias, weights, lnps, reductions=None):
    s = x.shape[0]
    bmaps = _bucket_maps()
    bias = _bias_tiles(rel_bias, bmaps)
    rc = _ret_consts(s)
    prefetch = getattr(weights, "prefetch", lambda v: v)
    saved = []
    h, h_b = x, x.astype(MXU_DTYPE)
    for l in range(DEPTH):
        h, h_b, sv = _layer_fwd(h, h_b, weights, l, lnps[l], bias, rc, prefetch)
        saved.append(sv)
    dh, loss_part = _loss_head(h, target)
    gws, smalls, dbiases = [None] * DEPTH, [None] * DEPTH, [None] * DEPTH
    for l in reversed(range(DEPTH)):
        dh, gws[l], smalls[l], dbiases[l] = _layer_bwd(dh, weights.layer(l), l, lnps[l], saved[l], bias, rc,
                                                       reductions)
    d_rel_bias = _bias_tiles_bwd(dbiases, bmaps)
    return loss_part, dh, gws, smalls, d_rel_bias


SMALL_NAMES = ("rel_bias", "b_in", "ln1_g", "ln1_b", "ln2_g", "ln2_b")
SMALL_ROWS = 32


def _pack_small(vals):
    flat = jnp.concatenate([vals[n].reshape(-1) for n in SMALL_NAMES])
    return jnp.pad(flat, (0, SMALL_ROWS * D_MODEL - flat.shape[0])).reshape(SMALL_ROWS, D_MODEL)


def _unpack_small(packed, like):
    flat = packed.reshape(-1)
    out, off = {}, 0
    for n in SMALL_NAMES:
        size = like[n].size
        out[n] = flat[off:off + size].reshape(like[n].shape)
        off += size
    return out


def kernel(x, rel_bias, w_in, b_in, w_attn_proj, w_ret_proj, w_out, ln1_g, ln1_b, w_ffn_gate, w_ffn_up, w_ffn_down, ln2_g, ln2_b, loss_target, m_rel_bias, m_w_in, m_b_in, m_w_attn_proj, m_w_ret_proj, m_w_out, m_ln1_g, m_ln1_b, m_w_ffn_gate, m_w_ffn_up, m_w_ffn_down, m_ln2_g, m_ln2_b, v_rel_bias, v_w_in, v_b_in, v_w_attn_proj, v_w_ret_proj, v_w_out, v_ln1_g, v_ln1_b, v_w_ffn_gate, v_w_ffn_up, v_w_ffn_down, v_ln2_g, v_ln2_b):
    big = dict(w_in=(w_in, m_w_in, v_w_in), w_attn_proj=(w_attn_proj, m_w_attn_proj, v_w_attn_proj),
               w_ret_proj=(w_ret_proj, m_w_ret_proj, v_w_ret_proj), w_out=(w_out, m_w_out, v_w_out),
               w_ffn_gate=(w_ffn_gate, m_w_ffn_gate, v_w_ffn_gate), w_ffn_up=(w_ffn_up, m_w_ffn_up, v_w_ffn_up),
               w_ffn_down=(w_ffn_down, m_w_ffn_down, v_w_ffn_down))
    small_w = dict(rel_bias=rel_bias, b_in=b_in, ln1_g=ln1_g, ln1_b=ln1_b, ln2_g=ln2_g, ln2_b=ln2_b)
    small_m = dict(rel_bias=m_rel_bias, b_in=m_b_in, ln1_g=m_ln1_g, ln1_b=m_ln1_b, ln2_g=m_ln2_g, ln2_b=m_ln2_b)
    small_v = dict(rel_bias=v_rel_bias, b_in=v_b_in, ln1_g=v_ln1_g, ln1_b=v_ln1_b, ln2_g=v_ln2_g, ln2_b=v_ln2_b)

    place = jnp.stack([2 * lax.axis_index("x") + lax.axis_index("y"), lax.axis_index("c")]).astype(jnp.int32)

    weights = _WeightGathers({n: w for n, (w, _, _) in big.items()})
    reductions = _GradReductions(place)
    lnps = [dict(b_in=b_in[l][None], ln1_g=ln1_g[l][None], ln1_b=ln1_b[l][None], ln2_g=ln2_g[l][None],
                 ln2_b=ln2_b[l][None]) for l in range(DEPTH)]
    loss_part, dx, _, smalls, d_rel_bias = _forward_backward(x[0], loss_target[0], rel_bias, weights, lnps,
                                                             reductions)
    loss = lax.psum(loss_part, ("x", "y", "c"))

    small_g = dict(rel_bias=d_rel_bias)
    for n in SMALL_NAMES[1:]:
        small_g[n] = jnp.concatenate([smalls[l][n] for l in range(DEPTH)], axis=0)
    small_g = _unpack_small(_all_reduce_small(_pack_small(small_g)), small_w)
    grads, delta, new_m, new_v = dict(small_g), {}, {}, {}
    d_, m_, v_ = _adamw(_pack_small(small_w), _pack_small(small_g), _pack_small(small_m), _pack_small(small_v),
                        name="adamw_small")
    d_ = reductions.tick(d_)
    delta.update(_unpack_small(d_, small_w))
    new_m.update(_unpack_small(m_, small_w))
    new_v.update(_unpack_small(v_, small_w))

    def update(n, g):
        w, m, v = big[n]
        turn = (lambda t: jnp.swapaxes(t, 1, 2)) if n in COLUMN_SHARDED else (lambda t: t)
        two_d = lambda t: t.reshape(-1, t.shape[-1])
        d_, m_, v_ = _adamw(two_d(turn(w)), two_d(g), two_d(turn(m)), two_d(turn(v)), name=f"adamw_{n}")
        grads[n], delta[n], new_m[n], new_v[n] = (turn(t.reshape(g.shape)) for t in (g, d_, m_, v_))

    rest = _unpack_shard_grads([reductions.all[(l, REST_GROUP)].result() for l in range(DEPTH)], REST_GROUP)
    for i, (n, g) in enumerate(rest.items()):
        update(n, reductions.tick(g) if i == len(rest) // 2 else g)
    first = _unpack_shard_grads([reductions.all[(l, IN_GROUP)].result() for l in range(DEPTH)], IN_GROUP)
    for n, g in first.items():
        update(n, g)

    order = ("rel_bias", "w_in", "b_in", "w_attn_proj", "w_ret_proj", "w_out", "ln1_g", "ln1_b", "w_ffn_gate",
             "w_ffn_up", "w_ffn_down", "ln2_g", "ln2_b")
    return (loss, dx[None], *[grads[n] for n in order], *[delta[n] for n in order], *[new_m[n] for n in order],
            *[new_v[n] for n in order])
```

```python
import functools

import numpy as np
import jax
import jax.numpy as jnp
from jax import lax
from jax.experimental import pallas as pl
from jax.experimental.pallas import tpu as pltpu
from jax.experimental.pallas import tpu_sc as plsc

F32 = jnp.float32
MXU_DTYPE = jnp.bfloat16

DEPTH = 2
D_MODEL = 1024
HEAD_DIM = 64
ATTN_GROUPS = ((128, 1), (512, 4), (2048, 16))
N_GROUPS = len(ATTN_GROUPS)
HEADS_PER_GROUP = 6
N_ATTN_HEADS = 18
ATTN_WIDTH = 1152
ATTN_OUT = 384
NUM_BUCKETS = 32
MAX_DISTANCE = 2048
RET_HEADS = 4
RET_QK = 256
RET_V = 512
RET_CHUNK = 128
ROPE_BASE = 10000.0
D_FF = 2816
IN_COLS = 11648
ALPHA = (2 * DEPTH) ** 0.25
LN_EPS = 1e-5
GN_EPS = 1e-5
ADAM_LR, ADAM_B1, ADAM_B2, ADAM_EPS, ADAM_WD, ADAM_STEP = 0.001, 0.9, 0.999, 1e-08, 0.01, 10

BLK = 128
NEG = -1e30
N_CHIPS = 4
VMEM_LIMIT = 48 * 1024 * 1024

SECTIONS = tuple(
    (f"a{g}", tuple((k * ATTN_WIDTH + g * ATTN_OUT, ATTN_OUT) for k in range(3)), True, dil)
    for g, (_, dil) in enumerate(ATTN_GROUPS)
) + (
    ("bq", ((3456, 1024),), False, 1),
    ("bk", ((4480, 1024),), False, 1),
    ("c", ((5504, 2048),), True, 1),
    ("d", ((7552, 2048),), False, 1),
    ("e", ((9600, 2048),), False, 1),
)
DX_PARTS = (SECTIONS[:5], SECTIONS[5:])
GROUP_WIDTH = 3 * ATTN_OUT
LANES = 128


def _section_rows(w, runs):
    axis = 1 if w.shape[0] == 1 else 0
    parts = [lax.slice_in_dim(w, first, first + width, axis=axis) for first, width in runs]
    return parts[0] if len(parts) == 1 else jnp.concatenate(parts, axis=axis)


def _unsection(pieces, axis):
    runs = []
    for name, sec_runs, _, _ in SECTIONS:
        off = 0
        for first, width in sec_runs:
            runs.append((first, lax.slice_in_dim(pieces[name], off, off + width, axis=axis)))
            off += width
    return jnp.concatenate([p for _, p in sorted(runs, key=lambda t: t[0])], axis=axis)
PACK_GROUPS = ((("w_in", 2912),),
               (("w_attn_proj", 96), ("w_ret_proj", 512), ("w_out", 256), ("w_ffn_gate", 704), ("w_ffn_up", 704),
                ("w_ffn_down", 704)))
IN_GROUP, REST_GROUP = 0, 1
COLUMN_SHARDED = ("w_in", "w_attn_proj", "w_ffn_gate", "w_ffn_up")

NN = ((1,), (0,))
NT = ((1,), (1,))
TN = ((0,), (0,))


def _dot(a, b, dims):
    return lax.dot_general(a.astype(MXU_DTYPE), b.astype(MXU_DTYPE), (dims, ((), ())),
                           preferred_element_type=F32)


def _pick(n, prefs):
    for p in prefs:
        if n % p == 0:
            return p
    raise ValueError(f"no tile for {n} among {prefs}")


def _row_tile(n, most=512, unit=16):
    return max(t for t in range(unit, most + 1, unit) if n % t == 0)


TOKEN_TILES = (1024, 512, 256, 128)
FEATURE_TILES = (1152, 1024, 1408, 384, 256, 128)


def _params(n_axes, limit=VMEM_LIMIT):
    return pltpu.CompilerParams(dimension_semantics=("arbitrary",) * n_axes, vmem_limit_bytes=limit)


def _resident(a):
    return pl.BlockSpec(a.shape, lambda i: (0,) * a.ndim, pipeline_mode=pl.Buffered(1))


def _sigmoid(x):
    return 1.0 / (1.0 + jnp.exp(-x))


def _norm_rows(u, eps):
    mu = jnp.mean(u, axis=-1, keepdims=True)
    xc = u - mu
    var = jnp.mean(xc * xc, axis=-1, keepdims=True)
    rstd = lax.rsqrt(var + eps)
    return xc * rstd, rstd


def _norm_rows_bwd(dxh, xh, rstd):
    c1 = jnp.mean(dxh, axis=-1, keepdims=True)
    c2 = jnp.mean(dxh * xh, axis=-1, keepdims=True)
    return rstd * (dxh - c1 - xh * c2)


def _phase_to_tokens(src_ref, scr, dil, width):
    n = scr.shape[1] // dil
    for r in range(dil):
        for cb in range(width // LANES):
            col = r * width + cb * LANES
            scr.at[cb][pl.ds(r, n, stride=dil), :] = src_ref[:, col:col + LANES].astype(F32)


def _tokens_to_phase(scr, dst_ref, dil, width):
    n = scr.shape[1] // dil
    for r in range(dil):
        for cb in range(width // LANES):
            col = r * width + cb * LANES
            dst_ref[:, col:col + LANES] = scr.at[cb][pl.ds(r, n, stride=dil), :].astype(dst_ref.dtype)


def _blocks_to_cols(scr):
    return jnp.concatenate([scr[cb] for cb in range(scr.shape[0])], axis=1)


def _cols_to_blocks(val, scr):
    for cb in range(scr.shape[0]):
        scr[cb] = val[:, cb * LANES:(cb + 1) * LANES]


def _matmul(a, b, mode, *, name, out_dtype=F32, bias=None, addend=None, colsum=False, a_phase=1, out_phase=1):
    d_a, d_o = a_phase, out_phase
    if mode == "tn":
        kd, m = a.shape[0] * d_a, a.shape[1] // d_a
        n = b.shape[1]
        tm, tn_ = (m if d_a > 1 else _pick(m, FEATURE_TILES)), _pick(n, FEATURE_TILES)
        tk = _pick(kd, TOKEN_TILES if d_a > 1 else (2048,) + TOKEN_TILES)
        a_spec = (pl.BlockSpec((tk // d_a, d_a * m), lambda i, j, k: (k, 0)) if d_a > 1 else
                  pl.BlockSpec((tk, tm), lambda i, j, k: (k, i)))
        b_spec = pl.BlockSpec((tk, tn_), lambda i, j, k: (k, j))
        feat, tok = m, tk
    else:
        m, kd = a.shape[0] * d_a, a.shape[1] // d_a
        n = b.shape[0] if mode == "nt" else b.shape[1]
        tm, tn_ = _pick(m, TOKEN_TILES), (n if d_o > 1 else _pick(n, FEATURE_TILES))
        tk = kd if d_a > 1 else _pick(kd, FEATURE_TILES)
        a_spec = (pl.BlockSpec((tm // d_a, d_a * kd), lambda i, j, k: (i, 0)) if d_a > 1 else
                  pl.BlockSpec((tm, tk), lambda i, j, k: (i, k)))
        if mode == "nt":
            b_spec = pl.BlockSpec((tn_, tk), lambda i, j, k: (j, k))
        else:
            b_spec = pl.BlockSpec((tk, tn_), lambda i, j, k: (k, j))
        feat, tok = kd, tm
    dims = {"nn": NN, "nt": NT, "tn": TN}[mode]
    nk = kd // tk
    has_bias, has_add = bias is not None, addend is not None
    assert not colsum or mode == "tn"
    assert d_o == 1 or (mode == "nt" and not has_add)

    def body(*refs):
        it = iter(refs)
        a_ref, b_ref = next(it), next(it)
        bias_ref = next(it) if has_bias else None
        add_ref = next(it) if has_add else None
        o_ref = next(it)
        cs_ref = next(it) if colsum else None
        acc_ref = next(it)
        a_scr = next(it) if d_a > 1 else None
        o_scr = next(it) if d_o > 1 else None
        j, k = pl.program_id(1), pl.program_id(2)

        @pl.when(k == 0)
        def _():
            if has_add:
                acc_ref[...] = add_ref[...].astype(F32)
            else:
                acc_ref[...] = jnp.zeros_like(acc_ref)

        if d_a > 1:
            _phase_to_tokens(a_ref, a_scr, d_a, feat)
            av = _blocks_to_cols(a_scr)
        else:
            av = a_ref[...]
        acc_ref[...] += _dot(av, b_ref[...], dims)
        if colsum:
            @pl.when(jnp.logical_and(j == 0, k == 0))
            def _():
                cs_ref[...] = jnp.zeros_like(cs_ref)

            @pl.when(j == 0)
            def _():
                cs_ref[...] += jnp.sum(av.astype(F32), axis=0, keepdims=True)

        @pl.when(k == nk - 1)
        def _():
            r = acc_ref[...]
            if has_bias:
                r = r + bias_ref[...]
            if d_o > 1:
                _cols_to_blocks(r, o_scr)
                _tokens_to_phase(o_scr, o_ref, d_o, n)
            else:
                o_ref[...] = r.astype(out_dtype)

    in_specs, args = [a_spec, b_spec], [a, b]
    if has_bias:
        in_specs.append(pl.BlockSpec((1, tn_), lambda i, j, k: (0, j)))
        args.append(bias)
    if has_add:
        in_specs.append(pl.BlockSpec((tm, tn_), lambda i, j, k: (i, j)))
        args.append(addend)
    if d_o > 1:
        out_shape = [jax.ShapeDtypeStruct((m // d_o, d_o * n), out_dtype)]
        out_specs = [pl.BlockSpec((tm // d_o, d_o * n), lambda i, j, k: (i, 0))]
    else:
        out_shape = [jax.ShapeDtypeStruct((m, n), out_dtype)]
        out_specs = [pl.BlockSpec((tm, tn_), lambda i, j, k: (i, j))]
    if colsum:
        out_shape.append(jax.ShapeDtypeStruct((1, m), F32))
        out_specs.append(pl.BlockSpec((1, tm), lambda i, j, k: (0, i)))
    scratch = [pltpu.VMEM((tm, tn_), F32)]
    if d_a > 1:
        scratch.append(pltpu.VMEM((feat // LANES, tok, LANES), F32))
    if d_o > 1:
        scratch.append(pltpu.VMEM((n // LANES, tm, LANES), F32))
    res = pl.pallas_call(
        body, out_shape=out_shape, grid=(m // tm, n // tn_, nk), in_specs=in_specs, out_specs=out_specs,
        scratch_shapes=scratch, compiler_params=_params(3), name=name,
    )(*args)
    return res if colsum else res[0]


def _in_proj(x_b, w_rows, bias_cols, sections, name):
    s = x_b.shape[0]
    tm = 512
    widths = [sum(w for _, w in runs) for _, runs, _, _ in sections]
    dils = [dil for _, _, _, dil in sections]
    offs = [sum(widths[:i]) for i in range(len(widths))]
    widest_phased = max([w for w, d in zip(widths, dils) if d > 1], default=0)

    def body(x_ref, w_ref, b_ref, *rest):
        o_refs = rest[:len(sections)]
        scr = rest[len(sections)] if widest_phased else None
        xv = x_ref[...]
        for o_ref, width, dil, off in zip(o_refs, widths, dils, offs):
            r = _dot(xv, w_ref[off:off + width, :], NT) + b_ref[:, off:off + width]
            if dil > 1:
                _cols_to_blocks(r, scr)
                _tokens_to_phase(scr, o_ref, dil, width)
            else:
                o_ref[...] = r.astype(o_ref.dtype)

    out_shape = [jax.ShapeDtypeStruct((s // d, d * w), MXU_DTYPE if narrow else F32)
                 for w, d, (_, _, narrow, _) in zip(widths, dils, sections)]
    out_specs = [pl.BlockSpec((tm // d, d * w), lambda i: (i, 0)) for w, d in zip(widths, dils)]
    scratch = [pltpu.VMEM((widest_phased // LANES, tm, LANES), F32)] if widest_phased else []
    return pl.pallas_call(
        body, out_shape=out_shape, grid=(s // tm,),
        in_specs=[pl.BlockSpec((tm, D_MODEL), lambda i: (i, 0)), _resident(w_rows), _resident(bias_cols)],
        out_specs=out_specs, scratch_shapes=scratch, compiler_params=_params(1), name=name,
    )(x_b, w_rows, bias_cols)


def _dx_in_proj(addend, dzs, w_rows, sections, name):
    s = addend.shape[0]
    tm = 512
    widths = [sum(w for _, w in runs) for _, runs, _, _ in sections]
    dils = [dil for _, _, _, dil in sections]
    offs = [sum(widths[:i]) for i in range(len(widths))]
    widest_phased = max([w for w, d in zip(widths, dils) if d > 1], default=0)

    def body(*refs):
        add_ref, w_ref, o_ref = refs[0], refs[1], refs[2 + len(dzs)]
        dz_refs = refs[2:2 + len(dzs)]
        scr = refs[3 + len(dzs)] if widest_phased else None
        acc = add_ref[...]
        for dz_ref, width, dil, off in zip(dz_refs, widths, dils, offs):
            if dil > 1:
                _phase_to_tokens(dz_ref, scr, dil, width)
                av = _blocks_to_cols(scr)
            else:
                av = dz_ref[...]
            acc = acc + _dot(av, w_ref[off:off + width, :], NN)
        o_ref[...] = acc

    row = pl.BlockSpec((tm, D_MODEL), lambda i: (i, 0))
    dz_specs = [pl.BlockSpec((tm // d, d * w), lambda i: (i, 0)) for w, d in zip(widths, dils)]
    scratch = [pltpu.VMEM((widest_phased // LANES, tm, LANES), F32)] if widest_phased else []
    return pl.pallas_call(
        body, out_shape=jax.ShapeDtypeStruct((s, D_MODEL), F32), grid=(s // tm,),
        in_specs=[row, _resident(w_rows)] + dz_specs, out_specs=row, scratch_shapes=scratch,
        compiler_params=_params(1), name=name,
    )(addend, w_rows, *dzs)


def _t5_bucket(dist):
    max_exact = NUM_BUCKETS // 2
    large = max_exact + (np.log(np.maximum(dist, max_exact) / max_exact)
                         / np.log(MAX_DISTANCE / max_exact) * (NUM_BUCKETS - max_exact)).astype(np.int32)
    large = np.minimum(large, NUM_BUCKETS - 1)
    return np.where(dist < max_exact, dist, large).astype(np.int32)


def _bucket_maps():
    qi = np.arange(BLK)[:, None]
    kj = np.arange(2 * BLK)[None, :]
    rel = np.clip(qi + BLK - kj, 0, BLK)
    return jnp.asarray(np.stack([_t5_bucket(rel * d) for _, d in ATTN_GROUPS]))


def _bias_tiles(rel_bias, bmaps):
    def body(tab_ref, bm_ref, o_ref):
        h = pl.program_id(0)
        bm = bm_ref[0]
        acc = jnp.zeros((BLK, 2 * BLK), F32)
        for b in range(NUM_BUCKETS):
            acc = jnp.where(bm == b, tab_ref[b, h], acc)
        o_ref[0] = acc

    return pl.pallas_call(
        body, out_shape=jax.ShapeDtypeStruct((N_ATTN_HEADS, BLK, 2 * BLK), F32), grid=(N_ATTN_HEADS,),
        in_specs=[pl.BlockSpec(memory_space=pltpu.SMEM),
                  pl.BlockSpec((1, BLK, 2 * BLK), lambda h: (h // HEADS_PER_GROUP, 0, 0))],
        out_specs=pl.BlockSpec((1, BLK, 2 * BLK), lambda h: (h, 0, 0)),
        compiler_params=_params(1), name="bias_tiles",
    )(rel_bias, bmaps)


def _bias_tiles_bwd(dbias_layers, bmaps):
    nl = len(dbias_layers)

    def body(*refs):
        bm = refs[nl][0]
        o_ref = refs[nl + 1]
        x = refs[0][0]
        for r in refs[1:nl]:
            x = x + r[0]
        lane = lax.broadcasted_iota(jnp.int32, (1, BLK), 1)
        row = jnp.zeros((1, BLK), F32)
        for b in range(NUM_BUCKETS):
            s = jnp.sum(jnp.where(bm == b, x, 0.0), axis=1, keepdims=True)
            s = jnp.sum(s, axis=0, keepdims=True)
            row = jnp.where(lane == b, s, row)
        o_ref[0] = row

    tile = pl.BlockSpec((1, BLK, 2 * BLK), lambda h: (h, 0, 0))
    out = pl.pallas_call(
        body, out_shape=jax.ShapeDtypeStruct((N_ATTN_HEADS, 1, BLK), F32), grid=(N_ATTN_HEADS,),
        in_specs=[tile] * nl + [pl.BlockSpec((1, BLK, 2 * BLK), lambda h: (h // HEADS_PER_GROUP, 0, 0))],
        out_specs=pl.BlockSpec((1, 1, BLK), lambda h: (h, 0, 0)),
        compiler_params=_params(1), name="bias_tiles_bwd",
    )(*dbias_layers, bmaps)
    return out[:, 0, :NUM_BUCKETS].T


def _pair_masks():
    lane = lax.broadcasted_iota(jnp.int32, (BLK, BLK), 1)
    row2 = lax.broadcasted_iota(jnp.int32, (2 * BLK, BLK), 0)
    lane2 = lax.broadcasted_iota(jnp.int32, (2 * BLK, BLK), 1)
    own = (lane2 // HEAD_DIM) == (row2 // BLK)
    qi = lax.broadcasted_iota(jnp.int32, (2 * BLK, 2 * BLK), 0) & (BLK - 1)
    kj = lax.broadcasted_iota(jnp.int32, (2 * BLK, 2 * BLK), 1)
    band = jnp.logical_and(kj >= qi, kj <= qi + BLK)
    return lane < HEAD_DIM, own, band, kj < BLK


def _pair_scores(q32, kb, bias2, own, band, is_prev, pen):
    qm = jnp.where(own, jnp.concatenate([q32, q32], axis=0), 0.0)
    s = _dot(qm, kb, NT) * (HEAD_DIM ** -0.5) + bias2
    if pen is not None:
        s = s + jnp.where(is_prev, pen, 0.0)
    return jnp.where(band, s, NEG), qm


HEAD_PAIRS = HEADS_PER_GROUP // 2


def _attn_fwd(qkv, bias, g, dil):
    rows = qkv.shape[0]
    nb = rows // BLK
    rb = 2 if nb % 2 == 0 else 1
    cur = pl.BlockSpec((rb * BLK, GROUP_WIDTH), lambda r, n: (n, r))
    prev = pl.BlockSpec((BLK, GROUP_WIDTH), lambda r, n: (jnp.maximum(rb * n - 1, 0), r))

    def body(c_ref, p_ref, b_ref, o_ref, l_ref):
        n = pl.program_id(1)
        pen0 = jnp.where(n > 0, 0.0, NEG)
        first, own, band, is_prev = _pair_masks()
        o_rows, l_rows = [], []
        for t in range(rb):
            rows_t = slice(t * BLK, (t + 1) * BLK)
            o_parts, l_parts = [], []
            for hp in range(HEAD_PAIRS):
                qc, kc, vc = (slice(k * ATTN_OUT + hp * BLK, k * ATTN_OUT + (hp + 1) * BLK) for k in range(3))
                if t == 0:
                    kp, vp, pen = p_ref[:, kc], p_ref[:, vc], pen0
                else:
                    before = slice((t - 1) * BLK, t * BLK)
                    kp, vp, pen = c_ref[before, kc], c_ref[before, vc], None
                kb = jnp.concatenate([kp, c_ref[rows_t, kc]], axis=0)
                vb = jnp.concatenate([vp, c_ref[rows_t, vc]], axis=0)
                bias2 = jnp.concatenate([b_ref[2 * hp], b_ref[2 * hp + 1]], axis=0)
                s, _ = _pair_scores(c_ref[rows_t, qc].astype(F32), kb, bias2, own, band, is_prev, pen)
                m = jnp.max(s, axis=1, keepdims=True)
                p = jnp.exp(s - m)
                l = jnp.sum(p, axis=1, keepdims=True)
                o2 = _dot(p * (1.0 / l), vb, NN)
                lse2 = m + jnp.log(l)
                o_parts.append(jnp.where(first, o2[:BLK], o2[BLK:]))
                l_parts.append(jnp.where(first, lse2[:BLK], lse2[BLK:]))
            o_rows.append(jnp.concatenate(o_parts, axis=1))
            l_rows.append(jnp.concatenate(l_parts, axis=1))
        o_ref[...] = jnp.concatenate(o_rows, axis=0)
        l_ref[...] = jnp.concatenate(l_rows, axis=0)

    out_spec = pl.BlockSpec((rb * BLK, ATTN_OUT), lambda r, n: (n, r))
    shape = jax.ShapeDtypeStruct((rows, dil * ATTN_OUT), F32)
    return pl.pallas_call(
        body, out_shape=[shape, shape], grid=(dil, nb // rb),
        in_specs=[cur, prev, pl.BlockSpec((HEADS_PER_GROUP, BLK, 2 * BLK), lambda r, n: (g, 0, 0))],
        out_specs=[out_spec, out_spec], compiler_params=_params(2), name=f"attn_fwd_g{g}",
    )(qkv, qkv, bias)


def _attn_bwd(qkv, bias, dy, ya, lt, g, dil):
    rows = qkv.shape[0]
    nb = rows // BLK
    rb = 2 if nb % 2 == 0 else 1
    ns = nb // rb
    cur = lambda w: pl.BlockSpec((rb * BLK, w), lambda r, n: (jnp.minimum(n, ns - 1), r))
    prev = pl.BlockSpec((BLK, GROUP_WIDTH), lambda r, n: (jnp.clip(rb * n - 1, 0, nb - 1), r))

    def body(c_ref, p_ref, b_ref, dy_ref, ya_ref, lt_ref, dz_ref, db_ref, cq_ref, ckv_full, ckv_part):
        r, n = pl.program_id(0), pl.program_id(1)

        @pl.when(jnp.logical_and(r == 0, n == 0))
        def _():
            db_ref[...] = jnp.zeros_like(db_ref)

        @pl.when(n == 0)
        def _():
            cq_ref[...] = jnp.zeros_like(cq_ref)
            ckv_full[...] = jnp.zeros_like(ckv_full)
            ckv_part[...] = jnp.zeros_like(ckv_part)

        def late(dk_add, dv_add):
            kv = ckv_part[...]
            k_last, v_last = kv[:, :ATTN_OUT] + dk_add, kv[:, ATTN_OUT:] + dv_add
            if rb == 1:
                return jnp.concatenate([cq_ref[...], k_last, v_last], axis=1)
            full = ckv_full[...]
            return jnp.concatenate([cq_ref[...],
                                    jnp.concatenate([full[:, :ATTN_OUT], k_last], axis=0),
                                    jnp.concatenate([full[:, ATTN_OUT:], v_last], axis=0)], axis=1)

        @pl.when(n < ns)
        def _():
            pen0 = jnp.where(n > 0, 0.0, NEG)
            first, own, band, is_prev = _pair_masks()
            second = jnp.logical_not(first)
            scale = HEAD_DIM ** -0.5
            db_sum, wides = None, []
            for t in range(rb):
                rows_t = slice(t * BLK, (t + 1) * BLK)
                parts = {k: [] for k in ("dq", "dkp", "dkc", "dvp", "dvc")}
                db_parts = []
                for hp in range(HEAD_PAIRS):
                    cols = slice(hp * BLK, (hp + 1) * BLK)
                    qc, kc, vc = (slice(k * ATTN_OUT + hp * BLK, k * ATTN_OUT + (hp + 1) * BLK) for k in range(3))
                    if t == 0:
                        kp, vp, pen = p_ref[:, kc], p_ref[:, vc], pen0
                    else:
                        before = slice((t - 1) * BLK, t * BLK)
                        kp, vp, pen = c_ref[before, kc], c_ref[before, vc], None
                    kb = jnp.concatenate([kp, c_ref[rows_t, kc]], axis=0)
                    vb = jnp.concatenate([vp, c_ref[rows_t, vc]], axis=0)
                    bias2 = jnp.concatenate([b_ref[2 * hp], b_ref[2 * hp + 1]], axis=0)
                    dy_, lt_ = dy_ref[rows_t, cols], lt_ref[rows_t, cols]
                    dyy = dy_ * ya_ref[rows_t, cols]
                    per_head = lambda v, red, fill: jnp.concatenate(
                        [red(jnp.where(first, v, fill), axis=1, keepdims=True),
                         red(jnp.where(second, v, fill), axis=1, keepdims=True)], axis=0)
                    lse2 = per_head(lt_, jnp.max, NEG)
                    delta2 = per_head(dyy, jnp.sum, 0.0)
                    s, qm = _pair_scores(c_ref[rows_t, qc].astype(F32), kb, bias2, own, band, is_prev, pen)
                    p = jnp.exp(s - lse2)
                    dym = jnp.where(own, jnp.concatenate([dy_, dy_], axis=0), 0.0)
                    ds = p * (_dot(dym, vb, NT) - delta2)
                    db_parts += [ds[:BLK], ds[BLK:]]
                    dq2 = _dot(ds, kb, NN) * scale
                    dkb = _dot(ds, qm, TN) * scale
                    dvb = _dot(p, dym, TN)
                    for k, val in (("dq", jnp.where(first, dq2[:BLK], dq2[BLK:])), ("dkp", dkb[:BLK]),
                                   ("dkc", dkb[BLK:]), ("dvp", dvb[:BLK]), ("dvc", dvb[BLK:])):
                        parts[k].append(val)
                wides.append({k: jnp.concatenate(val, axis=1) for k, val in parts.items()})
                db_t = jnp.stack(db_parts, axis=0)
                db_sum = db_t if db_sum is None else db_sum + db_t
            db_ref[...] += db_sum
            dz_ref[...] = late(wides[0]["dkp"], wides[0]["dvp"]).astype(dz_ref.dtype)
            cq_ref[...] = jnp.concatenate([w["dq"] for w in wides], axis=0)
            if rb == 2:
                ckv_full[...] = jnp.concatenate([wides[0]["dkc"] + wides[1]["dkp"],
                                                 wides[0]["dvc"] + wides[1]["dvp"]], axis=1)
            ckv_part[...] = jnp.concatenate([wides[-1]["dkc"], wides[-1]["dvc"]], axis=1)

        @pl.when(n == ns)
        def _():
            zero = jnp.zeros((BLK, ATTN_OUT), F32)
            dz_ref[...] = late(zero, zero).astype(dz_ref.dtype)

    out_late = pl.BlockSpec((rb * BLK, GROUP_WIDTH), lambda r, n: (jnp.maximum(n - 1, 0), r))
    kv_carry = pltpu.VMEM((BLK, 2 * ATTN_OUT), F32)
    return pl.pallas_call(
        body,
        out_shape=[jax.ShapeDtypeStruct((rows, dil * GROUP_WIDTH), MXU_DTYPE),
                   jax.ShapeDtypeStruct((HEADS_PER_GROUP, BLK, 2 * BLK), F32)],
        grid=(dil, ns + 1),
        in_specs=[cur(GROUP_WIDTH), prev, pl.BlockSpec((HEADS_PER_GROUP, BLK, 2 * BLK), lambda r, n: (g, 0, 0)),
                  cur(ATTN_OUT), cur(ATTN_OUT), cur(ATTN_OUT)],
        out_specs=[out_late, pl.BlockSpec((HEADS_PER_GROUP, BLK, 2 * BLK), lambda r, n: (0, 0, 0))],
        scratch_shapes=[pltpu.VMEM((rb * BLK, ATTN_OUT), F32), kv_carry, kv_carry],
        compiler_params=_params(2), name=f"attn_bwd_g{g}",
    )(qkv, qkv, bias, dy, ya, lt)


def _attn_combine(o_list, l_list):
    dils = [d for _, d in ATTN_GROUPS]
    s = o_list[0].shape[0] * dils[0]
    tr = _pick(s, (512, 256, 128))
    n_g = len(dils)

    def body(*refs):
        o_refs, l_refs = refs[:n_g], refs[n_g:2 * n_g]
        yb_ref = refs[2 * n_g]
        y_refs, lt_refs = refs[2 * n_g + 1:3 * n_g + 1], refs[3 * n_g + 1:4 * n_g + 1]
        scr_o, scr_l = refs[4 * n_g + 1], refs[4 * n_g + 2]
        os_, ls_ = [], []
        for g in range(n_g):
            if dils[g] == 1:
                os_.append(o_refs[g][...])
                ls_.append(l_refs[g][...])
            else:
                _phase_to_tokens(o_refs[g], scr_o, dils[g], ATTN_OUT)
                _phase_to_tokens(l_refs[g], scr_l, dils[g], ATTN_OUT)
                os_.append(_blocks_to_cols(scr_o))
                ls_.append(_blocks_to_cols(scr_l))
        mx = functools.reduce(jnp.maximum, ls_)
        es = [jnp.exp(a - mx) for a in ls_]
        den = functools.reduce(lambda a, b: a + b, es)
        inv = 1.0 / den
        y = functools.reduce(lambda a, b: a + b, [(e * inv) * o for e, o in zip(es, os_)])
        lt = mx + jnp.log(den)
        yb_ref[...] = y.astype(yb_ref.dtype)
        _cols_to_blocks(y, scr_o)
        _cols_to_blocks(lt, scr_l)
        for g in range(n_g):
            if dils[g] == 1:
                y_refs[g][...] = y
                lt_refs[g][...] = lt
            else:
                _tokens_to_phase(scr_o, y_refs[g], dils[g], ATTN_OUT)
                _tokens_to_phase(scr_l, lt_refs[g], dils[g], ATTN_OUT)

    view = lambda d: pl.BlockSpec((tr // d, d * ATTN_OUT), lambda i: (i, 0))
    views = [view(d) for d in dils]
    f = lambda d: jax.ShapeDtypeStruct((s // d, d * ATTN_OUT), F32)
    fs = [f(d) for d in dils]
    scratch = pltpu.VMEM((ATTN_OUT // LANES, tr, LANES), F32)
    res = pl.pallas_call(
        body, out_shape=[jax.ShapeDtypeStruct((s, ATTN_OUT), MXU_DTYPE)] + fs + fs, grid=(s // tr,),
        in_specs=views + views, out_specs=[view(1)] + views + views, scratch_shapes=[scratch, scratch],
        compiler_params=_params(1), name="attn_combine",
    )(*o_list, *l_list)
    return res[0], list(res[1:1 + n_g]), list(res[1 + n_g:])


def _ret_consts(s):
    half = RET_QK // 2
    pos = jnp.arange(s, dtype=F32)
    inv_freq = ROPE_BASE ** (-jnp.arange(half, dtype=F32) / half)
    ang = pos[:, None] * inv_freq[None]
    log_g = jnp.log(1.0 - 2.0 ** (-5.0 - jnp.arange(RET_HEADS, dtype=F32)))
    n = jnp.arange(RET_CHUNK, dtype=F32)
    diff = n[:, None] - n[None, :]
    dmask = jnp.where(diff >= 0, jnp.exp(log_g[:, None, None] * jnp.maximum(diff, 0.0)), 0.0)
    qdec = jnp.exp(log_g[:, None] * (n + 1.0))
    kdec = jnp.exp(log_g[:, None] * (RET_CHUNK - 1.0 - n))
    cdec = jnp.exp(log_g * RET_CHUNK)
    wide = (RET_HEADS, RET_CHUNK, RET_QK)
    return dict(cos=jnp.cos(ang), sin=jnp.sin(ang), dmask=dmask,
                qdec=jnp.broadcast_to(qdec[:, :, None], wide), kdec=jnp.broadcast_to(kdec[:, :, None], wide),
                cdec=cdec)


def _rot(t, cs, sn):
    half = RET_QK // 2
    t1, t2 = t[:, :half], t[:, half:]
    return jnp.concatenate([t1 * cs - t2 * sn, t1 * sn + t2 * cs], axis=1)


def _rot_bwd(d, cs, sn):
    half = RET_QK // 2
    d1, d2 = d[:, :half], d[:, half:]
    return jnp.concatenate([d1 * cs + d2 * sn, d2 * cs - d1 * sn], axis=1)


def _ret_fwd(zq, zk, zv, zg, rc):
    s = zq.shape[0]
    nc = s // RET_CHUNK
    c = RET_CHUNK
    cps = 2 if nc % 2 == 0 else 1

    def body(cd_ref, q_ref, k_ref, v_ref, g_ref, cos_ref, sin_ref, dm_ref, qd_ref, kd_ref,
             yb_ref, o_ref, st_ref, state):
        n = pl.program_id(0)

        @pl.when(n == 0)
        def _():
            state[...] = jnp.zeros_like(state)

        for h in range(RET_HEADS):
            qs, vs = slice(h * RET_QK, (h + 1) * RET_QK), slice(h * RET_V, (h + 1) * RET_V)
            st = state[h]
            for t in range(cps):
                rs = slice(t * c, (t + 1) * c)
                cs, sn = cos_ref[rs, :], sin_ref[rs, :]
                qr = _rot(q_ref[rs, qs], cs, sn)
                kr = _rot(k_ref[rs, qs], cs, sn) * (RET_QK ** -0.5)
                v = v_ref[rs, vs]
                st_ref[h, t] = st.astype(st_ref.dtype)
                sc = _dot(qr, kr, NT) * dm_ref[h]
                o = _dot(sc, v, NN) + _dot(qr * qd_ref[h], st, NN)
                st = st * cd_ref[h] + _dot(kr * kd_ref[h], v, TN)
                o_ref[rs, vs] = o
                xh, _ = _norm_rows(o, GN_EPS)
                gv = g_ref[rs, vs]
                yb_ref[rs, vs] = (gv * _sigmoid(gv) * xh).astype(yb_ref.dtype)
            state[h] = st

    row = lambda w: pl.BlockSpec((cps * c, w), lambda n: (n, 0))
    const = lambda a: pl.BlockSpec(a.shape, lambda n: (0, 0, 0))
    return pl.pallas_call(
        body,
        out_shape=[jax.ShapeDtypeStruct((s, RET_HEADS * RET_V), MXU_DTYPE),
                   jax.ShapeDtypeStruct((s, RET_HEADS * RET_V), F32),
                   jax.ShapeDtypeStruct((RET_HEADS, nc, RET_QK, RET_V), MXU_DTYPE)],
        grid=(nc // cps,),
        in_specs=[pl.BlockSpec(memory_space=pltpu.SMEM), row(RET_HEADS * RET_QK), row(RET_HEADS * RET_QK),
                  row(RET_HEADS * RET_V), row(RET_HEADS * RET_V), row(RET_QK // 2), row(RET_QK // 2),
                  const(rc["dmask"]), const(rc["qdec"]), const(rc["kdec"])],
        out_specs=[row(RET_HEADS * RET_V), row(RET_HEADS * RET_V),
                   pl.BlockSpec((RET_HEADS, cps, RET_QK, RET_V), lambda n: (0, n, 0, 0))],
        scratch_shapes=[pltpu.VMEM((RET_HEADS, RET_QK, RET_V), F32)],
        compiler_params=_params(1), name="ret_fwd",
    )(rc["cdec"], zq, zk, zv, zg, rc["cos"], rc["sin"], rc["dmask"], rc["qdec"], rc["kdec"])


def _ret_bwd(zq, zk, zv, zg, o_ret, states, dyb, rc):
    s = zq.shape[0]
    nc = s // RET_CHUNK
    c = RET_CHUNK
    cps = 2 if nc % 2 == 0 else 1
    last = nc // cps - 1

    def body(cd_ref, q_ref, k_ref, v_ref, g_ref, o_ref, st_ref, dy_ref, cos_ref, sin_ref, dm_ref, qd_ref,
             kd_ref, dq_ref, dk_ref, dv_ref, dg_ref, dstate):
        n = pl.program_id(0)

        @pl.when(n == 0)
        def _():
            dstate[...] = jnp.zeros_like(dstate)

        for h in range(RET_HEADS):
            qs, vs = slice(h * RET_QK, (h + 1) * RET_QK), slice(h * RET_V, (h + 1) * RET_V)
            dm, qd, kd = dm_ref[h], qd_ref[h], kd_ref[h]
            ds_next = dstate[h]
            for t in reversed(range(cps)):
                rs = slice(t * c, (t + 1) * c)
                cs, sn = cos_ref[rs, :], sin_ref[rs, :]
                qr = _rot(q_ref[rs, qs], cs, sn)
                kr = _rot(k_ref[rs, qs], cs, sn) * (RET_QK ** -0.5)
                v = v_ref[rs, vs]
                st = st_ref[h, t]
                xh, rstd = _norm_rows(o_ref[rs, vs], GN_EPS)
                gv, dy = g_ref[rs, vs], dy_ref[rs, vs]
                sg = _sigmoid(gv)
                dg_ref[rs, vs] = (dy * xh * (sg * (1.0 + gv * (1.0 - sg)))).astype(dg_ref.dtype)
                do = _norm_rows_bwd(dy * (gv * sg), xh, rstd)
                sc = _dot(qr, kr, NT) * dm
                da = _dot(do, v, NT) * dm
                dv = _dot(sc, do, TN) + _dot(kr * kd, ds_next, NN)
                dqr = _dot(da, kr, NN) + _dot(do, st, NT) * qd
                dkr = _dot(da, qr, TN) + _dot(v, ds_next, NT) * kd
                ds_next = ds_next * cd_ref[h] + _dot(qr * qd, do, TN)
                dq_ref[rs, qs] = _rot_bwd(dqr, cs, sn).astype(dq_ref.dtype)
                dk_ref[rs, qs] = _rot_bwd(dkr * (RET_QK ** -0.5), cs, sn).astype(dk_ref.dtype)
                dv_ref[rs, vs] = dv.astype(dv_ref.dtype)
            dstate[h] = ds_next

    row = lambda w: pl.BlockSpec((cps * c, w), lambda n: (last - n, 0))
    const = lambda a: pl.BlockSpec(a.shape, lambda n: (0, 0, 0))
    qk_w, v_w = RET_HEADS * RET_QK, RET_HEADS * RET_V
    g_qk, g_v = jax.ShapeDtypeStruct((s, qk_w), MXU_DTYPE), jax.ShapeDtypeStruct((s, v_w), MXU_DTYPE)
    return pl.pallas_call(
        body,
        out_shape=[g_qk, g_qk, g_v, g_v],
        grid=(nc // cps,),
        in_specs=[pl.BlockSpec(memory_space=pltpu.SMEM), row(qk_w), row(qk_w), row(v_w), row(v_w), row(v_w),
                  pl.BlockSpec((RET_HEADS, cps, RET_QK, RET_V), lambda n: (0, last - n, 0, 0)), row(v_w),
                  row(RET_QK // 2), row(RET_QK // 2), const(rc["dmask"]), const(rc["qdec"]), const(rc["kdec"])],
        out_specs=[row(qk_w), row(qk_w), row(v_w), row(v_w)],
        scratch_shapes=[pltpu.VMEM((RET_HEADS, RET_QK, RET_V), F32)],
        compiler_params=_params(1), name="ret_bwd",
    )(rc["cdec"], zq, zk, zv, zg, o_ret, states, dyb, rc["cos"], rc["sin"], rc["dmask"], rc["qdec"],
      rc["kdec"])


ROW_TILE = 256


def _mix_fwd(x, ya_b, yb_b, ze, wap_t, wrp, wout, gam, bet):
    s = x.shape[0]
    tm = ROW_TILE

    def body(x_ref, ya_ref, yb_ref, ze_ref, wap_ref, wrp_ref, wout_ref, g_ref, b_ref,
             x1_ref, u1_ref, a1_ref, a2_ref, mg_ref, x1b_ref):
        a1 = _dot(ya_ref[...], wap_ref[...], NT)
        a2 = _dot(yb_ref[...], wrp_ref[...], NN)
        ze_ = ze_ref[...]
        merged = _sigmoid(ze_[:, :D_MODEL]) * a1 + _sigmoid(ze_[:, D_MODEL:]) * a2
        u = ALPHA * x_ref[...] + _dot(merged, wout_ref[...], NN)
        xh, _ = _norm_rows(u, LN_EPS)
        x1 = xh * g_ref[...] + b_ref[...]
        x1_ref[...] = x1
        x1b_ref[...] = x1.astype(x1b_ref.dtype)
        u1_ref[...] = u
        a1_ref[...] = a1.astype(a1_ref.dtype)
        a2_ref[...] = a2.astype(a2_ref.dtype)
        mg_ref[...] = merged.astype(mg_ref.dtype)

    row = lambda w: pl.BlockSpec((tm, w), lambda i: (i, 0))
    full = _resident
    f = jax.ShapeDtypeStruct((s, D_MODEL), F32)
    m = jax.ShapeDtypeStruct((s, D_MODEL), MXU_DTYPE)
    return pl.pallas_call(
        body, out_shape=[f, f, m, m, m, m], grid=(s // tm,),
        in_specs=[row(D_MODEL), row(ATTN_OUT), row(RET_HEADS * RET_V), row(2 * D_MODEL), full(wap_t), full(wrp),
                  full(wout), full(gam), full(bet)],
        out_specs=[row(D_MODEL)] * 6, compiler_params=_params(1), name="mix_fwd",
    )(x, ya_b, yb_b, ze, wap_t, wrp, wout, gam, bet)


def _mix_bwd(dx1, u1, a1, a2, ze, wap_t, wrp, wout, gam):
    s = dx1.shape[0]
    tm = ROW_TILE

    dils = [d for _, d in ATTN_GROUPS]

    def body(dx_ref, u_ref, a1_ref, a2_ref, ze_ref, wap_ref, wrp_ref, wout_ref, g_ref,
             dres_ref, du_ref, da1_ref, da2_ref, dze_ref, dyb_ref, dgam_ref, dbet_ref, *rest):
        dya_refs, dya_scr = rest[:len(dils)], rest[len(dils)]
        @pl.when(pl.program_id(0) == 0)
        def _():
            dgam_ref[...] = jnp.zeros_like(dgam_ref)
            dbet_ref[...] = jnp.zeros_like(dbet_ref)

        dx = dx_ref[...]
        xh, rstd = _norm_rows(u_ref[...], LN_EPS)
        dgam_ref[...] += jnp.sum(dx * xh, axis=0, keepdims=True)
        dbet_ref[...] += jnp.sum(dx, axis=0, keepdims=True)
        du = _norm_rows_bwd(dx * g_ref[...], xh, rstd)
        dres_ref[...] = ALPHA * du
        du_ref[...] = du.astype(du_ref.dtype)
        dm = _dot(du, wout_ref[...], NT)
        ze_ = ze_ref[...]
        sa, sb = _sigmoid(ze_[:, :D_MODEL]), _sigmoid(ze_[:, D_MODEL:])
        da1, da2 = dm * sa, dm * sb
        dze_ref[...] = jnp.concatenate([dm * a1_ref[...] * (sa * (1.0 - sa)),
                                        dm * a2_ref[...] * (sb * (1.0 - sb))], axis=1).astype(dze_ref.dtype)
        da1_ref[...] = da1.astype(da1_ref.dtype)
        da2_ref[...] = da2.astype(da2_ref.dtype)
        dyb_ref[...] = _dot(da2, wrp_ref[...], NT)
        dya = _dot(da1, wap_ref[...], NN)
        _cols_to_blocks(dya, dya_scr)
        for g, d in enumerate(dils):
            if d == 1:
                dya_refs[g][...] = dya
            else:
                _tokens_to_phase(dya_scr, dya_refs[g], d, ATTN_OUT)

    row = lambda w: pl.BlockSpec((tm, w), lambda i: (i, 0))
    full = _resident
    vec = pl.BlockSpec((1, D_MODEL), lambda i: (0, 0))
    f = lambda w: jax.ShapeDtypeStruct((s, w), F32)
    m = lambda w: jax.ShapeDtypeStruct((s, w), MXU_DTYPE)
    v = jax.ShapeDtypeStruct((1, D_MODEL), F32)
    res = pl.pallas_call(
        body,
        out_shape=[f(D_MODEL), m(D_MODEL), m(D_MODEL), m(D_MODEL), m(2 * D_MODEL), f(RET_HEADS * RET_V), v, v]
        + [jax.ShapeDtypeStruct((s // d, d * ATTN_OUT), F32) for d in dils],
        grid=(s // tm,),
        in_specs=[row(D_MODEL)] * 4 + [row(2 * D_MODEL), full(wap_t), full(wrp), full(wout), full(gam)],
        out_specs=[row(D_MODEL)] * 4 + [row(2 * D_MODEL), row(RET_HEADS * RET_V), vec, vec]
        + [pl.BlockSpec((tm // d, d * ATTN_OUT), lambda i: (i, 0)) for d in dils],
        scratch_shapes=[pltpu.VMEM((ATTN_OUT // LANES, tm, LANES), F32)],
        compiler_params=_params(1), name="mix_bwd",
    )(dx1, u1, a1, a2, ze, wap_t, wrp, wout, gam)
    return (*res[:8], list(res[8:]))


FF_CHUNK = 1408


def _ffn_fwd(x1, wg_t, wu_t, wd, gam, bet):
    s = x1.shape[0]
    tm, fc = ROW_TILE, FF_CHUNK

    def body(x_ref, wg_ref, wu_ref, wd_ref, g_ref, b_ref, x2_ref, u2_ref, a_ref, b_out_ref, h_ref, x2b_ref):
        xv = x_ref[...]
        u = ALPHA * xv
        for f0 in range(0, D_FF, fc):
            ch = slice(f0, f0 + fc)
            a = _dot(xv, wg_ref[ch, :], NT)
            b = _dot(xv, wu_ref[ch, :], NT)
            hid = a * _sigmoid(a) * b
            u = u + _dot(hid, wd_ref[ch, :], NN)
            a_ref[:, ch] = a.astype(a_ref.dtype)
            b_out_ref[:, ch] = b.astype(b_out_ref.dtype)
            h_ref[:, ch] = hid.astype(h_ref.dtype)
        xh, _ = _norm_rows(u, LN_EPS)
        u2_ref[...] = u
        x2 = xh * g_ref[...] + b_ref[...]
        x2_ref[...] = x2
        x2b_ref[...] = x2.astype(x2b_ref.dtype)

    row = lambda w: pl.BlockSpec((tm, w), lambda i: (i, 0))
    f = lambda w: jax.ShapeDtypeStruct((s, w), F32)
    m = lambda w: jax.ShapeDtypeStruct((s, w), MXU_DTYPE)
    return pl.pallas_call(
        body, out_shape=[f(D_MODEL), f(D_MODEL), m(D_FF), m(D_FF), m(D_FF), m(D_MODEL)],
        grid=(s // tm,),
        in_specs=[row(D_MODEL), _resident(wg_t), _resident(wu_t), _resident(wd), _resident(gam), _resident(bet)],
        out_specs=[row(D_MODEL), row(D_MODEL), row(D_FF), row(D_FF), row(D_FF), row(D_MODEL)],
        compiler_params=_params(1), name="ffn_fwd",
    )(x1, wg_t, wu_t, wd, gam, bet)


def _ffn_bwd(dx2, u2, fa, fb, wg_t, wu_t, wd, gam):
    s = dx2.shape[0]
    tm, fc = ROW_TILE, FF_CHUNK

    def body(dx_ref, u_ref, a_ref, b_ref, wg_ref, wu_ref, wd_ref, g_ref,
             dx1_ref, du_ref, da_ref, db_ref, dgam_ref, dbet_ref):
        @pl.when(pl.program_id(0) == 0)
        def _():
            dgam_ref[...] = jnp.zeros_like(dgam_ref)
            dbet_ref[...] = jnp.zeros_like(dbet_ref)

        dx = dx_ref[...]
        xh, rstd = _norm_rows(u_ref[...], LN_EPS)
        dgam_ref[...] += jnp.sum(dx * xh, axis=0, keepdims=True)
        dbet_ref[...] += jnp.sum(dx, axis=0, keepdims=True)
        du = _norm_rows_bwd(dx * g_ref[...], xh, rstd)
        du_ref[...] = du.astype(du_ref.dtype)
        acc = ALPHA * du
        for f0 in range(0, D_FF, fc):
            ch = slice(f0, f0 + fc)
            dh = _dot(du, wd_ref[ch, :], NT)
            a, b = a_ref[:, ch].astype(F32), b_ref[:, ch].astype(F32)
            sg = _sigmoid(a)
            da = dh * b * (sg * (1.0 + a * (1.0 - sg)))
            db = dh * (a * sg)
            acc = acc + _dot(da, wg_ref[ch, :], NN) + _dot(db, wu_ref[ch, :], NN)
            da_ref[:, ch] = da.astype(da_ref.dtype)
            db_ref[:, ch] = db.astype(db_ref.dtype)
        dx1_ref[...] = acc

    row = lambda w: pl.BlockSpec((tm, w), lambda i: (i, 0))
    vec = pl.BlockSpec((1, D_MODEL), lambda i: (0, 0))
    v = jax.ShapeDtypeStruct((1, D_MODEL), F32)
    return pl.pallas_call(
        body,
        out_shape=[jax.ShapeDtypeStruct((s, D_MODEL), F32), jax.ShapeDtypeStruct((s, D_MODEL), MXU_DTYPE),
                   jax.ShapeDtypeStruct((s, D_FF), MXU_DTYPE), jax.ShapeDtypeStruct((s, D_FF), MXU_DTYPE), v, v],
        grid=(s // tm,),
        in_specs=[row(D_MODEL), row(D_MODEL), row(D_FF), row(D_FF), _resident(wg_t), _resident(wu_t),
                  _resident(wd), _resident(gam)],
        out_specs=[row(D_MODEL), row(D_MODEL), row(D_FF), row(D_FF), vec, vec],
        compiler_params=_params(1), name="ffn_bwd",
    )(dx2, u2, fa, fb, wg_t, wu_t, wd, gam)


def _loss_head(y, target):
    s = y.shape[0]
    tr = _pick(s, (512, 256, 128))

    def body(y_ref, t_ref, dy_ref, l_ref):
        @pl.when(pl.program_id(0) == 0)
        def _():
            l_ref[...] = jnp.zeros_like(l_ref)

        e = y_ref[...] - t_ref[...]
        dy_ref[...] = e * (1.0 / D_MODEL)
        part = jnp.sum(jnp.sum(e * e, axis=1, keepdims=True), axis=0, keepdims=True)
        l_ref[...] += part * (0.5 / D_MODEL)

    spec = pl.BlockSpec((tr, D_MODEL), lambda i: (i, 0))
    dy, part = pl.pallas_call(
        body, out_shape=[jax.ShapeDtypeStruct((s, D_MODEL), F32), jax.ShapeDtypeStruct((8, BLK), F32)],
        grid=(s // tr,), in_specs=[spec, spec], out_specs=[spec, pl.BlockSpec((8, BLK), lambda i: (0, 0))],
        compiler_params=_params(1), name="loss_head",
    )(y, target)
    return dy, part[0, 0]


def _adamw(w, g, m, v, name):
    rows, cols = w.shape
    budget = 1 << 20
    cands = [t for t in range(8, rows + 1, 8) if rows % t == 0 and t * cols * 4 <= budget]
    tr = max(cands) if cands else rows

    def body(w_ref, g_ref, m_ref, v_ref, d_ref, nm_ref, nv_ref):
        gv = g_ref[...]
        mn = ADAM_B1 * m_ref[...] + (1.0 - ADAM_B1) * gv
        vn = ADAM_B2 * v_ref[...] + (1.0 - ADAM_B2) * (gv * gv)
        m_hat = mn / (1.0 - ADAM_B1 ** ADAM_STEP)
        v_hat = vn / (1.0 - ADAM_B2 ** ADAM_STEP)
        d_ref[...] = -ADAM_LR * (m_hat / (jnp.sqrt(v_hat) + ADAM_EPS) + ADAM_WD * w_ref[...])
        nm_ref[...] = mn
        nv_ref[...] = vn

    spec = pl.BlockSpec((tr, cols), lambda i: (i, 0))
    shape = jax.ShapeDtypeStruct((rows, cols), F32)
    return pl.pallas_call(
        body, out_shape=[shape, shape, shape], grid=(rows // tr,), in_specs=[spec] * 4, out_specs=[spec] * 3,
        compiler_params=_params(1), name=name,
    )(w, g, m, v)


MESH_ID = pl.DeviceIdType.MESH
ANY = pl.BlockSpec(memory_space=pl.ANY)


def _place():
    x, y, c = lax.axis_index("x"), lax.axis_index("y"), lax.axis_index("c")
    other_chips = [(1 - x, y), (x, 1 - y), (1 - x, 1 - y)]
    return x, y, c, other_chips


def _chip_no(chip):
    return 2 * chip[0] + chip[1]


def _gather_exchange(f_ref, g_ref, send_sems, recv_sems, hr):
    x, y, c, chips = _place()
    sibling = (x, y, 1 - c)

    def piece(chip, half):
        return g_ref.at[_chip_no(chip), pl.ds(half * hr, hr), :]

    def copy(k, src, dst, to):
        return pltpu.make_async_remote_copy(src_ref=src, dst_ref=dst, send_sem=send_sems.at[k],
                                            recv_sem=recv_sems.at[k], device_id=to, device_id_type=MESH_ID)

    own = copy(6, f_ref, g_ref.at[_chip_no((x, y))], sibling)
    own.start()
    my_half = f_ref.at[pl.ds(c * hr, hr), :]
    first = [copy(k, my_half, piece((x, y), c), (*chip, c)) for k, chip in enumerate(chips)]
    for cp in first:
        cp.start()
    passed = [copy(3 + k, piece(chip, c), piece(chip, c), sibling) for k, chip in enumerate(chips)]
    for k, chip in enumerate(chips):
        copy(k, my_half, piece(chip, c), (*chip, c)).wait_recv()
        passed[k].start()
    for k, chip in enumerate(chips):
        copy(3 + k, my_half, piece(chip, 1 - c), sibling).wait_recv()
    for cp in first + passed:
        cp.wait_send()
    own.wait()


GATHER_SEMS = (pltpu.SemaphoreType.DMA((7,)), pltpu.SemaphoreType.DMA((7,)))


HBM = pltpu.MemorySpace.HBM
ONE_SEM_PAIR = (pltpu.SemaphoreType.DMA, pltpu.SemaphoreType.DMA)


def _sibling_of_me():
    x, y, c, _ = _place()
    return [(x, y, 1 - c)]


def _same_core_of_other_chips():
    x, y, c, chips = _place()
    return [(*chip, c) for chip in chips]


def _on_sequencer(name, collective_id, sems, peers, exchange):
    @pl.kernel(mesh=plsc.ScalarSubcoreMesh(axis_name="sequencer", num_cores=1), name=name, scratch_types=sems,
               compiler_params=pltpu.CompilerParams(collective_id=collective_id))
    def launch(*sem_refs):
        barrier = pltpu.get_barrier_semaphore()
        devices = peers()
        for peer in devices:
            pl.semaphore_signal(barrier, inc=1, device_id=peer, device_id_type=MESH_ID)
        pl.semaphore_wait(barrier, len(devices))
        exchange(*sem_refs)

    launch()


def _all_gather_rows(flat, behind=None):
    r, cols = flat.shape
    out = jax.ShapeDtypeStruct((N_CHIPS, r, cols), flat.dtype)
    if behind is not None:
        f_ref, g_ref = jax.new_ref(flat, memory_space=HBM), jax.empty_ref(out, memory_space=HBM)
        _on_sequencer(f"all_gather_weights_behind_{behind[1]}", behind[0], GATHER_SEMS,
                      lambda: _sibling_of_me() + _same_core_of_other_chips(),
                      lambda s, r_: _gather_exchange(f_ref, g_ref, s, r_, r // 2))
        return g_ref[...]

    def body(f_ref, g_ref, send_sems, recv_sems):
        _gather_exchange(f_ref, g_ref, send_sems, recv_sems, r // 2)

    return pl.pallas_call(body, out_shape=out, in_specs=[ANY], out_specs=ANY, scratch_shapes=list(GATHER_SEMS),
                          name="all_gather_weights")(flat)


def _swap_halves(gall, behind=None):
    _, r, cols = gall.shape
    hr = r // 2
    out = jax.ShapeDtypeStruct((N_CHIPS, hr, cols), gall.dtype)

    def exchange(g_ref, a_ref, send_sem, recv_sem):
        x, y, c, _ = _place()
        cp = pltpu.make_async_remote_copy(src_ref=g_ref.at[:, pl.ds((1 - c) * hr, hr), :], dst_ref=a_ref,
                                          send_sem=send_sem, recv_sem=recv_sem, device_id=(x, y, 1 - c),
                                          device_id_type=MESH_ID)
        cp.start()
        cp.wait()

    if behind is not None:
        g_ref, a_ref = jax.new_ref(gall, memory_space=HBM), jax.empty_ref(out, memory_space=HBM)
        _on_sequencer(f"grad_swap_halves_behind_{behind[1]}", behind[0], ONE_SEM_PAIR, _sibling_of_me,
                      lambda s, r_: exchange(g_ref, a_ref, s, r_))
        return a_ref[...]

    def body(g_ref, a_ref, send_sem, recv_sem):
        exchange(g_ref, a_ref, send_sem, recv_sem)

    return pl.pallas_call(body, out_shape=out, in_specs=[ANY], out_specs=ANY, scratch_shapes=list(ONE_SEM_PAIR),
                          name="grad_swap_halves")(gall)


def _scatter_to_chips(p, behind=None):
    _, hr, cols = p.shape
    out = jax.ShapeDtypeStruct((3, hr, cols), p.dtype)
    sems = (pltpu.SemaphoreType.DMA((3,)), pltpu.SemaphoreType.DMA((3,)))

    def exchange(p_ref, b_ref, send_sems, recv_sems):
        x, y, c, chips = _place()
        cps = [pltpu.make_async_remote_copy(src_ref=p_ref.at[_chip_no(chip)], dst_ref=b_ref.at[k],
                                            send_sem=send_sems.at[k], recv_sem=recv_sems.at[k],
                                            device_id=(*chip, c), device_id_type=MESH_ID)
               for k, chip in enumerate(chips)]
        for cp in cps:
            cp.start()
        for cp in cps:
            cp.wait()

    if behind is not None:
        p_ref, b_ref = jax.new_ref(p, memory_space=HBM), jax.empty_ref(out, memory_space=HBM)
        _on_sequencer(f"grad_scatter_chips_behind_{behind[1]}", behind[0], sems, _same_core_of_other_chips,
                      lambda s, r_: exchange(p_ref, b_ref, s, r_))
        return b_ref[...]

    def body(p_ref, b_ref, send_sems, recv_sems):
        exchange(p_ref, b_ref, send_sems, recv_sems)

    return pl.pallas_call(body, out_shape=out, in_specs=[ANY], out_specs=ANY, scratch_shapes=list(sems),
                          name="grad_scatter_chips")(p)


def _share_with_sibling(full, behind=None):
    r, cols = full.shape
    hr = r // 2

    def exchange(in_ref, out_ref, send_sem, recv_sem):
        x, y, c, _ = _place()
        cp = pltpu.make_async_remote_copy(src_ref=in_ref.at[pl.ds(c * hr, hr), :],
                                          dst_ref=out_ref.at[pl.ds(c * hr, hr), :],
                                          send_sem=send_sem, recv_sem=recv_sem, device_id=(x, y, 1 - c),
                                          device_id_type=MESH_ID)
        cp.start()
        cp.wait()

    if behind is not None:
        full_ref = jax.new_ref(full, memory_space=HBM)
        _on_sequencer(f"grad_share_sibling_behind_{behind[1]}", behind[0], ONE_SEM_PAIR, _sibling_of_me,
                      lambda s, r_: exchange(full_ref, full_ref, s, r_))
        return full_ref[...]

    def body(in_ref, out_ref, send_sem, recv_sem):
        exchange(in_ref, out_ref, send_sem, recv_sem)

    return pl.pallas_call(
        body, out_shape=jax.ShapeDtypeStruct((r, cols), full.dtype), in_specs=[ANY], out_specs=ANY,
        input_output_aliases={0: 0}, scratch_shapes=list(ONE_SEM_PAIR), name="grad_share_sibling",
    )(full)


def _all_reduce_small(v):
    r, cols = v.shape
    n_dev = 8

    def body(x_ref, out_ref, gat_ref, send_sems, recv_sems, local_sem):
        x, y, c, chips = _place()
        me, sibling = (x, y, c), (x, y, 1 - c)

        def slot(px, py, pc):
            return gat_ref.at[4 * px + 2 * py + pc]

        def copy(k, block, to, src=None):
            return pltpu.make_async_remote_copy(src_ref=slot(*block) if src is None else src, dst_ref=slot(*block),
                                                send_sem=send_sems.at[k], recv_sem=recv_sems.at[k], device_id=to,
                                                device_id_type=MESH_ID)

        mine = pltpu.make_async_copy(x_ref, slot(*me), local_sem)
        mine.start()
        first = [copy(0, me, sibling, src=x_ref)]
        first += [copy(1 + k, me, (*chip, c), src=x_ref) for k, chip in enumerate(chips)]
        for cp in first:
            cp.start()
        passed = [copy(4 + k, (*chip, c), sibling) for k, chip in enumerate(chips)]
        for k, chip in enumerate(chips):
            copy(1 + k, (*chip, c), me).wait_recv()
            passed[k].start()
        copy(0, sibling, me).wait_recv()
        for k, chip in enumerate(chips):
            copy(4 + k, (*chip, 1 - c), me).wait_recv()
        for cp in first + passed:
            cp.wait_send()
        mine.wait()
        acc = gat_ref[0]
        for d in range(1, n_dev):
            acc = acc + gat_ref[d]
        out_ref[...] = acc

    vmem = pl.BlockSpec(memory_space=pltpu.VMEM)
    return pl.pallas_call(
        body, out_shape=jax.ShapeDtypeStruct((r, cols), v.dtype), in_specs=[vmem], out_specs=vmem,
        scratch_shapes=[pltpu.VMEM((n_dev, r, cols), v.dtype), pltpu.SemaphoreType.DMA((7,)),
                        pltpu.SemaphoreType.DMA((7,)), pltpu.SemaphoreType.DMA],
        name="all_reduce_small",
    )(v)


WIRE_DTYPE = jnp.bfloat16


def _add_own_half(gall, got, place):
    _, r, cols = gall.shape
    hr = r // 2
    tr = _row_tile(hr)
    g4 = gall.reshape(N_CHIPS, 2, hr, cols)

    def body(s_ref, g_ref, a_ref, o_ref):
        o_ref[...] = (g_ref[0] + a_ref[...]).astype(o_ref.dtype)

    return pl.pallas_call(
        body, out_shape=jax.ShapeDtypeStruct((N_CHIPS, hr, cols), WIRE_DTYPE),
        grid_spec=pltpu.PrefetchScalarGridSpec(
            num_scalar_prefetch=1, grid=(N_CHIPS, hr // tr),
            in_specs=[pl.BlockSpec((1, 1, tr, cols), lambda j, i, s: (j, s[1], i, 0)),
                      pl.BlockSpec((1, tr, cols), lambda j, i, s: (j, i, 0))],
            out_specs=pl.BlockSpec((1, tr, cols), lambda j, i, s: (j, i, 0))),
        compiler_params=_params(2), name="grad_add_halves",
    )(place, g4, got)


def _add_chip_parts(gall, got, parts, place):
    _, r, cols = gall.shape
    hr = r // 2
    tr = _row_tile(hr)
    g4 = gall.reshape(N_CHIPS, 2, hr, cols)
    nt = hr // tr

    def body(s_ref, g_ref, a_ref, b0_ref, b1_ref, b2_ref, o_ref):
        own = g_ref[0, 0] + a_ref[0]
        o_ref[...] = ((own + b0_ref[0].astype(F32)) + b1_ref[0].astype(F32)) + b2_ref[0].astype(F32)

    part = lambda k: pl.BlockSpec((1, tr, cols), lambda i, s: (k, i, 0))
    return pl.pallas_call(
        body, out_shape=jax.ShapeDtypeStruct((r, cols), F32),
        grid_spec=pltpu.PrefetchScalarGridSpec(
            num_scalar_prefetch=1, grid=(nt,),
            in_specs=[pl.BlockSpec((1, 1, tr, cols), lambda i, s: (s[0], s[1], i, 0)),
                      pl.BlockSpec((1, tr, cols), lambda i, s: (s[0], i, 0)), part(0), part(1), part(2)],
            out_specs=pl.BlockSpec((tr, cols), lambda i, s: (s[1] * nt + i, 0))),
        compiler_params=_params(1), name="grad_add_chips",
    )(place, g4, got, parts, parts, parts)


def _pack_shards(weights, l, group, dtype):
    parts = []
    for name, _ in group:
        w = weights[name][l]
        parts.append((w.T if name in COLUMN_SHARDED else w).reshape(-1, D_MODEL).astype(dtype))
    return parts[0] if len(parts) == 1 else jnp.concatenate(parts, axis=0)


def _unpack_gathered(gathered, group):
    w, off = {}, 0
    for name, rows in group:
        blk = gathered[:, off:off + rows]
        if name == "w_attn_proj":
            w[name] = blk.reshape(N_CHIPS * 256, ATTN_OUT)
        else:
            w[name] = blk.reshape(N_CHIPS * rows, D_MODEL)
        off += rows
    return w


def _pack_full_grads(g, group):
    parts = [g[name].reshape(N_CHIPS, rows, D_MODEL) for name, rows in group]
    return parts[0] if len(parts) == 1 else jnp.concatenate(parts, axis=1)


def _unpack_shard_grads(fulls, gi):
    out, off = {}, 0
    for name, rows in PACK_GROUPS[gi]:
        blks = [full[off:off + rows] for full in fulls]
        out[name] = jnp.stack([b.reshape(256, ATTN_OUT) if name == "w_attn_proj" else b for b in blks])
        off += rows
    return out


class _StaticWeights:
    def __init__(self, layers):
        self.layers = layers

    def group(self, l, gi):
        return self.layers[l]

    def layer(self, l):
        return self.layers[l]


class _WeightGathers:
    def __init__(self, weights):
        self.flats = {(l, gi): _pack_shards(weights, l, g, MXU_DTYPE)
                      for l in range(DEPTH) for gi, g in enumerate(PACK_GROUPS)}
        self.order = sorted(self.flats)
        self.landed, self.unpacked, self.issued = {}, {}, 0
        self._issue()

    def _issue(self, after=None):
        key = self.order[self.issued]
        flat = self.flats[key]
        if self.issued == 0:
            self.landed[key] = _all_gather_rows(flat)
        else:
            prev = self.order[self.issued - 1]
            ties = (self.landed[prev], flat) if after is None else (self.landed[prev], flat, after)
            tied = lax.optimization_barrier(ties)
            self.landed[prev], flat = tied[0], tied[1]
            after = tied[2] if after is not None else None
            self.landed[key] = _all_gather_rows(flat, behind=(1, f"l{key[0]}g{key[1]}"))
        self.issued += 1
        return after

    def prefetch(self, after):
        return self._issue(after) if self.issued < len(self.order) else after

    def group(self, l, gi):
        key = (l, gi)
        if key not in self.unpacked:
            self.unpacked[key] = _unpack_gathered(self.landed[key], PACK_GROUPS[gi])
        return self.unpacked[key]

    def layer(self, l):
        w = {}
        for gi in range(len(PACK_GROUPS)):
            w.update(self.group(l, gi))
        return w


class _GradReduction:
    LAST = 3

    def __init__(self, gw, group, place, behind):
        self.gall, self.place, self.behind, self.stage = _pack_full_grads(gw, group), place, behind, 0
        self.pending = None

    def _how(self, k):
        return None if self.behind is None else (self.behind[0][k], self.behind[1])

    def advance(self, value=None):
        if self.stage > self.LAST:
            return value
        if self.stage > 0 and value is not None:
            value, self.pending = lax.optimization_barrier((value, self.pending))
        if self.stage == 0:
            self.pending = _swap_halves(self.gall, self._how(0))
        elif self.stage == 1:
            self.got = self.pending
            self.pending = _scatter_to_chips(_add_own_half(self.gall, self.got, self.place), self._how(1))
        elif self.stage == 2:
            self.pending = _share_with_sibling(
                _add_chip_parts(self.gall, self.got, self.pending, self.place), self._how(2))
        self.stage += 1
        return value

    def result(self):
        while self.stage < self.LAST:
            self.advance()
        return self.pending


class _GradReductions:
    def __init__(self, place):
        self.place, self.all = place, {}

    def start(self, l, gi, gw):
        slot = len(self.all) % 2
        how = ((2 + 3 * slot, 3 + 3 * slot, 4 + 3 * slot), f"l{l}g{gi}")
        red = _GradReduction(gw, PACK_GROUPS[gi], self.place, how)
        red.advance()
        self.all[(l, gi)] = red

    def tick(self, value):
        for red in self.all.values():
            if red.behind is not None:
                value = red.advance(value)
        return value

    def results(self):
        return {key: red.result() for key, red in self.all.items()}


def _layer_fwd(x, x_b, weights, l, lnp, bias, rc, prefetch):
    tag = f"l{l}"
    w_in = weights.group(l, IN_GROUP)["w_in"]
    z = {}
    for part, secs in enumerate(DX_PARTS):
        runs = tuple(run for _, sec_runs, _, _ in secs for run in sec_runs)
        outs = _in_proj(x_b, _section_rows(w_in, runs), _section_rows(lnp["b_in"], runs), secs,
                        name=f"in_proj_{part}_{tag}")
        z.update({name: o for (name, _, _, _), o in zip(secs, outs)})
        if part == 0:
            z["a0"] = prefetch(z["a0"])
    w = weights.group(l, REST_GROUP)
    o_list, l_list = [], []
    for g, (_, dil) in enumerate(ATTN_GROUPS):
        o, lse = _attn_fwd(z[f"a{g}"], bias, g, dil)
        o_list.append(o)
        l_list.append(lse)
    ya_b, ya_views, lt_views = _attn_combine(o_list, l_list)
    yb_b, o_ret, states = _ret_fwd(z["bq"], z["bk"], z["c"], z["d"], rc)
    x1, u1, a1, a2, mg_b, x1_b = _mix_fwd(x, ya_b, yb_b, z["e"], w["w_attn_proj"], w["w_ret_proj"], w["w_out"],
                                          lnp["ln1_g"], lnp["ln1_b"])
    x1 = prefetch(x1)
    x2, u2, fa, fb, h_b, x2_b = _ffn_fwd(x1, w["w_ffn_gate"], w["w_ffn_up"], w["w_ffn_down"], lnp["ln2_g"],
                                         lnp["ln2_b"])
    saved = dict(x_b=x_b, z=z, ya_views=ya_views, ya_b=ya_b, lt_views=lt_views, yb_b=yb_b, o_ret=o_ret, states=states,
                 x1_b=x1_b, u1=u1, a1=a1, a2=a2, mg_b=mg_b, u2=u2, fa=fa, fb=fb, h_b=h_b)
    return x2, x2_b, saved


def _layer_bwd(dx2, w, l, lnp, sv, bias, rc, reductions=None):
    s = dx2.shape[0]
    tag = f"l{l}"
    z = sv["z"]
    step = reductions.tick if reductions is not None else (lambda v: v)
    dx1, du2_b, da_b, db_b, dg2, dbt2 = _ffn_bwd(dx2, sv["u2"], sv["fa"], sv["fb"], w["w_ffn_gate"], w["w_ffn_up"],
                                                  w["w_ffn_down"], lnp["ln2_g"])
    dx1 = step(dx1)
    gw = {}
    gw["w_ffn_down"] = _matmul(sv["h_b"], du2_b, "tn", name=f"dw_ffn_down_{tag}")
    gw["w_ffn_gate"] = _matmul(da_b, sv["x1_b"], "tn", name=f"dw_ffn_gate_{tag}")
    gw["w_ffn_up"] = _matmul(db_b, sv["x1_b"], "tn", name=f"dw_ffn_up_{tag}")
    dres, du1_b, da1_b, da2_b, dze, dyb, dg1, dbt1, dya_views = _mix_bwd(
        dx1, sv["u1"], sv["a1"], sv["a2"], z["e"], w["w_attn_proj"], w["w_ret_proj"], w["w_out"], lnp["ln1_g"])
    gw["w_out"] = _matmul(sv["mg_b"], du1_b, "tn", name=f"dw_out_{tag}")
    gw["w_attn_proj"] = _matmul(da1_b, sv["ya_b"], "tn", name=f"dw_attn_proj_{tag}")
    gw["w_ret_proj"] = _matmul(sv["yb_b"], da2_b, "tn", name=f"dw_ret_proj_{tag}")
    dyb = step(dyb)
    if reductions is not None:
        reductions.start(l, REST_GROUP, gw)
    dzq, dzk, dzv, dzg = _ret_bwd(z["bq"], z["bk"], z["c"], z["d"], sv["o_ret"], sv["states"], dyb, rc)
    dzq = step(dzq)
    dz = {"bq": dzq, "bk": dzk, "c": dzv, "d": dzg, "e": dze}
    dbias_l = []
    for g, (_, dil) in enumerate(ATTN_GROUPS):
        dz[f"a{g}"], dbg = _attn_bwd(z[f"a{g}"], bias, dya_views[g], sv["ya_views"][g], sv["lt_views"][g], g, dil)
        dbias_l.append(dbg)
    last = f"a{N_GROUPS - 1}"
    dz[last] = step(dz[last])
    dx = dres
    for part, secs in enumerate(DX_PARTS):
        runs = tuple(run for _, sec_runs, _, _ in secs for run in sec_runs)
        dx = _dx_in_proj(dx, [dz[name] for name, _, _, _ in secs], _section_rows(w["w_in"], runs), secs,
                         name=f"dx_in_proj_{part}_{tag}")
    dx = step(dx)
    dw, db = {}, {}
    for name, _, _, dil in SECTIONS:
        dw[name], db[name] = _matmul(dz[name], sv["x_b"], "tn", colsum=True, a_phase=dil,
                                     name=f"dw_in_proj_{name}_{tag}")
    gw["w_in"] = _unsection(dw, axis=0)
    if reductions is not None:
        reductions.start(l, IN_GROUP, gw)
    small = dict(b_in=_unsection(db, axis=1), ln1_g=dg1, ln1_b=dbt1, ln2_g=dg2, ln2_b=dbt2)
    return dx, gw, small, jnp.concatenate(dbias_l, axis=0)


def _forward_backward(x, target, rel_bias, weights, lnps, reductions=None):
    s = x.shape[0]
    bmaps = _bucket_maps()
    bias = _bias_tiles(rel_bias, bmaps)
    rc = _ret_consts(s)
    prefetch = getattr(weights, "prefetch", lambda v: v)
    saved = []
    h, h_b = x, x.astype(MXU_DTYPE)
    for l in range(DEPTH):
        h, h_b, sv = _layer_fwd(h, h_b, weights, l, lnps[l], bias, rc, prefetch)
        saved.append(sv)
    dh, loss_part = _loss_head(h, target)
    gws, smalls, dbiases = [None] * DEPTH, [None] * DEPTH, [None] * DEPTH
    for l in reversed(range(DEPTH)):
        dh, gws[l], smalls[l], dbiases[l] = _layer_bwd(dh, weights.layer(l), l, lnps[l], saved[l], bias, rc,
                                                       reductions)
    d_rel_bias = _bias_tiles_bwd(dbiases, bmaps)
    return loss_part, dh, gws, smalls, d_rel_bias


SMALL_NAMES = ("rel_bias", "b_in", "ln1_g", "ln1_b", "ln2_g", "ln2_b")
SMALL_ROWS = 32


def _pack_small(vals):
    flat = jnp.concatenate([vals[n].reshape(-1) for n in SMALL_NAMES])
    return jnp.pad(flat, (0, SMALL_ROWS * D_MODEL - flat.shape[0])).reshape(SMALL_ROWS, D_MODEL)


def _unpack_small(packed, like):
    flat = packed.reshape(-1)
    out, off = {}, 0
    for n in SMALL_NAMES:
        size = like[n].size
        out[n] = flat[off:off + size].reshape(like[n].shape)
        off += size
    return out


def kernel(x, rel_bias, w_in, b_in, w_attn_proj, w_ret_proj, w_out, ln1_g, ln1_b, w_ffn_gate, w_ffn_up, w_ffn_down, ln2_g, ln2_b, loss_target, m_rel_bias, m_w_in, m_b_in, m_w_attn_proj, m_w_ret_proj, m_w_out, m_ln1_g, m_ln1_b, m_w_ffn_gate, m_w_ffn_up, m_w_ffn_down, m_ln2_g, m_ln2_b, v_rel_bias, v_w_in, v_b_in, v_w_attn_proj, v_w_ret_proj, v_w_out, v_ln1_g, v_ln1_b, v_w_ffn_gate, v_w_ffn_up, v_w_ffn_down, v_ln2_g, v_ln2_b):
    big = dict(w_in=(w_in, m_w_in, v_w_in), w_attn_proj=(w_attn_proj, m_w_attn_proj, v_w_attn_proj),
               w_ret_proj=(w_ret_proj, m_w_ret_proj, v_w_ret_proj), w_out=(w_out, m_w_out, v_w_out),
               w_ffn_gate=(w_ffn_gate, m_w_ffn_gate, v_w_ffn_gate), w_ffn_up=(w_ffn_up, m_w_ffn_up, v_w_ffn_up),
               w_ffn_down=(w_ffn_down, m_w_ffn_down, v_w_ffn_down))
    small_w = dict(rel_bias=rel_bias, b_in=b_in, ln1_g=ln1_g, ln1_b=ln1_b, ln2_g=ln2_g, ln2_b=ln2_b)
    small_m = dict(rel_bias=m_rel_bias, b_in=m_b_in, ln1_g=m_ln1_g, ln1_b=m_ln1_b, ln2_g=m_ln2_g, ln2_b=m_ln2_b)
    small_v = dict(rel_bias=v_rel_bias, b_in=v_b_in, ln1_g=v_ln1_g, ln1_b=v_ln1_b, ln2_g=v_ln2_g, ln2_b=v_ln2_b)

    place = jnp.stack([2 * lax.axis_index("x") + lax.axis_index("y"), lax.axis_index("c")]).astype(jnp.int32)

    weights = _WeightGathers({n: w for n, (w, _, _) in big.items()})
    reductions = _GradReductions(place)
    lnps = [dict(b_in=b_in[l][None], ln1_g=ln1_g[l][None], ln1_b=ln1_b[l][None], ln2_g=ln2_g[l][None],
                 ln2_b=ln2_b[l][None]) for l in range(DEPTH)]
    loss_part, dx, _, smalls, d_rel_bias = _forward_backward(x[0], loss_target[0], rel_bias, weights, lnps,
                                                             reductions)
    loss = lax.psum(loss_part, ("x", "y", "c"))

    small_g = dict(rel_bias=d_rel_bias)
    for n in SMALL_NAMES[1:]:
        small_g[n] = jnp.concatenate([smalls[l][n] for l in range(DEPTH)], axis=0)
    small_g = _unpack_small(_all_reduce_small(_pack_small(small_g)), small_w)
    grads, delta, new_m, new_v = dict(small_g), {}, {}, {}
    d_, m_, v_ = _adamw(_pack_small(small_w), _pack_small(small_g), _pack_small(small_m), _pack_small(small_v),
                        name="adamw_small")
    d_ = reductions.tick(d_)
    delta.update(_unpack_small(d_, small_w))
    new_m.update(_unpack_small(m_, small_w))
    new_v.update(_unpack_small(v_, small_w))

    def update(n, g):
        w, m, v = big[n]
        turn = (lambda t: jnp.swapaxes(t, 1, 2)) if n in COLUMN_SHARDED else (lambda t: t)
        two_d = lambda t: t.reshape(-1, t.shape[-1])
        d_, m_, v_ = _adamw(two_d(turn(w)), two_d(g), two_d(turn(m)), two_d(turn(v)), name=f"adamw_{n}")
        grads[n], delta[n], new_m[n], new_v[n] = (turn(t.reshape(g.shape)) for t in (g, d_, m_, v_))

    rest = _unpack_shard_grads([reductions.all[(l, REST_GROUP)].result() for l in range(DEPTH)], REST_GROUP)
    for i, (n, g) in enumerate(rest.items()):
        update(n, reductions.tick(g) if i == len(rest) // 2 else g)
    first = _unpack_shard_grads([reductions.all[(l, IN_GROUP)].result() for l in range(DEPTH)], IN_GROUP)
    for n, g in first.items():
        update(n, g)

    order = ("rel_bias", "w_in", "b_in", "w_attn_proj", "w_ret_proj", "w_out", "ln1_g", "ln1_b", "w_ffn_gate",
             "w_ffn_up", "w_ffn_down", "ln2_g", "ln2_b")
    return (loss, dx[None], *[grads[n] for n in order], *[delta[n] for n in order], *[new_m[n] for n in order],
            *[new_v[n] for n in order])
```

```python
import functools

import numpy as np
import jax
import jax.numpy as jnp
from jax import lax
from jax.experimental import pallas as pl
from jax.experimental.pallas import tpu as pltpu
from jax.experimental.pallas import tpu_sc as plsc

F32 = jnp.float32
MXU_DTYPE = jnp.bfloat16

DEPTH = 2
D_MODEL = 1024
HEAD_DIM = 64
ATTN_GROUPS = ((128, 1), (512, 4), (2048, 16))
N_GROUPS = len(ATTN_GROUPS)
HEADS_PER_GROUP = 6
N_ATTN_HEADS = 18
ATTN_WIDTH = 1152
ATTN_OUT = 384
NUM_BUCKETS = 32
MAX_DISTANCE = 2048
RET_HEADS = 4
RET_QK = 256
RET_V = 512
RET_CHUNK = 128
ROPE_BASE = 10000.0
D_FF = 2816
IN_COLS = 11648
ALPHA = (2 * DEPTH) ** 0.25
LN_EPS = 1e-5
GN_EPS = 1e-5
ADAM_LR, ADAM_B1, ADAM_B2, ADAM_EPS, ADAM_WD, ADAM_STEP = 0.001, 0.9, 0.999, 1e-08, 0.01, 10

BLK = 128
NEG = -1e30
N_CHIPS = 4
VMEM_LIMIT = 48 * 1024 * 1024

SECTIONS = tuple(
    (f"a{g}", tuple((k * ATTN_WIDTH + g * ATTN_OUT, ATTN_OUT) for k in range(3)), True, dil)
    for g, (_, dil) in enumerate(ATTN_GROUPS)
) + (
    ("bq", ((3456, 1024),), False, 1),
    ("bk", ((4480, 1024),), False, 1),
    ("c", ((5504, 2048),), True, 1),
    ("d", ((7552, 2048),), False, 1),
    ("e", ((9600, 2048),), False, 1),
)
DX_PARTS = (SECTIONS[:5], SECTIONS[5:])
GROUP_WIDTH = 3 * ATTN_OUT
LANES = 128


def _section_rows(w, runs):
    axis = 1 if w.shape[0] == 1 else 0
    parts = [lax.slice_in_dim(w, first, first + width, axis=axis) for first, width in runs]
    return parts[0] if len(parts) == 1 else jnp.concatenate(parts, axis=axis)


def _unsection(pieces, axis):
    runs = []
    for name, sec_runs, _, _ in SECTIONS:
        off = 0
        for first, width in sec_runs:
            runs.append((first, lax.slice_in_dim(pieces[name], off, off + width, axis=axis)))
            off += width
    return jnp.concatenate([p for _, p in sorted(runs, key=lambda t: t[0])], axis=axis)
PACK_GROUPS = ((("w_in", 2912),),
               (("w_attn_proj", 96), ("w_ret_proj", 512), ("w_out", 256), ("w_ffn_gate", 704), ("w_ffn_up", 704),
                ("w_ffn_down", 704)))
IN_GROUP, REST_GROUP = 0, 1
COLUMN_SHARDED = ("w_in", "w_attn_proj", "w_ffn_gate", "w_ffn_up")

NN = ((1,), (0,))
NT = ((1,), (1,))
TN = ((0,), (0,))


def _dot(a, b, dims):
    return lax.dot_general(a.astype(MXU_DTYPE), b.astype(MXU_DTYPE), (dims, ((), ())),
                           preferred_element_type=F32)


def _pick(n, prefs):
    for p in prefs:
        if n % p == 0:
            return p
    raise ValueError(f"no tile for {n} among {prefs}")


def _row_tile(n, most=512, unit=16):
    return max(t for t in range(unit, most + 1, unit) if n % t == 0)


TOKEN_TILES = (1024, 512, 256, 128)
FEATURE_TILES = (1152, 1024, 1408, 384, 256, 128)


def _params(n_axes, limit=VMEM_LIMIT):
    return pltpu.CompilerParams(dimension_semantics=("arbitrary",) * n_axes, vmem_limit_bytes=limit)


def _resident(a):
    return pl.BlockSpec(a.shape, lambda i: (0,) * a.ndim, pipeline_mode=pl.Buffered(1))


def _sigmoid(x):
    return 1.0 / (1.0 + jnp.exp(-x))


def _norm_rows(u, eps):
    mu = jnp.mean(u, axis=-1, keepdims=True)
    xc = u - mu
    var = jnp.mean(xc * xc, axis=-1, keepdims=True)
    rstd = lax.rsqrt(var + eps)
    return xc * rstd, rstd


def _norm_rows_bwd(dxh, xh, rstd):
    c1 = jnp.mean(dxh, axis=-1, keepdims=True)
    c2 = jnp.mean(dxh * xh, axis=-1, keepdims=True)
    return rstd * (dxh - c1 - xh * c2)


def _phase_to_tokens(src_ref, scr, dil, width):
    n = scr.shape[1] // dil
    for r in range(dil):
        for cb in range(width // LANES):
            col = r * width + cb * LANES
            scr.at[cb][pl.ds(r, n, stride=dil), :] = src_ref[:, col:col + LANES].astype(F32)


def _tokens_to_phase(scr, dst_ref, dil, width):
    n = scr.shape[1] // dil
    for r in range(dil):
        for cb in range(width // LANES):
            col = r * width + cb * LANES
            dst_ref[:, col:col + LANES] = scr.at[cb][pl.ds(r, n, stride=dil), :].astype(dst_ref.dtype)


def _blocks_to_cols(scr):
    return jnp.concatenate([scr[cb] for cb in range(scr.shape[0])], axis=1)


def _cols_to_blocks(val, scr):
    for cb in range(scr.shape[0]):
        scr[cb] = val[:, cb * LANES:(cb + 1) * LANES]


def _matmul(a, b, mode, *, name, out_dtype=F32, bias=None, addend=None, colsum=False, a_phase=1, out_phase=1):
    d_a, d_o = a_phase, out_phase
    if mode == "tn":
        kd, m = a.shape[0] * d_a, a.shape[1] // d_a
        n = b.shape[1]
        tm, tn_ = (m if d_a > 1 else _pick(m, FEATURE_TILES)), _pick(n, FEATURE_TILES)
        tk = _pick(kd, TOKEN_TILES if d_a > 1 else (2048,) + TOKEN_TILES)
        a_spec = (pl.BlockSpec((tk // d_a, d_a * m), lambda i, j, k: (k, 0)) if d_a > 1 else
                  pl.BlockSpec((tk, tm), lambda i, j, k: (k, i)))
        b_spec = pl.BlockSpec((tk, tn_), lambda i, j, k: (k, j))
        feat, tok = m, tk
    else:
        m, kd = a.shape[0] * d_a, a.shape[1] // d_a
        n = b.shape[0] if mode == "nt" else b.shape[1]
        tm, tn_ = _pick(m, TOKEN_TILES), (n if d_o > 1 else _pick(n, FEATURE_TILES))
        tk = kd if d_a > 1 else _pick(kd, FEATURE_TILES)
        a_spec = (pl.BlockSpec((tm // d_a, d_a * kd), lambda i, j, k: (i, 0)) if d_a > 1 else
                  pl.BlockSpec((tm, tk), lambda i, j, k: (i, k)))
        if mode == "nt":
            b_spec = pl.BlockSpec((tn_, tk), lambda i, j, k: (j, k))
        else:
            b_spec = pl.BlockSpec((tk, tn_), lambda i, j, k: (k, j))
        feat, tok = kd, tm
    dims = {"nn": NN, "nt": NT, "tn": TN}[mode]
    nk = kd // tk
    has_bias, has_add = bias is not None, addend is not None
    assert not colsum or mode == "tn"
    assert d_o == 1 or (mode == "nt" and not has_add)

    def body(*refs):
        it = iter(refs)
        a_ref, b_ref = next(it), next(it)
        bias_ref = next(it) if has_bias else None
        add_ref = next(it) if has_add else None
        o_ref = next(it)
        cs_ref = next(it) if colsum else None
        acc_ref = next(it)
        a_scr = next(it) if d_a > 1 else None
        o_scr = next(it) if d_o > 1 else None
        j, k = pl.program_id(1), pl.program_id(2)

        @pl.when(k == 0)
        def _():
            if has_add:
                acc_ref[...] = add_ref[...].astype(F32)
            else:
                acc_ref[...] = jnp.zeros_like(acc_ref)

        if d_a > 1:
            _phase_to_tokens(a_ref, a_scr, d_a, feat)
            av = _blocks_to_cols(a_scr)
        else:
            av = a_ref[...]
        acc_ref[...] += _dot(av, b_ref[...], dims)
        if colsum:
            @pl.when(jnp.logical_and(j == 0, k == 0))
            def _():
                cs_ref[...] = jnp.zeros_like(cs_ref)

            @pl.when(j == 0)
            def _():
                cs_ref[...] += jnp.sum(av.astype(F32), axis=0, keepdims=True)

        @pl.when(k == nk - 1)
        def _():
            r = acc_ref[...]
            if has_bias:
                r = r + bias_ref[...]
            if d_o > 1:
                _cols_to_blocks(r, o_scr)
                _tokens_to_phase(o_scr, o_ref, d_o, n)
            else:
                o_ref[...] = r.astype(out_dtype)

    in_specs, args = [a_spec, b_spec], [a, b]
    if has_bias:
        in_specs.append(pl.BlockSpec((1, tn_), lambda i, j, k: (0, j)))
        args.append(bias)
    if has_add:
        in_specs.append(pl.BlockSpec((tm, tn_), lambda i, j, k: (i, j)))
        args.append(addend)
    if d_o > 1:
        out_shape = [jax.ShapeDtypeStruct((m // d_o, d_o * n), out_dtype)]
        out_specs = [pl.BlockSpec((tm // d_o, d_o * n), lambda i, j, k: (i, 0))]
    else:
        out_shape = [jax.ShapeDtypeStruct((m, n), out_dtype)]
        out_specs = [pl.BlockSpec((tm, tn_), lambda i, j, k: (i, j))]
    if colsum:
        out_shape.append(jax.ShapeDtypeStruct((1, m), F32))
        out_specs.append(pl.BlockSpec((1, tm), lambda i, j, k: (0, i)))
    scratch = [pltpu.VMEM((tm, tn_), F32)]
    if d_a > 1:
        scratch.append(pltpu.VMEM((feat // LANES, tok, LANES), F32))
    if d_o > 1:
        scratch.append(pltpu.VMEM((n // LANES, tm, LANES), F32))
    res = pl.pallas_call(
        body, out_shape=out_shape, grid=(m // tm, n // tn_, nk), in_specs=in_specs, out_specs=out_specs,
        scratch_shapes=scratch, compiler_params=_params(3), name=name,
    )(*args)
    return res if colsum else res[0]


def _in_proj(x_b, w_rows, bias_cols, sections, name):
    s = x_b.shape[0]
    tm = 512
    widths = [sum(w for _, w in runs) for _, runs, _, _ in sections]
    dils = [dil for _, _, _, dil in sections]
    offs = [sum(widths[:i]) for i in range(len(widths))]
    widest_phased = max([w for w, d in zip(widths, dils) if d > 1], default=0)

    def body(x_ref, w_ref, b_ref, *rest):
        o_refs = rest[:len(sections)]
        scr = rest[len(sections)] if widest_phased else None
        xv = x_ref[...]
        for o_ref, width, dil, off in zip(o_refs, widths, dils, offs):
            r = _dot(xv, w_ref[off:off + width, :], NT) + b_ref[:, off:off + width]
            if dil > 1:
                _cols_to_blocks(r, scr)
                _tokens_to_phase(scr, o_ref, dil, width)
            else:
                o_ref[...] = r.astype(o_ref.dtype)

    out_shape = [jax.ShapeDtypeStruct((s // d, d * w), MXU_DTYPE if narrow else F32)
                 for w, d, (_, _, narrow, _) in zip(widths, dils, sections)]
    out_specs = [pl.BlockSpec((tm // d, d * w), lambda i: (i, 0)) for w, d in zip(widths, dils)]
    scratch = [pltpu.VMEM((widest_phased // LANES, tm, LANES), F32)] if widest_phased else []
    return pl.pallas_call(
        body, out_shape=out_shape, grid=(s // tm,),
        in_specs=[pl.BlockSpec((tm, D_MODEL), lambda i: (i, 0)), _resident(w_rows), _resident(bias_cols)],
        out_specs=out_specs, scratch_shapes=scratch, compiler_params=_params(1), name=name,
    )(x_b, w_rows, bias_cols)


def _dx_in_proj(addend, dzs, w_rows, sections, name):
    s = addend.shape[0]
    tm = 512
    widths = [sum(w for _, w in runs) for _, runs, _, _ in sections]
    dils = [dil for _, _, _, dil in sections]
    offs = [sum(widths[:i]) for i in range(len(widths))]
    widest_phased = max([w for w, d in zip(widths, dils) if d > 1], default=0)

    def body(*refs):
        add_ref, w_ref, o_ref = refs[0], refs[1], refs[2 + len(dzs)]
        dz_refs = refs[2:2 + len(dzs)]
        scr = refs[3 + len(dzs)] if widest_phased else None
        acc = add_ref[...]
        for dz_ref, width, dil, off in zip(dz_refs, widths, dils, offs):
            if dil > 1:
                _phase_to_tokens(dz_ref, scr, dil, width)
                av = _blocks_to_cols(scr)
            else:
                av = dz_ref[...]
            acc = acc + _dot(av, w_ref[off:off + width, :], NN)
        o_ref[...] = acc

    row = pl.BlockSpec((tm, D_MODEL), lambda i: (i, 0))
    dz_specs = [pl.BlockSpec((tm // d, d * w), lambda i: (i, 0)) for w, d in zip(widths, dils)]
    scratch = [pltpu.VMEM((widest_phased // LANES, tm, LANES), F32)] if widest_phased else []
    return pl.pallas_call(
        body, out_shape=jax.ShapeDtypeStruct((s, D_MODEL), F32), grid=(s // tm,),
        in_specs=[row, _resident(w_rows)] + dz_specs, out_specs=row, scratch_shapes=scratch,
        compiler_params=_params(1), name=name,
    )(addend, w_rows, *dzs)


def _t5_bucket(dist):
    max_exact = NUM_BUCKETS // 2
    large = max_exact + (np.log(np.maximum(dist, max_exact) / max_exact)
                         / np.log(MAX_DISTANCE / max_exact) * (NUM_BUCKETS - max_exact)).astype(np.int32)
    large = np.minimum(large, NUM_BUCKETS - 1)
    return np.where(dist < max_exact, dist, large).astype(np.int32)


def _bucket_maps():
    qi = np.arange(BLK)[:, None]
    kj = np.arange(2 * BLK)[None, :]
    rel = np.clip(qi + BLK - kj, 0, BLK)
    return jnp.asarray(np.stack([_t5_bucket(rel * d) for _, d in ATTN_GROUPS]))


def _bias_tiles(rel_bias, bmaps):
    def body(tab_ref, bm_ref, o_ref):
        h = pl.program_id(0)
        bm = bm_ref[0]
        acc = jnp.zeros((BLK, 2 * BLK), F32)
        for b in range(NUM_BUCKETS):
            acc = jnp.where(bm == b, tab_ref[b, h], acc)
        o_ref[0] = acc

    return pl.pallas_call(
        body, out_shape=jax.ShapeDtypeStruct((N_ATTN_HEADS, BLK, 2 * BLK), F32), grid=(N_ATTN_HEADS,),
        in_specs=[pl.BlockSpec(memory_space=pltpu.SMEM),
                  pl.BlockSpec((1, BLK, 2 * BLK), lambda h: (h // HEADS_PER_GROUP, 0, 0))],
        out_specs=pl.BlockSpec((1, BLK, 2 * BLK), lambda h: (h, 0, 0)),
        compiler_params=_params(1), name="bias_tiles",
    )(rel_bias, bmaps)


def _bias_tiles_bwd(dbias_layers, bmaps):
    nl = len(dbias_layers)

    def body(*refs):
        bm = refs[nl][0]
        o_ref = refs[nl + 1]
        x = refs[0][0]
        for r in refs[1:nl]:
            x = x + r[0]
        lane = lax.broadcasted_iota(jnp.int32, (1, BLK), 1)
        row = jnp.zeros((1, BLK), F32)
        for b in range(NUM_BUCKETS):
            s = jnp.sum(jnp.where(bm == b, x, 0.0), axis=1, keepdims=True)
            s = jnp.sum(s, axis=0, keepdims=True)
            row = jnp.where(lane == b, s, row)
        o_ref[0] = row

    tile = pl.BlockSpec((1, BLK, 2 * BLK), lambda h: (h, 0, 0))
    out = pl.pallas_call(
        body, out_shape=jax.ShapeDtypeStruct((N_ATTN_HEADS, 1, BLK), F32), grid=(N_ATTN_HEADS,),
        in_specs=[tile] * nl + [pl.BlockSpec((1, BLK, 2 * BLK), lambda h: (h // HEADS_PER_GROUP, 0, 0))],
        out_specs=pl.BlockSpec((1, 1, BLK), lambda h: (h, 0, 0)),
        compiler_params=_params(1), name="bias_tiles_bwd",
    )(*dbias_layers, bmaps)
    return out[:, 0, :NUM_BUCKETS].T


def _pair_masks():
    lane = lax.broadcasted_iota(jnp.int32, (BLK, BLK), 1)
    row2 = lax.broadcasted_iota(jnp.int32, (2 * BLK, BLK), 0)
    lane2 = lax.broadcasted_iota(jnp.int32, (2 * BLK, BLK), 1)
    own = (lane2 // HEAD_DIM) == (row2 // BLK)
    qi = lax.broadcasted_iota(jnp.int32, (2 * BLK, 2 * BLK), 0) & (BLK - 1)
    kj = lax.broadcasted_iota(jnp.int32, (2 * BLK, 2 * BLK), 1)
    band = jnp.logical_and(kj >= qi, kj <= qi + BLK)
    return lane < HEAD_DIM, own, band, kj < BLK


def _pair_scores(q32, kb, bias2, own, band, is_prev, pen):
    qm = jnp.where(own, jnp.concatenate([q32, q32], axis=0), 0.0)
    s = _dot(qm, kb, NT) * (HEAD_DIM ** -0.5) + bias2
    if pen is not None:
        s = s + jnp.where(is_prev, pen, 0.0)
    return jnp.where(band, s, NEG), qm


HEAD_PAIRS = HEADS_PER_GROUP // 2


def _attn_fwd(qkv, bias, g, dil):
    rows = qkv.shape[0]
    nb = rows // BLK
    rb = 2 if nb % 2 == 0 else 1
    cur = pl.BlockSpec((rb * BLK, GROUP_WIDTH), lambda r, n: (n, r))
    prev = pl.BlockSpec((BLK, GROUP_WIDTH), lambda r, n: (jnp.maximum(rb * n - 1, 0), r))

    def body(c_ref, p_ref, b_ref, o_ref, l_ref):
        n = pl.program_id(1)
        pen0 = jnp.where(n > 0, 0.0, NEG)
        first, own, band, is_prev = _pair_masks()
        o_rows, l_rows = [], []
        for t in range(rb):
            rows_t = slice(t * BLK, (t + 1) * BLK)
            o_parts, l_parts = [], []
            for hp in range(HEAD_PAIRS):
                qc, kc, vc = (slice(k * ATTN_OUT + hp * BLK, k * ATTN_OUT + (hp + 1) * BLK) for k in range(3))
                if t == 0:
                    kp, vp, pen = p_ref[:, kc], p_ref[:, vc], pen0
                else:
                    before = slice((t - 1) * BLK, t * BLK)
                    kp, vp, pen = c_ref[before, kc], c_ref[before, vc], None
                kb = jnp.concatenate([kp, c_ref[rows_t, kc]], axis=0)
                vb = jnp.concatenate([vp, c_ref[rows_t, vc]], axis=0)
                bias2 = jnp.concatenate([b_ref[2 * hp], b_ref[2 * hp + 1]], axis=0)
                s, _ = _pair_scores(c_ref[rows_t, qc].astype(F32), kb, bias2, own, band, is_prev, pen)
                m = jnp.max(s, axis=1, keepdims=True)
                p = jnp.exp(s - m)
                l = jnp.sum(p, axis=1, keepdims=True)
                o2 = _dot(p * (1.0 / l), vb, NN)
                lse2 = m + jnp.log(l)
                o_parts.append(jnp.where(first, o2[:BLK], o2[BLK:]))
                l_parts.append(jnp.where(first, lse2[:BLK], lse2[BLK:]))
            o_rows.append(jnp.concatenate(o_parts, axis=1))
            l_rows.append(jnp.concatenate(l_parts, axis=1))
        o_ref[...] = jnp.concatenate(o_rows, axis=0)
        l_ref[...] = jnp.concatenate(l_rows, axis=0)

    out_spec = pl.BlockSpec((rb * BLK, ATTN_OUT), lambda r, n: (n, r))
    shape = jax.ShapeDtypeStruct((rows, dil * ATTN_OUT), F32)
    return pl.pallas_call(
        body, out_shape=[shape, shape], grid=(dil, nb // rb),
        in_specs=[cur, prev, pl.BlockSpec((HEADS_PER_GROUP, BLK, 2 * BLK), lambda r, n: (g, 0, 0))],
        out_specs=[out_spec, out_spec], compiler_params=_params(2), name=f"attn_fwd_g{g}",
    )(qkv, qkv, bias)


def _attn_bwd(qkv, bias, dy, ya, lt, g, dil):
    rows = qkv.shape[0]
    nb = rows // BLK
    rb = 2 if nb % 2 == 0 else 1
    ns = nb // rb
    cur = lambda w: pl.BlockSpec((rb * BLK, w), lambda r, n: (jnp.minimum(n, ns - 1), r))
    prev = pl.BlockSpec((BLK, GROUP_WIDTH), lambda r, n: (jnp.clip(rb * n - 1, 0, nb - 1), r))

    def body(c_ref, p_ref, b_ref, dy_ref, ya_ref, lt_ref, dz_ref, db_ref, cq_ref, ckv_full, ckv_part):
        r, n = pl.program_id(0), pl.program_id(1)

        @pl.when(jnp.logical_and(r == 0, n == 0))
        def _():
            db_ref[...] = jnp.zeros_like(db_ref)

        @pl.when(n == 0)
        def _():
            cq_ref[...] = jnp.zeros_like(cq_ref)
            ckv_full[...] = jnp.zeros_like(ckv_full)
            ckv_part[...] = jnp.zeros_like(ckv_part)

        def late(dk_add, dv_add):
            kv = ckv_part[...]
            k_last, v_last = kv[:, :ATTN_OUT] + dk_add, kv[:, ATTN_OUT:] + dv_add
            if rb == 1:
                return jnp.concatenate([cq_ref[...], k_last, v_last], axis=1)
            full = ckv_full[...]
            return jnp.concatenate([cq_ref[...],
                                    jnp.concatenate([full[:, :ATTN_OUT], k_last], axis=0),
                                    jnp.concatenate([full[:, ATTN_OUT:], v_last], axis=0)], axis=1)

        @pl.when(n < ns)
        def _():
            pen0 = jnp.where(n > 0, 0.0, NEG)
            first, own, band, is_prev = _pair_masks()
            second = jnp.logical_not(first)
            scale = HEAD_DIM ** -0.5
            db_sum, wides = None, []
            for t in range(rb):
                rows_t = slice(t * BLK, (t + 1) * BLK)
                parts = {k: [] for k in ("dq", "dkp", "dkc", "dvp", "dvc")}
                db_parts = []
                for hp in range(HEAD_PAIRS):
                    cols = slice(hp * BLK, (hp + 1) * BLK)
                    qc, kc, vc = (slice(k * ATTN_OUT + hp * BLK, k * ATTN_OUT + (hp + 1) * BLK) for k in range(3))
                    if t == 0:
                        kp, vp, pen = p_ref[:, kc], p_ref[:, vc], pen0
                    else:
                        before = slice((t - 1) * BLK, t * BLK)
                        kp, vp, pen = c_ref[before, kc], c_ref[before, vc], None
                    kb = jnp.concatenate([kp, c_ref[rows_t, kc]], axis=0)
                    vb = jnp.concatenate([vp, c_ref[rows_t, vc]], axis=0)
                    bias2 = jnp.concatenate([b_ref[2 * hp], b_ref[2 * hp + 1]], axis=0)
                    dy_, lt_ = dy_ref[rows_t, cols], lt_ref[rows_t, cols]
                    dyy = dy_ * ya_ref[rows_t, cols]
                    per_head = lambda v, red, fill: jnp.concatenate(
                        [red(jnp.where(first, v, fill), axis=1, keepdims=True),
                         red(jnp.where(second, v, fill), axis=1, keepdims=True)], axis=0)
                    lse2 = per_head(lt_, jnp.max, NEG)
                    delta2 = per_head(dyy, jnp.sum, 0.0)
                    s, qm = _pair_scores(c_ref[rows_t, qc].astype(F32), kb, bias2, own, band, is_prev, pen)
                    p = jnp.exp(s - lse2)
                    dym = jnp.where(own, jnp.concatenate([dy_, dy_], axis=0), 0.0)
                    ds = p * (_dot(dym, vb, NT) - delta2)
                    db_parts += [ds[:BLK], ds[BLK:]]
                    dq2 = _dot(ds, kb, NN) * scale
                    dkb = _dot(ds, qm, TN) * scale
                    dvb = _dot(p, dym, TN)
                    for k, val in (("dq", jnp.where(first, dq2[:BLK], dq2[BLK:])), ("dkp", dkb[:BLK]),
                                   ("dkc", dkb[BLK:]), ("dvp", dvb[:BLK]), ("dvc", dvb[BLK:])):
                        parts[k].append(val)
                wides.append({k: jnp.concatenate(val, axis=1) for k, val in parts.items()})
                db_t = jnp.stack(db_parts, axis=0)
                db_sum = db_t if db_sum is None else db_sum + db_t
            db_ref[...] += db_sum
            dz_ref[...] = late(wides[0]["dkp"], wides[0]["dvp"]).astype(dz_ref.dtype)
            cq_ref[...] = jnp.concatenate([w["dq"] for w in wides], axis=0)
            if rb == 2:
                ckv_full[...] = jnp.concatenate([wides[0]["dkc"] + wides[1]["dkp"],
                                                 wides[0]["dvc"] + wides[1]["dvp"]], axis=1)
            ckv_part[...] = jnp.concatenate([wides[-1]["dkc"], wides[-1]["dvc"]], axis=1)

        @pl.when(n == ns)
        def _():
            zero = jnp.zeros((BLK, ATTN_OUT), F32)
            dz_ref[...] = late(zero, zero).astype(dz_ref.dtype)

    out_late = pl.BlockSpec((rb * BLK, GROUP_WIDTH), lambda r, n: (jnp.maximum(n - 1, 0), r))
    kv_carry = pltpu.VMEM((BLK, 2 * ATTN_OUT), F32)
    return pl.pallas_call(
        body,
        out_shape=[jax.ShapeDtypeStruct((rows, dil * GROUP_WIDTH), MXU_DTYPE),
                   jax.ShapeDtypeStruct((HEADS_PER_GROUP, BLK, 2 * BLK), F32)],
        grid=(dil, ns + 1),
        in_specs=[cur(GROUP_WIDTH), prev, pl.BlockSpec((HEADS_PER_GROUP, BLK, 2 * BLK), lambda r, n: (g, 0, 0)),
                  cur(ATTN_OUT), cur(ATTN_OUT), cur(ATTN_OUT)],
        out_specs=[out_late, pl.BlockSpec((HEADS_PER_GROUP, BLK, 2 * BLK), lambda r, n: (0, 0, 0))],
        scratch_shapes=[pltpu.VMEM((rb * BLK, ATTN_OUT), F32), kv_carry, kv_carry],
        compiler_params=_params(2), name=f"attn_bwd_g{g}",
    )(qkv, qkv, bias, dy, ya, lt)


def _attn_combine(o_list, l_list):
    dils = [d for _, d in ATTN_GROUPS]
    s = o_list[0].shape[0] * dils[0]
    tr = _pick(s, (512, 256, 128))
    n_g = len(dils)

    def body(*refs):
        o_refs, l_refs = refs[:n_g], refs[n_g:2 * n_g]
        yb_ref = refs[2 * n_g]
        y_refs, lt_refs = refs[2 * n_g + 1:3 * n_g + 1], refs[3 * n_g + 1:4 * n_g + 1]
        scr_o, scr_l = refs[4 * n_g + 1], refs[4 * n_g + 2]
        os_, ls_ = [], []
        for g in range(n_g):
            if dils[g] == 1:
                os_.append(o_refs[g][...])
                ls_.append(l_refs[g][...])
            else:
                _phase_to_tokens(o_refs[g], scr_o, dils[g], ATTN_OUT)
                _phase_to_tokens(l_refs[g], scr_l, dils[g], ATTN_OUT)
                os_.append(_blocks_to_cols(scr_o))
                ls_.append(_blocks_to_cols(scr_l))
        mx = functools.reduce(jnp.maximum, ls_)
        es = [jnp.exp(a - mx) for a in ls_]
        den = functools.reduce(lambda a, b: a + b, es)
        inv = 1.0 / den
        y = functools.reduce(lambda a, b: a + b, [(e * inv) * o for e, o in zip(es, os_)])
        lt = mx + jnp.log(den)
        yb_ref[...] = y.astype(yb_ref.dtype)
        _cols_to_blocks(y, scr_o)
        _cols_to_blocks(lt, scr_l)
        for g in range(n_g):
            if dils[g] == 1:
                y_refs[g][...] = y
                lt_refs[g][...] = lt
            else:
                _tokens_to_phase(scr_o, y_refs[g], dils[g], ATTN_OUT)
                _tokens_to_phase(scr_l, lt_refs[g], dils[g], ATTN_OUT)

    view = lambda d: pl.BlockSpec((tr // d, d * ATTN_OUT), lambda i: (i, 0))
    views = [view(d) for d in dils]
    f = lambda d: jax.ShapeDtypeStruct((s // d, d * ATTN_OUT), F32)
    fs = [f(d) for d in dils]
    scratch = pltpu.VMEM((ATTN_OUT // LANES, tr, LANES), F32)
    res = pl.pallas_call(
        body, out_shape=[jax.ShapeDtypeStruct((s, ATTN_OUT), MXU_DTYPE)] + fs + fs, grid=(s // tr,),
        in_specs=views + views, out_specs=[view(1)] + views + views, scratch_shapes=[scratch, scratch],
        compiler_params=_params(1), name="attn_combine",
    )(*o_list, *l_list)
    return res[0], list(res[1:1 + n_g]), list(res[1 + n_g:])


def _ret_consts(s):
    half = RET_QK // 2
    pos = jnp.arange(s, dtype=F32)
    inv_freq = ROPE_BASE ** (-jnp.arange(half, dtype=F32) / half)
    ang = pos[:, None] * inv_freq[None]
    log_g = jnp.log(1.0 - 2.0 ** (-5.0 - jnp.arange(RET_HEADS, dtype=F32)))
    n = jnp.arange(RET_CHUNK, dtype=F32)
    diff = n[:, None] - n[None, :]
    dmask = jnp.where(diff >= 0, jnp.exp(log_g[:, None, None] * jnp.maximum(diff, 0.0)), 0.0)
    qdec = jnp.exp(log_g[:, None] * (n + 1.0))
    kdec = jnp.exp(log_g[:, None] * (RET_CHUNK - 1.0 - n))
    cdec = jnp.exp(log_g * RET_CHUNK)
    wide = (RET_HEADS, RET_CHUNK, RET_QK)
    return dict(cos=jnp.cos(ang), sin=jnp.sin(ang), dmask=dmask,
                qdec=jnp.broadcast_to(qdec[:, :, None], wide), kdec=jnp.broadcast_to(kdec[:, :, None], wide),
                cdec=cdec)


def _rot(t, cs, sn):
    half = RET_QK // 2
    t1, t2 = t[:, :half], t[:, half:]
    return jnp.concatenate([t1 * cs - t2 * sn, t1 * sn + t2 * cs], axis=1)


def _rot_bwd(d, cs, sn):
    half = RET_QK // 2
    d1, d2 = d[:, :half], d[:, half:]
    return jnp.concatenate([d1 * cs + d2 * sn, d2 * cs - d1 * sn], axis=1)


def _ret_fwd(zq, zk, zv, zg, rc):
    s = zq.shape[0]
    nc = s // RET_CHUNK
    c = RET_CHUNK
    cps = 2 if nc % 2 == 0 else 1

    def body(cd_ref, q_ref, k_ref, v_ref, g_ref, cos_ref, sin_ref, dm_ref, qd_ref, kd_ref,
             yb_ref, o_ref, st_ref, state):
        n = pl.program_id(0)

        @pl.when(n == 0)
        def _():
            state[...] = jnp.zeros_like(state)

        for h in range(RET_HEADS):
            qs, vs = slice(h * RET_QK, (h + 1) * RET_QK), slice(h * RET_V, (h + 1) * RET_V)
            st = state[h]
            for t in range(cps):
                rs = slice(t * c, (t + 1) * c)
                cs, sn = cos_ref[rs, :], sin_ref[rs, :]
                qr = _rot(q_ref[rs, qs], cs, sn)
                kr = _rot(k_ref[rs, qs], cs, sn) * (RET_QK ** -0.5)
                v = v_ref[rs, vs]
                st_ref[h, t] = st.astype(st_ref.dtype)
                sc = _dot(qr, kr, NT) * dm_ref[h]
                o = _dot(sc, v, NN) + _dot(qr * qd_ref[h], st, NN)
                st = st * cd_ref[h] + _dot(kr * kd_ref[h], v, TN)
                o_ref[rs, vs] = o.astype(o_ref.dtype)
                xh, _ = _norm_rows(o, GN_EPS)
                gv = g_ref[rs, vs]
                yb_ref[rs, vs] = (gv * _sigmoid(gv) * xh).astype(yb_ref.dtype)
            state[h] = st

    row = lambda w: pl.BlockSpec((cps * c, w), lambda n: (n, 0))
    const = lambda a: pl.BlockSpec(a.shape, lambda n: (0, 0, 0))
    return pl.pallas_call(
        body,
        out_shape=[jax.ShapeDtypeStruct((s, RET_HEADS * RET_V), MXU_DTYPE),
                   jax.ShapeDtypeStruct((s, RET_HEADS * RET_V), MXU_DTYPE),
                   jax.ShapeDtypeStruct((RET_HEADS, nc, RET_QK, RET_V), MXU_DTYPE)],
        grid=(nc // cps,),
        in_specs=[pl.BlockSpec(memory_space=pltpu.SMEM), row(RET_HEADS * RET_QK), row(RET_HEADS * RET_QK),
                  row(RET_HEADS * RET_V), row(RET_HEADS * RET_V), row(RET_QK // 2), row(RET_QK // 2),
                  const(rc["dmask"]), const(rc["qdec"]), const(rc["kdec"])],
        out_specs=[row(RET_HEADS * RET_V), row(RET_HEADS * RET_V),
                   pl.BlockSpec((RET_HEADS, cps, RET_QK, RET_V), lambda n: (0, n, 0, 0))],
        scratch_shapes=[pltpu.VMEM((RET_HEADS, RET_QK, RET_V), F32)],
        compiler_params=_params(1), name="ret_fwd",
    )(rc["cdec"], zq, zk, zv, zg, rc["cos"], rc["sin"], rc["dmask"], rc["qdec"], rc["kdec"])


def _ret_bwd(zq, zk, zv, zg, o_ret, states, dyb, rc):
    s = zq.shape[0]
    nc = s // RET_CHUNK
    c = RET_CHUNK
    cps = 2 if nc % 2 == 0 else 1
    last = nc // cps - 1

    def body(cd_ref, q_ref, k_ref, v_ref, g_ref, o_ref, st_ref, dy_ref, cos_ref, sin_ref, dm_ref, qd_ref,
             kd_ref, dq_ref, dk_ref, dv_ref, dg_ref, dstate):
        n = pl.program_id(0)

        @pl.when(n == 0)
        def _():
            dstate[...] = jnp.zeros_like(dstate)

        for h in range(RET_HEADS):
            qs, vs = slice(h * RET_QK, (h + 1) * RET_QK), slice(h * RET_V, (h + 1) * RET_V)
            dm, qd, kd = dm_ref[h], qd_ref[h], kd_ref[h]
            ds_next = dstate[h]
            for t in reversed(range(cps)):
                rs = slice(t * c, (t + 1) * c)
                cs, sn = cos_ref[rs, :], sin_ref[rs, :]
                qr = _rot(q_ref[rs, qs], cs, sn)
                kr = _rot(k_ref[rs, qs], cs, sn) * (RET_QK ** -0.5)
                v = v_ref[rs, vs]
                st = st_ref[h, t]
                xh, rstd = _norm_rows(o_ref[rs, vs].astype(F32), GN_EPS)
                gv, dy = g_ref[rs, vs], dy_ref[rs, vs].astype(F32)
                sg = _sigmoid(gv)
                dg_ref[rs, vs] = (dy * xh * (sg * (1.0 + gv * (1.0 - sg)))).astype(dg_ref.dtype)
                do = _norm_rows_bwd(dy * (gv * sg), xh, rstd)
                sc = _dot(qr, kr, NT) * dm
                da = _dot(do, v, NT) * dm
                dv = _dot(sc, do, TN) + _dot(kr * kd, ds_next, NN)
                dqr = _dot(da, kr, NN) + _dot(do, st, NT) * qd
                dkr = _dot(da, qr, TN) + _dot(v, ds_next, NT) * kd
                ds_next = ds_next * cd_ref[h] + _dot(qr * qd, do, TN)
                dq_ref[rs, qs] = _rot_bwd(dqr, cs, sn).astype(dq_ref.dtype)
                dk_ref[rs, qs] = _rot_bwd(dkr * (RET_QK ** -0.5), cs, sn).astype(dk_ref.dtype)
                dv_ref[rs, vs] = dv.astype(dv_ref.dtype)
            dstate[h] = ds_next

    row = lambda w: pl.BlockSpec((cps * c, w), lambda n: (last - n, 0))
    const = lambda a: pl.BlockSpec(a.shape, lambda n: (0, 0, 0))
    qk_w, v_w = RET_HEADS * RET_QK, RET_HEADS * RET_V
    g_qk, g_v = jax.ShapeDtypeStruct((s, qk_w), MXU_DTYPE), jax.ShapeDtypeStruct((s, v_w), MXU_DTYPE)
    return pl.pallas_call(
        body,
        out_shape=[g_qk, g_qk, g_v, g_v],
        grid=(nc // cps,),
        in_specs=[pl.BlockSpec(memory_space=pltpu.SMEM), row(qk_w), row(qk_w), row(v_w), row(v_w), row(v_w),
                  pl.BlockSpec((RET_HEADS, cps, RET_QK, RET_V), lambda n: (0, last - n, 0, 0)), row(v_w),
                  row(RET_QK // 2), row(RET_QK // 2), const(rc["dmask"]), const(rc["qdec"]), const(rc["kdec"])],
        out_specs=[row(qk_w), row(qk_w), row(v_w), row(v_w)],
        scratch_shapes=[pltpu.VMEM((RET_HEADS, RET_QK, RET_V), F32)],
        compiler_params=_params(1), name="ret_bwd",
    )(rc["cdec"], zq, zk, zv, zg, o_ret, states, dyb, rc["cos"], rc["sin"], rc["dmask"], rc["qdec"],
      rc["kdec"])


ROW_TILE = 256


def _mix_fwd(x, ya_b, yb_b, ze, wap_t, wrp, wout, gam, bet):
    s = x.shape[0]
    tm = ROW_TILE

    def body(x_ref, ya_ref, yb_ref, ze_ref, wap_ref, wrp_ref, wout_ref, g_ref, b_ref,
             x1_ref, u1_ref, a1_ref, a2_ref, mg_ref, x1b_ref):
        a1 = _dot(ya_ref[...], wap_ref[...], NT)
        a2 = _dot(yb_ref[...], wrp_ref[...], NN)
        ze_ = ze_ref[...]
        merged = _sigmoid(ze_[:, :D_MODEL]) * a1 + _sigmoid(ze_[:, D_MODEL:]) * a2
        u = ALPHA * x_ref[...] + _dot(merged, wout_ref[...], NN)
        xh, _ = _norm_rows(u, LN_EPS)
        x1 = xh * g_ref[...] + b_ref[...]
        x1_ref[...] = x1
        x1b_ref[...] = x1.astype(x1b_ref.dtype)
        u1_ref[...] = u
        a1_ref[...] = a1.astype(a1_ref.dtype)
        a2_ref[...] = a2.astype(a2_ref.dtype)
        mg_ref[...] = merged.astype(mg_ref.dtype)

    row = lambda w: pl.BlockSpec((tm, w), lambda i: (i, 0))
    full = _resident
    f = jax.ShapeDtypeStruct((s, D_MODEL), F32)
    m = jax.ShapeDtypeStruct((s, D_MODEL), MXU_DTYPE)
    return pl.pallas_call(
        body, out_shape=[f, f, m, m, m, m], grid=(s // tm,),
        in_specs=[row(D_MODEL), row(ATTN_OUT), row(RET_HEADS * RET_V), row(2 * D_MODEL), full(wap_t), full(wrp),
                  full(wout), full(gam), full(bet)],
        out_specs=[row(D_MODEL)] * 6, compiler_params=_params(1), name="mix_fwd",
    )(x, ya_b, yb_b, ze, wap_t, wrp, wout, gam, bet)


def _mix_bwd(dx1, u1, a1, a2, ze, wap_t, wrp, wout, gam):
    s = dx1.shape[0]
    tm = ROW_TILE

    dils = [d for _, d in ATTN_GROUPS]

    def body(dx_ref, u_ref, a1_ref, a2_ref, ze_ref, wap_ref, wrp_ref, wout_ref, g_ref,
             dres_ref, du_ref, da1_ref, da2_ref, dze_ref, dyb_ref, dgam_ref, dbet_ref, *rest):
        dya_refs, dya_scr = rest[:len(dils)], rest[len(dils)]
        @pl.when(pl.program_id(0) == 0)
        def _():
            dgam_ref[...] = jnp.zeros_like(dgam_ref)
            dbet_ref[...] = jnp.zeros_like(dbet_ref)

        dx = dx_ref[...]
        xh, rstd = _norm_rows(u_ref[...], LN_EPS)
        dgam_ref[...] += jnp.sum(dx * xh, axis=0, keepdims=True)
        dbet_ref[...] += jnp.sum(dx, axis=0, keepdims=True)
        du = _norm_rows_bwd(dx * g_ref[...], xh, rstd)
        dres_ref[...] = ALPHA * du
        du_ref[...] = du.astype(du_ref.dtype)
        dm = _dot(du, wout_ref[...], NT)
        ze_ = ze_ref[...]
        sa, sb = _sigmoid(ze_[:, :D_MODEL]), _sigmoid(ze_[:, D_MODEL:])
        da1, da2 = dm * sa, dm * sb
        dze_ref[...] = jnp.concatenate([dm * a1_ref[...] * (sa * (1.0 - sa)),
                                        dm * a2_ref[...] * (sb * (1.0 - sb))], axis=1).astype(dze_ref.dtype)
        da1_ref[...] = da1.astype(da1_ref.dtype)
        da2_ref[...] = da2.astype(da2_ref.dtype)
        dyb_ref[...] = _dot(da2, wrp_ref[...], NT).astype(dyb_ref.dtype)
        dya = _dot(da1, wap_ref[...], NN)
        _cols_to_blocks(dya, dya_scr)
        for g, d in enumerate(dils):
            if d == 1:
                dya_refs[g][...] = dya
            else:
                _tokens_to_phase(dya_scr, dya_refs[g], d, ATTN_OUT)

    row = lambda w: pl.BlockSpec((tm, w), lambda i: (i, 0))
    full = _resident
    vec = pl.BlockSpec((1, D_MODEL), lambda i: (0, 0))
    f = lambda w: jax.ShapeDtypeStruct((s, w), F32)
    m = lambda w: jax.ShapeDtypeStruct((s, w), MXU_DTYPE)
    v = jax.ShapeDtypeStruct((1, D_MODEL), F32)
    res = pl.pallas_call(
        body,
        out_shape=[f(D_MODEL), m(D_MODEL), m(D_MODEL), m(D_MODEL), m(2 * D_MODEL), m(RET_HEADS * RET_V), v, v]
        + [jax.ShapeDtypeStruct((s // d, d * ATTN_OUT), F32) for d in dils],
        grid=(s // tm,),
        in_specs=[row(D_MODEL)] * 4 + [row(2 * D_MODEL), full(wap_t), full(wrp), full(wout), full(gam)],
        out_specs=[row(D_MODEL)] * 4 + [row(2 * D_MODEL), row(RET_HEADS * RET_V), vec, vec]
        + [pl.BlockSpec((tm // d, d * ATTN_OUT), lambda i: (i, 0)) for d in dils],
        scratch_shapes=[pltpu.VMEM((ATTN_OUT // LANES, tm, LANES), F32)],
        compiler_params=_params(1), name="mix_bwd",
    )(dx1, u1, a1, a2, ze, wap_t, wrp, wout, gam)
    return (*res[:8], list(res[8:]))


FF_CHUNK = 1408


def _ffn_fwd(x1, wg_t, wu_t, wd, gam, bet):
    s = x1.shape[0]
    tm, fc = ROW_TILE, FF_CHUNK

    def body(x_ref, wg_ref, wu_ref, wd_ref, g_ref, b_ref, x2_ref, u2_ref, a_ref, b_out_ref, h_ref, x2b_ref):
        xv = x_ref[...]
        u = ALPHA * xv
        for f0 in range(0, D_FF, fc):
            ch = slice(f0, f0 + fc)
            a = _dot(xv, wg_ref[ch, :], NT)
            b = _dot(xv, wu_ref[ch, :], NT)
            hid = a * _sigmoid(a) * b
            u = u + _dot(hid, wd_ref[ch, :], NN)
            a_ref[:, ch] = a.astype(a_ref.dtype)
            b_out_ref[:, ch] = b.astype(b_out_ref.dtype)
            h_ref[:, ch] = hid.astype(h_ref.dtype)
        xh, _ = _norm_rows(u, LN_EPS)
        u2_ref[...] = u
        x2 = xh * g_ref[...] + b_ref[...]
        x2_ref[...] = x2
        x2b_ref[...] = x2.astype(x2b_ref.dtype)

    row = lambda w: pl.BlockSpec((tm, w), lambda i: (i, 0))
    f = lambda w: jax.ShapeDtypeStruct((s, w), F32)
    m = lambda w: jax.ShapeDtypeStruct((s, w), MXU_DTYPE)
    return pl.pallas_call(
        body, out_shape=[f(D_MODEL), f(D_MODEL), m(D_FF), m(D_FF), m(D_FF), m(D_MODEL)],
        grid=(s // tm,),
        in_specs=[row(D_MODEL), _resident(wg_t), _resident(wu_t), _resident(wd), _resident(gam), _resident(bet)],
        out_specs=[row(D_MODEL), row(D_MODEL), row(D_FF), row(D_FF), row(D_FF), row(D_MODEL)],
        compiler_params=_params(1), name="ffn_fwd",
    )(x1, wg_t, wu_t, wd, gam, bet)


def _ffn_bwd(dx2, u2, fa, fb, wg_t, wu_t, wd, gam):
    s = dx2.shape[0]
    tm, fc = ROW_TILE, FF_CHUNK

    def body(dx_ref, u_ref, a_ref, b_ref, wg_ref, wu_ref, wd_ref, g_ref,
             dx1_ref, du_ref, da_ref, db_ref, dgam_ref, dbet_ref):
        @pl.when(pl.program_id(0) == 0)
        def _():
            dgam_ref[...] = jnp.zeros_like(dgam_ref)
            dbet_ref[...] = jnp.zeros_like(dbet_ref)

        dx = dx_ref[...]
        xh, rstd = _norm_rows(u_ref[...], LN_EPS)
        dgam_ref[...] += jnp.sum(dx * xh, axis=0, keepdims=True)
        dbet_ref[...] += jnp.sum(dx, axis=0, keepdims=True)
        du = _norm_rows_bwd(dx * g_ref[...], xh, rstd)
        du_ref[...] = du.astype(du_ref.dtype)
        acc = ALPHA * du
        for f0 in range(0, D_FF, fc):
            ch = slice(f0, f0 + fc)
            dh = _dot(du, wd_ref[ch, :], NT)
            a, b = a_ref[:, ch].astype(F32), b_ref[:, ch].astype(F32)
            sg = _sigmoid(a)
            da = dh * b * (sg * (1.0 + a * (1.0 - sg)))
            db = dh * (a * sg)
            acc = acc + _dot(da, wg_ref[ch, :], NN) + _dot(db, wu_ref[ch, :], NN)
            da_ref[:, ch] = da.astype(da_ref.dtype)
            db_ref[:, ch] = db.astype(db_ref.dtype)
        dx1_ref[...] = acc

    row = lambda w: pl.BlockSpec((tm, w), lambda i: (i, 0))
    vec = pl.BlockSpec((1, D_MODEL), lambda i: (0, 0))
    v = jax.ShapeDtypeStruct((1, D_MODEL), F32)
    return pl.pallas_call(
        body,
        out_shape=[jax.ShapeDtypeStruct((s, D_MODEL), F32), jax.ShapeDtypeStruct((s, D_MODEL), MXU_DTYPE),
                   jax.ShapeDtypeStruct((s, D_FF), MXU_DTYPE), jax.ShapeDtypeStruct((s, D_FF), MXU_DTYPE), v, v],
        grid=(s // tm,),
        in_specs=[row(D_MODEL), row(D_MODEL), row(D_FF), row(D_FF), _resident(wg_t), _resident(wu_t),
                  _resident(wd), _resident(gam)],
        out_specs=[row(D_MODEL), row(D_MODEL), row(D_FF), row(D_FF), vec, vec],
        compiler_params=_params(1), name="ffn_bwd",
    )(dx2, u2, fa, fb, wg_t, wu_t, wd, gam)


def _loss_head(y, target):
    s = y.shape[0]
    tr = _pick(s, (512, 256, 128))

    def body(y_ref, t_ref, dy_ref, l_ref):
        @pl.when(pl.program_id(0) == 0)
        def _():
            l_ref[...] = jnp.zeros_like(l_ref)

        e = y_ref[...] - t_ref[...]
        dy_ref[...] = e * (1.0 / D_MODEL)
        part = jnp.sum(jnp.sum(e * e, axis=1, keepdims=True), axis=0, keepdims=True)
        l_ref[...] += part * (0.5 / D_MODEL)

    spec = pl.BlockSpec((tr, D_MODEL), lambda i: (i, 0))
    dy, part = pl.pallas_call(
        body, out_shape=[jax.ShapeDtypeStruct((s, D_MODEL), F32), jax.ShapeDtypeStruct((8, BLK), F32)],
        grid=(s // tr,), in_specs=[spec, spec], out_specs=[spec, pl.BlockSpec((8, BLK), lambda i: (0, 0))],
        compiler_params=_params(1), name="loss_head",
    )(y, target)
    return dy, part[0, 0]


def _adamw(w, g, m, v, name):
    rows, cols = w.shape
    budget = 1 << 20
    cands = [t for t in range(8, rows + 1, 8) if rows % t == 0 and t * cols * 4 <= budget]
    tr = max(cands) if cands else rows

    def body(w_ref, g_ref, m_ref, v_ref, d_ref, nm_ref, nv_ref):
        gv = g_ref[...]
        mn = ADAM_B1 * m_ref[...] + (1.0 - ADAM_B1) * gv
        vn = ADAM_B2 * v_ref[...] + (1.0 - ADAM_B2) * (gv * gv)
        m_hat = mn / (1.0 - ADAM_B1 ** ADAM_STEP)
        v_hat = vn / (1.0 - ADAM_B2 ** ADAM_STEP)
        d_ref[...] = -ADAM_LR * (m_hat / (jnp.sqrt(v_hat) + ADAM_EPS) + ADAM_WD * w_ref[...])
        nm_ref[...] = mn
        nv_ref[...] = vn

    spec = pl.BlockSpec((tr, cols), lambda i: (i, 0))
    shape = jax.ShapeDtypeStruct((rows, cols), F32)
    return pl.pallas_call(
        body, out_shape=[shape, shape, shape], grid=(rows // tr,), in_specs=[spec] * 4, out_specs=[spec] * 3,
        compiler_params=_params(1), name=name,
    )(w, g, m, v)


MESH_ID = pl.DeviceIdType.MESH
ANY = pl.BlockSpec(memory_space=pl.ANY)


def _place():
    x, y, c = lax.axis_index("x"), lax.axis_index("y"), lax.axis_index("c")
    other_chips = [(1 - x, y), (x, 1 - y), (1 - x, 1 - y)]
    return x, y, c, other_chips


def _chip_no(chip):
    return 2 * chip[0] + chip[1]


def _gather_exchange(f_ref, g_ref, send_sems, recv_sems, hr):
    x, y, c, chips = _place()
    sibling = (x, y, 1 - c)

    def piece(chip, half):
        return g_ref.at[_chip_no(chip), pl.ds(half * hr, hr), :]

    def copy(k, src, dst, to):
        return pltpu.make_async_remote_copy(src_ref=src, dst_ref=dst, send_sem=send_sems.at[k],
                                            recv_sem=recv_sems.at[k], device_id=to, device_id_type=MESH_ID)

    own = copy(6, f_ref, g_ref.at[_chip_no((x, y))], sibling)
    own.start()
    my_half = f_ref.at[pl.ds(c * hr, hr), :]
    first = [copy(k, my_half, piece((x, y), c), (*chip, c)) for k, chip in enumerate(chips)]
    for cp in first:
        cp.start()
    passed = [copy(3 + k, piece(chip, c), piece(chip, c), sibling) for k, chip in enumerate(chips)]
    for k, chip in enumerate(chips):
        copy(k, my_half, piece(chip, c), (*chip, c)).wait_recv()
        passed[k].start()
    for k, chip in enumerate(chips):
        copy(3 + k, my_half, piece(chip, 1 - c), sibling).wait_recv()
    for cp in first + passed:
        cp.wait_send()
    own.wait()


GATHER_SEMS = (pltpu.SemaphoreType.DMA((7,)), pltpu.SemaphoreType.DMA((7,)))


HBM = pltpu.MemorySpace.HBM
ONE_SEM_PAIR = (pltpu.SemaphoreType.DMA, pltpu.SemaphoreType.DMA)


def _sibling_of_me():
    x, y, c, _ = _place()
    return [(x, y, 1 - c)]


def _same_core_of_other_chips():
    x, y, c, chips = _place()
    return [(*chip, c) for chip in chips]


def _on_sequencer(name, collective_id, sems, peers, exchange):
    @pl.kernel(mesh=plsc.ScalarSubcoreMesh(axis_name="sequencer", num_cores=1), name=name, scratch_types=sems,
               compiler_params=pltpu.CompilerParams(collective_id=collective_id))
    def launch(*sem_refs):
        barrier = pltpu.get_barrier_semaphore()
        devices = peers()
        for peer in devices:
            pl.semaphore_signal(barrier, inc=1, device_id=peer, device_id_type=MESH_ID)
        pl.semaphore_wait(barrier, len(devices))
        exchange(*sem_refs)

    launch()


def _all_gather_rows(flat, behind=None):
    r, cols = flat.shape
    out = jax.ShapeDtypeStruct((N_CHIPS, r, cols), flat.dtype)
    if behind is not None:
        f_ref, g_ref = jax.new_ref(flat, memory_space=HBM), jax.empty_ref(out, memory_space=HBM)
        _on_sequencer(f"all_gather_weights_behind_{behind[1]}", behind[0], GATHER_SEMS,
                      lambda: _sibling_of_me() + _same_core_of_other_chips(),
                      lambda s, r_: _gather_exchange(f_ref, g_ref, s, r_, r // 2))
        return g_ref[...]

    def body(f_ref, g_ref, send_sems, recv_sems):
        _gather_exchange(f_ref, g_ref, send_sems, recv_sems, r // 2)

    return pl.pallas_call(body, out_shape=out, in_specs=[ANY], out_specs=ANY, scratch_shapes=list(GATHER_SEMS),
                          name="all_gather_weights")(flat)


def _swap_halves(gall, behind=None):
    _, r, cols = gall.shape
    hr = r // 2
    out = jax.ShapeDtypeStruct((N_CHIPS, hr, cols), gall.dtype)

    def exchange(g_ref, a_ref, send_sem, recv_sem):
        x, y, c, _ = _place()
        cp = pltpu.make_async_remote_copy(src_ref=g_ref.at[:, pl.ds((1 - c) * hr, hr), :], dst_ref=a_ref,
                                          send_sem=send_sem, recv_sem=recv_sem, device_id=(x, y, 1 - c),
                                          device_id_type=MESH_ID)
        cp.start()
        cp.wait()

    if behind is not None:
        g_ref, a_ref = jax.new_ref(gall, memory_space=HBM), jax.empty_ref(out, memory_space=HBM)
        _on_sequencer(f"grad_swap_halves_behind_{behind[1]}", behind[0], ONE_SEM_PAIR, _sibling_of_me,
                      lambda s, r_: exchange(g_ref, a_ref, s, r_))
        return a_ref[...]

    def body(g_ref, a_ref, send_sem, recv_sem):
        exchange(g_ref, a_ref, send_sem, recv_sem)

    return pl.pallas_call(body, out_shape=out, in_specs=[ANY], out_specs=ANY, scratch_shapes=list(ONE_SEM_PAIR),
                          name="grad_swap_halves")(gall)


def _scatter_to_chips(p, behind=None):
    _, hr, cols = p.shape
    out = jax.ShapeDtypeStruct((3, hr, cols), p.dtype)
    sems = (pltpu.SemaphoreType.DMA((3,)), pltpu.SemaphoreType.DMA((3,)))

    def exchange(p_ref, b_ref, send_sems, recv_sems):
        x, y, c, chips = _place()
        cps = [pltpu.make_async_remote_copy(src_ref=p_ref.at[_chip_no(chip)], dst_ref=b_ref.at[k],
                                            send_sem=send_sems.at[k], recv_sem=recv_sems.at[k],
                                            device_id=(*chip, c), device_id_type=MESH_ID)
               for k, chip in enumerate(chips)]
        for cp in cps:
            cp.start()
        for cp in cps:
            cp.wait()

    if behind is not None:
        p_ref, b_ref = jax.new_ref(p, memory_space=HBM), jax.empty_ref(out, memory_space=HBM)
        _on_sequencer(f"grad_scatter_chips_behind_{behind[1]}", behind[0], sems, _same_core_of_other_chips,
                      lambda s, r_: exchange(p_ref, b_ref, s, r_))
        return b_ref[...]

    def body(p_ref, b_ref, send_sems, recv_sems):
        exchange(p_ref, b_ref, send_sems, recv_sems)

    return pl.pallas_call(body, out_shape=out, in_specs=[ANY], out_specs=ANY, scratch_shapes=list(sems),
                          name="grad_scatter_chips")(p)


def _share_with_sibling(full, behind=None):
    r, cols = full.shape
    hr = r // 2

    def exchange(in_ref, out_ref, send_sem, recv_sem):
        x, y, c, _ = _place()
        cp = pltpu.make_async_remote_copy(src_ref=in_ref.at[pl.ds(c * hr, hr), :],
                                          dst_ref=out_ref.at[pl.ds(c * hr, hr), :],
                                          send_sem=send_sem, recv_sem=recv_sem, device_id=(x, y, 1 - c),
                                          device_id_type=MESH_ID)
        cp.start()
        cp.wait()

    if behind is not None:
        full_ref = jax.new_ref(full, memory_space=HBM)
        _on_sequencer(f"grad_share_sibling_behind_{behind[1]}", behind[0], ONE_SEM_PAIR, _sibling_of_me,
                      lambda s, r_: exchange(full_ref, full_ref, s, r_))
        return full_ref[...]

    def body(in_ref, out_ref, send_sem, recv_sem):
        exchange(in_ref, out_ref, send_sem, recv_sem)

    return pl.pallas_call(
        body, out_shape=jax.ShapeDtypeStruct((r, cols), full.dtype), in_specs=[ANY], out_specs=ANY,
        input_output_aliases={0: 0}, scratch_shapes=list(ONE_SEM_PAIR), name="grad_share_sibling",
    )(full)


def _all_reduce_small(v):
    r, cols = v.shape
    n_dev = 8

    def body(x_ref, out_ref, gat_ref, send_sems, recv_sems, local_sem):
        x, y, c, chips = _place()
        me, sibling = (x, y, c), (x, y, 1 - c)

        def slot(px, py, pc):
            return gat_ref.at[4 * px + 2 * py + pc]

        def copy(k, block, to, src=None):
            return pltpu.make_async_remote_copy(src_ref=slot(*block) if src is None else src, dst_ref=slot(*block),
                                                send_sem=send_sems.at[k], recv_sem=recv_sems.at[k], device_id=to,
                                                device_id_type=MESH_ID)

        mine = pltpu.make_async_copy(x_ref, slot(*me), local_sem)
        mine.start()
        first = [copy(0, me, sibling, src=x_ref)]
        first += [copy(1 + k, me, (*chip, c), src=x_ref) for k, chip in enumerate(chips)]
        for cp in first:
            cp.start()
        passed = [copy(4 + k, (*chip, c), sibling) for k, chip in enumerate(chips)]
        for k, chip in enumerate(chips):
            copy(1 + k, (*chip, c), me).wait_recv()
            passed[k].start()
        copy(0, sibling, me).wait_recv()
        for k, chip in enumerate(chips):
            copy(4 + k, (*chip, 1 - c), me).wait_recv()
        for cp in first + passed:
            cp.wait_send()
        mine.wait()
        acc = gat_ref[0]
        for d in range(1, n_dev):
            acc = acc + gat_ref[d]
        out_ref[...] = acc

    vmem = pl.BlockSpec(memory_space=pltpu.VMEM)
    return pl.pallas_call(
        body, out_shape=jax.ShapeDtypeStruct((r, cols), v.dtype), in_specs=[vmem], out_specs=vmem,
        scratch_shapes=[pltpu.VMEM((n_dev, r, cols), v.dtype), pltpu.SemaphoreType.DMA((7,)),
                        pltpu.SemaphoreType.DMA((7,)), pltpu.SemaphoreType.DMA],
        name="all_reduce_small",
    )(v)


WIRE_DTYPE = jnp.bfloat16


def _add_own_half(gall, got, place):
    _, r, cols = gall.shape
    hr = r // 2
    tr = _row_tile(hr)
    g4 = gall.reshape(N_CHIPS, 2, hr, cols)

    def body(s_ref, g_ref, a_ref, o_ref):
        o_ref[...] = (g_ref[0] + a_ref[...]).astype(o_ref.dtype)

    return pl.pallas_call(
        body, out_shape=jax.ShapeDtypeStruct((N_CHIPS, hr, cols), WIRE_DTYPE),
        grid_spec=pltpu.PrefetchScalarGridSpec(
            num_scalar_prefetch=1, grid=(N_CHIPS, hr // tr),
            in_specs=[pl.BlockSpec((1, 1, tr, cols), lambda j, i, s: (j, s[1], i, 0)),
                      pl.BlockSpec((1, tr, cols), lambda j, i, s: (j, i, 0))],
            out_specs=pl.BlockSpec((1, tr, cols), lambda j, i, s: (j, i, 0))),
        compiler_params=_params(2), name="grad_add_halves",
    )(place, g4, got)


def _add_chip_parts(gall, got, parts, place):
    _, r, cols = gall.shape
    hr = r // 2
    tr = _row_tile(hr)
    g4 = gall.reshape(N_CHIPS, 2, hr, cols)
    nt = hr // tr

    def body(s_ref, g_ref, a_ref, b0_ref, b1_ref, b2_ref, o_ref):
        own = g_ref[0, 0] + a_ref[0]
        o_ref[...] = ((own + b0_ref[0].astype(F32)) + b1_ref[0].astype(F32)) + b2_ref[0].astype(F32)

    part = lambda k: pl.BlockSpec((1, tr, cols), lambda i, s: (k, i, 0))
    return pl.pallas_call(
        body, out_shape=jax.ShapeDtypeStruct((r, cols), F32),
        grid_spec=pltpu.PrefetchScalarGridSpec(
            num_scalar_prefetch=1, grid=(nt,),
            in_specs=[pl.BlockSpec((1, 1, tr, cols), lambda i, s: (s[0], s[1], i, 0)),
                      pl.BlockSpec((1, tr, cols), lambda i, s: (s[0], i, 0)), part(0), part(1), part(2)],
            out_specs=pl.BlockSpec((tr, cols), lambda i, s: (s[1] * nt + i, 0))),
        compiler_params=_params(1), name="grad_add_chips",
    )(place, g4, got, parts, parts, parts)


def _pack_shards(weights, l, group, dtype):
    parts = []
    for name, _ in group:
        w = weights[name][l]
        parts.append((w.T if name in COLUMN_SHARDED else w).reshape(-1, D_MODEL).astype(dtype))
    return parts[0] if len(parts) == 1 else jnp.concatenate(parts, axis=0)


def _unpack_gathered(gathered, group):
    w, off = {}, 0
    for name, rows in group:
        blk = gathered[:, off:off + rows]
        if name == "w_attn_proj":
            w[name] = blk.reshape(N_CHIPS * 256, ATTN_OUT)
        else:
            w[name] = blk.reshape(N_CHIPS * rows, D_MODEL)
        off += rows
    return w


def _pack_full_grads(g, group):
    parts = [g[name].reshape(N_CHIPS, rows, D_MODEL) for name, rows in group]
    return parts[0] if len(parts) == 1 else jnp.concatenate(parts, axis=1)


def _unpack_shard_grads(fulls, gi):
    out, off = {}, 0
    for name, rows in PACK_GROUPS[gi]:
        blks = [full[off:off + rows] for full in fulls]
        out[name] = jnp.stack([b.reshape(256, ATTN_OUT) if name == "w_attn_proj" else b for b in blks])
        off += rows
    return out


class _StaticWeights:
    def __init__(self, layers):
        self.layers = layers

    def group(self, l, gi):
        return self.layers[l]

    def layer(self, l):
        return self.layers[l]


class _WeightGathers:
    def __init__(self, weights):
        self.flats = {(l, gi): _pack_shards(weights, l, g, MXU_DTYPE)
                      for l in range(DEPTH) for gi, g in enumerate(PACK_GROUPS)}
        self.order = sorted(self.flats)
        self.landed, self.unpacked, self.issued = {}, {}, 0
        self._issue()

    def _issue(self, after=None):
        key = self.order[self.issued]
        flat = self.flats[key]
        if self.issued == 0:
            self.landed[key] = _all_gather_rows(flat, behind=(1, f"l{key[0]}g{key[1]}"))
        else:
            prev = self.order[self.issued - 1]
            ties = (self.landed[prev], flat) if after is None else (self.landed[prev], flat, after)
            tied = lax.optimization_barrier(ties)
            self.landed[prev], flat = tied[0], tied[1]
            after = tied[2] if after is not None else None
            self.landed[key] = _all_gather_rows(flat, behind=(1, f"l{key[0]}g{key[1]}"))
        self.issued += 1
        return after

    def prefetch(self, after):
        return self._issue(after) if self.issued < len(self.order) else after

    def group(self, l, gi):
        key = (l, gi)
        if key not in self.unpacked:
            self.unpacked[key] = _unpack_gathered(self.landed[key], PACK_GROUPS[gi])
        return self.unpacked[key]

    def layer(self, l):
        w = {}
        for gi in range(len(PACK_GROUPS)):
            w.update(self.group(l, gi))
        return w


class _GradReduction:
    LAST = 3

    def __init__(self, gw, group, place, behind):
        self.gall, self.place, self.behind, self.stage = _pack_full_grads(gw, group), place, behind, 0
        self.pending = None

    def _how(self, k):
        return None if self.behind is None else (self.behind[0][k], self.behind[1])

    def advance(self, value=None):
        if self.stage > self.LAST:
            return value
        if self.stage > 0 and value is not None:
            value, self.pending = lax.optimization_barrier((value, self.pending))
        if self.stage == 0:
            self.pending = _swap_halves(self.gall, self._how(0))
        elif self.stage == 1:
            self.got = self.pending
            self.pending = _scatter_to_chips(_add_own_half(self.gall, self.got, self.place), self._how(1))
        elif self.stage == 2:
            self.pending = _share_with_sibling(
                _add_chip_parts(self.gall, self.got, self.pending, self.place), self._how(2))
        self.stage += 1
        return value

    def result(self):
        while self.stage < self.LAST:
            self.advance()
        return self.pending


class _GradReductions:
    def __init__(self, place):
        self.place, self.all = place, {}

    def start(self, l, gi, gw):
        slot = len(self.all) % 2
        how = ((2 + 3 * slot, 3 + 3 * slot, 4 + 3 * slot), f"l{l}g{gi}")
        red = _GradReduction(gw, PACK_GROUPS[gi], self.place, how)
        red.advance()
        self.all[(l, gi)] = red

    def tick(self, value):
        for red in self.all.values():
            if red.behind is not None:
                value = red.advance(value)
        return value

    def results(self):
        return {key: red.result() for key, red in self.all.items()}


def _layer_fwd(x, x_b, weights, l, lnp, bias, rc, prefetch):
    tag = f"l{l}"
    w_in = weights.group(l, IN_GROUP)["w_in"]
    z = {}
    for part, secs in enumerate(DX_PARTS):
        runs = tuple(run for _, sec_runs, _, _ in secs for run in sec_runs)
        outs = _in_proj(x_b, _section_rows(w_in, runs), _section_rows(lnp["b_in"], runs), secs,
                        name=f"in_proj_{part}_{tag}")
        z.update({name: o for (name, _, _, _), o in zip(secs, outs)})
        if part == 0:
            z["a0"] = prefetch(z["a0"])
    w = weights.group(l, REST_GROUP)
    o_list, l_list = [], []
    for g, (_, dil) in enumerate(ATTN_GROUPS):
        o, lse = _attn_fwd(z[f"a{g}"], bias, g, dil)
        o_list.append(o)
        l_list.append(lse)
    ya_b, ya_views, lt_views = _attn_combine(o_list, l_list)
    yb_b, o_ret, states = _ret_fwd(z["bq"], z["bk"], z["c"], z["d"], rc)
    x1, u1, a1, a2, mg_b, x1_b = _mix_fwd(x, ya_b, yb_b, z["e"], w["w_attn_proj"], w["w_ret_proj"], w["w_out"],
                                          lnp["ln1_g"], lnp["ln1_b"])
    x1 = prefetch(x1)
    x2, u2, fa, fb, h_b, x2_b = _ffn_fwd(x1, w["w_ffn_gate"], w["w_ffn_up"], w["w_ffn_down"], lnp["ln2_g"],
                                         lnp["ln2_b"])
    saved = dict(x_b=x_b, z=z, ya_views=ya_views, ya_b=ya_b, lt_views=lt_views, yb_b=yb_b, o_ret=o_ret, states=states,
                 x1_b=x1_b, u1=u1, a1=a1, a2=a2, mg_b=mg_b, u2=u2, fa=fa, fb=fb, h_b=h_b)
    return x2, x2_b, saved


def _layer_bwd(dx2, w, l, lnp, sv, bias, rc, reductions=None):
    s = dx2.shape[0]
    tag = f"l{l}"
    z = sv["z"]
    step = reductions.tick if reductions is not None else (lambda v: v)
    dx1, du2_b, da_b, db_b, dg2, dbt2 = _ffn_bwd(dx2, sv["u2"], sv["fa"], sv["fb"], w["w_ffn_gate"], w["w_ffn_up"],
                                                  w["w_ffn_down"], lnp["ln2_g"])
    dx1 = step(dx1)
    gw = {}
    gw["w_ffn_down"] = _matmul(sv["h_b"], du2_b, "tn", name=f"dw_ffn_down_{tag}")
    gw["w_ffn_gate"] = _matmul(da_b, sv["x1_b"], "tn", name=f"dw_ffn_gate_{tag}")
    gw["w_ffn_up"] = _matmul(db_b, sv["x1_b"], "tn", name=f"dw_ffn_up_{tag}")
    dres, du1_b, da1_b, da2_b, dze, dyb, dg1, dbt1, dya_views = _mix_bwd(
        dx1, sv["u1"], sv["a1"], sv["a2"], z["e"], w["w_attn_proj"], w["w_ret_proj"], w["w_out"], lnp["ln1_g"])
    gw["w_out"] = _matmul(sv["mg_b"], du1_b, "tn", name=f"dw_out_{tag}")
    gw["w_attn_proj"] = _matmul(da1_b, sv["ya_b"], "tn", name=f"dw_attn_proj_{tag}")
    gw["w_ret_proj"] = _matmul(sv["yb_b"], da2_b, "tn", name=f"dw_ret_proj_{tag}")
    dyb = step(dyb)
    if reductions is not None:
        reductions.start(l, REST_GROUP, gw)
    dzq, dzk, dzv, dzg = _ret_bwd(z["bq"], z["bk"], z["c"], z["d"], sv["o_ret"], sv["states"], dyb, rc)
    dzq = step(dzq)
    dz = {"bq": dzq, "bk": dzk, "c": dzv, "d": dzg, "e": dze}
    dbias_l = []
    for g, (_, dil) in enumerate(ATTN_GROUPS):
        dz[f"a{g}"], dbg = _attn_bwd(z[f"a{g}"], bias, dya_views[g], sv["ya_views"][g], sv["lt_views"][g], g, dil)
        dbias_l.append(dbg)
    last = f"a{N_GROUPS - 1}"
    dz[last] = step(dz[last])
    dx = dres
    for part, secs in enumerate(DX_PARTS):
        runs = tuple(run for _, sec_runs, _, _ in secs for run in sec_runs)
        dx = _dx_in_proj(dx, [dz[name] for name, _, _, _ in secs], _section_rows(w["w_in"], runs), secs,
                         name=f"dx_in_proj_{part}_{tag}")
    dx = step(dx)
    dw, db = {}, {}
    for name, _, _, dil in SECTIONS:
        dw[name], db[name] = _matmul(dz[name], sv["x_b"], "tn", colsum=True, a_phase=dil,
                                     name=f"dw_in_proj_{name}_{tag}")
    gw["w_in"] = _unsection(dw, axis=0)
    if reductions is not None:
        reductions.start(l, IN_GROUP, gw)
    small = dict(b_in=_unsection(db, axis=1), ln1_g=dg1, ln1_b=dbt1, ln2_g=dg2, ln2_b=dbt2)
    return dx, gw, small, jnp.concatenate(dbias_l, axis=0)


def _forward_backward(x, target, rel_bias, weights, lnps, reductions=None):
    s = x.shape[0]
    bmaps = _bucket_maps()
    bias = _bias_tiles(rel_bias, bmaps)
    rc = _ret_consts(s)
    prefetch = getattr(weights, "prefetch", lambda v: v)
    saved = []
    h, h_b = x, x.astype(MXU_DTYPE)
    for l in range(DEPTH):
        h, h_b, sv = _layer_fwd(h, h_b, weights, l, lnps[l], bias, rc, prefetch)
        saved.append(sv)
    dh, loss_part = _loss_head(h, target)
    gws, smalls, dbiases = [None] * DEPTH, [None] * DEPTH, [None] * DEPTH
    for l in reversed(range(DEPTH)):
        dh, gws[l], smalls[l], dbiases[l] = _layer_bwd(dh, weights.layer(l), l, lnps[l], saved[l], bias, rc,
                                                       reductions)
    d_rel_bias = _bias_tiles_bwd(dbiases, bmaps)
    return loss_part, dh, gws, smalls, d_rel_bias


SMALL_NAMES = ("rel_bias", "b_in", "ln1_g", "ln1_b", "ln2_g", "ln2_b")
SMALL_ROWS = 32


def _pack_small(vals):
    flat = jnp.concatenate([vals[n].reshape(-1) for n in SMALL_NAMES])
    return jnp.pad(flat, (0, SMALL_ROWS * D_MODEL - flat.shape[0])).reshape(SMALL_ROWS, D_MODEL)


def _unpack_small(packed, like):
    flat = packed.reshape(-1)
    out, off = {}, 0
    for n in SMALL_NAMES:
        size = like[n].size
        out[n] = flat[off:off + size].reshape(like[n].shape)
        off += size
    return out


def kernel(x, rel_bias, w_in, b_in, w_attn_proj, w_ret_proj, w_out, ln1_g, ln1_b, w_ffn_gate, w_ffn_up, w_ffn_down, ln2_g, ln2_b, loss_target, m_rel_bias, m_w_in, m_b_in, m_w_attn_proj, m_w_ret_proj, m_w_out, m_ln1_g, m_ln1_b, m_w_ffn_gate, m_w_ffn_up, m_w_ffn_down, m_ln2_g, m_ln2_b, v_rel_bias, v_w_in, v_b_in, v_w_attn_proj, v_w_ret_proj, v_w_out, v_ln1_g, v_ln1_b, v_w_ffn_gate, v_w_ffn_up, v_w_ffn_down, v_ln2_g, v_ln2_b):
    big = dict(w_in=(w_in, m_w_in, v_w_in), w_attn_proj=(w_attn_proj, m_w_attn_proj, v_w_attn_proj),
               w_ret_proj=(w_ret_proj, m_w_ret_proj, v_w_ret_proj), w_out=(w_out, m_w_out, v_w_out),
               w_ffn_gate=(w_ffn_gate, m_w_ffn_gate, v_w_ffn_gate), w_ffn_up=(w_ffn_up, m_w_ffn_up, v_w_ffn_up),
               w_ffn_down=(w_ffn_down, m_w_ffn_down, v_w_ffn_down))
    small_w = dict(rel_bias=rel_bias, b_in=b_in, ln1_g=ln1_g, ln1_b=ln1_b, ln2_g=ln2_g, ln2_b=ln2_b)
    small_m = dict(rel_bias=m_rel_bias, b_in=m_b_in, ln1_g=m_ln1_g, ln1_b=m_ln1_b, ln2_g=m_ln2_g, ln2_b=m_ln2_b)
    small_v = dict(rel_bias=v_rel_bias, b_in=v_b_in, ln1_g=v_ln1_g, ln1_b=v_ln1_b, ln2_g=v_ln2_g, ln2_b=v_ln2_b)

    place = jnp.stack([2 * lax.axis_index("x") + lax.axis_index("y"), lax.axis_index("c")]).astype(jnp.int32)

    weights = _WeightGathers({n: w for n, (w, _, _) in big.items()})
    reductions = _GradReductions(place)
    lnps = [dict(b_in=b_in[l][None], ln1_g=ln1_g[l][None], ln1_b=ln1_b[l][None], ln2_g=ln2_g[l][None],
                 ln2_b=ln2_b[l][None]) for l in range(DEPTH)]
    loss_part, dx, _, smalls, d_rel_bias = _forward_backward(x[0], loss_target[0], rel_bias, weights, lnps,
                                                             reductions)
    loss = lax.psum(loss_part, ("x", "y", "c"))

    small_g = dict(rel_bias=d_rel_bias)
    for n in SMALL_NAMES[1:]:
        small_g[n] = jnp.concatenate([smalls[l][n] for l in range(DEPTH)], axis=0)
    small_g = _unpack_small(_all_reduce_small(_pack_small(small_g)), small_w)
    grads, delta, new_m, new_v = dict(small_g), {}, {}, {}
    d_, m_, v_ = _adamw(_pack_small(small_w), _pack_small(small_g), _pack_small(small_m), _pack_small(small_v),
                        name="adamw_small")
    d_ = reductions.tick(d_)
    delta.update(_unpack_small(d_, small_w))
    new_m.update(_unpack_small(m_, small_w))
    new_v.update(_unpack_small(v_, small_w))

    def update(n, g):
        w, m, v = big[n]
        turn = (lambda t: jnp.swapaxes(t, 1, 2)) if n in COLUMN_SHARDED else (lambda t: t)
        two_d = lambda t: t.reshape(-1, t.shape[-1])
        d_, m_, v_ = _adamw(two_d(turn(w)), two_d(g), two_d(turn(m)), two_d(turn(v)), name=f"adamw_{n}")
        grads[n], delta[n], new_m[n], new_v[n] = (turn(t.reshape(g.shape)) for t in (g, d_, m_, v_))

    rest = _unpack_shard_grads([reductions.all[(l, REST_GROUP)].result() for l in range(DEPTH)], REST_GROUP)
    for i, (n, g) in enumerate(rest.items()):
        update(n, reductions.tick(g) if i == len(rest) // 2 else g)
    first = _unpack_shard_grads([reductions.all[(l, IN_GROUP)].result() for l in range(DEPTH)], IN_GROUP)
    for n, g in first.items():
        update(n, g)

    order = ("rel_bias", "w_in", "b_in", "w_attn_proj", "w_ret_proj", "w_out", "ln1_g", "ln1_b", "w_ffn_gate",
             "w_ffn_up", "w_ffn_down", "ln2_g", "ln2_b")
    return (loss, dx[None], *[grads[n] for n in order], *[delta[n] for n in order], *[new_m[n] for n in order],
            *[new_v[n] for n in order])
```

```python
import functools

import numpy as np
import jax
import jax.numpy as jnp
from jax import lax
from jax.experimental import pallas as pl
from jax.experimental.pallas import tpu as pltpu
from jax.experimental.pallas import tpu_sc as plsc

F32 = jnp.float32
MXU_DTYPE = jnp.bfloat16

DEPTH = 2
D_MODEL = 1024
HEAD_DIM = 64
ATTN_GROUPS = ((128, 1), (512, 4), (2048, 16))
N_GROUPS = len(ATTN_GROUPS)
HEADS_PER_GROUP = 6
N_ATTN_HEADS = 18
ATTN_WIDTH = 1152
ATTN_OUT = 384
NUM_BUCKETS = 32
MAX_DISTANCE = 2048
RET_HEADS = 4
RET_QK = 256
RET_V = 512
RET_CHUNK = 128
ROPE_BASE = 10000.0
D_FF = 2816
IN_COLS = 11648
ALPHA = (2 * DEPTH) ** 0.25
LN_EPS = 1e-5
GN_EPS = 1e-5
ADAM_LR, ADAM_B1, ADAM_B2, ADAM_EPS, ADAM_WD, ADAM_STEP = 0.001, 0.9, 0.999, 1e-08, 0.01, 10

BLK = 128
NEG = -1e30
N_CHIPS = 4
VMEM_LIMIT = 48 * 1024 * 1024

SECTIONS = tuple(
    (f"a{g}", tuple((k * ATTN_WIDTH + g * ATTN_OUT, ATTN_OUT) for k in range(3)), True, dil)
    for g, (_, dil) in enumerate(ATTN_GROUPS)
) + (
    ("bq", ((3456, 1024),), False, 1),
    ("bk", ((4480, 1024),), False, 1),
    ("c", ((5504, 2048),), True, 1),
    ("d", ((7552, 2048),), False, 1),
    ("e", ((9600, 2048),), False, 1),
)
DX_PARTS = (SECTIONS[:5], SECTIONS[5:])
GROUP_WIDTH = 3 * ATTN_OUT
LANES = 128


def _section_rows(w, runs):
    axis = 1 if w.shape[0] == 1 else 0
    parts = [lax.slice_in_dim(w, first, first + width, axis=axis) for first, width in runs]
    return parts[0] if len(parts) == 1 else jnp.concatenate(parts, axis=axis)


def _unsection(pieces, axis):
    runs = []
    for name, sec_runs, _, _ in SECTIONS:
        off = 0
        for first, width in sec_runs:
            runs.append((first, lax.slice_in_dim(pieces[name], off, off + width, axis=axis)))
            off += width
    return jnp.concatenate([p for _, p in sorted(runs, key=lambda t: t[0])], axis=axis)
PACK_GROUPS = ((("w_in", 2912),),
               (("w_attn_proj", 96), ("w_ret_proj", 512), ("w_out", 256), ("w_ffn_gate", 704), ("w_ffn_up", 704),
                ("w_ffn_down", 704)))
IN_GROUP, REST_GROUP = 0, 1
COLUMN_SHARDED = ("w_in", "w_attn_proj", "w_ffn_gate", "w_ffn_up")

NN = ((1,), (0,))
NT = ((1,), (1,))
TN = ((0,), (0,))


def _dot(a, b, dims):
    return lax.dot_general(a.astype(MXU_DTYPE), b.astype(MXU_DTYPE), (dims, ((), ())),
                           preferred_element_type=F32)


def _pick(n, prefs):
    for p in prefs:
        if n % p == 0:
            return p
    raise ValueError(f"no tile for {n} among {prefs}")


def _row_tile(n, most=512, unit=16):
    return max(t for t in range(unit, most + 1, unit) if n % t == 0)


TOKEN_TILES = (1024, 512, 256, 128)
FEATURE_TILES = (1152, 1024, 1408, 384, 256, 128)


def _params(n_axes, limit=VMEM_LIMIT):
    return pltpu.CompilerParams(dimension_semantics=("arbitrary",) * n_axes, vmem_limit_bytes=limit)


def _resident(a):
    return pl.BlockSpec(a.shape, lambda i: (0,) * a.ndim, pipeline_mode=pl.Buffered(1))


def _sigmoid(x):
    return 1.0 / (1.0 + jnp.exp(-x))


def _norm_rows(u, eps):
    mu = jnp.mean(u, axis=-1, keepdims=True)
    xc = u - mu
    var = jnp.mean(xc * xc, axis=-1, keepdims=True)
    rstd = lax.rsqrt(var + eps)
    return xc * rstd, rstd


def _norm_rows_bwd(dxh, xh, rstd):
    c1 = jnp.mean(dxh, axis=-1, keepdims=True)
    c2 = jnp.mean(dxh * xh, axis=-1, keepdims=True)
    return rstd * (dxh - c1 - xh * c2)


def _phase_to_tokens(src_ref, scr, dil, width):
    n = scr.shape[1] // dil
    for r in range(dil):
        for cb in range(width // LANES):
            col = r * width + cb * LANES
            scr.at[cb][pl.ds(r, n, stride=dil), :] = src_ref[:, col:col + LANES].astype(F32)


def _tokens_to_phase(scr, dst_ref, dil, width):
    n = scr.shape[1] // dil
    for r in range(dil):
        for cb in range(width // LANES):
            col = r * width + cb * LANES
            dst_ref[:, col:col + LANES] = scr.at[cb][pl.ds(r, n, stride=dil), :].astype(dst_ref.dtype)


def _blocks_to_cols(scr):
    return jnp.concatenate([scr[cb] for cb in range(scr.shape[0])], axis=1)


def _cols_to_blocks(val, scr):
    for cb in range(scr.shape[0]):
        scr[cb] = val[:, cb * LANES:(cb + 1) * LANES]


def _shard_placements(row_runs, tm, shard_rows, first_row):
    out, pos = {}, 0
    for first, count in row_runs:
        done = 0
        while done < count:
            tile, in_tile = divmod(pos, tm)
            shard, in_shard = divmod(first + done, shard_rows)
            n = min(count - done, tm - in_tile, shard_rows - in_shard)
            out.setdefault(tile, []).append((in_tile, n, shard, first_row + in_shard))
            done, pos = done + n, pos + n
    return out


def _dw_matmul(a, b, *, name, a_phase=1, colsum=False, into=None):
    d_a = a_phase
    kd, m = a.shape[0] * d_a, a.shape[1] // d_a
    n = b.shape[1]
    tm = m if d_a > 1 else _pick(m, FEATURE_TILES)
    tk = _pick(kd, TOKEN_TILES if d_a > 1 else (2048,) + TOKEN_TILES)
    nk, n_tiles = kd // tk, m // tm
    a_spec = (pl.BlockSpec((tk // d_a, d_a * m), lambda i, k: (k, 0)) if d_a > 1 else
              pl.BlockSpec((tk, tm), lambda i, k: (k, i)))
    b_spec = pl.BlockSpec((tk, n), lambda i, k: (k, 0))
    places = _shard_placements(into[1], tm, into[2], into[3]) if into is not None else None
    n_copies = max(len(p) for p in places.values()) if into is not None else 0

    def body(*refs):
        it = iter(refs)
        a_ref, b_ref = next(it), next(it)
        if into is not None:
            next(it)
        o_ref = next(it)
        cs_ref = next(it) if colsum else None
        acc_ref = next(it)
        a_scr = next(it) if d_a > 1 else None
        sems = next(it) if into is not None else None
        i, k = pl.program_id(0), pl.program_id(1)

        @pl.when(k == 0)
        def _():
            acc_ref[...] = jnp.zeros_like(acc_ref)
            if colsum:
                cs_ref[...] = jnp.zeros_like(cs_ref)

        if d_a > 1:
            _phase_to_tokens(a_ref, a_scr, d_a, m)
            av = _blocks_to_cols(a_scr)
        else:
            av = a_ref[...]
        acc_ref[...] += _dot(av, b_ref[...], TN)
        if colsum:
            cs_ref[...] += jnp.sum(av.astype(F32), axis=0, keepdims=True)

        @pl.when(k == nk - 1)
        def _():
            if into is None:
                o_ref[...] = acc_ref[...]
                return
            for tile in range(n_tiles):
                @pl.when(i == tile)
                def _():
                    copies = [pltpu.make_async_copy(acc_ref.at[pl.ds(src, rows), :],
                                                    o_ref.at[shard, pl.ds(dst, rows), :], sems.at[q])
                              for q, (src, rows, shard, dst) in enumerate(places[tile])]
                    for cp in copies:
                        cp.start()
                    for cp in copies:
                        cp.wait()

    in_specs, args, aliases = [a_spec, b_spec], [a, b], {}
    if into is not None:
        in_specs.append(ANY)
        args.append(into[0])
        aliases = {2: 0}
        out_shape, out_specs = [jax.ShapeDtypeStruct(into[0].shape, F32)], [ANY]
    else:
        out_shape, out_specs = [jax.ShapeDtypeStruct((m, n), F32)], [pl.BlockSpec((tm, n), lambda i, k: (i, 0))]
    if colsum:
        out_shape.append(jax.ShapeDtypeStruct((1, m), F32))
        out_specs.append(pl.BlockSpec((1, tm), lambda i, k: (0, i)))
    scratch = [pltpu.VMEM((tm, n), F32)]
    if d_a > 1:
        scratch.append(pltpu.VMEM((m // LANES, tk, LANES), F32))
    if into is not None:
        scratch.append(pltpu.SemaphoreType.DMA((n_copies,)))
    res = pl.pallas_call(
        body, out_shape=out_shape, grid=(n_tiles, nk), in_specs=in_specs, out_specs=out_specs,
        scratch_shapes=scratch, input_output_aliases=aliases, compiler_params=_params(2), name=name,
    )(*args)
    return res if colsum else res[0]


def _in_proj(x_b, w_rows, bias_cols, sections, name):
    s = x_b.shape[0]
    tm = 512
    widths = [sum(w for _, w in runs) for _, runs, _, _ in sections]
    dils = [dil for _, _, _, dil in sections]
    offs = [sum(widths[:i]) for i in range(len(widths))]
    widest_phased = max([w for w, d in zip(widths, dils) if d > 1], default=0)

    def body(x_ref, w_ref, b_ref, *rest):
        o_refs = rest[:len(sections)]
        scr = rest[len(sections)] if widest_phased else None
        xv = x_ref[...]
        for o_ref, width, dil, off in zip(o_refs, widths, dils, offs):
            r = _dot(xv, w_ref[off:off + width, :], NT) + b_ref[:, off:off + width]
            if dil > 1:
                _cols_to_blocks(r, scr)
                _tokens_to_phase(scr, o_ref, dil, width)
            else:
                o_ref[...] = r.astype(o_ref.dtype)

    out_shape = [jax.ShapeDtypeStruct((s // d, d * w), MXU_DTYPE if narrow else F32)
                 for w, d, (_, _, narrow, _) in zip(widths, dils, sections)]
    out_specs = [pl.BlockSpec((tm // d, d * w), lambda i: (i, 0)) for w, d in zip(widths, dils)]
    scratch = [pltpu.VMEM((widest_phased // LANES, tm, LANES), F32)] if widest_phased else []
    return pl.pallas_call(
        body, out_shape=out_shape, grid=(s // tm,),
        in_specs=[pl.BlockSpec((tm, D_MODEL), lambda i: (i, 0)), _resident(w_rows), _resident(bias_cols)],
        out_specs=out_specs, scratch_shapes=scratch, compiler_params=_params(1), name=name,
    )(x_b, w_rows, bias_cols)


def _dx_in_proj(addend, dzs, w_rows, sections, name):
    s = addend.shape[0]
    tm = 512
    widths = [sum(w for _, w in runs) for _, runs, _, _ in sections]
    dils = [dil for _, _, _, dil in sections]
    offs = [sum(widths[:i]) for i in range(len(widths))]
    widest_phased = max([w for w, d in zip(widths, dils) if d > 1], default=0)

    def body(*refs):
        add_ref, w_ref, o_ref = refs[0], refs[1], refs[2 + len(dzs)]
        dz_refs = refs[2:2 + len(dzs)]
        scr = refs[3 + len(dzs)] if widest_phased else None
        acc = add_ref[...]
        for dz_ref, width, dil, off in zip(dz_refs, widths, dils, offs):
            if dil > 1:
                _phase_to_tokens(dz_ref, scr, dil, width)
                av = _blocks_to_cols(scr)
            else:
                av = dz_ref[...]
            acc = acc + _dot(av, w_ref[off:off + width, :], NN)
        o_ref[...] = acc

    row = pl.BlockSpec((tm, D_MODEL), lambda i: (i, 0))
    dz_specs = [pl.BlockSpec((tm // d, d * w), lambda i: (i, 0)) for w, d in zip(widths, dils)]
    scratch = [pltpu.VMEM((widest_phased // LANES, tm, LANES), F32)] if widest_phased else []
    return pl.pallas_call(
        body, out_shape=jax.ShapeDtypeStruct((s, D_MODEL), F32), grid=(s // tm,),
        in_specs=[row, _resident(w_rows)] + dz_specs, out_specs=row, scratch_shapes=scratch,
        compiler_params=_params(1), name=name,
    )(addend, w_rows, *dzs)


def _t5_bucket(dist):
    max_exact = NUM_BUCKETS // 2
    large = max_exact + (np.log(np.maximum(dist, max_exact) / max_exact)
                         / np.log(MAX_DISTANCE / max_exact) * (NUM_BUCKETS - max_exact)).astype(np.int32)
    large = np.minimum(large, NUM_BUCKETS - 1)
    return np.where(dist < max_exact, dist, large).astype(np.int32)


def _bucket_maps():
    qi = np.arange(BLK)[:, None]
    kj = np.arange(2 * BLK)[None, :]
    rel = np.clip(qi + BLK - kj, 0, BLK)
    return jnp.asarray(np.stack([_t5_bucket(rel * d) for _, d in ATTN_GROUPS]))


def _bias_tiles(rel_bias, bmaps):
    def body(tab_ref, bm_ref, o_ref):
        h = pl.program_id(0)
        bm = bm_ref[0]
        acc = jnp.zeros((BLK, 2 * BLK), F32)
        for b in range(NUM_BUCKETS):
            acc = jnp.where(bm == b, tab_ref[b, h], acc)
        o_ref[0] = acc

    return pl.pallas_call(
        body, out_shape=jax.ShapeDtypeStruct((N_ATTN_HEADS, BLK, 2 * BLK), F32), grid=(N_ATTN_HEADS,),
        in_specs=[pl.BlockSpec(memory_space=pltpu.SMEM),
                  pl.BlockSpec((1, BLK, 2 * BLK), lambda h: (h // HEADS_PER_GROUP, 0, 0))],
        out_specs=pl.BlockSpec((1, BLK, 2 * BLK), lambda h: (h, 0, 0)),
        compiler_params=_params(1), name="bias_tiles",
    )(rel_bias, bmaps)


def _bias_tiles_bwd(dbias_layers, bmaps):
    nl = len(dbias_layers)

    def body(*refs):
        bm = refs[nl][0]
        o_ref = refs[nl + 1]
        x = refs[0][0]
        for r in refs[1:nl]:
            x = x + r[0]
        lane = lax.broadcasted_iota(jnp.int32, (1, BLK), 1)
        row = jnp.zeros((1, BLK), F32)
        for b in range(NUM_BUCKETS):
            s = jnp.sum(jnp.where(bm == b, x, 0.0), axis=1, keepdims=True)
            s = jnp.sum(s, axis=0, keepdims=True)
            row = jnp.where(lane == b, s, row)
        o_ref[0] = row

    tile = pl.BlockSpec((1, BLK, 2 * BLK), lambda h: (h, 0, 0))
    out = pl.pallas_call(
        body, out_shape=jax.ShapeDtypeStruct((N_ATTN_HEADS, 1, BLK), F32), grid=(N_ATTN_HEADS,),
        in_specs=[tile] * nl + [pl.BlockSpec((1, BLK, 2 * BLK), lambda h: (h // HEADS_PER_GROUP, 0, 0))],
        out_specs=pl.BlockSpec((1, 1, BLK), lambda h: (h, 0, 0)),
        compiler_params=_params(1), name="bias_tiles_bwd",
    )(*dbias_layers, bmaps)
    return out[:, 0, :NUM_BUCKETS].T


def _pair_masks():
    lane = lax.broadcasted_iota(jnp.int32, (BLK, BLK), 1)
    row2 = lax.broadcasted_iota(jnp.int32, (2 * BLK, BLK), 0)
    lane2 = lax.broadcasted_iota(jnp.int32, (2 * BLK, BLK), 1)
    own = (lane2 // HEAD_DIM) == (row2 // BLK)
    qi = lax.broadcasted_iota(jnp.int32, (2 * BLK, 2 * BLK), 0) & (BLK - 1)
    kj = lax.broadcasted_iota(jnp.int32, (2 * BLK, 2 * BLK), 1)
    band = jnp.logical_and(kj >= qi, kj <= qi + BLK)
    return lane < HEAD_DIM, own, band, kj < BLK


def _pair_scores(q32, kb, bias2, own, band, is_prev, pen):
    qm = jnp.where(own, jnp.concatenate([q32, q32], axis=0), 0.0)
    s = _dot(qm, kb, NT) * (HEAD_DIM ** -0.5) + bias2
    if pen is not None:
        s = s + jnp.where(is_prev, pen, 0.0)
    return jnp.where(band, s, NEG), qm


HEAD_PAIRS = HEADS_PER_GROUP // 2


def _attn_fwd(qkv, bias, g, dil):
    rows = qkv.shape[0]
    nb = rows // BLK
    rb = 2 if nb % 2 == 0 else 1
    cur = pl.BlockSpec((rb * BLK, GROUP_WIDTH), lambda r, n: (n, r))
    prev = pl.BlockSpec((BLK, GROUP_WIDTH), lambda r, n: (jnp.maximum(rb * n - 1, 0), r))

    def body(c_ref, p_ref, b_ref, o_ref, l_ref):
        n = pl.program_id(1)
        pen0 = jnp.where(n > 0, 0.0, NEG)
        first, own, band, is_prev = _pair_masks()
        o_rows, l_rows = [], []
        for t in range(rb):
            rows_t = slice(t * BLK, (t + 1) * BLK)
            o_parts, l_parts = [], []
            for hp in range(HEAD_PAIRS):
                qc, kc, vc = (slice(k * ATTN_OUT + hp * BLK, k * ATTN_OUT + (hp + 1) * BLK) for k in range(3))
                if t == 0:
                    kp, vp, pen = p_ref[:, kc], p_ref[:, vc], pen0
                else:
                    before = slice((t - 1) * BLK, t * BLK)
                    kp, vp, pen = c_ref[before, kc], c_ref[before, vc], None
                kb = jnp.concatenate([kp, c_ref[rows_t, kc]], axis=0)
                vb = jnp.concatenate([vp, c_ref[rows_t, vc]], axis=0)
                bias2 = jnp.concatenate([b_ref[2 * hp], b_ref[2 * hp + 1]], axis=0)
                s, _ = _pair_scores(c_ref[rows_t, qc].astype(F32), kb, bias2, own, band, is_prev, pen)
                m = jnp.max(s, axis=1, keepdims=True)
                p = jnp.exp(s - m)
                l = jnp.sum(p, axis=1, keepdims=True)
                o2 = _dot(p * (1.0 / l), vb, NN)
                lse2 = m + jnp.log(l)
                o_parts.append(jnp.where(first, o2[:BLK], o2[BLK:]))
                l_parts.append(jnp.where(first, lse2[:BLK], lse2[BLK:]))
            o_rows.append(jnp.concatenate(o_parts, axis=1))
            l_rows.append(jnp.concatenate(l_parts, axis=1))
        o_ref[...] = jnp.concatenate(o_rows, axis=0)
        l_ref[...] = jnp.concatenate(l_rows, axis=0)

    out_spec = pl.BlockSpec((rb * BLK, ATTN_OUT), lambda r, n: (n, r))
    shape = jax.ShapeDtypeStruct((rows, dil * ATTN_OUT), F32)
    return pl.pallas_call(
        body, out_shape=[shape, shape], grid=(dil, nb // rb),
        in_specs=[cur, prev, pl.BlockSpec((HEADS_PER_GROUP, BLK, 2 * BLK), lambda r, n: (g, 0, 0))],
        out_specs=[out_spec, out_spec], compiler_params=_params(2), name=f"attn_fwd_g{g}",
    )(qkv, qkv, bias)


def _attn_bwd(qkv, bias, dy, ya, lt, g, dil):
    rows = qkv.shape[0]
    nb = rows // BLK
    rb = 2 if nb % 2 == 0 else 1
    ns = nb // rb
    cur = lambda w: pl.BlockSpec((rb * BLK, w), lambda r, n: (jnp.minimum(n, ns - 1), r))
    prev = pl.BlockSpec((BLK, GROUP_WIDTH), lambda r, n: (jnp.clip(rb * n - 1, 0, nb - 1), r))

    def body(c_ref, p_ref, b_ref, dy_ref, ya_ref, lt_ref, dz_ref, db_ref, cq_ref, ckv_full, ckv_part):
        r, n = pl.program_id(0), pl.program_id(1)

        @pl.when(jnp.logical_and(r == 0, n == 0))
        def _():
            db_ref[...] = jnp.zeros_like(db_ref)

        @pl.when(n == 0)
        def _():
            cq_ref[...] = jnp.zeros_like(cq_ref)
            ckv_full[...] = jnp.zeros_like(ckv_full)
            ckv_part[...] = jnp.zeros_like(ckv_part)

        def late(dk_add, dv_add):
            kv = ckv_part[...]
            k_last, v_last = kv[:, :ATTN_OUT] + dk_add, kv[:, ATTN_OUT:] + dv_add
            if rb == 1:
                return jnp.concatenate([cq_ref[...], k_last, v_last], axis=1)
            full = ckv_full[...]
            return jnp.concatenate([cq_ref[...],
                                    jnp.concatenate([full[:, :ATTN_OUT], k_last], axis=0),
                                    jnp.concatenate([full[:, ATTN_OUT:], v_last], axis=0)], axis=1)

        @pl.when(n < ns)
        def _():
            pen0 = jnp.where(n > 0, 0.0, NEG)
            first, own, band, is_prev = _pair_masks()
            second = jnp.logical_not(first)
            scale = HEAD_DIM ** -0.5
            db_sum, wides = None, []
            for t in range(rb):
                rows_t = slice(t * BLK, (t + 1) * BLK)
                parts = {k: [] for k in ("dq", "dkp", "dkc", "dvp", "dvc")}
                db_parts = []
                for hp in range(HEAD_PAIRS):
                    cols = slice(hp * BLK, (hp + 1) * BLK)
                    qc, kc, vc = (slice(k * ATTN_OUT + hp * BLK, k * ATTN_OUT + (hp + 1) * BLK) for k in range(3))
                    if t == 0:
                        kp, vp, pen = p_ref[:, kc], p_ref[:, vc], pen0
                    else:
                        before = slice((t - 1) * BLK, t * BLK)
                        kp, vp, pen = c_ref[before, kc], c_ref[before, vc], None
                    kb = jnp.concatenate([kp, c_ref[rows_t, kc]], axis=0)
                    vb = jnp.concatenate([vp, c_ref[rows_t, vc]], axis=0)
                    bias2 = jnp.concatenate([b_ref[2 * hp], b_ref[2 * hp + 1]], axis=0)
                    dy_, lt_ = dy_ref[rows_t, cols], lt_ref[rows_t, cols]
                    dyy = dy_ * ya_ref[rows_t, cols]
                    per_head = lambda v, red, fill: jnp.concatenate(
                        [red(jnp.where(first, v, fill), axis=1, keepdims=True),
                         red(jnp.where(second, v, fill), axis=1, keepdims=True)], axis=0)
                    lse2 = per_head(lt_, jnp.max, NEG)
                    delta2 = per_head(dyy, jnp.sum, 0.0)
                    s, qm = _pair_scores(c_ref[rows_t, qc].astype(F32), kb, bias2, own, band, is_prev, pen)
                    p = jnp.exp(s - lse2)
                    dym = jnp.where(own, jnp.concatenate([dy_, dy_], axis=0), 0.0)
                    ds = p * (_dot(dym, vb, NT) - delta2)
                    db_parts += [ds[:BLK], ds[BLK:]]
                    dq2 = _dot(ds, kb, NN) * scale
                    dkb = _dot(ds, qm, TN) * scale
                    dvb = _dot(p, dym, TN)
                    for k, val in (("dq", jnp.where(first, dq2[:BLK], dq2[BLK:])), ("dkp", dkb[:BLK]),
                                   ("dkc", dkb[BLK:]), ("dvp", dvb[:BLK]), ("dvc", dvb[BLK:])):
                        parts[k].append(val)
                wides.append({k: jnp.concatenate(val, axis=1) for k, val in parts.items()})
                db_t = jnp.stack(db_parts, axis=0)
                db_sum = db_t if db_sum is None else db_sum + db_t
            db_ref[...] += db_sum
            dz_ref[...] = late(wides[0]["dkp"], wides[0]["dvp"]).astype(dz_ref.dtype)
            cq_ref[...] = jnp.concatenate([w["dq"] for w in wides], axis=0)
            if rb == 2:
                ckv_full[...] = jnp.concatenate([wides[0]["dkc"] + wides[1]["dkp"],
                                                 wides[0]["dvc"] + wides[1]["dvp"]], axis=1)
            ckv_part[...] = jnp.concatenate([wides[-1]["dkc"], wides[-1]["dvc"]], axis=1)

        @pl.when(n == ns)
        def _():
            zero = jnp.zeros((BLK, ATTN_OUT), F32)
            dz_ref[...] = late(zero, zero).astype(dz_ref.dtype)

    out_late = pl.BlockSpec((rb * BLK, GROUP_WIDTH), lambda r, n: (jnp.maximum(n - 1, 0), r))
    kv_carry = pltpu.VMEM((BLK, 2 * ATTN_OUT), F32)
    return pl.pallas_call(
        body,
        out_shape=[jax.ShapeDtypeStruct((rows, dil * GROUP_WIDTH), MXU_DTYPE),
                   jax.ShapeDtypeStruct((HEADS_PER_GROUP, BLK, 2 * BLK), F32)],
        grid=(dil, ns + 1),
        in_specs=[cur(GROUP_WIDTH), prev, pl.BlockSpec((HEADS_PER_GROUP, BLK, 2 * BLK), lambda r, n: (g, 0, 0)),
                  cur(ATTN_OUT), cur(ATTN_OUT), cur(ATTN_OUT)],
        out_specs=[out_late, pl.BlockSpec((HEADS_PER_GROUP, BLK, 2 * BLK), lambda r, n: (0, 0, 0))],
        scratch_shapes=[pltpu.VMEM((rb * BLK, ATTN_OUT), F32), kv_carry, kv_carry],
        compiler_params=_params(2), name=f"attn_bwd_g{g}",
    )(qkv, qkv, bias, dy, ya, lt)


def _attn_combine(o_list, l_list):
    dils = [d for _, d in ATTN_GROUPS]
    s = o_list[0].shape[0] * dils[0]
    tr = _pick(s, (512, 256, 128))
    n_g = len(dils)

    def body(*refs):
        o_refs, l_refs = refs[:n_g], refs[n_g:2 * n_g]
        yb_ref = refs[2 * n_g]
        y_refs, lt_refs = refs[2 * n_g + 1:3 * n_g + 1], refs[3 * n_g + 1:4 * n_g + 1]
        scr_o, scr_l = refs[4 * n_g + 1], refs[4 * n_g + 2]
        os_, ls_ = [], []
        for g in range(n_g):
            if dils[g] == 1:
                os_.append(o_refs[g][...])
                ls_.append(l_refs[g][...])
            else:
                _phase_to_tokens(o_refs[g], scr_o, dils[g], ATTN_OUT)
                _phase_to_tokens(l_refs[g], scr_l, dils[g], ATTN_OUT)
                os_.append(_blocks_to_cols(scr_o))
                ls_.append(_blocks_to_cols(scr_l))
        mx = functools.reduce(jnp.maximum, ls_)
        es = [jnp.exp(a - mx) for a in ls_]
        den = functools.reduce(lambda a, b: a + b, es)
        inv = 1.0 / den
        y = functools.reduce(lambda a, b: a + b, [(e * inv) * o for e, o in zip(es, os_)])
        lt = mx + jnp.log(den)
        yb_ref[...] = y.astype(yb_ref.dtype)
        _cols_to_blocks(y, scr_o)
        _cols_to_blocks(lt, scr_l)
        for g in range(n_g):
            if dils[g] == 1:
                y_refs[g][...] = y
                lt_refs[g][...] = lt
            else:
                _tokens_to_phase(scr_o, y_refs[g], dils[g], ATTN_OUT)
                _tokens_to_phase(scr_l, lt_refs[g], dils[g], ATTN_OUT)

    view = lambda d: pl.BlockSpec((tr // d, d * ATTN_OUT), lambda i: (i, 0))
    views = [view(d) for d in dils]
    f = lambda d: jax.ShapeDtypeStruct((s // d, d * ATTN_OUT), F32)
    fs = [f(d) for d in dils]
    scratch = pltpu.VMEM((ATTN_OUT // LANES, tr, LANES), F32)
    res = pl.pallas_call(
        body, out_shape=[jax.ShapeDtypeStruct((s, ATTN_OUT), MXU_DTYPE)] + fs + fs, grid=(s // tr,),
        in_specs=views + views, out_specs=[view(1)] + views + views, scratch_shapes=[scratch, scratch],
        compiler_params=_params(1), name="attn_combine",
    )(*o_list, *l_list)
    return res[0], list(res[1:1 + n_g]), list(res[1 + n_g:])


def _ret_consts(s):
    half = RET_QK // 2
    pos = jnp.arange(s, dtype=F32)
    inv_freq = ROPE_BASE ** (-jnp.arange(half, dtype=F32) / half)
    ang = pos[:, None] * inv_freq[None]
    log_g = jnp.log(1.0 - 2.0 ** (-5.0 - jnp.arange(RET_HEADS, dtype=F32)))
    n = jnp.arange(RET_CHUNK, dtype=F32)
    diff = n[:, None] - n[None, :]
    dmask = jnp.where(diff >= 0, jnp.exp(log_g[:, None, None] * jnp.maximum(diff, 0.0)), 0.0)
    qdec = jnp.exp(log_g[:, None] * (n + 1.0))
    kdec = jnp.exp(log_g[:, None] * (RET_CHUNK - 1.0 - n))
    cdec = jnp.exp(log_g * RET_CHUNK)
    wide = (RET_HEADS, RET_CHUNK, RET_QK)
    return dict(cos=jnp.cos(ang), sin=jnp.sin(ang), dmask=dmask,
                qdec=jnp.broadcast_to(qdec[:, :, None], wide), kdec=jnp.broadcast_to(kdec[:, :, None], wide),
                cdec=cdec)


def _rot(t, cs, sn):
    half = RET_QK // 2
    t1, t2 = t[:, :half], t[:, half:]
    return jnp.concatenate([t1 * cs - t2 * sn, t1 * sn + t2 * cs], axis=1)


def _rot_bwd(d, cs, sn):
    half = RET_QK // 2
    d1, d2 = d[:, :half], d[:, half:]
    return jnp.concatenate([d1 * cs + d2 * sn, d2 * cs - d1 * sn], axis=1)


def _ret_fwd(zq, zk, zv, zg, rc):
    s = zq.shape[0]
    nc = s // RET_CHUNK
    c = RET_CHUNK
    cps = 2 if nc % 2 == 0 else 1

    def body(cd_ref, q_ref, k_ref, v_ref, g_ref, cos_ref, sin_ref, dm_ref, qd_ref, kd_ref,
             yb_ref, o_ref, st_ref, state):
        n = pl.program_id(0)

        @pl.when(n == 0)
        def _():
            state[...] = jnp.zeros_like(state)

        for h in range(RET_HEADS):
            qs, vs = slice(h * RET_QK, (h + 1) * RET_QK), slice(h * RET_V, (h + 1) * RET_V)
            st = state[h]
            for t in range(cps):
                rs = slice(t * c, (t + 1) * c)
                cs, sn = cos_ref[rs, :], sin_ref[rs, :]
                qr = _rot(q_ref[rs, qs], cs, sn)
                kr = _rot(k_ref[rs, qs], cs, sn) * (RET_QK ** -0.5)
                v = v_ref[rs, vs]
                st_ref[h, t] = st.astype(st_ref.dtype)
                sc = _dot(qr, kr, NT) * dm_ref[h]
                o = _dot(sc, v, NN) + _dot(qr * qd_ref[h], st, NN)
                st = st * cd_ref[h] + _dot(kr * kd_ref[h], v, TN)
                o_ref[rs, vs] = o.astype(o_ref.dtype)
                xh, _ = _norm_rows(o, GN_EPS)
                gv = g_ref[rs, vs]
                yb_ref[rs, vs] = (gv * _sigmoid(gv) * xh).astype(yb_ref.dtype)
            state[h] = st

    row = lambda w: pl.BlockSpec((cps * c, w), lambda n: (n, 0))
    const = lambda a: pl.BlockSpec(a.shape, lambda n: (0, 0, 0))
    return pl.pallas_call(
        body,
        out_shape=[jax.ShapeDtypeStruct((s, RET_HEADS * RET_V), MXU_DTYPE),
                   jax.ShapeDtypeStruct((s, RET_HEADS * RET_V), MXU_DTYPE),
                   jax.ShapeDtypeStruct((RET_HEADS, nc, RET_QK, RET_V), MXU_DTYPE)],
        grid=(nc // cps,),
        in_specs=[pl.BlockSpec(memory_space=pltpu.SMEM), row(RET_HEADS * RET_QK), row(RET_HEADS * RET_QK),
                  row(RET_HEADS * RET_V), row(RET_HEADS * RET_V), row(RET_QK // 2), row(RET_QK // 2),
                  const(rc["dmask"]), const(rc["qdec"]), const(rc["kdec"])],
        out_specs=[row(RET_HEADS * RET_V), row(RET_HEADS * RET_V),
                   pl.BlockSpec((RET_HEADS, cps, RET_QK, RET_V), lambda n: (0, n, 0, 0))],
        scratch_shapes=[pltpu.VMEM((RET_HEADS, RET_QK, RET_V), F32)],
        compiler_params=_params(1), name="ret_fwd",
    )(rc["cdec"], zq, zk, zv, zg, rc["cos"], rc["sin"], rc["dmask"], rc["qdec"], rc["kdec"])


def _ret_bwd(zq, zk, zv, zg, o_ret, states, dyb, rc):
    s = zq.shape[0]
    nc = s // RET_CHUNK
    c = RET_CHUNK
    cps = 2 if nc % 2 == 0 else 1
    last = nc // cps - 1

    def body(cd_ref, q_ref, k_ref, v_ref, g_ref, o_ref, st_ref, dy_ref, cos_ref, sin_ref, dm_ref, qd_ref,
             kd_ref, dq_ref, dk_ref, dv_ref, dg_ref, dstate):
        n = pl.program_id(0)

        @pl.when(n == 0)
        def _():
            dstate[...] = jnp.zeros_like(dstate)

        for h in range(RET_HEADS):
            qs, vs = slice(h * RET_QK, (h + 1) * RET_QK), slice(h * RET_V, (h + 1) * RET_V)
            dm, qd, kd = dm_ref[h], qd_ref[h], kd_ref[h]
            ds_next = dstate[h]
            for t in reversed(range(cps)):
                rs = slice(t * c, (t + 1) * c)
                cs, sn = cos_ref[rs, :], sin_ref[rs, :]
                qr = _rot(q_ref[rs, qs], cs, sn)
                kr = _rot(k_ref[rs, qs], cs, sn) * (RET_QK ** -0.5)
                v = v_ref[rs, vs]
                st = st_ref[h, t]
                xh, rstd = _norm_rows(o_ref[rs, vs].astype(F32), GN_EPS)
                gv, dy = g_ref[rs, vs], dy_ref[rs, vs].astype(F32)
                sg = _sigmoid(gv)
                dg_ref[rs, vs] = (dy * xh * (sg * (1.0 + gv * (1.0 - sg)))).astype(dg_ref.dtype)
                do = _norm_rows_bwd(dy * (gv * sg), xh, rstd)
                sc = _dot(qr, kr, NT) * dm
                da = _dot(do, v, NT) * dm
                dv = _dot(sc, do, TN) + _dot(kr * kd, ds_next, NN)
                dqr = _dot(da, kr, NN) + _dot(do, st, NT) * qd
                dkr = _dot(da, qr, TN) + _dot(v, ds_next, NT) * kd
                ds_next = ds_next * cd_ref[h] + _dot(qr * qd, do, TN)
                dq_ref[rs, qs] = _rot_bwd(dqr, cs, sn).astype(dq_ref.dtype)
                dk_ref[rs, qs] = _rot_bwd(dkr * (RET_QK ** -0.5), cs, sn).astype(dk_ref.dtype)
                dv_ref[rs, vs] = dv.astype(dv_ref.dtype)
            dstate[h] = ds_next

    row = lambda w: pl.BlockSpec((cps * c, w), lambda n: (last - n, 0))
    const = lambda a: pl.BlockSpec(a.shape, lambda n: (0, 0, 0))
    qk_w, v_w = RET_HEADS * RET_QK, RET_HEADS * RET_V
    g_qk, g_v = jax.ShapeDtypeStruct((s, qk_w), MXU_DTYPE), jax.ShapeDtypeStruct((s, v_w), MXU_DTYPE)
    return pl.pallas_call(
        body,
        out_shape=[g_qk, g_qk, g_v, g_v],
        grid=(nc // cps,),
        in_specs=[pl.BlockSpec(memory_space=pltpu.SMEM), row(qk_w), row(qk_w), row(v_w), row(v_w), row(v_w),
                  pl.BlockSpec((RET_HEADS, cps, RET_QK, RET_V), lambda n: (0, last - n, 0, 0)), row(v_w),
                  row(RET_QK // 2), row(RET_QK // 2), const(rc["dmask"]), const(rc["qdec"]), const(rc["kdec"])],
        out_specs=[row(qk_w), row(qk_w), row(v_w), row(v_w)],
        scratch_shapes=[pltpu.VMEM((RET_HEADS, RET_QK, RET_V), F32)],
        compiler_params=_params(1), name="ret_bwd",
    )(rc["cdec"], zq, zk, zv, zg, o_ret, states, dyb, rc["cos"], rc["sin"], rc["dmask"], rc["qdec"],
      rc["kdec"])


ROW_TILE = 256


def _mix_fwd(x, ya_b, yb_b, ze, wap_t, wrp, wout, gam, bet):
    s = x.shape[0]
    tm = ROW_TILE

    def body(x_ref, ya_ref, yb_ref, ze_ref, wap_ref, wrp_ref, wout_ref, g_ref, b_ref,
             x1_ref, u1_ref, a1_ref, a2_ref, mg_ref, x1b_ref):
        a1 = _dot(ya_ref[...], wap_ref[...], NT)
        a2 = _dot(yb_ref[...], wrp_ref[...], NN)
        ze_ = ze_ref[...]
        merged = _sigmoid(ze_[:, :D_MODEL]) * a1 + _sigmoid(ze_[:, D_MODEL:]) * a2
        u = ALPHA * x_ref[...] + _dot(merged, wout_ref[...], NN)
        xh, _ = _norm_rows(u, LN_EPS)
        x1 = xh * g_ref[...] + b_ref[...]
        x1_ref[...] = x1
        x1b_ref[...] = x1.astype(x1b_ref.dtype)
        u1_ref[...] = u
        a1_ref[...] = a1.astype(a1_ref.dtype)
        a2_ref[...] = a2.astype(a2_ref.dtype)
        mg_ref[...] = merged.astype(mg_ref.dtype)

    row = lambda w: pl.BlockSpec((tm, w), lambda i: (i, 0))
    full = _resident
    f = jax.ShapeDtypeStruct((s, D_MODEL), F32)
    m = jax.ShapeDtypeStruct((s, D_MODEL), MXU_DTYPE)
    return pl.pallas_call(
        body, out_shape=[f, f, m, m, m, m], grid=(s // tm,),
        in_specs=[row(D_MODEL), row(ATTN_OUT), row(RET_HEADS * RET_V), row(2 * D_MODEL), full(wap_t), full(wrp),
                  full(wout), full(gam), full(bet)],
        out_specs=[row(D_MODEL)] * 6, compiler_params=_params(1), name="mix_fwd",
    )(x, ya_b, yb_b, ze, wap_t, wrp, wout, gam, bet)


def _mix_bwd(dx1, u1, a1, a2, ze, wap_t, wrp, wout, gam):
    s = dx1.shape[0]
    tm = ROW_TILE

    dils = [d for _, d in ATTN_GROUPS]

    def body(dx_ref, u_ref, a1_ref, a2_ref, ze_ref, wap_ref, wrp_ref, wout_ref, g_ref,
             dres_ref, du_ref, da1_ref, da2_ref, dze_ref, dyb_ref, dgam_ref, dbet_ref, *rest):
        dya_refs, dya_scr = rest[:len(dils)], rest[len(dils)]
        @pl.when(pl.program_id(0) == 0)
        def _():
            dgam_ref[...] = jnp.zeros_like(dgam_ref)
            dbet_ref[...] = jnp.zeros_like(dbet_ref)

        dx = dx_ref[...]
        xh, rstd = _norm_rows(u_ref[...], LN_EPS)
        dgam_ref[...] += jnp.sum(dx * xh, axis=0, keepdims=True)
        dbet_ref[...] += jnp.sum(dx, axis=0, keepdims=True)
        du = _norm_rows_bwd(dx * g_ref[...], xh, rstd)
        dres_ref[...] = ALPHA * du
        du_ref[...] = du.astype(du_ref.dtype)
        dm = _dot(du, wout_ref[...], NT)
        ze_ = ze_ref[...]
        sa, sb = _sigmoid(ze_[:, :D_MODEL]), _sigmoid(ze_[:, D_MODEL:])
        da1, da2 = dm * sa, dm * sb
        dze_ref[...] = jnp.concatenate([dm * a1_ref[...] * (sa * (1.0 - sa)),
                                        dm * a2_ref[...] * (sb * (1.0 - sb))], axis=1).astype(dze_ref.dtype)
        da1_ref[...] = da1.astype(da1_ref.dtype)
        da2_ref[...] = da2.astype(da2_ref.dtype)
        dyb_ref[...] = _dot(da2, wrp_ref[...], NT).astype(dyb_ref.dtype)
        dya = _dot(da1, wap_ref[...], NN)
        _cols_to_blocks(dya, dya_scr)
        for g, d in enumerate(dils):
            if d == 1:
                dya_refs[g][...] = dya
            else:
                _tokens_to_phase(dya_scr, dya_refs[g], d, ATTN_OUT)

    row = lambda w: pl.BlockSpec((tm, w), lambda i: (i, 0))
    full = _resident
    vec = pl.BlockSpec((1, D_MODEL), lambda i: (0, 0))
    f = lambda w: jax.ShapeDtypeStruct((s, w), F32)
    m = lambda w: jax.ShapeDtypeStruct((s, w), MXU_DTYPE)
    v = jax.ShapeDtypeStruct((1, D_MODEL), F32)
    res = pl.pallas_call(
        body,
        out_shape=[f(D_MODEL), m(D_MODEL), m(D_MODEL), m(D_MODEL), m(2 * D_MODEL), m(RET_HEADS * RET_V), v, v]
        + [jax.ShapeDtypeStruct((s // d, d * ATTN_OUT), F32) for d in dils],
        grid=(s // tm,),
        in_specs=[row(D_MODEL)] * 4 + [row(2 * D_MODEL), full(wap_t), full(wrp), full(wout), full(gam)],
        out_specs=[row(D_MODEL)] * 4 + [row(2 * D_MODEL), row(RET_HEADS * RET_V), vec, vec]
        + [pl.BlockSpec((tm // d, d * ATTN_OUT), lambda i: (i, 0)) for d in dils],
        scratch_shapes=[pltpu.VMEM((ATTN_OUT // LANES, tm, LANES), F32)],
        compiler_params=_params(1), name="mix_bwd",
    )(dx1, u1, a1, a2, ze, wap_t, wrp, wout, gam)
    return (*res[:8], list(res[8:]))


FF_CHUNK = 1408


def _ffn_fwd(x1, wg_t, wu_t, wd, gam, bet):
    s = x1.shape[0]
    tm, fc = ROW_TILE, FF_CHUNK

    def body(x_ref, wg_ref, wu_ref, wd_ref, g_ref, b_ref, x2_ref, u2_ref, a_ref, b_out_ref, h_ref, x2b_ref):
        xv = x_ref[...]
        u = ALPHA * xv
        for f0 in range(0, D_FF, fc):
            ch = slice(f0, f0 + fc)
            a = _dot(xv, wg_ref[ch, :], NT)
            b = _dot(xv, wu_ref[ch, :], NT)
            hid = a * _sigmoid(a) * b
            u = u + _dot(hid, wd_ref[ch, :], NN)
            a_ref[:, ch] = a.astype(a_ref.dtype)
            b_out_ref[:, ch] = b.astype(b_out_ref.dtype)
            h_ref[:, ch] = hid.astype(h_ref.dtype)
        xh, _ = _norm_rows(u, LN_EPS)
        u2_ref[...] = u
        x2 = xh * g_ref[...] + b_ref[...]
        x2_ref[...] = x2
        x2b_ref[...] = x2.astype(x2b_ref.dtype)

    row = lambda w: pl.BlockSpec((tm, w), lambda i: (i, 0))
    f = lambda w: jax.ShapeDtypeStruct((s, w), F32)
    m = lambda w: jax.ShapeDtypeStruct((s, w), MXU_DTYPE)
    return pl.pallas_call(
        body, out_shape=[f(D_MODEL), f(D_MODEL), m(D_FF), m(D_FF), m(D_FF), m(D_MODEL)],
        grid=(s // tm,),
        in_specs=[row(D_MODEL), _resident(wg_t), _resident(wu_t), _resident(wd), _resident(gam), _resident(bet)],
        out_specs=[row(D_MODEL), row(D_MODEL), row(D_FF), row(D_FF), row(D_FF), row(D_MODEL)],
        compiler_params=_params(1), name="ffn_fwd",
    )(x1, wg_t, wu_t, wd, gam, bet)


def _ffn_bwd(dx2, u2, fa, fb, wg_t, wu_t, wd, gam):
    s = dx2.shape[0]
    tm, fc = ROW_TILE, FF_CHUNK

    def body(dx_ref, u_ref, a_ref, b_ref, wg_ref, wu_ref, wd_ref, g_ref,
             dx1_ref, du_ref, da_ref, db_ref, dgam_ref, dbet_ref):
        @pl.when(pl.program_id(0) == 0)
        def _():
            dgam_ref[...] = jnp.zeros_like(dgam_ref)
            dbet_ref[...] = jnp.zeros_like(dbet_ref)

        dx = dx_ref[...]
        xh, rstd = _norm_rows(u_ref[...], LN_EPS)
        dgam_ref[...] += jnp.sum(dx * xh, axis=0, keepdims=True)
        dbet_ref[...] += jnp.sum(dx, axis=0, keepdims=True)
        du = _norm_rows_bwd(dx * g_ref[...], xh, rstd)
        du_ref[...] = du.astype(du_ref.dtype)
        acc = ALPHA * du
        for f0 in range(0, D_FF, fc):
            ch = slice(f0, f0 + fc)
            dh = _dot(du, wd_ref[ch, :], NT)
            a, b = a_ref[:, ch].astype(F32), b_ref[:, ch].astype(F32)
            sg = _sigmoid(a)
            da = dh * b * (sg * (1.0 + a * (1.0 - sg)))
            db = dh * (a * sg)
            acc = acc + _dot(da, wg_ref[ch, :], NN) + _dot(db, wu_ref[ch, :], NN)
            da_ref[:, ch] = da.astype(da_ref.dtype)
            db_ref[:, ch] = db.astype(db_ref.dtype)
        dx1_ref[...] = acc

    row = lambda w: pl.BlockSpec((tm, w), lambda i: (i, 0))
    vec = pl.BlockSpec((1, D_MODEL), lambda i: (0, 0))
    v = jax.ShapeDtypeStruct((1, D_MODEL), F32)
    return pl.pallas_call(
        body,
        out_shape=[jax.ShapeDtypeStruct((s, D_MODEL), F32), jax.ShapeDtypeStruct((s, D_MODEL), MXU_DTYPE),
                   jax.ShapeDtypeStruct((s, D_FF), MXU_DTYPE), jax.ShapeDtypeStruct((s, D_FF), MXU_DTYPE), v, v],
        grid=(s // tm,),
        in_specs=[row(D_MODEL), row(D_MODEL), row(D_FF), row(D_FF), _resident(wg_t), _resident(wu_t),
                  _resident(wd), _resident(gam)],
        out_specs=[row(D_MODEL), row(D_MODEL), row(D_FF), row(D_FF), vec, vec],
        compiler_params=_params(1), name="ffn_bwd",
    )(dx2, u2, fa, fb, wg_t, wu_t, wd, gam)


def _loss_head(y, target):
    s = y.shape[0]
    tr = _pick(s, (512, 256, 128))

    def body(y_ref, t_ref, dy_ref, l_ref):
        @pl.when(pl.program_id(0) == 0)
        def _():
            l_ref[...] = jnp.zeros_like(l_ref)

        e = y_ref[...] - t_ref[...]
        dy_ref[...] = e * (1.0 / D_MODEL)
        part = jnp.sum(jnp.sum(e * e, axis=1, keepdims=True), axis=0, keepdims=True)
        l_ref[...] += part * (0.5 / D_MODEL)

    spec = pl.BlockSpec((tr, D_MODEL), lambda i: (i, 0))
    dy, part = pl.pallas_call(
        body, out_shape=[jax.ShapeDtypeStruct((s, D_MODEL), F32), jax.ShapeDtypeStruct((8, BLK), F32)],
        grid=(s // tr,), in_specs=[spec, spec], out_specs=[spec, pl.BlockSpec((8, BLK), lambda i: (0, 0))],
        compiler_params=_params(1), name="loss_head",
    )(y, target)
    return dy, part[0, 0]


def _adamw(w, g, m, v, name):
    rows, cols = w.shape
    budget = 1 << 20
    cands = [t for t in range(8, rows + 1, 8) if rows % t == 0 and t * cols * 4 <= budget]
    tr = max(cands) if cands else rows

    def body(w_ref, g_ref, m_ref, v_ref, d_ref, nm_ref, nv_ref):
        gv = g_ref[...]
        mn = ADAM_B1 * m_ref[...] + (1.0 - ADAM_B1) * gv
        vn = ADAM_B2 * v_ref[...] + (1.0 - ADAM_B2) * (gv * gv)
        m_hat = mn / (1.0 - ADAM_B1 ** ADAM_STEP)
        v_hat = vn / (1.0 - ADAM_B2 ** ADAM_STEP)
        d_ref[...] = -ADAM_LR * (m_hat / (jnp.sqrt(v_hat) + ADAM_EPS) + ADAM_WD * w_ref[...])
        nm_ref[...] = mn
        nv_ref[...] = vn

    spec = pl.BlockSpec((tr, cols), lambda i: (i, 0))
    shape = jax.ShapeDtypeStruct((rows, cols), F32)
    return pl.pallas_call(
        body, out_shape=[shape, shape, shape], grid=(rows // tr,), in_specs=[spec] * 4, out_specs=[spec] * 3,
        compiler_params=_params(1), name=name,
    )(w, g, m, v)


MESH_ID = pl.DeviceIdType.MESH
ANY = pl.BlockSpec(memory_space=pl.ANY)


def _place():
    x, y, c = lax.axis_index("x"), lax.axis_index("y"), lax.axis_index("c")
    other_chips = [(1 - x, y), (x, 1 - y), (1 - x, 1 - y)]
    return x, y, c, other_chips


def _chip_no(chip):
    return 2 * chip[0] + chip[1]


def _gather_exchange(f_ref, g_ref, send_sems, recv_sems, hr):
    x, y, c, chips = _place()
    sibling = (x, y, 1 - c)

    def piece(chip, half):
        return g_ref.at[_chip_no(chip), pl.ds(half * hr, hr), :]

    def copy(k, src, dst, to):
        return pltpu.make_async_remote_copy(src_ref=src, dst_ref=dst, send_sem=send_sems.at[k],
                                            recv_sem=recv_sems.at[k], device_id=to, device_id_type=MESH_ID)

    own = copy(6, f_ref, g_ref.at[_chip_no((x, y))], sibling)
    own.start()
    my_half = f_ref.at[pl.ds(c * hr, hr), :]
    first = [copy(k, my_half, piece((x, y), c), (*chip, c)) for k, chip in enumerate(chips)]
    for cp in first:
        cp.start()
    passed = [copy(3 + k, piece(chip, c), piece(chip, c), sibling) for k, chip in enumerate(chips)]
    for k, chip in enumerate(chips):
        copy(k, my_half, piece(chip, c), (*chip, c)).wait_recv()
        passed[k].start()
    for k, chip in enumerate(chips):
        copy(3 + k, my_half, piece(chip, 1 - c), sibling).wait_recv()
    for cp in first + passed:
        cp.wait_send()
    own.wait()


GATHER_SEMS = (pltpu.SemaphoreType.DMA((7,)), pltpu.SemaphoreType.DMA((7,)))


HBM = pltpu.MemorySpace.HBM
ONE_SEM_PAIR = (pltpu.SemaphoreType.DMA, pltpu.SemaphoreType.DMA)


def _sibling_of_me():
    x, y, c, _ = _place()
    return [(x, y, 1 - c)]


def _same_core_of_other_chips():
    x, y, c, chips = _place()
    return [(*chip, c) for chip in chips]


def _on_sequencer(name, collective_id, sems, peers, exchange):
    @pl.kernel(mesh=plsc.ScalarSubcoreMesh(axis_name="sequencer", num_cores=1), name=name, scratch_types=sems,
               compiler_params=pltpu.CompilerParams(collective_id=collective_id))
    def launch(*sem_refs):
        barrier = pltpu.get_barrier_semaphore()
        devices = peers()
        for peer in devices:
            pl.semaphore_signal(barrier, inc=1, device_id=peer, device_id_type=MESH_ID)
        pl.semaphore_wait(barrier, len(devices))
        exchange(*sem_refs)

    launch()


def _all_gather_rows(flat, behind=None):
    r, cols = flat.shape
    out = jax.ShapeDtypeStruct((N_CHIPS, r, cols), flat.dtype)
    if behind is not None:
        f_ref, g_ref = jax.new_ref(flat, memory_space=HBM), jax.empty_ref(out, memory_space=HBM)
        _on_sequencer(f"all_gather_weights_behind_{behind[1]}", behind[0], GATHER_SEMS,
                      lambda: _sibling_of_me() + _same_core_of_other_chips(),
                      lambda s, r_: _gather_exchange(f_ref, g_ref, s, r_, r // 2))
        return g_ref[...]

    def body(f_ref, g_ref, send_sems, recv_sems):
        _gather_exchange(f_ref, g_ref, send_sems, recv_sems, r // 2)

    return pl.pallas_call(body, out_shape=out, in_specs=[ANY], out_specs=ANY, scratch_shapes=list(GATHER_SEMS),
                          name="all_gather_weights")(flat)


def _swap_halves(gall, behind=None):
    _, r, cols = gall.shape
    hr = r // 2
    out = jax.ShapeDtypeStruct((N_CHIPS, hr, cols), gall.dtype)

    def exchange(g_ref, a_ref, send_sem, recv_sem):
        x, y, c, _ = _place()
        cp = pltpu.make_async_remote_copy(src_ref=g_ref.at[:, pl.ds((1 - c) * hr, hr), :], dst_ref=a_ref,
                                          send_sem=send_sem, recv_sem=recv_sem, device_id=(x, y, 1 - c),
                                          device_id_type=MESH_ID)
        cp.start()
        cp.wait()

    if behind is not None:
        g_ref, a_ref = jax.new_ref(gall, memory_space=HBM), jax.empty_ref(out, memory_space=HBM)
        _on_sequencer(f"grad_swap_halves_behind_{behind[1]}", behind[0], ONE_SEM_PAIR, _sibling_of_me,
                      lambda s, r_: exchange(g_ref, a_ref, s, r_))
        return a_ref[...]

    def body(g_ref, a_ref, send_sem, recv_sem):
        exchange(g_ref, a_ref, send_sem, recv_sem)

    return pl.pallas_call(body, out_shape=out, in_specs=[ANY], out_specs=ANY, scratch_shapes=list(ONE_SEM_PAIR),
                          name="grad_swap_halves")(gall)


def _scatter_to_chips(p, behind=None):
    _, hr, cols = p.shape
    out = jax.ShapeDtypeStruct((3, hr, cols), p.dtype)
    sems = (pltpu.SemaphoreType.DMA((3,)), pltpu.SemaphoreType.DMA((3,)))

    def exchange(p_ref, b_ref, send_sems, recv_sems):
        x, y, c, chips = _place()
        cps = [pltpu.make_async_remote_copy(src_ref=p_ref.at[_chip_no(chip)], dst_ref=b_ref.at[k],
                                            send_sem=send_sems.at[k], recv_sem=recv_sems.at[k],
                                            device_id=(*chip, c), device_id_type=MESH_ID)
               for k, chip in enumerate(chips)]
        for cp in cps:
            cp.start()
        for cp in cps:
            cp.wait()

    if behind is not None:
        p_ref, b_ref = jax.new_ref(p, memory_space=HBM), jax.empty_ref(out, memory_space=HBM)
        _on_sequencer(f"grad_scatter_chips_behind_{behind[1]}", behind[0], sems, _same_core_of_other_chips,
                      lambda s, r_: exchange(p_ref, b_ref, s, r_))
        return b_ref[...]

    def body(p_ref, b_ref, send_sems, recv_sems):
        exchange(p_ref, b_ref, send_sems, recv_sems)

    return pl.pallas_call(body, out_shape=out, in_specs=[ANY], out_specs=ANY, scratch_shapes=list(sems),
                          name="grad_scatter_chips")(p)


def _share_with_sibling(full, behind=None):
    r, cols = full.shape
    hr = r // 2

    def exchange(in_ref, out_ref, send_sem, recv_sem):
        x, y, c, _ = _place()
        cp = pltpu.make_async_remote_copy(src_ref=in_ref.at[pl.ds(c * hr, hr), :],
                                          dst_ref=out_ref.at[pl.ds(c * hr, hr), :],
                                          send_sem=send_sem, recv_sem=recv_sem, device_id=(x, y, 1 - c),
                                          device_id_type=MESH_ID)
        cp.start()
        cp.wait()

    if behind is not None:
        full_ref = jax.new_ref(full, memory_space=HBM)
        _on_sequencer(f"grad_share_sibling_behind_{behind[1]}", behind[0], ONE_SEM_PAIR, _sibling_of_me,
                      lambda s, r_: exchange(full_ref, full_ref, s, r_))
        return full_ref[...]

    def body(in_ref, out_ref, send_sem, recv_sem):
        exchange(in_ref, out_ref, send_sem, recv_sem)

    return pl.pallas_call(
        body, out_shape=jax.ShapeDtypeStruct((r, cols), full.dtype), in_specs=[ANY], out_specs=ANY,
        input_output_aliases={0: 0}, scratch_shapes=list(ONE_SEM_PAIR), name="grad_share_sibling",
    )(full)


def _all_reduce_small(v):
    r, cols = v.shape
    n_dev = 8

    def body(x_ref, out_ref, gat_ref, send_sems, recv_sems, local_sem):
        x, y, c, chips = _place()
        me, sibling = (x, y, c), (x, y, 1 - c)

        def slot(px, py, pc):
            return gat_ref.at[4 * px + 2 * py + pc]

        def copy(k, block, to, src=None):
            return pltpu.make_async_remote_copy(src_ref=slot(*block) if src is None else src, dst_ref=slot(*block),
                                                send_sem=send_sems.at[k], recv_sem=recv_sems.at[k], device_id=to,
                                                device_id_type=MESH_ID)

        mine = pltpu.make_async_copy(x_ref, slot(*me), local_sem)
        mine.start()
        first = [copy(0, me, sibling, src=x_ref)]
        first += [copy(1 + k, me, (*chip, c), src=x_ref) for k, chip in enumerate(chips)]
        for cp in first:
            cp.start()
        passed = [copy(4 + k, (*chip, c), sibling) for k, chip in enumerate(chips)]
        for k, chip in enumerate(chips):
            copy(1 + k, (*chip, c), me).wait_recv()
            passed[k].start()
        copy(0, sibling, me).wait_recv()
        for k, chip in enumerate(chips):
            copy(4 + k, (*chip, 1 - c), me).wait_recv()
        for cp in first + passed:
            cp.wait_send()
        mine.wait()
        acc = gat_ref[0]
        for d in range(1, n_dev):
            acc = acc + gat_ref[d]
        out_ref[...] = acc

    vmem = pl.BlockSpec(memory_space=pltpu.VMEM)
    return pl.pallas_call(
        body, out_shape=jax.ShapeDtypeStruct((r, cols), v.dtype), in_specs=[vmem], out_specs=vmem,
        scratch_shapes=[pltpu.VMEM((n_dev, r, cols), v.dtype), pltpu.SemaphoreType.DMA((7,)),
                        pltpu.SemaphoreType.DMA((7,)), pltpu.SemaphoreType.DMA],
        name="all_reduce_small",
    )(v)


WIRE_DTYPE = jnp.bfloat16


def _add_own_half(gall, got, place):
    _, r, cols = gall.shape
    hr = r // 2
    tr = _row_tile(hr)
    g4 = gall.reshape(N_CHIPS, 2, hr, cols)

    def body(s_ref, g_ref, a_ref, o_ref):
        o_ref[...] = (g_ref[0] + a_ref[...]).astype(o_ref.dtype)

    return pl.pallas_call(
        body, out_shape=jax.ShapeDtypeStruct((N_CHIPS, hr, cols), WIRE_DTYPE),
        grid_spec=pltpu.PrefetchScalarGridSpec(
            num_scalar_prefetch=1, grid=(N_CHIPS, hr // tr),
            in_specs=[pl.BlockSpec((1, 1, tr, cols), lambda j, i, s: (j, s[1], i, 0)),
                      pl.BlockSpec((1, tr, cols), lambda j, i, s: (j, i, 0))],
            out_specs=pl.BlockSpec((1, tr, cols), lambda j, i, s: (j, i, 0))),
        compiler_params=_params(2), name="grad_add_halves",
    )(place, g4, got)


def _add_chip_parts(gall, got, parts, place):
    _, r, cols = gall.shape
    hr = r // 2
    tr = _row_tile(hr)
    g4 = gall.reshape(N_CHIPS, 2, hr, cols)
    nt = hr // tr

    def body(s_ref, g_ref, a_ref, b0_ref, b1_ref, b2_ref, o_ref):
        own = g_ref[0, 0] + a_ref[0]
        o_ref[...] = ((own + b0_ref[0].astype(F32)) + b1_ref[0].astype(F32)) + b2_ref[0].astype(F32)

    part = lambda k: pl.BlockSpec((1, tr, cols), lambda i, s: (k, i, 0))
    return pl.pallas_call(
        body, out_shape=jax.ShapeDtypeStruct((r, cols), F32),
        grid_spec=pltpu.PrefetchScalarGridSpec(
            num_scalar_prefetch=1, grid=(nt,),
            in_specs=[pl.BlockSpec((1, 1, tr, cols), lambda i, s: (s[0], s[1], i, 0)),
                      pl.BlockSpec((1, tr, cols), lambda i, s: (s[0], i, 0)), part(0), part(1), part(2)],
            out_specs=pl.BlockSpec((tr, cols), lambda i, s: (s[1] * nt + i, 0))),
        compiler_params=_params(1), name="grad_add_chips",
    )(place, g4, got, parts, parts, parts)


def _pack_shards(weights, l, group, dtype):
    parts = []
    for name, _ in group:
        w = weights[name][l]
        parts.append((w.T if name in COLUMN_SHARDED else w).reshape(-1, D_MODEL).astype(dtype))
    return parts[0] if len(parts) == 1 else jnp.concatenate(parts, axis=0)


def _unpack_gathered(gathered, group):
    w, off = {}, 0
    for name, rows in group:
        blk = gathered[:, off:off + rows]
        if name == "w_attn_proj":
            w[name] = blk.reshape(N_CHIPS * 256, ATTN_OUT)
        else:
            w[name] = blk.reshape(N_CHIPS * rows, D_MODEL)
        off += rows
    return w


def _unpack_full_grads(packed, gi):
    return _unpack_gathered(packed, PACK_GROUPS[gi])


def _unpack_shard_grads(fulls, gi):
    out, off = {}, 0
    for name, rows in PACK_GROUPS[gi]:
        blks = [full[off:off + rows] for full in fulls]
        out[name] = jnp.stack([b.reshape(256, ATTN_OUT) if name == "w_attn_proj" else b for b in blks])
        off += rows
    return out


class _StaticWeights:
    def __init__(self, layers):
        self.layers = layers

    def group(self, l, gi):
        return self.layers[l]

    def layer(self, l):
        return self.layers[l]


class _WeightGathers:
    def __init__(self, weights):
        self.flats = {(l, gi): _pack_shards(weights, l, g, MXU_DTYPE)
                      for l in range(DEPTH) for gi, g in enumerate(PACK_GROUPS)}
        self.order = sorted(self.flats)
        self.landed, self.unpacked, self.issued = {}, {}, 0
        self._issue()

    def _issue(self, after=None):
        key = self.order[self.issued]
        flat = self.flats[key]
        if self.issued == 0:
            self.landed[key] = _all_gather_rows(flat, behind=(1, f"l{key[0]}g{key[1]}"))
        else:
            prev = self.order[self.issued - 1]
            ties = (self.landed[prev], flat) if after is None else (self.landed[prev], flat, after)
            tied = lax.optimization_barrier(ties)
            self.landed[prev], flat = tied[0], tied[1]
            after = tied[2] if after is not None else None
            self.landed[key] = _all_gather_rows(flat, behind=(1, f"l{key[0]}g{key[1]}"))
        self.issued += 1
        return after

    def prefetch(self, after):
        return self._issue(after) if self.issued < len(self.order) else after

    def group(self, l, gi):
        key = (l, gi)
        if key not in self.unpacked:
            self.unpacked[key] = _unpack_gathered(self.landed[key], PACK_GROUPS[gi])
        return self.unpacked[key]

    def layer(self, l):
        w = {}
        for gi in range(len(PACK_GROUPS)):
            w.update(self.group(l, gi))
        return w


class _GradReduction:
    LAST = 3

    def __init__(self, packed, place, behind):
        self.gall, self.place, self.behind, self.stage = packed, place, behind, 0
        self.pending = None

    def _how(self, k):
        return None if self.behind is None else (self.behind[0][k], self.behind[1])

    def advance(self, value=None):
        if self.stage > self.LAST:
            return value
        if self.stage > 0 and value is not None:
            value, self.pending = lax.optimization_barrier((value, self.pending))
        if self.stage == 0:
            self.pending = _swap_halves(self.gall, self._how(0))
        elif self.stage == 1:
            self.got = self.pending
            self.pending = _scatter_to_chips(_add_own_half(self.gall, self.got, self.place), self._how(1))
        elif self.stage == 2:
            self.pending = _share_with_sibling(
                _add_chip_parts(self.gall, self.got, self.pending, self.place), self._how(2))
        self.stage += 1
        return value

    def result(self):
        while self.stage < self.LAST:
            self.advance()
        return self.pending


class _GradReductions:
    def __init__(self, place):
        self.place, self.all = place, {}

    def start(self, l, gi, packed):
        slot = len(self.all) % 2
        how = ((2 + 3 * slot, 3 + 3 * slot, 4 + 3 * slot), f"l{l}g{gi}")
        red = _GradReduction(packed, self.place, how)
        red.advance()
        self.all[(l, gi)] = red

    def tick(self, value):
        for red in self.all.values():
            if red.behind is not None:
                value = red.advance(value)
        return value

    def results(self):
        return {key: red.result() for key, red in self.all.items()}


def _layer_fwd(x, x_b, weights, l, lnp, bias, rc, prefetch):
    tag = f"l{l}"
    w_in = weights.group(l, IN_GROUP)["w_in"]
    z = {}
    for part, secs in enumerate(DX_PARTS):
        runs = tuple(run for _, sec_runs, _, _ in secs for run in sec_runs)
        outs = _in_proj(x_b, _section_rows(w_in, runs), _section_rows(lnp["b_in"], runs), secs,
                        name=f"in_proj_{part}_{tag}")
        z.update({name: o for (name, _, _, _), o in zip(secs, outs)})
        if part == 0:
            z["a0"] = prefetch(z["a0"])
    w = weights.group(l, REST_GROUP)
    o_list, l_list = [], []
    for g, (_, dil) in enumerate(ATTN_GROUPS):
        o, lse = _attn_fwd(z[f"a{g}"], bias, g, dil)
        o_list.append(o)
        l_list.append(lse)
    ya_b, ya_views, lt_views = _attn_combine(o_list, l_list)
    yb_b, o_ret, states = _ret_fwd(z["bq"], z["bk"], z["c"], z["d"], rc)
    x1, u1, a1, a2, mg_b, x1_b = _mix_fwd(x, ya_b, yb_b, z["e"], w["w_attn_proj"], w["w_ret_proj"], w["w_out"],
                                          lnp["ln1_g"], lnp["ln1_b"])
    x1 = prefetch(x1)
    x2, u2, fa, fb, h_b, x2_b = _ffn_fwd(x1, w["w_ffn_gate"], w["w_ffn_up"], w["w_ffn_down"], lnp["ln2_g"],
                                         lnp["ln2_b"])
    saved = dict(x_b=x_b, z=z, ya_views=ya_views, ya_b=ya_b, lt_views=lt_views, yb_b=yb_b, o_ret=o_ret, states=states,
                 x1_b=x1_b, u1=u1, a1=a1, a2=a2, mg_b=mg_b, u2=u2, fa=fa, fb=fb, h_b=h_b)
    return x2, x2_b, saved


def _layer_bwd(dx2, w, l, lnp, sv, bias, rc, reductions=None):
    s = dx2.shape[0]
    tag = f"l{l}"
    z = sv["z"]
    step = reductions.tick if reductions is not None else (lambda v: v)
    dx1, du2_b, da_b, db_b, dg2, dbt2 = _ffn_bwd(dx2, sv["u2"], sv["fa"], sv["fb"], w["w_ffn_gate"], w["w_ffn_up"],
                                                  w["w_ffn_down"], lnp["ln2_g"])
    dx1 = step(dx1)
    dres, du1_b, da1_b, da2_b, dze, dyb, dg1, dbt1, dya_views = _mix_bwd(
        dx1, sv["u1"], sv["a1"], sv["a2"], z["e"], w["w_attn_proj"], w["w_ret_proj"], w["w_out"], lnp["ln1_g"])
    rest_rows = dict(PACK_GROUPS[REST_GROUP])
    first_row = {n: sum(r for _, r in PACK_GROUPS[REST_GROUP][:i]) for i, (n, _) in enumerate(PACK_GROUPS[REST_GROUP])}
    dw_ap = _dw_matmul(da1_b, sv["ya_b"], name=f"dw_attn_proj_{tag}")
    packed_rest = lax.dynamic_update_slice(
        lax.empty((N_CHIPS, sum(rest_rows.values()), D_MODEL), F32),
        dw_ap.reshape(N_CHIPS, rest_rows["w_attn_proj"], D_MODEL), (0, first_row["w_attn_proj"], 0))
    for n, lhs, rhs in (("w_ffn_down", sv["h_b"], du2_b), ("w_ffn_gate", da_b, sv["x1_b"]),
                        ("w_ffn_up", db_b, sv["x1_b"]), ("w_out", sv["mg_b"], du1_b),
                        ("w_ret_proj", sv["yb_b"], da2_b)):
        packed_rest = _dw_matmul(lhs, rhs, name=f"dw_{n[2:]}_{tag}",
                                 into=(packed_rest, ((0, N_CHIPS * rest_rows[n]),), rest_rows[n], first_row[n]))
    dyb = step(dyb)
    if reductions is not None:
        reductions.start(l, REST_GROUP, packed_rest)
    dzq, dzk, dzv, dzg = _ret_bwd(z["bq"], z["bk"], z["c"], z["d"], sv["o_ret"], sv["states"], dyb, rc)
    dzq = step(dzq)
    dz = {"bq": dzq, "bk": dzk, "c": dzv, "d": dzg, "e": dze}
    dbias_l = []
    for g, (_, dil) in enumerate(ATTN_GROUPS):
        dz[f"a{g}"], dbg = _attn_bwd(z[f"a{g}"], bias, dya_views[g], sv["ya_views"][g], sv["lt_views"][g], g, dil)
        dbias_l.append(dbg)
    last = f"a{N_GROUPS - 1}"
    dz[last] = step(dz[last])
    dx = dres
    for part, secs in enumerate(DX_PARTS):
        runs = tuple(run for _, sec_runs, _, _ in secs for run in sec_runs)
        dx = _dx_in_proj(dx, [dz[name] for name, _, _, _ in secs], _section_rows(w["w_in"], runs), secs,
                         name=f"dx_in_proj_{part}_{tag}")
    dx = step(dx)
    in_rows = PACK_GROUPS[IN_GROUP][0][1]
    packed_in, db = lax.empty((N_CHIPS, in_rows, D_MODEL), F32), {}
    for name, runs, _, dil in SECTIONS:
        packed_in, db[name] = _dw_matmul(dz[name], sv["x_b"], colsum=True, a_phase=dil,
                                         into=(packed_in, runs, in_rows, 0), name=f"dw_in_proj_{name}_{tag}")
    if reductions is not None:
        reductions.start(l, IN_GROUP, packed_in)
    small = dict(b_in=_unsection(db, axis=1), ln1_g=dg1, ln1_b=dbt1, ln2_g=dg2, ln2_b=dbt2)
    return dx, {IN_GROUP: packed_in, REST_GROUP: packed_rest}, small, jnp.concatenate(dbias_l, axis=0)


def _forward_backward(x, target, rel_bias, weights, lnps, reductions=None):
    s = x.shape[0]
    bmaps = _bucket_maps()
    bias = _bias_tiles(rel_bias, bmaps)
    rc = _ret_consts(s)
    prefetch = getattr(weights, "prefetch", lambda v: v)
    saved = []
    h, h_b = x, x.astype(MXU_DTYPE)
    for l in range(DEPTH):
        h, h_b, sv = _layer_fwd(h, h_b, weights, l, lnps[l], bias, rc, prefetch)
        saved.append(sv)
    dh, loss_part = _loss_head(h, target)
    gws, smalls, dbiases = [None] * DEPTH, [None] * DEPTH, [None] * DEPTH
    for l in reversed(range(DEPTH)):
        dh, gws[l], smalls[l], dbiases[l] = _layer_bwd(dh, weights.layer(l), l, lnps[l], saved[l], bias, rc,
                                                       reductions)
    d_rel_bias = _bias_tiles_bwd(dbiases, bmaps)
    return loss_part, dh, gws, smalls, d_rel_bias


SMALL_NAMES = ("rel_bias", "b_in", "ln1_g", "ln1_b", "ln2_g", "ln2_b")
SMALL_ROWS = 32


def _pack_small(vals):
    flat = jnp.concatenate([vals[n].reshape(-1) for n in SMALL_NAMES])
    return jnp.pad(flat, (0, SMALL_ROWS * D_MODEL - flat.shape[0])).reshape(SMALL_ROWS, D_MODEL)


def _unpack_small(packed, like):
    flat = packed.reshape(-1)
    out, off = {}, 0
    for n in SMALL_NAMES:
        size = like[n].size
        out[n] = flat[off:off + size].reshape(like[n].shape)
        off += size
    return out


def kernel(x, rel_bias, w_in, b_in, w_attn_proj, w_ret_proj, w_out, ln1_g, ln1_b, w_ffn_gate, w_ffn_up, w_ffn_down, ln2_g, ln2_b, loss_target, m_rel_bias, m_w_in, m_b_in, m_w_attn_proj, m_w_ret_proj, m_w_out, m_ln1_g, m_ln1_b, m_w_ffn_gate, m_w_ffn_up, m_w_ffn_down, m_ln2_g, m_ln2_b, v_rel_bias, v_w_in, v_b_in, v_w_attn_proj, v_w_ret_proj, v_w_out, v_ln1_g, v_ln1_b, v_w_ffn_gate, v_w_ffn_up, v_w_ffn_down, v_ln2_g, v_ln2_b):
    big = dict(w_in=(w_in, m_w_in, v_w_in), w_attn_proj=(w_attn_proj, m_w_attn_proj, v_w_attn_proj),
               w_ret_proj=(w_ret_proj, m_w_ret_proj, v_w_ret_proj), w_out=(w_out, m_w_out, v_w_out),
               w_ffn_gate=(w_ffn_gate, m_w_ffn_gate, v_w_ffn_gate), w_ffn_up=(w_ffn_up, m_w_ffn_up, v_w_ffn_up),
               w_ffn_down=(w_ffn_down, m_w_ffn_down, v_w_ffn_down))
    small_w = dict(rel_bias=rel_bias, b_in=b_in, ln1_g=ln1_g, ln1_b=ln1_b, ln2_g=ln2_g, ln2_b=ln2_b)
    small_m = dict(rel_bias=m_rel_bias, b_in=m_b_in, ln1_g=m_ln1_g, ln1_b=m_ln1_b, ln2_g=m_ln2_g, ln2_b=m_ln2_b)
    small_v = dict(rel_bias=v_rel_bias, b_in=v_b_in, ln1_g=v_ln1_g, ln1_b=v_ln1_b, ln2_g=v_ln2_g, ln2_b=v_ln2_b)

    place = jnp.stack([2 * lax.axis_index("x") + lax.axis_index("y"), lax.axis_index("c")]).astype(jnp.int32)

    weights = _WeightGathers({n: w for n, (w, _, _) in big.items()})
    reductions = _GradReductions(place)
    lnps = [dict(b_in=b_in[l][None], ln1_g=ln1_g[l][None], ln1_b=ln1_b[l][None], ln2_g=ln2_g[l][None],
                 ln2_b=ln2_b[l][None]) for l in range(DEPTH)]
    loss_part, dx, _, smalls, d_rel_bias = _forward_backward(x[0], loss_target[0], rel_bias, weights, lnps,
                                                             reductions)
    loss = lax.psum(loss_part, ("x", "y", "c"))

    small_g = dict(rel_bias=d_rel_bias)
    for n in SMALL_NAMES[1:]:
        small_g[n] = jnp.concatenate([smalls[l][n] for l in range(DEPTH)], axis=0)
    small_g = _unpack_small(_all_reduce_small(_pack_small(small_g)), small_w)
    grads, delta, new_m, new_v = dict(small_g), {}, {}, {}
    d_, m_, v_ = _adamw(_pack_small(small_w), _pack_small(small_g), _pack_small(small_m), _pack_small(small_v),
                        name="adamw_small")
    d_ = reductions.tick(d_)
    delta.update(_unpack_small(d_, small_w))
    new_m.update(_unpack_small(m_, small_w))
    new_v.update(_unpack_small(v_, small_w))

    def update(n, g):
        w, m, v = big[n]
        turn = (lambda t: jnp.swapaxes(t, 1, 2)) if n in COLUMN_SHARDED else (lambda t: t)
        two_d = lambda t: t.reshape(-1, t.shape[-1])
        d_, m_, v_ = _adamw(two_d(turn(w)), two_d(g), two_d(turn(m)), two_d(turn(v)), name=f"adamw_{n}")
        grads[n], delta[n], new_m[n], new_v[n] = (turn(t.reshape(g.shape)) for t in (g, d_, m_, v_))

    rest = _unpack_shard_grads([reductions.all[(l, REST_GROUP)].result() for l in range(DEPTH)], REST_GROUP)
    for i, (n, g) in enumerate(rest.items()):
        update(n, reductions.tick(g) if i == len(rest) // 2 else g)
    first = _unpack_shard_grads([reductions.all[(l, IN_GROUP)].result() for l in range(DEPTH)], IN_GROUP)
    for n, g in first.items():
        update(n, g)

    order = ("rel_bias", "w_in", "b_in", "w_attn_proj", "w_ret_proj", "w_out", "ln1_g", "ln1_b", "w_ffn_gate",
             "w_ffn_up", "w_ffn_down", "ln2_g", "ln2_b")
    return (loss, dx[None], *[grads[n] for n in order], *[delta[n] for n in order], *[new_m[n] for n in order],
            *[new_v[n] for n in order])
```
